```python
import math
import jax
import jax.numpy as jnp
from jax import lax
import numpy as np

D_MODEL = 1024
BATCH = 8
SEQ = 8192
DEPTH = 2

GRID_W = 64
CTX_LEN = 256
Q_BLOCK = 128
NORM_EPS = 1e-6

ATTN_HEADS = 8
ATTN_KV_HEADS = 2
ATTN_HEAD_DIM = 64
ATTN_AXIS_FREQS = ATTN_HEAD_DIM // 4
ROPE_THETA = 10000.0

SSD_HEADS = 8
SSD_HEAD_DIM = 64
SSD_D_INNER = SSD_HEADS * SSD_HEAD_DIM
SSD_GROUPS = 2
SSD_STATE = 128
SSD_CONV_K = 3
SSD_CONV_DIM = SSD_D_INNER + 2 * SSD_GROUPS * SSD_STATE
SSD_CHUNK = 128

RET_HEADS = 4
RET_DK = 128
RET_DV = 128
RET_CHUNK = 128

N_BRANCH = 3
BRANCH_W = 512
MLP_HIDDEN = 4 * D_MODEL

IN_SPLITS = (ATTN_HEADS * ATTN_HEAD_DIM, ATTN_KV_HEADS * ATTN_HEAD_DIM, ATTN_KV_HEADS * ATTN_HEAD_DIM,
             SSD_D_INNER, SSD_CONV_DIM, 2 * SSD_HEADS,
             RET_HEADS * RET_DK, RET_HEADS * RET_DK, RET_HEADS * RET_DV, RET_HEADS * RET_DV,
             N_BRANCH * D_MODEL)
IN_DIM = sum(IN_SPLITS)

kernel_name = 'hybrid_attn_ssd_retention_prefix_dit'


def rms_norm(x, w):
    xf = x.astype(jnp.float32)
    y = xf * lax.rsqrt(jnp.mean(xf * xf, axis=-1, keepdims=True) + NORM_EPS)
    return (y * w.astype(jnp.float32)).astype(x.dtype)


def modulate(x, shift, scale):
    return x * (1 + scale[:, None, :]) + shift[:, None, :]


def split_cols(p):
    out = []
    off = 0
    for size in IN_SPLITS:
        out.append(p[..., off:off + size])
        off += size
    return out


def flip(t):
    return jnp.flip(t, axis=1)


def rope_apply(x, cos, sin):
    half = x.shape[-1] // 2
    x1 = x[..., :half]
    x2 = x[..., half:]
    cs = cos[:, None, :].astype(x.dtype)
    sn = sin[:, None, :].astype(x.dtype)
    return jnp.concatenate([x1 * cs - x2 * sn, x1 * sn + x2 * cs], axis=-1)


def axial_angles(rows):
    row = jnp.repeat(jnp.arange(rows, dtype=jnp.float32), GRID_W)
    col = jnp.tile(jnp.arange(GRID_W, dtype=jnp.float32), rows)
    inv = ROPE_THETA ** (-jnp.arange(ATTN_AXIS_FREQS, dtype=jnp.float32) / ATTN_AXIS_FREQS)
    ang = jnp.concatenate([row[:, None] * inv, col[:, None] * inv], axis=-1)
    return jnp.cos(ang), jnp.sin(ang)


def seq_angles(start, n):
    pos = jnp.arange(n, dtype=jnp.float32) + start
    inv = ROPE_THETA ** (-jnp.linspace(0.0, 1.0, RET_DK // 2, dtype=jnp.float32))
    ang = pos[:, None] * inv
    return jnp.cos(ang), jnp.sin(ang)


def dwconv_centred(x, w, bias):
    y = lax.conv_general_dilated(
        x, w[:, None, :].astype(x.dtype), window_strides=(1,),
        padding=[(SSD_CONV_K // 2, SSD_CONV_K // 2)],
        dimension_numbers=('NWC', 'WIO', 'NWC'), feature_group_count=x.shape[-1])
    return y + bias.astype(x.dtype)


def gqa_attend(q, k, v):
    b, lq, h, hd = q.shape
    kvh = k.shape[2]
    grp = h // kvh
    qb = q.reshape(b, lq // Q_BLOCK, Q_BLOCK, kvh, grp, hd).transpose(1, 0, 2, 3, 4, 5)
    scale = hd ** -0.5

    def block(qblk):
        s = jnp.einsum('bqkgd,bskd->bkgqs', qblk, k, preferred_element_type=jnp.float32) * scale
        p = jax.nn.softmax(s, axis=-1).astype(v.dtype)
        return jnp.einsum('bkgqs,bskd->bqkgd', p, v)

    o = lax.map(block, qb)
    return o.transpose(1, 0, 2, 3, 4, 5).reshape(b, lq, h * hd)


def chunk_scan(init, states, decay):
    def step(s, inp):
        st, dc = inp
        return s * dc + st, s
    final, prev = lax.scan(step, init, (states, decay))
    return final, prev


def ssd_scan(xh, dt, a_neg, bm, cm, init, return_y):
    b, l, nh, hp = xh.shape
    ng, ns = bm.shape[2], bm.shape[3]
    r = nh // ng
    L = SSD_CHUNK
    nc = l // L
    dtf = dt.astype(jnp.float32)
    xd = (xh.astype(jnp.float32) * dtf[..., None]).reshape(b, nc, L, ng, r, hp)
    a = (dtf * a_neg).reshape(b, nc, L, ng, r).transpose(0, 3, 4, 1, 2)
    bc = bm.astype(jnp.float32).reshape(b, nc, L, ng, ns)
    a_cum = jnp.cumsum(a, axis=-1)
    decay_states = jnp.exp(a_cum[..., -1:] - a_cum)
    states = jnp.einsum('bclgn,bgrcl,bclgrp->cbgrpn', bc, decay_states, xd)
    chunk_decay = jnp.exp(a_cum[..., -1]).transpose(3, 0, 1, 2)[..., None, None]
    final, prev = chunk_scan(init, states, chunk_decay)
    if not return_y:
        return None, final
    cc = cm.astype(jnp.float32).reshape(b, nc, L, ng, ns)
    seg = a_cum[..., :, None] - a_cum[..., None, :]
    causal = jnp.tril(jnp.ones((L, L), dtype=bool))
    lmat = jnp.where(causal, jnp.exp(jnp.where(causal, seg, 0.0)), 0.0)
    cb = jnp.einsum('bclgn,bcsgn->bcgls', cc, bc)
    y_diag = jnp.einsum('bcgls,bgrcls,bcsgrp->bclgrp', cb, lmat, xd)
    y_off = jnp.einsum('bclgn,cbgrpn,bgrcl->bclgrp', cc, prev, jnp.exp(a_cum))
    return (y_diag + y_off).reshape(b, l, nh, hp), final


def retention_scan(q, k, v, lg, init, return_y):
    b, l, nh, dk = k.shape
    dv = v.shape[-1]
    L = RET_CHUNK
    nc = l // L
    kc = k.astype(jnp.float32).reshape(b, nc, L, nh, dk)
    vc = v.astype(jnp.float32).reshape(b, nc, L, nh, dv)
    pos = jnp.arange(L, dtype=jnp.float32)
    k_decay = jnp.exp((L - 1 - pos)[:, None] * lg)
    states = jnp.einsum('bcshk,sh,bcshv->cbhkv', kc, k_decay, vc)
    chunk_decay = jnp.broadcast_to(jnp.exp(L * lg)[None, None, :, None, None], (nc, 1, nh, 1, 1))
    final, prev = chunk_scan(init, states, chunk_decay)
    if not return_y:
        return None, final
    qc = q.astype(jnp.float32).reshape(b, nc, L, nh, dk)
    diff = pos[:, None] - pos[None, :]
    dmat = jnp.where(diff[None] >= 0, jnp.exp(jnp.maximum(diff, 0.0)[None] * lg[:, None, None]), 0.0)
    s = jnp.einsum('bclhk,bcshk->bchls', qc, kc) * dmat
    y_in = jnp.einsum('bchls,bcshv->bclhv', s, vc)
    q_decay = jnp.exp((pos + 1)[:, None] * lg)
    y_x = jnp.einsum('bclhk,cbhkv,lh->bclhv', qc, prev, q_decay)
    return (y_in + y_x).reshape(b, l, nh, dv), final


def attn_q(aq, q_norm, rope):
    b, l, _ = aq.shape
    q = rms_norm(aq.reshape(b, l, ATTN_HEADS, ATTN_HEAD_DIM), q_norm)
    return q if rope is None else rope_apply(q, *rope)


def attn_kv(ak, av, k_norm, rope):
    b, l, _ = ak.shape
    k = rms_norm(ak.reshape(b, l, ATTN_KV_HEADS, ATTN_HEAD_DIM), k_norm)
    if rope is not None:
        k = rope_apply(k, *rope)
    return k, av.reshape(b, l, ATTN_KV_HEADS, ATTN_HEAD_DIM)


def ssd_inputs(xbc_raw, dt_raw, conv_w, conv_b, dt_bias):
    b, l, _ = xbc_raw.shape
    gn = SSD_GROUPS * SSD_STATE
    xbc = jax.nn.silu(dwconv_centred(xbc_raw, conv_w, conv_b))
    xs = xbc[..., :SSD_D_INNER].reshape(b, l, SSD_HEADS, SSD_HEAD_DIM)
    bm = xbc[..., SSD_D_INNER:SSD_D_INNER + gn].reshape(b, l, SSD_GROUPS, SSD_STATE)
    cm = xbc[..., SSD_D_INNER + gn:].reshape(b, l, SSD_GROUPS, SSD_STATE)
    dt = jax.nn.softplus(dt_raw.astype(jnp.float32).reshape(b, l, 2, SSD_HEADS) + dt_bias.astype(jnp.float32))
    return xs, bm, cm, dt[:, :, 0], dt[:, :, 1]


def ssd_finish(y, xh, z, d_skip, norm_w):
    b, l, nh, hp = xh.shape
    y = y + d_skip.astype(jnp.float32)[:, None] * xh.astype(jnp.float32)
    y = y.reshape(b, l, nh * hp) * jax.nn.silu(z.astype(jnp.float32))
    return rms_norm(y, norm_w).astype(z.dtype)


def ret_q(rq, rope):
    b, l, _ = rq.shape
    return rope_apply(rq.reshape(b, l, RET_HEADS, RET_DK), *rope)


def ret_kv(rk, rv, rope):
    b, l, _ = rk.shape
    k = rope_apply(rk.reshape(b, l, RET_HEADS, RET_DK), *rope) * (RET_DK ** -0.5)
    return k, rv.reshape(b, l, RET_HEADS, RET_DV)


def ret_finish(y, g, gn_w):
    b, l, nh, dv = y.shape
    mu = jnp.mean(y, axis=-1, keepdims=True)
    yc = y - mu
    var = jnp.mean(yc * yc, axis=-1, keepdims=True)
    yn = (yc * lax.rsqrt(var + NORM_EPS)).reshape(b, l, nh * dv) * gn_w.astype(jnp.float32)
    return (yn * jax.nn.silu(g.astype(jnp.float32))).astype(g.dtype)


def merge_branches(br_attn, br_ssd, br_ret, gate_logits, w_branch, w_out):
    b, l, _ = gate_logits.shape
    gates = jax.nn.sigmoid(gate_logits.reshape(b, l, N_BRANCH, D_MODEL))
    merged = (gates[:, :, 0] * (br_attn @ w_branch[0])
              + gates[:, :, 1] * (br_ssd @ w_branch[1])
              + gates[:, :, 2] * (br_ret @ w_branch[2]))
    return merged @ w_out


def sq_relu_mlp(x, w1, w2):
    h = jax.nn.relu(x @ w1)
    return (h * h) @ w2


def hybrid_mixer(u_lat, u_ctx, w_in, q_norm, k_norm, conv_w, conv_b, dt_bias, a_log, d_skip,
                 ssd_norm_w, ret_log_decay, ret_gn_w, w_branch, w_out,
                 rope_lat, ret_rope_ctx, ret_rope_lat, need_ctx):
    b = u_lat.shape[0]
    pl = split_cols(u_lat @ w_in)
    pc = split_cols(u_ctx @ w_in)

    k_c, v_c = attn_kv(pc[1], pc[2], k_norm, None)
    k_l, v_l = attn_kv(pl[1], pl[2], k_norm, rope_lat)
    q_l = attn_q(pl[0], q_norm, rope_lat)
    attn_l = gqa_attend(q_l, jnp.concatenate([k_c, k_l], axis=1), jnp.concatenate([v_c, v_l], axis=1))

    x_c, b_c, c_c, dtf_c, dtb_c = ssd_inputs(pc[4], pc[5], conv_w, conv_b, dt_bias)
    x_l, b_l, c_l, dtf_l, dtb_l = ssd_inputs(pl[4], pl[5], conv_w, conv_b, dt_bias)
    a_neg = -jnp.exp(a_log.astype(jnp.float32))
    s_init = jnp.zeros((b, SSD_GROUPS, SSD_HEADS // SSD_GROUPS, SSD_HEAD_DIM, SSD_STATE), jnp.float32)
    yc_f, st_f = ssd_scan(x_c, dtf_c, a_neg[0], b_c, c_c, s_init, need_ctx)
    yc_b, st_b = ssd_scan(flip(x_c), flip(dtb_c), a_neg[1], flip(b_c), flip(c_c), s_init, need_ctx)
    yl_f, _ = ssd_scan(x_l, dtf_l, a_neg[0], b_l, c_l, st_f, True)
    yl_b, _ = ssd_scan(flip(x_l), flip(dtb_l), a_neg[1], flip(b_l), flip(c_l), st_b, True)
    ssd_l = ssd_finish(yl_f + flip(yl_b), x_l, pl[3], d_skip, ssd_norm_w)

    lg = -jnp.exp(ret_log_decay.astype(jnp.float32))
    r_init = jnp.zeros((b, RET_HEADS, RET_DK, RET_DV), jnp.float32)
    rk_c, rv_c = ret_kv(pc[7], pc[8], ret_rope_ctx)
    rq_c = ret_q(pc[6], ret_rope_ctx) if need_ctx else None
    rk_l, rv_l = ret_kv(pl[7], pl[8], ret_rope_lat)
    rq_l = ret_q(pl[6], ret_rope_lat)
    rc_f, rs_f = retention_scan(rq_c, rk_c, rv_c, lg[0], r_init, need_ctx)
    rc_b, rs_b = retention_scan(None if rq_c is None else flip(rq_c), flip(rk_c), flip(rv_c), lg[1], r_init, need_ctx)
    rl_f, _ = retention_scan(rq_l, rk_l, rv_l, lg[0], rs_f, True)
    rl_b, _ = retention_scan(flip(rq_l), flip(rk_l), flip(rv_l), lg[1], rs_b, True)
    ret_l = ret_finish(rl_f + flip(rl_b), pl[9], ret_gn_w)

    out_l = merge_branches(attn_l, ssd_l, ret_l, pl[10], w_branch, w_out)
    if not need_ctx:
        return out_l, None

    attn_c = gqa_attend(attn_q(pc[0], q_norm, None), k_c, v_c)
    ssd_c = ssd_finish(yc_f + flip(yc_b), x_c, pc[3], d_skip, ssd_norm_w)
    ret_c = ret_finish(rc_f + flip(rc_b), pc[9], ret_gn_w)
    out_c = merge_branches(attn_c, ssd_c, ret_c, pc[10], w_branch, w_out)
    return out_l, out_c


def _fwd_setup_inputs(seed: int = 0) -> dict:
    key = jax.random.key(seed)
    ks = jax.random.split(key, 24)
    f32 = jnp.float32

    def nrm(k, shape, scale):
        return jax.random.normal(k, shape, f32) * scale

    dt = jnp.exp(jax.random.uniform(ks[13], (DEPTH, 2, SSD_HEADS), f32, math.log(1e-3), math.log(1e-1)))
    return {
        'x': nrm(ks[0], (BATCH, SEQ, D_MODEL), 1.0),
        'c': nrm(ks[1], (BATCH, D_MODEL), 1.0),
        'ctx': nrm(ks[2], (BATCH, CTX_LEN, D_MODEL), 1.0),
        'c_ctx': nrm(ks[3], (D_MODEL,), 1.0),
        'w_mod': nrm(ks[4], (DEPTH, D_MODEL, 6 * D_MODEL), 0.5 * D_MODEL ** -0.5),
        'b_mod': nrm(ks[5], (DEPTH, 6 * D_MODEL), 0.01),
        'norm1_w': 1.0 + nrm(ks[6], (DEPTH, D_MODEL), 0.02),
        'norm2_w': 1.0 + nrm(ks[7], (DEPTH, D_MODEL), 0.02),
        'w_in': nrm(ks[8], (DEPTH, D_MODEL, IN_DIM), D_MODEL ** -0.5),
        'attn_q_norm': 1.0 + nrm(ks[9], (DEPTH, ATTN_HEAD_DIM), 0.02),
        'attn_k_norm': 1.0 + nrm(ks[10], (DEPTH, ATTN_HEAD_DIM), 0.02),
        'ssd_conv_w': nrm(ks[11], (DEPTH, SSD_CONV_K, SSD_CONV_DIM), SSD_CONV_K ** -0.5),
        'ssd_conv_b': nrm(ks[12], (DEPTH, SSD_CONV_DIM), 0.01),
        'ssd_dt_bias': dt + jnp.log(-jnp.expm1(-dt)),
        'ssd_a_log': jnp.log(jax.random.uniform(ks[14], (DEPTH, 2, SSD_HEADS), f32, 1.0, 16.0)),
        'ssd_d': 1.0 + nrm(ks[15], (DEPTH, SSD_HEADS), 0.1),
        'ssd_norm_w': 1.0 + nrm(ks[16], (DEPTH, SSD_D_INNER), 0.02),
        'ret_log_decay': (-5.0 - jnp.arange(RET_HEADS, dtype=f32)) * math.log(2.0)
                         + nrm(ks[17], (DEPTH, 2, RET_HEADS), 0.1),
        'ret_gn_w': 1.0 + nrm(ks[18], (DEPTH, RET_HEADS * RET_DV), 0.02),
        'w_branch': nrm(ks[19], (DEPTH, N_BRANCH, BRANCH_W, D_MODEL), BRANCH_W ** -0.5),
        'w_out': nrm(ks[20], (DEPTH, D_MODEL, D_MODEL), D_MODEL ** -0.5),
        'w_mlp1': nrm(ks[21], (DEPTH, D_MODEL, MLP_HIDDEN), D_MODEL ** -0.5),
        'w_mlp2': nrm(ks[22], (DEPTH, MLP_HIDDEN, D_MODEL), MLP_HIDDEN ** -0.5),
        'final_norm_w': 1.0 + nrm(ks[23], (D_MODEL,), 0.02),
    }


def _fwd_reference(x, c, ctx, c_ctx, w_mod, b_mod, norm1_w, norm2_w, w_in, attn_q_norm, attn_k_norm,
              ssd_conv_w, ssd_conv_b, ssd_dt_bias, ssd_a_log, ssd_d, ssd_norm_w, ret_log_decay,
              ret_gn_w, w_branch, w_out, w_mlp1, w_mlp2, final_norm_w):
    n = x.shape[1]
    m = ctx.shape[1]
    ROWS = n // GRID_W
    rope_lat = axial_angles(ROWS)
    ret_rope_ctx = seq_angles(0, m)
    ret_rope_lat = seq_angles(m, n)
    h_lat, h_ctx = x, ctx
    for layer in range(DEPTH):
        need_ctx = layer < DEPTH - 1
        mod_lat = jnp.split(jax.nn.silu(c) @ w_mod[layer] + b_mod[layer], 6, axis=-1)
        mod_ctx = jnp.split(jax.nn.silu(c_ctx)[None, :] @ w_mod[layer] + b_mod[layer], 6, axis=-1)
        u_lat = modulate(rms_norm(h_lat, norm1_w[layer]), mod_lat[0], mod_lat[1])
        u_ctx = modulate(rms_norm(h_ctx, norm1_w[layer]), mod_ctx[0], mod_ctx[1])
        mix_lat, mix_ctx = hybrid_mixer(
            u_lat, u_ctx, w_in[layer], attn_q_norm[layer], attn_k_norm[layer],
            ssd_conv_w[layer], ssd_conv_b[layer], ssd_dt_bias[layer], ssd_a_log[layer], ssd_d[layer],
            ssd_norm_w[layer], ret_log_decay[layer], ret_gn_w[layer], w_branch[layer], w_out[layer],
            rope_lat, ret_rope_ctx, ret_rope_lat, need_ctx)
        h_lat = h_lat + mod_lat[2][:, None, :] * mix_lat
        v_lat = modulate(rms_norm(h_lat, norm2_w[layer]), mod_lat[3], mod_lat[4])
        h_lat = h_lat + mod_lat[5][:, None, :] * sq_relu_mlp(v_lat, w_mlp1[layer], w_mlp2[layer])
        if need_ctx:
            h_ctx = h_ctx + mod_ctx[2][:, None, :] * mix_ctx
            v_ctx = modulate(rms_norm(h_ctx, norm2_w[layer]), mod_ctx[3], mod_ctx[4])
            h_ctx = h_ctx + mod_ctx[5][:, None, :] * sq_relu_mlp(v_ctx, w_mlp1[layer], w_mlp2[layer])
    return rms_norm(h_lat, final_norm_w)


import jax as _jax
import jax.numpy as _jnp

TWIN_FORMAT = 'train_step'
FWD_PARAMS = ['x', 'c', 'ctx', 'c_ctx', 'w_mod', 'b_mod', 'norm1_w', 'norm2_w', 'w_in', 'attn_q_norm', 'attn_k_norm', 'ssd_conv_w', 'ssd_conv_b', 'ssd_dt_bias', 'ssd_a_log', 'ssd_d', 'ssd_norm_w', 'ret_log_decay', 'ret_gn_w', 'w_branch', 'w_out', 'w_mlp1', 'w_mlp2', 'final_norm_w']
TWIN_WEIGHTS = ['c_ctx', 'w_mod', 'b_mod', 'norm1_w', 'norm2_w', 'w_in', 'attn_q_norm', 'attn_k_norm', 'ssd_conv_w', 'ssd_conv_b', 'ssd_dt_bias', 'ssd_a_log', 'ssd_d', 'ssd_norm_w', 'ret_log_decay', 'ret_gn_w', 'w_branch', 'w_out', 'w_mlp1', 'w_mlp2', 'final_norm_w']
TWIN_DIFF_INPUT = 'x'
TWIN_INPUTS = ['x', 'c', 'ctx', 'c_ctx', 'w_mod', 'b_mod', 'norm1_w', 'norm2_w', 'w_in', 'attn_q_norm', 'attn_k_norm', 'ssd_conv_w', 'ssd_conv_b', 'ssd_dt_bias', 'ssd_a_log', 'ssd_d', 'ssd_norm_w', 'ret_log_decay', 'ret_gn_w', 'w_branch', 'w_out', 'w_mlp1', 'w_mlp2', 'final_norm_w', 'loss_target', 'm_c_ctx', 'm_w_mod', 'm_b_mod', 'm_norm1_w', 'm_norm2_w', 'm_w_in', 'm_attn_q_norm', 'm_attn_k_norm', 'm_ssd_conv_w', 'm_ssd_conv_b', 'm_ssd_dt_bias', 'm_ssd_a_log', 'm_ssd_d', 'm_ssd_norm_w', 'm_ret_log_decay', 'm_ret_gn_w', 'm_w_branch', 'm_w_out', 'm_w_mlp1', 'm_w_mlp2', 'm_final_norm_w', 'v_c_ctx', 'v_w_mod', 'v_b_mod', 'v_norm1_w', 'v_norm2_w', 'v_w_in', 'v_attn_q_norm', 'v_attn_k_norm', 'v_ssd_conv_w', 'v_ssd_conv_b', 'v_ssd_dt_bias', 'v_ssd_a_log', 'v_ssd_d', 'v_ssd_norm_w', 'v_ret_log_decay', 'v_ret_gn_w', 'v_w_branch', 'v_w_out', 'v_w_mlp1', 'v_w_mlp2', 'v_final_norm_w']
TWIN_OUTPUTS = ['loss', 'grad_x', 'grad_c_ctx', 'grad_w_mod', 'grad_b_mod', 'grad_norm1_w', 'grad_norm2_w', 'grad_w_in', 'grad_attn_q_norm', 'grad_attn_k_norm', 'grad_ssd_conv_w', 'grad_ssd_conv_b', 'grad_ssd_dt_bias', 'grad_ssd_a_log', 'grad_ssd_d', 'grad_ssd_norm_w', 'grad_ret_log_decay', 'grad_ret_gn_w', 'grad_w_branch', 'grad_w_out', 'grad_w_mlp1', 'grad_w_mlp2', 'grad_final_norm_w', 'delta_c_ctx', 'delta_w_mod', 'delta_b_mod', 'delta_norm1_w', 'delta_norm2_w', 'delta_w_in', 'delta_attn_q_norm', 'delta_attn_k_norm', 'delta_ssd_conv_w', 'delta_ssd_conv_b', 'delta_ssd_dt_bias', 'delta_ssd_a_log', 'delta_ssd_d', 'delta_ssd_norm_w', 'delta_ret_log_decay', 'delta_ret_gn_w', 'delta_w_branch', 'delta_w_out', 'delta_w_mlp1', 'delta_w_mlp2', 'delta_final_norm_w', 'new_m_c_ctx', 'new_m_w_mod', 'new_m_b_mod', 'new_m_norm1_w', 'new_m_norm2_w', 'new_m_w_in', 'new_m_attn_q_norm', 'new_m_attn_k_norm', 'new_m_ssd_conv_w', 'new_m_ssd_conv_b', 'new_m_ssd_dt_bias', 'new_m_ssd_a_log', 'new_m_ssd_d', 'new_m_ssd_norm_w', 'new_m_ret_log_decay', 'new_m_ret_gn_w', 'new_m_w_branch', 'new_m_w_out', 'new_m_w_mlp1', 'new_m_w_mlp2', 'new_m_final_norm_w', 'new_v_c_ctx', 'new_v_w_mod', 'new_v_b_mod', 'new_v_norm1_w', 'new_v_norm2_w', 'new_v_w_in', 'new_v_attn_q_norm', 'new_v_attn_k_norm', 'new_v_ssd_conv_w', 'new_v_ssd_conv_b', 'new_v_ssd_dt_bias', 'new_v_ssd_a_log', 'new_v_ssd_d', 'new_v_ssd_norm_w', 'new_v_ret_log_decay', 'new_v_ret_gn_w', 'new_v_w_branch', 'new_v_w_out', 'new_v_w_mlp1', 'new_v_w_mlp2', 'new_v_final_norm_w']
TWIN_LEAF_KINDS = {'loss': 'loss', 'grad_x': 'grad_x', 'grad_c_ctx': 'grad_w', 'grad_w_mod': 'grad_w', 'grad_b_mod': 'grad_w', 'grad_norm1_w': 'grad_w', 'grad_norm2_w': 'grad_w', 'grad_w_in': 'grad_w', 'grad_attn_q_norm': 'grad_w', 'grad_attn_k_norm': 'grad_w', 'grad_ssd_conv_w': 'grad_w', 'grad_ssd_conv_b': 'grad_w', 'grad_ssd_dt_bias': 'grad_w', 'grad_ssd_a_log': 'grad_w', 'grad_ssd_d': 'grad_w', 'grad_ssd_norm_w': 'grad_w', 'grad_ret_log_decay': 'grad_w', 'grad_ret_gn_w': 'grad_w', 'grad_w_branch': 'grad_w', 'grad_w_out': 'grad_w', 'grad_w_mlp1': 'grad_w', 'grad_w_mlp2': 'grad_w', 'grad_final_norm_w': 'grad_w', 'delta_c_ctx': 'delta_w', 'delta_w_mod': 'delta_w', 'delta_b_mod': 'delta_w', 'delta_norm1_w': 'delta_w', 'delta_norm2_w': 'delta_w', 'delta_w_in': 'delta_w', 'delta_attn_q_norm': 'delta_w', 'delta_attn_k_norm': 'delta_w', 'delta_ssd_conv_w': 'delta_w', 'delta_ssd_conv_b': 'delta_w', 'delta_ssd_dt_bias': 'delta_w', 'delta_ssd_a_log': 'delta_w', 'delta_ssd_d': 'delta_w', 'delta_ssd_norm_w': 'delta_w', 'delta_ret_log_decay': 'delta_w', 'delta_ret_gn_w': 'delta_w', 'delta_w_branch': 'delta_w', 'delta_w_out': 'delta_w', 'delta_w_mlp1': 'delta_w', 'delta_w_mlp2': 'delta_w', 'delta_final_norm_w': 'delta_w', 'new_m_c_ctx': 'new_m', 'new_m_w_mod': 'new_m', 'new_m_b_mod': 'new_m', 'new_m_norm1_w': 'new_m', 'new_m_norm2_w': 'new_m', 'new_m_w_in': 'new_m', 'new_m_attn_q_norm': 'new_m', 'new_m_attn_k_norm': 'new_m', 'new_m_ssd_conv_w': 'new_m', 'new_m_ssd_conv_b': 'new_m', 'new_m_ssd_dt_bias': 'new_m', 'new_m_ssd_a_log': 'new_m', 'new_m_ssd_d': 'new_m', 'new_m_ssd_norm_w': 'new_m', 'new_m_ret_log_decay': 'new_m', 'new_m_ret_gn_w': 'new_m', 'new_m_w_branch': 'new_m', 'new_m_w_out': 'new_m', 'new_m_w_mlp1': 'new_m', 'new_m_w_mlp2': 'new_m', 'new_m_final_norm_w': 'new_m', 'new_v_c_ctx': 'new_v', 'new_v_w_mod': 'new_v', 'new_v_b_mod': 'new_v', 'new_v_norm1_w': 'new_v', 'new_v_norm2_w': 'new_v', 'new_v_w_in': 'new_v', 'new_v_attn_q_norm': 'new_v', 'new_v_attn_k_norm': 'new_v', 'new_v_ssd_conv_w': 'new_v', 'new_v_ssd_conv_b': 'new_v', 'new_v_ssd_dt_bias': 'new_v', 'new_v_ssd_a_log': 'new_v', 'new_v_ssd_d': 'new_v', 'new_v_ssd_norm_w': 'new_v', 'new_v_ret_log_decay': 'new_v', 'new_v_ret_gn_w': 'new_v', 'new_v_w_branch': 'new_v', 'new_v_w_out': 'new_v', 'new_v_w_mlp1': 'new_v', 'new_v_w_mlp2': 'new_v', 'new_v_final_norm_w': 'new_v'}


def _forward(args):
    return _fwd_reference(*[args[k] for k in FWD_PARAMS])


def _output_shape():
    def fwd():
        inp = _fwd_setup_inputs(0)
        return _fwd_reference(*[inp[k] for k in FWD_PARAMS])
    out = _jax.eval_shape(fwd)
    return out.shape, out.dtype

N_MICROBATCH = 1
ADAM_LR = 0.001
ADAM_B1 = 0.9
ADAM_B2 = 0.999
ADAM_EPS = 1e-08
ADAM_WD = 0.01
ADAM_STEP = 10
PER_EXAMPLE_BATCH_AXIS = {'x': 0, 'c': 0, 'ctx': 0, 'loss_target': 0}
SHARED_INPUTS = []
_WEIGHT_DTYPES = {'c_ctx': _jnp.float32, 'w_mod': _jnp.float32, 'b_mod': _jnp.float32, 'norm1_w': _jnp.float32, 'norm2_w': _jnp.float32, 'w_in': _jnp.float32, 'attn_q_norm': _jnp.float32, 'attn_k_norm': _jnp.float32, 'ssd_conv_w': _jnp.float32, 'ssd_conv_b': _jnp.float32, 'ssd_dt_bias': _jnp.float32, 'ssd_a_log': _jnp.float32, 'ssd_d': _jnp.float32, 'ssd_norm_w': _jnp.float32, 'ret_log_decay': _jnp.float32, 'ret_gn_w': _jnp.float32, 'w_branch': _jnp.float32, 'w_out': _jnp.float32, 'w_mlp1': _jnp.float32, 'w_mlp2': _jnp.float32, 'final_norm_w': _jnp.float32}
MOMENT_SCALE = {'c_ctx': 2.264378e-02, 'w_mod': 1.161210e-01, 'b_mod': 2.016309e-01, 'norm1_w': 7.863734e-02, 'norm2_w': 1.084982e-01, 'w_in': 3.208794e-02, 'attn_q_norm': 1.436613e-02, 'attn_k_norm': 1.463506e-02, 'ssd_conv_w': 4.589635e-02, 'ssd_conv_b': 5.894703e-02, 'ssd_dt_bias': 1.440846e-01, 'ssd_a_log': 1.654496e-01, 'ssd_d': 1.882692e-01, 'ssd_norm_w': 6.174726e-02, 'ret_log_decay': 1.599812e-01, 'ret_gn_w': 3.974371e-02, 'w_branch': 2.899625e-02, 'w_out': 5.063601e-02, 'w_mlp1': 5.657339e-02, 'w_mlp2': 1.040339e-01, 'final_norm_w': 6.440719e+01}


def _to_microbatches(a, axis):
    t = _jnp.moveaxis(a, axis, 0)
    t = t.reshape((N_MICROBATCH, t.shape[0] // N_MICROBATCH) + t.shape[1:])
    return _jnp.moveaxis(t, 1, axis + 1)


def setup_inputs(seed: int = 0) -> dict:
    inp = _fwd_setup_inputs(seed)
    key = _jax.random.fold_in(_jax.random.key(seed), 7919)
    shape, _ = _output_shape()
    out = dict(inp)
    out["loss_target"] = _jax.random.normal(_jax.random.fold_in(key, 0), shape, _jnp.float32)
    for i, name in enumerate(TWIN_WEIGHTS):
        w = inp[name].astype(_jnp.float32)
        if MOMENT_SCALE is None:
            s = _jnp.sqrt(_jnp.mean(_jnp.square(w)) + 1e-30)
        else:
            s = MOMENT_SCALE[name]
        km, kv = _jax.random.split(_jax.random.fold_in(key, i + 1))
        out[name] = w
        out["m_" + name] = s * _jax.random.normal(km, w.shape, _jnp.float32)
        out["v_" + name] = (s * s) * _jax.random.uniform(kv, w.shape, _jnp.float32, 0.5, 1.5)
    if N_MICROBATCH > 1:
        for name, axis in PER_EXAMPLE_BATCH_AXIS.items():
            out[name] = _to_microbatches(out[name], axis)
    return {'x': out['x'], 'c': out['c'], 'ctx': out['ctx'], 'c_ctx': out['c_ctx'], 'w_mod': out['w_mod'], 'b_mod': out['b_mod'], 'norm1_w': out['norm1_w'], 'norm2_w': out['norm2_w'], 'w_in': out['w_in'], 'attn_q_norm': out['attn_q_norm'], 'attn_k_norm': out['attn_k_norm'], 'ssd_conv_w': out['ssd_conv_w'], 'ssd_conv_b': out['ssd_conv_b'], 'ssd_dt_bias': out['ssd_dt_bias'], 'ssd_a_log': out['ssd_a_log'], 'ssd_d': out['ssd_d'], 'ssd_norm_w': out['ssd_norm_w'], 'ret_log_decay': out['ret_log_decay'], 'ret_gn_w': out['ret_gn_w'], 'w_branch': out['w_branch'], 'w_out': out['w_out'], 'w_mlp1': out['w_mlp1'], 'w_mlp2': out['w_mlp2'], 'final_norm_w': out['final_norm_w'], 'loss_target': out['loss_target'], 'm_c_ctx': out['m_c_ctx'], 'm_w_mod': out['m_w_mod'], 'm_b_mod': out['m_b_mod'], 'm_norm1_w': out['m_norm1_w'], 'm_norm2_w': out['m_norm2_w'], 'm_w_in': out['m_w_in'], 'm_attn_q_norm': out['m_attn_q_norm'], 'm_attn_k_norm': out['m_attn_k_norm'], 'm_ssd_conv_w': out['m_ssd_conv_w'], 'm_ssd_conv_b': out['m_ssd_conv_b'], 'm_ssd_dt_bias': out['m_ssd_dt_bias'], 'm_ssd_a_log': out['m_ssd_a_log'], 'm_ssd_d': out['m_ssd_d'], 'm_ssd_norm_w': out['m_ssd_norm_w'], 'm_ret_log_decay': out['m_ret_log_decay'], 'm_ret_gn_w': out['m_ret_gn_w'], 'm_w_branch': out['m_w_branch'], 'm_w_out': out['m_w_out'], 'm_w_mlp1': out['m_w_mlp1'], 'm_w_mlp2': out['m_w_mlp2'], 'm_final_norm_w': out['m_final_norm_w'], 'v_c_ctx': out['v_c_ctx'], 'v_w_mod': out['v_w_mod'], 'v_b_mod': out['v_b_mod'], 'v_norm1_w': out['v_norm1_w'], 'v_norm2_w': out['v_norm2_w'], 'v_w_in': out['v_w_in'], 'v_attn_q_norm': out['v_attn_q_norm'], 'v_attn_k_norm': out['v_attn_k_norm'], 'v_ssd_conv_w': out['v_ssd_conv_w'], 'v_ssd_conv_b': out['v_ssd_conv_b'], 'v_ssd_dt_bias': out['v_ssd_dt_bias'], 'v_ssd_a_log': out['v_ssd_a_log'], 'v_ssd_d': out['v_ssd_d'], 'v_ssd_norm_w': out['v_ssd_norm_w'], 'v_ret_log_decay': out['v_ret_log_decay'], 'v_ret_gn_w': out['v_ret_gn_w'], 'v_w_branch': out['v_w_branch'], 'v_w_out': out['v_w_out'], 'v_w_mlp1': out['v_w_mlp1'], 'v_w_mlp2': out['v_w_mlp2'], 'v_final_norm_w': out['v_final_norm_w']}


def _loss(weights, diff, rest, loss_target):
    with _jax.named_scope("forward"):
        args = {**rest, TWIN_DIFF_INPUT: diff, **{k: w.astype(_WEIGHT_DTYPES[k]) for k, w in weights.items()}}
        y = _forward(args)
    with _jax.named_scope("loss_head"):
        err = _jnp.square(y.astype(_jnp.float32) - loss_target)
        return 0.5 * _jnp.sum(_jnp.mean(err, axis=-1)) if err.ndim else 0.5 * err


def _adamw(w, g, m, v):
    m = ADAM_B1 * m + (1.0 - ADAM_B1) * g
    v = ADAM_B2 * v + (1.0 - ADAM_B2) * _jnp.square(g)
    m_hat = m / (1.0 - ADAM_B1 ** ADAM_STEP)
    v_hat = v / (1.0 - ADAM_B2 ** ADAM_STEP)
    delta = -ADAM_LR * (m_hat / (_jnp.sqrt(v_hat) + ADAM_EPS) + ADAM_WD * w)
    return delta, m, v


def reference(x, c, ctx, c_ctx, w_mod, b_mod, norm1_w, norm2_w, w_in, attn_q_norm, attn_k_norm, ssd_conv_w, ssd_conv_b, ssd_dt_bias, ssd_a_log, ssd_d, ssd_norm_w, ret_log_decay, ret_gn_w, w_branch, w_out, w_mlp1, w_mlp2, final_norm_w, loss_target, m_c_ctx, m_w_mod, m_b_mod, m_norm1_w, m_norm2_w, m_w_in, m_attn_q_norm, m_attn_k_norm, m_ssd_conv_w, m_ssd_conv_b, m_ssd_dt_bias, m_ssd_a_log, m_ssd_d, m_ssd_norm_w, m_ret_log_decay, m_ret_gn_w, m_w_branch, m_w_out, m_w_mlp1, m_w_mlp2, m_final_norm_w, v_c_ctx, v_w_mod, v_b_mod, v_norm1_w, v_norm2_w, v_w_in, v_attn_q_norm, v_attn_k_norm, v_ssd_conv_w, v_ssd_conv_b, v_ssd_dt_bias, v_ssd_a_log, v_ssd_d, v_ssd_norm_w, v_ret_log_decay, v_ret_gn_w, v_w_branch, v_w_out, v_w_mlp1, v_w_mlp2, v_final_norm_w):
    given = dict(x=x, c=c, ctx=ctx, c_ctx=c_ctx, w_mod=w_mod, b_mod=b_mod, norm1_w=norm1_w, norm2_w=norm2_w, w_in=w_in, attn_q_norm=attn_q_norm, attn_k_norm=attn_k_norm, ssd_conv_w=ssd_conv_w, ssd_conv_b=ssd_conv_b, ssd_dt_bias=ssd_dt_bias, ssd_a_log=ssd_a_log, ssd_d=ssd_d, ssd_norm_w=ssd_norm_w, ret_log_decay=ret_log_decay, ret_gn_w=ret_gn_w, w_branch=w_branch, w_out=w_out, w_mlp1=w_mlp1, w_mlp2=w_mlp2, final_norm_w=final_norm_w, loss_target=loss_target, m_c_ctx=m_c_ctx, m_w_mod=m_w_mod, m_b_mod=m_b_mod, m_norm1_w=m_norm1_w, m_norm2_w=m_norm2_w, m_w_in=m_w_in, m_attn_q_norm=m_attn_q_norm, m_attn_k_norm=m_attn_k_norm, m_ssd_conv_w=m_ssd_conv_w, m_ssd_conv_b=m_ssd_conv_b, m_ssd_dt_bias=m_ssd_dt_bias, m_ssd_a_log=m_ssd_a_log, m_ssd_d=m_ssd_d, m_ssd_norm_w=m_ssd_norm_w, m_ret_log_decay=m_ret_log_decay, m_ret_gn_w=m_ret_gn_w, m_w_branch=m_w_branch, m_w_out=m_w_out, m_w_mlp1=m_w_mlp1, m_w_mlp2=m_w_mlp2, m_final_norm_w=m_final_norm_w, v_c_ctx=v_c_ctx, v_w_mod=v_w_mod, v_b_mod=v_b_mod, v_norm1_w=v_norm1_w, v_norm2_w=v_norm2_w, v_w_in=v_w_in, v_attn_q_norm=v_attn_q_norm, v_attn_k_norm=v_attn_k_norm, v_ssd_conv_w=v_ssd_conv_w, v_ssd_conv_b=v_ssd_conv_b, v_ssd_dt_bias=v_ssd_dt_bias, v_ssd_a_log=v_ssd_a_log, v_ssd_d=v_ssd_d, v_ssd_norm_w=v_ssd_norm_w, v_ret_log_decay=v_ret_log_decay, v_ret_gn_w=v_ret_gn_w, v_w_branch=v_w_branch, v_w_out=v_w_out, v_w_mlp1=v_w_mlp1, v_w_mlp2=v_w_mlp2, v_final_norm_w=v_final_norm_w)
    weights = {n: given[n] for n in TWIN_WEIGHTS}
    shared = {n: given[n] for n in SHARED_INPUTS}
    per_example = {n: given[n] for n in ['x', 'c', 'ctx']}
    grad_fn = _jax.value_and_grad(_loss, argnums=(0, 1))

    def one_microbatch(ex, loss_target):
        ex = dict(ex)
        diff = ex.pop(TWIN_DIFF_INPUT)
        return grad_fn(weights, diff, {**shared, **ex}, loss_target)

    if N_MICROBATCH == 1:
        loss, (grad_w, grad_x) = one_microbatch(per_example, given["loss_target"])
    else:
        def body(carry, xs):
            loss_sum, grad_sum = carry
            l_k, (gw_k, gx_k) = one_microbatch(xs[0], xs[1])
            with _jax.named_scope("update"):
                return (loss_sum + l_k, _jax.tree.map(_jnp.add, grad_sum, gw_k)), gx_k

        init = (_jnp.zeros((), _jnp.float32), _jax.tree.map(_jnp.zeros_like, weights))
        (loss, grad_w), grad_x = _jax.lax.scan(body, init, (per_example, given["loss_target"]))
    with _jax.named_scope("update"):
        delta_w, new_m, new_v = {}, {}, {}
        for n in TWIN_WEIGHTS:
            delta_w[n], new_m[n], new_v[n] = _adamw(weights[n], grad_w[n], given["m_" + n], given["v_" + n])
    return (loss, grad_x, *[grad_w[n] for n in TWIN_WEIGHTS], *[delta_w[n] for n in TWIN_WEIGHTS],
            *[new_m[n] for n in TWIN_WEIGHTS], *[new_v[n] for n in TWIN_WEIGHTS])
```

```python
import functools
import math
from typing import NamedTuple

import jax
import jax.numpy as jnp
from jax import lax
from jax.experimental import pallas as pl
from jax.experimental.pallas import tpu as pltpu

F32 = jnp.float32
CDT = jnp.bfloat16
NORM_EPS = 1e-6
ROPE_THETA = 10000.0
GRID_W = 64
D_MODEL = 1024
ATTN_HEADS, ATTN_KV, ATTN_HD = 8, 2, 64
SSD_HEADS, SSD_HD, SSD_STATE = 8, 64, 128
RET_HEADS, RET_DK = 4, 128
CHUNK = 128
ROW_TILE = 256
MM_ROWS = 768
ATTN_TQ, ATTN_TK = 256, 256
LANES = 128
PACK_ROWS = 512
COND_ROWS = 16
VMEM_LIMIT = 56 * 1024 * 1024

ADAM_LR, ADAM_B1, ADAM_B2, ADAM_EPS, ADAM_WD, ADAM_STEP = 0.001, 0.9, 0.999, 1e-08, 0.01, 10

IN_LAYOUT = {
    "gates": (0, 3072, 4368, 3072), "xbc": (3072, 1024, 1280, 1024), "q": (4096, 512, 0, 512),
    "z": (4608, 512, 768, 512), "rq": (5120, 512, 2320, 512), "rk": (5632, 512, 2832, 512),
    "rv": (6144, 512, 3344, 512), "rg": (6656, 512, 3856, 512), "k": (7168, 128, 512, 128),
    "v": (7296, 128, 640, 128), "dt": (7424, 128, 2304, 16),
}
IN_PAD = 7680
IN_ORIG_ORDER = ("q", "k", "v", "z", "xbc", "dt", "rq", "rk", "rv", "rg", "gates")
IN_NEW_ORDER = ("gates", "xbc", "q", "z", "rq", "rk", "rv", "rg", "k", "v", "dt")

BIG = ("w_mod", "w_in", "w_branch", "w_out", "w_mlp1", "w_mlp2")
BIG_AXIS = {"w_mod": 2, "w_in": 2, "w_branch": 3, "w_out": 1, "w_mlp1": 2, "w_mlp2": 1}
SMALL = ("c_ctx", "b_mod", "norm1_w", "norm2_w", "attn_q_norm", "attn_k_norm", "ssd_conv_w", "ssd_conv_b",
         "ssd_dt_bias", "ssd_a_log", "ssd_d", "ssd_norm_w", "ret_log_decay", "ret_gn_w", "final_norm_w")
WEIGHTS = ("c_ctx", "w_mod", "b_mod", "norm1_w", "norm2_w", "w_in", "attn_q_norm", "attn_k_norm", "ssd_conv_w",
           "ssd_conv_b", "ssd_dt_bias", "ssd_a_log", "ssd_d", "ssd_norm_w", "ret_log_decay", "ret_gn_w",
           "w_branch", "w_out", "w_mlp1", "w_mlp2", "final_norm_w")


def _cp(sem):
    return pltpu.CompilerParams(dimension_semantics=sem, vmem_limit_bytes=VMEM_LIMIT)


class Cols(NamedTuple):
    arr: jax.Array
    off: int
    width: int


def _width(item):
    return item.width if isinstance(item, Cols) else item.shape[1]


def _row_in(item, rows, imap=None):
    imap = imap or (lambda i: i)
    if isinstance(item, Cols):
        assert item.off % item.width == 0
        blk = item.off // item.width
        return item.arr, pl.BlockSpec((rows, item.width), lambda i, blk=blk: (imap(i), blk))
    return item, pl.BlockSpec((rows, item.shape[1]), lambda i: (imap(i), 0))


def _const_spec(shape):
    return pl.BlockSpec(shape, lambda *_: (0,) * len(shape))


def _mxu(a, b, dims=(((1,), (0,)), ((), ()))):
    return lax.dot_general(a.astype(CDT), b.astype(CDT), dims, preferred_element_type=F32)


_NT = (((1,), (1,)), ((), ()))
_TN = (((0,), (0,)), ((), ()))


@jax.custom_vjp
def _softplus(x):
    return jnp.maximum(x, 0.0) + jnp.log1p(jnp.exp(-jnp.abs(x)))


def _softplus_fwd(x):
    return _softplus(x), x


def _softplus_bwd(x, g):
    return (g * jax.nn.sigmoid(x),)


_softplus.defvjp(_softplus_fwd, _softplus_bwd)


def _group_mean_impl(x, gmat):
    hi = x.astype(CDT)
    lo = (x - hi.astype(F32)).astype(CDT)
    return (jnp.dot(hi, gmat, preferred_element_type=F32) + jnp.dot(lo, gmat, preferred_element_type=F32))


@jax.custom_vjp
def _group_mean(x, gmat):
    return _group_mean_impl(x, gmat)


def _group_mean_fwd(x, gmat):
    return _group_mean_impl(x, gmat), gmat


def _group_mean_bwd(gmat, g):
    return _group_mean_impl(g, gmat), jnp.zeros_like(gmat)


_group_mean.defvjp(_group_mean_fwd, _group_mean_bwd)


def _group_matrix(width, group):
    r = jnp.arange(width) // group
    return jnp.where(r[:, None] == r[None, :], 1.0 / group, 0.0).astype(CDT)


def _make_rope(half):
    def partner(x):
        w = x.shape[1]
        lane = lax.broadcasted_iota(jnp.int32, x.shape, 1)
        first = (lane % (2 * half)) < half
        return jnp.where(first, pltpu.roll(x, w - half, axis=1), pltpu.roll(x, half, axis=1))

    def impl(x, cos_full, sin_signed):
        return x * cos_full + partner(x) * sin_signed

    @jax.custom_vjp
    def rope(x, cos_full, sin_signed):
        return impl(x, cos_full, sin_signed)

    def fwd(x, cos_full, sin_signed):
        return impl(x, cos_full, sin_signed), (cos_full, sin_signed)

    def bwd(res, g):
        cos_full, sin_signed = res
        return impl(g, cos_full, -sin_signed), jnp.zeros_like(cos_full), jnp.zeros_like(sin_signed)

    rope.defvjp(fwd, bwd)
    return rope


_rope32 = _make_rope(32)
_rope64 = _make_rope(64)


def _rms(x, w):
    return x * lax.rsqrt(jnp.mean(x * x, axis=-1, keepdims=True) + NORM_EPS) * w


def _col(v, lane_index):
    lane = lax.broadcasted_iota(jnp.int32, v.shape, 1)
    return jnp.sum(jnp.where(lane == lane_index, v, 0.0), axis=1, keepdims=True)


def _typed_spec(width, nb_ctx):
    return pl.BlockSpec((None, 1, width), lambda i: (jnp.where(i >= nb_ctx, 1, 0), 0, 0))


def rowwise_fwd(name, f, rows, typed, shared, outs, n_rows, nb_ctx, tm=ROW_TILE):
    tm = min(tm, n_rows)
    nin = len(rows) + len(typed) + len(shared)

    def body(*refs):
        res = f(*[r[...] for r in refs[:nin]])
        for o_ref, o in zip(refs[nin:], res):
            o_ref[...] = o.astype(o_ref.dtype)

    arrs, specs = [], []
    for it in rows:
        a, s = _row_in(it, tm)
        arrs.append(a)
        specs.append(s)
    for t in typed:
        arrs.append(t)
        specs.append(_typed_spec(t.shape[-1], nb_ctx))
    for s_ in shared:
        arrs.append(s_)
        specs.append(_const_spec(s_.shape))
    res = pl.pallas_call(
        body, name=name, grid=(n_rows // tm,), in_specs=specs,
        out_specs=[pl.BlockSpec((tm, w), lambda i: (i, 0)) for w, _ in outs],
        out_shape=[jax.ShapeDtypeStruct((n_rows, w), dt) for w, dt in outs],
        compiler_params=_cp(("parallel",)))(*arrs)
    return res


def rowwise_bwd(name, f, rows, typed, shared, cots, row_diff, shared_diff, drow_dtypes, n_rows, nb_ctx, tm=ROW_TILE):
    tm = min(tm, n_rows)
    nr, nt, ns, nc = len(rows), len(typed), len(shared), len(cots)
    nin = nr + nt + ns
    d_rows = [k for k in range(nr) if row_diff[k]]
    d_sh = [k for k in range(ns) if shared_diff[k]]

    def body(*refs):
        rvals = [r[...] for r in refs[:nr]]
        tvals = [r[...] for r in refs[nr:nr + nt]]
        svals = [r[...] for r in refs[nr + nt:nin]]
        cvals = [r[...].astype(F32) for r in refs[nin:nin + nc]]
        out_refs = refs[nin + nc:]

        def g(*dv):
            dv = list(dv)
            rv = list(rvals)
            for k in d_rows:
                rv[k] = dv.pop(0)
            tv = [dv.pop(0) for _ in range(nt)]
            sv = list(svals)
            for k in d_sh:
                sv[k] = dv.pop(0)
            return tuple(o.astype(F32) for o in f(*rv, *tv, *sv))

        prim = [rvals[k].astype(F32) for k in d_rows] + tvals + [svals[k] for k in d_sh]
        _, vjp = jax.vjp(g, *prim)
        grads = list(vjp(tuple(cvals)))
        i = pl.program_id(0)
        for ref in out_refs[:len(d_rows)]:
            ref[...] = grads.pop(0).astype(ref.dtype)
        first_typed = (i == 0) | (i == nb_ctx)
        for ref in out_refs[len(d_rows):len(d_rows) + nt]:
            gr = grads.pop(0)

            @pl.when(first_typed)
            def _(ref=ref, gr=gr):
                ref[...] = gr

            @pl.when(jnp.logical_not(first_typed))
            def _(ref=ref, gr=gr):
                ref[...] += gr
        for ref in out_refs[len(d_rows) + nt:]:
            gr = grads.pop(0)

            @pl.when(i == 0)
            def _(ref=ref, gr=gr):
                ref[...] = gr

            @pl.when(i != 0)
            def _(ref=ref, gr=gr):
                ref[...] += gr

    arrs, specs = [], []
    for it in list(rows):
        a, s = _row_in(it, tm)
        arrs.append(a)
        specs.append(s)
    for t in typed:
        arrs.append(t)
        specs.append(_typed_spec(t.shape[-1], nb_ctx))
    for s_ in shared:
        arrs.append(s_)
        specs.append(_const_spec(s_.shape))
    for c_ in cots:
        a, s = _row_in(c_, tm)
        arrs.append(a)
        specs.append(s)
    out_specs, out_shape = [], []
    for k, dt in zip(d_rows, drow_dtypes):
        w = _width(rows[k])
        out_specs.append(pl.BlockSpec((tm, w), lambda i: (i, 0)))
        out_shape.append(jax.ShapeDtypeStruct((n_rows, w), dt))
    for t in typed:
        out_specs.append(_typed_spec(t.shape[-1], nb_ctx))
        out_shape.append(jax.ShapeDtypeStruct(t.shape, F32))
    for k in d_sh:
        out_specs.append(_const_spec(shared[k].shape))
        out_shape.append(jax.ShapeDtypeStruct(shared[k].shape, F32))
    res = pl.pallas_call(body, name=name, grid=(n_rows // tm,), in_specs=specs, out_specs=out_specs,
                         out_shape=out_shape, compiler_params=_cp(("arbitrary",)))(*arrs)
    n1, n2 = len(d_rows), len(d_rows) + nt
    return list(res[:n1]), list(res[n1:n2]), list(res[n2:])


def _pick(n, prefs):
    for p in prefs:
        if n % p == 0:
            return p
    return n


def mm(name, a, b, out_dtype):
    k, n = b.shape
    m = (a.arr if isinstance(a, Cols) else a).shape[0]
    assert _width(a) == k
    tm = _pick(m, (MM_ROWS, 256))
    tn = _pick(n, (512, 256, 128))

    def body(a_ref, b_ref, o_ref):
        o_ref[...] = jnp.dot(a_ref[...], b_ref[...], preferred_element_type=F32).astype(o_ref.dtype)

    a_arr, a_spec = _row_in(a, tm)
    a_spec = pl.BlockSpec(a_spec.block_shape, lambda j, i, f=a_spec.index_map: f(i))
    return pl.pallas_call(
        body, name=name, grid=(n // tn, m // tm),
        in_specs=[a_spec, pl.BlockSpec((k, tn), lambda j, i: (0, j))],
        out_specs=pl.BlockSpec((tm, tn), lambda j, i: (i, j)),
        out_shape=jax.ShapeDtypeStruct((m, n), out_dtype),
        compiler_params=_cp(("parallel", "parallel")))(a_arr, b)


def mm_tn(name, a, b):
    t = (a.arr if isinstance(a, Cols) else a).shape[0]
    k, n = _width(a), _width(b)
    tt = _pick(t, (MM_ROWS, 256))
    tk = _pick(k, (512, 256, 128))
    tn = _pick(n, (1280, 1024, 512, 256, 128))

    def body(a_ref, b_ref, o_ref):
        part = lax.dot_general(a_ref[...], b_ref[...], _TN, preferred_element_type=F32)

        @pl.when(pl.program_id(2) == 0)
        def _():
            o_ref[...] = part

        @pl.when(pl.program_id(2) != 0)
        def _():
            o_ref[...] += part

    def win(item, width):
        if isinstance(item, Cols):
            assert item.off % width == 0
            return item.arr, item.off // width
        return item, 0

    a_arr, a0 = win(a, tk)
    b_arr, b0 = win(b, tn)
    return pl.pallas_call(
        body, name=name, grid=(k // tk, n // tn, t // tt),
        in_specs=[pl.BlockSpec((tt, tk), lambda ki, ni, ti: (ti, a0 + ki)),
                  pl.BlockSpec((tt, tn), lambda ki, ni, ti: (ti, b0 + ni))],
        out_specs=pl.BlockSpec((tk, tn), lambda ki, ni, ti: (ki, ni)),
        out_shape=jax.ShapeDtypeStruct((k, n), F32),
        compiler_params=_cp(("parallel", "parallel", "arbitrary")))(a_arr, b_arr)


def _row_of(col, rows):
    return jnp.broadcast_to(col, (rows, LANES)).T[0:1, :]


def _col_of(row, rows):
    return jnp.broadcast_to(row, (LANES, rows)).T[:, 0:1]


def attn_fwd(qs, kT, vv, m_ctx):
    _, nq, r, hd = qs.shape
    t = kT.shape[2]
    tq, tk = r // 4, ATTN_TK
    nqc, nkc, nk = m_ctx // tq, m_ctx // tk, t // tk

    def body(q_ref, kT_ref, v_ref, o_ref, lse_ref):
        i = pl.program_id(1)
        q = q_ref[...]

        def step(j, carry):
            mi, li, acc = carry
            off = pl.multiple_of(j * tk, tk)
            s = jnp.dot(q, kT_ref[:, pl.ds(off, tk)], preferred_element_type=F32)
            mn = jnp.maximum(mi, jnp.max(s, axis=1, keepdims=True))
            p = jnp.exp(s - mn)
            al = jnp.exp(mi - mn)
            li = al * li + jnp.sum(p, axis=1, keepdims=True)
            acc = al * acc + jnp.dot(p.astype(CDT), v_ref[pl.ds(off, tk), :], preferred_element_type=F32)
            return mn, li, acc

        init = (jnp.full((r, 1), -1e30, F32), jnp.zeros((r, 1), F32), jnp.zeros((r, hd), F32))
        mi, li, acc = lax.fori_loop(0, jnp.where(i < nqc, nkc, nk), step, init)
        o_ref[...] = (acc / li).astype(o_ref.dtype)
        lse_ref[...] = _row_of(mi + jnp.log(li), r)

    return pl.pallas_call(
        body, name="attn_fwd", grid=(2, nq),
        in_specs=[pl.BlockSpec((None, None, r, hd), lambda g, i: (g, i, 0, 0)),
                  pl.BlockSpec((None, hd, t), lambda g, i: (g, 0, 0)),
                  pl.BlockSpec((None, t, hd), lambda g, i: (g, 0, 0))],
        out_specs=[pl.BlockSpec((None, None, r, hd), lambda g, i: (g, i, 0, 0)),
                   pl.BlockSpec((None, None, 1, r), lambda g, i: (g, i, 0, 0))],
        out_shape=[jax.ShapeDtypeStruct(qs.shape, CDT), jax.ShapeDtypeStruct((2, nq, 1, r), F32)],
        compiler_params=_cp(("parallel", "arbitrary")))(qs, kT, vv)


def attn_dq(qs, kT, kk, vT, dos, os_, lse, m_ctx):
    _, nq, r, hd = qs.shape
    t = kT.shape[2]
    tq, tk = r // 4, ATTN_TK
    nqc, nkc, nk = m_ctx // tq, m_ctx // tk, t // tk

    def body(q_ref, kT_ref, k_ref, vT_ref, do_ref, o_ref, lse_ref, dq_ref, delta_ref):
        i = pl.program_id(1)
        q = q_ref[...]
        do = do_ref[...]
        lse_c = _col_of(lse_ref[...], r)
        delta = jnp.sum(do.astype(F32) * o_ref[...].astype(F32), axis=1, keepdims=True)

        def step(j, dq):
            off = pl.multiple_of(j * tk, tk)
            s = jnp.dot(q, kT_ref[:, pl.ds(off, tk)], preferred_element_type=F32)
            p = jnp.exp(s - lse_c)
            dp = jnp.dot(do, vT_ref[:, pl.ds(off, tk)], preferred_element_type=F32)
            ds = p * (dp - delta)
            return dq + jnp.dot(ds.astype(CDT), k_ref[pl.ds(off, tk), :], preferred_element_type=F32)

        dq_ref[...] = lax.fori_loop(0, jnp.where(i < nqc, nkc, nk), step, jnp.zeros((r, hd), F32))
        delta_ref[...] = _row_of(delta, r)

    blk = pl.BlockSpec((None, None, r, hd), lambda g, i: (g, i, 0, 0))
    row = pl.BlockSpec((None, None, 1, r), lambda g, i: (g, i, 0, 0))
    return pl.pallas_call(
        body, name="attn_dq", grid=(2, nq),
        in_specs=[blk, pl.BlockSpec((None, hd, t), lambda g, i: (g, 0, 0)),
                  pl.BlockSpec((None, t, hd), lambda g, i: (g, 0, 0)),
                  pl.BlockSpec((None, hd, t), lambda g, i: (g, 0, 0)), blk, blk, row],
        out_specs=[blk, row],
        out_shape=[jax.ShapeDtypeStruct(qs.shape, F32), jax.ShapeDtypeStruct((2, nq, 1, r), F32)],
        compiler_params=_cp(("parallel", "arbitrary")))(qs, kT, kk, vT, dos, os_, lse)


def attn_dkv(qs, qT, dos, doT, lse, delta, kk, vv, m_ctx):
    _, nq, r, hd = qs.shape
    t = kk.shape[1]
    tq, tk = r // 4, ATTN_TK
    nqc, nkc, nk = m_ctx // tq, m_ctx // tk, t // tk

    def body(q_ref, qT_ref, do_ref, doT_ref, lse_ref, delta_ref, k_ref, v_ref, dk_ref, dv_ref, dk_acc, dv_acc):
        j, i = pl.program_id(1), pl.program_id(2)

        @pl.when(i == 0)
        def _():
            dk_acc[...] = jnp.zeros_like(dk_acc)
            dv_acc[...] = jnp.zeros_like(dv_acc)

        @pl.when((j < nkc) | (i >= nqc))
        def _():
            st = jnp.dot(k_ref[...], qT_ref[...], preferred_element_type=F32)
            pt = jnp.exp(st - lse_ref[...])
            dv_acc[...] += jnp.dot(pt.astype(CDT), do_ref[...], preferred_element_type=F32)
            dpt = jnp.dot(v_ref[...], doT_ref[...], preferred_element_type=F32)
            dst = pt * (dpt - delta_ref[...])
            dk_acc[...] += jnp.dot(dst.astype(CDT), q_ref[...], preferred_element_type=F32)

        @pl.when(i == nq - 1)
        def _():
            dk_ref[...] = dk_acc[...]
            dv_ref[...] = dv_acc[...]

    blk = pl.BlockSpec((None, None, r, hd), lambda g, j, i: (g, i, 0, 0))
    blk_t = pl.BlockSpec((None, None, hd, r), lambda g, j, i: (g, i, 0, 0))
    row = pl.BlockSpec((None, None, 1, r), lambda g, j, i: (g, i, 0, 0))
    kv = pl.BlockSpec((None, tk, hd), lambda g, j, i: (g, j, 0))
    return pl.pallas_call(
        body, name="attn_dkv", grid=(2, nk, nq),
        in_specs=[blk, blk_t, blk, blk_t, row, row, kv, kv],
        out_specs=[kv, kv],
        out_shape=[jax.ShapeDtypeStruct(kk.shape, F32), jax.ShapeDtypeStruct(kk.shape, F32)],
        scratch_shapes=[pltpu.VMEM((tk, hd), F32), pltpu.VMEM((tk, hd), F32)],
        compiler_params=_cp(("parallel", "parallel", "arbitrary")))(qs, qT, dos, doT, lse, delta, kk, vv)


def _stack_heads(a, tq):
    t = a.shape[0]
    return a.reshape(t // tq, tq, 2, 4, ATTN_HD).transpose(2, 0, 3, 1, 4).reshape(2, t // tq, 4 * tq, ATTN_HD)


def _unstack_heads(a):
    _, nq, r, hd = a.shape
    tq = r // 4
    return a.reshape(2, nq, 4, tq, hd).transpose(1, 3, 0, 2, 4).reshape(nq * tq, 8 * hd)


def _split_kv(a):
    return a.reshape(a.shape[0], 2, ATTN_HD).transpose(1, 0, 2)


def _merge_kv(a):
    return a.transpose(1, 0, 2).reshape(a.shape[1], 2 * ATTN_HD)


def _chunk_order(rev, ncc, nct):
    if not rev:
        return lambda s: s
    return lambda s: jnp.where(s < ncc, ncc - 1 - s, nct - 1 - (s - ncc))


def scan_fwd(name, fn, rows, shared, n_state, y_width, n_rows, m_ctx, rev):
    nct, ncc = n_rows // CHUNK, m_ctx // CHUNK
    order = _chunk_order(rev, ncc, nct)
    nr, ns = len(rows), len(shared)

    def body(*refs):
        rvals = [r[...] for r in refs[:nr]]
        svals = [r[...] for r in refs[nr:nr + ns]]
        y_ref, sin_ref, st = refs[nr + ns:]

        @pl.when(pl.program_id(0) == 0)
        def _():
            st[...] = jnp.zeros_like(st)

        prev = [st[k] for k in range(n_state)]
        sin_ref[...] = st[...]
        y, new = fn(rvals, svals, prev)
        y_ref[...] = y
        for k in range(n_state):
            st[k] = new[k]

    arrs, specs = [], []
    for it in rows:
        a, s = _row_in(it, CHUNK, order)
        arrs.append(a)
        specs.append(s)
    for s_ in shared:
        arrs.append(s_)
        specs.append(_const_spec(s_.shape))
    return pl.pallas_call(
        body, name=name, grid=(nct,), in_specs=specs,
        out_specs=[pl.BlockSpec((CHUNK, y_width), lambda s: (order(s), 0)),
                   pl.BlockSpec((None, n_state, LANES, LANES), lambda s: (order(s), 0, 0, 0))],
        out_shape=[jax.ShapeDtypeStruct((n_rows, y_width), F32),
                   jax.ShapeDtypeStruct((nct, n_state, LANES, LANES), F32)],
        scratch_shapes=[pltpu.VMEM((n_state, LANES, LANES), F32)],
        compiler_params=_cp(("arbitrary",)))(*arrs)


def scan_bwd(name, fn, rows, shared, states_in, dy, post, outs, addends, n_state, n_rows, m_ctx, rev):
    nct, ncc = n_rows // CHUNK, m_ctx // CHUNK
    fwd_order = _chunk_order(rev, ncc, nct)
    order = lambda r: fwd_order(nct - 1 - r)
    nr, ns, na = len(rows), len(shared), len(addends)
    nin = nr + ns

    def body(*refs):
        rvals = [r[...] for r in refs[:nr]]
        svals = [r[...] for r in refs[nr:nin]]
        sin_ref, dy_ref = refs[nin], refs[nin + 1]
        add_refs = refs[nin + 2:nin + 2 + na]
        out_refs = refs[nin + 2 + na:nin + 2 + na + len(outs)]
        dsh_refs = refs[nin + 2 + na + len(outs):-1]
        dst = refs[-1]
        r = pl.program_id(0)

        @pl.when(r == 0)
        def _():
            dst[...] = jnp.zeros_like(dst)

        prev = [sin_ref[k] for k in range(n_state)]
        _, vjp = jax.vjp(fn, rvals, svals, prev)
        d_rows, d_shared, d_prev = vjp((dy_ref[...], [dst[k] for k in range(n_state)]))
        res = post(d_rows)
        for k, (ref, val) in enumerate(zip(out_refs, res)):
            if k < na:
                val = val + add_refs[k][...]
            ref[...] = val.astype(ref.dtype)
        for ref, gr in zip(dsh_refs, d_shared):
            @pl.when(r == 0)
            def _(ref=ref, gr=gr):
                ref[...] = gr

            @pl.when(r != 0)
            def _(ref=ref, gr=gr):
                ref[...] += gr
        for k in range(n_state):
            dst[k] = d_prev[k]

    arrs, specs = [], []
    for it in rows:
        a, s = _row_in(it, CHUNK, order)
        arrs.append(a)
        specs.append(s)
    for s_ in shared:
        arrs.append(s_)
        specs.append(_const_spec(s_.shape))
    arrs.append(states_in)
    specs.append(pl.BlockSpec((None, n_state, LANES, LANES), lambda r: (order(r), 0, 0, 0)))
    for it in [dy] + list(addends):
        a, s = _row_in(it, CHUNK, order)
        arrs.append(a)
        specs.append(s)
    out_specs = [pl.BlockSpec((CHUNK, w), lambda r: (order(r), 0)) for w, _ in outs]
    out_shape = [jax.ShapeDtypeStruct((n_rows, w), dt) for w, dt in outs]
    for s_ in shared:
        out_specs.append(_const_spec(s_.shape))
        out_shape.append(jax.ShapeDtypeStruct(s_.shape, F32))
    res = pl.pallas_call(body, name=name, grid=(nct,), in_specs=specs, out_specs=out_specs, out_shape=out_shape,
                         scratch_shapes=[pltpu.VMEM((n_state, LANES, LANES), F32)],
                         compiler_params=_cp(("arbitrary",)))(*arrs)
    return list(res[:len(outs)]), list(res[len(outs):])


def _make_ssd_chunk(direction):
    rev = direction == 1
    base = 8 * direction

    def fn(rows, shared, prev):
        xs, bms, cms, dtraw = rows[0:4], rows[4:6], rows[6:8], rows[8]
        dt_bias, a_log = shared
        ln = dtraw.shape[0]
        dt_all = _softplus(dtraw + dt_bias)
        a_all = dt_all * (-jnp.exp(a_log))
        r_i = lax.broadcasted_iota(jnp.int32, (ln, ln), 0)
        c_i = lax.broadcasted_iota(jnp.int32, (ln, ln), 1)
        tri = (r_i <= c_i) if rev else (r_i >= c_i)
        a_cum_all = jnp.dot(tri.astype(F32), a_all, precision=lax.Precision.HIGHEST, preferred_element_type=F32)
        a_tot_all = jnp.sum(a_all, axis=0, keepdims=True)
        first = lax.broadcasted_iota(jnp.int32, (ln, LANES), 1) < SSD_HD
        first_row = lax.broadcasted_iota(jnp.int32, (LANES, 1), 0) < SSD_HD

        def lmat(acol):
            a_b = jnp.broadcast_to(acol, (ln, ln))
            seg = a_b - a_b.T
            return jnp.where(tri, jnp.exp(jnp.where(tri, seg, 0.0)), 0.0)

        ys, new = [], []
        for g in range(2):
            bm, cm = bms[g], cms[g]
            cb = _mxu(cm, bm, _NT)
            for jj in range(2):
                pr = 2 * g + jj
                h0, h1 = base + 2 * pr, base + 2 * pr + 1
                ac0, ac1 = _col(a_cum_all, h0), _col(a_cum_all, h1)
                at0, at1 = _col(a_tot_all, h0), _col(a_tot_all, h1)
                dt_pair = jnp.where(first, _col(dt_all, h0), _col(dt_all, h1))
                acum_pair = jnp.where(first, ac0, ac1)
                atot_pair = jnp.where(first[0:1], at0, at1)
                xd = xs[pr] * dt_pair
                st = _mxu(xd * jnp.exp(atot_pair - acum_pair), bm, _TN)
                new.append(prev[pr] * jnp.where(first_row, jnp.exp(at0), jnp.exp(at1)) + st)
                y0 = _mxu(cb * lmat(ac0), xd)
                y1 = _mxu(cb * lmat(ac1), xd)
                y_off = _mxu(cm, prev[pr], _NT) * jnp.exp(acum_pair)
                ys.append(jnp.where(first, y0, y1) + y_off)
        return jnp.concatenate(ys, axis=1), new

    return fn


def _make_ret_chunk(direction):
    rev = direction == 1
    base = 4 * direction

    def fn(rows, shared, prev):
        qs, ks, vs = rows[0:4], rows[4:8], rows[8:12]
        lg_all = -jnp.exp(shared[0])
        ln = qs[0].shape[0]
        pos = lax.broadcasted_iota(jnp.int32, (ln, 1), 0).astype(F32)
        r_i = lax.broadcasted_iota(jnp.int32, (ln, ln), 0)
        c_i = lax.broadcasted_iota(jnp.int32, (ln, ln), 1)
        diff = ((c_i - r_i) if rev else (r_i - c_i))
        mask = diff >= 0
        dpos = jnp.maximum(diff, 0).astype(F32)
        k_pow = pos if rev else (ln - 1.0 - pos)
        q_pow = (ln - pos) if rev else (pos + 1.0)
        ys, new = [], []
        for h in range(RET_HEADS):
            lg = _col(lg_all, base + h)
            dmat = jnp.where(mask, jnp.exp(dpos * lg), 0.0)
            st = _mxu(ks[h] * jnp.exp(k_pow * lg), vs[h], _TN)
            new.append(prev[h] * jnp.exp(ln * lg) + st)
            s = _mxu(qs[h], ks[h], _NT) * dmat
            ys.append(_mxu(s, vs[h]) + _mxu(qs[h], prev[h]) * jnp.exp(q_pow * lg))
        return jnp.concatenate(ys, axis=1), new

    return fn


def _conv_pre(x, w, b, t_idx, n_rows, m_ctx):
    is_start = (t_idx == 0) | (t_idx == m_ctx)
    is_end = (t_idx == m_ctx - 1) | (t_idx == n_rows - 1)
    xp = jnp.where(is_start, 0.0, pltpu.roll(x, 1, axis=0))
    xn = jnp.where(is_end, 0.0, pltpu.roll(x, n_rows - 1, axis=0))
    return w[0:1] * xp + w[1:2] * x + w[2:3] * xn + b, xp, xn, is_start, is_end


def conv_fwd(x, conv_w, conv_b, m_ctx):
    n_rows, width = x.arr.shape[0], x.width
    c0 = x.off // LANES

    def body(x_ref, w_ref, b_ref, o_ref):
        t_idx = lax.broadcasted_iota(jnp.int32, (n_rows, 1), 0)
        pre = _conv_pre(x_ref[...], w_ref[...], b_ref[...], t_idx, n_rows, m_ctx)[0]
        o_ref[...] = pre * jax.nn.sigmoid(pre)

    return pl.pallas_call(
        body, name="conv_fwd", grid=(width // LANES,),
        in_specs=[pl.BlockSpec((n_rows, LANES), lambda c: (0, c0 + c)),
                  pl.BlockSpec((3, LANES), lambda c: (0, c)), pl.BlockSpec((1, LANES), lambda c: (0, c))],
        out_specs=pl.BlockSpec((n_rows, LANES), lambda c: (0, c)),
        out_shape=jax.ShapeDtypeStruct((n_rows, width), F32),
        compiler_params=_cp(("parallel",)))(x.arr, conv_w, conv_b)


def conv_bwd(x, conv_w, conv_b, dy, dxs_extra, m_ctx):
    n_rows, width = x.arr.shape[0], x.width
    c0 = x.off // LANES
    n_extra = dxs_extra.shape[1] // LANES

    def body(x_ref, w_ref, b_ref, dy_ref, ex_ref, dx_ref, dw_ref, db_ref):
        c = pl.program_id(0)
        t_idx = lax.broadcasted_iota(jnp.int32, (n_rows, 1), 0)
        w = w_ref[...]
        pre, xp, xn, is_start, is_end = _conv_pre(x_ref[...], w, b_ref[...], t_idx, n_rows, m_ctx)
        sg = jax.nn.sigmoid(pre)
        dyv = dy_ref[...] + jnp.where(c < n_extra, ex_ref[...], 0.0)
        dpre = dyv * (sg * (1.0 + pre * (1.0 - sg)))
        d_next = jnp.where(is_end, 0.0, pltpu.roll(dpre, n_rows - 1, axis=0))
        d_prev = jnp.where(is_start, 0.0, pltpu.roll(dpre, 1, axis=0))
        dx_ref[...] = (w[1:2] * dpre + w[0:1] * d_next + w[2:3] * d_prev).astype(dx_ref.dtype)
        dw_ref[...] = jnp.concatenate([jnp.sum(dpre * xp, axis=0, keepdims=True),
                                       jnp.sum(dpre * x_ref[...], axis=0, keepdims=True),
                                       jnp.sum(dpre * xn, axis=0, keepdims=True)], axis=0)
        db_ref[...] = jnp.sum(dpre, axis=0, keepdims=True)

    return pl.pallas_call(
        body, name="conv_bwd", grid=(width // LANES,),
        in_specs=[pl.BlockSpec((n_rows, LANES), lambda c: (0, c0 + c)),
                  pl.BlockSpec((3, LANES), lambda c: (0, c)), pl.BlockSpec((1, LANES), lambda c: (0, c)),
                  pl.BlockSpec((n_rows, LANES), lambda c: (0, c)),
                  pl.BlockSpec((n_rows, LANES), lambda c: (0, jnp.minimum(c, n_extra - 1)))],
        out_specs=[pl.BlockSpec((n_rows, LANES), lambda c: (0, c)),
                   pl.BlockSpec((3, LANES), lambda c: (0, c)), pl.BlockSpec((1, LANES), lambda c: (0, c))],
        out_shape=[jax.ShapeDtypeStruct((n_rows, width), CDT), jax.ShapeDtypeStruct((3, width), F32),
                   jax.ShapeDtypeStruct((1, width), F32)],
        compiler_params=_cp(("parallel",)))(x.arr, conv_w, conv_b, dy, dxs_extra)


def loss_head(h, target, final_w, m_ctx):
    n_rows, d = h.shape
    tm = min(ROW_TILE, n_rows)
    nb_ctx = m_ctx // tm

    def f(hb, w, tgt):
        err = _rms(hb, w) - tgt
        return 0.5 * jnp.sum(jnp.mean(err * err, axis=-1))

    def body(h_ref, t_ref, w_ref, loss_ref, dh_ref, dw_ref):
        i = pl.program_id(0)

        @pl.when(i < nb_ctx)
        def _():
            dh_ref[...] = jnp.zeros_like(dh_ref)

        @pl.when(i == 0)
        def _():
            loss_ref[...] = jnp.zeros_like(loss_ref)
            dw_ref[...] = jnp.zeros_like(dw_ref)

        @pl.when(i >= nb_ctx)
        def _():
            val, vjp = jax.vjp(lambda hb, w: f(hb, w, t_ref[...]), h_ref[...], w_ref[...])
            dh, dw = vjp(jnp.ones((), F32))
            dh_ref[...] = dh
            dw_ref[...] += dw
            loss_ref[...] += jnp.broadcast_to(val, loss_ref.shape)

    return pl.pallas_call(
        body, name="loss_head", grid=(n_rows // tm,),
        in_specs=[pl.BlockSpec((tm, d), lambda i: (i, 0)),
                  pl.BlockSpec((tm, d), lambda i: (jnp.maximum(i - nb_ctx, 0), 0)), _const_spec((1, d))],
        out_specs=[_const_spec((1, LANES)), pl.BlockSpec((tm, d), lambda i: (i, 0)), _const_spec((1, d))],
        out_shape=[jax.ShapeDtypeStruct((1, LANES), F32), jax.ShapeDtypeStruct((n_rows, d), F32),
                   jax.ShapeDtypeStruct((1, d), F32)],
        compiler_params=_cp(("arbitrary",)))(h, target, final_w)


def adamw(name, w, m, v, g_parts):
    rows, cols = w.shape
    tr = _pick(rows, (256, 128, 64, 32, 16, 8))
    npart = len(g_parts)
    c1 = 1.0 - ADAM_B1 ** ADAM_STEP
    c2 = 1.0 - ADAM_B2 ** ADAM_STEP

    def body(*refs):
        w_ref, m_ref, v_ref = refs[:3]
        g = refs[3][...].astype(F32)
        for r in refs[4:3 + npart]:
            g = g + r[...].astype(F32)
        g_ref, d_ref, nm_ref, nv_ref = refs[3 + npart:]
        nm = ADAM_B1 * m_ref[...] + (1.0 - ADAM_B1) * g
        nv = ADAM_B2 * v_ref[...] + (1.0 - ADAM_B2) * (g * g)
        g_ref[...] = g
        nm_ref[...] = nm
        nv_ref[...] = nv
        d_ref[...] = -ADAM_LR * ((nm / c1) / (jnp.sqrt(nv / c2) + ADAM_EPS) + ADAM_WD * w_ref[...])

    spec = pl.BlockSpec((tr, cols), lambda i: (i, 0))
    return pl.pallas_call(
        body, name=name, grid=(rows // tr,), in_specs=[spec] * (3 + npart), out_specs=[spec] * 4,
        out_shape=[jax.ShapeDtypeStruct((rows, cols), F32)] * 4, compiler_params=_cp(("parallel",)))(w, m, v, *g_parts)


def sum_parts(name, parts):
    npart, rows, cols = parts.shape
    tr = _pick(rows, (512, 256, 8))

    def body(p_ref, o_ref):
        acc = p_ref[0].astype(F32)
        for k in range(1, npart):
            acc = acc + p_ref[k].astype(F32)
        o_ref[...] = acc

    return pl.pallas_call(
        body, name=name, grid=(rows // tr,), in_specs=[pl.BlockSpec((npart, tr, cols), lambda i: (0, i, 0))],
        out_specs=pl.BlockSpec((tr, cols), lambda i: (i, 0)), out_shape=jax.ShapeDtypeStruct((rows, cols), F32),
        compiler_params=_cp(("parallel",)))(parts)


MESH = pl.DeviceIdType.MESH
_HBM = pl.BlockSpec(memory_space=pl.ANY)


def _chip_peers():
    x, y, c = lax.axis_index("x"), lax.axis_index("y"), lax.axis_index("c")
    return x, y, c, [(1 - x, y), (x, 1 - y), (1 - x, 1 - y)]


def gather_chips(name, shard):
    def body(x_ref, out_ref, send_sems, recv_sems, local_sem):
        x, y, c, peers = _chip_peers()
        me = 2 * x + y
        mine = pltpu.make_async_copy(x_ref, out_ref.at[me], local_sem)
        mine.start()
        sends = []
        for k, (px, py) in enumerate(peers):
            cp = pltpu.make_async_remote_copy(src_ref=x_ref, dst_ref=out_ref.at[me], send_sem=send_sems.at[k],
                                              recv_sem=recv_sems.at[k], device_id=(px, py, c), device_id_type=MESH)
            cp.start()
            sends.append(cp)
        for k, (px, py) in enumerate(peers):
            pltpu.make_async_remote_copy(src_ref=x_ref, dst_ref=out_ref.at[2 * px + py], send_sem=send_sems.at[k],
                                         recv_sem=recv_sems.at[k], device_id=(px, py, c),
                                         device_id_type=MESH).wait_recv()
        for cp in sends:
            cp.wait_send()
        mine.wait()

    return pl.pallas_call(
        body, name=name, in_specs=[_HBM], out_specs=_HBM,
        out_shape=jax.ShapeDtypeStruct((4,) + shard.shape, shard.dtype),
        scratch_shapes=[pltpu.SemaphoreType.DMA((3,)), pltpu.SemaphoreType.DMA((3,)), pltpu.SemaphoreType.DMA],
        )(shard)


def scatter_chips(name, pieces):
    def body(p_ref, out_ref, send_sems, recv_sems, local_sem):
        x, y, c, peers = _chip_peers()
        me = 2 * x + y
        mine = pltpu.make_async_copy(p_ref.at[me], out_ref.at[me], local_sem)
        mine.start()
        sends = []
        for k, (px, py) in enumerate(peers):
            cp = pltpu.make_async_remote_copy(src_ref=p_ref.at[2 * px + py], dst_ref=out_ref.at[me],
                                              send_sem=send_sems.at[k], recv_sem=recv_sems.at[k],
                                              device_id=(px, py, c), device_id_type=MESH)
            cp.start()
            sends.append(cp)
        for k, (px, py) in enumerate(peers):
            pltpu.make_async_remote_copy(src_ref=p_ref.at[me], dst_ref=out_ref.at[2 * px + py],
                                         send_sem=send_sems.at[k], recv_sem=recv_sems.at[k], device_id=(px, py, c),
                                         device_id_type=MESH).wait_recv()
        for cp in sends:
            cp.wait_send()
        mine.wait()

    return pl.pallas_call(
        body, name=name, in_specs=[_HBM], out_specs=_HBM, out_shape=jax.ShapeDtypeStruct(pieces.shape, pieces.dtype),
        scratch_shapes=[pltpu.SemaphoreType.DMA((3,)), pltpu.SemaphoreType.DMA((3,)), pltpu.SemaphoreType.DMA],
        )(pieces)


def swap_cores(name, part):
    def body(p_ref, out_ref, send_sem, recv_sem, local_sem):
        x, y, c = lax.axis_index("x"), lax.axis_index("y"), lax.axis_index("c")
        mine = pltpu.make_async_copy(p_ref, out_ref.at[c], local_sem)
        mine.start()
        cp = pltpu.make_async_remote_copy(src_ref=p_ref, dst_ref=out_ref.at[c], send_sem=send_sem, recv_sem=recv_sem,
                                          device_id=(x, y, 1 - c), device_id_type=MESH)
        cp.start()
        pltpu.make_async_remote_copy(src_ref=p_ref, dst_ref=out_ref.at[1 - c], send_sem=send_sem, recv_sem=recv_sem,
                                     device_id=(x, y, 1 - c), device_id_type=MESH).wait_recv()
        cp.wait_send()
        mine.wait()

    return pl.pallas_call(
        body, name=name, in_specs=[_HBM], out_specs=_HBM, out_shape=jax.ShapeDtypeStruct((2,) + part.shape, part.dtype),
        scratch_shapes=[pltpu.SemaphoreType.DMA, pltpu.SemaphoreType.DMA, pltpu.SemaphoreType.DMA],
        )(part)


def allreduce_small(name, buf):
    rows = buf.shape[0]

    def body(x_ref, out_ref, gath, send_sems, recv_sems):
        x, y, c = lax.axis_index("x"), lax.axis_index("y"), lax.axis_index("c")
        me = 4 * x + 2 * y + c
        masks = [(k >> 2 & 1, k >> 1 & 1, k & 1) for k in range(1, 8)]

        def flip(v, bit):
            return 1 - v if bit else v

        sends = []
        for k, (bx, by, bc) in enumerate(masks):
            cp = pltpu.make_async_remote_copy(src_ref=x_ref, dst_ref=gath.at[me], send_sem=send_sems.at[k],
                                              recv_sem=recv_sems.at[k],
                                              device_id=(flip(x, bx), flip(y, by), flip(c, bc)), device_id_type=MESH)
            cp.start()
            sends.append(cp)
        gath[me] = x_ref[...]
        for k, (bx, by, bc) in enumerate(masks):
            px, py, pc = flip(x, bx), flip(y, by), flip(c, bc)
            pltpu.make_async_remote_copy(src_ref=x_ref, dst_ref=gath.at[4 * px + 2 * py + pc],
                                         send_sem=send_sems.at[k], recv_sem=recv_sems.at[k],
                                         device_id=(px, py, pc), device_id_type=MESH).wait_recv()
        for cp in sends:
            cp.wait_send()
        acc = gath[0]
        for d in range(1, 8):
            acc = acc + gath[d]
        out_ref[...] = acc

    return pl.pallas_call(
        body, name=name, in_specs=[pl.BlockSpec(memory_space=pltpu.VMEM)],
        out_specs=pl.BlockSpec(memory_space=pltpu.VMEM), out_shape=jax.ShapeDtypeStruct(buf.shape, F32),
        scratch_shapes=[pltpu.VMEM((8, rows, LANES), F32), pltpu.SemaphoreType.DMA((7,)),
                        pltpu.SemaphoreType.DMA((7,))],
        )(buf)


def _pack_flat(arrs, dtype, width, row_mult=8):
    flat = jnp.concatenate([a.reshape(-1).astype(dtype) for a in arrs])
    pad = (-flat.shape[0]) % (row_mult * width)
    if pad:
        flat = jnp.concatenate([flat, jnp.zeros((pad,), dtype)])
    return flat.reshape(-1, width)


def _unpack_flat(buf, shapes):
    flat = buf.reshape(-1)
    out, off = [], 0
    for s in shapes:
        n = math.prod(s)
        out.append(flat[off:off + n].reshape(s))
        off += n
    return out


def _in_to_padded(w):
    parts = []
    for name in IN_NEW_ORDER:
        _, width, o_off, o_w = IN_LAYOUT[name]
        parts.append(w[..., o_off:o_off + o_w])
        if o_w < width:
            parts.append(jnp.zeros(w.shape[:-1] + (width - o_w,), w.dtype))
    used = sum(IN_LAYOUT[n][1] for n in IN_NEW_ORDER)
    parts.append(jnp.zeros(w.shape[:-1] + (IN_PAD - used,), w.dtype))
    return jnp.concatenate(parts, axis=-1)


def _in_from_padded(g):
    parts = []
    for name in IN_ORIG_ORDER:
        off, _, _, o_w = IN_LAYOUT[name]
        parts.append(g[..., off:off + o_w])
    return jnp.concatenate(parts, axis=-1)


def _pcol(p, name):
    off, width, _, _ = IN_LAYOUT[name]
    return Cols(p, off, width)


def _lane_pad(v, width=LANES):
    v = v.reshape(-1)
    return jnp.concatenate([v, jnp.zeros((width - v.shape[0],), v.dtype)]).reshape(1, width)


def _f_norm_mod(h, sh, sc, w):
    return (_rms(h, w) * (1.0 + sc) + sh,)


def _f_norm_mod_thru(h, sh, sc, w):
    return h, _rms(h, w) * (1.0 + sc) + sh


def _f_attn_prep(qraw, kraw, vraw, cq, sq, ck, sk, qw, kw, gq, gk):
    q = qraw * lax.rsqrt(_group_mean(qraw * qraw, gq) + NORM_EPS) * qw
    q = _rope32(q, cq, sq) * (ATTN_HD ** -0.5)
    k = kraw * lax.rsqrt(_group_mean(kraw * kraw, gk) + NORM_EPS) * kw
    return q, _rope32(k, ck, sk), vraw


def _f_ssd_finish(yf, yb, xs, z, d_exp, nw):
    y = (yf + yb + d_exp * xs) * (z * jax.nn.sigmoid(z))
    return (_rms(y, nw),)


def _f_ret_prep(rq, rk, cos_full, sin_signed):
    return _rope64(rq, cos_full, sin_signed), _rope64(rk, cos_full, sin_signed) * (RET_DK ** -0.5)


def _f_ret_finish(yf, yb, g, gw):
    y = yf + yb
    outs = []
    for h in range(RET_HEADS):
        yh = y[:, h * RET_DK:(h + 1) * RET_DK]
        yc = yh - jnp.mean(yh, axis=-1, keepdims=True)
        outs.append(yc * lax.rsqrt(jnp.mean(yc * yc, axis=-1, keepdims=True) + NORM_EPS))
    return (jnp.concatenate(outs, axis=1) * gw * (g * jax.nn.sigmoid(g)),)


def _f_merge(p0, p1, p2, g0, g1, g2):
    return (jax.nn.sigmoid(g0) * p0 + jax.nn.sigmoid(g1) * p1 + jax.nn.sigmoid(g2) * p2,)


def _f_mid(h, mix, g1, sh2, sc2, w2):
    h_mid = h + g1 * mix
    return h_mid, _rms(h_mid, w2) * (1.0 + sc2) + sh2


def _f_sqrelu(a):
    r = jnp.maximum(a, 0.0)
    return (r * r,)


def _f_residual(h_mid, o, g2):
    return (h_mid + g2 * o,)


def _f_silu(x):
    return (x * jax.nn.sigmoid(x),)


def _f_bias(x, b):
    return (x + b,)


def _ssd_rows(xbc, p):
    rows = [Cols(xbc, LANES * k, LANES) for k in range(4)]
    rows += [Cols(xbc, 512 + LANES * g, LANES) for g in range(2)]
    rows += [Cols(xbc, 768 + LANES * g, LANES) for g in range(2)]
    return rows + [_pcol(p, "dt")]


def _ret_rows(rq, rk, p):
    off_v = IN_LAYOUT["rv"][0]
    return ([Cols(rq, LANES * h, LANES) for h in range(4)] + [Cols(rk, LANES * h, LANES) for h in range(4)]
            + [Cols(p, off_v + LANES * h, LANES) for h in range(4)])


def layer_fwd(li, h, mod, lw, tabs, m_ctx):
    t = h.shape[0]
    nb = m_ctx // min(ROW_TILE, t)
    sh1, sc1, g1, sh2, sc2, g2 = mod
    nm = lambda s: f"l{li}_{s}"
    sv = {}
    (u,) = rowwise_fwd(nm("norm1"), _f_norm_mod, [h], [sh1, sc1], [lw["norm1_w"]], [(D_MODEL, CDT)], t, nb)
    p = mm(nm("in_proj"), u, lw["w_in"], F32)
    q, k, v = rowwise_fwd(
        nm("attn_prep"), _f_attn_prep,
        [_pcol(p, "q"), _pcol(p, "k"), _pcol(p, "v"), tabs["cq"], tabs["sq"], tabs["ck"], tabs["sk"]], [],
        [lw["qw"], lw["kw"], tabs["gq"], tabs["gk"]], [(512, CDT), (128, CDT), (128, CDT)], t, nb)
    tq = min(ATTN_TQ, m_ctx)
    qs, kk, vv = _stack_heads(q, tq), _split_kv(k), _split_kv(v)
    kT = kk.transpose(0, 2, 1)
    o_s, lse = attn_fwd(qs, kT, vv, m_ctx)
    attn_o = _unstack_heads(o_s)

    xbc = conv_fwd(_pcol(p, "xbc"), lw["conv_w"], lw["conv_b"], m_ctx)
    ssd_sh = [lw["dt_bias"], lw["a_log"]]
    yf, sf = scan_fwd(nm("ssd_f"), _make_ssd_chunk(0), _ssd_rows(xbc, p), ssd_sh, 4, 512, t, m_ctx, False)
    yb, sb = scan_fwd(nm("ssd_b"), _make_ssd_chunk(1), _ssd_rows(xbc, p), ssd_sh, 4, 512, t, m_ctx, True)
    (ssd_o,) = rowwise_fwd(nm("ssd_fin"), _f_ssd_finish, [yf, yb, Cols(xbc, 0, 512), _pcol(p, "z")], [],
                           [lw["d_exp"], lw["ssd_nw"]], [(512, CDT)], t, nb)

    rq, rk = rowwise_fwd(nm("ret_prep"), _f_ret_prep, [_pcol(p, "rq"), _pcol(p, "rk"), tabs["rc"], tabs["rs"]],
                         [], [], [(512, F32), (512, F32)], t, nb)
    rf, rsf = scan_fwd(nm("ret_f"), _make_ret_chunk(0), _ret_rows(rq, rk, p), [lw["ret_lg"]], 4, 512, t, m_ctx, False)
    rb, rsb = scan_fwd(nm("ret_b"), _make_ret_chunk(1), _ret_rows(rq, rk, p), [lw["ret_lg"]], 4, 512, t, m_ctx, True)
    (ret_o,) = rowwise_fwd(nm("ret_fin"), _f_ret_finish, [rf, rb, _pcol(p, "rg")], [], [lw["ret_gw"]],
                           [(512, CDT)], t, nb)

    pbs = [mm(nm(f"branch{b}"), br, lw["w_branch"][b], F32) for b, br in enumerate((attn_o, ssd_o, ret_o))]
    gl = [Cols(p, 1024 * b, 1024) for b in range(3)]
    (merged,) = rowwise_fwd(nm("merge"), _f_merge, pbs + gl, [], [], [(D_MODEL, CDT)], t, nb)
    mix = mm(nm("out_proj"), merged, lw["w_out"], F32)
    h_mid, vv2 = rowwise_fwd(nm("mid"), _f_mid, [h, mix], [g1, sh2, sc2], [lw["norm2_w"]],
                             [(D_MODEL, F32), (D_MODEL, CDT)], t, nb)
    a = mm(nm("mlp1"), vv2, lw["w_mlp1"], F32)
    (hh,) = rowwise_fwd(nm("sqrelu"), _f_sqrelu, [a], [], [], [(a.shape[1], CDT)], t, nb)
    o = mm(nm("mlp2"), hh, lw["w_mlp2"], F32)
    (h_out,) = rowwise_fwd(nm("resid"), _f_residual, [h_mid, o], [g2], [], [(D_MODEL, F32)], t, nb)
    sv.update(h=h, u=u, p=p, qs=qs, kk=kk, vv=vv, kT=kT, o_s=o_s, lse=lse, attn_o=attn_o, xbc=xbc, yf=yf, yb=yb,
              sf=sf, sb=sb, ssd_o=ssd_o, rq=rq, rk=rk, rf=rf, rb=rb, rsf=rsf, rsb=rsb, ret_o=ret_o, pbs=pbs,
              merged=merged, mix=mix, h_mid=h_mid, v=vv2, a=a, hh=hh, o=o)
    return h_out, sv


def layer_bwd(li, dh_out, sv, mod, lw, lwt, tabs, m_ctx):
    t = dh_out.shape[0]
    nb = m_ctx // min(ROW_TILE, t)
    sh1, sc1, g1, sh2, sc2, g2 = mod
    nm = lambda s: f"l{li}_{s}_bwd"
    gw = {}
    p = sv["p"]
    (do,), (dg2,), _ = rowwise_bwd(nm("resid"), _f_residual, [sv["h_mid"], sv["o"]], [g2], [], [dh_out],
                                   [False, True], [], [CDT], t, nb)
    dhh = mm(nm("mlp2_dx"), do, lwt["w_mlp2"], F32)
    gw["w_mlp2"] = mm_tn(nm("mlp2_dw"), sv["hh"], do)
    (da,), _, _ = rowwise_bwd(nm("sqrelu"), _f_sqrelu, [sv["a"]], [], [], [dhh], [True], [], [CDT], t, nb)
    dv = mm(nm("mlp1_dx"), da, lwt["w_mlp1"], F32)
    gw["w_mlp1"] = mm_tn(nm("mlp1_dw"), sv["v"], da)
    (dh_a, dmix), (dg1, dsh2, dsc2), (gw["norm2_w"],) = rowwise_bwd(
        nm("mid"), _f_mid, [sv["h"], sv["mix"]], [g1, sh2, sc2], [lw["norm2_w"]], [dh_out, dv],
        [True, True], [True], [F32, CDT], t, nb)
    dmerged = mm(nm("out_dx"), dmix, lwt["w_out"], F32)
    gw["w_out"] = mm_tn(nm("out_dw"), sv["merged"], dmix)
    gl = [Cols(p, 1024 * b, 1024) for b in range(3)]
    dmg, _, _ = rowwise_bwd(nm("merge"), _f_merge, sv["pbs"] + gl, [], [], [dmerged], [True] * 6, [], [CDT] * 6,
                            t, nb)
    dpb, dgl = dmg[:3], dmg[3:]
    brs = (sv["attn_o"], sv["ssd_o"], sv["ret_o"])
    d_attn_o = mm(nm("branch0_dx"), dpb[0], lwt["w_branch"][0], CDT)
    d_ssd_o = mm(nm("branch1_dx"), dpb[1], lwt["w_branch"][1], F32)
    d_ret_o = mm(nm("branch2_dx"), dpb[2], lwt["w_branch"][2], F32)
    gw["w_branch"] = jnp.stack([mm_tn(nm(f"branch{b}_dw"), brs[b], dpb[b]) for b in range(3)])
    tq = min(ATTN_TQ, m_ctx)
    dos = _stack_heads(d_attn_o, tq)
    dq_s, delta = attn_dq(sv["qs"], sv["kT"], sv["kk"], sv["vv"].transpose(0, 2, 1), dos, sv["o_s"], sv["lse"], m_ctx)
    dk_s, dv_s = attn_dkv(sv["qs"], sv["qs"].transpose(0, 1, 3, 2), dos, dos.transpose(0, 1, 3, 2), sv["lse"],
                          delta, sv["kk"], sv["vv"], m_ctx)
    (dq_raw, dk_raw, dv_raw), _, (gw["qw"], gw["kw"]) = rowwise_bwd(
        nm("attn_prep"), _f_attn_prep,
        [_pcol(p, "q"), _pcol(p, "k"), _pcol(p, "v"), tabs["cq"], tabs["sq"], tabs["ck"], tabs["sk"]], [],
        [lw["qw"], lw["kw"], tabs["gq"], tabs["gk"]],
        [_unstack_heads(dq_s), _merge_kv(dk_s), _merge_kv(dv_s)],
        [True, True, True, False, False, False, False], [True, True, False, False], [CDT] * 3, t, nb)
    (dy_ssd, dxs_fin, dz), _, (gw["d_exp"], gw["ssd_nw"]) = rowwise_bwd(
        nm("ssd_fin"), _f_ssd_finish, [sv["yf"], sv["yb"], Cols(sv["xbc"], 0, 512), _pcol(p, "z")], [],
        [lw["d_exp"], lw["ssd_nw"]], [d_ssd_o], [True, False, True, True], [True, True], [F32, F32, CDT], t, nb)
    ssd_sh = [lw["dt_bias"], lw["a_log"]]
    post_ssd = lambda d: [jnp.concatenate(d[0:8], axis=1), d[8]]
    (dxbc_f, ddt_f), dsh_f = scan_bwd(nm("ssd_f"), _make_ssd_chunk(0), _ssd_rows(sv["xbc"], p), ssd_sh, sv["sf"],
                                      dy_ssd, post_ssd, [(1024, F32), (LANES, F32)], [], 4, t, m_ctx, False)
    (dxbc, ddt), dsh_b = scan_bwd(nm("ssd_b"), _make_ssd_chunk(1), _ssd_rows(sv["xbc"], p), ssd_sh, sv["sb"],
                                  dy_ssd, post_ssd, [(1024, F32), (LANES, CDT)], [dxbc_f, ddt_f], 4, t, m_ctx, True)
    gw["dt_bias"] = dsh_f[0] + dsh_b[0]
    gw["a_log"] = dsh_f[1] + dsh_b[1]
    dxbc_raw, gw["conv_w"], gw["conv_b"] = conv_bwd(_pcol(p, "xbc"), lw["conv_w"], lw["conv_b"], dxbc, dxs_fin, m_ctx)
    (dy_ret, drg), _, (gw["ret_gw"],) = rowwise_bwd(
        nm("ret_fin"), _f_ret_finish, [sv["rf"], sv["rb"], _pcol(p, "rg")], [], [lw["ret_gw"]], [d_ret_o],
        [True, False, True], [True], [F32, CDT], t, nb)
    post_ret = lambda d: [jnp.concatenate(d[0:4], axis=1), jnp.concatenate(d[4:8], axis=1),
                          jnp.concatenate(d[8:12], axis=1)]
    rrows = _ret_rows(sv["rq"], sv["rk"], p)
    r3 = [(512, F32)] * 3
    part, dlg_f = scan_bwd(nm("ret_f"), _make_ret_chunk(0), rrows, [lw["ret_lg"]], sv["rsf"], dy_ret, post_ret, r3,
                           [], 4, t, m_ctx, False)
    (drq_r, drk_r, drv), dlg_b = scan_bwd(nm("ret_b"), _make_ret_chunk(1), rrows, [lw["ret_lg"]], sv["rsb"], dy_ret,
                                          post_ret, [(512, F32), (512, F32), (512, CDT)], part, 4, t, m_ctx, True)
    gw["ret_lg"] = dlg_f[0] + dlg_b[0]
    (drq, drk), _, _ = rowwise_bwd(nm("ret_prep"), _f_ret_prep,
                                   [_pcol(p, "rq"), _pcol(p, "rk"), tabs["rc"], tabs["rs"]], [], [], [drq_r, drk_r],
                                   [True, True, False, False], [], [CDT, CDT], t, nb)
    pieces = {"gates": None, "xbc": dxbc_raw, "q": dq_raw, "z": dz, "rq": drq, "rk": drk, "rv": drv, "rg": drg,
              "k": dk_raw, "v": dv_raw, "dt": ddt}
    cols = list(dgl) + [pieces[n] for n in IN_NEW_ORDER[1:]]
    used = sum(c.shape[1] for c in cols)
    cols.append(jnp.zeros((t, IN_PAD - used), CDT))
    dp = jnp.concatenate(cols, axis=1)
    du = mm(nm("in_dx"), dp, lwt["w_in"], F32)
    gw["w_in"] = mm_tn(nm("in_dw"), sv["u"], dp)
    (dh_in,), (dsh1, dsc1), (gw["norm1_w"],) = rowwise_bwd(
        nm("norm1"), _f_norm_mod_thru, [sv["h"]], [sh1, sc1], [lw["norm1_w"]], [dh_a, du], [True], [True], [F32],
        t, nb)
    return dh_in, [dsh1, dsc1, dg1, dsh2, dsc2, dg2], gw


def _rope_tables(n_lat, m_ctx):
    rows = n_lat // GRID_W
    row = jnp.repeat(jnp.arange(rows, dtype=F32), GRID_W)
    col = jnp.tile(jnp.arange(GRID_W, dtype=F32), rows)
    nfreq = ATTN_HD // 4
    inv = ROPE_THETA ** (-jnp.arange(nfreq, dtype=F32) / nfreq)
    ang = jnp.concatenate([row[:, None] * inv, col[:, None] * inv], axis=-1)
    cos = jnp.concatenate([jnp.ones((m_ctx, ATTN_HD // 2), F32), jnp.cos(ang)], axis=0)
    sin = jnp.concatenate([jnp.zeros((m_ctx, ATTN_HD // 2), F32), jnp.sin(ang)], axis=0)
    c64 = jnp.concatenate([cos, cos], axis=1)
    s64 = jnp.concatenate([-sin, sin], axis=1)
    pos = jnp.arange(m_ctx + n_lat, dtype=F32)
    inv_r = ROPE_THETA ** (-jnp.linspace(0.0, 1.0, RET_DK // 2, dtype=F32))
    ang_r = pos[:, None] * inv_r
    rc = jnp.concatenate([jnp.cos(ang_r)] * 2, axis=1)
    rs = jnp.concatenate([-jnp.sin(ang_r), jnp.sin(ang_r)], axis=1)
    return dict(cq=jnp.tile(c64, (1, 8)), sq=jnp.tile(s64, (1, 8)), ck=jnp.tile(c64, (1, 2)), sk=jnp.tile(s64, (1, 2)),
                rc=jnp.tile(rc, (1, 4)), rs=jnp.tile(rs, (1, 4)), gq=_group_matrix(512, ATTN_HD),
                gk=_group_matrix(128, ATTN_HD))


def _layer_weights(full, small, layer):
    lw = dict(
        w_in=_in_to_padded(full["w_in"][layer]), w_branch=full["w_branch"][layer], w_out=full["w_out"][layer],
        w_mlp1=full["w_mlp1"][layer], w_mlp2=full["w_mlp2"][layer],
        norm1_w=small["norm1_w"][layer][None], norm2_w=small["norm2_w"][layer][None],
        qw=jnp.tile(small["attn_q_norm"][layer], 8)[None], kw=jnp.tile(small["attn_k_norm"][layer], 2)[None],
        conv_w=small["ssd_conv_w"][layer], conv_b=small["ssd_conv_b"][layer][None],
        dt_bias=_lane_pad(small["ssd_dt_bias"][layer]), a_log=_lane_pad(small["ssd_a_log"][layer]),
        d_exp=jnp.repeat(small["ssd_d"][layer], SSD_HD)[None], ssd_nw=small["ssd_norm_w"][layer][None],
        ret_lg=_lane_pad(small["ret_log_decay"][layer]), ret_gw=small["ret_gn_w"][layer][None])
    lwt = dict(w_in=lw["w_in"].T, w_branch=jnp.swapaxes(lw["w_branch"], 1, 2), w_out=lw["w_out"].T,
               w_mlp1=lw["w_mlp1"].T, w_mlp2=lw["w_mlp2"].T)
    return lw, lwt


def local_step(x, c, ctx, full, small, loss_target):
    n_lat, d = x.shape
    m_ctx = ctx.shape[0]
    t = n_lat + m_ctx
    depth = small["norm1_w"].shape[0]
    tabs = _rope_tables(n_lat, m_ctx)
    h = jnp.concatenate([ctx, x], axis=0)
    cc = jnp.concatenate([small["c_ctx"][None], c, jnp.zeros((COND_ROWS - 2, d), F32)], axis=0)
    (scc,) = rowwise_fwd("cond_silu", _f_silu, [cc], [], [], [(d, CDT)], COND_ROWS, 0)
    mods, saved, lws = [], [], []
    for layer in range(depth):
        lw, lwt = _layer_weights(full, small, layer)
        mod_raw = mm(f"l{layer}_mod", scc, full["w_mod"][layer], F32)
        (mod8,) = rowwise_fwd(f"l{layer}_mod_bias", _f_bias, [mod_raw], [], [small["b_mod"][layer][None]],
                              [(6 * d, F32)], COND_ROWS, 0)
        mod = [mod8[0:2, k * d:(k + 1) * d].reshape(2, 1, d) for k in range(6)]
        h, sv = layer_fwd(layer, h, mod, lw, tabs, m_ctx)
        mods.append(mod)
        saved.append(sv)
        lws.append((lw, lwt))
    loss, dh, d_final = loss_head(h, loss_target, small["final_norm_w"][None], m_ctx)

    gbig = {k: [None] * depth for k in BIG}
    gs = {k: [None] * depth for k in SMALL if k not in ("c_ctx", "final_norm_w")}
    d_scc = None
    for layer in reversed(range(depth)):
        lw, lwt = lws[layer]
        dh, dmod, gw = layer_bwd(layer, dh, saved[layer], mods[layer], lw, lwt, tabs, m_ctx)
        dmod8 = jnp.concatenate([jnp.concatenate([g_.reshape(2, d) for g_ in dmod], axis=1),
                                 jnp.zeros((COND_ROWS - 2, 6 * d), F32)], axis=0)
        (dmod_c,), _, (db_mod,) = rowwise_bwd(f"l{layer}_mod_bias_bwd", _f_bias, [dmod8], [],
                                              [small["b_mod"][layer][None]], [dmod8], [True], [True], [CDT], COND_ROWS, 0)
        gbig["w_mod"][layer] = mm_tn(f"l{layer}_mod_dw", scc, dmod_c)
        part = mm(f"l{layer}_mod_dx", dmod_c, full["w_mod"][layer].T, F32)
        d_scc = part if d_scc is None else d_scc + part
        gbig["w_in"][layer] = _in_from_padded(gw["w_in"])
        for k in ("w_branch", "w_out", "w_mlp1", "w_mlp2"):
            gbig[k][layer] = gw[k]
        gs["b_mod"][layer] = db_mod.reshape(-1)
        gs["norm1_w"][layer] = gw["norm1_w"].reshape(-1)
        gs["norm2_w"][layer] = gw["norm2_w"].reshape(-1)
        gs["attn_q_norm"][layer] = gw["qw"].reshape(8, ATTN_HD).sum(0)
        gs["attn_k_norm"][layer] = gw["kw"].reshape(2, ATTN_HD).sum(0)
        gs["ssd_conv_w"][layer] = gw["conv_w"]
        gs["ssd_conv_b"][layer] = gw["conv_b"].reshape(-1)
        gs["ssd_dt_bias"][layer] = gw["dt_bias"][0, :16].reshape(2, 8)
        gs["ssd_a_log"][layer] = gw["a_log"][0, :16].reshape(2, 8)
        gs["ssd_d"][layer] = gw["d_exp"].reshape(SSD_HEADS, SSD_HD).sum(1)
        gs["ssd_norm_w"][layer] = gw["ssd_nw"].reshape(-1)
        gs["ret_log_decay"][layer] = gw["ret_lg"][0, :8].reshape(2, 4)
        gs["ret_gn_w"][layer] = gw["ret_gw"].reshape(-1)
    (d_cc,), _, _ = rowwise_bwd("cond_silu_bwd", _f_silu, [cc], [], [], [d_scc], [True], [], [F32], COND_ROWS, 0)
    g_small = {k: jnp.stack(v) for k, v in gs.items()}
    g_small["c_ctx"] = d_cc[0]
    g_small["final_norm_w"] = d_final.reshape(-1)
    g_big = {k: jnp.stack(v) for k, v in gbig.items()}
    return loss, dh[m_ctx:], g_big, g_small


def _shards(a, axis):
    return jnp.split(a, 4, axis=axis)


def kernel(x, c, ctx, c_ctx, w_mod, b_mod, norm1_w, norm2_w, w_in, attn_q_norm, attn_k_norm, ssd_conv_w, ssd_conv_b, ssd_dt_bias, ssd_a_log, ssd_d, ssd_norm_w, ret_log_decay, ret_gn_w, w_branch, w_out, w_mlp1, w_mlp2, final_norm_w, loss_target, m_c_ctx, m_w_mod, m_b_mod, m_norm1_w, m_norm2_w, m_w_in, m_attn_q_norm, m_attn_k_norm, m_ssd_conv_w, m_ssd_conv_b, m_ssd_dt_bias, m_ssd_a_log, m_ssd_d, m_ssd_norm_w, m_ret_log_decay, m_ret_gn_w, m_w_branch, m_w_out, m_w_mlp1, m_w_mlp2, m_final_norm_w, v_c_ctx, v_w_mod, v_b_mod, v_norm1_w, v_norm2_w, v_w_in, v_attn_q_norm, v_attn_k_norm, v_ssd_conv_w, v_ssd_conv_b, v_ssd_dt_bias, v_ssd_a_log, v_ssd_d, v_ssd_norm_w, v_ret_log_decay, v_ret_gn_w, v_w_branch, v_w_out, v_w_mlp1, v_w_mlp2, v_final_norm_w):
    env = dict(locals())
    w_loc = {k: env[k] for k in WEIGHTS}
    m_loc = {k: env["m_" + k] for k in WEIGHTS}
    v_loc = {k: env["v_" + k] for k in WEIGHTS}
    chip = 2 * lax.axis_index("x") + lax.axis_index("y")
    core = lax.axis_index("c")

    big_shapes = [w_loc[k].shape for k in BIG]
    gathered = gather_chips("gather_weights", _pack_flat([w_loc[k] for k in BIG], CDT, 1024, PACK_ROWS))
    per_chip = [_unpack_flat(gathered[j], big_shapes) for j in range(4)]
    full = {k: jnp.concatenate([per_chip[j][i] for j in range(4)], axis=BIG_AXIS[k]) for i, k in enumerate(BIG)}

    cw = w_loc["ssd_conv_w"]
    cw_w = cw.shape[-1]
    placed = lax.dynamic_update_slice(jnp.zeros(cw.shape[:-1] + (4 * cw_w,), F32),
                                      cw * (core == 0).astype(F32), (0, 0, chip * cw_w))
    conv_full = _unpack_flat(allreduce_small("gather_conv_w", _pack_flat([placed], F32, LANES)), [placed.shape])[0]
    small = {k: w_loc[k] for k in SMALL}
    small["ssd_conv_w"] = conv_full

    loss_l, grad_x, g_big, g_small = local_step(x[0], c, ctx[0], full, small, loss_target[0])

    small_shapes = [g_small[k].shape for k in SMALL] + [(LANES,)]
    summed = _unpack_flat(allreduce_small("reduce_small", _pack_flat([g_small[k] for k in SMALL] + [loss_l], F32, LANES)),
                          small_shapes)
    gsum = dict(zip(SMALL, summed[:-1]))
    loss = summed[-1][0]
    gsum["ssd_conv_w"] = lax.dynamic_slice(gsum["ssd_conv_w"], (0, 0, chip * cw_w), cw.shape)

    pieces = jnp.stack([_pack_flat([_shards(g_big[k], BIG_AXIS[k])[j] for k in BIG], CDT, 1024, PACK_ROWS) for j in range(4)])
    chip_sum = sum_parts("sum_chips", scatter_chips("scatter_grads", pieces))
    both = swap_cores("swap_cores", chip_sum)
    g0 = _unpack_flat(both[0], big_shapes)
    g1 = _unpack_flat(both[1], big_shapes)

    grads, deltas, new_m, new_v = {}, {}, {}, {}
    for i, k in enumerate(BIG):
        shp = w_loc[k].shape
        two_d = lambda a, shp=shp: a.reshape(-1, shp[-1])
        res = adamw("adamw_" + k, two_d(w_loc[k]), two_d(m_loc[k]), two_d(v_loc[k]), [two_d(g0[i]), two_d(g1[i])])
        grads[k], deltas[k], new_m[k], new_v[k] = [r.reshape(shp) for r in res]
    small_loc_shapes = [w_loc[k].shape for k in SMALL]
    res = adamw("adamw_small", _pack_flat([w_loc[k] for k in SMALL], F32, LANES),
                _pack_flat([m_loc[k] for k in SMALL], F32, LANES), _pack_flat([v_loc[k] for k in SMALL], F32, LANES),
                [_pack_flat([gsum[k] for k in SMALL], F32, LANES)])
    for dst, r in zip((grads, deltas, new_m, new_v), res):
        dst.update(dict(zip(SMALL, _unpack_flat(r, small_loc_shapes))))

    return (loss, grad_x[None], *[grads[k] for k in WEIGHTS], *[deltas[k] for k in WEIGHTS],
            *[new_m[k] for k in WEIGHTS], *[new_v[k] for k in WEIGHTS])
```

```python
import functools
import math
from typing import NamedTuple

import jax
import jax.numpy as jnp
from jax import lax
from jax.experimental import pallas as pl
from jax.experimental.pallas import tpu as pltpu

F32 = jnp.float32
CDT = jnp.bfloat16
NORM_EPS = 1e-6
ROPE_THETA = 10000.0
GRID_W = 64
D_MODEL = 1024
ATTN_HEADS, ATTN_KV, ATTN_HD = 8, 2, 64
SSD_HEADS, SSD_HD, SSD_STATE = 8, 64, 128
RET_HEADS, RET_DK = 4, 128
CHUNK = 128
ROW_TILE = 256
MM_ROWS = 768
ATTN_TQ, ATTN_TK = 256, 256
ATTN_TK_FWD = 512
LANES = 128
PACK_ROWS = 512
SWAP_CHUNKS = 16
GATHER_CHUNKS = 4
COND_ROWS = 16
VMEM_LIMIT = 56 * 1024 * 1024

ADAM_LR, ADAM_B1, ADAM_B2, ADAM_EPS, ADAM_WD, ADAM_STEP = 0.001, 0.9, 0.999, 1e-08, 0.01, 10

IN_LAYOUT = {
    "gates": (0, 3072, 4368, 3072), "xbc": (3072, 1024, 1280, 1024), "q": (4096, 512, 0, 512),
    "z": (4608, 512, 768, 512), "rq": (5120, 512, 2320, 512), "rk": (5632, 512, 2832, 512),
    "rv": (6144, 512, 3344, 512), "rg": (6656, 512, 3856, 512), "k": (7168, 128, 512, 128),
    "v": (7296, 128, 640, 128), "dt": (7424, 128, 2304, 16),
}
IN_PAD = 7680
IN_ORIG_ORDER = ("q", "k", "v", "z", "xbc", "dt", "rq", "rk", "rv", "rg", "gates")
IN_NEW_ORDER = ("gates", "xbc", "q", "z", "rq", "rk", "rv", "rg", "k", "v", "dt")

BIG = ("w_mod", "w_in", "w_branch", "w_out", "w_mlp1", "w_mlp2")
BIG_AXIS = {"w_mod": 2, "w_in": 2, "w_branch": 3, "w_out": 1, "w_mlp1": 2, "w_mlp2": 1}
SMALL = ("c_ctx", "b_mod", "norm1_w", "norm2_w", "attn_q_norm", "attn_k_norm", "ssd_conv_w", "ssd_conv_b",
         "ssd_dt_bias", "ssd_a_log", "ssd_d", "ssd_norm_w", "ret_log_decay", "ret_gn_w", "final_norm_w")
WEIGHTS = ("c_ctx", "w_mod", "b_mod", "norm1_w", "norm2_w", "w_in", "attn_q_norm", "attn_k_norm", "ssd_conv_w",
           "ssd_conv_b", "ssd_dt_bias", "ssd_a_log", "ssd_d", "ssd_norm_w", "ret_log_decay", "ret_gn_w",
           "w_branch", "w_out", "w_mlp1", "w_mlp2", "final_norm_w")


def _cp(sem):
    return pltpu.CompilerParams(dimension_semantics=sem, vmem_limit_bytes=VMEM_LIMIT)


class Cols(NamedTuple):
    arr: jax.Array
    off: int
    width: int


def _width(item):
    return item.width if isinstance(item, Cols) else item.shape[1]


def _row_in(item, rows, imap=None):
    imap = imap or (lambda i: i)
    if isinstance(item, Cols):
        assert item.off % item.width == 0
        blk = item.off // item.width
        return item.arr, pl.BlockSpec((rows, item.width), lambda i, blk=blk: (imap(i), blk))
    return item, pl.BlockSpec((rows, item.shape[1]), lambda i: (imap(i), 0))


def _const_spec(shape):
    return pl.BlockSpec(shape, lambda *_: (0,) * len(shape))


def _mxu(a, b, dims=(((1,), (0,)), ((), ()))):
    return lax.dot_general(a.astype(CDT), b.astype(CDT), dims, preferred_element_type=F32)


_NT = (((1,), (1,)), ((), ()))
_TN = (((0,), (0,)), ((), ()))


@jax.custom_vjp
def _softplus(x):
    return jnp.maximum(x, 0.0) + jnp.log1p(jnp.exp(-jnp.abs(x)))


def _softplus_fwd(x):
    return _softplus(x), x


def _softplus_bwd(x, g):
    return (g * jax.nn.sigmoid(x),)


_softplus.defvjp(_softplus_fwd, _softplus_bwd)


def _group_mean_impl(x, gmat):
    hi = x.astype(CDT)
    lo = (x - hi.astype(F32)).astype(CDT)
    return (jnp.dot(hi, gmat, preferred_element_type=F32) + jnp.dot(lo, gmat, preferred_element_type=F32))


@jax.custom_vjp
def _group_mean(x, gmat):
    return _group_mean_impl(x, gmat)


def _group_mean_fwd(x, gmat):
    return _group_mean_impl(x, gmat), gmat


def _group_mean_bwd(gmat, g):
    return _group_mean_impl(g, gmat), jnp.zeros_like(gmat)


_group_mean.defvjp(_group_mean_fwd, _group_mean_bwd)


def _group_matrix(width, group):
    r = jnp.arange(width) // group
    return jnp.where(r[:, None] == r[None, :], 1.0 / group, 0.0).astype(CDT)


def _make_rope(half):
    def partner(x):
        w = x.shape[1]
        lane = lax.broadcasted_iota(jnp.int32, x.shape, 1)
        first = (lane % (2 * half)) < half
        return jnp.where(first, pltpu.roll(x, w - half, axis=1), pltpu.roll(x, half, axis=1))

    def impl(x, cos_full, sin_signed):
        return x * cos_full + partner(x) * sin_signed

    @jax.custom_vjp
    def rope(x, cos_full, sin_signed):
        return impl(x, cos_full, sin_signed)

    def fwd(x, cos_full, sin_signed):
        return impl(x, cos_full, sin_signed), (cos_full, sin_signed)

    def bwd(res, g):
        cos_full, sin_signed = res
        return impl(g, cos_full, -sin_signed), jnp.zeros_like(cos_full), jnp.zeros_like(sin_signed)

    rope.defvjp(fwd, bwd)
    return rope


_rope32 = _make_rope(32)
_rope64 = _make_rope(64)


def _rms(x, w):
    return x * lax.rsqrt(jnp.mean(x * x, axis=-1, keepdims=True) + NORM_EPS) * w


def _col(v, lane_index):
    lane = lax.broadcasted_iota(jnp.int32, v.shape, 1)
    return jnp.sum(jnp.where(lane == lane_index, v, 0.0), axis=1, keepdims=True)


def _typed_spec(width, nb_ctx):
    return pl.BlockSpec((None, 1, width), lambda i: (jnp.where(i >= nb_ctx, 1, 0), 0, 0))


def rowwise_fwd(name, f, rows, typed, shared, outs, n_rows, nb_ctx, tm=ROW_TILE):
    tm = min(tm, n_rows)
    nin = len(rows) + len(typed) + len(shared)

    def body(*refs):
        res = f(*[r[...] for r in refs[:nin]])
        for o_ref, o in zip(refs[nin:], res):
            o_ref[...] = o.astype(o_ref.dtype)

    arrs, specs = [], []
    for it in rows:
        a, s = _row_in(it, tm)
        arrs.append(a)
        specs.append(s)
    for t in typed:
        arrs.append(t)
        specs.append(_typed_spec(t.shape[-1], nb_ctx))
    for s_ in shared:
        arrs.append(s_)
        specs.append(_const_spec(s_.shape))
    res = pl.pallas_call(
        body, name=name, grid=(n_rows // tm,), in_specs=specs,
        out_specs=[pl.BlockSpec((tm, w), lambda i: (i, 0)) for w, _ in outs],
        out_shape=[jax.ShapeDtypeStruct((n_rows, w), dt) for w, dt in outs],
        compiler_params=_cp(("parallel",)))(*arrs)
    return res


def rowwise_bwd(name, f, rows, typed, shared, cots, row_diff, shared_diff, drow_dtypes, n_rows, nb_ctx, tm=ROW_TILE):
    tm = min(tm, n_rows)
    nr, nt, ns, nc = len(rows), len(typed), len(shared), len(cots)
    nin = nr + nt + ns
    d_rows = [k for k in range(nr) if row_diff[k]]
    d_sh = [k for k in range(ns) if shared_diff[k]]

    def body(*refs):
        rvals = [r[...] for r in refs[:nr]]
        tvals = [r[...] for r in refs[nr:nr + nt]]
        svals = [r[...] for r in refs[nr + nt:nin]]
        cvals = [r[...].astype(F32) for r in refs[nin:nin + nc]]
        out_refs = refs[nin + nc:]

        def g(*dv):
            dv = list(dv)
            rv = list(rvals)
            for k in d_rows:
                rv[k] = dv.pop(0)
            tv = [dv.pop(0) for _ in range(nt)]
            sv = list(svals)
            for k in d_sh:
                sv[k] = dv.pop(0)
            return tuple(o.astype(F32) for o in f(*rv, *tv, *sv))

        prim = [rvals[k].astype(F32) for k in d_rows] + tvals + [svals[k] for k in d_sh]
        _, vjp = jax.vjp(g, *prim)
        grads = list(vjp(tuple(cvals)))
        i = pl.program_id(0)
        for ref in out_refs[:len(d_rows)]:
            ref[...] = grads.pop(0).astype(ref.dtype)
        first_typed = (i == 0) | (i == nb_ctx)
        for ref in out_refs[len(d_rows):len(d_rows) + nt]:
            gr = grads.pop(0)

            @pl.when(first_typed)
            def _(ref=ref, gr=gr):
                ref[...] = gr

            @pl.when(jnp.logical_not(first_typed))
            def _(ref=ref, gr=gr):
                ref[...] += gr
        for ref in out_refs[len(d_rows) + nt:]:
            gr = grads.pop(0)

            @pl.when(i == 0)
            def _(ref=ref, gr=gr):
                ref[...] = gr

            @pl.when(i != 0)
            def _(ref=ref, gr=gr):
                ref[...] += gr

    arrs, specs = [], []
    for it in list(rows):
        a, s = _row_in(it, tm)
        arrs.append(a)
        specs.append(s)
    for t in typed:
        arrs.append(t)
        specs.append(_typed_spec(t.shape[-1], nb_ctx))
    for s_ in shared:
        arrs.append(s_)
        specs.append(_const_spec(s_.shape))
    for c_ in cots:
        a, s = _row_in(c_, tm)
        arrs.append(a)
        specs.append(s)
    out_specs, out_shape = [], []
    for k, dt in zip(d_rows, drow_dtypes):
        w = _width(rows[k])
        out_specs.append(pl.BlockSpec((tm, w), lambda i: (i, 0)))
        out_shape.append(jax.ShapeDtypeStruct((n_rows, w), dt))
    for t in typed:
        out_specs.append(_typed_spec(t.shape[-1], nb_ctx))
        out_shape.append(jax.ShapeDtypeStruct(t.shape, F32))
    for k in d_sh:
        out_specs.append(_const_spec(shared[k].shape))
        out_shape.append(jax.ShapeDtypeStruct(shared[k].shape, F32))
    res = pl.pallas_call(body, name=name, grid=(n_rows // tm,), in_specs=specs, out_specs=out_specs,
                         out_shape=out_shape, compiler_params=_cp(("arbitrary",)))(*arrs)
    n1, n2 = len(d_rows), len(d_rows) + nt
    return list(res[:n1]), list(res[n1:n2]), list(res[n2:])


def _pick(n, prefs):
    for p in prefs:
        if n % p == 0:
            return p
    return n


def mm(name, a, b, out_dtype):
    k, n = b.shape
    m = (a.arr if isinstance(a, Cols) else a).shape[0]
    assert _width(a) == k
    tm = _pick(m, (MM_ROWS, 256))
    tn = _pick(n, (512, 256, 128))

    def body(a_ref, b_ref, o_ref):
        o_ref[...] = jnp.dot(a_ref[...], b_ref[...], preferred_element_type=F32).astype(o_ref.dtype)

    a_arr, a_spec = _row_in(a, tm)
    a_spec = pl.BlockSpec(a_spec.block_shape, lambda j, i, f=a_spec.index_map: f(i))
    return pl.pallas_call(
        body, name=name, grid=(n // tn, m // tm),
        in_specs=[a_spec, pl.BlockSpec((k, tn), lambda j, i: (0, j))],
        out_specs=pl.BlockSpec((tm, tn), lambda j, i: (i, j)),
        out_shape=jax.ShapeDtypeStruct((m, n), out_dtype),
        compiler_params=_cp(("parallel", "parallel")))(a_arr, b)


def mm_tn(name, a, b):
    t = (a.arr if isinstance(a, Cols) else a).shape[0]
    k, n = _width(a), _width(b)
    tt = _pick(t, (MM_ROWS, 256))
    tk = _pick(k, (512, 256, 128))
    tn = _pick(n, (1280, 1024, 512, 256, 128))

    def body(a_ref, b_ref, o_ref):
        part = lax.dot_general(a_ref[...], b_ref[...], _TN, preferred_element_type=F32)

        @pl.when(pl.program_id(2) == 0)
        def _():
            o_ref[...] = part

        @pl.when(pl.program_id(2) != 0)
        def _():
            o_ref[...] += part

    def win(item, width):
        if isinstance(item, Cols):
            assert item.off % width == 0
            return item.arr, item.off // width
        return item, 0

    a_arr, a0 = win(a, tk)
    b_arr, b0 = win(b, tn)
    return pl.pallas_call(
        body, name=name, grid=(k // tk, n // tn, t // tt),
        in_specs=[pl.BlockSpec((tt, tk), lambda ki, ni, ti: (ti, a0 + ki)),
                  pl.BlockSpec((tt, tn), lambda ki, ni, ti: (ti, b0 + ni))],
        out_specs=pl.BlockSpec((tk, tn), lambda ki, ni, ti: (ki, ni)),
        out_shape=jax.ShapeDtypeStruct((k, n), F32),
        compiler_params=_cp(("parallel", "parallel", "arbitrary")))(a_arr, b_arr)


def attn_fwd(name, qT, kk, vT, m_ctx):
    _, nq, hd, r = qT.shape
    t = kk.shape[1]
    tq = r // 4
    tk = _pick(t - m_ctx, (ATTN_TK_FWD, ATTN_TK))
    nqc, n_lat_tiles = m_ctx // tq, (t - m_ctx) // tk

    def body(qT_ref, k_ref, vT_ref, oT_ref, lse_ref):
        i = pl.program_id(1)
        q_t = qT_ref[...]

        def tile(off, size, carry):
            mi, li, acc = carry
            st = jnp.dot(k_ref[pl.ds(off, size), :], q_t, preferred_element_type=F32)
            mn = jnp.maximum(mi, jnp.max(st, axis=0, keepdims=True))
            pt = jnp.exp(st - mn)
            al = jnp.exp(mi - mn)
            li = al * li + jnp.sum(pt, axis=0, keepdims=True)
            acc = al * acc + jnp.dot(vT_ref[:, pl.ds(off, size)], pt.astype(CDT), preferred_element_type=F32)
            return mn, li, acc

        init = (jnp.full((1, r), -1e30, F32), jnp.zeros((1, r), F32), jnp.zeros((hd, r), F32))
        carry = tile(0, m_ctx, init)
        mi, li, acc = lax.fori_loop(
            0, jnp.where(i < nqc, 0, n_lat_tiles),
            lambda j, cr: tile(pl.multiple_of(m_ctx + j * tk, ATTN_TK), tk, cr), carry)
        oT_ref[...] = (acc / li).astype(oT_ref.dtype)
        lse_ref[...] = mi + jnp.log(li)

    return pl.pallas_call(
        body, name=name, grid=(2, nq),
        in_specs=[pl.BlockSpec((None, None, hd, r), lambda g, i: (g, i, 0, 0)),
                  pl.BlockSpec((None, t, hd), lambda g, i: (g, 0, 0)),
                  pl.BlockSpec((None, hd, t), lambda g, i: (g, 0, 0))],
        out_specs=[pl.BlockSpec((None, None, hd, r), lambda g, i: (g, i, 0, 0)),
                   pl.BlockSpec((None, None, 1, r), lambda g, i: (g, i, 0, 0))],
        out_shape=[jax.ShapeDtypeStruct(qT.shape, CDT), jax.ShapeDtypeStruct((2, nq, 1, r), F32)],
        compiler_params=_cp(("parallel", "arbitrary")))(qT, kk, vT)


def attn_bwd(name, qT, qs, doT, dos, oT, lse, kk, kT, vv, m_ctx):
    _, nq, hd, r = qT.shape
    t = kk.shape[1]
    tq, tk = r // 4, ATTN_TK
    nqc, nkc, nk = m_ctx // tq, m_ctx // tk, t // tk

    def body(qT_ref, q_ref, doT_ref, do_ref, oT_ref, lse_ref, k_ref, kT_ref, v_ref,
             dq_ref, dk_ref, dv_ref, dk_acc, dv_acc, delta_s):
        j, i = pl.program_id(1), pl.program_id(2)

        @pl.when(i == 0)
        def _():
            dk_acc[...] = jnp.zeros_like(dk_acc)
            dv_acc[...] = jnp.zeros_like(dv_acc)

        @pl.when(j == 0)
        def _():
            delta_s[i] = jnp.sum(doT_ref[...].astype(F32) * oT_ref[...].astype(F32), axis=0, keepdims=True)

        @pl.when((j < nkc) | (i >= nqc))
        def _():
            st = jnp.dot(k_ref[...], qT_ref[...], preferred_element_type=F32)
            pt = jnp.exp(st - lse_ref[...])
            dv_acc[...] += jnp.dot(pt.astype(CDT), do_ref[...], preferred_element_type=F32)
            dpt = jnp.dot(v_ref[...], doT_ref[...], preferred_element_type=F32)
            dst = (pt * (dpt - delta_s[i])).astype(CDT)
            dk_acc[...] += jnp.dot(dst, q_ref[...], preferred_element_type=F32)
            part = jnp.dot(kT_ref[...], dst, preferred_element_type=F32)

            @pl.when(j == 0)
            def _():
                dq_ref[i] = part

            @pl.when(j != 0)
            def _():
                dq_ref[i] += part

        @pl.when(i == nq - 1)
        def _():
            dk_ref[...] = dk_acc[...]
            dv_ref[...] = dv_acc[...]

    blk = pl.BlockSpec((None, None, r, hd), lambda g, j, i: (g, i, 0, 0))
    blk_t = pl.BlockSpec((None, None, hd, r), lambda g, j, i: (g, i, 0, 0))
    row = pl.BlockSpec((None, None, 1, r), lambda g, j, i: (g, i, 0, 0))
    kv = pl.BlockSpec((None, tk, hd), lambda g, j, i: (g, j, 0))
    kv_t = pl.BlockSpec((None, hd, tk), lambda g, j, i: (g, 0, j))
    return pl.pallas_call(
        body, name=name, grid=(2, nk, nq),
        in_specs=[blk_t, blk, blk_t, blk, blk_t, row, kv, kv_t, kv],
        out_specs=[pl.BlockSpec((None, nq, hd, r), lambda g, j, i: (g, 0, 0, 0)), kv, kv],
        out_shape=[jax.ShapeDtypeStruct(qT.shape, F32), jax.ShapeDtypeStruct(kk.shape, F32),
                   jax.ShapeDtypeStruct(kk.shape, F32)],
        scratch_shapes=[pltpu.VMEM((tk, hd), F32), pltpu.VMEM((tk, hd), F32), pltpu.VMEM((nq, 1, r), F32)],
        compiler_params=_cp(("parallel", "arbitrary", "arbitrary")))(qT, qs, doT, dos, oT, lse, kk, kT, vv)


def _stack_heads(a, tq):
    t = a.shape[0]
    return a.reshape(t // tq, tq, 2, 4, ATTN_HD).transpose(2, 0, 3, 1, 4).reshape(2, t // tq, 4 * tq, ATTN_HD)


def _stack_heads_t(a, tq):
    t = a.shape[0]
    return a.reshape(t // tq, tq, 2, 4, ATTN_HD).transpose(2, 0, 4, 3, 1).reshape(2, t // tq, ATTN_HD, 4 * tq)


def _unstack_heads_t(a):
    _, nq, hd, r = a.shape
    tq = r // 4
    return a.reshape(2, nq, hd, 4, tq).transpose(1, 4, 0, 3, 2).reshape(nq * tq, 8 * hd)


def _split_kv(a):
    return a.reshape(a.shape[0], 2, ATTN_HD).transpose(1, 0, 2)


def _merge_kv(a):
    return a.transpose(1, 0, 2).reshape(a.shape[1], 2 * ATTN_HD)


def _chunk_order(rev, ncc, nct):
    if not rev:
        return lambda s: s
    return lambda s: jnp.where(s < ncc, ncc - 1 - s, nct - 1 - (s - ncc))


def scan_fwd(name, fn, rows, shared, n_state, y_width, n_rows, m_ctx, rev):
    nct, ncc = n_rows // CHUNK, m_ctx // CHUNK
    order = _chunk_order(rev, ncc, nct)
    nr, ns = len(rows), len(shared)

    def body(*refs):
        rvals = [r[...] for r in refs[:nr]]
        svals = [r[...] for r in refs[nr:nr + ns]]
        y_ref, sin_ref, st = refs[nr + ns:]

        @pl.when(pl.program_id(0) == 0)
        def _():
            st[...] = jnp.zeros_like(st)

        prev = [st[k] for k in range(n_state)]
        sin_ref[...] = st[...]
        y, new = fn(rvals, svals, prev)
        y_ref[...] = y
        for k in range(n_state):
            st[k] = new[k]

    arrs, specs = [], []
    for it in rows:
        a, s = _row_in(it, CHUNK, order)
        arrs.append(a)
        specs.append(s)
    for s_ in shared:
        arrs.append(s_)
        specs.append(_const_spec(s_.shape))
    return pl.pallas_call(
        body, name=name, grid=(nct,), in_specs=specs,
        out_specs=[pl.BlockSpec((CHUNK, y_width), lambda s: (order(s), 0)),
                   pl.BlockSpec((None, n_state, LANES, LANES), lambda s: (order(s), 0, 0, 0))],
        out_shape=[jax.ShapeDtypeStruct((n_rows, y_width), F32),
                   jax.ShapeDtypeStruct((nct, n_state, LANES, LANES), F32)],
        scratch_shapes=[pltpu.VMEM((n_state, LANES, LANES), F32)],
        compiler_params=_cp(("arbitrary",)))(*arrs)


def scan_bwd(name, fn, rows, shared, states_in, dy, post, outs, addends, n_state, n_rows, m_ctx, rev):
    nct, ncc = n_rows // CHUNK, m_ctx // CHUNK
    fwd_order = _chunk_order(rev, ncc, nct)
    order = lambda r: fwd_order(nct - 1 - r)
    nr, ns, na = len(rows), len(shared), len(addends)
    nin = nr + ns

    def body(*refs):
        rvals = [r[...] for r in refs[:nr]]
        svals = [r[...] for r in refs[nr:nin]]
        sin_ref, dy_ref = refs[nin], refs[nin + 1]
        add_refs = refs[nin + 2:nin + 2 + na]
        out_refs = refs[nin + 2 + na:nin + 2 + na + len(outs)]
        dsh_refs = refs[nin + 2 + na + len(outs):-1]
        dst = refs[-1]
        r = pl.program_id(0)

        @pl.when(r == 0)
        def _():
            dst[...] = jnp.zeros_like(dst)

        prev = [sin_ref[k] for k in range(n_state)]
        _, vjp = jax.vjp(fn, rvals, svals, prev)
        d_rows, d_shared, d_prev = vjp((dy_ref[...], [dst[k] for k in range(n_state)]))
        res = post(d_rows)
        for k, (ref, val) in enumerate(zip(out_refs, res)):
            if k < na:
                val = val + add_refs[k][...]
            ref[...] = val.astype(ref.dtype)
        for ref, gr in zip(dsh_refs, d_shared):
            @pl.when(r == 0)
            def _(ref=ref, gr=gr):
                ref[...] = gr

            @pl.when(r != 0)
            def _(ref=ref, gr=gr):
                ref[...] += gr
        for k in range(n_state):
            dst[k] = d_prev[k]

    arrs, specs = [], []
    for it in rows:
        a, s = _row_in(it, CHUNK, order)
        arrs.append(a)
        specs.append(s)
    for s_ in shared:
        arrs.append(s_)
        specs.append(_const_spec(s_.shape))
    arrs.append(states_in)
    specs.append(pl.BlockSpec((None, n_state, LANES, LANES), lambda r: (order(r), 0, 0, 0)))
    for it in [dy] + list(addends):
        a, s = _row_in(it, CHUNK, order)
        arrs.append(a)
        specs.append(s)
    out_specs = [pl.BlockSpec((CHUNK, w), lambda r: (order(r), 0)) for w, _ in outs]
    out_shape = [jax.ShapeDtypeStruct((n_rows, w), dt) for w, dt in outs]
    for s_ in shared:
        out_specs.append(_const_spec(s_.shape))
        out_shape.append(jax.ShapeDtypeStruct(s_.shape, F32))
    res = pl.pallas_call(body, name=name, grid=(nct,), in_specs=specs, out_specs=out_specs, out_shape=out_shape,
                         scratch_shapes=[pltpu.VMEM((n_state, LANES, LANES), F32)],
                         compiler_params=_cp(("arbitrary",)))(*arrs)
    return list(res[:len(outs)]), list(res[len(outs):])


def _make_ssd_chunk(direction):
    rev = direction == 1
    base = 8 * direction

    def fn(rows, shared, prev):
        xs, bms, cms, dtraw = rows[0:4], rows[4:6], rows[6:8], rows[8]
        dt_bias, a_log = shared
        ln = dtraw.shape[0]
        dt_all = _softplus(dtraw + dt_bias)
        a_all = dt_all * (-jnp.exp(a_log))
        r_i = lax.broadcasted_iota(jnp.int32, (ln, ln), 0)
        c_i = lax.broadcasted_iota(jnp.int32, (ln, ln), 1)
        tri = (r_i <= c_i) if rev else (r_i >= c_i)
        a_cum_all = jnp.dot(tri.astype(F32), a_all, precision=lax.Precision.HIGHEST, preferred_element_type=F32)
        a_tot_all = jnp.sum(a_all, axis=0, keepdims=True)
        first = lax.broadcasted_iota(jnp.int32, (ln, LANES), 1) < SSD_HD
        first_row = lax.broadcasted_iota(jnp.int32, (LANES, 1), 0) < SSD_HD

        def lmat(acol):
            a_b = jnp.broadcast_to(acol, (ln, ln))
            seg = a_b - a_b.T
            return jnp.where(tri, jnp.exp(jnp.where(tri, seg, 0.0)), 0.0)

        ys, new = [], []
        for g in range(2):
            bm, cm = bms[g], cms[g]
            cb = _mxu(cm, bm, _NT)
            for jj in range(2):
                pr = 2 * g + jj
                h0, h1 = base + 2 * pr, base + 2 * pr + 1
                ac0, ac1 = _col(a_cum_all, h0), _col(a_cum_all, h1)
                at0, at1 = _col(a_tot_all, h0), _col(a_tot_all, h1)
                dt_pair = jnp.where(first, _col(dt_all, h0), _col(dt_all, h1))
                acum_pair = jnp.where(first, ac0, ac1)
                atot_pair = jnp.where(first[0:1], at0, at1)
                xd = xs[pr] * dt_pair
                st = _mxu(xd * jnp.exp(atot_pair - acum_pair), bm, _TN)
                new.append(prev[pr] * jnp.where(first_row, jnp.exp(at0), jnp.exp(at1)) + st)
                y0 = _mxu(cb * lmat(ac0), xd)
                y1 = _mxu(cb * lmat(ac1), xd)
                y_off = _mxu(cm, prev[pr], _NT) * jnp.exp(acum_pair)
                ys.append(jnp.where(first, y0, y1) + y_off)
        return jnp.concatenate(ys, axis=1), new

    return fn


def _make_ret_chunk(direction):
    rev = direction == 1
    base = 4 * direction

    def fn(rows, shared, prev):
        qs, ks, vs = rows[0:4], rows[4:8], rows[8:12]
        lg_all = -jnp.exp(shared[0])
        ln = qs[0].shape[0]
        pos = lax.broadcasted_iota(jnp.int32, (ln, 1), 0).astype(F32)
        r_i = lax.broadcasted_iota(jnp.int32, (ln, ln), 0)
        c_i = lax.broadcasted_iota(jnp.int32, (ln, ln), 1)
        diff = ((c_i - r_i) if rev else (r_i - c_i))
        mask = diff >= 0
        dpos = jnp.maximum(diff, 0).astype(F32)
        k_pow = pos if rev else (ln - 1.0 - pos)
        q_pow = (ln - pos) if rev else (pos + 1.0)
        ys, new = [], []
        for h in range(RET_HEADS):
            lg = _col(lg_all, base + h)
            dmat = jnp.where(mask, jnp.exp(dpos * lg), 0.0)
            st = _mxu(ks[h] * jnp.exp(k_pow * lg), vs[h], _TN)
            new.append(prev[h] * jnp.exp(ln * lg) + st)
            s = _mxu(qs[h], ks[h], _NT) * dmat
            ys.append(_mxu(s, vs[h]) + _mxu(qs[h], prev[h]) * jnp.exp(q_pow * lg))
        return jnp.concatenate(ys, axis=1), new

    return fn


def _conv_pre(x, w, b, t_idx, n_rows, m_ctx):
    is_start = (t_idx == 0) | (t_idx == m_ctx)
    is_end = (t_idx == m_ctx - 1) | (t_idx == n_rows - 1)
    xp = jnp.where(is_start, 0.0, pltpu.roll(x, 1, axis=0))
    xn = jnp.where(is_end, 0.0, pltpu.roll(x, n_rows - 1, axis=0))
    return w[0:1] * xp + w[1:2] * x + w[2:3] * xn + b, xp, xn, is_start, is_end


def conv_fwd(x, conv_w, conv_b, m_ctx):
    n_rows, width = x.arr.shape[0], x.width
    c0 = x.off // LANES

    def body(x_ref, w_ref, b_ref, o_ref):
        t_idx = lax.broadcasted_iota(jnp.int32, (n_rows, 1), 0)
        pre = _conv_pre(x_ref[...], w_ref[...], b_ref[...], t_idx, n_rows, m_ctx)[0]
        o_ref[...] = pre * jax.nn.sigmoid(pre)

    return pl.pallas_call(
        body, name="conv_fwd", grid=(width // LANES,),
        in_specs=[pl.BlockSpec((n_rows, LANES), lambda c: (0, c0 + c)),
                  pl.BlockSpec((3, LANES), lambda c: (0, c)), pl.BlockSpec((1, LANES), lambda c: (0, c))],
        out_specs=pl.BlockSpec((n_rows, LANES), lambda c: (0, c)),
        out_shape=jax.ShapeDtypeStruct((n_rows, width), F32),
        compiler_params=_cp(("parallel",)))(x.arr, conv_w, conv_b)


def conv_bwd(x, conv_w, conv_b, dy, dxs_extra, m_ctx):
    n_rows, width = x.arr.shape[0], x.width
    c0 = x.off // LANES
    n_extra = dxs_extra.shape[1] // LANES

    def body(x_ref, w_ref, b_ref, dy_ref, ex_ref, dx_ref, dw_ref, db_ref):
        c = pl.program_id(0)
        t_idx = lax.broadcasted_iota(jnp.int32, (n_rows, 1), 0)
        w = w_ref[...]
        pre, xp, xn, is_start, is_end = _conv_pre(x_ref[...], w, b_ref[...], t_idx, n_rows, m_ctx)
        sg = jax.nn.sigmoid(pre)
        dyv = dy_ref[...] + jnp.where(c < n_extra, ex_ref[...], 0.0)
        dpre = dyv * (sg * (1.0 + pre * (1.0 - sg)))
        d_next = jnp.where(is_end, 0.0, pltpu.roll(dpre, n_rows - 1, axis=0))
        d_prev = jnp.where(is_start, 0.0, pltpu.roll(dpre, 1, axis=0))
        dx_ref[...] = (w[1:2] * dpre + w[0:1] * d_next + w[2:3] * d_prev).astype(dx_ref.dtype)
        dw_ref[...] = jnp.concatenate([jnp.sum(dpre * xp, axis=0, keepdims=True),
                                       jnp.sum(dpre * x_ref[...], axis=0, keepdims=True),
                                       jnp.sum(dpre * xn, axis=0, keepdims=True)], axis=0)
        db_ref[...] = jnp.sum(dpre, axis=0, keepdims=True)

    return pl.pallas_call(
        body, name="conv_bwd", grid=(width // LANES,),
        in_specs=[pl.BlockSpec((n_rows, LANES), lambda c: (0, c0 + c)),
                  pl.BlockSpec((3, LANES), lambda c: (0, c)), pl.BlockSpec((1, LANES), lambda c: (0, c)),
                  pl.BlockSpec((n_rows, LANES), lambda c: (0, c)),
                  pl.BlockSpec((n_rows, LANES), lambda c: (0, jnp.minimum(c, n_extra - 1)))],
        out_specs=[pl.BlockSpec((n_rows, LANES), lambda c: (0, c)),
                   pl.BlockSpec((3, LANES), lambda c: (0, c)), pl.BlockSpec((1, LANES), lambda c: (0, c))],
        out_shape=[jax.ShapeDtypeStruct((n_rows, width), CDT), jax.ShapeDtypeStruct((3, width), F32),
                   jax.ShapeDtypeStruct((1, width), F32)],
        compiler_params=_cp(("parallel",)))(x.arr, conv_w, conv_b, dy, dxs_extra)


def loss_head(h, target, final_w, m_ctx):
    n_rows, d = h.shape
    tm = min(ROW_TILE, n_rows)
    nb_ctx = m_ctx // tm

    def f(hb, w, tgt):
        err = _rms(hb, w) - tgt
        return 0.5 * jnp.sum(jnp.mean(err * err, axis=-1))

    def body(h_ref, t_ref, w_ref, loss_ref, dh_ref, dw_ref):
        i = pl.program_id(0)

        @pl.when(i < nb_ctx)
        def _():
            dh_ref[...] = jnp.zeros_like(dh_ref)

        @pl.when(i == 0)
        def _():
            loss_ref[...] = jnp.zeros_like(loss_ref)
            dw_ref[...] = jnp.zeros_like(dw_ref)

        @pl.when(i >= nb_ctx)
        def _():
            val, vjp = jax.vjp(lambda hb, w: f(hb, w, t_ref[...]), h_ref[...], w_ref[...])
            dh, dw = vjp(jnp.ones((), F32))
            dh_ref[...] = dh
            dw_ref[...] += dw
            loss_ref[...] += jnp.broadcast_to(val, loss_ref.shape)

    return pl.pallas_call(
        body, name="loss_head", grid=(n_rows // tm,),
        in_specs=[pl.BlockSpec((tm, d), lambda i: (i, 0)),
                  pl.BlockSpec((tm, d), lambda i: (jnp.maximum(i - nb_ctx, 0), 0)), _const_spec((1, d))],
        out_specs=[_const_spec((1, LANES)), pl.BlockSpec((tm, d), lambda i: (i, 0)), _const_spec((1, d))],
        out_shape=[jax.ShapeDtypeStruct((1, LANES), F32), jax.ShapeDtypeStruct((n_rows, d), F32),
                   jax.ShapeDtypeStruct((1, d), F32)],
        compiler_params=_cp(("arbitrary",)))(h, target, final_w)


def adamw(name, w, m, v, g_parts):
    rows, cols = w.shape
    tr = _pick(rows, (256, 128, 64, 32, 16, 8))
    npart = len(g_parts)
    c1 = 1.0 - ADAM_B1 ** ADAM_STEP
    c2 = 1.0 - ADAM_B2 ** ADAM_STEP

    def body(*refs):
        w_ref, m_ref, v_ref = refs[:3]
        g = refs[3][...].astype(F32)
        for r in refs[4:3 + npart]:
            g = g + r[...].astype(F32)
        g_ref, d_ref, nm_ref, nv_ref = refs[3 + npart:]
        nm = ADAM_B1 * m_ref[...] + (1.0 - ADAM_B1) * g
        nv = ADAM_B2 * v_ref[...] + (1.0 - ADAM_B2) * (g * g)
        g_ref[...] = g
        nm_ref[...] = nm
        nv_ref[...] = nv
        d_ref[...] = -ADAM_LR * ((nm / c1) / (jnp.sqrt(nv / c2) + ADAM_EPS) + ADAM_WD * w_ref[...])

    spec = pl.BlockSpec((tr, cols), lambda i: (i, 0))
    return pl.pallas_call(
        body, name=name, grid=(rows // tr,), in_specs=[spec] * (3 + npart), out_specs=[spec] * 4,
        out_shape=[jax.ShapeDtypeStruct((rows, cols), F32)] * 4, compiler_params=_cp(("parallel",)))(w, m, v, *g_parts)


def sum_parts(name, parts):
    npart, rows, cols = parts.shape
    tr = _pick(rows, (512, 256, 8))

    def body(p_ref, o_ref):
        acc = p_ref[0].astype(F32)
        for k in range(1, npart):
            acc = acc + p_ref[k].astype(F32)
        o_ref[...] = acc

    return pl.pallas_call(
        body, name=name, grid=(rows // tr,), in_specs=[pl.BlockSpec((npart, tr, cols), lambda i: (0, i, 0))],
        out_specs=pl.BlockSpec((tr, cols), lambda i: (i, 0)), out_shape=jax.ShapeDtypeStruct((rows, cols), F32),
        compiler_params=_cp(("parallel",)))(parts)


MESH = pl.DeviceIdType.MESH
_HBM = pl.BlockSpec(memory_space=pl.ANY)


def _chip_peers():
    x, y, c = lax.axis_index("x"), lax.axis_index("y"), lax.axis_index("c")
    return x, y, c, [(1 - x, y), (x, 1 - y), (1 - x, 1 - y)]


def gather_chips(name, shard):
    rows = shard.shape[0]
    half = rows // 2
    nch = _pick(half, (GATHER_CHUNKS, 1))
    per = half // nch
    n_ici = 3 * nch

    def body(x_ref, out_ref, send_sems, recv_sems, local_sem):
        x, y, c, peers = _chip_peers()
        me = 2 * x + y
        sib = (x, y, 1 - c)
        mine = pltpu.make_async_copy(x_ref, out_ref.at[me], local_sem)
        mine.start()

        def over_ici(k, q, chip, to, src=None):
            rws = pl.ds(c * half + q * per, per)
            return pltpu.make_async_remote_copy(
                src_ref=x_ref.at[rws] if src is None else src, dst_ref=out_ref.at[chip, rws],
                send_sem=send_sems.at[k * nch + q], recv_sem=recv_sems.at[k * nch + q], device_id=to,
                device_id_type=MESH)

        def to_sibling(k, q, chip, half_of):
            rws = pl.ds(half_of * half + q * per, per)
            return pltpu.make_async_remote_copy(
                src_ref=out_ref.at[chip, rws], dst_ref=out_ref.at[chip, rws], send_sem=send_sems.at[n_ici + k * nch + q],
                recv_sem=recv_sems.at[n_ici + k * nch + q], device_id=sib, device_id_type=MESH)

        sends = []
        for q in range(nch):
            for k, (px, py) in enumerate(peers):
                cp = over_ici(k, q, me, (px, py, c))
                cp.start()
                sends.append(cp)
        for q in range(nch):
            for k, (px, py) in enumerate(peers):
                over_ici(k, q, 2 * px + py, (px, py, c)).wait_recv()
                cp = to_sibling(k, q, 2 * px + py, c)
                cp.start()
                sends.append(cp)
        for q in range(nch):
            for k, (px, py) in enumerate(peers):
                to_sibling(k, q, 2 * px + py, 1 - c).wait_recv()
        for cp in sends:
            cp.wait_send()
        mine.wait()

    return pl.pallas_call(
        body, name=name, in_specs=[_HBM], out_specs=_HBM,
        out_shape=jax.ShapeDtypeStruct((4,) + shard.shape, shard.dtype),
        scratch_shapes=[pltpu.SemaphoreType.DMA((2 * n_ici,)), pltpu.SemaphoreType.DMA((2 * n_ici,)),
                        pltpu.SemaphoreType.DMA],
        )(shard)


def swap_halves(name, pieces):
    _, rows, cols = pieces.shape
    half = rows // 2
    nch = _pick(half, (GATHER_CHUNKS, 1))
    per = half // nch

    def body(p_ref, mine_ref, got_ref, send_sems, recv_sems, local_sem):
        x, y, c = lax.axis_index("x"), lax.axis_index("y"), lax.axis_index("c")
        mine = pltpu.make_async_copy(p_ref.at[:, pl.ds(c * half, half)], mine_ref, local_sem)
        mine.start()

        def chunk(j, q):
            return pltpu.make_async_remote_copy(
                src_ref=p_ref.at[j, pl.ds((1 - c) * half + q * per, per)], dst_ref=got_ref.at[j, pl.ds(q * per, per)],
                send_sem=send_sems.at[j * nch + q], recv_sem=recv_sems.at[j * nch + q], device_id=(x, y, 1 - c),
                device_id_type=MESH)

        sends = [chunk(j, q) for j in range(4) for q in range(nch)]
        for cp in sends:
            cp.start()
        for cp in sends:
            cp.wait_recv()
        for cp in sends:
            cp.wait_send()
        mine.wait()

    out = jax.ShapeDtypeStruct((4, half, cols), pieces.dtype)
    return pl.pallas_call(
        body, name=name, in_specs=[_HBM], out_specs=[_HBM, _HBM], out_shape=[out, out],
        scratch_shapes=[pltpu.SemaphoreType.DMA((4 * nch,)), pltpu.SemaphoreType.DMA((4 * nch,)),
                        pltpu.SemaphoreType.DMA],
        )(pieces)


def add_cast(name, a, b):
    npart, rows, cols = a.shape
    tr = _pick(rows, (512, 256, 16))

    def body(a_ref, b_ref, o_ref):
        o_ref[...] = (a_ref[...].astype(F32) + b_ref[...].astype(F32)).astype(o_ref.dtype)

    spec = pl.BlockSpec((npart, tr, cols), lambda i: (0, i, 0))
    return pl.pallas_call(body, name=name, grid=(rows // tr,), in_specs=[spec, spec], out_specs=spec,
                          out_shape=jax.ShapeDtypeStruct(a.shape, CDT), compiler_params=_cp(("parallel",)))(a, b)


def scatter_chips(name, pieces):
    def body(p_ref, out_ref, send_sems, recv_sems, local_sem):
        x, y, c, peers = _chip_peers()
        me = 2 * x + y
        mine = pltpu.make_async_copy(p_ref.at[me], out_ref.at[me], local_sem)
        mine.start()
        sends = []
        for k, (px, py) in enumerate(peers):
            cp = pltpu.make_async_remote_copy(src_ref=p_ref.at[2 * px + py], dst_ref=out_ref.at[me],
                                              send_sem=send_sems.at[k], recv_sem=recv_sems.at[k],
                                              device_id=(px, py, c), device_id_type=MESH)
            cp.start()
            sends.append(cp)
        for k, (px, py) in enumerate(peers):
            pltpu.make_async_remote_copy(src_ref=p_ref.at[me], dst_ref=out_ref.at[2 * px + py],
                                         send_sem=send_sems.at[k], recv_sem=recv_sems.at[k], device_id=(px, py, c),
                                         device_id_type=MESH).wait_recv()
        for cp in sends:
            cp.wait_send()
        mine.wait()

    return pl.pallas_call(
        body, name=name, in_specs=[_HBM], out_specs=_HBM, out_shape=jax.ShapeDtypeStruct(pieces.shape, pieces.dtype),
        scratch_shapes=[pltpu.SemaphoreType.DMA((3,)), pltpu.SemaphoreType.DMA((3,)), pltpu.SemaphoreType.DMA],
        )(pieces)


def swap_cores(name, part):
    rows = part.shape[0]
    nch = _pick(rows, (SWAP_CHUNKS, 1))
    per = rows // nch

    def body(p_ref, out_ref, send_sems, recv_sems, local_sem):
        x, y, c = lax.axis_index("x"), lax.axis_index("y"), lax.axis_index("c")
        mine = pltpu.make_async_copy(p_ref, out_ref.at[c], local_sem)
        mine.start()

        def chunk(k, slot):
            rows_k = pl.ds(k * per, per)
            return pltpu.make_async_remote_copy(src_ref=p_ref.at[rows_k], dst_ref=out_ref.at[slot, rows_k],
                                                send_sem=send_sems.at[k], recv_sem=recv_sems.at[k],
                                                device_id=(x, y, 1 - c), device_id_type=MESH)

        sends = [chunk(k, c) for k in range(nch)]
        for cp in sends:
            cp.start()
        for k in range(nch):
            chunk(k, 1 - c).wait_recv()
        for cp in sends:
            cp.wait_send()
        mine.wait()

    return pl.pallas_call(
        body, name=name, in_specs=[_HBM], out_specs=_HBM, out_shape=jax.ShapeDtypeStruct((2,) + part.shape, part.dtype),
        scratch_shapes=[pltpu.SemaphoreType.DMA((nch,)), pltpu.SemaphoreType.DMA((nch,)), pltpu.SemaphoreType.DMA],
        )(part)


def allreduce_small(name, buf):
    rows = buf.shape[0]

    def body(x_ref, out_ref, gath, send_sems, recv_sems):
        x, y, c = lax.axis_index("x"), lax.axis_index("y"), lax.axis_index("c")
        me = 4 * x + 2 * y + c
        masks = [(k >> 2 & 1, k >> 1 & 1, k & 1) for k in range(1, 8)]

        def flip(v, bit):
            return 1 - v if bit else v

        sends = []
        for k, (bx, by, bc) in enumerate(masks):
            cp = pltpu.make_async_remote_copy(src_ref=x_ref, dst_ref=gath.at[me], send_sem=send_sems.at[k],
                                              recv_sem=recv_sems.at[k],
                                              device_id=(flip(x, bx), flip(y, by), flip(c, bc)), device_id_type=MESH)
            cp.start()
            sends.append(cp)
        gath[me] = x_ref[...]
        for k, (bx, by, bc) in enumerate(masks):
            px, py, pc = flip(x, bx), flip(y, by), flip(c, bc)
            pltpu.make_async_remote_copy(src_ref=x_ref, dst_ref=gath.at[4 * px + 2 * py + pc],
                                         send_sem=send_sems.at[k], recv_sem=recv_sems.at[k],
                                         device_id=(px, py, pc), device_id_type=MESH).wait_recv()
        for cp in sends:
            cp.wait_send()
        acc = gath[0]
        for d in range(1, 8):
            acc = acc + gath[d]
        out_ref[...] = acc

    return pl.pallas_call(
        body, name=name, in_specs=[pl.BlockSpec(memory_space=pltpu.VMEM)],
        out_specs=pl.BlockSpec(memory_space=pltpu.VMEM), out_shape=jax.ShapeDtypeStruct(buf.shape, F32),
        scratch_shapes=[pltpu.VMEM((8, rows, LANES), F32), pltpu.SemaphoreType.DMA((7,)),
                        pltpu.SemaphoreType.DMA((7,))],
        )(buf)


def _pack_flat(arrs, dtype, width, row_mult=8):
    flat = jnp.concatenate([a.reshape(-1).astype(dtype) for a in arrs])
    pad = (-flat.shape[0]) % (row_mult * width)
    if pad:
        flat = jnp.concatenate([flat, jnp.zeros((pad,), dtype)])
    return flat.reshape(-1, width)


def _unpack_flat(buf, shapes):
    flat = buf.reshape(-1)
    out, off = [], 0
    for s in shapes:
        n = math.prod(s)
        out.append(flat[off:off + n].reshape(s))
        off += n
    return out


def _in_to_padded(w):
    parts = []
    for name in IN_NEW_ORDER:
        _, width, o_off, o_w = IN_LAYOUT[name]
        parts.append(w[..., o_off:o_off + o_w])
        if o_w < width:
            parts.append(jnp.zeros(w.shape[:-1] + (width - o_w,), w.dtype))
    used = sum(IN_LAYOUT[n][1] for n in IN_NEW_ORDER)
    parts.append(jnp.zeros(w.shape[:-1] + (IN_PAD - used,), w.dtype))
    return jnp.concatenate(parts, axis=-1)


def _in_from_padded(g):
    parts = []
    for name in IN_ORIG_ORDER:
        off, _, _, o_w = IN_LAYOUT[name]
        parts.append(g[..., off:off + o_w])
    return jnp.concatenate(parts, axis=-1)


def _pcol(p, name):
    off, width, _, _ = IN_LAYOUT[name]
    return Cols(p, off, width)


def _lane_pad(v, width=LANES):
    v = v.reshape(-1)
    return jnp.concatenate([v, jnp.zeros((width - v.shape[0],), v.dtype)]).reshape(1, width)


def _f_norm_mod(h, sh, sc, w):
    return (_rms(h, w) * (1.0 + sc) + sh,)


def _f_norm_mod_thru(h, sh, sc, w):
    return h, _rms(h, w) * (1.0 + sc) + sh


def _f_attn_prep(qraw, kraw, vraw, cq, sq, ck, sk, qw, kw, gq, gk):
    q = qraw * lax.rsqrt(_group_mean(qraw * qraw, gq) + NORM_EPS) * qw
    q = _rope32(q, cq, sq) * (ATTN_HD ** -0.5)
    k = kraw * lax.rsqrt(_group_mean(kraw * kraw, gk) + NORM_EPS) * kw
    return q, _rope32(k, ck, sk), vraw


def _f_ssd_finish(yf, yb, xs, z, d_exp, nw):
    y = (yf + yb + d_exp * xs) * (z * jax.nn.sigmoid(z))
    return (_rms(y, nw),)


def _f_ret_prep(rq, rk, cos_full, sin_signed):
    return _rope64(rq, cos_full, sin_signed), _rope64(rk, cos_full, sin_signed) * (RET_DK ** -0.5)


def _f_ret_finish(yf, yb, g, gw):
    y = yf + yb
    outs = []
    for h in range(RET_HEADS):
        yh = y[:, h * RET_DK:(h + 1) * RET_DK]
        yc = yh - jnp.mean(yh, axis=-1, keepdims=True)
        outs.append(yc * lax.rsqrt(jnp.mean(yc * yc, axis=-1, keepdims=True) + NORM_EPS))
    return (jnp.concatenate(outs, axis=1) * gw * (g * jax.nn.sigmoid(g)),)


def _f_merge(p0, p1, p2, g0, g1, g2):
    return (jax.nn.sigmoid(g0) * p0 + jax.nn.sigmoid(g1) * p1 + jax.nn.sigmoid(g2) * p2,)


def _f_mid(h, mix, g1, sh2, sc2, w2):
    h_mid = h + g1 * mix
    return h_mid, _rms(h_mid, w2) * (1.0 + sc2) + sh2


def _f_sqrelu(a):
    r = jnp.maximum(a, 0.0)
    return (r * r,)


def _f_residual(h_mid, o, g2):
    return (h_mid + g2 * o,)


def _f_silu(x):
    return (x * jax.nn.sigmoid(x),)


def _f_bias(x, b):
    return (x + b,)


def _ssd_rows(xbc, p):
    rows = [Cols(xbc, LANES * k, LANES) for k in range(4)]
    rows += [Cols(xbc, 512 + LANES * g, LANES) for g in range(2)]
    rows += [Cols(xbc, 768 + LANES * g, LANES) for g in range(2)]
    return rows + [_pcol(p, "dt")]


def _ret_rows(rq, rk, p):
    off_v = IN_LAYOUT["rv"][0]
    return ([Cols(rq, LANES * h, LANES) for h in range(4)] + [Cols(rk, LANES * h, LANES) for h in range(4)]
            + [Cols(p, off_v + LANES * h, LANES) for h in range(4)])


def layer_fwd(li, h, mod, lw, tabs, m_ctx):
    t = h.shape[0]
    nb = m_ctx // min(ROW_TILE, t)
    sh1, sc1, g1, sh2, sc2, g2 = mod
    nm = lambda s: f"l{li}_{s}"
    sv = {}
    (u,) = rowwise_fwd(nm("norm1"), _f_norm_mod, [h], [sh1, sc1], [lw["norm1_w"]], [(D_MODEL, CDT)], t, nb)
    p = mm(nm("in_proj"), u, lw["w_in"], F32)
    q, k, v = rowwise_fwd(
        nm("attn_prep"), _f_attn_prep,
        [_pcol(p, "q"), _pcol(p, "k"), _pcol(p, "v"), tabs["cq"], tabs["sq"], tabs["ck"], tabs["sk"]], [],
        [lw["qw"], lw["kw"], tabs["gq"], tabs["gk"]], [(512, CDT), (128, CDT), (128, CDT)], t, nb)
    tq = min(ATTN_TQ, m_ctx)
    qT, kk, vv = _stack_heads_t(q, tq), _split_kv(k), _split_kv(v)
    oT, lse = attn_fwd(nm("attn"), qT, kk, vv.transpose(0, 2, 1), m_ctx)
    attn_o = _unstack_heads_t(oT)

    xbc = conv_fwd(_pcol(p, "xbc"), lw["conv_w"], lw["conv_b"], m_ctx)
    ssd_sh = [lw["dt_bias"], lw["a_log"]]
    yf, sf = scan_fwd(nm("ssd_f"), _make_ssd_chunk(0), _ssd_rows(xbc, p), ssd_sh, 4, 512, t, m_ctx, False)
    yb, sb = scan_fwd(nm("ssd_b"), _make_ssd_chunk(1), _ssd_rows(xbc, p), ssd_sh, 4, 512, t, m_ctx, True)
    (ssd_o,) = rowwise_fwd(nm("ssd_fin"), _f_ssd_finish, [yf, yb, Cols(xbc, 0, 512), _pcol(p, "z")], [],
                           [lw["d_exp"], lw["ssd_nw"]], [(512, CDT)], t, nb)

    rq, rk = rowwise_fwd(nm("ret_prep"), _f_ret_prep, [_pcol(p, "rq"), _pcol(p, "rk"), tabs["rc"], tabs["rs"]],
                         [], [], [(512, F32), (512, F32)], t, nb)
    rf, rsf = scan_fwd(nm("ret_f"), _make_ret_chunk(0), _ret_rows(rq, rk, p), [lw["ret_lg"]], 4, 512, t, m_ctx, False)
    rb, rsb = scan_fwd(nm("ret_b"), _make_ret_chunk(1), _ret_rows(rq, rk, p), [lw["ret_lg"]], 4, 512, t, m_ctx, True)
    (ret_o,) = rowwise_fwd(nm("ret_fin"), _f_ret_finish, [rf, rb, _pcol(p, "rg")], [], [lw["ret_gw"]],
                           [(512, CDT)], t, nb)

    pbs = [mm(nm(f"branch{b}"), br, lw["w_branch"][b], F32) for b, br in enumerate((attn_o, ssd_o, ret_o))]
    gl = [Cols(p, 1024 * b, 1024) for b in range(3)]
    (merged,) = rowwise_fwd(nm("merge"), _f_merge, pbs + gl, [], [], [(D_MODEL, CDT)], t, nb)
    mix = mm(nm("out_proj"), merged, lw["w_out"], F32)
    h_mid, vv2 = rowwise_fwd(nm("mid"), _f_mid, [h, mix], [g1, sh2, sc2], [lw["norm2_w"]],
                             [(D_MODEL, F32), (D_MODEL, CDT)], t, nb)
    a = mm(nm("mlp1"), vv2, lw["w_mlp1"], F32)
    (hh,) = rowwise_fwd(nm("sqrelu"), _f_sqrelu, [a], [], [], [(a.shape[1], CDT)], t, nb)
    o = mm(nm("mlp2"), hh, lw["w_mlp2"], F32)
    (h_out,) = rowwise_fwd(nm("resid"), _f_residual, [h_mid, o], [g2], [], [(D_MODEL, F32)], t, nb)
    sv.update(h=h, u=u, p=p, q=q, qT=qT, kk=kk, vv=vv, oT=oT, lse=lse, attn_o=attn_o, xbc=xbc, yf=yf, yb=yb,
              sf=sf, sb=sb, ssd_o=ssd_o, rq=rq, rk=rk, rf=rf, rb=rb, rsf=rsf, rsb=rsb, ret_o=ret_o, pbs=pbs,
              merged=merged, mix=mix, h_mid=h_mid, v=vv2, a=a, hh=hh, o=o)
    return h_out, sv


def layer_bwd(li, dh_out, sv, mod, lw, lwt, tabs, m_ctx):
    t = dh_out.shape[0]
    nb = m_ctx // min(ROW_TILE, t)
    sh1, sc1, g1, sh2, sc2, g2 = mod
    nm = lambda s: f"l{li}_{s}_bwd"
    gw = {}
    p = sv["p"]
    (do,), (dg2,), _ = rowwise_bwd(nm("resid"), _f_residual, [sv["h_mid"], sv["o"]], [g2], [], [dh_out],
                                   [False, True], [], [CDT], t, nb)
    dhh = mm(nm("mlp2_dx"), do, lwt["w_mlp2"], F32)
    gw["w_mlp2"] = mm_tn(nm("mlp2_dw"), sv["hh"], do)
    (da,), _, _ = rowwise_bwd(nm("sqrelu"), _f_sqrelu, [sv["a"]], [], [], [dhh], [True], [], [CDT], t, nb)
    dv = mm(nm("mlp1_dx"), da, lwt["w_mlp1"], F32)
    gw["w_mlp1"] = mm_tn(nm("mlp1_dw"), sv["v"], da)
    (dh_a, dmix), (dg1, dsh2, dsc2), (gw["norm2_w"],) = rowwise_bwd(
        nm("mid"), _f_mid, [sv["h"], sv["mix"]], [g1, sh2, sc2], [lw["norm2_w"]], [dh_out, dv],
        [True, True], [True], [F32, CDT], t, nb)
    dmerged = mm(nm("out_dx"), dmix, lwt["w_out"], F32)
    gw["w_out"] = mm_tn(nm("out_dw"), sv["merged"], dmix)
    gl = [Cols(p, 1024 * b, 1024) for b in range(3)]
    dmg, _, _ = rowwise_bwd(nm("merge"), _f_merge, sv["pbs"] + gl, [], [], [dmerged], [True] * 6, [], [CDT] * 6,
                            t, nb)
    dpb, dgl = dmg[:3], dmg[3:]
    brs = (sv["attn_o"], sv["ssd_o"], sv["ret_o"])
    d_attn_o = mm(nm("branch0_dx"), dpb[0], lwt["w_branch"][0], CDT)
    d_ssd_o = mm(nm("branch1_dx"), dpb[1], lwt["w_branch"][1], F32)
    d_ret_o = mm(nm("branch2_dx"), dpb[2], lwt["w_branch"][2], F32)
    gw["w_branch"] = jnp.stack([mm_tn(nm(f"branch{b}_dw"), brs[b], dpb[b]) for b in range(3)])
    tq = min(ATTN_TQ, m_ctx)
    dq_t, dk_s, dv_s = attn_bwd(nm("attn"), sv["qT"], _stack_heads(sv["q"], tq), _stack_heads_t(d_attn_o, tq),
                                _stack_heads(d_attn_o, tq), sv["oT"], sv["lse"], sv["kk"],
                                sv["kk"].transpose(0, 2, 1), sv["vv"], m_ctx)
    (dq_raw, dk_raw, dv_raw), _, (gw["qw"], gw["kw"]) = rowwise_bwd(
        nm("attn_prep"), _f_attn_prep,
        [_pcol(p, "q"), _pcol(p, "k"), _pcol(p, "v"), tabs["cq"], tabs["sq"], tabs["ck"], tabs["sk"]], [],
        [lw["qw"], lw["kw"], tabs["gq"], tabs["gk"]],
        [_unstack_heads_t(dq_t), _merge_kv(dk_s), _merge_kv(dv_s)],
        [True, True, True, False, False, False, False], [True, True, False, False], [CDT] * 3, t, nb)
    (dy_ssd, dxs_fin, dz), _, (gw["d_exp"], gw["ssd_nw"]) = rowwise_bwd(
        nm("ssd_fin"), _f_ssd_finish, [sv["yf"], sv["yb"], Cols(sv["xbc"], 0, 512), _pcol(p, "z")], [],
        [lw["d_exp"], lw["ssd_nw"]], [d_ssd_o], [True, False, True, True], [True, True], [F32, F32, CDT], t, nb)
    ssd_sh = [lw["dt_bias"], lw["a_log"]]
    post_ssd = lambda d: [jnp.concatenate(d[0:8], axis=1), d[8]]
    (dxbc_f, ddt_f), dsh_f = scan_bwd(nm("ssd_f"), _make_ssd_chunk(0), _ssd_rows(sv["xbc"], p), ssd_sh, sv["sf"],
                                      dy_ssd, post_ssd, [(1024, F32), (LANES, F32)], [], 4, t, m_ctx, False)
    (dxbc, ddt), dsh_b = scan_bwd(nm("ssd_b"), _make_ssd_chunk(1), _ssd_rows(sv["xbc"], p), ssd_sh, sv["sb"],
                                  dy_ssd, post_ssd, [(1024, F32), (LANES, CDT)], [dxbc_f, ddt_f], 4, t, m_ctx, True)
    gw["dt_bias"] = dsh_f[0] + dsh_b[0]
    gw["a_log"] = dsh_f[1] + dsh_b[1]
    dxbc_raw, gw["conv_w"], gw["conv_b"] = conv_bwd(_pcol(p, "xbc"), lw["conv_w"], lw["conv_b"], dxbc, dxs_fin, m_ctx)
    (dy_ret, drg), _, (gw["ret_gw"],) = rowwise_bwd(
        nm("ret_fin"), _f_ret_finish, [sv["rf"], sv["rb"], _pcol(p, "rg")], [], [lw["ret_gw"]], [d_ret_o],
        [True, False, True], [True], [F32, CDT], t, nb)
    post_ret = lambda d: [jnp.concatenate(d[0:4], axis=1), jnp.concatenate(d[4:8], axis=1),
                          jnp.concatenate(d[8:12], axis=1)]
    rrows = _ret_rows(sv["rq"], sv["rk"], p)
    r3 = [(512, F32)] * 3
    part, dlg_f = scan_bwd(nm("ret_f"), _make_ret_chunk(0), rrows, [lw["ret_lg"]], sv["rsf"], dy_ret, post_ret, r3,
                           [], 4, t, m_ctx, False)
    (drq_r, drk_r, drv), dlg_b = scan_bwd(nm("ret_b"), _make_ret_chunk(1), rrows, [lw["ret_lg"]], sv["rsb"], dy_ret,
                                          post_ret, [(512, F32), (512, F32), (512, CDT)], part, 4, t, m_ctx, True)
    gw["ret_lg"] = dlg_f[0] + dlg_b[0]
    (drq, drk), _, _ = rowwise_bwd(nm("ret_prep"), _f_ret_prep,
                                   [_pcol(p, "rq"), _pcol(p, "rk"), tabs["rc"], tabs["rs"]], [], [], [drq_r, drk_r],
                                   [True, True, False, False], [], [CDT, CDT], t, nb)
    pieces = {"gates": None, "xbc": dxbc_raw, "q": dq_raw, "z": dz, "rq": drq, "rk": drk, "rv": drv, "rg": drg,
              "k": dk_raw, "v": dv_raw, "dt": ddt}
    cols = list(dgl) + [pieces[n] for n in IN_NEW_ORDER[1:]]
    used = sum(c.shape[1] for c in cols)
    cols.append(jnp.zeros((t, IN_PAD - used), CDT))
    dp = jnp.concatenate(cols, axis=1)
    du = mm(nm("in_dx"), dp, lwt["w_in"], F32)
    gw["w_in"] = mm_tn(nm("in_dw"), sv["u"], dp)
    (dh_in,), (dsh1, dsc1), (gw["norm1_w"],) = rowwise_bwd(
        nm("norm1"), _f_norm_mod_thru, [sv["h"]], [sh1, sc1], [lw["norm1_w"]], [dh_a, du], [True], [True], [F32],
        t, nb)
    return dh_in, [dsh1, dsc1, dg1, dsh2, dsc2, dg2], gw


def _rope_tables(n_lat, m_ctx):
    rows = n_lat // GRID_W
    row = jnp.repeat(jnp.arange(rows, dtype=F32), GRID_W)
    col = jnp.tile(jnp.arange(GRID_W, dtype=F32), rows)
    nfreq = ATTN_HD // 4
    inv = ROPE_THETA ** (-jnp.arange(nfreq, dtype=F32) / nfreq)
    ang = jnp.concatenate([row[:, None] * inv, col[:, None] * inv], axis=-1)
    cos = jnp.concatenate([jnp.ones((m_ctx, ATTN_HD // 2), F32), jnp.cos(ang)], axis=0)
    sin = jnp.concatenate([jnp.zeros((m_ctx, ATTN_HD // 2), F32), jnp.sin(ang)], axis=0)
    c64 = jnp.concatenate([cos, cos], axis=1)
    s64 = jnp.concatenate([-sin, sin], axis=1)
    pos = jnp.arange(m_ctx + n_lat, dtype=F32)
    inv_r = ROPE_THETA ** (-jnp.linspace(0.0, 1.0, RET_DK // 2, dtype=F32))
    ang_r = pos[:, None] * inv_r
    rc = jnp.concatenate([jnp.cos(ang_r)] * 2, axis=1)
    rs = jnp.concatenate([-jnp.sin(ang_r), jnp.sin(ang_r)], axis=1)
    return dict(cq=jnp.tile(c64, (1, 8)), sq=jnp.tile(s64, (1, 8)), ck=jnp.tile(c64, (1, 2)), sk=jnp.tile(s64, (1, 2)),
                rc=jnp.tile(rc, (1, 4)), rs=jnp.tile(rs, (1, 4)), gq=_group_matrix(512, ATTN_HD),
                gk=_group_matrix(128, ATTN_HD))


def _layer_weights(full, small, layer):
    lw = dict(
        w_in=_in_to_padded(full["w_in"][layer]), w_branch=full["w_branch"][layer], w_out=full["w_out"][layer],
        w_mlp1=full["w_mlp1"][layer], w_mlp2=full["w_mlp2"][layer],
        norm1_w=small["norm1_w"][layer][None], norm2_w=small["norm2_w"][layer][None],
        qw=jnp.tile(small["attn_q_norm"][layer], 8)[None], kw=jnp.tile(small["attn_k_norm"][layer], 2)[None],
        conv_w=small["ssd_conv_w"][layer], conv_b=small["ssd_conv_b"][layer][None],
        dt_bias=_lane_pad(small["ssd_dt_bias"][layer]), a_log=_lane_pad(small["ssd_a_log"][layer]),
        d_exp=jnp.repeat(small["ssd_d"][layer], SSD_HD)[None], ssd_nw=small["ssd_norm_w"][layer][None],
        ret_lg=_lane_pad(small["ret_log_decay"][layer]), ret_gw=small["ret_gn_w"][layer][None])
    lwt = dict(w_in=lw["w_in"].T, w_branch=jnp.swapaxes(lw["w_branch"], 1, 2), w_out=lw["w_out"].T,
               w_mlp1=lw["w_mlp1"].T, w_mlp2=lw["w_mlp2"].T)
    return lw, lwt


def local_step(x, c, ctx, full, small, loss_target):
    n_lat, d = x.shape
    m_ctx = ctx.shape[0]
    t = n_lat + m_ctx
    depth = small["norm1_w"].shape[0]
    tabs = _rope_tables(n_lat, m_ctx)
    h = jnp.concatenate([ctx, x], axis=0)
    cc = jnp.concatenate([small["c_ctx"][None], c, jnp.zeros((COND_ROWS - 2, d), F32)], axis=0)
    (scc,) = rowwise_fwd("cond_silu", _f_silu, [cc], [], [], [(d, CDT)], COND_ROWS, 0)
    mods, saved, lws = [], [], []
    for layer in range(depth):
        lw, lwt = _layer_weights(full, small, layer)
        mod_raw = mm(f"l{layer}_mod", scc, full["w_mod"][layer], F32)
        (mod8,) = rowwise_fwd(f"l{layer}_mod_bias", _f_bias, [mod_raw], [], [small["b_mod"][layer][None]],
                              [(6 * d, F32)], COND_ROWS, 0)
        mod = [mod8[0:2, k * d:(k + 1) * d].reshape(2, 1, d) for k in range(6)]
        h, sv = layer_fwd(layer, h, mod, lw, tabs, m_ctx)
        mods.append(mod)
        saved.append(sv)
        lws.append((lw, lwt))
    loss, dh, d_final = loss_head(h, loss_target, small["final_norm_w"][None], m_ctx)

    gbig = {k: [None] * depth for k in BIG}
    gs = {k: [None] * depth for k in SMALL if k not in ("c_ctx", "final_norm_w")}
    d_scc = None
    for layer in reversed(range(depth)):
        lw, lwt = lws[layer]
        dh, dmod, gw = layer_bwd(layer, dh, saved[layer], mods[layer], lw, lwt, tabs, m_ctx)
        dmod8 = jnp.concatenate([jnp.concatenate([g_.reshape(2, d) for g_ in dmod], axis=1),
                                 jnp.zeros((COND_ROWS - 2, 6 * d), F32)], axis=0)
        (dmod_c,), _, (db_mod,) = rowwise_bwd(f"l{layer}_mod_bias_bwd", _f_bias, [dmod8], [],
                                              [small["b_mod"][layer][None]], [dmod8], [True], [True], [CDT], COND_ROWS, 0)
        gbig["w_mod"][layer] = mm_tn(f"l{layer}_mod_dw", scc, dmod_c)
        part = mm(f"l{layer}_mod_dx", dmod_c, full["w_mod"][layer].T, F32)
        d_scc = part if d_scc is None else d_scc + part
        gbig["w_in"][layer] = _in_from_padded(gw["w_in"])
        for k in ("w_branch", "w_out", "w_mlp1", "w_mlp2"):
            gbig[k][layer] = gw[k]
        gs["b_mod"][layer] = db_mod.reshape(-1)
        gs["norm1_w"][layer] = gw["norm1_w"].reshape(-1)
        gs["norm2_w"][layer] = gw["norm2_w"].reshape(-1)
        gs["attn_q_norm"][layer] = gw["qw"].reshape(8, ATTN_HD).sum(0)
        gs["attn_k_norm"][layer] = gw["kw"].reshape(2, ATTN_HD).sum(0)
        gs["ssd_conv_w"][layer] = gw["conv_w"]
        gs["ssd_conv_b"][layer] = gw["conv_b"].reshape(-1)
        gs["ssd_dt_bias"][layer] = gw["dt_bias"][0, :16].reshape(2, 8)
        gs["ssd_a_log"][layer] = gw["a_log"][0, :16].reshape(2, 8)
        gs["ssd_d"][layer] = gw["d_exp"].reshape(SSD_HEADS, SSD_HD).sum(1)
        gs["ssd_norm_w"][layer] = gw["ssd_nw"].reshape(-1)
        gs["ret_log_decay"][layer] = gw["ret_lg"][0, :8].reshape(2, 4)
        gs["ret_gn_w"][layer] = gw["ret_gw"].reshape(-1)
    (d_cc,), _, _ = rowwise_bwd("cond_silu_bwd", _f_silu, [cc], [], [], [d_scc], [True], [], [F32], COND_ROWS, 0)
    g_small = {k: jnp.stack(v) for k, v in gs.items()}
    g_small["c_ctx"] = d_cc[0]
    g_small["final_norm_w"] = d_final.reshape(-1)
    g_big = {k: jnp.stack(v) for k, v in gbig.items()}
    return loss, dh[m_ctx:], g_big, g_small


def _shards(a, axis):
    return jnp.split(a, 4, axis=axis)


def kernel(x, c, ctx, c_ctx, w_mod, b_mod, norm1_w, norm2_w, w_in, attn_q_norm, attn_k_norm, ssd_conv_w, ssd_conv_b, ssd_dt_bias, ssd_a_log, ssd_d, ssd_norm_w, ret_log_decay, ret_gn_w, w_branch, w_out, w_mlp1, w_mlp2, final_norm_w, loss_target, m_c_ctx, m_w_mod, m_b_mod, m_norm1_w, m_norm2_w, m_w_in, m_attn_q_norm, m_attn_k_norm, m_ssd_conv_w, m_ssd_conv_b, m_ssd_dt_bias, m_ssd_a_log, m_ssd_d, m_ssd_norm_w, m_ret_log_decay, m_ret_gn_w, m_w_branch, m_w_out, m_w_mlp1, m_w_mlp2, m_final_norm_w, v_c_ctx, v_w_mod, v_b_mod, v_norm1_w, v_norm2_w, v_w_in, v_attn_q_norm, v_attn_k_norm, v_ssd_conv_w, v_ssd_conv_b, v_ssd_dt_bias, v_ssd_a_log, v_ssd_d, v_ssd_norm_w, v_ret_log_decay, v_ret_gn_w, v_w_branch, v_w_out, v_w_mlp1, v_w_mlp2, v_final_norm_w):
    env = dict(locals())
    w_loc = {k: env[k] for k in WEIGHTS}
    m_loc = {k: env["m_" + k] for k in WEIGHTS}
    v_loc = {k: env["v_" + k] for k in WEIGHTS}
    chip = 2 * lax.axis_index("x") + lax.axis_index("y")
    core = lax.axis_index("c")

    big_shapes = [w_loc[k].shape for k in BIG]
    gathered = gather_chips("gather_weights", _pack_flat([w_loc[k] for k in BIG], CDT, 1024, PACK_ROWS))
    per_chip = [_unpack_flat(gathered[j], big_shapes) for j in range(4)]
    full = {k: jnp.concatenate([per_chip[j][i] for j in range(4)], axis=BIG_AXIS[k]) for i, k in enumerate(BIG)}

    cw = w_loc["ssd_conv_w"]
    cw_w = cw.shape[-1]
    placed = lax.dynamic_update_slice(jnp.zeros(cw.shape[:-1] + (4 * cw_w,), F32),
                                      cw * (core == 0).astype(F32), (0, 0, chip * cw_w))
    conv_full = _unpack_flat(allreduce_small("gather_conv_w", _pack_flat([placed], F32, LANES)), [placed.shape])[0]
    small = {k: w_loc[k] for k in SMALL}
    small["ssd_conv_w"] = conv_full

    loss_l, grad_x, g_big, g_small = local_step(x[0], c, ctx[0], full, small, loss_target[0])

    small_shapes = [g_small[k].shape for k in SMALL] + [(LANES,)]
    summed = _unpack_flat(allreduce_small("reduce_small", _pack_flat([g_small[k] for k in SMALL] + [loss_l], F32, LANES)),
                          small_shapes)
    gsum = dict(zip(SMALL, summed[:-1]))
    loss = summed[-1][0]
    gsum["ssd_conv_w"] = lax.dynamic_slice(gsum["ssd_conv_w"], (0, 0, chip * cw_w), cw.shape)

    pieces = jnp.stack([_pack_flat([_shards(g_big[k], BIG_AXIS[k])[j] for k in BIG], CDT, 1024, PACK_ROWS) for j in range(4)])
    pair = add_cast("add_cores", *swap_halves("swap_halves", pieces))
    chip_sum = sum_parts("sum_chips", scatter_chips("scatter_grads", pair))
    both = swap_cores("swap_cores", chip_sum)
    g_sum = _unpack_flat(both, big_shapes)

    grads, deltas, new_m, new_v = {}, {}, {}, {}
    for i, k in enumerate(BIG):
        shp = w_loc[k].shape
        two_d = lambda a, shp=shp: a.reshape(-1, shp[-1])
        res = adamw("adamw_" + k, two_d(w_loc[k]), two_d(m_loc[k]), two_d(v_loc[k]), [two_d(g_sum[i])])
        grads[k], deltas[k], new_m[k], new_v[k] = [r.reshape(shp) for r in res]
    small_loc_shapes = [w_loc[k].shape for k in SMALL]
    res = adamw("adamw_small", _pack_flat([w_loc[k] for k in SMALL], F32, LANES),
                _pack_flat([m_loc[k] for k in SMALL], F32, LANES), _pack_flat([v_loc[k] for k in SMALL], F32, LANES),
                [_pack_flat([gsum[k] for k in SMALL], F32, LANES)])
    for dst, r in zip((grads, deltas, new_m, new_v), res):
        dst.update(dict(zip(SMALL, _unpack_flat(r, small_loc_shapes))))

    return (loss, grad_x[None], *[grads[k] for k in WEIGHTS], *[deltas[k] for k in WEIGHTS],
            *[new_m[k] for k in WEIGHTS], *[new_v[k] for k in WEIGHTS])
```

```python
import functools
import math
from typing import NamedTuple

import jax
import jax.numpy as jnp
from jax import lax
from jax.experimental import pallas as pl
from jax.experimental.pallas import tpu as pltpu

F32 = jnp.float32
CDT = jnp.bfloat16
NORM_EPS = 1e-6
ROPE_THETA = 10000.0
GRID_W = 64
D_MODEL = 1024
ATTN_HEADS, ATTN_KV, ATTN_HD = 8, 2, 64
SSD_HEADS, SSD_HD, SSD_STATE = 8, 64, 128
RET_HEADS, RET_DK = 4, 128
CHUNK = 128
ROW_TILE = 256
MM_ROWS = 768
ATTN_TQ, ATTN_TK = 256, 256
ATTN_TK_FWD = 512
LANES = 128
PACK_ROWS = 512
PAIR_ROWS = 1024
COND_ROWS = 16
VMEM_LIMIT = 56 * 1024 * 1024

ADAM_LR, ADAM_B1, ADAM_B2, ADAM_EPS, ADAM_WD, ADAM_STEP = 0.001, 0.9, 0.999, 1e-08, 0.01, 10

IN_LAYOUT = {
    "gates": (0, 3072, 4368, 3072), "xbc": (3072, 1024, 1280, 1024), "q": (4096, 512, 0, 512),
    "z": (4608, 512, 768, 512), "rq": (5120, 512, 2320, 512), "rk": (5632, 512, 2832, 512),
    "rv": (6144, 512, 3344, 512), "rg": (6656, 512, 3856, 512), "k": (7168, 128, 512, 128),
    "v": (7296, 128, 640, 128), "dt": (7424, 128, 2304, 16),
}
IN_PAD = 7680
IN_ORIG_ORDER = ("q", "k", "v", "z", "xbc", "dt", "rq", "rk", "rv", "rg", "gates")
IN_NEW_ORDER = ("gates", "xbc", "q", "z", "rq", "rk", "rv", "rg", "k", "v", "dt")

BIG = ("w_mod", "w_in", "w_branch", "w_out", "w_mlp1", "w_mlp2")
BIG_AXIS = {"w_mod": 2, "w_in": 2, "w_branch": 3, "w_out": 1, "w_mlp1": 2, "w_mlp2": 1}
SMALL = ("c_ctx", "b_mod", "norm1_w", "norm2_w", "attn_q_norm", "attn_k_norm", "ssd_conv_w", "ssd_conv_b",
         "ssd_dt_bias", "ssd_a_log", "ssd_d", "ssd_norm_w", "ret_log_decay", "ret_gn_w", "final_norm_w")
WEIGHTS = ("c_ctx", "w_mod", "b_mod", "norm1_w", "norm2_w", "w_in", "attn_q_norm", "attn_k_norm", "ssd_conv_w",
           "ssd_conv_b", "ssd_dt_bias", "ssd_a_log", "ssd_d", "ssd_norm_w", "ret_log_decay", "ret_gn_w",
           "w_branch", "w_out", "w_mlp1", "w_mlp2", "final_norm_w")


def _cp(sem):
    return pltpu.CompilerParams(dimension_semantics=sem, vmem_limit_bytes=VMEM_LIMIT)


class Cols(NamedTuple):
    arr: jax.Array
    off: int
    width: int


def _width(item):
    return item.width if isinstance(item, Cols) else item.shape[1]


def _row_in(item, rows, imap=None):
    imap = imap or (lambda i: i)
    if isinstance(item, Cols):
        assert item.off % item.width == 0
        blk = item.off // item.width
        return item.arr, pl.BlockSpec((rows, item.width), lambda i, blk=blk: (imap(i), blk))
    return item, pl.BlockSpec((rows, item.shape[1]), lambda i: (imap(i), 0))


def _const_spec(shape):
    return pl.BlockSpec(shape, lambda *_: (0,) * len(shape))


def _mxu(a, b, dims=(((1,), (0,)), ((), ()))):
    return lax.dot_general(a.astype(CDT), b.astype(CDT), dims, preferred_element_type=F32)


_NT = (((1,), (1,)), ((), ()))
_TN = (((0,), (0,)), ((), ()))


@jax.custom_vjp
def _softplus(x):
    return jnp.maximum(x, 0.0) + jnp.log1p(jnp.exp(-jnp.abs(x)))


def _softplus_fwd(x):
    return _softplus(x), x


def _softplus_bwd(x, g):
    return (g * jax.nn.sigmoid(x),)


_softplus.defvjp(_softplus_fwd, _softplus_bwd)


def _group_mean_impl(x, gmat):
    hi = x.astype(CDT)
    lo = (x - hi.astype(F32)).astype(CDT)
    return (jnp.dot(hi, gmat, preferred_element_type=F32) + jnp.dot(lo, gmat, preferred_element_type=F32))


@jax.custom_vjp
def _group_mean(x, gmat):
    return _group_mean_impl(x, gmat)


def _group_mean_fwd(x, gmat):
    return _group_mean_impl(x, gmat), gmat


def _group_mean_bwd(gmat, g):
    return _group_mean_impl(g, gmat), jnp.zeros_like(gmat)


_group_mean.defvjp(_group_mean_fwd, _group_mean_bwd)


def _group_matrix(width, group):
    r = jnp.arange(width) // group
    return jnp.where(r[:, None] == r[None, :], 1.0 / group, 0.0).astype(CDT)


def _make_rope(half):
    def partner(x):
        w = x.shape[1]
        lane = lax.broadcasted_iota(jnp.int32, x.shape, 1)
        first = (lane % (2 * half)) < half
        return jnp.where(first, pltpu.roll(x, w - half, axis=1), pltpu.roll(x, half, axis=1))

    def impl(x, cos_full, sin_signed):
        return x * cos_full + partner(x) * sin_signed

    @jax.custom_vjp
    def rope(x, cos_full, sin_signed):
        return impl(x, cos_full, sin_signed)

    def fwd(x, cos_full, sin_signed):
        return impl(x, cos_full, sin_signed), (cos_full, sin_signed)

    def bwd(res, g):
        cos_full, sin_signed = res
        return impl(g, cos_full, -sin_signed), jnp.zeros_like(cos_full), jnp.zeros_like(sin_signed)

    rope.defvjp(fwd, bwd)
    return rope


_rope32 = _make_rope(32)
_rope64 = _make_rope(64)


def _rms(x, w):
    return x * lax.rsqrt(jnp.mean(x * x, axis=-1, keepdims=True) + NORM_EPS) * w


def _col(v, lane_index):
    lane = lax.broadcasted_iota(jnp.int32, v.shape, 1)
    return jnp.sum(jnp.where(lane == lane_index, v, 0.0), axis=1, keepdims=True)


def _typed_spec(width, nb_ctx):
    return pl.BlockSpec((None, 1, width), lambda i: (jnp.where(i >= nb_ctx, 1, 0), 0, 0))


def rowwise_fwd(name, f, rows, typed, shared, outs, n_rows, nb_ctx, tm=ROW_TILE):
    tm = min(tm, n_rows)
    nin = len(rows) + len(typed) + len(shared)

    def body(*refs):
        res = f(*[r[...] for r in refs[:nin]])
        for o_ref, o in zip(refs[nin:], res):
            o_ref[...] = o.astype(o_ref.dtype)

    arrs, specs = [], []
    for it in rows:
        a, s = _row_in(it, tm)
        arrs.append(a)
        specs.append(s)
    for t in typed:
        arrs.append(t)
        specs.append(_typed_spec(t.shape[-1], nb_ctx))
    for s_ in shared:
        arrs.append(s_)
        specs.append(_const_spec(s_.shape))
    res = pl.pallas_call(
        body, name=name, grid=(n_rows // tm,), in_specs=specs,
        out_specs=[pl.BlockSpec((tm, w), lambda i: (i, 0)) for w, _ in outs],
        out_shape=[jax.ShapeDtypeStruct((n_rows, w), dt) for w, dt in outs],
        compiler_params=_cp(("parallel",)))(*arrs)
    return res


def rowwise_bwd(name, f, rows, typed, shared, cots, row_diff, shared_diff, drow_dtypes, n_rows, nb_ctx, tm=ROW_TILE):
    tm = min(tm, n_rows)
    nr, nt, ns, nc = len(rows), len(typed), len(shared), len(cots)
    nin = nr + nt + ns
    d_rows = [k for k in range(nr) if row_diff[k]]
    d_sh = [k for k in range(ns) if shared_diff[k]]

    def body(*refs):
        rvals = [r[...] for r in refs[:nr]]
        tvals = [r[...] for r in refs[nr:nr + nt]]
        svals = [r[...] for r in refs[nr + nt:nin]]
        cvals = [r[...].astype(F32) for r in refs[nin:nin + nc]]
        out_refs = refs[nin + nc:]

        def g(*dv):
            dv = list(dv)
            rv = list(rvals)
            for k in d_rows:
                rv[k] = dv.pop(0)
            tv = [dv.pop(0) for _ in range(nt)]
            sv = list(svals)
            for k in d_sh:
                sv[k] = dv.pop(0)
            return tuple(o.astype(F32) for o in f(*rv, *tv, *sv))

        prim = [rvals[k].astype(F32) for k in d_rows] + tvals + [svals[k] for k in d_sh]
        _, vjp = jax.vjp(g, *prim)
        grads = list(vjp(tuple(cvals)))
        i = pl.program_id(0)
        for ref in out_refs[:len(d_rows)]:
            ref[...] = grads.pop(0).astype(ref.dtype)
        first_typed = (i == 0) | (i == nb_ctx)
        for ref in out_refs[len(d_rows):len(d_rows) + nt]:
            gr = grads.pop(0)

            @pl.when(first_typed)
            def _(ref=ref, gr=gr):
                ref[...] = gr

            @pl.when(jnp.logical_not(first_typed))
            def _(ref=ref, gr=gr):
                ref[...] += gr
        for ref in out_refs[len(d_rows) + nt:]:
            gr = grads.pop(0)

            @pl.when(i == 0)
            def _(ref=ref, gr=gr):
                ref[...] = gr

            @pl.when(i != 0)
            def _(ref=ref, gr=gr):
                ref[...] += gr

    arrs, specs = [], []
    for it in list(rows):
        a, s = _row_in(it, tm)
        arrs.append(a)
        specs.append(s)
    for t in typed:
        arrs.append(t)
        specs.append(_typed_spec(t.shape[-1], nb_ctx))
    for s_ in shared:
        arrs.append(s_)
        specs.append(_const_spec(s_.shape))
    for c_ in cots:
        a, s = _row_in(c_, tm)
        arrs.append(a)
        specs.append(s)
    out_specs, out_shape = [], []
    for k, dt in zip(d_rows, drow_dtypes):
        w = _width(rows[k])
        out_specs.append(pl.BlockSpec((tm, w), lambda i: (i, 0)))
        out_shape.append(jax.ShapeDtypeStruct((n_rows, w), dt))
    for t in typed:
        out_specs.append(_typed_spec(t.shape[-1], nb_ctx))
        out_shape.append(jax.ShapeDtypeStruct(t.shape, F32))
    for k in d_sh:
        out_specs.append(_const_spec(shared[k].shape))
        out_shape.append(jax.ShapeDtypeStruct(shared[k].shape, F32))
    res = pl.pallas_call(body, name=name, grid=(n_rows // tm,), in_specs=specs, out_specs=out_specs,
                         out_shape=out_shape, compiler_params=_cp(("arbitrary",)))(*arrs)
    n1, n2 = len(d_rows), len(d_rows) + nt
    return list(res[:n1]), list(res[n1:n2]), list(res[n2:])


def _pick(n, prefs):
    for p in prefs:
        if n % p == 0:
            return p
    return n


def mm(name, a, b, out_dtype):
    k, n = b.shape
    m = (a.arr if isinstance(a, Cols) else a).shape[0]
    assert _width(a) == k
    tm = _pick(m, (MM_ROWS, 256))
    tn = _pick(n, (512, 256, 128))

    def body(a_ref, b_ref, o_ref):
        o_ref[...] = jnp.dot(a_ref[...], b_ref[...], preferred_element_type=F32).astype(o_ref.dtype)

    a_arr, a_spec = _row_in(a, tm)
    a_spec = pl.BlockSpec(a_spec.block_shape, lambda j, i, f=a_spec.index_map: f(i))
    return pl.pallas_call(
        body, name=name, grid=(n // tn, m // tm),
        in_specs=[a_spec, pl.BlockSpec((k, tn), lambda j, i: (0, j))],
        out_specs=pl.BlockSpec((tm, tn), lambda j, i: (i, j)),
        out_shape=jax.ShapeDtypeStruct((m, n), out_dtype),
        compiler_params=_cp(("parallel", "parallel")))(a_arr, b)


def mm_tn(name, a, b):
    t = (a.arr if isinstance(a, Cols) else a).shape[0]
    k, n = _width(a), _width(b)
    tt = _pick(t, (MM_ROWS, 256))
    tk = _pick(k, (512, 256, 128))
    tn = _pick(n, (1280, 1024, 512, 256, 128))

    def body(a_ref, b_ref, o_ref):
        part = lax.dot_general(a_ref[...], b_ref[...], _TN, preferred_element_type=F32)

        @pl.when(pl.program_id(2) == 0)
        def _():
            o_ref[...] = part

        @pl.when(pl.program_id(2) != 0)
        def _():
            o_ref[...] += part

    def win(item, width):
        if isinstance(item, Cols):
            assert item.off % width == 0
            return item.arr, item.off // width
        return item, 0

    a_arr, a0 = win(a, tk)
    b_arr, b0 = win(b, tn)
    return pl.pallas_call(
        body, name=name, grid=(k // tk, n // tn, t // tt),
        in_specs=[pl.BlockSpec((tt, tk), lambda ki, ni, ti: (ti, a0 + ki)),
                  pl.BlockSpec((tt, tn), lambda ki, ni, ti: (ti, b0 + ni))],
        out_specs=pl.BlockSpec((tk, tn), lambda ki, ni, ti: (ki, ni)),
        out_shape=jax.ShapeDtypeStruct((k, n), F32),
        compiler_params=_cp(("parallel", "parallel", "arbitrary")))(a_arr, b_arr)


def attn_fwd(name, qT, kk, vT, m_ctx):
    _, nq, hd, r = qT.shape
    t = kk.shape[1]
    tq = r // 4
    tk = _pick(t - m_ctx, (ATTN_TK_FWD, ATTN_TK))
    nqc, n_lat_tiles = m_ctx // tq, (t - m_ctx) // tk

    def body(qT_ref, k_ref, vT_ref, oT_ref, lse_ref):
        i = pl.program_id(1)
        q_t = qT_ref[...]

        def tile(off, size, carry):
            mi, li, acc = carry
            st = jnp.dot(k_ref[pl.ds(off, size), :], q_t, preferred_element_type=F32)
            mn = jnp.maximum(mi, jnp.max(st, axis=0, keepdims=True))
            pt = jnp.exp(st - mn)
            al = jnp.exp(mi - mn)
            li = al * li + jnp.sum(pt, axis=0, keepdims=True)
            acc = al * acc + jnp.dot(vT_ref[:, pl.ds(off, size)], pt.astype(CDT), preferred_element_type=F32)
            return mn, li, acc

        init = (jnp.full((1, r), -1e30, F32), jnp.zeros((1, r), F32), jnp.zeros((hd, r), F32))
        carry = tile(0, m_ctx, init)
        mi, li, acc = lax.fori_loop(
            0, jnp.where(i < nqc, 0, n_lat_tiles),
            lambda j, cr: tile(pl.multiple_of(m_ctx + j * tk, ATTN_TK), tk, cr), carry)
        oT_ref[...] = (acc / li).astype(oT_ref.dtype)
        lse_ref[...] = mi + jnp.log(li)

    return pl.pallas_call(
        body, name=name, grid=(2, nq),
        in_specs=[pl.BlockSpec((None, None, hd, r), lambda g, i: (g, i, 0, 0)),
                  pl.BlockSpec((None, t, hd), lambda g, i: (g, 0, 0)),
                  pl.BlockSpec((None, hd, t), lambda g, i: (g, 0, 0))],
        out_specs=[pl.BlockSpec((None, None, hd, r), lambda g, i: (g, i, 0, 0)),
                   pl.BlockSpec((None, None, 1, r), lambda g, i: (g, i, 0, 0))],
        out_shape=[jax.ShapeDtypeStruct(qT.shape, CDT), jax.ShapeDtypeStruct((2, nq, 1, r), F32)],
        compiler_params=_cp(("parallel", "arbitrary")))(qT, kk, vT)


def attn_bwd(name, qT, doT, oT, lse, kk, kT, vv, m_ctx):
    _, nq, hd, r = qT.shape
    t = kk.shape[1]
    tq, tk = r // 4, ATTN_TK
    nqc, nkc, nk = m_ctx // tq, m_ctx // tk, t // tk

    def body(qT_ref, doT_ref, oT_ref, lse_ref, k_ref, kT_ref, v_ref,
             dq_ref, dk_ref, dv_ref, dk_acc, dv_acc, delta_s):
        j, i = pl.program_id(1), pl.program_id(2)

        @pl.when(i == 0)
        def _():
            dk_acc[...] = jnp.zeros_like(dk_acc)
            dv_acc[...] = jnp.zeros_like(dv_acc)

        @pl.when(j == 0)
        def _():
            delta_s[i] = jnp.sum(doT_ref[...].astype(F32) * oT_ref[...].astype(F32), axis=0, keepdims=True)

        @pl.when((j < nkc) | (i >= nqc))
        def _():
            st = jnp.dot(k_ref[...], qT_ref[...], preferred_element_type=F32)
            pt = jnp.exp(st - lse_ref[...])
            dv_acc[...] += lax.dot_general(pt.astype(CDT), doT_ref[...], _NT, preferred_element_type=F32)
            dpt = jnp.dot(v_ref[...], doT_ref[...], preferred_element_type=F32)
            dst = (pt * (dpt - delta_s[i])).astype(CDT)
            dk_acc[...] += lax.dot_general(dst, qT_ref[...], _NT, preferred_element_type=F32)
            part = jnp.dot(kT_ref[...], dst, preferred_element_type=F32)

            @pl.when(j == 0)
            def _():
                dq_ref[i] = part

            @pl.when(j != 0)
            def _():
                dq_ref[i] += part

        @pl.when(i == nq - 1)
        def _():
            dk_ref[...] = dk_acc[...]
            dv_ref[...] = dv_acc[...]

    blk_t = pl.BlockSpec((None, None, hd, r), lambda g, j, i: (g, i, 0, 0))
    row = pl.BlockSpec((None, None, 1, r), lambda g, j, i: (g, i, 0, 0))
    kv = pl.BlockSpec((None, tk, hd), lambda g, j, i: (g, j, 0))
    kv_t = pl.BlockSpec((None, hd, tk), lambda g, j, i: (g, 0, j))
    return pl.pallas_call(
        body, name=name, grid=(2, nk, nq),
        in_specs=[blk_t, blk_t, blk_t, row, kv, kv_t, kv],
        out_specs=[pl.BlockSpec((None, nq, hd, r), lambda g, j, i: (g, 0, 0, 0)), kv, kv],
        out_shape=[jax.ShapeDtypeStruct(qT.shape, F32), jax.ShapeDtypeStruct(kk.shape, F32),
                   jax.ShapeDtypeStruct(kk.shape, F32)],
        scratch_shapes=[pltpu.VMEM((tk, hd), F32), pltpu.VMEM((tk, hd), F32), pltpu.VMEM((nq, 1, r), F32)],
        compiler_params=_cp(("parallel", "arbitrary", "arbitrary")))(qT, doT, oT, lse, kk, kT, vv)


def to_heads_t(name, a, tq):
    t = a.shape[0]
    half = 4 * ATTN_HD

    def body(a_ref, o_ref):
        blk = a_ref[...].astype(F32).T
        o_ref[...] = jnp.concatenate([blk[hh * ATTN_HD:(hh + 1) * ATTN_HD, :] for hh in range(4)],
                                     axis=1).astype(o_ref.dtype)

    return pl.pallas_call(
        body, name=name, grid=(2, t // tq), in_specs=[pl.BlockSpec((tq, half), lambda g, i: (i, g))],
        out_specs=pl.BlockSpec((None, None, ATTN_HD, 4 * tq), lambda g, i: (g, i, 0, 0)),
        out_shape=jax.ShapeDtypeStruct((2, t // tq, ATTN_HD, 4 * tq), a.dtype),
        compiler_params=_cp(("parallel", "parallel")))(a)


def from_heads_t(name, a_t, out_dtype):
    _, nq, hd, r = a_t.shape
    tq = r // 4

    def body(a_ref, o_ref):
        blk = a_ref[...].astype(F32)
        stacked = jnp.concatenate([blk[:, hh * tq:(hh + 1) * tq] for hh in range(4)], axis=0)
        o_ref[...] = stacked.T.astype(o_ref.dtype)

    return pl.pallas_call(
        body, name=name, grid=(2, nq), in_specs=[pl.BlockSpec((None, None, hd, r), lambda g, i: (g, i, 0, 0))],
        out_specs=pl.BlockSpec((tq, 4 * hd), lambda g, i: (i, g)),
        out_shape=jax.ShapeDtypeStruct((nq * tq, 8 * hd), out_dtype),
        compiler_params=_cp(("parallel", "parallel")))(a_t)


def _split_kv(a):
    return a.reshape(a.shape[0], 2, ATTN_HD).transpose(1, 0, 2)


def _merge_kv(a):
    return a.transpose(1, 0, 2).reshape(a.shape[1], 2 * ATTN_HD)


def _chunk_order(rev, ncc, nct):
    if not rev:
        return lambda s: s
    return lambda s: jnp.where(s < ncc, ncc - 1 - s, nct - 1 - (s - ncc))


def scan_fwd(name, fn, rows, shared, n_state, y_width, n_rows, m_ctx, rev):
    nct, ncc = n_rows // CHUNK, m_ctx // CHUNK
    order = _chunk_order(rev, ncc, nct)
    nr, ns = len(rows), len(shared)

    def body(*refs):
        rvals = [r[...] for r in refs[:nr]]
        svals = [r[...] for r in refs[nr:nr + ns]]
        y_ref, sin_ref, st = refs[nr + ns:]

        @pl.when(pl.program_id(0) == 0)
        def _():
            st[...] = jnp.zeros_like(st)

        prev = [st[k] for k in range(n_state)]
        sin_ref[...] = st[...]
        y, new = fn(rvals, svals, prev)
        y_ref[...] = y
        for k in range(n_state):
            st[k] = new[k]

    arrs, specs = [], []
    for it in rows:
        a, s = _row_in(it, CHUNK, order)
        arrs.append(a)
        specs.append(s)
    for s_ in shared:
        arrs.append(s_)
        specs.append(_const_spec(s_.shape))
    return pl.pallas_call(
        body, name=name, grid=(nct,), in_specs=specs,
        out_specs=[pl.BlockSpec((CHUNK, y_width), lambda s: (order(s), 0)),
                   pl.BlockSpec((None, n_state, LANES, LANES), lambda s: (order(s), 0, 0, 0))],
        out_shape=[jax.ShapeDtypeStruct((n_rows, y_width), F32),
                   jax.ShapeDtypeStruct((nct, n_state, LANES, LANES), F32)],
        scratch_shapes=[pltpu.VMEM((n_state, LANES, LANES), F32)],
        compiler_params=_cp(("arbitrary",)))(*arrs)


def scan_bwd(name, fn, rows, shared, states_in, dy, post, outs, addends, n_state, n_rows, m_ctx, rev):
    nct, ncc = n_rows // CHUNK, m_ctx // CHUNK
    fwd_order = _chunk_order(rev, ncc, nct)
    order = lambda r: fwd_order(nct - 1 - r)
    nr, ns, na = len(rows), len(shared), len(addends)
    nin = nr + ns

    def body(*refs):
        rvals = [r[...] for r in refs[:nr]]
        svals = [r[...] for r in refs[nr:nin]]
        sin_ref, dy_ref = refs[nin], refs[nin + 1]
        add_refs = refs[nin + 2:nin + 2 + na]
        out_refs = refs[nin + 2 + na:nin + 2 + na + len(outs)]
        dsh_refs = refs[nin + 2 + na + len(outs):-1]
        dst = refs[-1]
        r = pl.program_id(0)

        @pl.when(r == 0)
        def _():
            dst[...] = jnp.zeros_like(dst)

        prev = [sin_ref[k] for k in range(n_state)]
        _, vjp = jax.vjp(fn, rvals, svals, prev)
        d_rows, d_shared, d_prev = vjp((dy_ref[...], [dst[k] for k in range(n_state)]))
        res = post(d_rows)
        for k, (ref, val) in enumerate(zip(out_refs, res)):
            if k < na:
                val = val + add_refs[k][...]
            ref[...] = val.astype(ref.dtype)
        for ref, gr in zip(dsh_refs, d_shared):
            @pl.when(r == 0)
            def _(ref=ref, gr=gr):
                ref[...] = gr

            @pl.when(r != 0)
            def _(ref=ref, gr=gr):
                ref[...] += gr
        for k in range(n_state):
            dst[k] = d_prev[k]

    arrs, specs = [], []
    for it in rows:
        a, s = _row_in(it, CHUNK, order)
        arrs.append(a)
        specs.append(s)
    for s_ in shared:
        arrs.append(s_)
        specs.append(_const_spec(s_.shape))
    arrs.append(states_in)
    specs.append(pl.BlockSpec((None, n_state, LANES, LANES), lambda r: (order(r), 0, 0, 0)))
    for it in [dy] + list(addends):
        a, s = _row_in(it, CHUNK, order)
        arrs.append(a)
        specs.append(s)
    out_specs = [pl.BlockSpec((CHUNK, w), lambda r: (order(r), 0)) for w, _ in outs]
    out_shape = [jax.ShapeDtypeStruct((n_rows, w), dt) for w, dt in outs]
    for s_ in shared:
        out_specs.append(_const_spec(s_.shape))
        out_shape.append(jax.ShapeDtypeStruct(s_.shape, F32))
    res = pl.pallas_call(body, name=name, grid=(nct,), in_specs=specs, out_specs=out_specs, out_shape=out_shape,
                         scratch_shapes=[pltpu.VMEM((n_state, LANES, LANES), F32)],
                         compiler_params=_cp(("arbitrary",)))(*arrs)
    return list(res[:len(outs)]), list(res[len(outs):])


def _make_ssd_chunk(direction):
    rev = direction == 1
    base = 8 * direction

    def fn(rows, shared, prev):
        xs, bms, cms, dtraw = rows[0:4], rows[4:6], rows[6:8], rows[8]
        dt_bias, a_log = shared
        ln = dtraw.shape[0]
        dt_all = _softplus(dtraw + dt_bias)
        a_all = dt_all * (-jnp.exp(a_log))
        r_i = lax.broadcasted_iota(jnp.int32, (ln, ln), 0)
        c_i = lax.broadcasted_iota(jnp.int32, (ln, ln), 1)
        tri = (r_i <= c_i) if rev else (r_i >= c_i)
        a_cum_all = jnp.dot(tri.astype(F32), a_all, precision=lax.Precision.HIGHEST, preferred_element_type=F32)
        a_tot_all = jnp.sum(a_all, axis=0, keepdims=True)
        first = lax.broadcasted_iota(jnp.int32, (ln, LANES), 1) < SSD_HD
        first_row = lax.broadcasted_iota(jnp.int32, (LANES, 1), 0) < SSD_HD

        def lmat(acol):
            a_b = jnp.broadcast_to(acol, (ln, ln))
            seg = a_b - a_b.T
            return jnp.where(tri, jnp.exp(jnp.where(tri, seg, 0.0)), 0.0)

        ys, new = [], []
        for g in range(2):
            bm, cm = bms[g], cms[g]
            cb = _mxu(cm, bm, _NT)
            for jj in range(2):
                pr = 2 * g + jj
                h0, h1 = base + 2 * pr, base + 2 * pr + 1
                ac0, ac1 = _col(a_cum_all, h0), _col(a_cum_all, h1)
                at0, at1 = _col(a_tot_all, h0), _col(a_tot_all, h1)
                dt_pair = jnp.where(first, _col(dt_all, h0), _col(dt_all, h1))
                acum_pair = jnp.where(first, ac0, ac1)
                atot_pair = jnp.where(first[0:1], at0, at1)
                xd = xs[pr] * dt_pair
                st = _mxu(xd * jnp.exp(atot_pair - acum_pair), bm, _TN)
                new.append(prev[pr] * jnp.where(first_row, jnp.exp(at0), jnp.exp(at1)) + st)
                y0 = _mxu(cb * lmat(ac0), xd)
                y1 = _mxu(cb * lmat(ac1), xd)
                y_off = _mxu(cm, prev[pr], _NT) * jnp.exp(acum_pair)
                ys.append(jnp.where(first, y0, y1) + y_off)
        return jnp.concatenate(ys, axis=1), new

    return fn


def _make_ret_chunk(direction):
    rev = direction == 1
    base = 4 * direction

    def fn(rows, shared, prev):
        qs, ks, vs = rows[0:4], rows[4:8], rows[8:12]
        lg_all = -jnp.exp(shared[0])
        ln = qs[0].shape[0]
        pos = lax.broadcasted_iota(jnp.int32, (ln, 1), 0).astype(F32)
        r_i = lax.broadcasted_iota(jnp.int32, (ln, ln), 0)
        c_i = lax.broadcasted_iota(jnp.int32, (ln, ln), 1)
        diff = ((c_i - r_i) if rev else (r_i - c_i))
        mask = diff >= 0
        dpos = jnp.maximum(diff, 0).astype(F32)
        k_pow = pos if rev else (ln - 1.0 - pos)
        q_pow = (ln - pos) if rev else (pos + 1.0)
        ys, new = [], []
        for h in range(RET_HEADS):
            lg = _col(lg_all, base + h)
            dmat = jnp.where(mask, jnp.exp(dpos * lg), 0.0)
            st = _mxu(ks[h] * jnp.exp(k_pow * lg), vs[h], _TN)
            new.append(prev[h] * jnp.exp(ln * lg) + st)
            s = _mxu(qs[h], ks[h], _NT) * dmat
            ys.append(_mxu(s, vs[h]) + _mxu(qs[h], prev[h]) * jnp.exp(q_pow * lg))
        return jnp.concatenate(ys, axis=1), new

    return fn


def _conv_pre(x, w, b, t_idx, n_rows, m_ctx):
    is_start = (t_idx == 0) | (t_idx == m_ctx)
    is_end = (t_idx == m_ctx - 1) | (t_idx == n_rows - 1)
    xp = jnp.where(is_start, 0.0, pltpu.roll(x, 1, axis=0))
    xn = jnp.where(is_end, 0.0, pltpu.roll(x, n_rows - 1, axis=0))
    return w[0:1] * xp + w[1:2] * x + w[2:3] * xn + b, xp, xn, is_start, is_end


def conv_fwd(x, conv_w, conv_b, m_ctx):
    n_rows, width = x.arr.shape[0], x.width
    c0 = x.off // LANES

    def body(x_ref, w_ref, b_ref, o_ref):
        t_idx = lax.broadcasted_iota(jnp.int32, (n_rows, 1), 0)
        pre = _conv_pre(x_ref[...], w_ref[...], b_ref[...], t_idx, n_rows, m_ctx)[0]
        o_ref[...] = pre * jax.nn.sigmoid(pre)

    return pl.pallas_call(
        body, name="conv_fwd", grid=(width // LANES,),
        in_specs=[pl.BlockSpec((n_rows, LANES), lambda c: (0, c0 + c)),
                  pl.BlockSpec((3, LANES), lambda c: (0, c)), pl.BlockSpec((1, LANES), lambda c: (0, c))],
        out_specs=pl.BlockSpec((n_rows, LANES), lambda c: (0, c)),
        out_shape=jax.ShapeDtypeStruct((n_rows, width), F32),
        compiler_params=_cp(("parallel",)))(x.arr, conv_w, conv_b)


def conv_bwd(x, conv_w, conv_b, dy, dxs_extra, m_ctx):
    n_rows, width = x.arr.shape[0], x.width
    c0 = x.off // LANES
    n_extra = dxs_extra.shape[1] // LANES

    def body(x_ref, w_ref, b_ref, dy_ref, ex_ref, dx_ref, dw_ref, db_ref):
        c = pl.program_id(0)
        t_idx = lax.broadcasted_iota(jnp.int32, (n_rows, 1), 0)
        w = w_ref[...]
        pre, xp, xn, is_start, is_end = _conv_pre(x_ref[...], w, b_ref[...], t_idx, n_rows, m_ctx)
        sg = jax.nn.sigmoid(pre)
        dyv = dy_ref[...] + jnp.where(c < n_extra, ex_ref[...], 0.0)
        dpre = dyv * (sg * (1.0 + pre * (1.0 - sg)))
        d_next = jnp.where(is_end, 0.0, pltpu.roll(dpre, n_rows - 1, axis=0))
        d_prev = jnp.where(is_start, 0.0, pltpu.roll(dpre, 1, axis=0))
        dx_ref[...] = (w[1:2] * dpre + w[0:1] * d_next + w[2:3] * d_prev).astype(dx_ref.dtype)
        dw_ref[...] = jnp.concatenate([jnp.sum(dpre * xp, axis=0, keepdims=True),
                                       jnp.sum(dpre * x_ref[...], axis=0, keepdims=True),
                                       jnp.sum(dpre * xn, axis=0, keepdims=True)], axis=0)
        db_ref[...] = jnp.sum(dpre, axis=0, keepdims=True)

    return pl.pallas_call(
        body, name="conv_bwd", grid=(width // LANES,),
        in_specs=[pl.BlockSpec((n_rows, LANES), lambda c: (0, c0 + c)),
                  pl.BlockSpec((3, LANES), lambda c: (0, c)), pl.BlockSpec((1, LANES), lambda c: (0, c)),
                  pl.BlockSpec((n_rows, LANES), lambda c: (0, c)),
                  pl.BlockSpec((n_rows, LANES), lambda c: (0, jnp.minimum(c, n_extra - 1)))],
        out_specs=[pl.BlockSpec((n_rows, LANES), lambda c: (0, c)),
                   pl.BlockSpec((3, LANES), lambda c: (0, c)), pl.BlockSpec((1, LANES), lambda c: (0, c))],
        out_shape=[jax.ShapeDtypeStruct((n_rows, width), CDT), jax.ShapeDtypeStruct((3, width), F32),
                   jax.ShapeDtypeStruct((1, width), F32)],
        compiler_params=_cp(("parallel",)))(x.arr, conv_w, conv_b, dy, dxs_extra)


def loss_head(h, target, final_w, m_ctx):
    n_rows, d = h.shape
    tm = min(ROW_TILE, n_rows)
    nb_ctx = m_ctx // tm

    def f(hb, w, tgt):
        err = _rms(hb, w) - tgt
        return 0.5 * jnp.sum(jnp.mean(err * err, axis=-1))

    def body(h_ref, t_ref, w_ref, loss_ref, dh_ref, dw_ref):
        i = pl.program_id(0)

        @pl.when(i < nb_ctx)
        def _():
            dh_ref[...] = jnp.zeros_like(dh_ref)

        @pl.when(i == 0)
        def _():
            loss_ref[...] = jnp.zeros_like(loss_ref)
            dw_ref[...] = jnp.zeros_like(dw_ref)

        @pl.when(i >= nb_ctx)
        def _():
            val, vjp = jax.vjp(lambda hb, w: f(hb, w, t_ref[...]), h_ref[...], w_ref[...])
            dh, dw = vjp(jnp.ones((), F32))
            dh_ref[...] = dh
            dw_ref[...] += dw
            loss_ref[...] += jnp.broadcast_to(val, loss_ref.shape)

    return pl.pallas_call(
        body, name="loss_head", grid=(n_rows // tm,),
        in_specs=[pl.BlockSpec((tm, d), lambda i: (i, 0)),
                  pl.BlockSpec((tm, d), lambda i: (jnp.maximum(i - nb_ctx, 0), 0)), _const_spec((1, d))],
        out_specs=[_const_spec((1, LANES)), pl.BlockSpec((tm, d), lambda i: (i, 0)), _const_spec((1, d))],
        out_shape=[jax.ShapeDtypeStruct((1, LANES), F32), jax.ShapeDtypeStruct((n_rows, d), F32),
                   jax.ShapeDtypeStruct((1, d), F32)],
        compiler_params=_cp(("arbitrary",)))(h, target, final_w)


def adamw(name, w, m, v, g_parts):
    rows, cols = w.shape
    tr = _pick(rows, (256, 128, 64, 32, 16, 8))
    npart = len(g_parts)
    c1 = 1.0 - ADAM_B1 ** ADAM_STEP
    c2 = 1.0 - ADAM_B2 ** ADAM_STEP

    def body(*refs):
        w_ref, m_ref, v_ref = refs[:3]
        g = refs[3][...].astype(F32)
        for r in refs[4:3 + npart]:
            g = g + r[...].astype(F32)
        g_ref, d_ref, nm_ref, nv_ref = refs[3 + npart:]
        nm = ADAM_B1 * m_ref[...] + (1.0 - ADAM_B1) * g
        nv = ADAM_B2 * v_ref[...] + (1.0 - ADAM_B2) * (g * g)
        g_ref[...] = g
        nm_ref[...] = nm
        nv_ref[...] = nv
        d_ref[...] = -ADAM_LR * ((nm / c1) / (jnp.sqrt(nv / c2) + ADAM_EPS) + ADAM_WD * w_ref[...])

    spec = pl.BlockSpec((tr, cols), lambda i: (i, 0))
    return pl.pallas_call(
        body, name=name, grid=(rows // tr,), in_specs=[spec] * (3 + npart), out_specs=[spec] * 4,
        out_shape=[jax.ShapeDtypeStruct((rows, cols), F32)] * 4, compiler_params=_cp(("parallel",)))(w, m, v, *g_parts)


def sum_parts(name, parts):
    npart, rows, cols = parts.shape
    tr = _pick(rows, (512, 256, 8))

    def body(p_ref, o_ref):
        acc = p_ref[0].astype(F32)
        for k in range(1, npart):
            acc = acc + p_ref[k].astype(F32)
        o_ref[...] = acc

    return pl.pallas_call(
        body, name=name, grid=(rows // tr,), in_specs=[pl.BlockSpec((npart, tr, cols), lambda i: (0, i, 0))],
        out_specs=pl.BlockSpec((tr, cols), lambda i: (i, 0)), out_shape=jax.ShapeDtypeStruct((rows, cols), F32),
        compiler_params=_cp(("parallel",)))(parts)


MESH = pl.DeviceIdType.MESH
_HBM = pl.BlockSpec(memory_space=pl.ANY)


def _chip_peers():
    x, y, c = lax.axis_index("x"), lax.axis_index("y"), lax.axis_index("c")
    return x, y, c, [(1 - x, y), (x, 1 - y), (1 - x, 1 - y)]


def gather_chips(name, shard):
    half = shard.shape[0] // 2

    def body(x_ref, out_ref, send_sems, recv_sems, local_sem):
        x, y, c, peers = _chip_peers()
        me = 2 * x + y
        mine = pltpu.make_async_copy(x_ref, out_ref.at[me], local_sem)
        mine.start()

        def over_ici(k, chip, to):
            rws = pl.ds(c * half, half)
            return pltpu.make_async_remote_copy(
                src_ref=x_ref.at[rws], dst_ref=out_ref.at[chip, rws], send_sem=send_sems.at[k],
                recv_sem=recv_sems.at[k], device_id=to, device_id_type=MESH)

        sends = []
        for k, (px, py) in enumerate(peers):
            cp = over_ici(k, me, (px, py, c))
            cp.start()
            sends.append(cp)
        for k, (px, py) in enumerate(peers):
            over_ici(k, 2 * px + py, (px, py, c)).wait_recv()
        for cp in sends:
            cp.wait_send()
        mine.wait()

    return pl.pallas_call(
        body, name=name, in_specs=[_HBM], out_specs=_HBM,
        out_shape=jax.ShapeDtypeStruct((4,) + shard.shape, shard.dtype),
        scratch_shapes=[pltpu.SemaphoreType.DMA((3,)), pltpu.SemaphoreType.DMA((3,)), pltpu.SemaphoreType.DMA],
        )(shard)


def _pair_step(n_steps, x_ref, land, send_sems, recv_sems, credits, consume):
    x, y, c = lax.axis_index("x"), lax.axis_index("y"), lax.axis_index("c")
    sib = (x, y, 1 - c)
    i = pl.program_id(0)
    slot = i % 2

    @pl.when(i >= 2)
    def _():
        pl.semaphore_wait(credits.at[slot], 1)

    cp = pltpu.make_async_remote_copy(src_ref=x_ref, dst_ref=land.at[slot], send_sem=send_sems.at[slot],
                                      recv_sem=recv_sems.at[slot], device_id=sib, device_id_type=MESH)
    cp.start()
    cp.wait_recv()
    consume(land[slot])

    @pl.when(i < n_steps - 2)
    def _():
        pl.semaphore_signal(credits.at[slot], inc=1, device_id=sib, device_id_type=MESH)

    cp.wait_send()


def _pair_call(name, body, n_steps, in_spec, out_spec, out_shape, blk_shape, dtype, operands, aliases=None):
    grid_spec = pltpu.PrefetchScalarGridSpec(
        num_scalar_prefetch=1, grid=(n_steps,), in_specs=[in_spec], out_specs=out_spec,
        scratch_shapes=[pltpu.VMEM((2,) + blk_shape, dtype), pltpu.SemaphoreType.DMA((2,)),
                        pltpu.SemaphoreType.DMA((2,)), pltpu.SemaphoreType.REGULAR((2,))])
    return pl.pallas_call(body, name=name, grid_spec=grid_spec, out_shape=out_shape,
                          input_output_aliases=aliases or {}, compiler_params=_cp(("arbitrary",)))(*operands)


def _place():
    return jnp.stack([lax.axis_index("x"), lax.axis_index("y"), lax.axis_index("c")]).astype(jnp.int32)


def share_gathered(name, gathered):
    _, rows, cols = gathered.shape
    half = rows // 2
    tr = _pick(half, (PAIR_ROWS, 512, 16))
    nb = half // tr
    n_steps = 3 * nb

    def chip_of(k, s):
        px = jnp.where(k == 1, s[0], 1 - s[0])
        py = jnp.where(k == 0, s[1], 1 - s[1])
        return 2 * px + py

    def body(s_ref, x_ref, o_ref, land, send_sems, recv_sems, credits):
        def consume(v):
            o_ref[...] = v
        _pair_step(n_steps, x_ref, land, send_sems, recv_sems, credits, consume)

    in_spec = pl.BlockSpec((tr, cols), lambda i, s: ((2 * chip_of(i // nb, s) + s[2]) * nb + i % nb, 0))
    out_spec = pl.BlockSpec((tr, cols), lambda i, s: ((2 * chip_of(i // nb, s) + 1 - s[2]) * nb + i % nb, 0))
    flat = gathered.reshape(4 * rows, cols)
    return _pair_call(name, body, n_steps, in_spec, out_spec, jax.ShapeDtypeStruct(flat.shape, flat.dtype),
                      (tr, cols), flat.dtype, (_place(), flat), aliases={1: 0}).reshape(gathered.shape)


def exchange_halves(name, pieces):
    _, rows, cols = pieces.shape
    half = rows // 2
    tr = _pick(half, (PAIR_ROWS, 512, 16))
    nb = half // tr
    n_steps = 4 * nb

    def body(s_ref, x_ref, o_ref, land, send_sems, recv_sems, credits):
        def consume(v):
            o_ref[...] = v
        _pair_step(n_steps, x_ref, land, send_sems, recv_sems, credits, consume)

    in_spec = pl.BlockSpec((tr, cols), lambda i, s: ((2 * (i // nb) + 1 - s[2]) * nb + i % nb, 0))
    out_spec = pl.BlockSpec((tr, cols), lambda i, s: (i, 0))
    return _pair_call(name, body, n_steps, in_spec, out_spec, jax.ShapeDtypeStruct((4 * half, cols), pieces.dtype),
                      (tr, cols), pieces.dtype, (_place(), pieces.reshape(4 * rows, cols))).reshape(4, half, cols)


def add_halves(name, pieces, got):
    _, rows, cols = pieces.shape
    half = rows // 2
    tr = _pick(half, (512, 256, 16))
    nb = half // tr

    def body(s_ref, a_ref, b_ref, o_ref):
        o_ref[...] = (a_ref[...].astype(F32) + b_ref[...].astype(F32)).astype(o_ref.dtype)

    grid_spec = pltpu.PrefetchScalarGridSpec(
        num_scalar_prefetch=1, grid=(nb,),
        in_specs=[pl.BlockSpec((4, tr, cols), lambda i, s: (0, s[2] * nb + i, 0)),
                  pl.BlockSpec((4, tr, cols), lambda i, s: (0, i, 0))],
        out_specs=pl.BlockSpec((4, tr, cols), lambda i, s: (0, i, 0)))
    return pl.pallas_call(body, name=name, grid_spec=grid_spec, out_shape=jax.ShapeDtypeStruct(got.shape, CDT),
                          compiler_params=_cp(("arbitrary",)))(_place(), pieces, got)


def exchange_sums(name, part):
    rows, cols = part.shape
    tr = _pick(rows, (PAIR_ROWS // 2, 256, 8))
    n_steps = rows // tr

    def body(s_ref, x_ref, o_ref, land, send_sems, recv_sems, credits):
        c = lax.axis_index("c")
        o_ref[c] = x_ref[...]

        def consume(v):
            o_ref[1 - c] = v
        _pair_step(n_steps, x_ref, land, send_sems, recv_sems, credits, consume)

    in_spec = pl.BlockSpec((tr, cols), lambda i, s: (i, 0))
    out_spec = pl.BlockSpec((2, tr, cols), lambda i, s: (0, i, 0))
    return _pair_call(name, body, n_steps, in_spec, out_spec, jax.ShapeDtypeStruct((2, rows, cols), part.dtype),
                      (tr, cols), part.dtype, (_place(), part))


def scatter_chips(name, pieces):
    def body(p_ref, out_ref, send_sems, recv_sems, local_sem):
        x, y, c, peers = _chip_peers()
        me = 2 * x + y
        mine = pltpu.make_async_copy(p_ref.at[me], out_ref.at[me], local_sem)
        mine.start()
        sends = []
        for k, (px, py) in enumerate(peers):
            cp = pltpu.make_async_remote_copy(src_ref=p_ref.at[2 * px + py], dst_ref=out_ref.at[me],
                                              send_sem=send_sems.at[k], recv_sem=recv_sems.at[k],
                                              device_id=(px, py, c), device_id_type=MESH)
            cp.start()
            sends.append(cp)
        for k, (px, py) in enumerate(peers):
            pltpu.make_async_remote_copy(src_ref=p_ref.at[me], dst_ref=out_ref.at[2 * px + py],
                                         send_sem=send_sems.at[k], recv_sem=recv_sems.at[k], device_id=(px, py, c),
                                         device_id_type=MESH).wait_recv()
        for cp in sends:
            cp.wait_send()
        mine.wait()

    return pl.pallas_call(
        body, name=name, in_specs=[_HBM], out_specs=_HBM, out_shape=jax.ShapeDtypeStruct(pieces.shape, pieces.dtype),
        scratch_shapes=[pltpu.SemaphoreType.DMA((3,)), pltpu.SemaphoreType.DMA((3,)), pltpu.SemaphoreType.DMA],
        )(pieces)


def allreduce_small(name, buf):
    rows = buf.shape[0]

    def body(x_ref, out_ref, gath, send_sems, recv_sems):
        x, y, c = lax.axis_index("x"), lax.axis_index("y"), lax.axis_index("c")
        me = 4 * x + 2 * y + c
        masks = [(k >> 2 & 1, k >> 1 & 1, k & 1) for k in range(1, 8)]

        def flip(v, bit):
            return 1 - v if bit else v

        sends = []
        for k, (bx, by, bc) in enumerate(masks):
            cp = pltpu.make_async_remote_copy(src_ref=x_ref, dst_ref=gath.at[me], send_sem=send_sems.at[k],
                                              recv_sem=recv_sems.at[k],
                                              device_id=(flip(x, bx), flip(y, by), flip(c, bc)), device_id_type=MESH)
            cp.start()
            sends.append(cp)
        gath[me] = x_ref[...]
        for k, (bx, by, bc) in enumerate(masks):
            px, py, pc = flip(x, bx), flip(y, by), flip(c, bc)
            pltpu.make_async_remote_copy(src_ref=x_ref, dst_ref=gath.at[4 * px + 2 * py + pc],
                                         send_sem=send_sems.at[k], recv_sem=recv_sems.at[k],
                                         device_id=(px, py, pc), device_id_type=MESH).wait_recv()
        for cp in sends:
            cp.wait_send()
        acc = gath[0]
        for d in range(1, 8):
            acc = acc + gath[d]
        out_ref[...] = acc

    return pl.pallas_call(
        body, name=name, in_specs=[pl.BlockSpec(memory_space=pltpu.VMEM)],
        out_specs=pl.BlockSpec(memory_space=pltpu.VMEM), out_shape=jax.ShapeDtypeStruct(buf.shape, F32),
        scratch_shapes=[pltpu.VMEM((8, rows, LANES), F32), pltpu.SemaphoreType.DMA((7,)),
                        pltpu.SemaphoreType.DMA((7,))],
        )(buf)


def _pack_flat(arrs, dtype, width, row_mult=8):
    flat = jnp.concatenate([a.reshape(-1).astype(dtype) for a in arrs])
    pad = (-flat.shape[0]) % (row_mult * width)
    if pad:
        flat = jnp.concatenate([flat, jnp.zeros((pad,), dtype)])
    return flat.reshape(-1, width)


def _unpack_flat(buf, shapes):
    flat = buf.reshape(-1)
    out, off = [], 0
    for s in shapes:
        n = math.prod(s)
        out.append(flat[off:off + n].reshape(s))
        off += n
    return out


def _in_to_padded(w):
    parts = []
    for name in IN_NEW_ORDER:
        _, width, o_off, o_w = IN_LAYOUT[name]
        parts.append(w[..., o_off:o_off + o_w])
        if o_w < width:
            parts.append(jnp.zeros(w.shape[:-1] + (width - o_w,), w.dtype))
    used = sum(IN_LAYOUT[n][1] for n in IN_NEW_ORDER)
    parts.append(jnp.zeros(w.shape[:-1] + (IN_PAD - used,), w.dtype))
    return jnp.concatenate(parts, axis=-1)


def _in_from_padded(g):
    parts = []
    for name in IN_ORIG_ORDER:
        off, _, _, o_w = IN_LAYOUT[name]
        parts.append(g[..., off:off + o_w])
    return jnp.concatenate(parts, axis=-1)


def _pcol(p, name):
    off, width, _, _ = IN_LAYOUT[name]
    return Cols(p, off, width)


def _lane_pad(v, width=LANES):
    v = v.reshape(-1)
    return jnp.concatenate([v, jnp.zeros((width - v.shape[0],), v.dtype)]).reshape(1, width)


def _f_norm_mod(h, sh, sc, w):
    return (_rms(h, w) * (1.0 + sc) + sh,)


def _f_norm_mod_thru(h, sh, sc, w):
    return h, _rms(h, w) * (1.0 + sc) + sh


def _f_attn_prep(qraw, kraw, vraw, cq, sq, ck, sk, qw, kw, gq, gk):
    q = qraw * lax.rsqrt(_group_mean(qraw * qraw, gq) + NORM_EPS) * qw
    q = _rope32(q, cq, sq) * (ATTN_HD ** -0.5)
    k = kraw * lax.rsqrt(_group_mean(kraw * kraw, gk) + NORM_EPS) * kw
    return q, _rope32(k, ck, sk), vraw


def _f_ssd_finish(yf, yb, xs, z, d_exp, nw):
    y = (yf + yb + d_exp * xs) * (z * jax.nn.sigmoid(z))
    return (_rms(y, nw),)


def _f_ret_prep(rq, rk, cos_full, sin_signed):
    return _rope64(rq, cos_full, sin_signed), _rope64(rk, cos_full, sin_signed) * (RET_DK ** -0.5)


def _f_ret_finish(yf, yb, g, gw):
    y = yf + yb
    outs = []
    for h in range(RET_HEADS):
        yh = y[:, h * RET_DK:(h + 1) * RET_DK]
        yc = yh - jnp.mean(yh, axis=-1, keepdims=True)
        outs.append(yc * lax.rsqrt(jnp.mean(yc * yc, axis=-1, keepdims=True) + NORM_EPS))
    return (jnp.concatenate(outs, axis=1) * gw * (g * jax.nn.sigmoid(g)),)


def _f_merge(p0, p1, p2, g0, g1, g2):
    return (jax.nn.sigmoid(g0) * p0 + jax.nn.sigmoid(g1) * p1 + jax.nn.sigmoid(g2) * p2,)


def _f_mid(h, mix, g1, sh2, sc2, w2):
    h_mid = h + g1 * mix
    return h_mid, _rms(h_mid, w2) * (1.0 + sc2) + sh2


def _f_sqrelu(a):
    r = jnp.maximum(a, 0.0)
    return (r * r,)


def _f_residual(h_mid, o, g2):
    return (h_mid + g2 * o,)


def _f_silu(x):
    return (x * jax.nn.sigmoid(x),)


def _f_bias(x, b):
    return (x + b,)


def _ssd_rows(xbc, p):
    rows = [Cols(xbc, LANES * k, LANES) for k in range(4)]
    rows += [Cols(xbc, 512 + LANES * g, LANES) for g in range(2)]
    rows += [Cols(xbc, 768 + LANES * g, LANES) for g in range(2)]
    return rows + [_pcol(p, "dt")]


def _ret_rows(rq, rk, p):
    off_v = IN_LAYOUT["rv"][0]
    return ([Cols(rq, LANES * h, LANES) for h in range(4)] + [Cols(rk, LANES * h, LANES) for h in range(4)]
            + [Cols(p, off_v + LANES * h, LANES) for h in range(4)])


def layer_fwd(li, h, mod, lw, tabs, m_ctx):
    t = h.shape[0]
    nb = m_ctx // min(ROW_TILE, t)
    sh1, sc1, g1, sh2, sc2, g2 = mod
    nm = lambda s: f"l{li}_{s}"
    sv = {}
    (u,) = rowwise_fwd(nm("norm1"), _f_norm_mod, [h], [sh1, sc1], [lw["norm1_w"]], [(D_MODEL, CDT)], t, nb)
    p = mm(nm("in_proj"), u, lw["w_in"], F32)
    q, k, v = rowwise_fwd(
        nm("attn_prep"), _f_attn_prep,
        [_pcol(p, "q"), _pcol(p, "k"), _pcol(p, "v"), tabs["cq"], tabs["sq"], tabs["ck"], tabs["sk"]], [],
        [lw["qw"], lw["kw"], tabs["gq"], tabs["gk"]], [(512, CDT), (128, CDT), (128, CDT)], t, nb)
    tq = min(ATTN_TQ, m_ctx)
    qT, kk, vv = to_heads_t(nm("q_heads"), q, tq), _split_kv(k), _split_kv(v)
    oT, lse = attn_fwd(nm("attn"), qT, kk, vv.transpose(0, 2, 1), m_ctx)
    attn_o = from_heads_t(nm("o_rows"), oT, CDT)

    xbc = conv_fwd(_pcol(p, "xbc"), lw["conv_w"], lw["conv_b"], m_ctx)
    ssd_sh = [lw["dt_bias"], lw["a_log"]]
    yf, sf = scan_fwd(nm("ssd_f"), _make_ssd_chunk(0), _ssd_rows(xbc, p), ssd_sh, 4, 512, t, m_ctx, False)
    yb, sb = scan_fwd(nm("ssd_b"), _make_ssd_chunk(1), _ssd_rows(xbc, p), ssd_sh, 4, 512, t, m_ctx, True)
    (ssd_o,) = rowwise_fwd(nm("ssd_fin"), _f_ssd_finish, [yf, yb, Cols(xbc, 0, 512), _pcol(p, "z")], [],
                           [lw["d_exp"], lw["ssd_nw"]], [(512, CDT)], t, nb)

    rq, rk = rowwise_fwd(nm("ret_prep"), _f_ret_prep, [_pcol(p, "rq"), _pcol(p, "rk"), tabs["rc"], tabs["rs"]],
                         [], [], [(512, F32), (512, F32)], t, nb)
    rf, rsf = scan_fwd(nm("ret_f"), _make_ret_chunk(0), _ret_rows(rq, rk, p), [lw["ret_lg"]], 4, 512, t, m_ctx, False)
    rb, rsb = scan_fwd(nm("ret_b"), _make_ret_chunk(1), _ret_rows(rq, rk, p), [lw["ret_lg"]], 4, 512, t, m_ctx, True)
    (ret_o,) = rowwise_fwd(nm("ret_fin"), _f_ret_finish, [rf, rb, _pcol(p, "rg")], [], [lw["ret_gw"]],
                           [(512, CDT)], t, nb)

    pbs = [mm(nm(f"branch{b}"), br, lw["w_branch"][b], F32) for b, br in enumerate((attn_o, ssd_o, ret_o))]
    gl = [Cols(p, 1024 * b, 1024) for b in range(3)]
    (merged,) = rowwise_fwd(nm("merge"), _f_merge, pbs + gl, [], [], [(D_MODEL, CDT)], t, nb)
    mix = mm(nm("out_proj"), merged, lw["w_out"], F32)
    h_mid, vv2 = rowwise_fwd(nm("mid"), _f_mid, [h, mix], [g1, sh2, sc2], [lw["norm2_w"]],
                             [(D_MODEL, F32), (D_MODEL, CDT)], t, nb)
    a = mm(nm("mlp1"), vv2, lw["w_mlp1"], F32)
    (hh,) = rowwise_fwd(nm("sqrelu"), _f_sqrelu, [a], [], [], [(a.shape[1], CDT)], t, nb)
    o = mm(nm("mlp2"), hh, lw["w_mlp2"], F32)
    (h_out,) = rowwise_fwd(nm("resid"), _f_residual, [h_mid, o], [g2], [], [(D_MODEL, F32)], t, nb)
    sv.update(h=h, u=u, p=p, q=q, qT=qT, kk=kk, vv=vv, oT=oT, lse=lse, attn_o=attn_o, xbc=xbc, yf=yf, yb=yb,
              sf=sf, sb=sb, ssd_o=ssd_o, rq=rq, rk=rk, rf=rf, rb=rb, rsf=rsf, rsb=rsb, ret_o=ret_o, pbs=pbs,
              merged=merged, mix=mix, h_mid=h_mid, v=vv2, a=a, hh=hh, o=o)
    return h_out, sv


def layer_bwd(li, dh_out, sv, mod, lw, lwt, tabs, m_ctx):
    t = dh_out.shape[0]
    nb = m_ctx // min(ROW_TILE, t)
    sh1, sc1, g1, sh2, sc2, g2 = mod
    nm = lambda s: f"l{li}_{s}_bwd"
    gw = {}
    p = sv["p"]
    (do,), (dg2,), _ = rowwise_bwd(nm("resid"), _f_residual, [sv["h_mid"], sv["o"]], [g2], [], [dh_out],
                                   [False, True], [], [CDT], t, nb)
    dhh = mm(nm("mlp2_dx"), do, lwt["w_mlp2"], F32)
    gw["w_mlp2"] = mm_tn(nm("mlp2_dw"), sv["hh"], do)
    (da,), _, _ = rowwise_bwd(nm("sqrelu"), _f_sqrelu, [sv["a"]], [], [], [dhh], [True], [], [CDT], t, nb)
    dv = mm(nm("mlp1_dx"), da, lwt["w_mlp1"], F32)
    gw["w_mlp1"] = mm_tn(nm("mlp1_dw"), sv["v"], da)
    (dh_a, dmix), (dg1, dsh2, dsc2), (gw["norm2_w"],) = rowwise_bwd(
        nm("mid"), _f_mid, [sv["h"], sv["mix"]], [g1, sh2, sc2], [lw["norm2_w"]], [dh_out, dv],
        [True, True], [True], [F32, CDT], t, nb)
    dmerged = mm(nm("out_dx"), dmix, lwt["w_out"], F32)
    gw["w_out"] = mm_tn(nm("out_dw"), sv["merged"], dmix)
    gl = [Cols(p, 1024 * b, 1024) for b in range(3)]
    dmg, _, _ = rowwise_bwd(nm("merge"), _f_merge, sv["pbs"] + gl, [], [], [dmerged], [True] * 6, [], [CDT] * 6,
                            t, nb)
    dpb, dgl = dmg[:3], dmg[3:]
    brs = (sv["attn_o"], sv["ssd_o"], sv["ret_o"])
    d_attn_o = mm(nm("branch0_dx"), dpb[0], lwt["w_branch"][0], CDT)
    d_ssd_o = mm(nm("branch1_dx"), dpb[1], lwt["w_branch"][1], F32)
    d_ret_o = mm(nm("branch2_dx"), dpb[2], lwt["w_branch"][2], F32)
    gw["w_branch"] = jnp.stack([mm_tn(nm(f"branch{b}_dw"), brs[b], dpb[b]) for b in range(3)])
    tq = min(ATTN_TQ, m_ctx)
    dq_t, dk_s, dv_s = attn_bwd(nm("attn"), sv["qT"], to_heads_t(nm("do_heads"), d_attn_o, tq), sv["oT"], sv["lse"],
                                sv["kk"], sv["kk"].transpose(0, 2, 1), sv["vv"], m_ctx)
    (dq_raw, dk_raw, dv_raw), _, (gw["qw"], gw["kw"]) = rowwise_bwd(
        nm("attn_prep"), _f_attn_prep,
        [_pcol(p, "q"), _pcol(p, "k"), _pcol(p, "v"), tabs["cq"], tabs["sq"], tabs["ck"], tabs["sk"]], [],
        [lw["qw"], lw["kw"], tabs["gq"], tabs["gk"]],
        [from_heads_t(nm("dq_rows"), dq_t, F32), _merge_kv(dk_s), _merge_kv(dv_s)],
        [True, True, True, False, False, False, False], [True, True, False, False], [CDT] * 3, t, nb)
    (dy_ssd, dxs_fin, dz), _, (gw["d_exp"], gw["ssd_nw"]) = rowwise_bwd(
        nm("ssd_fin"), _f_ssd_finish, [sv["yf"], sv["yb"], Cols(sv["xbc"], 0, 512), _pcol(p, "z")], [],
        [lw["d_exp"], lw["ssd_nw"]], [d_ssd_o], [True, False, True, True], [True, True], [F32, F32, CDT], t, nb)
    ssd_sh = [lw["dt_bias"], lw["a_log"]]
    post_ssd = lambda d: [jnp.concatenate(d[0:8], axis=1), d[8]]
    (dxbc_f, ddt_f), dsh_f = scan_bwd(nm("ssd_f"), _make_ssd_chunk(0), _ssd_rows(sv["xbc"], p), ssd_sh, sv["sf"],
                                      dy_ssd, post_ssd, [(1024, F32), (LANES, F32)], [], 4, t, m_ctx, False)
    (dxbc, ddt), dsh_b = scan_bwd(nm("ssd_b"), _make_ssd_chunk(1), _ssd_rows(sv["xbc"], p), ssd_sh, sv["sb"],
                                  dy_ssd, post_ssd, [(1024, F32), (LANES, CDT)], [dxbc_f, ddt_f], 4, t, m_ctx, True)
    gw["dt_bias"] = dsh_f[0] + dsh_b[0]
    gw["a_log"] = dsh_f[1] + dsh_b[1]
    dxbc_raw, gw["conv_w"], gw["conv_b"] = conv_bwd(_pcol(p, "xbc"), lw["conv_w"], lw["conv_b"], dxbc, dxs_fin, m_ctx)
    (dy_ret, drg), _, (gw["ret_gw"],) = rowwise_bwd(
        nm("ret_fin"), _f_ret_finish, [sv["rf"], sv["rb"], _pcol(p, "rg")], [], [lw["ret_gw"]], [d_ret_o],
        [True, False, True], [True], [F32, CDT], t, nb)
    post_ret = lambda d: [jnp.concatenate(d[0:4], axis=1), jnp.concatenate(d[4:8], axis=1),
                          jnp.concatenate(d[8:12], axis=1)]
    rrows = _ret_rows(sv["rq"], sv["rk"], p)
    r3 = [(512, F32)] * 3
    part, dlg_f = scan_bwd(nm("ret_f"), _make_ret_chunk(0), rrows, [lw["ret_lg"]], sv["rsf"], dy_ret, post_ret, r3,
                           [], 4, t, m_ctx, False)
    (drq_r, drk_r, drv), dlg_b = scan_bwd(nm("ret_b"), _make_ret_chunk(1), rrows, [lw["ret_lg"]], sv["rsb"], dy_ret,
                                          post_ret, [(512, F32), (512, F32), (512, CDT)], part, 4, t, m_ctx, True)
    gw["ret_lg"] = dlg_f[0] + dlg_b[0]
    (drq, drk), _, _ = rowwise_bwd(nm("ret_prep"), _f_ret_prep,
                                   [_pcol(p, "rq"), _pcol(p, "rk"), tabs["rc"], tabs["rs"]], [], [], [drq_r, drk_r],
                                   [True, True, False, False], [], [CDT, CDT], t, nb)
    pieces = {"gates": None, "xbc": dxbc_raw, "q": dq_raw, "z": dz, "rq": drq, "rk": drk, "rv": drv, "rg": drg,
              "k": dk_raw, "v": dv_raw, "dt": ddt}
    cols = list(dgl) + [pieces[n] for n in IN_NEW_ORDER[1:]]
    used = sum(c.shape[1] for c in cols)
    cols.append(jnp.zeros((t, IN_PAD - used), CDT))
    dp = jnp.concatenate(cols, axis=1)
    du = mm(nm("in_dx"), dp, lwt["w_in"], F32)
    gw["w_in"] = mm_tn(nm("in_dw"), sv["u"], dp)
    (dh_in,), (dsh1, dsc1), (gw["norm1_w"],) = rowwise_bwd(
        nm("norm1"), _f_norm_mod_thru, [sv["h"]], [sh1, sc1], [lw["norm1_w"]], [dh_a, du], [True], [True], [F32],
        t, nb)
    return dh_in, [dsh1, dsc1, dg1, dsh2, dsc2, dg2], gw


def _rope_tables(n_lat, m_ctx):
    rows = n_lat // GRID_W
    row = jnp.repeat(jnp.arange(rows, dtype=F32), GRID_W)
    col = jnp.tile(jnp.arange(GRID_W, dtype=F32), rows)
    nfreq = ATTN_HD // 4
    inv = ROPE_THETA ** (-jnp.arange(nfreq, dtype=F32) / nfreq)
    ang = jnp.concatenate([row[:, None] * inv, col[:, None] * inv], axis=-1)
    cos = jnp.concatenate([jnp.ones((m_ctx, ATTN_HD // 2), F32), jnp.cos(ang)], axis=0)
    sin = jnp.concatenate([jnp.zeros((m_ctx, ATTN_HD // 2), F32), jnp.sin(ang)], axis=0)
    c64 = jnp.concatenate([cos, cos], axis=1)
    s64 = jnp.concatenate([-sin, sin], axis=1)
    pos = jnp.arange(m_ctx + n_lat, dtype=F32)
    inv_r = ROPE_THETA ** (-jnp.linspace(0.0, 1.0, RET_DK // 2, dtype=F32))
    ang_r = pos[:, None] * inv_r
    rc = jnp.concatenate([jnp.cos(ang_r)] * 2, axis=1)
    rs = jnp.concatenate([-jnp.sin(ang_r), jnp.sin(ang_r)], axis=1)
    return dict(cq=jnp.tile(c64, (1, 8)), sq=jnp.tile(s64, (1, 8)), ck=jnp.tile(c64, (1, 2)), sk=jnp.tile(s64, (1, 2)),
                rc=jnp.tile(rc, (1, 4)), rs=jnp.tile(rs, (1, 4)), gq=_group_matrix(512, ATTN_HD),
                gk=_group_matrix(128, ATTN_HD))


def _layer_weights(full, small, layer):
    lw = dict(
        w_in=_in_to_padded(full["w_in"][layer]), w_branch=full["w_branch"][layer], w_out=full["w_out"][layer],
        w_mlp1=full["w_mlp1"][layer], w_mlp2=full["w_mlp2"][layer],
        norm1_w=small["norm1_w"][layer][None], norm2_w=small["norm2_w"][layer][None],
        qw=jnp.tile(small["attn_q_norm"][layer], 8)[None], kw=jnp.tile(small["attn_k_norm"][layer], 2)[None],
        conv_w=small["ssd_conv_w"][layer], conv_b=small["ssd_conv_b"][layer][None],
        dt_bias=_lane_pad(small["ssd_dt_bias"][layer]), a_log=_lane_pad(small["ssd_a_log"][layer]),
        d_exp=jnp.repeat(small["ssd_d"][layer], SSD_HD)[None], ssd_nw=small["ssd_norm_w"][layer][None],
        ret_lg=_lane_pad(small["ret_log_decay"][layer]), ret_gw=small["ret_gn_w"][layer][None])
    lwt = dict(w_in=lw["w_in"].T, w_branch=jnp.swapaxes(lw["w_branch"], 1, 2), w_out=lw["w_out"].T,
               w_mlp1=lw["w_mlp1"].T, w_mlp2=lw["w_mlp2"].T)
    return lw, lwt


def local_step(x, c, ctx, full, small, loss_target):
    n_lat, d = x.shape
    m_ctx = ctx.shape[0]
    t = n_lat + m_ctx
    depth = small["norm1_w"].shape[0]
    tabs = _rope_tables(n_lat, m_ctx)
    h = jnp.concatenate([ctx, x], axis=0)
    cc = jnp.concatenate([small["c_ctx"][None], c, jnp.zeros((COND_ROWS - 2, d), F32)], axis=0)
    (scc,) = rowwise_fwd("cond_silu", _f_silu, [cc], [], [], [(d, CDT)], COND_ROWS, 0)
    mods, saved, lws = [], [], []
    for layer in range(depth):
        lw, lwt = _layer_weights(full, small, layer)
        mod_raw = mm(f"l{layer}_mod", scc, full["w_mod"][layer], F32)
        (mod8,) = rowwise_fwd(f"l{layer}_mod_bias", _f_bias, [mod_raw], [], [small["b_mod"][layer][None]],
                              [(6 * d, F32)], COND_ROWS, 0)
        mod = [mod8[0:2, k * d:(k + 1) * d].reshape(2, 1, d) for k in range(6)]
        h, sv = layer_fwd(layer, h, mod, lw, tabs, m_ctx)
        mods.append(mod)
        saved.append(sv)
        lws.append((lw, lwt))
    loss, dh, d_final = loss_head(h, loss_target, small["final_norm_w"][None], m_ctx)

    gbig = {k: [None] * depth for k in BIG}
    gs = {k: [None] * depth for k in SMALL if k not in ("c_ctx", "final_norm_w")}
    d_scc = None
    for layer in reversed(range(depth)):
        lw, lwt = lws[layer]
        dh, dmod, gw = layer_bwd(layer, dh, saved[layer], mods[layer], lw, lwt, tabs, m_ctx)
        dmod8 = jnp.concatenate([jnp.concatenate([g_.reshape(2, d) for g_ in dmod], axis=1),
                                 jnp.zeros((COND_ROWS - 2, 6 * d), F32)], axis=0)
        (dmod_c,), _, (db_mod,) = rowwise_bwd(f"l{layer}_mod_bias_bwd", _f_bias, [dmod8], [],
                                              [small["b_mod"][layer][None]], [dmod8], [True], [True], [CDT], COND_ROWS, 0)
        gbig["w_mod"][layer] = mm_tn(f"l{layer}_mod_dw", scc, dmod_c)
        part = mm(f"l{layer}_mod_dx", dmod_c, full["w_mod"][layer].T, F32)
        d_scc = part if d_scc is None else d_scc + part
        gbig["w_in"][layer] = _in_from_padded(gw["w_in"])
        for k in ("w_branch", "w_out", "w_mlp1", "w_mlp2"):
            gbig[k][layer] = gw[k]
        gs["b_mod"][layer] = db_mod.reshape(-1)
        gs["norm1_w"][layer] = gw["norm1_w"].reshape(-1)
        gs["norm2_w"][layer] = gw["norm2_w"].reshape(-1)
        gs["attn_q_norm"][layer] = gw["qw"].reshape(8, ATTN_HD).sum(0)
        gs["attn_k_norm"][layer] = gw["kw"].reshape(2, ATTN_HD).sum(0)
        gs["ssd_conv_w"][layer] = gw["conv_w"]
        gs["ssd_conv_b"][layer] = gw["conv_b"].reshape(-1)
        gs["ssd_dt_bias"][layer] = gw["dt_bias"][0, :16].reshape(2, 8)
        gs["ssd_a_log"][layer] = gw["a_log"][0, :16].reshape(2, 8)
        gs["ssd_d"][layer] = gw["d_exp"].reshape(SSD_HEADS, SSD_HD).sum(1)
        gs["ssd_norm_w"][layer] = gw["ssd_nw"].reshape(-1)
        gs["ret_log_decay"][layer] = gw["ret_lg"][0, :8].reshape(2, 4)
        gs["ret_gn_w"][layer] = gw["ret_gw"].reshape(-1)
    (d_cc,), _, _ = rowwise_bwd("cond_silu_bwd", _f_silu, [cc], [], [], [d_scc], [True], [], [F32], COND_ROWS, 0)
    g_small = {k: jnp.stack(v) for k, v in gs.items()}
    g_small["c_ctx"] = d_cc[0]
    g_small["final_norm_w"] = d_final.reshape(-1)
    g_big = {k: jnp.stack(v) for k, v in gbig.items()}
    return loss, dh[m_ctx:], g_big, g_small


def _shards(a, axis):
    return jnp.split(a, 4, axis=axis)


def kernel(x, c, ctx, c_ctx, w_mod, b_mod, norm1_w, norm2_w, w_in, attn_q_norm, attn_k_norm, ssd_conv_w, ssd_conv_b, ssd_dt_bias, ssd_a_log, ssd_d, ssd_norm_w, ret_log_decay, ret_gn_w, w_branch, w_out, w_mlp1, w_mlp2, final_norm_w, loss_target, m_c_ctx, m_w_mod, m_b_mod, m_norm1_w, m_norm2_w, m_w_in, m_attn_q_norm, m_attn_k_norm, m_ssd_conv_w, m_ssd_conv_b, m_ssd_dt_bias, m_ssd_a_log, m_ssd_d, m_ssd_norm_w, m_ret_log_decay, m_ret_gn_w, m_w_branch, m_w_out, m_w_mlp1, m_w_mlp2, m_final_norm_w, v_c_ctx, v_w_mod, v_b_mod, v_norm1_w, v_norm2_w, v_w_in, v_attn_q_norm, v_attn_k_norm, v_ssd_conv_w, v_ssd_conv_b, v_ssd_dt_bias, v_ssd_a_log, v_ssd_d, v_ssd_norm_w, v_ret_log_decay, v_ret_gn_w, v_w_branch, v_w_out, v_w_mlp1, v_w_mlp2, v_final_norm_w):
    env = dict(locals())
    w_loc = {k: env[k] for k in WEIGHTS}
    m_loc = {k: env["m_" + k] for k in WEIGHTS}
    v_loc = {k: env["v_" + k] for k in WEIGHTS}
    chip = 2 * lax.axis_index("x") + lax.axis_index("y")
    core = lax.axis_index("c")

    big_shapes = [w_loc[k].shape for k in BIG]
    gathered = share_gathered("share_weights",
                              gather_chips("gather_weights", _pack_flat([w_loc[k] for k in BIG], CDT, 1024, PACK_ROWS)))
    per_chip = [_unpack_flat(gathered[j], big_shapes) for j in range(4)]
    full = {k: jnp.concatenate([per_chip[j][i] for j in range(4)], axis=BIG_AXIS[k]) for i, k in enumerate(BIG)}

    cw = w_loc["ssd_conv_w"]
    cw_w = cw.shape[-1]
    placed = lax.dynamic_update_slice(jnp.zeros(cw.shape[:-1] + (4 * cw_w,), F32),
                                      cw * (core == 0).astype(F32), (0, 0, chip * cw_w))
    conv_full = _unpack_flat(allreduce_small("gather_conv_w", _pack_flat([placed], F32, LANES)), [placed.shape])[0]
    small = {k: w_loc[k] for k in SMALL}
    small["ssd_conv_w"] = conv_full

    loss_l, grad_x, g_big, g_small = local_step(x[0], c, ctx[0], full, small, loss_target[0])

    small_shapes = [g_small[k].shape for k in SMALL] + [(LANES,)]
    summed = _unpack_flat(allreduce_small("reduce_small", _pack_flat([g_small[k] for k in SMALL] + [loss_l], F32, LANES)),
                          small_shapes)
    gsum = dict(zip(SMALL, summed[:-1]))
    loss = summed[-1][0]
    gsum["ssd_conv_w"] = lax.dynamic_slice(gsum["ssd_conv_w"], (0, 0, chip * cw_w), cw.shape)

    pieces = jnp.stack([_pack_flat([_shards(g_big[k], BIG_AXIS[k])[j] for k in BIG], CDT, 1024, PACK_ROWS) for j in range(4)])
    pair = add_halves("add_cores", pieces, exchange_halves("exchange_halves", pieces))
    chip_sum = sum_parts("sum_chips", scatter_chips("scatter_grads", pair))
    both = exchange_sums("exchange_sums", chip_sum)
    g_sum = _unpack_flat(both, big_shapes)

    grads, deltas, new_m, new_v = {}, {}, {}, {}
    for i, k in enumerate(BIG):
        shp = w_loc[k].shape
        two_d = lambda a, shp=shp: a.reshape(-1, shp[-1])
        res = adamw("adamw_" + k, two_d(w_loc[k]), two_d(m_loc[k]), two_d(v_loc[k]), [two_d(g_sum[i])])
        grads[k], deltas[k], new_m[k], new_v[k] = [r.reshape(shp) for r in res]
    small_loc_shapes = [w_loc[k].shape for k in SMALL]
    res = adamw("adamw_small", _pack_flat([w_loc[k] for k in SMALL], F32, LANES),
                _pack_flat([m_loc[k] for k in SMALL], F32, LANES), _pack_flat([v_loc[k] for k in SMALL], F32, LANES),
                [_pack_flat([gsum[k] for k in SMALL], F32, LANES)])
    for dst, r in zip((grads, deltas, new_m, new_v), res):
        dst.update(dict(zip(SMALL, _unpack_flat(r, small_loc_shapes))))

    return (loss, grad_x[None], *[grads[k] for k in WEIGHTS], *[deltas[k] for k in WEIGHTS],
            *[new_m[k] for k in WEIGHTS], *[new_v[k] for k in WEIGHTS])
```

```python
import functools
import math
from typing import NamedTuple

import jax
import jax.numpy as jnp
from jax import lax
from jax.experimental import pallas as pl
from jax.experimental.pallas import tpu as pltpu

F32 = jnp.float32
CDT = jnp.bfloat16
NORM_EPS = 1e-6
ROPE_THETA = 10000.0
GRID_W = 64
D_MODEL = 1024
ATTN_HEADS, ATTN_KV, ATTN_HD = 8, 2, 64
SSD_HEADS, SSD_HD, SSD_STATE = 8, 64, 128
RET_HEADS, RET_DK = 4, 128
CHUNK = 128
ROW_TILE = 256
MM_ROWS = 768
MM_VMEM_BUDGET = 44 * 1024 * 1024
ATTN_TQ, ATTN_TK = 256, 256
ATTN_TK_FWD = 512
LANES = 128
PACK_ROWS = 512
PAIR_ROWS = 1024
COND_ROWS = 16
VMEM_LIMIT = 56 * 1024 * 1024

ADAM_LR, ADAM_B1, ADAM_B2, ADAM_EPS, ADAM_WD, ADAM_STEP = 0.001, 0.9, 0.999, 1e-08, 0.01, 10

IN_LAYOUT = {
    "gates": (0, 3072, 4368, 3072), "xbc": (3072, 1024, 1280, 1024), "q": (4096, 512, 0, 512),
    "z": (4608, 512, 768, 512), "rq": (5120, 512, 2320, 512), "rk": (5632, 512, 2832, 512),
    "rv": (6144, 512, 3344, 512), "rg": (6656, 512, 3856, 512), "k": (7168, 128, 512, 128),
    "v": (7296, 128, 640, 128), "dt": (7424, 128, 2304, 16),
}
IN_PAD = 7680
IN_ORIG_ORDER = ("q", "k", "v", "z", "xbc", "dt", "rq", "rk", "rv", "rg", "gates")
IN_NEW_ORDER = ("gates", "xbc", "q", "z", "rq", "rk", "rv", "rg", "k", "v", "dt")

BIG = ("w_mod", "w_in", "w_branch", "w_out", "w_mlp1", "w_mlp2")
BIG_AXIS = {"w_mod": 2, "w_in": 2, "w_branch": 3, "w_out": 1, "w_mlp1": 2, "w_mlp2": 1}
SMALL = ("c_ctx", "b_mod", "norm1_w", "norm2_w", "attn_q_norm", "attn_k_norm", "ssd_conv_w", "ssd_conv_b",
         "ssd_dt_bias", "ssd_a_log", "ssd_d", "ssd_norm_w", "ret_log_decay", "ret_gn_w", "final_norm_w")
WEIGHTS = ("c_ctx", "w_mod", "b_mod", "norm1_w", "norm2_w", "w_in", "attn_q_norm", "attn_k_norm", "ssd_conv_w",
           "ssd_conv_b", "ssd_dt_bias", "ssd_a_log", "ssd_d", "ssd_norm_w", "ret_log_decay", "ret_gn_w",
           "w_branch", "w_out", "w_mlp1", "w_mlp2", "final_norm_w")


def _cp(sem):
    return pltpu.CompilerParams(dimension_semantics=sem, vmem_limit_bytes=VMEM_LIMIT)


class Cols(NamedTuple):
    arr: jax.Array
    off: int
    width: int


def _width(item):
    return item.width if isinstance(item, Cols) else item.shape[1]


def _row_in(item, rows, imap=None):
    imap = imap or (lambda i: i)
    if isinstance(item, Cols):
        assert item.off % item.width == 0
        blk = item.off // item.width
        return item.arr, pl.BlockSpec((rows, item.width), lambda i, blk=blk: (imap(i), blk))
    return item, pl.BlockSpec((rows, item.shape[1]), lambda i: (imap(i), 0))


def _const_spec(shape):
    return pl.BlockSpec(shape, lambda *_: (0,) * len(shape))


def _mxu(a, b, dims=(((1,), (0,)), ((), ()))):
    return lax.dot_general(a.astype(CDT), b.astype(CDT), dims, preferred_element_type=F32)


_NT = (((1,), (1,)), ((), ()))
_TN = (((0,), (0,)), ((), ()))


@jax.custom_vjp
def _softplus(x):
    return jnp.maximum(x, 0.0) + jnp.log1p(jnp.exp(-jnp.abs(x)))


def _softplus_fwd(x):
    return _softplus(x), x


def _softplus_bwd(x, g):
    return (g * jax.nn.sigmoid(x),)


_softplus.defvjp(_softplus_fwd, _softplus_bwd)


def _group_mean_impl(x, gmat):
    hi = x.astype(CDT)
    lo = (x - hi.astype(F32)).astype(CDT)
    return (jnp.dot(hi, gmat, preferred_element_type=F32) + jnp.dot(lo, gmat, preferred_element_type=F32))


@jax.custom_vjp
def _group_mean(x, gmat):
    return _group_mean_impl(x, gmat)


def _group_mean_fwd(x, gmat):
    return _group_mean_impl(x, gmat), gmat


def _group_mean_bwd(gmat, g):
    return _group_mean_impl(g, gmat), jnp.zeros_like(gmat)


_group_mean.defvjp(_group_mean_fwd, _group_mean_bwd)


def _group_matrix(width, group):
    r = jnp.arange(width) // group
    return jnp.where(r[:, None] == r[None, :], 1.0 / group, 0.0).astype(CDT)


def _make_rope(half):
    def partner(x):
        w = x.shape[1]
        lane = lax.broadcasted_iota(jnp.int32, x.shape, 1)
        first = (lane % (2 * half)) < half
        return jnp.where(first, pltpu.roll(x, w - half, axis=1), pltpu.roll(x, half, axis=1))

    def impl(x, cos_full, sin_signed):
        return x * cos_full + partner(x) * sin_signed

    @jax.custom_vjp
    def rope(x, cos_full, sin_signed):
        return impl(x, cos_full, sin_signed)

    def fwd(x, cos_full, sin_signed):
        return impl(x, cos_full, sin_signed), (cos_full, sin_signed)

    def bwd(res, g):
        cos_full, sin_signed = res
        return impl(g, cos_full, -sin_signed), jnp.zeros_like(cos_full), jnp.zeros_like(sin_signed)

    rope.defvjp(fwd, bwd)
    return rope


_rope32 = _make_rope(32)
_rope64 = _make_rope(64)


def _rms(x, w):
    return x * lax.rsqrt(jnp.mean(x * x, axis=-1, keepdims=True) + NORM_EPS) * w


def _col(v, lane_index):
    lane = lax.broadcasted_iota(jnp.int32, v.shape, 1)
    return jnp.sum(jnp.where(lane == lane_index, v, 0.0), axis=1, keepdims=True)


def _typed_spec(width, nb_ctx):
    return pl.BlockSpec((None, 1, width), lambda i: (jnp.where(i >= nb_ctx, 1, 0), 0, 0))


def rowwise_fwd(name, f, rows, typed, shared, outs, n_rows, nb_ctx, tm=ROW_TILE):
    tm = min(tm, n_rows)
    nin = len(rows) + len(typed) + len(shared)

    def body(*refs):
        res = f(*[r[...] for r in refs[:nin]])
        for o_ref, o in zip(refs[nin:], res):
            o_ref[...] = o.astype(o_ref.dtype)

    arrs, specs = [], []
    for it in rows:
        a, s = _row_in(it, tm)
        arrs.append(a)
        specs.append(s)
    for t in typed:
        arrs.append(t)
        specs.append(_typed_spec(t.shape[-1], nb_ctx))
    for s_ in shared:
        arrs.append(s_)
        specs.append(_const_spec(s_.shape))
    res = pl.pallas_call(
        body, name=name, grid=(n_rows // tm,), in_specs=specs,
        out_specs=[pl.BlockSpec((tm, w), lambda i: (i, 0)) for w, _ in outs],
        out_shape=[jax.ShapeDtypeStruct((n_rows, w), dt) for w, dt in outs],
        compiler_params=_cp(("parallel",)))(*arrs)
    return res


def rowwise_bwd(name, f, rows, typed, shared, cots, row_diff, shared_diff, drow_dtypes, n_rows, nb_ctx, tm=ROW_TILE):
    tm = min(tm, n_rows)
    nr, nt, ns, nc = len(rows), len(typed), len(shared), len(cots)
    nin = nr + nt + ns
    d_rows = [k for k in range(nr) if row_diff[k]]
    d_sh = [k for k in range(ns) if shared_diff[k]]

    def body(*refs):
        rvals = [r[...] for r in refs[:nr]]
        tvals = [r[...] for r in refs[nr:nr + nt]]
        svals = [r[...] for r in refs[nr + nt:nin]]
        cvals = [r[...].astype(F32) for r in refs[nin:nin + nc]]
        out_refs = refs[nin + nc:]

        def g(*dv):
            dv = list(dv)
            rv = list(rvals)
            for k in d_rows:
                rv[k] = dv.pop(0)
            tv = [dv.pop(0) for _ in range(nt)]
            sv = list(svals)
            for k in d_sh:
                sv[k] = dv.pop(0)
            return tuple(o.astype(F32) for o in f(*rv, *tv, *sv))

        prim = [rvals[k].astype(F32) for k in d_rows] + tvals + [svals[k] for k in d_sh]
        _, vjp = jax.vjp(g, *prim)
        grads = list(vjp(tuple(cvals)))
        i = pl.program_id(0)
        for ref in out_refs[:len(d_rows)]:
            ref[...] = grads.pop(0).astype(ref.dtype)
        first_typed = (i == 0) | (i == nb_ctx)
        for ref in out_refs[len(d_rows):len(d_rows) + nt]:
            gr = grads.pop(0)

            @pl.when(first_typed)
            def _(ref=ref, gr=gr):
                ref[...] = gr

            @pl.when(jnp.logical_not(first_typed))
            def _(ref=ref, gr=gr):
                ref[...] += gr
        for ref in out_refs[len(d_rows) + nt:]:
            gr = grads.pop(0)

            @pl.when(i == 0)
            def _(ref=ref, gr=gr):
                ref[...] = gr

            @pl.when(i != 0)
            def _(ref=ref, gr=gr):
                ref[...] += gr

    arrs, specs = [], []
    for it in list(rows):
        a, s = _row_in(it, tm)
        arrs.append(a)
        specs.append(s)
    for t in typed:
        arrs.append(t)
        specs.append(_typed_spec(t.shape[-1], nb_ctx))
    for s_ in shared:
        arrs.append(s_)
        specs.append(_const_spec(s_.shape))
    for c_ in cots:
        a, s = _row_in(c_, tm)
        arrs.append(a)
        specs.append(s)
    out_specs, out_shape = [], []
    for k, dt in zip(d_rows, drow_dtypes):
        w = _width(rows[k])
        out_specs.append(pl.BlockSpec((tm, w), lambda i: (i, 0)))
        out_shape.append(jax.ShapeDtypeStruct((n_rows, w), dt))
    for t in typed:
        out_specs.append(_typed_spec(t.shape[-1], nb_ctx))
        out_shape.append(jax.ShapeDtypeStruct(t.shape, F32))
    for k in d_sh:
        out_specs.append(_const_spec(shared[k].shape))
        out_shape.append(jax.ShapeDtypeStruct(shared[k].shape, F32))
    res = pl.pallas_call(body, name=name, grid=(n_rows // tm,), in_specs=specs, out_specs=out_specs,
                         out_shape=out_shape, compiler_params=_cp(("arbitrary",)))(*arrs)
    n1, n2 = len(d_rows), len(d_rows) + nt
    return list(res[:n1]), list(res[n1:n2]), list(res[n2:])


def _pick(n, prefs):
    for p in prefs:
        if n % p == 0:
            return p
    return n


def mm(name, a, b, out_dtype):
    k, n = b.shape
    m = (a.arr if isinstance(a, Cols) else a).shape[0]
    assert _width(a) == k
    tm = _pick(m, (MM_ROWS, 256))
    osz = jnp.dtype(out_dtype).itemsize
    tn = next(c for c in (2560, 2048, 1536, 1024, 512, 256, 128, n)
              if n % c == 0 and 2 * (tm * k * 2 + k * c * 2 + tm * c * osz) <= MM_VMEM_BUDGET or c == n)

    def body(a_ref, b_ref, o_ref):
        o_ref[...] = jnp.dot(a_ref[...], b_ref[...], preferred_element_type=F32).astype(o_ref.dtype)

    a_arr, a_spec = _row_in(a, tm)
    a_spec = pl.BlockSpec(a_spec.block_shape, lambda j, i, f=a_spec.index_map: f(i))
    return pl.pallas_call(
        body, name=name, grid=(n // tn, m // tm),
        in_specs=[a_spec, pl.BlockSpec((k, tn), lambda j, i: (0, j))],
        out_specs=pl.BlockSpec((tm, tn), lambda j, i: (i, j)),
        out_shape=jax.ShapeDtypeStruct((m, n), out_dtype),
        compiler_params=_cp(("parallel", "parallel")))(a_arr, b)


def mm_tn(name, a, b):
    t = (a.arr if isinstance(a, Cols) else a).shape[0]
    k, n = _width(a), _width(b)
    tt = _pick(t, (MM_ROWS, 256))
    tk = _pick(k, (1024, 512, 256, 128))
    tn = _pick(n, (1280, 1024, 512, 256, 128))

    def body(a_ref, b_ref, o_ref):
        part = lax.dot_general(a_ref[...], b_ref[...], _TN, preferred_element_type=F32)

        @pl.when(pl.program_id(2) == 0)
        def _():
            o_ref[...] = part

        @pl.when(pl.program_id(2) != 0)
        def _():
            o_ref[...] += part

    def win(item, width):
        if isinstance(item, Cols):
            assert item.off % width == 0
            return item.arr, item.off // width
        return item, 0

    a_arr, a0 = win(a, tk)
    b_arr, b0 = win(b, tn)
    return pl.pallas_call(
        body, name=name, grid=(k // tk, n // tn, t // tt),
        in_specs=[pl.BlockSpec((tt, tk), lambda ki, ni, ti: (ti, a0 + ki)),
                  pl.BlockSpec((tt, tn), lambda ki, ni, ti: (ti, b0 + ni))],
        out_specs=pl.BlockSpec((tk, tn), lambda ki, ni, ti: (ki, ni)),
        out_shape=jax.ShapeDtypeStruct((k, n), F32),
        compiler_params=_cp(("parallel", "parallel", "arbitrary")))(a_arr, b_arr)


def _heads_t(rows_blk):
    blk = rows_blk.astype(F32).T
    return jnp.concatenate([blk[hh * ATTN_HD:(hh + 1) * ATTN_HD, :] for hh in range(4)], axis=1)


def _heads_rows(t_blk):
    tq = t_blk.shape[1] // 4
    return jnp.concatenate([t_blk[:, hh * tq:(hh + 1) * tq] for hh in range(4)], axis=0).T


def attn_fwd(name, q, kk, vT, m_ctx, tq):
    t, hd = kk.shape[1], ATTN_HD
    nq, r = t // tq, 4 * tq
    tk = _pick(t - m_ctx, (ATTN_TK_FWD, ATTN_TK))
    nqc, n_lat_tiles = m_ctx // tq, (t - m_ctx) // tk

    def body(q_ref, k_ref, vT_ref, o_ref, qT_ref, oT_ref, lse_ref):
        i = pl.program_id(1)
        q_t = _heads_t(q_ref[...]).astype(CDT)
        qT_ref[...] = q_t

        def tile(off, size, carry):
            mi, li, acc = carry
            st = jnp.dot(k_ref[pl.ds(off, size), :], q_t, preferred_element_type=F32)
            mn = jnp.maximum(mi, jnp.max(st, axis=0, keepdims=True))
            pt = jnp.exp(st - mn)
            al = jnp.exp(mi - mn)
            li = al * li + jnp.sum(pt, axis=0, keepdims=True)
            acc = al * acc + jnp.dot(vT_ref[:, pl.ds(off, size)], pt.astype(CDT), preferred_element_type=F32)
            return mn, li, acc

        init = (jnp.full((1, r), -1e30, F32), jnp.zeros((1, r), F32), jnp.zeros((hd, r), F32))
        carry = tile(0, m_ctx, init)
        mi, li, acc = lax.fori_loop(
            0, jnp.where(i < nqc, 0, n_lat_tiles),
            lambda j, cr: tile(pl.multiple_of(m_ctx + j * tk, ATTN_TK), tk, cr), carry)
        o_t = acc / li
        oT_ref[...] = o_t.astype(oT_ref.dtype)
        o_ref[...] = _heads_rows(o_t).astype(o_ref.dtype)
        lse_ref[...] = mi + jnp.log(li)

    blk_t = pl.BlockSpec((None, None, hd, r), lambda g, i: (g, i, 0, 0))
    rows = pl.BlockSpec((tq, 4 * hd), lambda g, i: (i, g))
    return pl.pallas_call(
        body, name=name, grid=(2, nq),
        in_specs=[rows, pl.BlockSpec((None, t, hd), lambda g, i: (g, 0, 0)),
                  pl.BlockSpec((None, hd, t), lambda g, i: (g, 0, 0))],
        out_specs=[rows, blk_t, blk_t, pl.BlockSpec((None, None, 1, r), lambda g, i: (g, i, 0, 0))],
        out_shape=[jax.ShapeDtypeStruct((t, 8 * hd), CDT), jax.ShapeDtypeStruct((2, nq, hd, r), CDT),
                   jax.ShapeDtypeStruct((2, nq, hd, r), CDT), jax.ShapeDtypeStruct((2, nq, 1, r), F32)],
        compiler_params=_cp(("parallel", "arbitrary")))(q, kk, vT)


def attn_bwd(name, qT, do, oT, lse, kk, kT, vv, m_ctx):
    _, nq, hd, r = qT.shape
    t = kk.shape[1]
    tq, tk = r // 4, ATTN_TK
    nqc, nkc, nk = m_ctx // tq, m_ctx // tk, t // tk

    def body(qT_ref, do_ref, oT_ref, lse_ref, k_ref, kT_ref, v_ref,
             dq_ref, dk_ref, dv_ref, dk_acc, dv_acc, delta_s, doT_s):
        j, i = pl.program_id(1), pl.program_id(2)

        @pl.when(i == 0)
        def _():
            dk_acc[...] = jnp.zeros_like(dk_acc)
            dv_acc[...] = jnp.zeros_like(dv_acc)

        @pl.when(j == 0)
        def _():
            do_t = _heads_t(do_ref[...])
            doT_s[i] = do_t.astype(CDT)
            delta_s[i] = jnp.sum(do_t * oT_ref[...].astype(F32), axis=0, keepdims=True)

        @pl.when((j < nkc) | (i >= nqc))
        def _():
            do_t = doT_s[i]
            st = jnp.dot(k_ref[...], qT_ref[...], preferred_element_type=F32)
            pt = jnp.exp(st - lse_ref[...])
            dv_acc[...] += lax.dot_general(pt.astype(CDT), do_t, _NT, preferred_element_type=F32)
            dpt = jnp.dot(v_ref[...], do_t, preferred_element_type=F32)
            dst = (pt * (dpt - delta_s[i])).astype(CDT)
            dk_acc[...] += lax.dot_general(dst, qT_ref[...], _NT, preferred_element_type=F32)
            part = jnp.dot(kT_ref[...], dst, preferred_element_type=F32)

            @pl.when(j == 0)
            def _():
                dq_ref[i] = part

            @pl.when(j != 0)
            def _():
                dq_ref[i] += part

        @pl.when(i == nq - 1)
        def _():
            dk_ref[...] = dk_acc[...]
            dv_ref[...] = dv_acc[...]

    blk_t = pl.BlockSpec((None, None, hd, r), lambda g, j, i: (g, i, 0, 0))
    row = pl.BlockSpec((None, None, 1, r), lambda g, j, i: (g, i, 0, 0))
    kv = pl.BlockSpec((None, tk, hd), lambda g, j, i: (g, j, 0))
    kv_t = pl.BlockSpec((None, hd, tk), lambda g, j, i: (g, 0, j))
    rows = pl.BlockSpec((tq, 4 * hd), lambda g, j, i: (i, g))
    return pl.pallas_call(
        body, name=name, grid=(2, nk, nq),
        in_specs=[blk_t, rows, blk_t, row, kv, kv_t, kv],
        out_specs=[pl.BlockSpec((None, nq, hd, r), lambda g, j, i: (g, 0, 0, 0)), kv, kv],
        out_shape=[jax.ShapeDtypeStruct(qT.shape, F32), jax.ShapeDtypeStruct(kk.shape, F32),
                   jax.ShapeDtypeStruct(kk.shape, F32)],
        scratch_shapes=[pltpu.VMEM((tk, hd), F32), pltpu.VMEM((tk, hd), F32), pltpu.VMEM((nq, 1, r), F32),
                        pltpu.VMEM((nq, hd, r), CDT)],
        compiler_params=_cp(("parallel", "arbitrary", "arbitrary")))(qT, do, oT, lse, kk, kT, vv)


def from_heads_t(name, a_t, out_dtype):
    _, nq, hd, r = a_t.shape

    def body(a_ref, o_ref):
        o_ref[...] = _heads_rows(a_ref[...].astype(F32)).astype(o_ref.dtype)

    tq = r // 4
    return pl.pallas_call(
        body, name=name, grid=(2, nq), in_specs=[pl.BlockSpec((None, None, hd, r), lambda g, i: (g, i, 0, 0))],
        out_specs=pl.BlockSpec((tq, 4 * hd), lambda g, i: (i, g)),
        out_shape=jax.ShapeDtypeStruct((nq * tq, 8 * hd), out_dtype),
        compiler_params=_cp(("parallel", "parallel")))(a_t)


def _split_kv(a):
    return a.reshape(a.shape[0], 2, ATTN_HD).transpose(1, 0, 2)


def _merge_kv(a):
    return a.transpose(1, 0, 2).reshape(a.shape[1], 2 * ATTN_HD)


def _chunk_order(rev, ncc, nct):
    if not rev:
        return lambda s: s
    return lambda s: jnp.where(s < ncc, ncc - 1 - s, nct - 1 - (s - ncc))


def scan_fwd(name, fn, rows, shared, n_state, y_width, n_rows, m_ctx, rev):
    nct, ncc = n_rows // CHUNK, m_ctx // CHUNK
    order = _chunk_order(rev, ncc, nct)
    nr, ns = len(rows), len(shared)

    def body(*refs):
        rvals = [r[...] for r in refs[:nr]]
        svals = [r[...] for r in refs[nr:nr + ns]]
        y_ref, sin_ref, st = refs[nr + ns:]

        @pl.when(pl.program_id(0) == 0)
        def _():
            st[...] = jnp.zeros_like(st)

        prev = [st[k] for k in range(n_state)]
        sin_ref[...] = st[...]
        y, new = fn(rvals, svals, prev)
        y_ref[...] = y
        for k in range(n_state):
            st[k] = new[k]

    arrs, specs = [], []
    for it in rows:
        a, s = _row_in(it, CHUNK, order)
        arrs.append(a)
        specs.append(s)
    for s_ in shared:
        arrs.append(s_)
        specs.append(_const_spec(s_.shape))
    return pl.pallas_call(
        body, name=name, grid=(nct,), in_specs=specs,
        out_specs=[pl.BlockSpec((CHUNK, y_width), lambda s: (order(s), 0)),
                   pl.BlockSpec((None, n_state, LANES, LANES), lambda s: (order(s), 0, 0, 0))],
        out_shape=[jax.ShapeDtypeStruct((n_rows, y_width), F32),
                   jax.ShapeDtypeStruct((nct, n_state, LANES, LANES), F32)],
        scratch_shapes=[pltpu.VMEM((n_state, LANES, LANES), F32)],
        compiler_params=_cp(("arbitrary",)))(*arrs)


def scan_bwd(name, fn, rows, shared, states_in, dy, post, outs, addends, n_state, n_rows, m_ctx, rev):
    nct, ncc = n_rows // CHUNK, m_ctx // CHUNK
    fwd_order = _chunk_order(rev, ncc, nct)
    order = lambda r: fwd_order(nct - 1 - r)
    nr, ns, na = len(rows), len(shared), len(addends)
    nin = nr + ns

    def body(*refs):
        rvals = [r[...] for r in refs[:nr]]
        svals = [r[...] for r in refs[nr:nin]]
        sin_ref, dy_ref = refs[nin], refs[nin + 1]
        add_refs = refs[nin + 2:nin + 2 + na]
        out_refs = refs[nin + 2 + na:nin + 2 + na + len(outs)]
        dsh_refs = refs[nin + 2 + na + len(outs):-1]
        dst = refs[-1]
        r = pl.program_id(0)

        @pl.when(r == 0)
        def _():
            dst[...] = jnp.zeros_like(dst)

        prev = [sin_ref[k] for k in range(n_state)]
        _, vjp = jax.vjp(fn, rvals, svals, prev)
        d_rows, d_shared, d_prev = vjp((dy_ref[...], [dst[k] for k in range(n_state)]))
        res = post(d_rows)
        for k, (ref, val) in enumerate(zip(out_refs, res)):
            if k < na:
                val = val + add_refs[k][...]
            ref[...] = val.astype(ref.dtype)
        for ref, gr in zip(dsh_refs, d_shared):
            @pl.when(r == 0)
            def _(ref=ref, gr=gr):
                ref[...] = gr

            @pl.when(r != 0)
            def _(ref=ref, gr=gr):
                ref[...] += gr
        for k in range(n_state):
            dst[k] = d_prev[k]

    arrs, specs = [], []
    for it in rows:
        a, s = _row_in(it, CHUNK, order)
        arrs.append(a)
        specs.append(s)
    for s_ in shared:
        arrs.append(s_)
        specs.append(_const_spec(s_.shape))
    arrs.append(states_in)
    specs.append(pl.BlockSpec((None, n_state, LANES, LANES), lambda r: (order(r), 0, 0, 0)))
    for it in [dy] + list(addends):
        a, s = _row_in(it, CHUNK, order)
        arrs.append(a)
        specs.append(s)
    out_specs = [pl.BlockSpec((CHUNK, w), lambda r: (order(r), 0)) for w, _ in outs]
    out_shape = [jax.ShapeDtypeStruct((n_rows, w), dt) for w, dt in outs]
    for s_ in shared:
        out_specs.append(_const_spec(s_.shape))
        out_shape.append(jax.ShapeDtypeStruct(s_.shape, F32))
    res = pl.pallas_call(body, name=name, grid=(nct,), in_specs=specs, out_specs=out_specs, out_shape=out_shape,
                         scratch_shapes=[pltpu.VMEM((n_state, LANES, LANES), F32)],
                         compiler_params=_cp(("arbitrary",)))(*arrs)
    return list(res[:len(outs)]), list(res[len(outs):])


def _make_ssd_chunk(direction):
    rev = direction == 1
    base = 8 * direction

    def fn(rows, shared, prev):
        xs, bms, cms, dtraw = rows[0:4], rows[4:6], rows[6:8], rows[8]
        dt_bias, a_log = shared
        ln = dtraw.shape[0]
        dt_all = _softplus(dtraw + dt_bias)
        a_all = dt_all * (-jnp.exp(a_log))
        r_i = lax.broadcasted_iota(jnp.int32, (ln, ln), 0)
        c_i = lax.broadcasted_iota(jnp.int32, (ln, ln), 1)
        tri = (r_i <= c_i) if rev else (r_i >= c_i)
        a_cum_all = jnp.dot(tri.astype(F32), a_all, precision=lax.Precision.HIGHEST, preferred_element_type=F32)
        a_tot_all = jnp.sum(a_all, axis=0, keepdims=True)
        first = lax.broadcasted_iota(jnp.int32, (ln, LANES), 1) < SSD_HD
        first_row = lax.broadcasted_iota(jnp.int32, (LANES, 1), 0) < SSD_HD

        def lmat(acol):
            a_b = jnp.broadcast_to(acol, (ln, ln))
            seg = a_b - a_b.T
            return jnp.where(tri, jnp.exp(jnp.where(tri, seg, 0.0)), 0.0)

        ys, new = [], []
        for g in range(2):
            bm, cm = bms[g], cms[g]
            cb = _mxu(cm, bm, _NT)
            for jj in range(2):
                pr = 2 * g + jj
                h0, h1 = base + 2 * pr, base + 2 * pr + 1
                ac0, ac1 = _col(a_cum_all, h0), _col(a_cum_all, h1)
                at0, at1 = _col(a_tot_all, h0), _col(a_tot_all, h1)
                dt_pair = jnp.where(first, _col(dt_all, h0), _col(dt_all, h1))
                acum_pair = jnp.where(first, ac0, ac1)
                atot_pair = jnp.where(first[0:1], at0, at1)
                xd = xs[pr] * dt_pair
                st = _mxu(xd * jnp.exp(atot_pair - acum_pair), bm, _TN)
                new.append(prev[pr] * jnp.where(first_row, jnp.exp(at0), jnp.exp(at1)) + st)
                y0 = _mxu(cb * lmat(ac0), xd)
                y1 = _mxu(cb * lmat(ac1), xd)
                y_off = _mxu(cm, prev[pr], _NT) * jnp.exp(acum_pair)
                ys.append(jnp.where(first, y0, y1) + y_off)
        return jnp.concatenate(ys, axis=1), new

    return fn


def _make_ret_chunk(direction):
    rev = direction == 1
    base = 4 * direction

    def fn(rows, shared, prev):
        qs, ks, vs = rows[0:4], rows[4:8], rows[8:12]
        lg_all = -jnp.exp(shared[0])
        ln = qs[0].shape[0]
        pos = lax.broadcasted_iota(jnp.int32, (ln, 1), 0).astype(F32)
        r_i = lax.broadcasted_iota(jnp.int32, (ln, ln), 0)
        c_i = lax.broadcasted_iota(jnp.int32, (ln, ln), 1)
        diff = ((c_i - r_i) if rev else (r_i - c_i))
        mask = diff >= 0
        dpos = jnp.maximum(diff, 0).astype(F32)
        k_pow = pos if rev else (ln - 1.0 - pos)
        q_pow = (ln - pos) if rev else (pos + 1.0)
        ys, new = [], []
        for h in range(RET_HEADS):
            lg = _col(lg_all, base + h)
            dmat = jnp.where(mask, jnp.exp(dpos * lg), 0.0)
            st = _mxu(ks[h] * jnp.exp(k_pow * lg), vs[h], _TN)
            new.append(prev[h] * jnp.exp(ln * lg) + st)
            s = _mxu(qs[h], ks[h], _NT) * dmat
            ys.append(_mxu(s, vs[h]) + _mxu(qs[h], prev[h]) * jnp.exp(q_pow * lg))
        return jnp.concatenate(ys, axis=1), new

    return fn


def _conv_pre(x, w, b, t_idx, n_rows, m_ctx):
    is_start = (t_idx == 0) | (t_idx == m_ctx)
    is_end = (t_idx == m_ctx - 1) | (t_idx == n_rows - 1)
    xp = jnp.where(is_start, 0.0, pltpu.roll(x, 1, axis=0))
    xn = jnp.where(is_end, 0.0, pltpu.roll(x, n_rows - 1, axis=0))
    return w[0:1] * xp + w[1:2] * x + w[2:3] * xn + b, xp, xn, is_start, is_end


def conv_fwd(x, conv_w, conv_b, m_ctx):
    n_rows, width = x.arr.shape[0], x.width
    c0 = x.off // LANES

    def body(x_ref, w_ref, b_ref, o_ref):
        t_idx = lax.broadcasted_iota(jnp.int32, (n_rows, 1), 0)
        pre = _conv_pre(x_ref[...], w_ref[...], b_ref[...], t_idx, n_rows, m_ctx)[0]
        o_ref[...] = pre * jax.nn.sigmoid(pre)

    return pl.pallas_call(
        body, name="conv_fwd", grid=(width // LANES,),
        in_specs=[pl.BlockSpec((n_rows, LANES), lambda c: (0, c0 + c)),
                  pl.BlockSpec((3, LANES), lambda c: (0, c)), pl.BlockSpec((1, LANES), lambda c: (0, c))],
        out_specs=pl.BlockSpec((n_rows, LANES), lambda c: (0, c)),
        out_shape=jax.ShapeDtypeStruct((n_rows, width), F32),
        compiler_params=_cp(("parallel",)))(x.arr, conv_w, conv_b)


def conv_bwd(x, conv_w, conv_b, dy, dxs_extra, m_ctx):
    n_rows, width = x.arr.shape[0], x.width
    c0 = x.off // LANES
    n_extra = dxs_extra.shape[1] // LANES

    def body(x_ref, w_ref, b_ref, dy_ref, ex_ref, dx_ref, dw_ref, db_ref):
        c = pl.program_id(0)
        t_idx = lax.broadcasted_iota(jnp.int32, (n_rows, 1), 0)
        w = w_ref[...]
        pre, xp, xn, is_start, is_end = _conv_pre(x_ref[...], w, b_ref[...], t_idx, n_rows, m_ctx)
        sg = jax.nn.sigmoid(pre)
        dyv = dy_ref[...] + jnp.where(c < n_extra, ex_ref[...], 0.0)
        dpre = dyv * (sg * (1.0 + pre * (1.0 - sg)))
        d_next = jnp.where(is_end, 0.0, pltpu.roll(dpre, n_rows - 1, axis=0))
        d_prev = jnp.where(is_start, 0.0, pltpu.roll(dpre, 1, axis=0))
        dx_ref[...] = (w[1:2] * dpre + w[0:1] * d_next + w[2:3] * d_prev).astype(dx_ref.dtype)
        dw_ref[...] = jnp.concatenate([jnp.sum(dpre * xp, axis=0, keepdims=True),
                                       jnp.sum(dpre * x_ref[...], axis=0, keepdims=True),
                                       jnp.sum(dpre * xn, axis=0, keepdims=True)], axis=0)
        db_ref[...] = jnp.sum(dpre, axis=0, keepdims=True)

    return pl.pallas_call(
        body, name="conv_bwd", grid=(width // LANES,),
        in_specs=[pl.BlockSpec((n_rows, LANES), lambda c: (0, c0 + c)),
                  pl.BlockSpec((3, LANES), lambda c: (0, c)), pl.BlockSpec((1, LANES), lambda c: (0, c)),
                  pl.BlockSpec((n_rows, LANES), lambda c: (0, c)),
                  pl.BlockSpec((n_rows, LANES), lambda c: (0, jnp.minimum(c, n_extra - 1)))],
        out_specs=[pl.BlockSpec((n_rows, LANES), lambda c: (0, c)),
                   pl.BlockSpec((3, LANES), lambda c: (0, c)), pl.BlockSpec((1, LANES), lambda c: (0, c))],
        out_shape=[jax.ShapeDtypeStruct((n_rows, width), CDT), jax.ShapeDtypeStruct((3, width), F32),
                   jax.ShapeDtypeStruct((1, width), F32)],
        compiler_params=_cp(("parallel",)))(x.arr, conv_w, conv_b, dy, dxs_extra)


def loss_head(h, target, final_w, m_ctx):
    n_rows, d = h.shape
    tm = min(ROW_TILE, n_rows)
    nb_ctx = m_ctx // tm

    def f(hb, w, tgt):
        err = _rms(hb, w) - tgt
        return 0.5 * jnp.sum(jnp.mean(err * err, axis=-1))

    def body(h_ref, t_ref, w_ref, loss_ref, dh_ref, dw_ref):
        i = pl.program_id(0)

        @pl.when(i < nb_ctx)
        def _():
            dh_ref[...] = jnp.zeros_like(dh_ref)

        @pl.when(i == 0)
        def _():
            loss_ref[...] = jnp.zeros_like(loss_ref)
            dw_ref[...] = jnp.zeros_like(dw_ref)

        @pl.when(i >= nb_ctx)
        def _():
            val, vjp = jax.vjp(lambda hb, w: f(hb, w, t_ref[...]), h_ref[...], w_ref[...])
            dh, dw = vjp(jnp.ones((), F32))
            dh_ref[...] = dh
            dw_ref[...] += dw
            loss_ref[...] += jnp.broadcast_to(val, loss_ref.shape)

    return pl.pallas_call(
        body, name="loss_head", grid=(n_rows // tm,),
        in_specs=[pl.BlockSpec((tm, d), lambda i: (i, 0)),
                  pl.BlockSpec((tm, d), lambda i: (jnp.maximum(i - nb_ctx, 0), 0)), _const_spec((1, d))],
        out_specs=[_const_spec((1, LANES)), pl.BlockSpec((tm, d), lambda i: (i, 0)), _const_spec((1, d))],
        out_shape=[jax.ShapeDtypeStruct((1, LANES), F32), jax.ShapeDtypeStruct((n_rows, d), F32),
                   jax.ShapeDtypeStruct((1, d), F32)],
        compiler_params=_cp(("arbitrary",)))(h, target, final_w)


def adamw(name, w, m, v, g_parts):
    rows, cols = w.shape
    tr = _pick(rows, (256, 128, 64, 32, 16, 8))
    npart = len(g_parts)
    c1 = 1.0 - ADAM_B1 ** ADAM_STEP
    c2 = 1.0 - ADAM_B2 ** ADAM_STEP

    def body(*refs):
        w_ref, m_ref, v_ref = refs[:3]
        g = refs[3][...].astype(F32)
        for r in refs[4:3 + npart]:
            g = g + r[...].astype(F32)
        g_ref, d_ref, nm_ref, nv_ref = refs[3 + npart:]
        nm = ADAM_B1 * m_ref[...] + (1.0 - ADAM_B1) * g
        nv = ADAM_B2 * v_ref[...] + (1.0 - ADAM_B2) * (g * g)
        g_ref[...] = g
        nm_ref[...] = nm
        nv_ref[...] = nv
        d_ref[...] = -ADAM_LR * ((nm / c1) / (jnp.sqrt(nv / c2) + ADAM_EPS) + ADAM_WD * w_ref[...])

    spec = pl.BlockSpec((tr, cols), lambda i: (i, 0))
    return pl.pallas_call(
        body, name=name, grid=(rows // tr,), in_specs=[spec] * (3 + npart), out_specs=[spec] * 4,
        out_shape=[jax.ShapeDtypeStruct((rows, cols), F32)] * 4, compiler_params=_cp(("parallel",)))(w, m, v, *g_parts)


def sum_parts(name, parts):
    npart, rows, cols = parts.shape
    tr = _pick(rows, (512, 256, 8))

    def body(p_ref, o_ref):
        acc = p_ref[0].astype(F32)
        for k in range(1, npart):
            acc = acc + p_ref[k].astype(F32)
        o_ref[...] = acc

    return pl.pallas_call(
        body, name=name, grid=(rows // tr,), in_specs=[pl.BlockSpec((npart, tr, cols), lambda i: (0, i, 0))],
        out_specs=pl.BlockSpec((tr, cols), lambda i: (i, 0)), out_shape=jax.ShapeDtypeStruct((rows, cols), F32),
        compiler_params=_cp(("parallel",)))(parts)


MESH = pl.DeviceIdType.MESH
_HBM = pl.BlockSpec(memory_space=pl.ANY)


def _chip_peers():
    x, y, c = lax.axis_index("x"), lax.axis_index("y"), lax.axis_index("c")
    return x, y, c, [(1 - x, y), (x, 1 - y), (1 - x, 1 - y)]


def gather_chips(name, shard):
    half = shard.shape[0] // 2

    def body(x_ref, out_ref, send_sems, recv_sems, local_sem):
        x, y, c, peers = _chip_peers()
        me = 2 * x + y
        mine = pltpu.make_async_copy(x_ref, out_ref.at[me], local_sem)
        mine.start()

        def over_ici(k, chip, to):
            rws = pl.ds(c * half, half)
            return pltpu.make_async_remote_copy(
                src_ref=x_ref.at[rws], dst_ref=out_ref.at[chip, rws], send_sem=send_sems.at[k],
                recv_sem=recv_sems.at[k], device_id=to, device_id_type=MESH)

        sends = []
        for k, (px, py) in enumerate(peers):
            cp = over_ici(k, me, (px, py, c))
            cp.start()
            sends.append(cp)
        for k, (px, py) in enumerate(peers):
            over_ici(k, 2 * px + py, (px, py, c)).wait_recv()
        for cp in sends:
            cp.wait_send()
        mine.wait()

    return pl.pallas_call(
        body, name=name, in_specs=[_HBM], out_specs=_HBM,
        out_shape=jax.ShapeDtypeStruct((4,) + shard.shape, shard.dtype),
        scratch_shapes=[pltpu.SemaphoreType.DMA((3,)), pltpu.SemaphoreType.DMA((3,)), pltpu.SemaphoreType.DMA],
        )(shard)


def _pair_step(n_steps, x_ref, land, send_sems, recv_sems, credits, consume):
    x, y, c = lax.axis_index("x"), lax.axis_index("y"), lax.axis_index("c")
    sib = (x, y, 1 - c)
    i = pl.program_id(0)
    slot = i % 2

    @pl.when(i >= 2)
    def _():
        pl.semaphore_wait(credits.at[slot], 1)

    cp = pltpu.make_async_remote_copy(src_ref=x_ref, dst_ref=land.at[slot], send_sem=send_sems.at[slot],
                                      recv_sem=recv_sems.at[slot], device_id=sib, device_id_type=MESH)
    cp.start()
    cp.wait_recv()
    consume(land[slot])

    @pl.when(i < n_steps - 2)
    def _():
        pl.semaphore_signal(credits.at[slot], inc=1, device_id=sib, device_id_type=MESH)

    cp.wait_send()


def _pair_call(name, body, n_steps, in_spec, out_spec, out_shape, blk_shape, dtype, operands, aliases=None):
    grid_spec = pltpu.PrefetchScalarGridSpec(
        num_scalar_prefetch=1, grid=(n_steps,), in_specs=[in_spec], out_specs=out_spec,
        scratch_shapes=[pltpu.VMEM((2,) + blk_shape, dtype), pltpu.SemaphoreType.DMA((2,)),
                        pltpu.SemaphoreType.DMA((2,)), pltpu.SemaphoreType.REGULAR((2,))])
    return pl.pallas_call(body, name=name, grid_spec=grid_spec, out_shape=out_shape,
                          input_output_aliases=aliases or {}, compiler_params=_cp(("arbitrary",)))(*operands)


def _place():
    return jnp.stack([lax.axis_index("x"), lax.axis_index("y"), lax.axis_index("c")]).astype(jnp.int32)


def share_gathered(name, gathered):
    _, rows, cols = gathered.shape
    half = rows // 2
    tr = _pick(half, (PAIR_ROWS, 512, 16))
    nb = half // tr
    n_steps = 3 * nb

    def chip_of(k, s):
        px = jnp.where(k == 1, s[0], 1 - s[0])
        py = jnp.where(k == 0, s[1], 1 - s[1])
        return 2 * px + py

    def body(s_ref, x_ref, o_ref, land, send_sems, recv_sems, credits):
        def consume(v):
            o_ref[...] = v
        _pair_step(n_steps, x_ref, land, send_sems, recv_sems, credits, consume)

    in_spec = pl.BlockSpec((tr, cols), lambda i, s: ((2 * chip_of(i // nb, s) + s[2]) * nb + i % nb, 0))
    out_spec = pl.BlockSpec((tr, cols), lambda i, s: ((2 * chip_of(i // nb, s) + 1 - s[2]) * nb + i % nb, 0))
    flat = gathered.reshape(4 * rows, cols)
    return _pair_call(name, body, n_steps, in_spec, out_spec, jax.ShapeDtypeStruct(flat.shape, flat.dtype),
                      (tr, cols), flat.dtype, (_place(), flat), aliases={1: 0}).reshape(gathered.shape)


def exchange_halves(name, pieces):
    _, rows, cols = pieces.shape
    half = rows // 2
    tr = _pick(half, (PAIR_ROWS, 512, 16))
    nb = half // tr
    n_steps = 4 * nb

    def body(s_ref, x_ref, o_ref, land, send_sems, recv_sems, credits):
        def consume(v):
            o_ref[...] = v
        _pair_step(n_steps, x_ref, land, send_sems, recv_sems, credits, consume)

    in_spec = pl.BlockSpec((tr, cols), lambda i, s: ((2 * (i // nb) + 1 - s[2]) * nb + i % nb, 0))
    out_spec = pl.BlockSpec((tr, cols), lambda i, s: (i, 0))
    return _pair_call(name, body, n_steps, in_spec, out_spec, jax.ShapeDtypeStruct((4 * half, cols), pieces.dtype),
                      (tr, cols), pieces.dtype, (_place(), pieces.reshape(4 * rows, cols))).reshape(4, half, cols)


def add_halves(name, pieces, got):
    _, rows, cols = pieces.shape
    half = rows // 2
    tr = _pick(half, (512, 256, 16))
    nb = half // tr

    def body(s_ref, a_ref, b_ref, o_ref):
        o_ref[...] = (a_ref[...].astype(F32) + b_ref[...].astype(F32)).astype(o_ref.dtype)

    grid_spec = pltpu.PrefetchScalarGridSpec(
        num_scalar_prefetch=1, grid=(nb,),
        in_specs=[pl.BlockSpec((4, tr, cols), lambda i, s: (0, s[2] * nb + i, 0)),
                  pl.BlockSpec((4, tr, cols), lambda i, s: (0, i, 0))],
        out_specs=pl.BlockSpec((4, tr, cols), lambda i, s: (0, i, 0)))
    return pl.pallas_call(body, name=name, grid_spec=grid_spec, out_shape=jax.ShapeDtypeStruct(got.shape, CDT),
                          compiler_params=_cp(("arbitrary",)))(_place(), pieces, got)


def exchange_sums(name, part):
    rows, cols = part.shape
    tr = _pick(rows, (PAIR_ROWS // 2, 256, 8))
    n_steps = rows // tr

    def body(s_ref, x_ref, o_ref, land, send_sems, recv_sems, credits):
        c = lax.axis_index("c")
        o_ref[c] = x_ref[...]

        def consume(v):
            o_ref[1 - c] = v
        _pair_step(n_steps, x_ref, land, send_sems, recv_sems, credits, consume)

    in_spec = pl.BlockSpec((tr, cols), lambda i, s: (i, 0))
    out_spec = pl.BlockSpec((2, tr, cols), lambda i, s: (0, i, 0))
    return _pair_call(name, body, n_steps, in_spec, out_spec, jax.ShapeDtypeStruct((2, rows, cols), part.dtype),
                      (tr, cols), part.dtype, (_place(), part))


def scatter_chips(name, pieces):
    def body(p_ref, out_ref, send_sems, recv_sems, local_sem):
        x, y, c, peers = _chip_peers()
        me = 2 * x + y
        mine = pltpu.make_async_copy(p_ref.at[me], out_ref.at[me], local_sem)
        mine.start()
        sends = []
        for k, (px, py) in enumerate(peers):
            cp = pltpu.make_async_remote_copy(src_ref=p_ref.at[2 * px + py], dst_ref=out_ref.at[me],
                                              send_sem=send_sems.at[k], recv_sem=recv_sems.at[k],
                                              device_id=(px, py, c), device_id_type=MESH)
            cp.start()
            sends.append(cp)
        for k, (px, py) in enumerate(peers):
            pltpu.make_async_remote_copy(src_ref=p_ref.at[me], dst_ref=out_ref.at[2 * px + py],
                                         send_sem=send_sems.at[k], recv_sem=recv_sems.at[k], device_id=(px, py, c),
                                         device_id_type=MESH).wait_recv()
        for cp in sends:
            cp.wait_send()
        mine.wait()

    return pl.pallas_call(
        body, name=name, in_specs=[_HBM], out_specs=_HBM, out_shape=jax.ShapeDtypeStruct(pieces.shape, pieces.dtype),
        scratch_shapes=[pltpu.SemaphoreType.DMA((3,)), pltpu.SemaphoreType.DMA((3,)), pltpu.SemaphoreType.DMA],
        )(pieces)


def allreduce_small(name, buf):
    rows = buf.shape[0]

    def body(x_ref, out_ref, gath, send_sems, recv_sems):
        x, y, c = lax.axis_index("x"), lax.axis_index("y"), lax.axis_index("c")
        me = 4 * x + 2 * y + c
        masks = [(k >> 2 & 1, k >> 1 & 1, k & 1) for k in range(1, 8)]

        def flip(v, bit):
            return 1 - v if bit else v

        sends = []
        for k, (bx, by, bc) in enumerate(masks):
            cp = pltpu.make_async_remote_copy(src_ref=x_ref, dst_ref=gath.at[me], send_sem=send_sems.at[k],
                                              recv_sem=recv_sems.at[k],
                                              device_id=(flip(x, bx), flip(y, by), flip(c, bc)), device_id_type=MESH)
            cp.start()
            sends.append(cp)
        gath[me] = x_ref[...]
        for k, (bx, by, bc) in enumerate(masks):
            px, py, pc = flip(x, bx), flip(y, by), flip(c, bc)
            pltpu.make_async_remote_copy(src_ref=x_ref, dst_ref=gath.at[4 * px + 2 * py + pc],
                                         send_sem=send_sems.at[k], recv_sem=recv_sems.at[k],
                                         device_id=(px, py, pc), device_id_type=MESH).wait_recv()
        for cp in sends:
            cp.wait_send()
        acc = gath[0]
        for d in range(1, 8):
            acc = acc + gath[d]
        out_ref[...] = acc

    return pl.pallas_call(
        body, name=name, in_specs=[pl.BlockSpec(memory_space=pltpu.VMEM)],
        out_specs=pl.BlockSpec(memory_space=pltpu.VMEM), out_shape=jax.ShapeDtypeStruct(buf.shape, F32),
        scratch_shapes=[pltpu.VMEM((8, rows, LANES), F32), pltpu.SemaphoreType.DMA((7,)),
                        pltpu.SemaphoreType.DMA((7,))],
        )(buf)


def _pack_flat(arrs, dtype, width, row_mult=8):
    flat = jnp.concatenate([a.reshape(-1).astype(dtype) for a in arrs])
    pad = (-flat.shape[0]) % (row_mult * width)
    if pad:
        flat = jnp.concatenate([flat, jnp.zeros((pad,), dtype)])
    return flat.reshape(-1, width)


def _unpack_flat(buf, shapes):
    flat = buf.reshape(-1)
    out, off = [], 0
    for s in shapes:
        n = math.prod(s)
        out.append(flat[off:off + n].reshape(s))
        off += n
    return out


def _in_to_padded(w):
    parts = []
    for name in IN_NEW_ORDER:
        _, width, o_off, o_w = IN_LAYOUT[name]
        parts.append(w[..., o_off:o_off + o_w])
        if o_w < width:
            parts.append(jnp.zeros(w.shape[:-1] + (width - o_w,), w.dtype))
    used = sum(IN_LAYOUT[n][1] for n in IN_NEW_ORDER)
    parts.append(jnp.zeros(w.shape[:-1] + (IN_PAD - used,), w.dtype))
    return jnp.concatenate(parts, axis=-1)


def _in_from_padded(g):
    parts = []
    for name in IN_ORIG_ORDER:
        off, _, _, o_w = IN_LAYOUT[name]
        parts.append(g[..., off:off + o_w])
    return jnp.concatenate(parts, axis=-1)


def _pcol(p, name):
    off, width, _, _ = IN_LAYOUT[name]
    return Cols(p, off, width)


def _lane_pad(v, width=LANES):
    v = v.reshape(-1)
    return jnp.concatenate([v, jnp.zeros((width - v.shape[0],), v.dtype)]).reshape(1, width)


def _f_norm_mod(h, sh, sc, w):
    return (_rms(h, w) * (1.0 + sc) + sh,)


def _f_norm_mod_thru(h, sh, sc, w):
    return h, _rms(h, w) * (1.0 + sc) + sh


def _f_attn_prep(qraw, kraw, vraw, cq, sq, ck, sk, qw, kw, gq, gk):
    q = qraw * lax.rsqrt(_group_mean(qraw * qraw, gq) + NORM_EPS) * qw
    q = _rope32(q, cq, sq) * (ATTN_HD ** -0.5)
    k = kraw * lax.rsqrt(_group_mean(kraw * kraw, gk) + NORM_EPS) * kw
    return q, _rope32(k, ck, sk), vraw


def _f_ssd_finish(yf, yb, xs, z, d_exp, nw):
    y = (yf + yb + d_exp * xs) * (z * jax.nn.sigmoid(z))
    return (_rms(y, nw),)


def _f_ret_prep(rq, rk, cos_full, sin_signed):
    return _rope64(rq, cos_full, sin_signed), _rope64(rk, cos_full, sin_signed) * (RET_DK ** -0.5)


def _f_ret_finish(yf, yb, g, gw):
    y = yf + yb
    outs = []
    for h in range(RET_HEADS):
        yh = y[:, h * RET_DK:(h + 1) * RET_DK]
        yc = yh - jnp.mean(yh, axis=-1, keepdims=True)
        outs.append(yc * lax.rsqrt(jnp.mean(yc * yc, axis=-1, keepdims=True) + NORM_EPS))
    return (jnp.concatenate(outs, axis=1) * gw * (g * jax.nn.sigmoid(g)),)


def _f_merge(p0, p1, p2, g0, g1, g2):
    return (jax.nn.sigmoid(g0) * p0 + jax.nn.sigmoid(g1) * p1 + jax.nn.sigmoid(g2) * p2,)


def _f_mid(h, mix, g1, sh2, sc2, w2):
    h_mid = h + g1 * mix
    return h_mid, _rms(h_mid, w2) * (1.0 + sc2) + sh2


def _f_sqrelu(a):
    r = jnp.maximum(a, 0.0)
    return (r * r,)


def _f_residual(h_mid, o, g2):
    return (h_mid + g2 * o,)


def _f_silu(x):
    return (x * jax.nn.sigmoid(x),)


def _f_bias(x, b):
    return (x + b,)


def _ssd_rows(xbc, p):
    rows = [Cols(xbc, LANES * k, LANES) for k in range(4)]
    rows += [Cols(xbc, 512 + LANES * g, LANES) for g in range(2)]
    rows += [Cols(xbc, 768 + LANES * g, LANES) for g in range(2)]
    return rows + [_pcol(p, "dt")]


def _ret_rows(rq, rk, p):
    off_v = IN_LAYOUT["rv"][0]
    return ([Cols(rq, LANES * h, LANES) for h in range(4)] + [Cols(rk, LANES * h, LANES) for h in range(4)]
            + [Cols(p, off_v + LANES * h, LANES) for h in range(4)])


def layer_fwd(li, h, mod, lw, tabs, m_ctx):
    t = h.shape[0]
    nb = m_ctx // min(ROW_TILE, t)
    sh1, sc1, g1, sh2, sc2, g2 = mod
    nm = lambda s: f"l{li}_{s}"
    sv = {}
    (u,) = rowwise_fwd(nm("norm1"), _f_norm_mod, [h], [sh1, sc1], [lw["norm1_w"]], [(D_MODEL, CDT)], t, nb)
    p = mm(nm("in_proj"), u, lw["w_in"], F32)
    q, k, v = rowwise_fwd(
        nm("attn_prep"), _f_attn_prep,
        [_pcol(p, "q"), _pcol(p, "k"), _pcol(p, "v"), tabs["cq"], tabs["sq"], tabs["ck"], tabs["sk"]], [],
        [lw["qw"], lw["kw"], tabs["gq"], tabs["gk"]], [(512, CDT), (128, CDT), (128, CDT)], t, nb)
    tq = min(ATTN_TQ, m_ctx)
    kk, vv = _split_kv(k), _split_kv(v)
    attn_o, qT, oT, lse = attn_fwd(nm("attn"), q, kk, vv.transpose(0, 2, 1), m_ctx, tq)

    xbc = conv_fwd(_pcol(p, "xbc"), lw["conv_w"], lw["conv_b"], m_ctx)
    ssd_sh = [lw["dt_bias"], lw["a_log"]]
    yf, sf = scan_fwd(nm("ssd_f"), _make_ssd_chunk(0), _ssd_rows(xbc, p), ssd_sh, 4, 512, t, m_ctx, False)
    yb, sb = scan_fwd(nm("ssd_b"), _make_ssd_chunk(1), _ssd_rows(xbc, p), ssd_sh, 4, 512, t, m_ctx, True)
    (ssd_o,) = rowwise_fwd(nm("ssd_fin"), _f_ssd_finish, [yf, yb, Cols(xbc, 0, 512), _pcol(p, "z")], [],
                           [lw["d_exp"], lw["ssd_nw"]], [(512, CDT)], t, nb)

    rq, rk = rowwise_fwd(nm("ret_prep"), _f_ret_prep, [_pcol(p, "rq"), _pcol(p, "rk"), tabs["rc"], tabs["rs"]],
                         [], [], [(512, F32), (512, F32)], t, nb)
    rf, rsf = scan_fwd(nm("ret_f"), _make_ret_chunk(0), _ret_rows(rq, rk, p), [lw["ret_lg"]], 4, 512, t, m_ctx, False)
    rb, rsb = scan_fwd(nm("ret_b"), _make_ret_chunk(1), _ret_rows(rq, rk, p), [lw["ret_lg"]], 4, 512, t, m_ctx, True)
    (ret_o,) = rowwise_fwd(nm("ret_fin"), _f_ret_finish, [rf, rb, _pcol(p, "rg")], [], [lw["ret_gw"]],
                           [(512, CDT)], t, nb)

    pbs = [mm(nm(f"branch{b}"), br, lw["w_branch"][b], CDT) for b, br in enumerate((attn_o, ssd_o, ret_o))]
    gl = [Cols(p, 1024 * b, 1024) for b in range(3)]
    (merged,) = rowwise_fwd(nm("merge"), _f_merge, pbs + gl, [], [], [(D_MODEL, CDT)], t, nb)
    mix = mm(nm("out_proj"), merged, lw["w_out"], F32)
    h_mid, vv2 = rowwise_fwd(nm("mid"), _f_mid, [h, mix], [g1, sh2, sc2], [lw["norm2_w"]],
                             [(D_MODEL, F32), (D_MODEL, CDT)], t, nb)
    a = mm(nm("mlp1"), vv2, lw["w_mlp1"], CDT)
    (hh,) = rowwise_fwd(nm("sqrelu"), _f_sqrelu, [a], [], [], [(a.shape[1], CDT)], t, nb)
    o = mm(nm("mlp2"), hh, lw["w_mlp2"], F32)
    (h_out,) = rowwise_fwd(nm("resid"), _f_residual, [h_mid, o], [g2], [], [(D_MODEL, F32)], t, nb)
    sv.update(h=h, u=u, p=p, qT=qT, kk=kk, vv=vv, oT=oT, lse=lse, attn_o=attn_o, xbc=xbc, yf=yf, yb=yb,
              sf=sf, sb=sb, ssd_o=ssd_o, rq=rq, rk=rk, rf=rf, rb=rb, rsf=rsf, rsb=rsb, ret_o=ret_o, pbs=pbs,
              merged=merged, mix=mix, h_mid=h_mid, v=vv2, a=a, hh=hh, o=o)
    return h_out, sv


def layer_bwd(li, dh_out, sv, mod, lw, lwt, tabs, m_ctx):
    t = dh_out.shape[0]
    nb = m_ctx // min(ROW_TILE, t)
    sh1, sc1, g1, sh2, sc2, g2 = mod
    nm = lambda s: f"l{li}_{s}_bwd"
    gw = {}
    p = sv["p"]
    (do,), (dg2,), _ = rowwise_bwd(nm("resid"), _f_residual, [sv["h_mid"], sv["o"]], [g2], [], [dh_out],
                                   [False, True], [], [CDT], t, nb)
    dhh = mm(nm("mlp2_dx"), do, lwt["w_mlp2"], CDT)
    gw["w_mlp2"] = mm_tn(nm("mlp2_dw"), sv["hh"], do)
    (da,), _, _ = rowwise_bwd(nm("sqrelu"), _f_sqrelu, [sv["a"]], [], [], [dhh], [True], [], [CDT], t, nb)
    dv = mm(nm("mlp1_dx"), da, lwt["w_mlp1"], F32)
    gw["w_mlp1"] = mm_tn(nm("mlp1_dw"), sv["v"], da)
    (dh_a, dmix), (dg1, dsh2, dsc2), (gw["norm2_w"],) = rowwise_bwd(
        nm("mid"), _f_mid, [sv["h"], sv["mix"]], [g1, sh2, sc2], [lw["norm2_w"]], [dh_out, dv],
        [True, True], [True], [F32, CDT], t, nb)
    dmerged = mm(nm("out_dx"), dmix, lwt["w_out"], CDT)
    gw["w_out"] = mm_tn(nm("out_dw"), sv["merged"], dmix)
    gl = [Cols(p, 1024 * b, 1024) for b in range(3)]
    dmg, _, _ = rowwise_bwd(nm("merge"), _f_merge, sv["pbs"] + gl, [], [], [dmerged], [True] * 6, [], [CDT] * 6,
                            t, nb)
    dpb, dgl = dmg[:3], dmg[3:]
    brs = (sv["attn_o"], sv["ssd_o"], sv["ret_o"])
    d_attn_o = mm(nm("branch0_dx"), dpb[0], lwt["w_branch"][0], CDT)
    d_ssd_o = mm(nm("branch1_dx"), dpb[1], lwt["w_branch"][1], F32)
    d_ret_o = mm(nm("branch2_dx"), dpb[2], lwt["w_branch"][2], F32)
    gw["w_branch"] = jnp.stack([mm_tn(nm(f"branch{b}_dw"), brs[b], dpb[b]) for b in range(3)])
    tq = min(ATTN_TQ, m_ctx)
    dq_t, dk_s, dv_s = attn_bwd(nm("attn"), sv["qT"], d_attn_o, sv["oT"], sv["lse"], sv["kk"],
                                sv["kk"].transpose(0, 2, 1), sv["vv"], m_ctx)
    (dq_raw, dk_raw, dv_raw), _, (gw["qw"], gw["kw"]) = rowwise_bwd(
        nm("attn_prep"), _f_attn_prep,
        [_pcol(p, "q"), _pcol(p, "k"), _pcol(p, "v"), tabs["cq"], tabs["sq"], tabs["ck"], tabs["sk"]], [],
        [lw["qw"], lw["kw"], tabs["gq"], tabs["gk"]],
        [from_heads_t(nm("dq_rows"), dq_t, F32), _merge_kv(dk_s), _merge_kv(dv_s)],
        [True, True, True, False, False, False, False], [True, True, False, False], [CDT] * 3, t, nb)
    (dy_ssd, dxs_fin, dz), _, (gw["d_exp"], gw["ssd_nw"]) = rowwise_bwd(
        nm("ssd_fin"), _f_ssd_finish, [sv["yf"], sv["yb"], Cols(sv["xbc"], 0, 512), _pcol(p, "z")], [],
        [lw["d_exp"], lw["ssd_nw"]], [d_ssd_o], [True, False, True, True], [True, True], [F32, F32, CDT], t, nb)
    ssd_sh = [lw["dt_bias"], lw["a_log"]]
    post_ssd = lambda d: [jnp.concatenate(d[0:8], axis=1), d[8]]
    (dxbc_f, ddt_f), dsh_f = scan_bwd(nm("ssd_f"), _make_ssd_chunk(0), _ssd_rows(sv["xbc"], p), ssd_sh, sv["sf"],
                                      dy_ssd, post_ssd, [(1024, F32), (LANES, F32)], [], 4, t, m_ctx, False)
    (dxbc, ddt), dsh_b = scan_bwd(nm("ssd_b"), _make_ssd_chunk(1), _ssd_rows(sv["xbc"], p), ssd_sh, sv["sb"],
                                  dy_ssd, post_ssd, [(1024, F32), (LANES, CDT)], [dxbc_f, ddt_f], 4, t, m_ctx, True)
    gw["dt_bias"] = dsh_f[0] + dsh_b[0]
    gw["a_log"] = dsh_f[1] + dsh_b[1]
    dxbc_raw, gw["conv_w"], gw["conv_b"] = conv_bwd(_pcol(p, "xbc"), lw["conv_w"], lw["conv_b"], dxbc, dxs_fin, m_ctx)
    (dy_ret, drg), _, (gw["ret_gw"],) = rowwise_bwd(
        nm("ret_fin"), _f_ret_finish, [sv["rf"], sv["rb"], _pcol(p, "rg")], [], [lw["ret_gw"]], [d_ret_o],
        [True, False, True], [True], [F32, CDT], t, nb)
    post_ret = lambda d: [jnp.concatenate(d[0:4], axis=1), jnp.concatenate(d[4:8], axis=1),
                          jnp.concatenate(d[8:12], axis=1)]
    rrows = _ret_rows(sv["rq"], sv["rk"], p)
    r3 = [(512, F32)] * 3
    part, dlg_f = scan_bwd(nm("ret_f"), _make_ret_chunk(0), rrows, [lw["ret_lg"]], sv["rsf"], dy_ret, post_ret, r3,
                           [], 4, t, m_ctx, False)
    (drq_r, drk_r, drv), dlg_b = scan_bwd(nm("ret_b"), _make_ret_chunk(1), rrows, [lw["ret_lg"]], sv["rsb"], dy_ret,
                                          post_ret, [(512, F32), (512, F32), (512, CDT)], part, 4, t, m_ctx, True)
    gw["ret_lg"] = dlg_f[0] + dlg_b[0]
    (drq, drk), _, _ = rowwise_bwd(nm("ret_prep"), _f_ret_prep,
                                   [_pcol(p, "rq"), _pcol(p, "rk"), tabs["rc"], tabs["rs"]], [], [], [drq_r, drk_r],
                                   [True, True, False, False], [], [CDT, CDT], t, nb)
    pieces = {"gates": None, "xbc": dxbc_raw, "q": dq_raw, "z": dz, "rq": drq, "rk": drk, "rv": drv, "rg": drg,
              "k": dk_raw, "v": dv_raw, "dt": ddt}
    cols = list(dgl) + [pieces[n] for n in IN_NEW_ORDER[1:]]
    used = sum(c.shape[1] for c in cols)
    cols.append(jnp.zeros((t, IN_PAD - used), CDT))
    dp = jnp.concatenate(cols, axis=1)
    du = mm(nm("in_dx"), dp, lwt["w_in"], F32)
    gw["w_in"] = mm_tn(nm("in_dw"), sv["u"], dp)
    (dh_in,), (dsh1, dsc1), (gw["norm1_w"],) = rowwise_bwd(
        nm("norm1"), _f_norm_mod_thru, [sv["h"]], [sh1, sc1], [lw["norm1_w"]], [dh_a, du], [True], [True], [F32],
        t, nb)
    return dh_in, [dsh1, dsc1, dg1, dsh2, dsc2, dg2], gw


def _rope_tables(n_lat, m_ctx):
    rows = n_lat // GRID_W
    row = jnp.repeat(jnp.arange(rows, dtype=F32), GRID_W)
    col = jnp.tile(jnp.arange(GRID_W, dtype=F32), rows)
    nfreq = ATTN_HD // 4
    inv = ROPE_THETA ** (-jnp.arange(nfreq, dtype=F32) / nfreq)
    ang = jnp.concatenate([row[:, None] * inv, col[:, None] * inv], axis=-1)
    cos = jnp.concatenate([jnp.ones((m_ctx, ATTN_HD // 2), F32), jnp.cos(ang)], axis=0)
    sin = jnp.concatenate([jnp.zeros((m_ctx, ATTN_HD // 2), F32), jnp.sin(ang)], axis=0)
    c64 = jnp.concatenate([cos, cos], axis=1)
    s64 = jnp.concatenate([-sin, sin], axis=1)
    pos = jnp.arange(m_ctx + n_lat, dtype=F32)
    inv_r = ROPE_THETA ** (-jnp.linspace(0.0, 1.0, RET_DK // 2, dtype=F32))
    ang_r = pos[:, None] * inv_r
    rc = jnp.concatenate([jnp.cos(ang_r)] * 2, axis=1)
    rs = jnp.concatenate([-jnp.sin(ang_r), jnp.sin(ang_r)], axis=1)
    return dict(cq=jnp.tile(c64, (1, 8)), sq=jnp.tile(s64, (1, 8)), ck=jnp.tile(c64, (1, 2)), sk=jnp.tile(s64, (1, 2)),
                rc=jnp.tile(rc, (1, 4)), rs=jnp.tile(rs, (1, 4)), gq=_group_matrix(512, ATTN_HD),
                gk=_group_matrix(128, ATTN_HD))


def _layer_weights(full, small, layer):
    lw = dict(
        w_in=_in_to_padded(full["w_in"][layer]), w_branch=full["w_branch"][layer], w_out=full["w_out"][layer],
        w_mlp1=full["w_mlp1"][layer], w_mlp2=full["w_mlp2"][layer],
        norm1_w=small["norm1_w"][layer][None], norm2_w=small["norm2_w"][layer][None],
        qw=jnp.tile(small["attn_q_norm"][layer], 8)[None], kw=jnp.tile(small["attn_k_norm"][layer], 2)[None],
        conv_w=small["ssd_conv_w"][layer], conv_b=small["ssd_conv_b"][layer][None],
        dt_bias=_lane_pad(small["ssd_dt_bias"][layer]), a_log=_lane_pad(small["ssd_a_log"][layer]),
        d_exp=jnp.repeat(small["ssd_d"][layer], SSD_HD)[None], ssd_nw=small["ssd_norm_w"][layer][None],
        ret_lg=_lane_pad(small["ret_log_decay"][layer]), ret_gw=small["ret_gn_w"][layer][None])
    lwt = dict(w_in=lw["w_in"].T, w_branch=jnp.swapaxes(lw["w_branch"], 1, 2), w_out=lw["w_out"].T,
               w_mlp1=lw["w_mlp1"].T, w_mlp2=lw["w_mlp2"].T)
    return lw, lwt


def local_step(x, c, ctx, full, small, loss_target):
    n_lat, d = x.shape
    m_ctx = ctx.shape[0]
    t = n_lat + m_ctx
    depth = small["norm1_w"].shape[0]
    tabs = _rope_tables(n_lat, m_ctx)
    h = jnp.concatenate([ctx, x], axis=0)
    cc = jnp.concatenate([small["c_ctx"][None], c, jnp.zeros((COND_ROWS - 2, d), F32)], axis=0)
    (scc,) = rowwise_fwd("cond_silu", _f_silu, [cc], [], [], [(d, CDT)], COND_ROWS, 0)
    mods, saved, lws = [], [], []
    for layer in range(depth):
        lw, lwt = _layer_weights(full, small, layer)
        mod_raw = mm(f"l{layer}_mod", scc, full["w_mod"][layer], F32)
        (mod8,) = rowwise_fwd(f"l{layer}_mod_bias", _f_bias, [mod_raw], [], [small["b_mod"][layer][None]],
                              [(6 * d, F32)], COND_ROWS, 0)
        mod = [mod8[0:2, k * d:(k + 1) * d].reshape(2, 1, d) for k in range(6)]
        h, sv = layer_fwd(layer, h, mod, lw, tabs, m_ctx)
        mods.append(mod)
        saved.append(sv)
        lws.append((lw, lwt))
    loss, dh, d_final = loss_head(h, loss_target, small["final_norm_w"][None], m_ctx)

    gbig = {k: [None] * depth for k in BIG}
    gs = {k: [None] * depth for k in SMALL if k not in ("c_ctx", "final_norm_w")}
    d_scc = None
    for layer in reversed(range(depth)):
        lw, lwt = lws[layer]
        dh, dmod, gw = layer_bwd(layer, dh, saved[layer], mods[layer], lw, lwt, tabs, m_ctx)
        dmod8 = jnp.concatenate([jnp.concatenate([g_.reshape(2, d) for g_ in dmod], axis=1),
                                 jnp.zeros((COND_ROWS - 2, 6 * d), F32)], axis=0)
        (dmod_c,), _, (db_mod,) = rowwise_bwd(f"l{layer}_mod_bias_bwd", _f_bias, [dmod8], [],
                                              [small["b_mod"][layer][None]], [dmod8], [True], [True], [CDT], COND_ROWS, 0)
        gbig["w_mod"][layer] = mm_tn(f"l{layer}_mod_dw", scc, dmod_c)
        part = mm(f"l{layer}_mod_dx", dmod_c, full["w_mod"][layer].T, F32)
        d_scc = part if d_scc is None else d_scc + part
        gbig["w_in"][layer] = _in_from_padded(gw["w_in"])
        for k in ("w_branch", "w_out", "w_mlp1", "w_mlp2"):
            gbig[k][layer] = gw[k]
        gs["b_mod"][layer] = db_mod.reshape(-1)
        gs["norm1_w"][layer] = gw["norm1_w"].reshape(-1)
        gs["norm2_w"][layer] = gw["norm2_w"].reshape(-1)
        gs["attn_q_norm"][layer] = gw["qw"].reshape(8, ATTN_HD).sum(0)
        gs["attn_k_norm"][layer] = gw["kw"].reshape(2, ATTN_HD).sum(0)
        gs["ssd_conv_w"][layer] = gw["conv_w"]
        gs["ssd_conv_b"][layer] = gw["conv_b"].reshape(-1)
        gs["ssd_dt_bias"][layer] = gw["dt_bias"][0, :16].reshape(2, 8)
        gs["ssd_a_log"][layer] = gw["a_log"][0, :16].reshape(2, 8)
        gs["ssd_d"][layer] = gw["d_exp"].reshape(SSD_HEADS, SSD_HD).sum(1)
        gs["ssd_norm_w"][layer] = gw["ssd_nw"].reshape(-1)
        gs["ret_log_decay"][layer] = gw["ret_lg"][0, :8].reshape(2, 4)
        gs["ret_gn_w"][layer] = gw["ret_gw"].reshape(-1)
    (d_cc,), _, _ = rowwise_bwd("cond_silu_bwd", _f_silu, [cc], [], [], [d_scc], [True], [], [F32], COND_ROWS, 0)
    g_small = {k: jnp.stack(v) for k, v in gs.items()}
    g_small["c_ctx"] = d_cc[0]
    g_small["final_norm_w"] = d_final.reshape(-1)
    g_big = {k: jnp.stack(v) for k, v in gbig.items()}
    return loss, dh[m_ctx:], g_big, g_small


def _shards(a, axis):
    return jnp.split(a, 4, axis=axis)


def kernel(x, c, ctx, c_ctx, w_mod, b_mod, norm1_w, norm2_w, w_in, attn_q_norm, attn_k_norm, ssd_conv_w, ssd_conv_b, ssd_dt_bias, ssd_a_log, ssd_d, ssd_norm_w, ret_log_decay, ret_gn_w, w_branch, w_out, w_mlp1, w_mlp2, final_norm_w, loss_target, m_c_ctx, m_w_mod, m_b_mod, m_norm1_w, m_norm2_w, m_w_in, m_attn_q_norm, m_attn_k_norm, m_ssd_conv_w, m_ssd_conv_b, m_ssd_dt_bias, m_ssd_a_log, m_ssd_d, m_ssd_norm_w, m_ret_log_decay, m_ret_gn_w, m_w_branch, m_w_out, m_w_mlp1, m_w_mlp2, m_final_norm_w, v_c_ctx, v_w_mod, v_b_mod, v_norm1_w, v_norm2_w, v_w_in, v_attn_q_norm, v_attn_k_norm, v_ssd_conv_w, v_ssd_conv_b, v_ssd_dt_bias, v_ssd_a_log, v_ssd_d, v_ssd_norm_w, v_ret_log_decay, v_ret_gn_w, v_w_branch, v_w_out, v_w_mlp1, v_w_mlp2, v_final_norm_w):
    env = dict(locals())
    w_loc = {k: env[k] for k in WEIGHTS}
    m_loc = {k: env["m_" + k] for k in WEIGHTS}
    v_loc = {k: env["v_" + k] for k in WEIGHTS}
    chip = 2 * lax.axis_index("x") + lax.axis_index("y")
    core = lax.axis_index("c")

    big_shapes = [w_loc[k].shape for k in BIG]
    gathered = share_gathered("share_weights",
                              gather_chips("gather_weights", _pack_flat([w_loc[k] for k in BIG], CDT, 1024, PACK_ROWS)))
    per_chip = [_unpack_flat(gathered[j], big_shapes) for j in range(4)]
    full = {k: jnp.concatenate([per_chip[j][i] for j in range(4)], axis=BIG_AXIS[k]) for i, k in enumerate(BIG)}

    cw = w_loc["ssd_conv_w"]
    cw_w = cw.shape[-1]
    placed = lax.dynamic_update_slice(jnp.zeros(cw.shape[:-1] + (4 * cw_w,), F32),
                                      cw * (core == 0).astype(F32), (0, 0, chip * cw_w))
    conv_full = _unpack_flat(allreduce_small("gather_conv_w", _pack_flat([placed], F32, LANES)), [placed.shape])[0]
    small = {k: w_loc[k] for k in SMALL}
    small["ssd_conv_w"] = conv_full

    loss_l, grad_x, g_big, g_small = local_step(x[0], c, ctx[0], full, small, loss_target[0])

    small_shapes = [g_small[k].shape for k in SMALL] + [(LANES,)]
    summed = _unpack_flat(allreduce_small("reduce_small", _pack_flat([g_small[k] for k in SMALL] + [loss_l], F32, LANES)),
                          small_shapes)
    gsum = dict(zip(SMALL, summed[:-1]))
    loss = summed[-1][0]
    gsum["ssd_conv_w"] = lax.dynamic_slice(gsum["ssd_conv_w"], (0, 0, chip * cw_w), cw.shape)

    pieces = jnp.stack([_pack_flat([_shards(g_big[k], BIG_AXIS[k])[j] for k in BIG], CDT, 1024, PACK_ROWS) for j in range(4)])
    pair = add_halves("add_cores", pieces, exchange_halves("exchange_halves", pieces))
    chip_sum = sum_parts("sum_chips", scatter_chips("scatter_grads", pair))
    both = exchange_sums("exchange_sums", chip_sum)
    g_sum = _unpack_flat(both, big_shapes)

    grads, deltas, new_m, new_v = {}, {}, {}, {}
    for i, k in enumerate(BIG):
        shp = w_loc[k].shape
        two_d = lambda a, shp=shp: a.reshape(-1, shp[-1])
        res = adamw("adamw_" + k, two_d(w_loc[k]), two_d(m_loc[k]), two_d(v_loc[k]), [two_d(g_sum[i])])
        grads[k], deltas[k], new_m[k], new_v[k] = [r.reshape(shp) for r in res]
    small_loc_shapes = [w_loc[k].shape for k in SMALL]
    res = adamw("adamw_small", _pack_flat([w_loc[k] for k in SMALL], F32, LANES),
                _pack_flat([m_loc[k] for k in SMALL], F32, LANES), _pack_flat([v_loc[k] for k in SMALL], F32, LANES),
                [_pack_flat([gsum[k] for k in SMALL], F32, LANES)])
    for dst, r in zip((grads, deltas, new_m, new_v), res):
        dst.update(dict(zip(SMALL, _unpack_flat(r, small_loc_shapes))))

    return (loss, grad_x[None], *[grads[k] for k in WEIGHTS], *[deltas[k] for k in WEIGHTS],
            *[new_m[k] for k in WEIGHTS], *[new_v[k] for k in WEIGHTS])
```

```python
import functools
import math
from typing import NamedTuple

import jax
import jax.numpy as jnp
from jax import lax
from jax.experimental import pallas as pl
from jax.experimental.pallas import tpu as pltpu

F32 = jnp.float32
CDT = jnp.bfloat16
NORM_EPS = 1e-6
ROPE_THETA = 10000.0
GRID_W = 64
D_MODEL = 1024
ATTN_HEADS, ATTN_KV, ATTN_HD = 8, 2, 64
SSD_HEADS, SSD_HD, SSD_STATE = 8, 64, 128
RET_HEADS, RET_DK = 4, 128
CHUNK = 128
ROW_TILE = 256
MM_ROWS = 768
MM_VMEM_BUDGET = 44 * 1024 * 1024
ATTN_TQ, ATTN_TK = 256, 256
ATTN_TK_FWD = 512
LANES = 128
PAIR_BLOCK_BYTES = 2 * 1024 * 1024
COND_ROWS = 16
VMEM_LIMIT = 56 * 1024 * 1024

ADAM_LR, ADAM_B1, ADAM_B2, ADAM_EPS, ADAM_WD, ADAM_STEP = 0.001, 0.9, 0.999, 1e-08, 0.01, 10

IN_LAYOUT = {
    "gates": (0, 3072, 4368, 3072), "xbc": (3072, 1024, 1280, 1024), "q": (4096, 512, 0, 512),
    "z": (4608, 512, 768, 512), "rq": (5120, 512, 2320, 512), "rk": (5632, 512, 2832, 512),
    "rv": (6144, 512, 3344, 512), "rg": (6656, 512, 3856, 512), "k": (7168, 128, 512, 128),
    "v": (7296, 128, 640, 128), "dt": (7424, 128, 2304, 16),
}
IN_PAD = 7680
IN_ORIG_ORDER = ("q", "k", "v", "z", "xbc", "dt", "rq", "rk", "rv", "rg", "gates")
IN_NEW_ORDER = ("gates", "xbc", "q", "z", "rq", "rk", "rv", "rg", "k", "v", "dt")

BIG = ("w_mod", "w_in", "w_branch", "w_out", "w_mlp1", "w_mlp2")
BIG_KIND = {"w_mod": "cols", "w_in": "slices", "w_branch": "cols", "w_out": "rows", "w_mlp1": "cols", "w_mlp2": "rows"}
SMALL = ("c_ctx", "b_mod", "norm1_w", "norm2_w", "attn_q_norm", "attn_k_norm", "ssd_conv_w", "ssd_conv_b",
         "ssd_dt_bias", "ssd_a_log", "ssd_d", "ssd_norm_w", "ret_log_decay", "ret_gn_w", "final_norm_w")
WEIGHTS = ("c_ctx", "w_mod", "b_mod", "norm1_w", "norm2_w", "w_in", "attn_q_norm", "attn_k_norm", "ssd_conv_w",
           "ssd_conv_b", "ssd_dt_bias", "ssd_a_log", "ssd_d", "ssd_norm_w", "ret_log_decay", "ret_gn_w",
           "w_branch", "w_out", "w_mlp1", "w_mlp2", "final_norm_w")


def _cp(sem):
    return pltpu.CompilerParams(dimension_semantics=sem, vmem_limit_bytes=VMEM_LIMIT)


class Cols(NamedTuple):
    arr: jax.Array
    off: int
    width: int


def _width(item):
    return item.width if isinstance(item, Cols) else item.shape[1]


def _row_in(item, rows, imap=None):
    imap = imap or (lambda i: i)
    if isinstance(item, Cols):
        assert item.off % item.width == 0
        blk = item.off // item.width
        return item.arr, pl.BlockSpec((rows, item.width), lambda i, blk=blk: (imap(i), blk))
    return item, pl.BlockSpec((rows, item.shape[1]), lambda i: (imap(i), 0))


def _const_spec(shape):
    return pl.BlockSpec(shape, lambda *_: (0,) * len(shape))


def _mxu(a, b, dims=(((1,), (0,)), ((), ()))):
    return lax.dot_general(a.astype(CDT), b.astype(CDT), dims, preferred_element_type=F32)


_NT = (((1,), (1,)), ((), ()))
_TN = (((0,), (0,)), ((), ()))


@jax.custom_vjp
def _softplus(x):
    return jnp.maximum(x, 0.0) + jnp.log1p(jnp.exp(-jnp.abs(x)))


def _softplus_fwd(x):
    return _softplus(x), x


def _softplus_bwd(x, g):
    return (g * jax.nn.sigmoid(x),)


_softplus.defvjp(_softplus_fwd, _softplus_bwd)


def _group_mean_impl(x, gmat):
    hi = x.astype(CDT)
    lo = (x - hi.astype(F32)).astype(CDT)
    return (jnp.dot(hi, gmat, preferred_element_type=F32) + jnp.dot(lo, gmat, preferred_element_type=F32))


@jax.custom_vjp
def _group_mean(x, gmat):
    return _group_mean_impl(x, gmat)


def _group_mean_fwd(x, gmat):
    return _group_mean_impl(x, gmat), gmat


def _group_mean_bwd(gmat, g):
    return _group_mean_impl(g, gmat), jnp.zeros_like(gmat)


_group_mean.defvjp(_group_mean_fwd, _group_mean_bwd)


def _group_matrix(width, group):
    r = jnp.arange(width) // group
    return jnp.where(r[:, None] == r[None, :], 1.0 / group, 0.0).astype(CDT)


def _make_rope(half):
    def partner(x):
        w = x.shape[1]
        lane = lax.broadcasted_iota(jnp.int32, x.shape, 1)
        first = (lane % (2 * half)) < half
        return jnp.where(first, pltpu.roll(x, w - half, axis=1), pltpu.roll(x, half, axis=1))

    def impl(x, cos_full, sin_signed):
        return x * cos_full + partner(x) * sin_signed

    @jax.custom_vjp
    def rope(x, cos_full, sin_signed):
        return impl(x, cos_full, sin_signed)

    def fwd(x, cos_full, sin_signed):
        return impl(x, cos_full, sin_signed), (cos_full, sin_signed)

    def bwd(res, g):
        cos_full, sin_signed = res
        return impl(g, cos_full, -sin_signed), jnp.zeros_like(cos_full), jnp.zeros_like(sin_signed)

    rope.defvjp(fwd, bwd)
    return rope


_rope32 = _make_rope(32)
_rope64 = _make_rope(64)


def _rms(x, w):
    return x * lax.rsqrt(jnp.mean(x * x, axis=-1, keepdims=True) + NORM_EPS) * w


def _col(v, lane_index):
    lane = lax.broadcasted_iota(jnp.int32, v.shape, 1)
    return jnp.sum(jnp.where(lane == lane_index, v, 0.0), axis=1, keepdims=True)


def _typed_spec(width, nb_ctx):
    return pl.BlockSpec((None, 1, width), lambda i: (jnp.where(i >= nb_ctx, 1, 0), 0, 0))


def rowwise_fwd(name, f, rows, typed, shared, outs, n_rows, nb_ctx, tm=ROW_TILE):
    tm = min(tm, n_rows)
    nin = len(rows) + len(typed) + len(shared)

    def body(*refs):
        res = f(*[r[...] for r in refs[:nin]])
        for o_ref, o in zip(refs[nin:], res):
            o_ref[...] = o.astype(o_ref.dtype)

    arrs, specs = [], []
    for it in rows:
        a, s = _row_in(it, tm)
        arrs.append(a)
        specs.append(s)
    for t in typed:
        arrs.append(t)
        specs.append(_typed_spec(t.shape[-1], nb_ctx))
    for s_ in shared:
        arrs.append(s_)
        specs.append(_const_spec(s_.shape))
    res = pl.pallas_call(
        body, name=name, grid=(n_rows // tm,), in_specs=specs,
        out_specs=[pl.BlockSpec((tm, w), lambda i: (i, 0)) for w, _ in outs],
        out_shape=[jax.ShapeDtypeStruct((n_rows, w), dt) for w, dt in outs],
        compiler_params=_cp(("parallel",)))(*arrs)
    return res


def rowwise_bwd(name, f, rows, typed, shared, cots, row_diff, shared_diff, drow_dtypes, n_rows, nb_ctx, tm=ROW_TILE):
    tm = min(tm, n_rows)
    nr, nt, ns, nc = len(rows), len(typed), len(shared), len(cots)
    nin = nr + nt + ns
    d_rows = [k for k in range(nr) if row_diff[k]]
    d_sh = [k for k in range(ns) if shared_diff[k]]

    def body(*refs):
        rvals = [r[...] for r in refs[:nr]]
        tvals = [r[...] for r in refs[nr:nr + nt]]
        svals = [r[...] for r in refs[nr + nt:nin]]
        cvals = [r[...].astype(F32) for r in refs[nin:nin + nc]]
        out_refs = refs[nin + nc:]

        def g(*dv):
            dv = list(dv)
            rv = list(rvals)
            for k in d_rows:
                rv[k] = dv.pop(0)
            tv = [dv.pop(0) for _ in range(nt)]
            sv = list(svals)
            for k in d_sh:
                sv[k] = dv.pop(0)
            return tuple(o.astype(F32) for o in f(*rv, *tv, *sv))

        prim = [rvals[k].astype(F32) for k in d_rows] + tvals + [svals[k] for k in d_sh]
        _, vjp = jax.vjp(g, *prim)
        grads = list(vjp(tuple(cvals)))
        i = pl.program_id(0)
        for ref in out_refs[:len(d_rows)]:
            ref[...] = grads.pop(0).astype(ref.dtype)
        first_typed = (i == 0) | (i == nb_ctx)
        for ref in out_refs[len(d_rows):len(d_rows) + nt]:
            gr = grads.pop(0)

            @pl.when(first_typed)
            def _(ref=ref, gr=gr):
                ref[...] = gr

            @pl.when(jnp.logical_not(first_typed))
            def _(ref=ref, gr=gr):
                ref[...] += gr
        for ref in out_refs[len(d_rows) + nt:]:
            gr = grads.pop(0)

            @pl.when(i == 0)
            def _(ref=ref, gr=gr):
                ref[...] = gr

            @pl.when(i != 0)
            def _(ref=ref, gr=gr):
                ref[...] += gr

    arrs, specs = [], []
    for it in list(rows):
        a, s = _row_in(it, tm)
        arrs.append(a)
        specs.append(s)
    for t in typed:
        arrs.append(t)
        specs.append(_typed_spec(t.shape[-1], nb_ctx))
    for s_ in shared:
        arrs.append(s_)
        specs.append(_const_spec(s_.shape))
    for c_ in cots:
        a, s = _row_in(c_, tm)
        arrs.append(a)
        specs.append(s)
    out_specs, out_shape = [], []
    for k, dt in zip(d_rows, drow_dtypes):
        w = _width(rows[k])
        out_specs.append(pl.BlockSpec((tm, w), lambda i: (i, 0)))
        out_shape.append(jax.ShapeDtypeStruct((n_rows, w), dt))
    for t in typed:
        out_specs.append(_typed_spec(t.shape[-1], nb_ctx))
        out_shape.append(jax.ShapeDtypeStruct(t.shape, F32))
    for k in d_sh:
        out_specs.append(_const_spec(shared[k].shape))
        out_shape.append(jax.ShapeDtypeStruct(shared[k].shape, F32))
    res = pl.pallas_call(body, name=name, grid=(n_rows // tm,), in_specs=specs, out_specs=out_specs,
                         out_shape=out_shape, compiler_params=_cp(("arbitrary",)))(*arrs)
    n1, n2 = len(d_rows), len(d_rows) + nt
    return list(res[:n1]), list(res[n1:n2]), list(res[n2:])


def _pick(n, prefs):
    for p in prefs:
        if n % p == 0:
            return p
    return n


def mm(name, a, b, out_dtype, transpose_b=False):
    n, k = b.shape if transpose_b else b.shape[::-1]
    m = (a.arr if isinstance(a, Cols) else a).shape[0]
    assert _width(a) == k
    tm = _pick(m, (MM_ROWS, 256))
    osz = jnp.dtype(out_dtype).itemsize
    tn = next(c for c in (2560, 2048, 1536, 1024, 512, 256, 128, n)
              if n % c == 0 and 2 * (tm * k * 2 + k * c * 2 + tm * c * osz) <= MM_VMEM_BUDGET or c == n)
    dims = _NT if transpose_b else (((1,), (0,)), ((), ()))

    def body(a_ref, b_ref, o_ref):
        o_ref[...] = lax.dot_general(a_ref[...], b_ref[...], dims, preferred_element_type=F32).astype(o_ref.dtype)

    a_arr, a_spec = _row_in(a, tm)
    a_spec = pl.BlockSpec(a_spec.block_shape, lambda j, i, f=a_spec.index_map: f(i))
    b_spec = pl.BlockSpec((tn, k), lambda j, i: (j, 0)) if transpose_b else pl.BlockSpec((k, tn), lambda j, i: (0, j))
    return pl.pallas_call(
        body, name=name, grid=(n // tn, m // tm), in_specs=[a_spec, b_spec],
        out_specs=pl.BlockSpec((tm, tn), lambda j, i: (i, j)),
        out_shape=jax.ShapeDtypeStruct((m, n), out_dtype),
        compiler_params=_cp(("parallel", "parallel")))(a_arr, b)


def mm_tn(name, a, b, out_dtype=F32, pieces=None):
    t = (a.arr if isinstance(a, Cols) else a).shape[0]
    k, n = _width(a), _width(b)
    tt = _pick(t, (MM_ROWS, 256))
    k_unit = pieces[1] if pieces and pieces[0] == "rows" else k
    n_unit = pieces[1] if pieces and pieces[0] == "cols" else n
    tk = _pick(k_unit, (1024, 512, 256, 128))
    tn = _pick(n_unit, (1280, 1024, 512, 256, 128))
    n_t = t // tt

    def body(a_ref, b_ref, o_ref, acc):
        part = lax.dot_general(a_ref[...], b_ref[...], _TN, preferred_element_type=F32)
        ti = pl.program_id(2)

        @pl.when(ti == 0)
        def _():
            acc[...] = part

        @pl.when(ti != 0)
        def _():
            acc[...] += part

        @pl.when(ti == n_t - 1)
        def _():
            o_ref[...] = acc[...].astype(o_ref.dtype)

    def win(item, width):
        if isinstance(item, Cols):
            assert item.off % width == 0
            return item.arr, item.off // width
        return item, 0

    a_arr, a0 = win(a, tk)
    b_arr, b0 = win(b, tn)
    if pieces is None:
        out_spec = pl.BlockSpec((tk, tn), lambda ki, ni, ti: (ki, ni))
        out_shape = (k, n)
    elif pieces[0] == "cols":
        per = n_unit // tn
        out_spec = pl.BlockSpec((None, tk, tn), lambda ki, ni, ti: (ni // per, ki, ni % per))
        out_shape = (4, k, n_unit)
    else:
        per = k_unit // tk
        out_spec = pl.BlockSpec((None, tk, tn), lambda ki, ni, ti: (ki // per, ki % per, ni))
        out_shape = (4, k_unit, n)
    return pl.pallas_call(
        body, name=name, grid=(k // tk, n // tn, n_t),
        in_specs=[pl.BlockSpec((tt, tk), lambda ki, ni, ti: (ti, a0 + ki)),
                  pl.BlockSpec((tt, tn), lambda ki, ni, ti: (ti, b0 + ni))],
        out_specs=out_spec, out_shape=jax.ShapeDtypeStruct(out_shape, out_dtype),
        scratch_shapes=[pltpu.VMEM((tk, tn), F32)],
        compiler_params=_cp(("parallel", "parallel", "arbitrary")))(a_arr, b_arr)


def _heads_t(rows_blk):
    blk = rows_blk.astype(F32).T
    return jnp.concatenate([blk[hh * ATTN_HD:(hh + 1) * ATTN_HD, :] for hh in range(4)], axis=1)


def _heads_rows(t_blk):
    tq = t_blk.shape[1] // 4
    return jnp.concatenate([t_blk[:, hh * tq:(hh + 1) * tq] for hh in range(4)], axis=0).T


def attn_fwd(name, q, kk, vT, m_ctx, tq):
    t, hd = kk.shape[1], ATTN_HD
    nq, r = t // tq, 4 * tq
    tk = _pick(t - m_ctx, (ATTN_TK_FWD, ATTN_TK))
    nqc, n_lat_tiles = m_ctx // tq, (t - m_ctx) // tk

    def body(q_ref, k_ref, vT_ref, o_ref, qT_ref, oT_ref, lse_ref):
        i = pl.program_id(1)
        q_t = _heads_t(q_ref[...]).astype(CDT)
        qT_ref[...] = q_t

        def tile(off, size, carry):
            mi, li, acc = carry
            st = jnp.dot(k_ref[pl.ds(off, size), :], q_t, preferred_element_type=F32)
            mn = jnp.maximum(mi, jnp.max(st, axis=0, keepdims=True))
            pt = jnp.exp(st - mn)
            al = jnp.exp(mi - mn)
            li = al * li + jnp.sum(pt, axis=0, keepdims=True)
            acc = al * acc + jnp.dot(vT_ref[:, pl.ds(off, size)], pt.astype(CDT), preferred_element_type=F32)
            return mn, li, acc

        init = (jnp.full((1, r), -1e30, F32), jnp.zeros((1, r), F32), jnp.zeros((hd, r), F32))
        carry = tile(0, m_ctx, init)
        mi, li, acc = lax.fori_loop(
            0, jnp.where(i < nqc, 0, n_lat_tiles),
            lambda j, cr: tile(pl.multiple_of(m_ctx + j * tk, ATTN_TK), tk, cr), carry)
        o_t = acc / li
        oT_ref[...] = o_t.astype(oT_ref.dtype)
        o_ref[...] = _heads_rows(o_t).astype(o_ref.dtype)
        lse_ref[...] = mi + jnp.log(li)

    blk_t = pl.BlockSpec((None, None, hd, r), lambda g, i: (g, i, 0, 0))
    rows = pl.BlockSpec((tq, 4 * hd), lambda g, i: (i, g))
    return pl.pallas_call(
        body, name=name, grid=(2, nq),
        in_specs=[rows, pl.BlockSpec((None, t, hd), lambda g, i: (g, 0, 0)),
                  pl.BlockSpec((None, hd, t), lambda g, i: (g, 0, 0))],
        out_specs=[rows, blk_t, blk_t, pl.BlockSpec((None, None, 1, r), lambda g, i: (g, i, 0, 0))],
        out_shape=[jax.ShapeDtypeStruct((t, 8 * hd), CDT), jax.ShapeDtypeStruct((2, nq, hd, r), CDT),
                   jax.ShapeDtypeStruct((2, nq, hd, r), CDT), jax.ShapeDtypeStruct((2, nq, 1, r), F32)],
        compiler_params=_cp(("parallel", "arbitrary")))(q, kk, vT)


def attn_bwd(name, qT, do, oT, lse, kk, kT, vv, m_ctx):
    _, nq, hd, r = qT.shape
    t = kk.shape[1]
    tq, tk = r // 4, ATTN_TK
    nqc, nkc, nk = m_ctx // tq, m_ctx // tk, t // tk

    def body(qT_ref, do_ref, oT_ref, lse_ref, k_ref, kT_ref, v_ref,
             dq_ref, dk_ref, dv_ref, dk_acc, dv_acc, delta_s, doT_s):
        j, i = pl.program_id(1), pl.program_id(2)

        @pl.when(i == 0)
        def _():
            dk_acc[...] = jnp.zeros_like(dk_acc)
            dv_acc[...] = jnp.zeros_like(dv_acc)

        @pl.when(j == 0)
        def _():
            do_t = _heads_t(do_ref[...])
            doT_s[i] = do_t.astype(CDT)
            delta_s[i] = jnp.sum(do_t * oT_ref[...].astype(F32), axis=0, keepdims=True)

        @pl.when((j < nkc) | (i >= nqc))
        def _():
            do_t = doT_s[i]
            st = jnp.dot(k_ref[...], qT_ref[...], preferred_element_type=F32)
            pt = jnp.exp(st - lse_ref[...])
            dv_acc[...] += lax.dot_general(pt.astype(CDT), do_t, _NT, preferred_element_type=F32)
            dpt = jnp.dot(v_ref[...], do_t, preferred_element_type=F32)
            dst = (pt * (dpt - delta_s[i])).astype(CDT)
            dk_acc[...] += lax.dot_general(dst, qT_ref[...], _NT, preferred_element_type=F32)
            part = jnp.dot(kT_ref[...], dst, preferred_element_type=F32)

            @pl.when(j == 0)
            def _():
                dq_ref[i] = part

            @pl.when(j != 0)
            def _():
                dq_ref[i] += part

        @pl.when(i == nq - 1)
        def _():
            dk_ref[...] = dk_acc[...]
            dv_ref[...] = dv_acc[...]

    blk_t = pl.BlockSpec((None, None, hd, r), lambda g, j, i: (g, i, 0, 0))
    row = pl.BlockSpec((None, None, 1, r), lambda g, j, i: (g, i, 0, 0))
    kv = pl.BlockSpec((None, tk, hd), lambda g, j, i: (g, j, 0))
    kv_t = pl.BlockSpec((None, hd, tk), lambda g, j, i: (g, 0, j))
    rows = pl.BlockSpec((tq, 4 * hd), lambda g, j, i: (i, g))
    return pl.pallas_call(
        body, name=name, grid=(2, nk, nq),
        in_specs=[blk_t, rows, blk_t, row, kv, kv_t, kv],
        out_specs=[pl.BlockSpec((None, nq, hd, r), lambda g, j, i: (g, 0, 0, 0)), kv, kv],
        out_shape=[jax.ShapeDtypeStruct(qT.shape, F32), jax.ShapeDtypeStruct(kk.shape, F32),
                   jax.ShapeDtypeStruct(kk.shape, F32)],
        scratch_shapes=[pltpu.VMEM((tk, hd), F32), pltpu.VMEM((tk, hd), F32), pltpu.VMEM((nq, 1, r), F32),
                        pltpu.VMEM((nq, hd, r), CDT)],
        compiler_params=_cp(("parallel", "arbitrary", "arbitrary")))(qT, do, oT, lse, kk, kT, vv)


def from_heads_t(name, a_t, out_dtype):
    _, nq, hd, r = a_t.shape

    def body(a_ref, o_ref):
        o_ref[...] = _heads_rows(a_ref[...].astype(F32)).astype(o_ref.dtype)

    tq = r // 4
    return pl.pallas_call(
        body, name=name, grid=(2, nq), in_specs=[pl.BlockSpec((None, None, hd, r), lambda g, i: (g, i, 0, 0))],
        out_specs=pl.BlockSpec((tq, 4 * hd), lambda g, i: (i, g)),
        out_shape=jax.ShapeDtypeStruct((nq * tq, 8 * hd), out_dtype),
        compiler_params=_cp(("parallel", "parallel")))(a_t)


def _split_kv(a):
    return a.reshape(a.shape[0], 2, ATTN_HD).transpose(1, 0, 2)


def _merge_kv(a):
    return a.transpose(1, 0, 2).reshape(a.shape[1], 2 * ATTN_HD)


def _chunk_order(rev, ncc, nct):
    if not rev:
        return lambda s: s
    return lambda s: jnp.where(s < ncc, ncc - 1 - s, nct - 1 - (s - ncc))


def scan_fwd(name, fn, rows, shared, n_state, y_width, n_rows, m_ctx, rev):
    nct, ncc = n_rows // CHUNK, m_ctx // CHUNK
    order = _chunk_order(rev, ncc, nct)
    nr, ns = len(rows), len(shared)

    def body(*refs):
        rvals = [r[...] for r in refs[:nr]]
        svals = [r[...] for r in refs[nr:nr + ns]]
        y_ref, sin_ref, st = refs[nr + ns:]

        @pl.when(pl.program_id(0) == 0)
        def _():
            st[...] = jnp.zeros_like(st)

        prev = [st[k] for k in range(n_state)]
        sin_ref[...] = st[...]
        y, new = fn(rvals, svals, prev)
        y_ref[...] = y
        for k in range(n_state):
            st[k] = new[k]

    arrs, specs = [], []
    for it in rows:
        a, s = _row_in(it, CHUNK, order)
        arrs.append(a)
        specs.append(s)
    for s_ in shared:
        arrs.append(s_)
        specs.append(_const_spec(s_.shape))
    return pl.pallas_call(
        body, name=name, grid=(nct,), in_specs=specs,
        out_specs=[pl.BlockSpec((CHUNK, y_width), lambda s: (order(s), 0)),
                   pl.BlockSpec((None, n_state, LANES, LANES), lambda s: (order(s), 0, 0, 0))],
        out_shape=[jax.ShapeDtypeStruct((n_rows, y_width), F32),
                   jax.ShapeDtypeStruct((nct, n_state, LANES, LANES), F32)],
        scratch_shapes=[pltpu.VMEM((n_state, LANES, LANES), F32)],
        compiler_params=_cp(("arbitrary",)))(*arrs)


def scan_bwd(name, fn, rows, shared, states_in, dy, post, outs, addends, n_state, n_rows, m_ctx, rev):
    nct, ncc = n_rows // CHUNK, m_ctx // CHUNK
    fwd_order = _chunk_order(rev, ncc, nct)
    order = lambda r: fwd_order(nct - 1 - r)
    nr, ns, na = len(rows), len(shared), len(addends)
    nin = nr + ns

    def body(*refs):
        rvals = [r[...] for r in refs[:nr]]
        svals = [r[...] for r in refs[nr:nin]]
        sin_ref, dy_ref = refs[nin], refs[nin + 1]
        add_refs = refs[nin + 2:nin + 2 + na]
        out_refs = refs[nin + 2 + na:nin + 2 + na + len(outs)]
        dsh_refs = refs[nin + 2 + na + len(outs):-1]
        dst = refs[-1]
        r = pl.program_id(0)

        @pl.when(r == 0)
        def _():
            dst[...] = jnp.zeros_like(dst)

        prev = [sin_ref[k] for k in range(n_state)]
        _, vjp = jax.vjp(fn, rvals, svals, prev)
        d_rows, d_shared, d_prev = vjp((dy_ref[...], [dst[k] for k in range(n_state)]))
        res = post(d_rows)
        for k, (ref, val) in enumerate(zip(out_refs, res)):
            if k < na:
                val = val + add_refs[k][...]
            ref[...] = val.astype(ref.dtype)
        for ref, gr in zip(dsh_refs, d_shared):
            @pl.when(r == 0)
            def _(ref=ref, gr=gr):
                ref[...] = gr

            @pl.when(r != 0)
            def _(ref=ref, gr=gr):
                ref[...] += gr
        for k in range(n_state):
            dst[k] = d_prev[k]

    arrs, specs = [], []
    for it in rows:
        a, s = _row_in(it, CHUNK, order)
        arrs.append(a)
        specs.append(s)
    for s_ in shared:
        arrs.append(s_)
        specs.append(_const_spec(s_.shape))
    arrs.append(states_in)
    specs.append(pl.BlockSpec((None, n_state, LANES, LANES), lambda r: (order(r), 0, 0, 0)))
    for it in [dy] + list(addends):
        a, s = _row_in(it, CHUNK, order)
        arrs.append(a)
        specs.append(s)
    out_specs = [pl.BlockSpec((CHUNK, w), lambda r: (order(r), 0)) for w, _ in outs]
    out_shape = [jax.ShapeDtypeStruct((n_rows, w), dt) for w, dt in outs]
    for s_ in shared:
        out_specs.append(_const_spec(s_.shape))
        out_shape.append(jax.ShapeDtypeStruct(s_.shape, F32))
    res = pl.pallas_call(body, name=name, grid=(nct,), in_specs=specs, out_specs=out_specs, out_shape=out_shape,
                         scratch_shapes=[pltpu.VMEM((n_state, LANES, LANES), F32)],
                         compiler_params=_cp(("arbitrary",)))(*arrs)
    return list(res[:len(outs)]), list(res[len(outs):])


def _make_ssd_chunk(direction):
    rev = direction == 1
    base = 8 * direction

    def fn(rows, shared, prev):
        xs, bms, cms, dtraw = rows[0:4], rows[4:6], rows[6:8], rows[8]
        dt_bias, a_log = shared
        ln = dtraw.shape[0]
        dt_all = _softplus(dtraw + dt_bias)
        a_all = dt_all * (-jnp.exp(a_log))
        r_i = lax.broadcasted_iota(jnp.int32, (ln, ln), 0)
        c_i = lax.broadcasted_iota(jnp.int32, (ln, ln), 1)
        tri = (r_i <= c_i) if rev else (r_i >= c_i)
        a_cum_all = jnp.dot(tri.astype(F32), a_all, precision=lax.Precision.HIGHEST, preferred_element_type=F32)
        a_tot_all = jnp.sum(a_all, axis=0, keepdims=True)
        first = lax.broadcasted_iota(jnp.int32, (ln, LANES), 1) < SSD_HD
        first_row = lax.broadcasted_iota(jnp.int32, (LANES, 1), 0) < SSD_HD

        def lmat(acol):
            a_b = jnp.broadcast_to(acol, (ln, ln))
            seg = a_b - a_b.T
            return jnp.where(tri, jnp.exp(jnp.where(tri, seg, 0.0)), 0.0)

        ys, new = [], []
        for g in range(2):
            bm, cm = bms[g], cms[g]
            cb = _mxu(cm, bm, _NT)
            for jj in range(2):
                pr = 2 * g + jj
                h0, h1 = base + 2 * pr, base + 2 * pr + 1
                ac0, ac1 = _col(a_cum_all, h0), _col(a_cum_all, h1)
                at0, at1 = _col(a_tot_all, h0), _col(a_tot_all, h1)
                dt_pair = jnp.where(first, _col(dt_all, h0), _col(dt_all, h1))
                acum_pair = jnp.where(first, ac0, ac1)
                atot_pair = jnp.where(first[0:1], at0, at1)
                xd = xs[pr] * dt_pair
                st = _mxu(xd * jnp.exp(atot_pair - acum_pair), bm, _TN)
                new.append(prev[pr] * jnp.where(first_row, jnp.exp(at0), jnp.exp(at1)) + st)
                y0 = _mxu(cb * lmat(ac0), xd)
                y1 = _mxu(cb * lmat(ac1), xd)
                y_off = _mxu(cm, prev[pr], _NT) * jnp.exp(acum_pair)
                ys.append(jnp.where(first, y0, y1) + y_off)
        return jnp.concatenate(ys, axis=1), new

    return fn


def _make_ret_chunk(direction):
    rev = direction == 1
    base = 4 * direction

    def fn(rows, shared, prev):
        qs, ks, vs = rows[0:4], rows[4:8], rows[8:12]
        lg_all = -jnp.exp(shared[0])
        ln = qs[0].shape[0]
        pos = lax.broadcasted_iota(jnp.int32, (ln, 1), 0).astype(F32)
        r_i = lax.broadcasted_iota(jnp.int32, (ln, ln), 0)
        c_i = lax.broadcasted_iota(jnp.int32, (ln, ln), 1)
        diff = ((c_i - r_i) if rev else (r_i - c_i))
        mask = diff >= 0
        dpos = jnp.maximum(diff, 0).astype(F32)
        k_pow = pos if rev else (ln - 1.0 - pos)
        q_pow = (ln - pos) if rev else (pos + 1.0)
        ys, new = [], []
        for h in range(RET_HEADS):
            lg = _col(lg_all, base + h)
            dmat = jnp.where(mask, jnp.exp(dpos * lg), 0.0)
            st = _mxu(ks[h] * jnp.exp(k_pow * lg), vs[h], _TN)
            new.append(prev[h] * jnp.exp(ln * lg) + st)
            s = _mxu(qs[h], ks[h], _NT) * dmat
            ys.append(_mxu(s, vs[h]) + _mxu(qs[h], prev[h]) * jnp.exp(q_pow * lg))
        return jnp.concatenate(ys, axis=1), new

    return fn


def _conv_pre(x, w, b, t_idx, n_rows, m_ctx):
    is_start = (t_idx == 0) | (t_idx == m_ctx)
    is_end = (t_idx == m_ctx - 1) | (t_idx == n_rows - 1)
    xp = jnp.where(is_start, 0.0, pltpu.roll(x, 1, axis=0))
    xn = jnp.where(is_end, 0.0, pltpu.roll(x, n_rows - 1, axis=0))
    return w[0:1] * xp + w[1:2] * x + w[2:3] * xn + b, xp, xn, is_start, is_end


def conv_fwd(x, conv_w, conv_b, m_ctx):
    n_rows, width = x.arr.shape[0], x.width
    c0 = x.off // LANES

    def body(x_ref, w_ref, b_ref, o_ref):
        t_idx = lax.broadcasted_iota(jnp.int32, (n_rows, 1), 0)
        pre = _conv_pre(x_ref[...], w_ref[...], b_ref[...], t_idx, n_rows, m_ctx)[0]
        o_ref[...] = pre * jax.nn.sigmoid(pre)

    return pl.pallas_call(
        body, name="conv_fwd", grid=(width // LANES,),
        in_specs=[pl.BlockSpec((n_rows, LANES), lambda c: (0, c0 + c)),
                  pl.BlockSpec((3, LANES), lambda c: (0, c)), pl.BlockSpec((1, LANES), lambda c: (0, c))],
        out_specs=pl.BlockSpec((n_rows, LANES), lambda c: (0, c)),
        out_shape=jax.ShapeDtypeStruct((n_rows, width), F32),
        compiler_params=_cp(("parallel",)))(x.arr, conv_w, conv_b)


def conv_bwd(x, conv_w, conv_b, dy, dxs_extra, m_ctx):
    n_rows, width = x.arr.shape[0], x.width
    c0 = x.off // LANES
    n_extra = dxs_extra.shape[1] // LANES

    def body(x_ref, w_ref, b_ref, dy_ref, ex_ref, dx_ref, dw_ref, db_ref):
        c = pl.program_id(0)
        t_idx = lax.broadcasted_iota(jnp.int32, (n_rows, 1), 0)
        w = w_ref[...]
        pre, xp, xn, is_start, is_end = _conv_pre(x_ref[...], w, b_ref[...], t_idx, n_rows, m_ctx)
        sg = jax.nn.sigmoid(pre)
        dyv = dy_ref[...] + jnp.where(c < n_extra, ex_ref[...], 0.0)
        dpre = dyv * (sg * (1.0 + pre * (1.0 - sg)))
        d_next = jnp.where(is_end, 0.0, pltpu.roll(dpre, n_rows - 1, axis=0))
        d_prev = jnp.where(is_start, 0.0, pltpu.roll(dpre, 1, axis=0))
        dx_ref[...] = (w[1:2] * dpre + w[0:1] * d_next + w[2:3] * d_prev).astype(dx_ref.dtype)
        dw_ref[...] = jnp.concatenate([jnp.sum(dpre * xp, axis=0, keepdims=True),
                                       jnp.sum(dpre * x_ref[...], axis=0, keepdims=True),
                                       jnp.sum(dpre * xn, axis=0, keepdims=True)], axis=0)
        db_ref[...] = jnp.sum(dpre, axis=0, keepdims=True)

    return pl.pallas_call(
        body, name="conv_bwd", grid=(width // LANES,),
        in_specs=[pl.BlockSpec((n_rows, LANES), lambda c: (0, c0 + c)),
                  pl.BlockSpec((3, LANES), lambda c: (0, c)), pl.BlockSpec((1, LANES), lambda c: (0, c)),
                  pl.BlockSpec((n_rows, LANES), lambda c: (0, c)),
                  pl.BlockSpec((n_rows, LANES), lambda c: (0, jnp.minimum(c, n_extra - 1)))],
        out_specs=[pl.BlockSpec((n_rows, LANES), lambda c: (0, c)),
                   pl.BlockSpec((3, LANES), lambda c: (0, c)), pl.BlockSpec((1, LANES), lambda c: (0, c))],
        out_shape=[jax.ShapeDtypeStruct((n_rows, width), CDT), jax.ShapeDtypeStruct((3, width), F32),
                   jax.ShapeDtypeStruct((1, width), F32)],
        compiler_params=_cp(("parallel",)))(x.arr, conv_w, conv_b, dy, dxs_extra)


def loss_head(h, target, final_w, m_ctx):
    n_rows, d = h.shape
    tm = min(ROW_TILE, n_rows)
    nb_ctx = m_ctx // tm

    def f(hb, w, tgt):
        err = _rms(hb, w) - tgt
        return 0.5 * jnp.sum(jnp.mean(err * err, axis=-1))

    def body(h_ref, t_ref, w_ref, loss_ref, dh_ref, dw_ref):
        i = pl.program_id(0)

        @pl.when(i < nb_ctx)
        def _():
            dh_ref[...] = jnp.zeros_like(dh_ref)

        @pl.when(i == 0)
        def _():
            loss_ref[...] = jnp.zeros_like(loss_ref)
            dw_ref[...] = jnp.zeros_like(dw_ref)

        @pl.when(i >= nb_ctx)
        def _():
            val, vjp = jax.vjp(lambda hb, w: f(hb, w, t_ref[...]), h_ref[...], w_ref[...])
            dh, dw = vjp(jnp.ones((), F32))
            dh_ref[...] = dh
            dw_ref[...] += dw
            loss_ref[...] += jnp.broadcast_to(val, loss_ref.shape)

    return pl.pallas_call(
        body, name="loss_head", grid=(n_rows // tm,),
        in_specs=[pl.BlockSpec((tm, d), lambda i: (i, 0)),
                  pl.BlockSpec((tm, d), lambda i: (jnp.maximum(i - nb_ctx, 0), 0)), _const_spec((1, d))],
        out_specs=[_const_spec((1, LANES)), pl.BlockSpec((tm, d), lambda i: (i, 0)), _const_spec((1, d))],
        out_shape=[jax.ShapeDtypeStruct((1, LANES), F32), jax.ShapeDtypeStruct((n_rows, d), F32),
                   jax.ShapeDtypeStruct((1, d), F32)],
        compiler_params=_cp(("arbitrary",)))(h, target, final_w)


def adamw(name, w, m, v, g_parts):
    rows, cols = w.shape
    tr = _pick(rows, (256, 128, 64, 32, 16, 8))
    npart = len(g_parts)
    c1 = 1.0 - ADAM_B1 ** ADAM_STEP
    c2 = 1.0 - ADAM_B2 ** ADAM_STEP

    def body(*refs):
        w_ref, m_ref, v_ref = refs[:3]
        g = refs[3][...].astype(F32)
        for r in refs[4:3 + npart]:
            g = g + r[...].astype(F32)
        g_ref, d_ref, nm_ref, nv_ref = refs[3 + npart:]
        nm = ADAM_B1 * m_ref[...] + (1.0 - ADAM_B1) * g
        nv = ADAM_B2 * v_ref[...] + (1.0 - ADAM_B2) * (g * g)
        g_ref[...] = g
        nm_ref[...] = nm
        nv_ref[...] = nv
        d_ref[...] = -ADAM_LR * ((nm / c1) / (jnp.sqrt(nv / c2) + ADAM_EPS) + ADAM_WD * w_ref[...])

    spec = pl.BlockSpec((tr, cols), lambda i: (i, 0))
    return pl.pallas_call(
        body, name=name, grid=(rows // tr,), in_specs=[spec] * (3 + npart), out_specs=[spec] * 4,
        out_shape=[jax.ShapeDtypeStruct((rows, cols), F32)] * 4, compiler_params=_cp(("parallel",)))(w, m, v, *g_parts)


MESH = pl.DeviceIdType.MESH
_HBM = pl.BlockSpec(memory_space=pl.ANY)


def _chip_peers():
    x, y, c = lax.axis_index("x"), lax.axis_index("y"), lax.axis_index("c")
    return x, y, c, [(1 - x, y), (x, 1 - y), (1 - x, 1 - y)]


def _window(ref, kind, chip, rows, cols):
    if kind == "cols":
        return ref.at[:, pl.ds(pl.multiple_of(chip * cols, LANES), cols)]
    if kind == "rows":
        return ref.at[pl.ds(pl.multiple_of(chip * rows, 8), rows), :]
    return ref.at[chip]


def _gathered_shape(kind, rows, cols):
    return {"cols": (rows, 4 * cols), "rows": (4 * rows, cols), "slices": (4, rows, cols)}[kind]


def gather_layers(name, shards, kinds):
    n = len(shards)

    def body(*refs):
        x_refs, o_refs = refs[:n], refs[n:2 * n]
        send_sems, recv_sems, local_sems = refs[2 * n:]
        x, y, c, peers = _chip_peers()
        me = 2 * x + y
        started = []
        for a in range(n):
            _, rows, cols = shards[a].shape
            src = x_refs[a].at[c]
            mine = pltpu.make_async_copy(src, _window(o_refs[a], kinds[a], me, rows, cols), local_sems.at[a])
            mine.start()
            started.append(mine.wait)
            for k, (px, py) in enumerate(peers):
                cp = pltpu.make_async_remote_copy(
                    src_ref=src, dst_ref=_window(o_refs[a], kinds[a], me, rows, cols), send_sem=send_sems.at[3 * a + k],
                    recv_sem=recv_sems.at[3 * a + k], device_id=(px, py, c), device_id_type=MESH)
                cp.start()
                started.append(cp.wait_send)
        for a in range(n):
            _, rows, cols = shards[a].shape
            for k, (px, py) in enumerate(peers):
                pltpu.make_async_remote_copy(
                    src_ref=x_refs[a].at[c], dst_ref=_window(o_refs[a], kinds[a], 2 * px + py, rows, cols),
                    send_sem=send_sems.at[3 * a + k], recv_sem=recv_sems.at[3 * a + k], device_id=(px, py, c),
                    device_id_type=MESH).wait_recv()
        for wait in started:
            wait()

    return pl.pallas_call(
        body, name=name, in_specs=[_HBM] * n, out_specs=[_HBM] * n,
        out_shape=[jax.ShapeDtypeStruct(_gathered_shape(kinds[a], *shards[a].shape[1:]), shards[a].dtype)
                   for a in range(n)],
        scratch_shapes=[pltpu.SemaphoreType.DMA((3 * n,)), pltpu.SemaphoreType.DMA((3 * n,)),
                        pltpu.SemaphoreType.DMA((n,))],
        )(*shards)


def scatter_pieces(name, pieces):
    n = len(pieces)

    def body(*refs):
        p_refs, o_refs = refs[:n], refs[n:2 * n]
        send_sems, recv_sems, local_sems = refs[2 * n:]
        x, y, c, peers = _chip_peers()
        me = 2 * x + y
        started = []
        for a in range(n):
            mine = pltpu.make_async_copy(p_refs[a].at[me], o_refs[a].at[me], local_sems.at[a])
            mine.start()
            started.append(mine.wait)
            for k, (px, py) in enumerate(peers):
                cp = pltpu.make_async_remote_copy(
                    src_ref=p_refs[a].at[2 * px + py], dst_ref=o_refs[a].at[me], send_sem=send_sems.at[3 * a + k],
                    recv_sem=recv_sems.at[3 * a + k], device_id=(px, py, c), device_id_type=MESH)
                cp.start()
                started.append(cp.wait_send)
        for a in range(n):
            for k, (px, py) in enumerate(peers):
                pltpu.make_async_remote_copy(
                    src_ref=p_refs[a].at[me], dst_ref=o_refs[a].at[2 * px + py], send_sem=send_sems.at[3 * a + k],
                    recv_sem=recv_sems.at[3 * a + k], device_id=(px, py, c), device_id_type=MESH).wait_recv()
        for wait in started:
            wait()

    return pl.pallas_call(
        body, name=name, in_specs=[_HBM] * n, out_specs=[_HBM] * n,
        out_shape=[jax.ShapeDtypeStruct(p.shape, p.dtype) for p in pieces],
        scratch_shapes=[pltpu.SemaphoreType.DMA((3 * n,)), pltpu.SemaphoreType.DMA((3 * n,)),
                        pltpu.SemaphoreType.DMA((n,))],
        )(*pieces)


def _pair_step(n_steps, x_ref, land, send_sems, recv_sems, credits, consume):
    x, y, c = lax.axis_index("x"), lax.axis_index("y"), lax.axis_index("c")
    sib = (x, y, 1 - c)
    i = pl.program_id(0)
    slot = i % 2

    @pl.when(i >= 2)
    def _():
        pl.semaphore_wait(credits.at[slot], 1)

    cp = pltpu.make_async_remote_copy(src_ref=x_ref, dst_ref=land.at[slot], send_sem=send_sems.at[slot],
                                      recv_sem=recv_sems.at[slot], device_id=sib, device_id_type=MESH)
    cp.start()
    cp.wait_recv()
    consume(land[slot])

    @pl.when(i < n_steps - 2)
    def _():
        pl.semaphore_signal(credits.at[slot], inc=1, device_id=sib, device_id_type=MESH)

    cp.wait_send()


def _pair_call(name, body, n_steps, in_specs, out_spec, out_shape, blk_shape, dtype, operands, extra_scratch=()):
    grid_spec = pltpu.PrefetchScalarGridSpec(
        num_scalar_prefetch=1, grid=(n_steps,), in_specs=in_specs, out_specs=out_spec,
        scratch_shapes=[pltpu.VMEM((2,) + blk_shape, dtype), pltpu.SemaphoreType.DMA((2,)),
                        pltpu.SemaphoreType.DMA((2,)), pltpu.SemaphoreType.REGULAR((2,)), *extra_scratch])
    return pl.pallas_call(body, name=name, grid_spec=grid_spec, out_shape=out_shape,
                          compiler_params=_cp(("arbitrary",)))(*operands)


def _place():
    return jnp.stack([lax.axis_index("x"), lax.axis_index("y"), lax.axis_index("c")]).astype(jnp.int32)


def _pair_rows(rows, row_bytes):
    for cand in (4096, 2048, 1024, 768, 512, 384, 256, 192, 128, 96, 64, 48, 32, 16):
        if rows % cand == 0 and cand * row_bytes <= PAIR_BLOCK_BYTES:
            return cand
    return _pick(rows, (16, 8))


def exchange_both(name, mine):
    rows, cols = mine.shape
    tr = _pair_rows(rows, cols * mine.dtype.itemsize)
    n_steps = rows // tr

    def body(s_ref, x_ref, o_ref, land, send_sems, recv_sems, credits):
        c = lax.axis_index("c")
        o_ref[c] = x_ref[...]

        def consume(v):
            o_ref[1 - c] = v
        _pair_step(n_steps, x_ref, land, send_sems, recv_sems, credits, consume)

    return _pair_call(name, body, n_steps, [pl.BlockSpec((tr, cols), lambda i, s: (i, 0))],
                      pl.BlockSpec((2, tr, cols), lambda i, s: (0, i, 0)),
                      jax.ShapeDtypeStruct((2, rows, cols), mine.dtype), (tr, cols), mine.dtype, (_place(), mine))


def exchange_add(name, both_layers):
    _, rows, cols = both_layers.shape
    tr = _pair_rows(rows, cols * both_layers.dtype.itemsize)
    nb = rows // tr
    flat = both_layers.reshape(2 * rows, cols)

    def body(s_ref, x_ref, m_ref, o_ref, land, send_sems, recv_sems, credits):
        def consume(v):
            o_ref[...] = (m_ref[...].astype(F32) + v.astype(F32)).astype(o_ref.dtype)
        _pair_step(nb, x_ref, land, send_sems, recv_sems, credits, consume)

    return _pair_call(name, body, nb,
                      [pl.BlockSpec((tr, cols), lambda i, s: ((1 - s[2]) * nb + i, 0)),
                       pl.BlockSpec((tr, cols), lambda i, s: (s[2] * nb + i, 0))],
                      pl.BlockSpec((tr, cols), lambda i, s: (i, 0)), jax.ShapeDtypeStruct((rows, cols), CDT),
                      (tr, cols), flat.dtype, (_place(), flat, flat))


def sum_exchange(name, parts):
    npart, rows, cols = parts.shape
    tr = _pair_rows(rows, cols * 4)
    n_steps = rows // tr

    def body(s_ref, x_ref, o_ref, land, send_sems, recv_sems, credits, mine):
        c = lax.axis_index("c")
        acc = x_ref[0].astype(F32)
        for k in range(1, npart):
            acc = acc + x_ref[k].astype(F32)
        mine[...] = acc
        o_ref[c] = acc

        def consume(v):
            o_ref[1 - c] = v
        _pair_step(n_steps, mine, land, send_sems, recv_sems, credits, consume)

    return _pair_call(name, body, n_steps, [pl.BlockSpec((npart, tr, cols), lambda i, s: (0, i, 0))],
                      pl.BlockSpec((2, tr, cols), lambda i, s: (0, i, 0)),
                      jax.ShapeDtypeStruct((2, rows, cols), F32), (tr, cols), F32, (_place(), parts),
                      extra_scratch=(pltpu.VMEM((tr, cols), F32),))


def allreduce_small(name, buf):
    rows = buf.shape[0]

    def body(x_ref, out_ref, gath, send_sems, recv_sems):
        x, y, c = lax.axis_index("x"), lax.axis_index("y"), lax.axis_index("c")
        me = 4 * x + 2 * y + c
        masks = [(k >> 2 & 1, k >> 1 & 1, k & 1) for k in range(1, 8)]

        def flip(v, bit):
            return 1 - v if bit else v

        sends = []
        for k, (bx, by, bc) in enumerate(masks):
            cp = pltpu.make_async_remote_copy(src_ref=x_ref, dst_ref=gath.at[me], send_sem=send_sems.at[k],
                                              recv_sem=recv_sems.at[k],
                                              device_id=(flip(x, bx), flip(y, by), flip(c, bc)), device_id_type=MESH)
            cp.start()
            sends.append(cp)
        gath[me] = x_ref[...]
        for k, (bx, by, bc) in enumerate(masks):
            px, py, pc = flip(x, bx), flip(y, by), flip(c, bc)
            pltpu.make_async_remote_copy(src_ref=x_ref, dst_ref=gath.at[4 * px + 2 * py + pc],
                                         send_sem=send_sems.at[k], recv_sem=recv_sems.at[k],
                                         device_id=(px, py, pc), device_id_type=MESH).wait_recv()
        for cp in sends:
            cp.wait_send()
        acc = gath[0]
        for d in range(1, 8):
            acc = acc + gath[d]
        out_ref[...] = acc

    return pl.pallas_call(
        body, name=name, in_specs=[pl.BlockSpec(memory_space=pltpu.VMEM)],
        out_specs=pl.BlockSpec(memory_space=pltpu.VMEM), out_shape=jax.ShapeDtypeStruct(buf.shape, F32),
        scratch_shapes=[pltpu.VMEM((8, rows, LANES), F32), pltpu.SemaphoreType.DMA((7,)),
                        pltpu.SemaphoreType.DMA((7,))],
        )(buf)


def _pack_flat(arrs, dtype, width, row_mult=8):
    flat = jnp.concatenate([a.reshape(-1).astype(dtype) for a in arrs])
    pad = (-flat.shape[0]) % (row_mult * width)
    if pad:
        flat = jnp.concatenate([flat, jnp.zeros((pad,), dtype)])
    return flat.reshape(-1, width)


def _unpack_flat(buf, shapes):
    flat = buf.reshape(-1)
    out, off = [], 0
    for s in shapes:
        n = math.prod(s)
        out.append(flat[off:off + n].reshape(s))
        off += n
    return out


def _in_to_padded(w):
    parts = []
    for name in IN_NEW_ORDER:
        _, width, o_off, o_w = IN_LAYOUT[name]
        parts.append(w[..., o_off:o_off + o_w])
        if o_w < width:
            parts.append(jnp.zeros(w.shape[:-1] + (width - o_w,), w.dtype))
    used = sum(IN_LAYOUT[n][1] for n in IN_NEW_ORDER)
    parts.append(jnp.zeros(w.shape[:-1] + (IN_PAD - used,), w.dtype))
    return jnp.concatenate(parts, axis=-1)


def _in_from_padded(g):
    parts = []
    for name in IN_ORIG_ORDER:
        off, _, _, o_w = IN_LAYOUT[name]
        parts.append(g[..., off:off + o_w])
    return jnp.concatenate(parts, axis=-1)


def _pcol(p, name):
    off, width, _, _ = IN_LAYOUT[name]
    return Cols(p, off, width)


def _lane_pad(v, width=LANES):
    v = v.reshape(-1)
    return jnp.concatenate([v, jnp.zeros((width - v.shape[0],), v.dtype)]).reshape(1, width)


def _f_norm_mod(h, sh, sc, w):
    return (_rms(h, w) * (1.0 + sc) + sh,)


def _f_norm_mod_thru(h, sh, sc, w):
    return h, _rms(h, w) * (1.0 + sc) + sh


def _f_attn_prep(qraw, kraw, vraw, cq, sq, ck, sk, qw, kw, gq, gk):
    q = qraw * lax.rsqrt(_group_mean(qraw * qraw, gq) + NORM_EPS) * qw
    q = _rope32(q, cq, sq) * (ATTN_HD ** -0.5)
    k = kraw * lax.rsqrt(_group_mean(kraw * kraw, gk) + NORM_EPS) * kw
    return q, _rope32(k, ck, sk), vraw


def _f_ssd_finish(yf, yb, xs, z, d_exp, nw):
    y = (yf + yb + d_exp * xs) * (z * jax.nn.sigmoid(z))
    return (_rms(y, nw),)


def _f_ret_prep(rq, rk, cos_full, sin_signed):
    return _rope64(rq, cos_full, sin_signed), _rope64(rk, cos_full, sin_signed) * (RET_DK ** -0.5)


def _f_ret_finish(yf, yb, g, gw):
    y = yf + yb
    outs = []
    for h in range(RET_HEADS):
        yh = y[:, h * RET_DK:(h + 1) * RET_DK]
        yc = yh - jnp.mean(yh, axis=-1, keepdims=True)
        outs.append(yc * lax.rsqrt(jnp.mean(yc * yc, axis=-1, keepdims=True) + NORM_EPS))
    return (jnp.concatenate(outs, axis=1) * gw * (g * jax.nn.sigmoid(g)),)


def _f_merge(p0, p1, p2, g0, g1, g2):
    return (jax.nn.sigmoid(g0) * p0 + jax.nn.sigmoid(g1) * p1 + jax.nn.sigmoid(g2) * p2,)


def _f_mid(h, mix, g1, sh2, sc2, w2):
    h_mid = h + g1 * mix
    return h_mid, _rms(h_mid, w2) * (1.0 + sc2) + sh2


def _f_sqrelu(a):
    r = jnp.maximum(a, 0.0)
    return (r * r,)


def _f_residual(h_mid, o, g2):
    return (h_mid + g2 * o,)


def _f_silu(x):
    return (x * jax.nn.sigmoid(x),)


def _f_bias(x, b):
    return (x + b,)


def _ssd_rows(xbc, p):
    rows = [Cols(xbc, LANES * k, LANES) for k in range(4)]
    rows += [Cols(xbc, 512 + LANES * g, LANES) for g in range(2)]
    rows += [Cols(xbc, 768 + LANES * g, LANES) for g in range(2)]
    return rows + [_pcol(p, "dt")]


def _ret_rows(rq, rk, p):
    off_v = IN_LAYOUT["rv"][0]
    return ([Cols(rq, LANES * h, LANES) for h in range(4)] + [Cols(rk, LANES * h, LANES) for h in range(4)]
            + [Cols(p, off_v + LANES * h, LANES) for h in range(4)])


def layer_fwd(li, h, mod, lw, tabs, m_ctx):
    t = h.shape[0]
    nb = m_ctx // min(ROW_TILE, t)
    sh1, sc1, g1, sh2, sc2, g2 = mod
    nm = lambda s: f"l{li}_{s}"
    sv = {}
    (u,) = rowwise_fwd(nm("norm1"), _f_norm_mod, [h], [sh1, sc1], [lw["norm1_w"]], [(D_MODEL, CDT)], t, nb)
    p = mm(nm("in_proj"), u, lw["w_in"], F32)
    q, k, v = rowwise_fwd(
        nm("attn_prep"), _f_attn_prep,
        [_pcol(p, "q"), _pcol(p, "k"), _pcol(p, "v"), tabs["cq"], tabs["sq"], tabs["ck"], tabs["sk"]], [],
        [lw["qw"], lw["kw"], tabs["gq"], tabs["gk"]], [(512, CDT), (128, CDT), (128, CDT)], t, nb)
    tq = min(ATTN_TQ, m_ctx)
    kk, vv = _split_kv(k), _split_kv(v)
    attn_o, qT, oT, lse = attn_fwd(nm("attn"), q, kk, vv.transpose(0, 2, 1), m_ctx, tq)

    xbc = conv_fwd(_pcol(p, "xbc"), lw["conv_w"], lw["conv_b"], m_ctx)
    ssd_sh = [lw["dt_bias"], lw["a_log"]]
    yf, sf = scan_fwd(nm("ssd_f"), _make_ssd_chunk(0), _ssd_rows(xbc, p), ssd_sh, 4, 512, t, m_ctx, False)
    yb, sb = scan_fwd(nm("ssd_b"), _make_ssd_chunk(1), _ssd_rows(xbc, p), ssd_sh, 4, 512, t, m_ctx, True)
    (ssd_o,) = rowwise_fwd(nm("ssd_fin"), _f_ssd_finish, [yf, yb, Cols(xbc, 0, 512), _pcol(p, "z")], [],
                           [lw["d_exp"], lw["ssd_nw"]], [(512, CDT)], t, nb)

    rq, rk = rowwise_fwd(nm("ret_prep"), _f_ret_prep, [_pcol(p, "rq"), _pcol(p, "rk"), tabs["rc"], tabs["rs"]],
                         [], [], [(512, F32), (512, F32)], t, nb)
    rf, rsf = scan_fwd(nm("ret_f"), _make_ret_chunk(0), _ret_rows(rq, rk, p), [lw["ret_lg"]], 4, 512, t, m_ctx, False)
    rb, rsb = scan_fwd(nm("ret_b"), _make_ret_chunk(1), _ret_rows(rq, rk, p), [lw["ret_lg"]], 4, 512, t, m_ctx, True)
    (ret_o,) = rowwise_fwd(nm("ret_fin"), _f_ret_finish, [rf, rb, _pcol(p, "rg")], [], [lw["ret_gw"]],
                           [(512, CDT)], t, nb)

    pbs = [mm(nm(f"branch{b}"), br, lw["w_branch"][b], CDT) for b, br in enumerate((attn_o, ssd_o, ret_o))]
    gl = [Cols(p, 1024 * b, 1024) for b in range(3)]
    (merged,) = rowwise_fwd(nm("merge"), _f_merge, pbs + gl, [], [], [(D_MODEL, CDT)], t, nb)
    mix = mm(nm("out_proj"), merged, lw["w_out"], F32)
    h_mid, vv2 = rowwise_fwd(nm("mid"), _f_mid, [h, mix], [g1, sh2, sc2], [lw["norm2_w"]],
                             [(D_MODEL, F32), (D_MODEL, CDT)], t, nb)
    a = mm(nm("mlp1"), vv2, lw["w_mlp1"], CDT)
    (hh,) = rowwise_fwd(nm("sqrelu"), _f_sqrelu, [a], [], [], [(a.shape[1], CDT)], t, nb)
    o = mm(nm("mlp2"), hh, lw["w_mlp2"], F32)
    (h_out,) = rowwise_fwd(nm("resid"), _f_residual, [h_mid, o], [g2], [], [(D_MODEL, F32)], t, nb)
    sv.update(h=h, u=u, p=p, qT=qT, kk=kk, vv=vv, oT=oT, lse=lse, attn_o=attn_o, xbc=xbc, yf=yf, yb=yb,
              sf=sf, sb=sb, ssd_o=ssd_o, rq=rq, rk=rk, rf=rf, rb=rb, rsf=rsf, rsb=rsb, ret_o=ret_o, pbs=pbs,
              merged=merged, mix=mix, h_mid=h_mid, v=vv2, a=a, hh=hh, o=o)
    return h_out, sv


def layer_bwd(li, dh_out, sv, mod, lw, tabs, m_ctx):
    t = dh_out.shape[0]
    nb = m_ctx // min(ROW_TILE, t)
    sh1, sc1, g1, sh2, sc2, g2 = mod
    nm = lambda s: f"l{li}_{s}_bwd"
    gw = {}
    p = sv["p"]
    (do,), (dg2,), _ = rowwise_bwd(nm("resid"), _f_residual, [sv["h_mid"], sv["o"]], [g2], [], [dh_out],
                                   [False, True], [], [CDT], t, nb)
    dhh = mm(nm("mlp2_dx"), do, lw["w_mlp2"], CDT, transpose_b=True)
    gw["w_mlp2"] = mm_tn(nm("mlp2_dw"), sv["hh"], do, CDT, ("rows", lw["w_mlp2"].shape[0] // 4))
    (da,), _, _ = rowwise_bwd(nm("sqrelu"), _f_sqrelu, [sv["a"]], [], [], [dhh], [True], [], [CDT], t, nb)
    dv = mm(nm("mlp1_dx"), da, lw["w_mlp1"], F32, transpose_b=True)
    gw["w_mlp1"] = mm_tn(nm("mlp1_dw"), sv["v"], da, CDT, ("cols", lw["w_mlp1"].shape[1] // 4))
    (dh_a, dmix), (dg1, dsh2, dsc2), (gw["norm2_w"],) = rowwise_bwd(
        nm("mid"), _f_mid, [sv["h"], sv["mix"]], [g1, sh2, sc2], [lw["norm2_w"]], [dh_out, dv],
        [True, True], [True], [F32, CDT], t, nb)
    dmerged = mm(nm("out_dx"), dmix, lw["w_out"], CDT, transpose_b=True)
    gw["w_out"] = mm_tn(nm("out_dw"), sv["merged"], dmix, CDT, ("rows", lw["w_out"].shape[0] // 4))
    gl = [Cols(p, 1024 * b, 1024) for b in range(3)]
    dmg, _, _ = rowwise_bwd(nm("merge"), _f_merge, sv["pbs"] + gl, [], [], [dmerged], [True] * 6, [], [CDT] * 6,
                            t, nb)
    dpb, dgl = dmg[:3], dmg[3:]
    brs = (sv["attn_o"], sv["ssd_o"], sv["ret_o"])
    d_attn_o = mm(nm("branch0_dx"), dpb[0], lw["w_branch"][0], CDT, transpose_b=True)
    d_ssd_o = mm(nm("branch1_dx"), dpb[1], lw["w_branch"][1], F32, transpose_b=True)
    d_ret_o = mm(nm("branch2_dx"), dpb[2], lw["w_branch"][2], F32, transpose_b=True)
    n_loc = lw["w_branch"].shape[2] // 4
    gw["w_branch"] = jnp.stack([mm_tn(nm(f"branch{b}_dw"), brs[b], dpb[b], CDT, ("cols", n_loc)) for b in range(3)],
                               axis=1).reshape(4, -1, n_loc)
    tq = min(ATTN_TQ, m_ctx)
    dq_t, dk_s, dv_s = attn_bwd(nm("attn"), sv["qT"], d_attn_o, sv["oT"], sv["lse"], sv["kk"],
                                sv["kk"].transpose(0, 2, 1), sv["vv"], m_ctx)
    (dq_raw, dk_raw, dv_raw), _, (gw["qw"], gw["kw"]) = rowwise_bwd(
        nm("attn_prep"), _f_attn_prep,
        [_pcol(p, "q"), _pcol(p, "k"), _pcol(p, "v"), tabs["cq"], tabs["sq"], tabs["ck"], tabs["sk"]], [],
        [lw["qw"], lw["kw"], tabs["gq"], tabs["gk"]],
        [from_heads_t(nm("dq_rows"), dq_t, F32), _merge_kv(dk_s), _merge_kv(dv_s)],
        [True, True, True, False, False, False, False], [True, True, False, False], [CDT] * 3, t, nb)
    (dy_ssd, dxs_fin, dz), _, (gw["d_exp"], gw["ssd_nw"]) = rowwise_bwd(
        nm("ssd_fin"), _f_ssd_finish, [sv["yf"], sv["yb"], Cols(sv["xbc"], 0, 512), _pcol(p, "z")], [],
        [lw["d_exp"], lw["ssd_nw"]], [d_ssd_o], [True, False, True, True], [True, True], [F32, F32, CDT], t, nb)
    ssd_sh = [lw["dt_bias"], lw["a_log"]]
    post_ssd = lambda d: [jnp.concatenate(d[0:8], axis=1), d[8]]
    (dxbc_f, ddt_f), dsh_f = scan_bwd(nm("ssd_f"), _make_ssd_chunk(0), _ssd_rows(sv["xbc"], p), ssd_sh, sv["sf"],
                                      dy_ssd, post_ssd, [(1024, F32), (LANES, F32)], [], 4, t, m_ctx, False)
    (dxbc, ddt), dsh_b = scan_bwd(nm("ssd_b"), _make_ssd_chunk(1), _ssd_rows(sv["xbc"], p), ssd_sh, sv["sb"],
                                  dy_ssd, post_ssd, [(1024, F32), (LANES, CDT)], [dxbc_f, ddt_f], 4, t, m_ctx, True)
    gw["dt_bias"] = dsh_f[0] + dsh_b[0]
    gw["a_log"] = dsh_f[1] + dsh_b[1]
    dxbc_raw, gw["conv_w"], gw["conv_b"] = conv_bwd(_pcol(p, "xbc"), lw["conv_w"], lw["conv_b"], dxbc, dxs_fin, m_ctx)
    (dy_ret, drg), _, (gw["ret_gw"],) = rowwise_bwd(
        nm("ret_fin"), _f_ret_finish, [sv["rf"], sv["rb"], _pcol(p, "rg")], [], [lw["ret_gw"]], [d_ret_o],
        [True, False, True], [True], [F32, CDT], t, nb)
    post_ret = lambda d: [jnp.concatenate(d[0:4], axis=1), jnp.concatenate(d[4:8], axis=1),
                          jnp.concatenate(d[8:12], axis=1)]
    rrows = _ret_rows(sv["rq"], sv["rk"], p)
    r3 = [(512, F32)] * 3
    part, dlg_f = scan_bwd(nm("ret_f"), _make_ret_chunk(0), rrows, [lw["ret_lg"]], sv["rsf"], dy_ret, post_ret, r3,
                           [], 4, t, m_ctx, False)
    (drq_r, drk_r, drv), dlg_b = scan_bwd(nm("ret_b"), _make_ret_chunk(1), rrows, [lw["ret_lg"]], sv["rsb"], dy_ret,
                                          post_ret, [(512, F32), (512, F32), (512, CDT)], part, 4, t, m_ctx, True)
    gw["ret_lg"] = dlg_f[0] + dlg_b[0]
    (drq, drk), _, _ = rowwise_bwd(nm("ret_prep"), _f_ret_prep,
                                   [_pcol(p, "rq"), _pcol(p, "rk"), tabs["rc"], tabs["rs"]], [], [], [drq_r, drk_r],
                                   [True, True, False, False], [], [CDT, CDT], t, nb)
    pieces = {"gates": None, "xbc": dxbc_raw, "q": dq_raw, "z": dz, "rq": drq, "rk": drk, "rv": drv, "rg": drg,
              "k": dk_raw, "v": dv_raw, "dt": ddt}
    cols = list(dgl) + [pieces[n] for n in IN_NEW_ORDER[1:]]
    used = sum(c.shape[1] for c in cols)
    cols.append(jnp.zeros((t, IN_PAD - used), CDT))
    dp = jnp.concatenate(cols, axis=1)
    du = mm(nm("in_dx"), dp, lw["w_in"], F32, transpose_b=True)
    gw["w_in"] = mm_tn(nm("in_dw"), sv["u"], dp, CDT)
    (dh_in,), (dsh1, dsc1), (gw["norm1_w"],) = rowwise_bwd(
        nm("norm1"), _f_norm_mod_thru, [sv["h"]], [sh1, sc1], [lw["norm1_w"]], [dh_a, du], [True], [True], [F32],
        t, nb)
    return dh_in, [dsh1, dsc1, dg1, dsh2, dsc2, dg2], gw


def _rope_tables(n_lat, m_ctx):
    rows = n_lat // GRID_W
    row = jnp.repeat(jnp.arange(rows, dtype=F32), GRID_W)
    col = jnp.tile(jnp.arange(GRID_W, dtype=F32), rows)
    nfreq = ATTN_HD // 4
    inv = ROPE_THETA ** (-jnp.arange(nfreq, dtype=F32) / nfreq)
    ang = jnp.concatenate([row[:, None] * inv, col[:, None] * inv], axis=-1)
    cos = jnp.concatenate([jnp.ones((m_ctx, ATTN_HD // 2), F32), jnp.cos(ang)], axis=0)
    sin = jnp.concatenate([jnp.zeros((m_ctx, ATTN_HD // 2), F32), jnp.sin(ang)], axis=0)
    c64 = jnp.concatenate([cos, cos], axis=1)
    s64 = jnp.concatenate([-sin, sin], axis=1)
    pos = jnp.arange(m_ctx + n_lat, dtype=F32)
    inv_r = ROPE_THETA ** (-jnp.linspace(0.0, 1.0, RET_DK // 2, dtype=F32))
    ang_r = pos[:, None] * inv_r
    rc = jnp.concatenate([jnp.cos(ang_r)] * 2, axis=1)
    rs = jnp.concatenate([-jnp.sin(ang_r), jnp.sin(ang_r)], axis=1)
    return dict(cq=jnp.tile(c64, (1, 8)), sq=jnp.tile(s64, (1, 8)), ck=jnp.tile(c64, (1, 2)), sk=jnp.tile(s64, (1, 2)),
                rc=jnp.tile(rc, (1, 4)), rs=jnp.tile(rs, (1, 4)), gq=_group_matrix(512, ATTN_HD),
                gk=_group_matrix(128, ATTN_HD))


def _layer_weights(full, small, layer):
    return dict(
        w_in=full["w_in"][layer], w_branch=full["w_branch"][layer], w_out=full["w_out"][layer],
        w_mlp1=full["w_mlp1"][layer], w_mlp2=full["w_mlp2"][layer],
        norm1_w=small["norm1_w"][layer][None], norm2_w=small["norm2_w"][layer][None],
        qw=jnp.tile(small["attn_q_norm"][layer], 8)[None], kw=jnp.tile(small["attn_k_norm"][layer], 2)[None],
        conv_w=small["ssd_conv_w"][layer], conv_b=small["ssd_conv_b"][layer][None],
        dt_bias=_lane_pad(small["ssd_dt_bias"][layer]), a_log=_lane_pad(small["ssd_a_log"][layer]),
        d_exp=jnp.repeat(small["ssd_d"][layer], SSD_HD)[None], ssd_nw=small["ssd_norm_w"][layer][None],
        ret_lg=_lane_pad(small["ret_log_decay"][layer]), ret_gw=small["ret_gn_w"][layer][None])


def local_step(x, c, ctx, full, small, loss_target):
    n_lat, d = x.shape
    m_ctx = ctx.shape[0]
    t = n_lat + m_ctx
    depth = small["norm1_w"].shape[0]
    tabs = _rope_tables(n_lat, m_ctx)
    h = jnp.concatenate([ctx, x], axis=0)
    cc = jnp.concatenate([small["c_ctx"][None], c, jnp.zeros((COND_ROWS - 2, d), F32)], axis=0)
    (scc,) = rowwise_fwd("cond_silu", _f_silu, [cc], [], [], [(d, CDT)], COND_ROWS, 0)
    mods, saved, lws = [], [], []
    for layer in range(depth):
        lw = _layer_weights(full, small, layer)
        mod_raw = mm(f"l{layer}_mod", scc, full["w_mod"][layer], F32)
        (mod8,) = rowwise_fwd(f"l{layer}_mod_bias", _f_bias, [mod_raw], [], [small["b_mod"][layer][None]],
                              [(6 * d, F32)], COND_ROWS, 0)
        mod = [mod8[0:2, k * d:(k + 1) * d].reshape(2, 1, d) for k in range(6)]
        h, sv = layer_fwd(layer, h, mod, lw, tabs, m_ctx)
        mods.append(mod)
        saved.append(sv)
        lws.append(lw)
    loss, dh, d_final = loss_head(h, loss_target, small["final_norm_w"][None], m_ctx)

    gbig = {k: [None] * depth for k in BIG}
    gs = {k: [None] * depth for k in SMALL if k not in ("c_ctx", "final_norm_w")}
    d_scc = None
    for layer in reversed(range(depth)):
        lw = lws[layer]
        dh, dmod, gw = layer_bwd(layer, dh, saved[layer], mods[layer], lw, tabs, m_ctx)
        dmod8 = jnp.concatenate([jnp.concatenate([g_.reshape(2, d) for g_ in dmod], axis=1),
                                 jnp.zeros((COND_ROWS - 2, 6 * d), F32)], axis=0)
        (dmod_c,), _, (db_mod,) = rowwise_bwd(f"l{layer}_mod_bias_bwd", _f_bias, [dmod8], [],
                                              [small["b_mod"][layer][None]], [dmod8], [True], [True], [CDT], COND_ROWS, 0)
        gbig["w_mod"][layer] = mm_tn(f"l{layer}_mod_dw", scc, dmod_c, CDT, ("cols", 6 * d // 4))
        part = mm(f"l{layer}_mod_dx", dmod_c, full["w_mod"][layer], F32, transpose_b=True)
        d_scc = part if d_scc is None else d_scc + part
        g_in = _in_from_padded(gw["w_in"])
        gbig["w_in"][layer] = g_in.reshape(d, 4, g_in.shape[1] // 4).transpose(1, 0, 2)
        for k in ("w_branch", "w_out", "w_mlp1", "w_mlp2"):
            gbig[k][layer] = gw[k]
        gs["b_mod"][layer] = db_mod.reshape(-1)
        gs["norm1_w"][layer] = gw["norm1_w"].reshape(-1)
        gs["norm2_w"][layer] = gw["norm2_w"].reshape(-1)
        gs["attn_q_norm"][layer] = gw["qw"].reshape(8, ATTN_HD).sum(0)
        gs["attn_k_norm"][layer] = gw["kw"].reshape(2, ATTN_HD).sum(0)
        gs["ssd_conv_w"][layer] = gw["conv_w"]
        gs["ssd_conv_b"][layer] = gw["conv_b"].reshape(-1)
        gs["ssd_dt_bias"][layer] = gw["dt_bias"][0, :16].reshape(2, 8)
        gs["ssd_a_log"][layer] = gw["a_log"][0, :16].reshape(2, 8)
        gs["ssd_d"][layer] = gw["d_exp"].reshape(SSD_HEADS, SSD_HD).sum(1)
        gs["ssd_norm_w"][layer] = gw["ssd_nw"].reshape(-1)
        gs["ret_log_decay"][layer] = gw["ret_lg"][0, :8].reshape(2, 4)
        gs["ret_gn_w"][layer] = gw["ret_gw"].reshape(-1)
    (d_cc,), _, _ = rowwise_bwd("cond_silu_bwd", _f_silu, [cc], [], [], [d_scc], [True], [], [F32], COND_ROWS, 0)
    g_small = {k: jnp.stack(v) for k, v in gs.items()}
    g_small["c_ctx"] = d_cc[0]
    g_small["final_norm_w"] = d_final.reshape(-1)
    g_big = {k: jnp.stack(v) for k, v in gbig.items()}
    return loss, dh[m_ctx:], g_big, g_small


def kernel(x, c, ctx, c_ctx, w_mod, b_mod, norm1_w, norm2_w, w_in, attn_q_norm, attn_k_norm, ssd_conv_w, ssd_conv_b, ssd_dt_bias, ssd_a_log, ssd_d, ssd_norm_w, ret_log_decay, ret_gn_w, w_branch, w_out, w_mlp1, w_mlp2, final_norm_w, loss_target, m_c_ctx, m_w_mod, m_b_mod, m_norm1_w, m_norm2_w, m_w_in, m_attn_q_norm, m_attn_k_norm, m_ssd_conv_w, m_ssd_conv_b, m_ssd_dt_bias, m_ssd_a_log, m_ssd_d, m_ssd_norm_w, m_ret_log_decay, m_ret_gn_w, m_w_branch, m_w_out, m_w_mlp1, m_w_mlp2, m_final_norm_w, v_c_ctx, v_w_mod, v_b_mod, v_norm1_w, v_norm2_w, v_w_in, v_attn_q_norm, v_attn_k_norm, v_ssd_conv_w, v_ssd_conv_b, v_ssd_dt_bias, v_ssd_a_log, v_ssd_d, v_ssd_norm_w, v_ret_log_decay, v_ret_gn_w, v_w_branch, v_w_out, v_w_mlp1, v_w_mlp2, v_final_norm_w):
    env = dict(locals())
    w_loc = {k: env[k] for k in WEIGHTS}
    m_loc = {k: env["m_" + k] for k in WEIGHTS}
    v_loc = {k: env["v_" + k] for k in WEIGHTS}
    chip = 2 * lax.axis_index("x") + lax.axis_index("y")
    core = lax.axis_index("c")

    depth = w_loc["w_mod"].shape[0]
    assert depth == 2, "the exchanges split the layers between a chip's two cores"
    shards = [w_loc[k].astype(CDT).reshape(depth, -1, w_loc[k].shape[-1]) for k in BIG]
    mine = gather_layers("gather_weights", shards, [BIG_KIND[k] for k in BIG])
    full = {}
    for k, arr in zip(BIG, mine):
        both = exchange_both("share_" + k, arr.reshape(-1, arr.shape[-1]))
        if k == "w_in":
            both = both.reshape(depth, 4, -1, both.shape[-1]).transpose(0, 2, 1, 3)
            both = _in_to_padded(both.reshape(depth, both.shape[1], -1))
        full[k] = both.reshape((depth,) + w_loc[k].shape[1:-1] + (-1,)) if BIG_KIND[k] == "cols" else \
            both.reshape((depth,) + w_loc[k].shape[1:-2] + (-1, w_loc[k].shape[-1])) if BIG_KIND[k] == "rows" else both

    cw = w_loc["ssd_conv_w"]
    cw_w = cw.shape[-1]
    placed = lax.dynamic_update_slice(jnp.zeros(cw.shape[:-1] + (4 * cw_w,), F32),
                                      cw * (core == 0).astype(F32), (0, 0, chip * cw_w))
    conv_full = _unpack_flat(allreduce_small("gather_conv_w", _pack_flat([placed], F32, LANES)), [placed.shape])[0]
    small = {k: w_loc[k] for k in SMALL}
    small["ssd_conv_w"] = conv_full

    loss_l, grad_x, g_big, g_small = local_step(x[0], c, ctx[0], full, small, loss_target[0])

    small_shapes = [g_small[k].shape for k in SMALL] + [(LANES,)]
    summed = _unpack_flat(allreduce_small("reduce_small", _pack_flat([g_small[k] for k in SMALL] + [loss_l], F32, LANES)),
                          small_shapes)
    gsum = dict(zip(SMALL, summed[:-1]))
    loss = summed[-1][0]
    gsum["ssd_conv_w"] = lax.dynamic_slice(gsum["ssd_conv_w"], (0, 0, chip * cw_w), cw.shape)

    pair = []
    for k in BIG:
        _, _, rows, cols = g_big[k].shape
        pair.append(exchange_add("pair_" + k, g_big[k].reshape(depth, 4 * rows, cols)).reshape(4, rows, cols))
    landed = scatter_pieces("scatter_grads", pair)
    g_sum = [sum_exchange("sum_" + k, parts) for k, parts in zip(BIG, landed)]

    grads, deltas, new_m, new_v = {}, {}, {}, {}
    for i, k in enumerate(BIG):
        shp = w_loc[k].shape
        two_d = lambda a, shp=shp: a.reshape(-1, shp[-1])
        res = adamw("adamw_" + k, two_d(w_loc[k]), two_d(m_loc[k]), two_d(v_loc[k]), [two_d(g_sum[i])])
        grads[k], deltas[k], new_m[k], new_v[k] = [r.reshape(shp) for r in res]
    small_loc_shapes = [w_loc[k].shape for k in SMALL]
    res = adamw("adamw_small", _pack_flat([w_loc[k] for k in SMALL], F32, LANES),
                _pack_flat([m_loc[k] for k in SMALL], F32, LANES), _pack_flat([v_loc[k] for k in SMALL], F32, LANES),
                [_pack_flat([gsum[k] for k in SMALL], F32, LANES)])
    for dst, r in zip((grads, deltas, new_m, new_v), res):
        dst.update(dict(zip(SMALL, _unpack_flat(r, small_loc_shapes))))

    return (loss, grad_x[None], *[grads[k] for k in WEIGHTS], *[deltas[k] for k in WEIGHTS],
            *[new_m[k] for k in WEIGHTS], *[new_v[k] for k in WEIGHTS])
```

```python
import functools
import math
from typing import NamedTuple

import jax
import jax.numpy as jnp
from jax import lax
from jax.experimental import pallas as pl
from jax.experimental.pallas import tpu as pltpu

F32 = jnp.float32
CDT = jnp.bfloat16
NORM_EPS = 1e-6
ROPE_THETA = 10000.0
GRID_W = 64
D_MODEL = 1024
ATTN_HEADS, ATTN_KV, ATTN_HD = 8, 2, 64
SSD_HEADS, SSD_HD, SSD_STATE = 8, 64, 128
RET_HEADS, RET_DK = 4, 128
CHUNK = 128
ROW_TILE = 256
MM_ROWS = 768
MM_VMEM_BUDGET = 44 * 1024 * 1024
ATTN_TQ, ATTN_TK = 256, 256
ATTN_TK_FWD = 2048
LANES = 128
PAIR_BLOCK_BYTES = 2 * 1024 * 1024
COND_ROWS = 16
VMEM_LIMIT = 56 * 1024 * 1024

ADAM_LR, ADAM_B1, ADAM_B2, ADAM_EPS, ADAM_WD, ADAM_STEP = 0.001, 0.9, 0.999, 1e-08, 0.01, 10

IN_LAYOUT = {
    "gates": (0, 3072, 4368, 3072), "xbc": (3072, 1024, 1280, 1024), "q": (4096, 512, 0, 512),
    "z": (4608, 512, 768, 512), "rq": (5120, 512, 2320, 512), "rk": (5632, 512, 2832, 512),
    "rv": (6144, 512, 3344, 512), "rg": (6656, 512, 3856, 512), "k": (7168, 128, 512, 128),
    "v": (7296, 128, 640, 128), "dt": (7424, 128, 2304, 16),
}
IN_PAD = 7680
IN_ORIG_ORDER = ("q", "k", "v", "z", "xbc", "dt", "rq", "rk", "rv", "rg", "gates")
IN_NEW_ORDER = ("gates", "xbc", "q", "z", "rq", "rk", "rv", "rg", "k", "v", "dt")

BIG = ("w_mod", "w_in", "w_branch", "w_out", "w_mlp1", "w_mlp2")
BIG_KIND = {"w_mod": "cols", "w_in": "slices", "w_branch": "cols", "w_out": "rows", "w_mlp1": "cols", "w_mlp2": "rows"}
SMALL = ("c_ctx", "b_mod", "norm1_w", "norm2_w", "attn_q_norm", "attn_k_norm", "ssd_conv_w", "ssd_conv_b",
         "ssd_dt_bias", "ssd_a_log", "ssd_d", "ssd_norm_w", "ret_log_decay", "ret_gn_w", "final_norm_w")
WEIGHTS = ("c_ctx", "w_mod", "b_mod", "norm1_w", "norm2_w", "w_in", "attn_q_norm", "attn_k_norm", "ssd_conv_w",
           "ssd_conv_b", "ssd_dt_bias", "ssd_a_log", "ssd_d", "ssd_norm_w", "ret_log_decay", "ret_gn_w",
           "w_branch", "w_out", "w_mlp1", "w_mlp2", "final_norm_w")


def _cp(sem):
    return pltpu.CompilerParams(dimension_semantics=sem, vmem_limit_bytes=VMEM_LIMIT)


class Cols(NamedTuple):
    arr: jax.Array
    off: int
    width: int


def _width(item):
    return item.width if isinstance(item, Cols) else item.shape[1]


def _row_in(item, rows, imap=None):
    imap = imap or (lambda i: i)
    if isinstance(item, Cols):
        assert item.off % item.width == 0
        blk = item.off // item.width
        return item.arr, pl.BlockSpec((rows, item.width), lambda i, blk=blk: (imap(i), blk))
    return item, pl.BlockSpec((rows, item.shape[1]), lambda i: (imap(i), 0))


def _const_spec(shape):
    return pl.BlockSpec(shape, lambda *_: (0,) * len(shape))


def _mxu(a, b, dims=(((1,), (0,)), ((), ()))):
    return lax.dot_general(a.astype(CDT), b.astype(CDT), dims, preferred_element_type=F32)


_NT = (((1,), (1,)), ((), ()))
_TN = (((0,), (0,)), ((), ()))


@jax.custom_vjp
def _softplus(x):
    return jnp.maximum(x, 0.0) + jnp.log1p(jnp.exp(-jnp.abs(x)))


def _softplus_fwd(x):
    return _softplus(x), x


def _softplus_bwd(x, g):
    return (g * jax.nn.sigmoid(x),)


_softplus.defvjp(_softplus_fwd, _softplus_bwd)


def _group_mean_impl(x, gmat):
    hi = x.astype(CDT)
    lo = (x - hi.astype(F32)).astype(CDT)
    return (jnp.dot(hi, gmat, preferred_element_type=F32) + jnp.dot(lo, gmat, preferred_element_type=F32))


@jax.custom_vjp
def _group_mean(x, gmat):
    return _group_mean_impl(x, gmat)


def _group_mean_fwd(x, gmat):
    return _group_mean_impl(x, gmat), gmat


def _group_mean_bwd(gmat, g):
    return _group_mean_impl(g, gmat), jnp.zeros_like(gmat)


_group_mean.defvjp(_group_mean_fwd, _group_mean_bwd)


def _group_matrix(width, group):
    r = jnp.arange(width) // group
    return jnp.where(r[:, None] == r[None, :], 1.0 / group, 0.0).astype(CDT)


def _make_rope(half):
    def partner(x):
        w = x.shape[1]
        lane = lax.broadcasted_iota(jnp.int32, x.shape, 1)
        first = (lane % (2 * half)) < half
        return jnp.where(first, pltpu.roll(x, w - half, axis=1), pltpu.roll(x, half, axis=1))

    def impl(x, cos_full, sin_signed):
        return x * cos_full + partner(x) * sin_signed

    @jax.custom_vjp
    def rope(x, cos_full, sin_signed):
        return impl(x, cos_full, sin_signed)

    def fwd(x, cos_full, sin_signed):
        return impl(x, cos_full, sin_signed), (cos_full, sin_signed)

    def bwd(res, g):
        cos_full, sin_signed = res
        return impl(g, cos_full, -sin_signed), jnp.zeros_like(cos_full), jnp.zeros_like(sin_signed)

    rope.defvjp(fwd, bwd)
    return rope


_rope32 = _make_rope(32)
_rope64 = _make_rope(64)


def _rms(x, w):
    return x * lax.rsqrt(jnp.mean(x * x, axis=-1, keepdims=True) + NORM_EPS) * w


def _col(v, lane_index):
    lane = lax.broadcasted_iota(jnp.int32, v.shape, 1)
    return jnp.sum(jnp.where(lane == lane_index, v, 0.0), axis=1, keepdims=True)


def _typed_spec(width, nb_ctx):
    return pl.BlockSpec((None, 1, width), lambda i: (jnp.where(i >= nb_ctx, 1, 0), 0, 0))


def rowwise_fwd(name, f, rows, typed, shared, outs, n_rows, nb_ctx, tm=ROW_TILE):
    tm = min(tm, n_rows)
    nin = len(rows) + len(typed) + len(shared)

    def body(*refs):
        res = f(*[r[...] for r in refs[:nin]])
        for o_ref, o in zip(refs[nin:], res):
            o_ref[...] = o.astype(o_ref.dtype)

    arrs, specs = [], []
    for it in rows:
        a, s = _row_in(it, tm)
        arrs.append(a)
        specs.append(s)
    for t in typed:
        arrs.append(t)
        specs.append(_typed_spec(t.shape[-1], nb_ctx))
    for s_ in shared:
        arrs.append(s_)
        specs.append(_const_spec(s_.shape))
    res = pl.pallas_call(
        body, name=name, grid=(n_rows // tm,), in_specs=specs,
        out_specs=[pl.BlockSpec((tm, w), lambda i: (i, 0)) for w, _ in outs],
        out_shape=[jax.ShapeDtypeStruct((n_rows, w), dt) for w, dt in outs],
        compiler_params=_cp(("parallel",)))(*arrs)
    return res


def rowwise_bwd(name, f, rows, typed, shared, cots, row_diff, shared_diff, drow_dtypes, n_rows, nb_ctx, tm=ROW_TILE):
    tm = min(tm, n_rows)
    cot_groups = [c_ if isinstance(c_, tuple) else (c_,) for c_ in cots]
    cots = [a for grp in cot_groups for a in grp]
    nr, nt, ns, nc = len(rows), len(typed), len(shared), len(cots)
    nin = nr + nt + ns
    d_rows = [k for k in range(nr) if row_diff[k]]
    d_sh = [k for k in range(ns) if shared_diff[k]]

    def body(*refs):
        rvals = [r[...] for r in refs[:nr]]
        tvals = [r[...] for r in refs[nr:nr + nt]]
        svals = [r[...] for r in refs[nr + nt:nin]]
        cparts = [r[...].astype(F32) for r in refs[nin:nin + nc]]
        cvals = []
        for grp in cot_groups:
            cvals.append(sum(cparts[1:len(grp)], cparts[0]))
            cparts = cparts[len(grp):]
        out_refs = refs[nin + nc:]

        def g(*dv):
            dv = list(dv)
            rv = list(rvals)
            for k in d_rows:
                rv[k] = dv.pop(0)
            tv = [dv.pop(0) for _ in range(nt)]
            sv = list(svals)
            for k in d_sh:
                sv[k] = dv.pop(0)
            return tuple(o.astype(F32) for o in f(*rv, *tv, *sv))

        prim = [rvals[k].astype(F32) for k in d_rows] + tvals + [svals[k] for k in d_sh]
        _, vjp = jax.vjp(g, *prim)
        grads = list(vjp(tuple(cvals)))
        i = pl.program_id(0)
        for ref in out_refs[:len(d_rows)]:
            ref[...] = grads.pop(0).astype(ref.dtype)
        first_typed = (i == 0) | (i == nb_ctx)
        for ref in out_refs[len(d_rows):len(d_rows) + nt]:
            gr = grads.pop(0)

            @pl.when(first_typed)
            def _(ref=ref, gr=gr):
                ref[...] = gr

            @pl.when(jnp.logical_not(first_typed))
            def _(ref=ref, gr=gr):
                ref[...] += gr
        for ref in out_refs[len(d_rows) + nt:]:
            gr = grads.pop(0)

            @pl.when(i == 0)
            def _(ref=ref, gr=gr):
                ref[...] = gr

            @pl.when(i != 0)
            def _(ref=ref, gr=gr):
                ref[...] += gr

    arrs, specs = [], []
    for it in list(rows):
        a, s = _row_in(it, tm)
        arrs.append(a)
        specs.append(s)
    for t in typed:
        arrs.append(t)
        specs.append(_typed_spec(t.shape[-1], nb_ctx))
    for s_ in shared:
        arrs.append(s_)
        specs.append(_const_spec(s_.shape))
    for c_ in cots:
        a, s = _row_in(c_, tm)
        arrs.append(a)
        specs.append(s)
    out_specs, out_shape = [], []
    for k, dt in zip(d_rows, drow_dtypes):
        w = _width(rows[k])
        out_specs.append(pl.BlockSpec((tm, w), lambda i: (i, 0)))
        out_shape.append(jax.ShapeDtypeStruct((n_rows, w), dt))
    for t in typed:
        out_specs.append(_typed_spec(t.shape[-1], nb_ctx))
        out_shape.append(jax.ShapeDtypeStruct(t.shape, F32))
    for k in d_sh:
        out_specs.append(_const_spec(shared[k].shape))
        out_shape.append(jax.ShapeDtypeStruct(shared[k].shape, F32))
    res = pl.pallas_call(body, name=name, grid=(n_rows // tm,), in_specs=specs, out_specs=out_specs,
                         out_shape=out_shape, compiler_params=_cp(("arbitrary",)))(*arrs)
    n1, n2 = len(d_rows), len(d_rows) + nt
    return list(res[:n1]), list(res[n1:n2]), list(res[n2:])


def _pick(n, prefs):
    for p in prefs:
        if n % p == 0:
            return p
    return n


def mm(name, a, b, out_dtype, transpose_b=False):
    n, k = b.shape if transpose_b else b.shape[::-1]
    m = (a.arr if isinstance(a, Cols) else a).shape[0]
    assert _width(a) == k
    tm = _pick(m, (MM_ROWS, 256))
    osz = jnp.dtype(out_dtype).itemsize
    tn = next(c for c in (2560, 2048, 1536, 1024, 512, 256, 128, n)
              if n % c == 0 and 2 * (tm * k * 2 + k * c * 2 + tm * c * osz) <= MM_VMEM_BUDGET or c == n)
    dims = _NT if transpose_b else (((1,), (0,)), ((), ()))

    def body(a_ref, b_ref, o_ref):
        o_ref[...] = lax.dot_general(a_ref[...], b_ref[...], dims, preferred_element_type=F32).astype(o_ref.dtype)

    a_arr, a_spec = _row_in(a, tm)
    a_spec = pl.BlockSpec(a_spec.block_shape, lambda j, i, f=a_spec.index_map: f(i))
    b_spec = pl.BlockSpec((tn, k), lambda j, i: (j, 0)) if transpose_b else pl.BlockSpec((k, tn), lambda j, i: (0, j))
    return pl.pallas_call(
        body, name=name, grid=(n // tn, m // tm), in_specs=[a_spec, b_spec],
        out_specs=pl.BlockSpec((tm, tn), lambda j, i: (i, j)),
        out_shape=jax.ShapeDtypeStruct((m, n), out_dtype),
        compiler_params=_cp(("parallel", "parallel")))(a_arr, b)


def mm_tn(name, a, b, out_dtype=F32, pieces=None):
    t = (a.arr if isinstance(a, Cols) else a).shape[0]
    k, n = _width(a), _width(b)
    tt = _pick(t, (MM_ROWS, 256))
    k_unit = pieces[1] if pieces and pieces[0] == "rows" else k
    n_unit = pieces[1] if pieces and pieces[0] == "cols" else n
    tk = _pick(k_unit, (1024, 512, 256, 128))
    tn = _pick(n_unit, (1280, 1024, 512, 256, 128))
    n_t = t // tt

    def body(a_ref, b_ref, o_ref, acc):
        part = lax.dot_general(a_ref[...], b_ref[...], _TN, preferred_element_type=F32)
        ti = pl.program_id(2)

        @pl.when(ti == 0)
        def _():
            acc[...] = part

        @pl.when(ti != 0)
        def _():
            acc[...] += part

        @pl.when(ti == n_t - 1)
        def _():
            o_ref[...] = acc[...].astype(o_ref.dtype)

    def win(item, width):
        if isinstance(item, Cols):
            assert item.off % width == 0
            return item.arr, item.off // width
        return item, 0

    a_arr, a0 = win(a, tk)
    b_arr, b0 = win(b, tn)
    if pieces is None:
        out_spec = pl.BlockSpec((tk, tn), lambda ki, ni, ti: (ki, ni))
        out_shape = (k, n)
    elif pieces[0] == "cols":
        per = n_unit // tn
        out_spec = pl.BlockSpec((None, tk, tn), lambda ki, ni, ti: (ni // per, ki, ni % per))
        out_shape = (4, k, n_unit)
    else:
        per = k_unit // tk
        out_spec = pl.BlockSpec((None, tk, tn), lambda ki, ni, ti: (ki // per, ki % per, ni))
        out_shape = (4, k_unit, n)
    return pl.pallas_call(
        body, name=name, grid=(k // tk, n // tn, n_t),
        in_specs=[pl.BlockSpec((tt, tk), lambda ki, ni, ti: (ti, a0 + ki)),
                  pl.BlockSpec((tt, tn), lambda ki, ni, ti: (ti, b0 + ni))],
        out_specs=out_spec, out_shape=jax.ShapeDtypeStruct(out_shape, out_dtype),
        scratch_shapes=[pltpu.VMEM((tk, tn), F32)],
        compiler_params=_cp(("parallel", "parallel", "arbitrary")))(a_arr, b_arr)


def _heads_t(rows_blk):
    blk = rows_blk.astype(F32).T
    return jnp.concatenate([blk[hh * ATTN_HD:(hh + 1) * ATTN_HD, :] for hh in range(4)], axis=1)


def _heads_rows(t_blk):
    tq = t_blk.shape[1] // 4
    return jnp.concatenate([t_blk[:, hh * tq:(hh + 1) * tq] for hh in range(4)], axis=0).T


def attn_fwd(name, q, kk, vT, m_ctx, tq):
    t, hd = kk.shape[1], ATTN_HD
    nq, r = t // tq, 4 * tq
    tk = _pick(t - m_ctx, (ATTN_TK_FWD, ATTN_TK))
    nqc, n_lat_tiles = m_ctx // tq, (t - m_ctx) // tk

    def body(q_ref, k_ref, vT_ref, o_ref, qT_ref, oT_ref, lse_ref):
        i = pl.program_id(1)
        q_t = _heads_t(q_ref[...]).astype(CDT)
        qT_ref[...] = q_t

        def tile(off, size, carry):
            mi, li, acc = carry
            sub = min(size, ATTN_TK)
            offs = [off + u * sub for u in range(size // sub)]
            sts = [jnp.dot(k_ref[pl.ds(o, sub), :], q_t, preferred_element_type=F32) for o in offs]
            for o, st in zip(offs, sts):
                mn = jnp.maximum(mi, jnp.max(st, axis=0, keepdims=True))
                pt = jnp.exp(st - mn)
                al = jnp.exp(mi - mn)
                li = al * li + jnp.sum(pt, axis=0, keepdims=True)
                acc = al * acc + jnp.dot(vT_ref[:, pl.ds(o, sub)], pt.astype(CDT), preferred_element_type=F32)
                mi = mn
            return mi, li, acc

        init = (jnp.full((1, r), -1e30, F32), jnp.zeros((1, r), F32), jnp.zeros((hd, r), F32))
        carry = tile(0, m_ctx, init)
        mi, li, acc = lax.fori_loop(
            0, jnp.where(i < nqc, 0, n_lat_tiles),
            lambda j, cr: tile(pl.multiple_of(m_ctx + j * tk, ATTN_TK), tk, cr), carry)
        o_t = acc / li
        oT_ref[...] = o_t.astype(oT_ref.dtype)
        o_ref[...] = _heads_rows(o_t).astype(o_ref.dtype)
        lse_ref[...] = mi + jnp.log(li)

    blk_t = pl.BlockSpec((None, None, hd, r), lambda g, i: (g, i, 0, 0))
    rows = pl.BlockSpec((tq, 4 * hd), lambda g, i: (i, g))
    return pl.pallas_call(
        body, name=name, grid=(2, nq),
        in_specs=[rows, pl.BlockSpec((None, t, hd), lambda g, i: (g, 0, 0)),
                  pl.BlockSpec((None, hd, t), lambda g, i: (g, 0, 0))],
        out_specs=[rows, blk_t, blk_t, pl.BlockSpec((None, None, 1, r), lambda g, i: (g, i, 0, 0))],
        out_shape=[jax.ShapeDtypeStruct((t, 8 * hd), CDT), jax.ShapeDtypeStruct((2, nq, hd, r), CDT),
                   jax.ShapeDtypeStruct((2, nq, hd, r), CDT), jax.ShapeDtypeStruct((2, nq, 1, r), F32)],
        compiler_params=_cp(("parallel", "arbitrary")))(q, kk, vT)


def attn_bwd(name, qT, do, oT, lse, kk, kT, vv, m_ctx):
    _, nq, hd, r = qT.shape
    t = kk.shape[1]
    tq, tk = r // 4, ATTN_TK
    nqc, nkc, nk = m_ctx // tq, m_ctx // tk, t // tk

    def body(qT_ref, do_ref, oT_ref, lse_ref, k_ref, kT_ref, v_ref,
             dq_ref, dk_ref, dv_ref, dk_acc, dv_acc, delta_s, doT_s):
        j, i = pl.program_id(1), pl.program_id(2)

        @pl.when(i == 0)
        def _():
            dk_acc[...] = jnp.zeros_like(dk_acc)
            dv_acc[...] = jnp.zeros_like(dv_acc)

        @pl.when(j == 0)
        def _():
            do_t = _heads_t(do_ref[...])
            doT_s[i] = do_t.astype(CDT)
            delta_s[i] = jnp.sum(do_t * oT_ref[...].astype(F32), axis=0, keepdims=True)

        @pl.when((j < nkc) | (i >= nqc))
        def _():
            do_t = doT_s[i]
            st = jnp.dot(k_ref[...], qT_ref[...], preferred_element_type=F32)
            pt = jnp.exp(st - lse_ref[...])
            dv_acc[...] += lax.dot_general(pt.astype(CDT), do_t, _NT, preferred_element_type=F32)
            dpt = jnp.dot(v_ref[...], do_t, preferred_element_type=F32)
            dst = (pt * (dpt - delta_s[i])).astype(CDT)
            dk_acc[...] += lax.dot_general(dst, qT_ref[...], _NT, preferred_element_type=F32)
            part = jnp.dot(kT_ref[...], dst, preferred_element_type=F32)

            @pl.when(j == 0)
            def _():
                dq_ref[i] = part

            @pl.when(j != 0)
            def _():
                dq_ref[i] += part

        @pl.when(i == nq - 1)
        def _():
            dk_ref[...] = dk_acc[...]
            dv_ref[...] = dv_acc[...]

    blk_t = pl.BlockSpec((None, None, hd, r), lambda g, j, i: (g, i, 0, 0))
    row = pl.BlockSpec((None, None, 1, r), lambda g, j, i: (g, i, 0, 0))
    kv = pl.BlockSpec((None, tk, hd), lambda g, j, i: (g, j, 0))
    kv_t = pl.BlockSpec((None, hd, tk), lambda g, j, i: (g, 0, j))
    rows = pl.BlockSpec((tq, 4 * hd), lambda g, j, i: (i, g))
    return pl.pallas_call(
        body, name=name, grid=(2, nk, nq),
        in_specs=[blk_t, rows, blk_t, row, kv, kv_t, kv],
        out_specs=[pl.BlockSpec((None, nq, hd, r), lambda g, j, i: (g, 0, 0, 0)), kv, kv],
        out_shape=[jax.ShapeDtypeStruct(qT.shape, F32), jax.ShapeDtypeStruct(kk.shape, F32),
                   jax.ShapeDtypeStruct(kk.shape, F32)],
        scratch_shapes=[pltpu.VMEM((tk, hd), F32), pltpu.VMEM((tk, hd), F32), pltpu.VMEM((nq, 1, r), F32),
                        pltpu.VMEM((nq, hd, r), CDT)],
        compiler_params=_cp(("parallel", "arbitrary", "arbitrary")))(qT, do, oT, lse, kk, kT, vv)


def from_heads_t(name, a_t, out_dtype):
    _, nq, hd, r = a_t.shape

    def body(a_ref, o_ref):
        o_ref[...] = _heads_rows(a_ref[...].astype(F32)).astype(o_ref.dtype)

    tq = r // 4
    return pl.pallas_call(
        body, name=name, grid=(2, nq), in_specs=[pl.BlockSpec((None, None, hd, r), lambda g, i: (g, i, 0, 0))],
        out_specs=pl.BlockSpec((tq, 4 * hd), lambda g, i: (i, g)),
        out_shape=jax.ShapeDtypeStruct((nq * tq, 8 * hd), out_dtype),
        compiler_params=_cp(("parallel", "parallel")))(a_t)


def _split_kv(a):
    return a.reshape(a.shape[0], 2, ATTN_HD).transpose(1, 0, 2)


def _merge_kv(a):
    return a.transpose(1, 0, 2).reshape(a.shape[1], 2 * ATTN_HD)


def _chunk_order(rev, ncc, nct):
    if not rev:
        return lambda s: s
    return lambda s: jnp.where(s < ncc, ncc - 1 - s, nct - 1 - (s - ncc))


def scan_fwd(name, make_fn, rows, shared, n_state, y_width, n_rows, m_ctx):
    nct, ncc = n_rows // CHUNK, m_ctx // CHUNK
    orders = [_chunk_order(rev, ncc, nct) for rev in (False, True)]
    fns = [make_fn(0), make_fn(1)]
    nr, ns = len(rows), len(shared)

    def body(*refs):
        svals = [r[...] for r in refs[2 * nr:2 * nr + ns]]
        y_refs, sin_refs, st = refs[2 * nr + ns:2 * nr + ns + 2], refs[2 * nr + ns + 2:2 * nr + ns + 4], refs[-1]

        @pl.when(pl.program_id(0) == 0)
        def _():
            st[...] = jnp.zeros_like(st)

        for d in range(2):
            rvals = [r[...] for r in refs[d * nr:(d + 1) * nr]]
            prev = [st[d, k] for k in range(n_state)]
            sin_refs[d][...] = st[d]
            y, new = fns[d](rvals, svals, prev)
            y_refs[d][...] = y
            for k in range(n_state):
                st[d, k] = new[k]

    arrs, specs = [], []
    for order in orders:
        for it in rows:
            a, s = _row_in(it, CHUNK, order)
            arrs.append(a)
            specs.append(s)
    for s_ in shared:
        arrs.append(s_)
        specs.append(_const_spec(s_.shape))
    return pl.pallas_call(
        body, name=name, grid=(nct,), in_specs=specs,
        out_specs=[pl.BlockSpec((CHUNK, y_width), lambda s, o=o: (o(s), 0)) for o in orders]
        + [pl.BlockSpec((None, n_state, LANES, LANES), lambda s, o=o: (o(s), 0, 0, 0)) for o in orders],
        out_shape=[jax.ShapeDtypeStruct((n_rows, y_width), F32)] * 2
        + [jax.ShapeDtypeStruct((nct, n_state, LANES, LANES), F32)] * 2,
        scratch_shapes=[pltpu.VMEM((2, n_state, LANES, LANES), F32)],
        compiler_params=_cp(("arbitrary",)))(*arrs)


def scan_bwd(name, make_fn, rows, shared, states_in, dy, post, outs, n_state, n_rows, m_ctx):
    nct, ncc = n_rows // CHUNK, m_ctx // CHUNK
    orders = [(lambda r, f=_chunk_order(rev, ncc, nct): f(nct - 1 - r)) for rev in (False, True)]
    fns = [make_fn(0), make_fn(1)]
    nr, ns, no = len(rows), len(shared), len(outs)
    n_in = 2 * nr + ns

    def body(*refs):
        svals = [r[...] for r in refs[2 * nr:n_in]]
        sin_refs, dy_refs = refs[n_in:n_in + 2], refs[n_in + 2:n_in + 4]
        out_refs = refs[n_in + 4:n_in + 4 + 2 * no]
        dsh_refs = refs[n_in + 4 + 2 * no:-1]
        dst = refs[-1]
        r = pl.program_id(0)

        @pl.when(r == 0)
        def _():
            dst[...] = jnp.zeros_like(dst)

        d_shared = None
        for d in range(2):
            rvals = [x[...] for x in refs[d * nr:(d + 1) * nr]]
            prev = [sin_refs[d][k] for k in range(n_state)]
            _, vjp = jax.vjp(fns[d], rvals, svals, prev)
            d_rows, d_sh, d_prev = vjp((dy_refs[d][...], [dst[d, k] for k in range(n_state)]))
            for ref, val in zip(out_refs[d * no:(d + 1) * no], post(d_rows)):
                ref[...] = val.astype(ref.dtype)
            d_shared = d_sh if d_shared is None else [a + b for a, b in zip(d_shared, d_sh)]
            for k in range(n_state):
                dst[d, k] = d_prev[k]
        for ref, gr in zip(dsh_refs, d_shared):
            @pl.when(r == 0)
            def _(ref=ref, gr=gr):
                ref[...] = gr

            @pl.when(r != 0)
            def _(ref=ref, gr=gr):
                ref[...] += gr

    arrs, specs = [], []
    for order in orders:
        for it in rows:
            a, s = _row_in(it, CHUNK, order)
            arrs.append(a)
            specs.append(s)
    for s_ in shared:
        arrs.append(s_)
        specs.append(_const_spec(s_.shape))
    for sin, order in zip(states_in, orders):
        arrs.append(sin)
        specs.append(pl.BlockSpec((None, n_state, LANES, LANES), lambda r, o=order: (o(r), 0, 0, 0)))
    for order in orders:
        a, s = _row_in(dy, CHUNK, order)
        arrs.append(a)
        specs.append(s)
    out_specs = [pl.BlockSpec((CHUNK, w), lambda r, o=o: (o(r), 0)) for o in orders for w, _ in outs]
    out_shape = [jax.ShapeDtypeStruct((n_rows, w), dt) for _ in orders for w, dt in outs]
    for s_ in shared:
        out_specs.append(_const_spec(s_.shape))
        out_shape.append(jax.ShapeDtypeStruct(s_.shape, F32))
    res = pl.pallas_call(body, name=name, grid=(nct,), in_specs=specs, out_specs=out_specs, out_shape=out_shape,
                         scratch_shapes=[pltpu.VMEM((2, n_state, LANES, LANES), F32)],
                         compiler_params=_cp(("arbitrary",)))(*arrs)
    return list(res[:no]), list(res[no:2 * no]), list(res[2 * no:])


def _make_ssd_chunk(direction):
    rev = direction == 1
    base = 8 * direction

    def fn(rows, shared, prev):
        xs, bms, cms, dtraw = rows[0:4], rows[4:6], rows[6:8], rows[8]
        dt_bias, a_log = shared
        ln = dtraw.shape[0]
        dt_all = _softplus(dtraw + dt_bias)
        a_all = dt_all * (-jnp.exp(a_log))
        r_i = lax.broadcasted_iota(jnp.int32, (ln, ln), 0)
        c_i = lax.broadcasted_iota(jnp.int32, (ln, ln), 1)
        tri = (r_i <= c_i) if rev else (r_i >= c_i)
        a_cum_all = jnp.dot(tri.astype(F32), a_all, precision=lax.Precision.HIGHEST, preferred_element_type=F32)
        a_tot_all = jnp.sum(a_all, axis=0, keepdims=True)
        first = lax.broadcasted_iota(jnp.int32, (ln, LANES), 1) < SSD_HD
        first_row = lax.broadcasted_iota(jnp.int32, (LANES, 1), 0) < SSD_HD

        def lmat(acol):
            a_b = jnp.broadcast_to(acol, (ln, ln))
            seg = a_b - a_b.T
            return jnp.where(tri, jnp.exp(jnp.where(tri, seg, 0.0)), 0.0)

        ys, new = [], []
        for g in range(2):
            bm, cm = bms[g], cms[g]
            cb = _mxu(cm, bm, _NT)
            for jj in range(2):
                pr = 2 * g + jj
                h0, h1 = base + 2 * pr, base + 2 * pr + 1
                ac0, ac1 = _col(a_cum_all, h0), _col(a_cum_all, h1)
                at0, at1 = _col(a_tot_all, h0), _col(a_tot_all, h1)
                dt_pair = jnp.where(first, _col(dt_all, h0), _col(dt_all, h1))
                acum_pair = jnp.where(first, ac0, ac1)
                atot_pair = jnp.where(first[0:1], at0, at1)
                xd = xs[pr] * dt_pair
                st = _mxu(xd * jnp.exp(atot_pair - acum_pair), bm, _TN)
                new.append(prev[pr] * jnp.where(first_row, jnp.exp(at0), jnp.exp(at1)) + st)
                y0 = _mxu(cb * lmat(ac0), xd)
                y1 = _mxu(cb * lmat(ac1), xd)
                y_off = _mxu(cm, prev[pr], _NT) * jnp.exp(acum_pair)
                ys.append(jnp.where(first, y0, y1) + y_off)
        return jnp.concatenate(ys, axis=1), new

    return fn


def _make_ret_chunk(direction):
    rev = direction == 1
    base = 4 * direction

    def fn(rows, shared, prev):
        qs, ks, vs = rows[0:4], rows[4:8], rows[8:12]
        lg_all = -jnp.exp(shared[0])
        ln = qs[0].shape[0]
        pos = lax.broadcasted_iota(jnp.int32, (ln, 1), 0).astype(F32)
        r_i = lax.broadcasted_iota(jnp.int32, (ln, ln), 0)
        c_i = lax.broadcasted_iota(jnp.int32, (ln, ln), 1)
        diff = ((c_i - r_i) if rev else (r_i - c_i))
        mask = diff >= 0
        dpos = jnp.maximum(diff, 0).astype(F32)
        k_pow = pos if rev else (ln - 1.0 - pos)
        q_pow = (ln - pos) if rev else (pos + 1.0)
        ys, new = [], []
        for h in range(RET_HEADS):
            lg = _col(lg_all, base + h)
            dmat = jnp.where(mask, jnp.exp(dpos * lg), 0.0)
            st = _mxu(ks[h] * jnp.exp(k_pow * lg), vs[h], _TN)
            new.append(prev[h] * jnp.exp(ln * lg) + st)
            s = _mxu(qs[h], ks[h], _NT) * dmat
            ys.append(_mxu(s, vs[h]) + _mxu(qs[h], prev[h]) * jnp.exp(q_pow * lg))
        return jnp.concatenate(ys, axis=1), new

    return fn


def _conv_pre(x, w, b, t_idx, n_rows, m_ctx):
    is_start = (t_idx == 0) | (t_idx == m_ctx)
    is_end = (t_idx == m_ctx - 1) | (t_idx == n_rows - 1)
    xp = jnp.where(is_start, 0.0, pltpu.roll(x, 1, axis=0))
    xn = jnp.where(is_end, 0.0, pltpu.roll(x, n_rows - 1, axis=0))
    return w[0:1] * xp + w[1:2] * x + w[2:3] * xn + b, xp, xn, is_start, is_end


def conv_fwd(x, conv_w, conv_b, m_ctx):
    n_rows, width = x.arr.shape[0], x.width
    c0 = x.off // LANES

    def body(x_ref, w_ref, b_ref, o_ref):
        t_idx = lax.broadcasted_iota(jnp.int32, (n_rows, 1), 0)
        pre = _conv_pre(x_ref[...], w_ref[...], b_ref[...], t_idx, n_rows, m_ctx)[0]
        o_ref[...] = pre * jax.nn.sigmoid(pre)

    return pl.pallas_call(
        body, name="conv_fwd", grid=(width // LANES,),
        in_specs=[pl.BlockSpec((n_rows, LANES), lambda c: (0, c0 + c)),
                  pl.BlockSpec((3, LANES), lambda c: (0, c)), pl.BlockSpec((1, LANES), lambda c: (0, c))],
        out_specs=pl.BlockSpec((n_rows, LANES), lambda c: (0, c)),
        out_shape=jax.ShapeDtypeStruct((n_rows, width), F32),
        compiler_params=_cp(("parallel",)))(x.arr, conv_w, conv_b)


def conv_bwd(x, conv_w, conv_b, dy_a, dy_b, dxs_extra, m_ctx):
    n_rows, width = x.arr.shape[0], x.width
    c0 = x.off // LANES
    n_extra = dxs_extra.shape[1] // LANES

    def body(x_ref, w_ref, b_ref, dya_ref, dyb_ref, ex_ref, dx_ref, dw_ref, db_ref):
        c = pl.program_id(0)
        t_idx = lax.broadcasted_iota(jnp.int32, (n_rows, 1), 0)
        w = w_ref[...]
        pre, xp, xn, is_start, is_end = _conv_pre(x_ref[...], w, b_ref[...], t_idx, n_rows, m_ctx)
        sg = jax.nn.sigmoid(pre)
        dyv = dya_ref[...] + dyb_ref[...] + jnp.where(c < n_extra, ex_ref[...], 0.0)
        dpre = dyv * (sg * (1.0 + pre * (1.0 - sg)))
        d_next = jnp.where(is_end, 0.0, pltpu.roll(dpre, n_rows - 1, axis=0))
        d_prev = jnp.where(is_start, 0.0, pltpu.roll(dpre, 1, axis=0))
        dx_ref[...] = (w[1:2] * dpre + w[0:1] * d_next + w[2:3] * d_prev).astype(dx_ref.dtype)
        dw_ref[...] = jnp.concatenate([jnp.sum(dpre * xp, axis=0, keepdims=True),
                                       jnp.sum(dpre * x_ref[...], axis=0, keepdims=True),
                                       jnp.sum(dpre * xn, axis=0, keepdims=True)], axis=0)
        db_ref[...] = jnp.sum(dpre, axis=0, keepdims=True)

    return pl.pallas_call(
        body, name="conv_bwd", grid=(width // LANES,),
        in_specs=[pl.BlockSpec((n_rows, LANES), lambda c: (0, c0 + c)),
                  pl.BlockSpec((3, LANES), lambda c: (0, c)), pl.BlockSpec((1, LANES), lambda c: (0, c)),
                  pl.BlockSpec((n_rows, LANES), lambda c: (0, c)), pl.BlockSpec((n_rows, LANES), lambda c: (0, c)),
                  pl.BlockSpec((n_rows, LANES), lambda c: (0, jnp.minimum(c, n_extra - 1)))],
        out_specs=[pl.BlockSpec((n_rows, LANES), lambda c: (0, c)),
                   pl.BlockSpec((3, LANES), lambda c: (0, c)), pl.BlockSpec((1, LANES), lambda c: (0, c))],
        out_shape=[jax.ShapeDtypeStruct((n_rows, width), CDT), jax.ShapeDtypeStruct((3, width), F32),
                   jax.ShapeDtypeStruct((1, width), F32)],
        compiler_params=_cp(("parallel",)))(x.arr, conv_w, conv_b, dy_a, dy_b, dxs_extra)


def loss_head(h, target, final_w, m_ctx):
    n_rows, d = h.shape
    tm = min(ROW_TILE, n_rows)
    nb_ctx = m_ctx // tm

    def f(hb, w, tgt):
        err = _rms(hb, w) - tgt
        return 0.5 * jnp.sum(jnp.mean(err * err, axis=-1))

    def body(h_ref, t_ref, w_ref, loss_ref, dh_ref, dw_ref):
        i = pl.program_id(0)

        @pl.when(i < nb_ctx)
        def _():
            dh_ref[...] = jnp.zeros_like(dh_ref)

        @pl.when(i == 0)
        def _():
            loss_ref[...] = jnp.zeros_like(loss_ref)
            dw_ref[...] = jnp.zeros_like(dw_ref)

        @pl.when(i >= nb_ctx)
        def _():
            val, vjp = jax.vjp(lambda hb, w: f(hb, w, t_ref[...]), h_ref[...], w_ref[...])
            dh, dw = vjp(jnp.ones((), F32))
            dh_ref[...] = dh
            dw_ref[...] += dw
            loss_ref[...] += jnp.broadcast_to(val, loss_ref.shape)

    return pl.pallas_call(
        body, name="loss_head", grid=(n_rows // tm,),
        in_specs=[pl.BlockSpec((tm, d), lambda i: (i, 0)),
                  pl.BlockSpec((tm, d), lambda i: (jnp.maximum(i - nb_ctx, 0), 0)), _const_spec((1, d))],
        out_specs=[_const_spec((1, LANES)), pl.BlockSpec((tm, d), lambda i: (i, 0)), _const_spec((1, d))],
        out_shape=[jax.ShapeDtypeStruct((1, LANES), F32), jax.ShapeDtypeStruct((n_rows, d), F32),
                   jax.ShapeDtypeStruct((1, d), F32)],
        compiler_params=_cp(("arbitrary",)))(h, target, final_w)


def adamw(name, w, m, v, g_parts):
    rows, cols = w.shape
    tr = _pick(rows, (256, 128, 64, 32, 16, 8))
    npart = len(g_parts)
    c1 = 1.0 - ADAM_B1 ** ADAM_STEP
    c2 = 1.0 - ADAM_B2 ** ADAM_STEP

    def body(*refs):
        w_ref, m_ref, v_ref = refs[:3]
        g = refs[3][...].astype(F32)
        for r in refs[4:3 + npart]:
            g = g + r[...].astype(F32)
        g_ref, d_ref, nm_ref, nv_ref = refs[3 + npart:]
        nm = ADAM_B1 * m_ref[...] + (1.0 - ADAM_B1) * g
        nv = ADAM_B2 * v_ref[...] + (1.0 - ADAM_B2) * (g * g)
        g_ref[...] = g
        nm_ref[...] = nm
        nv_ref[...] = nv
        d_ref[...] = -ADAM_LR * ((nm / c1) / (jnp.sqrt(nv / c2) + ADAM_EPS) + ADAM_WD * w_ref[...])

    spec = pl.BlockSpec((tr, cols), lambda i: (i, 0))
    return pl.pallas_call(
        body, name=name, grid=(rows // tr,), in_specs=[spec] * (3 + npart), out_specs=[spec] * 4,
        out_shape=[jax.ShapeDtypeStruct((rows, cols), F32)] * 4, compiler_params=_cp(("parallel",)))(w, m, v, *g_parts)


MESH = pl.DeviceIdType.MESH
_HBM = pl.BlockSpec(memory_space=pl.ANY)


def _chip_peers():
    x, y, c = lax.axis_index("x"), lax.axis_index("y"), lax.axis_index("c")
    return x, y, c, [(1 - x, y), (x, 1 - y), (1 - x, 1 - y)]


def _window(ref, kind, chip, rows, cols):
    if kind == "cols":
        return ref.at[:, pl.ds(pl.multiple_of(chip * cols, LANES), cols)]
    if kind == "rows":
        return ref.at[pl.ds(pl.multiple_of(chip * rows, 8), rows), :]
    return ref.at[chip]


def _gathered_shape(kind, rows, cols):
    return {"cols": (rows, 4 * cols), "rows": (4 * rows, cols), "slices": (4, rows, cols)}[kind]


def gather_layers(name, shards, kinds):
    n = len(shards)

    def body(*refs):
        x_refs, o_refs = refs[:n], refs[n:2 * n]
        send_sems, recv_sems, local_sems = refs[2 * n:]
        x, y, c, peers = _chip_peers()
        me = 2 * x + y
        started = []
        for a in range(n):
            _, rows, cols = shards[a].shape
            src = x_refs[a].at[c]
            mine = pltpu.make_async_copy(src, _window(o_refs[a], kinds[a], me, rows, cols), local_sems.at[a])
            mine.start()
            started.append(mine.wait)
            for k, (px, py) in enumerate(peers):
                cp = pltpu.make_async_remote_copy(
                    src_ref=src, dst_ref=_window(o_refs[a], kinds[a], me, rows, cols), send_sem=send_sems.at[3 * a + k],
                    recv_sem=recv_sems.at[3 * a + k], device_id=(px, py, c), device_id_type=MESH)
                cp.start()
                started.append(cp.wait_send)
        for a in range(n):
            _, rows, cols = shards[a].shape
            for k, (px, py) in enumerate(peers):
                pltpu.make_async_remote_copy(
                    src_ref=x_refs[a].at[c], dst_ref=_window(o_refs[a], kinds[a], 2 * px + py, rows, cols),
                    send_sem=send_sems.at[3 * a + k], recv_sem=recv_sems.at[3 * a + k], device_id=(px, py, c),
                    device_id_type=MESH).wait_recv()
        for wait in started:
            wait()

    return pl.pallas_call(
        body, name=name, in_specs=[_HBM] * n, out_specs=[_HBM] * n,
        out_shape=[jax.ShapeDtypeStruct(_gathered_shape(kinds[a], *shards[a].shape[1:]), shards[a].dtype)
                   for a in range(n)],
        scratch_shapes=[pltpu.SemaphoreType.DMA((3 * n,)), pltpu.SemaphoreType.DMA((3 * n,)),
                        pltpu.SemaphoreType.DMA((n,))],
        )(*shards)


def scatter_pieces(name, pieces):
    n = len(pieces)

    def body(*refs):
        p_refs, o_refs = refs[:n], refs[n:2 * n]
        send_sems, recv_sems, local_sems = refs[2 * n:]
        x, y, c, peers = _chip_peers()
        me = 2 * x + y
        started = []
        for a in range(n):
            mine = pltpu.make_async_copy(p_refs[a].at[me], o_refs[a].at[me], local_sems.at[a])
            mine.start()
            started.append(mine.wait)
            for k, (px, py) in enumerate(peers):
                cp = pltpu.make_async_remote_copy(
                    src_ref=p_refs[a].at[2 * px + py], dst_ref=o_refs[a].at[me], send_sem=send_sems.at[3 * a + k],
                    recv_sem=recv_sems.at[3 * a + k], device_id=(px, py, c), device_id_type=MESH)
                cp.start()
                started.append(cp.wait_send)
        for a in range(n):
            for k, (px, py) in enumerate(peers):
                pltpu.make_async_remote_copy(
                    src_ref=p_refs[a].at[me], dst_ref=o_refs[a].at[2 * px + py], send_sem=send_sems.at[3 * a + k],
                    recv_sem=recv_sems.at[3 * a + k], device_id=(px, py, c), device_id_type=MESH).wait_recv()
        for wait in started:
            wait()

    return pl.pallas_call(
        body, name=name, in_specs=[_HBM] * n, out_specs=[_HBM] * n,
        out_shape=[jax.ShapeDtypeStruct(p.shape, p.dtype) for p in pieces],
        scratch_shapes=[pltpu.SemaphoreType.DMA((3 * n,)), pltpu.SemaphoreType.DMA((3 * n,)),
                        pltpu.SemaphoreType.DMA((n,))],
        )(*pieces)


def _pair_step(n_steps, x_ref, land, send_sems, recv_sems, credits, consume):
    x, y, c = lax.axis_index("x"), lax.axis_index("y"), lax.axis_index("c")
    sib = (x, y, 1 - c)
    i = pl.program_id(0)
    slot = i % 2

    @pl.when(i >= 2)
    def _():
        pl.semaphore_wait(credits.at[slot], 1)

    cp = pltpu.make_async_remote_copy(src_ref=x_ref, dst_ref=land.at[slot], send_sem=send_sems.at[slot],
                                      recv_sem=recv_sems.at[slot], device_id=sib, device_id_type=MESH)
    cp.start()
    cp.wait_recv()
    consume(land[slot])

    @pl.when(i < n_steps - 2)
    def _():
        pl.semaphore_signal(credits.at[slot], inc=1, device_id=sib, device_id_type=MESH)

    cp.wait_send()


def _pair_call(name, body, n_steps, in_specs, out_spec, out_shape, blk_shape, dtype, operands, extra_scratch=()):
    grid_spec = pltpu.PrefetchScalarGridSpec(
        num_scalar_prefetch=1, grid=(n_steps,), in_specs=in_specs, out_specs=out_spec,
        scratch_shapes=[pltpu.VMEM((2,) + blk_shape, dtype), pltpu.SemaphoreType.DMA((2,)),
                        pltpu.SemaphoreType.DMA((2,)), pltpu.SemaphoreType.REGULAR((2,)), *extra_scratch])
    return pl.pallas_call(body, name=name, grid_spec=grid_spec, out_shape=out_shape,
                          compiler_params=_cp(("arbitrary",)))(*operands)


def _place():
    return jnp.stack([lax.axis_index("x"), lax.axis_index("y"), lax.axis_index("c")]).astype(jnp.int32)


def _pair_rows(rows, row_bytes):
    for cand in (4096, 2048, 1024, 768, 512, 384, 256, 192, 128, 96, 64, 48, 32, 16):
        if rows % cand == 0 and cand * row_bytes <= PAIR_BLOCK_BYTES:
            return cand
    return _pick(rows, (16, 8))


def exchange_both(name, mine):
    rows, cols = mine.shape
    tr = _pair_rows(rows, cols * mine.dtype.itemsize)
    n_steps = rows // tr

    def body(s_ref, x_ref, o_ref, land, send_sems, recv_sems, credits):
        c = lax.axis_index("c")
        o_ref[c] = x_ref[...]

        def consume(v):
            o_ref[1 - c] = v
        _pair_step(n_steps, x_ref, land, send_sems, recv_sems, credits, consume)

    return _pair_call(name, body, n_steps, [pl.BlockSpec((tr, cols), lambda i, s: (i, 0))],
                      pl.BlockSpec((2, tr, cols), lambda i, s: (0, i, 0)),
                      jax.ShapeDtypeStruct((2, rows, cols), mine.dtype), (tr, cols), mine.dtype, (_place(), mine))


def exchange_add(name, both_layers):
    _, rows, cols = both_layers.shape
    tr = _pair_rows(rows, cols * both_layers.dtype.itemsize)
    nb = rows // tr
    flat = both_layers.reshape(2 * rows, cols)

    def body(s_ref, x_ref, m_ref, o_ref, land, send_sems, recv_sems, credits):
        def consume(v):
            o_ref[...] = (m_ref[...].astype(F32) + v.astype(F32)).astype(o_ref.dtype)
        _pair_step(nb, x_ref, land, send_sems, recv_sems, credits, consume)

    return _pair_call(name, body, nb,
                      [pl.BlockSpec((tr, cols), lambda i, s: ((1 - s[2]) * nb + i, 0)),
                       pl.BlockSpec((tr, cols), lambda i, s: (s[2] * nb + i, 0))],
                      pl.BlockSpec((tr, cols), lambda i, s: (i, 0)), jax.ShapeDtypeStruct((rows, cols), CDT),
                      (tr, cols), flat.dtype, (_place(), flat, flat))


def sum_exchange(name, parts):
    npart, rows, cols = parts.shape
    tr = _pair_rows(rows, cols * 4)
    n_steps = rows // tr

    def body(s_ref, x_ref, o_ref, land, send_sems, recv_sems, credits, mine):
        c = lax.axis_index("c")
        acc = x_ref[0].astype(F32)
        for k in range(1, npart):
            acc = acc + x_ref[k].astype(F32)
        mine[...] = acc
        o_ref[c] = acc

        def consume(v):
            o_ref[1 - c] = v
        _pair_step(n_steps, mine, land, send_sems, recv_sems, credits, consume)

    return _pair_call(name, body, n_steps, [pl.BlockSpec((npart, tr, cols), lambda i, s: (0, i, 0))],
                      pl.BlockSpec((2, tr, cols), lambda i, s: (0, i, 0)),
                      jax.ShapeDtypeStruct((2, rows, cols), F32), (tr, cols), F32, (_place(), parts),
                      extra_scratch=(pltpu.VMEM((tr, cols), F32),))


def allreduce_small(name, buf):
    rows = buf.shape[0]

    def body(x_ref, out_ref, gath, send_sems, recv_sems):
        x, y, c = lax.axis_index("x"), lax.axis_index("y"), lax.axis_index("c")
        me = 4 * x + 2 * y + c
        masks = [(k >> 2 & 1, k >> 1 & 1, k & 1) for k in range(1, 8)]

        def flip(v, bit):
            return 1 - v if bit else v

        sends = []
        for k, (bx, by, bc) in enumerate(masks):
            cp = pltpu.make_async_remote_copy(src_ref=x_ref, dst_ref=gath.at[me], send_sem=send_sems.at[k],
                                              recv_sem=recv_sems.at[k],
                                              device_id=(flip(x, bx), flip(y, by), flip(c, bc)), device_id_type=MESH)
            cp.start()
            sends.append(cp)
        gath[me] = x_ref[...]
        for k, (bx, by, bc) in enumerate(masks):
            px, py, pc = flip(x, bx), flip(y, by), flip(c, bc)
            pltpu.make_async_remote_copy(src_ref=x_ref, dst_ref=gath.at[4 * px + 2 * py + pc],
                                         send_sem=send_sems.at[k], recv_sem=recv_sems.at[k],
                                         device_id=(px, py, pc), device_id_type=MESH).wait_recv()
        for cp in sends:
            cp.wait_send()
        acc = gath[0]
        for d in range(1, 8):
            acc = acc + gath[d]
        out_ref[...] = acc

    return pl.pallas_call(
        body, name=name, in_specs=[pl.BlockSpec(memory_space=pltpu.VMEM)],
        out_specs=pl.BlockSpec(memory_space=pltpu.VMEM), out_shape=jax.ShapeDtypeStruct(buf.shape, F32),
        scratch_shapes=[pltpu.VMEM((8, rows, LANES), F32), pltpu.SemaphoreType.DMA((7,)),
                        pltpu.SemaphoreType.DMA((7,))],
        )(buf)


def _pack_flat(arrs, dtype, width, row_mult=8):
    flat = jnp.concatenate([a.reshape(-1).astype(dtype) for a in arrs])
    pad = (-flat.shape[0]) % (row_mult * width)
    if pad:
        flat = jnp.concatenate([flat, jnp.zeros((pad,), dtype)])
    return flat.reshape(-1, width)


def _unpack_flat(buf, shapes):
    flat = buf.reshape(-1)
    out, off = [], 0
    for s in shapes:
        n = math.prod(s)
        out.append(flat[off:off + n].reshape(s))
        off += n
    return out


def _in_to_padded(w):
    parts = []
    for name in IN_NEW_ORDER:
        _, width, o_off, o_w = IN_LAYOUT[name]
        parts.append(w[..., o_off:o_off + o_w])
        if o_w < width:
            parts.append(jnp.zeros(w.shape[:-1] + (width - o_w,), w.dtype))
    used = sum(IN_LAYOUT[n][1] for n in IN_NEW_ORDER)
    parts.append(jnp.zeros(w.shape[:-1] + (IN_PAD - used,), w.dtype))
    return jnp.concatenate(parts, axis=-1)


def _in_from_padded(g):
    parts = []
    for name in IN_ORIG_ORDER:
        off, _, _, o_w = IN_LAYOUT[name]
        parts.append(g[..., off:off + o_w])
    return jnp.concatenate(parts, axis=-1)


def _pcol(p, name):
    off, width, _, _ = IN_LAYOUT[name]
    return Cols(p, off, width)


def _lane_pad(v, width=LANES):
    v = v.reshape(-1)
    return jnp.concatenate([v, jnp.zeros((width - v.shape[0],), v.dtype)]).reshape(1, width)


def _f_norm_mod(h, sh, sc, w):
    return (_rms(h, w) * (1.0 + sc) + sh,)


def _f_norm_mod_thru(h, sh, sc, w):
    return h, _rms(h, w) * (1.0 + sc) + sh


def _f_attn_prep(qraw, kraw, vraw, cq, sq, ck, sk, qw, kw, gq, gk):
    q = qraw * lax.rsqrt(_group_mean(qraw * qraw, gq) + NORM_EPS) * qw
    q = _rope32(q, cq, sq) * (ATTN_HD ** -0.5)
    k = kraw * lax.rsqrt(_group_mean(kraw * kraw, gk) + NORM_EPS) * kw
    return q, _rope32(k, ck, sk), vraw


def _f_ssd_finish(yf, yb, xs, z, d_exp, nw):
    y = (yf + yb + d_exp * xs) * (z * jax.nn.sigmoid(z))
    return (_rms(y, nw),)


def _f_ret_prep(rq, rk, cos_full, sin_signed):
    return _rope64(rq, cos_full, sin_signed), _rope64(rk, cos_full, sin_signed) * (RET_DK ** -0.5)


def _f_ret_finish(yf, yb, g, gw):
    y = yf + yb
    outs = []
    for h in range(RET_HEADS):
        yh = y[:, h * RET_DK:(h + 1) * RET_DK]
        yc = yh - jnp.mean(yh, axis=-1, keepdims=True)
        outs.append(yc * lax.rsqrt(jnp.mean(yc * yc, axis=-1, keepdims=True) + NORM_EPS))
    return (jnp.concatenate(outs, axis=1) * gw * (g * jax.nn.sigmoid(g)),)


def _f_merge(p0, p1, p2, g0, g1, g2):
    return (jax.nn.sigmoid(g0) * p0 + jax.nn.sigmoid(g1) * p1 + jax.nn.sigmoid(g2) * p2,)


def _f_mid(h, mix, g1, sh2, sc2, w2):
    h_mid = h + g1 * mix
    return h_mid, _rms(h_mid, w2) * (1.0 + sc2) + sh2


def _f_sqrelu(a):
    r = jnp.maximum(a, 0.0)
    return (r * r,)


def _f_residual(h_mid, o, g2):
    return (h_mid + g2 * o,)


def _f_silu(x):
    return (x * jax.nn.sigmoid(x),)


def _f_bias(x, b):
    return (x + b,)


def _ssd_rows(xbc, p):
    rows = [Cols(xbc, LANES * k, LANES) for k in range(4)]
    rows += [Cols(xbc, 512 + LANES * g, LANES) for g in range(2)]
    rows += [Cols(xbc, 768 + LANES * g, LANES) for g in range(2)]
    return rows + [_pcol(p, "dt")]


def _ret_rows(rq, rk, p):
    off_v = IN_LAYOUT["rv"][0]
    return ([Cols(rq, LANES * h, LANES) for h in range(4)] + [Cols(rk, LANES * h, LANES) for h in range(4)]
            + [Cols(p, off_v + LANES * h, LANES) for h in range(4)])


def layer_fwd(li, h, mod, lw, tabs, m_ctx):
    t = h.shape[0]
    nb = m_ctx // min(ROW_TILE, t)
    sh1, sc1, g1, sh2, sc2, g2 = mod
    nm = lambda s: f"l{li}_{s}"
    sv = {}
    (u,) = rowwise_fwd(nm("norm1"), _f_norm_mod, [h], [sh1, sc1], [lw["norm1_w"]], [(D_MODEL, CDT)], t, nb)
    p = mm(nm("in_proj"), u, lw["w_in"], F32)
    q, k, v = rowwise_fwd(
        nm("attn_prep"), _f_attn_prep,
        [_pcol(p, "q"), _pcol(p, "k"), _pcol(p, "v"), tabs["cq"], tabs["sq"], tabs["ck"], tabs["sk"]], [],
        [lw["qw"], lw["kw"], tabs["gq"], tabs["gk"]], [(512, CDT), (128, CDT), (128, CDT)], t, nb)
    tq = min(ATTN_TQ, m_ctx)
    kk, vv = _split_kv(k), _split_kv(v)
    attn_o, qT, oT, lse = attn_fwd(nm("attn"), q, kk, vv.transpose(0, 2, 1), m_ctx, tq)

    xbc = conv_fwd(_pcol(p, "xbc"), lw["conv_w"], lw["conv_b"], m_ctx)
    ssd_sh = [lw["dt_bias"], lw["a_log"]]
    yf, yb, sf, sb = scan_fwd(nm("ssd"), _make_ssd_chunk, _ssd_rows(xbc, p), ssd_sh, 4, 512, t, m_ctx)
    (ssd_o,) = rowwise_fwd(nm("ssd_fin"), _f_ssd_finish, [yf, yb, Cols(xbc, 0, 512), _pcol(p, "z")], [],
                           [lw["d_exp"], lw["ssd_nw"]], [(512, CDT)], t, nb)

    rq, rk = rowwise_fwd(nm("ret_prep"), _f_ret_prep, [_pcol(p, "rq"), _pcol(p, "rk"), tabs["rc"], tabs["rs"]],
                         [], [], [(512, F32), (512, F32)], t, nb)
    rf, rb, rsf, rsb = scan_fwd(nm("ret"), _make_ret_chunk, _ret_rows(rq, rk, p), [lw["ret_lg"]], 4, 512, t, m_ctx)
    (ret_o,) = rowwise_fwd(nm("ret_fin"), _f_ret_finish, [rf, rb, _pcol(p, "rg")], [], [lw["ret_gw"]],
                           [(512, CDT)], t, nb)

    pbs = [mm(nm(f"branch{b}"), br, lw["w_branch"][b], CDT) for b, br in enumerate((attn_o, ssd_o, ret_o))]
    gl = [Cols(p, 1024 * b, 1024) for b in range(3)]
    (merged,) = rowwise_fwd(nm("merge"), _f_merge, pbs + gl, [], [], [(D_MODEL, CDT)], t, nb)
    mix = mm(nm("out_proj"), merged, lw["w_out"], F32)
    h_mid, vv2 = rowwise_fwd(nm("mid"), _f_mid, [h, mix], [g1, sh2, sc2], [lw["norm2_w"]],
                             [(D_MODEL, F32), (D_MODEL, CDT)], t, nb)
    a = mm(nm("mlp1"), vv2, lw["w_mlp1"], CDT)
    (hh,) = rowwise_fwd(nm("sqrelu"), _f_sqrelu, [a], [], [], [(a.shape[1], CDT)], t, nb)
    o = mm(nm("mlp2"), hh, lw["w_mlp2"], F32)
    (h_out,) = rowwise_fwd(nm("resid"), _f_residual, [h_mid, o], [g2], [], [(D_MODEL, F32)], t, nb)
    sv.update(h=h, u=u, p=p, qT=qT, kk=kk, vv=vv, oT=oT, lse=lse, attn_o=attn_o, xbc=xbc, yf=yf, yb=yb,
              sf=sf, sb=sb, ssd_o=ssd_o, rq=rq, rk=rk, rf=rf, rb=rb, rsf=rsf, rsb=rsb, ret_o=ret_o, pbs=pbs,
              merged=merged, mix=mix, h_mid=h_mid, v=vv2, a=a, hh=hh, o=o)
    return h_out, sv


def layer_bwd(li, dh_out, sv, mod, lw, tabs, m_ctx):
    t = dh_out.shape[0]
    nb = m_ctx // min(ROW_TILE, t)
    sh1, sc1, g1, sh2, sc2, g2 = mod
    nm = lambda s: f"l{li}_{s}_bwd"
    gw = {}
    p = sv["p"]
    (do,), (dg2,), _ = rowwise_bwd(nm("resid"), _f_residual, [sv["h_mid"], sv["o"]], [g2], [], [dh_out],
                                   [False, True], [], [CDT], t, nb)
    dhh = mm(nm("mlp2_dx"), do, lw["w_mlp2"], CDT, transpose_b=True)
    gw["w_mlp2"] = mm_tn(nm("mlp2_dw"), sv["hh"], do, CDT, ("rows", lw["w_mlp2"].shape[0] // 4))
    (da,), _, _ = rowwise_bwd(nm("sqrelu"), _f_sqrelu, [sv["a"]], [], [], [dhh], [True], [], [CDT], t, nb)
    dv = mm(nm("mlp1_dx"), da, lw["w_mlp1"], F32, transpose_b=True)
    gw["w_mlp1"] = mm_tn(nm("mlp1_dw"), sv["v"], da, CDT, ("cols", lw["w_mlp1"].shape[1] // 4))
    (dh_a, dmix), (dg1, dsh2, dsc2), (gw["norm2_w"],) = rowwise_bwd(
        nm("mid"), _f_mid, [sv["h"], sv["mix"]], [g1, sh2, sc2], [lw["norm2_w"]], [dh_out, dv],
        [True, True], [True], [F32, CDT], t, nb)
    dmerged = mm(nm("out_dx"), dmix, lw["w_out"], CDT, transpose_b=True)
    gw["w_out"] = mm_tn(nm("out_dw"), sv["merged"], dmix, CDT, ("rows", lw["w_out"].shape[0] // 4))
    gl = [Cols(p, 1024 * b, 1024) for b in range(3)]
    dmg, _, _ = rowwise_bwd(nm("merge"), _f_merge, sv["pbs"] + gl, [], [], [dmerged], [True] * 6, [], [CDT] * 6,
                            t, nb)
    dpb, dgl = dmg[:3], dmg[3:]
    brs = (sv["attn_o"], sv["ssd_o"], sv["ret_o"])
    d_attn_o = mm(nm("branch0_dx"), dpb[0], lw["w_branch"][0], CDT, transpose_b=True)
    d_ssd_o = mm(nm("branch1_dx"), dpb[1], lw["w_branch"][1], F32, transpose_b=True)
    d_ret_o = mm(nm("branch2_dx"), dpb[2], lw["w_branch"][2], F32, transpose_b=True)
    n_loc = lw["w_branch"].shape[2] // 4
    gw["w_branch"] = jnp.stack([mm_tn(nm(f"branch{b}_dw"), brs[b], dpb[b], CDT, ("cols", n_loc)) for b in range(3)],
                               axis=1).reshape(4, -1, n_loc)
    tq = min(ATTN_TQ, m_ctx)
    dq_t, dk_s, dv_s = attn_bwd(nm("attn"), sv["qT"], d_attn_o, sv["oT"], sv["lse"], sv["kk"],
                                sv["kk"].transpose(0, 2, 1), sv["vv"], m_ctx)
    (dq_raw, dk_raw, dv_raw), _, (gw["qw"], gw["kw"]) = rowwise_bwd(
        nm("attn_prep"), _f_attn_prep,
        [_pcol(p, "q"), _pcol(p, "k"), _pcol(p, "v"), tabs["cq"], tabs["sq"], tabs["ck"], tabs["sk"]], [],
        [lw["qw"], lw["kw"], tabs["gq"], tabs["gk"]],
        [from_heads_t(nm("dq_rows"), dq_t, F32), _merge_kv(dk_s), _merge_kv(dv_s)],
        [True, True, True, False, False, False, False], [True, True, False, False], [CDT] * 3, t, nb)
    (dy_ssd, dxs_fin, dz), _, (gw["d_exp"], gw["ssd_nw"]) = rowwise_bwd(
        nm("ssd_fin"), _f_ssd_finish, [sv["yf"], sv["yb"], Cols(sv["xbc"], 0, 512), _pcol(p, "z")], [],
        [lw["d_exp"], lw["ssd_nw"]], [d_ssd_o], [True, False, True, True], [True, True], [F32, F32, CDT], t, nb)
    ssd_sh = [lw["dt_bias"], lw["a_log"]]
    post_ssd = lambda d: [jnp.concatenate(d[0:8], axis=1), d[8]]
    (dxbc_f, ddt_f), (dxbc_b, ddt_b), (gw["dt_bias"], gw["a_log"]) = scan_bwd(
        nm("ssd"), _make_ssd_chunk, _ssd_rows(sv["xbc"], p), ssd_sh, (sv["sf"], sv["sb"]), dy_ssd, post_ssd,
        [(1024, F32), (LANES, F32)], 4, t, m_ctx)
    ddt = (ddt_f + ddt_b).astype(CDT)
    dxbc_raw, gw["conv_w"], gw["conv_b"] = conv_bwd(_pcol(p, "xbc"), lw["conv_w"], lw["conv_b"], dxbc_f, dxbc_b,
                                                    dxs_fin, m_ctx)
    (dy_ret, drg), _, (gw["ret_gw"],) = rowwise_bwd(
        nm("ret_fin"), _f_ret_finish, [sv["rf"], sv["rb"], _pcol(p, "rg")], [], [lw["ret_gw"]], [d_ret_o],
        [True, False, True], [True], [F32, CDT], t, nb)
    post_ret = lambda d: [jnp.concatenate(d[0:4], axis=1), jnp.concatenate(d[4:8], axis=1),
                          jnp.concatenate(d[8:12], axis=1)]
    rrows = _ret_rows(sv["rq"], sv["rk"], p)
    (dq_f, dk_f, dv_f), (dq_b, dk_b, dv_b), (gw["ret_lg"],) = scan_bwd(
        nm("ret"), _make_ret_chunk, rrows, [lw["ret_lg"]], (sv["rsf"], sv["rsb"]), dy_ret, post_ret,
        [(512, F32)] * 3, 4, t, m_ctx)
    drv = (dv_f + dv_b).astype(CDT)
    (drq, drk), _, _ = rowwise_bwd(nm("ret_prep"), _f_ret_prep,
                                   [_pcol(p, "rq"), _pcol(p, "rk"), tabs["rc"], tabs["rs"]], [], [],
                                   [(dq_f, dq_b), (dk_f, dk_b)], [True, True, False, False], [], [CDT, CDT], t, nb)
    pieces = {"gates": None, "xbc": dxbc_raw, "q": dq_raw, "z": dz, "rq": drq, "rk": drk, "rv": drv, "rg": drg,
              "k": dk_raw, "v": dv_raw, "dt": ddt}
    cols = list(dgl) + [pieces[n] for n in IN_NEW_ORDER[1:]]
    used = sum(c.shape[1] for c in cols)
    cols.append(jnp.zeros((t, IN_PAD - used), CDT))
    dp = jnp.concatenate(cols, axis=1)
    du = mm(nm("in_dx"), dp, lw["w_in"], F32, transpose_b=True)
    gw["w_in"] = mm_tn(nm("in_dw"), sv["u"], dp, CDT)
    (dh_in,), (dsh1, dsc1), (gw["norm1_w"],) = rowwise_bwd(
        nm("norm1"), _f_norm_mod_thru, [sv["h"]], [sh1, sc1], [lw["norm1_w"]], [dh_a, du], [True], [True], [F32],
        t, nb)
    return dh_in, [dsh1, dsc1, dg1, dsh2, dsc2, dg2], gw


def _rope_tables(n_lat, m_ctx):
    rows = n_lat // GRID_W
    row = jnp.repeat(jnp.arange(rows, dtype=F32), GRID_W)
    col = jnp.tile(jnp.arange(GRID_W, dtype=F32), rows)
    nfreq = ATTN_HD // 4
    inv = ROPE_THETA ** (-jnp.arange(nfreq, dtype=F32) / nfreq)
    ang = jnp.concatenate([row[:, None] * inv, col[:, None] * inv], axis=-1)
    cos = jnp.concatenate([jnp.ones((m_ctx, ATTN_HD // 2), F32), jnp.cos(ang)], axis=0)
    sin = jnp.concatenate([jnp.zeros((m_ctx, ATTN_HD // 2), F32), jnp.sin(ang)], axis=0)
    c64 = jnp.concatenate([cos, cos], axis=1)
    s64 = jnp.concatenate([-sin, sin], axis=1)
    pos = jnp.arange(m_ctx + n_lat, dtype=F32)
    inv_r = ROPE_THETA ** (-jnp.linspace(0.0, 1.0, RET_DK // 2, dtype=F32))
    ang_r = pos[:, None] * inv_r
    rc = jnp.concatenate([jnp.cos(ang_r)] * 2, axis=1)
    rs = jnp.concatenate([-jnp.sin(ang_r), jnp.sin(ang_r)], axis=1)
    return dict(cq=jnp.tile(c64, (1, 8)), sq=jnp.tile(s64, (1, 8)), ck=jnp.tile(c64, (1, 2)), sk=jnp.tile(s64, (1, 2)),
                rc=jnp.tile(rc, (1, 4)), rs=jnp.tile(rs, (1, 4)), gq=_group_matrix(512, ATTN_HD),
                gk=_group_matrix(128, ATTN_HD))


def _layer_weights(full, small, layer):
    return dict(
        w_in=full["w_in"][layer], w_branch=full["w_branch"][layer], w_out=full["w_out"][layer],
        w_mlp1=full["w_mlp1"][layer], w_mlp2=full["w_mlp2"][layer],
        norm1_w=small["norm1_w"][layer][None], norm2_w=small["norm2_w"][layer][None],
        qw=jnp.tile(small["attn_q_norm"][layer], 8)[None], kw=jnp.tile(small["attn_k_norm"][layer], 2)[None],
        conv_w=small["ssd_conv_w"][layer], conv_b=small["ssd_conv_b"][layer][None],
        dt_bias=_lane_pad(small["ssd_dt_bias"][layer]), a_log=_lane_pad(small["ssd_a_log"][layer]),
        d_exp=jnp.repeat(small["ssd_d"][layer], SSD_HD)[None], ssd_nw=small["ssd_norm_w"][layer][None],
        ret_lg=_lane_pad(small["ret_log_decay"][layer]), ret_gw=small["ret_gn_w"][layer][None])


def local_step(x, c, ctx, full, small, loss_target):
    n_lat, d = x.shape
    m_ctx = ctx.shape[0]
    t = n_lat + m_ctx
    depth = small["norm1_w"].shape[0]
    tabs = _rope_tables(n_lat, m_ctx)
    h = jnp.concatenate([ctx, x], axis=0)
    cc = jnp.concatenate([small["c_ctx"][None], c, jnp.zeros((COND_ROWS - 2, d), F32)], axis=0)
    (scc,) = rowwise_fwd("cond_silu", _f_silu, [cc], [], [], [(d, CDT)], COND_ROWS, 0)
    mods, saved, lws = [], [], []
    for layer in range(depth):
        lw = _layer_weights(full, small, layer)
        mod_raw = mm(f"l{layer}_mod", scc, full["w_mod"][layer], F32)
        (mod8,) = rowwise_fwd(f"l{layer}_mod_bias", _f_bias, [mod_raw], [], [small["b_mod"][layer][None]],
                              [(6 * d, F32)], COND_ROWS, 0)
        mod = [mod8[0:2, k * d:(k + 1) * d].reshape(2, 1, d) for k in range(6)]
        h, sv = layer_fwd(layer, h, mod, lw, tabs, m_ctx)
        mods.append(mod)
        saved.append(sv)
        lws.append(lw)
    loss, dh, d_final = loss_head(h, loss_target, small["final_norm_w"][None], m_ctx)

    gbig = {k: [None] * depth for k in BIG}
    gs = {k: [None] * depth for k in SMALL if k not in ("c_ctx", "final_norm_w")}
    d_scc = None
    for layer in reversed(range(depth)):
        lw = lws[layer]
        dh, dmod, gw = layer_bwd(layer, dh, saved[layer], mods[layer], lw, tabs, m_ctx)
        dmod8 = jnp.concatenate([jnp.concatenate([g_.reshape(2, d) for g_ in dmod], axis=1),
                                 jnp.zeros((COND_ROWS - 2, 6 * d), F32)], axis=0)
        (dmod_c,), _, (db_mod,) = rowwise_bwd(f"l{layer}_mod_bias_bwd", _f_bias, [dmod8], [],
                                              [small["b_mod"][layer][None]], [dmod8], [True], [True], [CDT], COND_ROWS, 0)
        gbig["w_mod"][layer] = mm_tn(f"l{layer}_mod_dw", scc, dmod_c, CDT, ("cols", 6 * d // 4))
        part = mm(f"l{layer}_mod_dx", dmod_c, full["w_mod"][layer], F32, transpose_b=True)
        d_scc = part if d_scc is None else d_scc + part
        g_in = _in_from_padded(gw["w_in"])
        gbig["w_in"][layer] = g_in.reshape(d, 4, g_in.shape[1] // 4).transpose(1, 0, 2)
        for k in ("w_branch", "w_out", "w_mlp1", "w_mlp2"):
            gbig[k][layer] = gw[k]
        gs["b_mod"][layer] = db_mod.reshape(-1)
        gs["norm1_w"][layer] = gw["norm1_w"].reshape(-1)
        gs["norm2_w"][layer] = gw["norm2_w"].reshape(-1)
        gs["attn_q_norm"][layer] = gw["qw"].reshape(8, ATTN_HD).sum(0)
        gs["attn_k_norm"][layer] = gw["kw"].reshape(2, ATTN_HD).sum(0)
        gs["ssd_conv_w"][layer] = gw["conv_w"]
        gs["ssd_conv_b"][layer] = gw["conv_b"].reshape(-1)
        gs["ssd_dt_bias"][layer] = gw["dt_bias"][0, :16].reshape(2, 8)
        gs["ssd_a_log"][layer] = gw["a_log"][0, :16].reshape(2, 8)
        gs["ssd_d"][layer] = gw["d_exp"].reshape(SSD_HEADS, SSD_HD).sum(1)
        gs["ssd_norm_w"][layer] = gw["ssd_nw"].reshape(-1)
        gs["ret_log_decay"][layer] = gw["ret_lg"][0, :8].reshape(2, 4)
        gs["ret_gn_w"][layer] = gw["ret_gw"].reshape(-1)
    (d_cc,), _, _ = rowwise_bwd("cond_silu_bwd", _f_silu, [cc], [], [], [d_scc], [True], [], [F32], COND_ROWS, 0)
    g_small = {k: jnp.stack(v) for k, v in gs.items()}
    g_small["c_ctx"] = d_cc[0]
    g_small["final_norm_w"] = d_final.reshape(-1)
    g_big = {k: jnp.stack(v) for k, v in gbig.items()}
    return loss, dh[m_ctx:], g_big, g_small


def kernel(x, c, ctx, c_ctx, w_mod, b_mod, norm1_w, norm2_w, w_in, attn_q_norm, attn_k_norm, ssd_conv_w, ssd_conv_b, ssd_dt_bias, ssd_a_log, ssd_d, ssd_norm_w, ret_log_decay, ret_gn_w, w_branch, w_out, w_mlp1, w_mlp2, final_norm_w, loss_target, m_c_ctx, m_w_mod, m_b_mod, m_norm1_w, m_norm2_w, m_w_in, m_attn_q_norm, m_attn_k_norm, m_ssd_conv_w, m_ssd_conv_b, m_ssd_dt_bias, m_ssd_a_log, m_ssd_d, m_ssd_norm_w, m_ret_log_decay, m_ret_gn_w, m_w_branch, m_w_out, m_w_mlp1, m_w_mlp2, m_final_norm_w, v_c_ctx, v_w_mod, v_b_mod, v_norm1_w, v_norm2_w, v_w_in, v_attn_q_norm, v_attn_k_norm, v_ssd_conv_w, v_ssd_conv_b, v_ssd_dt_bias, v_ssd_a_log, v_ssd_d, v_ssd_norm_w, v_ret_log_decay, v_ret_gn_w, v_w_branch, v_w_out, v_w_mlp1, v_w_mlp2, v_final_norm_w):
    env = dict(locals())
    w_loc = {k: env[k] for k in WEIGHTS}
    m_loc = {k: env["m_" + k] for k in WEIGHTS}
    v_loc = {k: env["v_" + k] for k in WEIGHTS}
    chip = 2 * lax.axis_index("x") + lax.axis_index("y")
    core = lax.axis_index("c")

    depth = w_loc["w_mod"].shape[0]
    assert depth == 2, "the exchanges split the layers between a chip's two cores"
    shards = [w_loc[k].astype(CDT).reshape(depth, -1, w_loc[k].shape[-1]) for k in BIG]
    mine = gather_layers("gather_weights", shards, [BIG_KIND[k] for k in BIG])
    full = {}
    for k, arr in zip(BIG, mine):
        both = exchange_both("share_" + k, arr.reshape(-1, arr.shape[-1]))
        if k == "w_in":
            both = both.reshape(depth, 4, -1, both.shape[-1]).transpose(0, 2, 1, 3)
            both = _in_to_padded(both.reshape(depth, both.shape[1], -1))
        full[k] = both.reshape((depth,) + w_loc[k].shape[1:-1] + (-1,)) if BIG_KIND[k] == "cols" else \
            both.reshape((depth,) + w_loc[k].shape[1:-2] + (-1, w_loc[k].shape[-1])) if BIG_KIND[k] == "rows" else both

    cw = w_loc["ssd_conv_w"]
    cw_w = cw.shape[-1]
    placed = lax.dynamic_update_slice(jnp.zeros(cw.shape[:-1] + (4 * cw_w,), F32),
                                      cw * (core == 0).astype(F32), (0, 0, chip * cw_w))
    conv_full = _unpack_flat(allreduce_small("gather_conv_w", _pack_flat([placed], F32, LANES)), [placed.shape])[0]
    small = {k: w_loc[k] for k in SMALL}
    small["ssd_conv_w"] = conv_full

    loss_l, grad_x, g_big, g_small = local_step(x[0], c, ctx[0], full, small, loss_target[0])

    small_shapes = [g_small[k].shape for k in SMALL] + [(LANES,)]
    summed = _unpack_flat(allreduce_small("reduce_small", _pack_flat([g_small[k] for k in SMALL] + [loss_l], F32, LANES)),
                          small_shapes)
    gsum = dict(zip(SMALL, summed[:-1]))
    loss = summed[-1][0]
    gsum["ssd_conv_w"] = lax.dynamic_slice(gsum["ssd_conv_w"], (0, 0, chip * cw_w), cw.shape)

    pair = []
    for k in BIG:
        _, _, rows, cols = g_big[k].shape
        pair.append(exchange_add("pair_" + k, g_big[k].reshape(depth, 4 * rows, cols)).reshape(4, rows, cols))
    landed = scatter_pieces("scatter_grads", pair)
    g_sum = [sum_exchange("sum_" + k, parts) for k, parts in zip(BIG, landed)]

    grads, deltas, new_m, new_v = {}, {}, {}, {}
    for i, k in enumerate(BIG):
        shp = w_loc[k].shape
        two_d = lambda a, shp=shp: a.reshape(-1, shp[-1])
        res = adamw("adamw_" + k, two_d(w_loc[k]), two_d(m_loc[k]), two_d(v_loc[k]), [two_d(g_sum[i])])
        grads[k], deltas[k], new_m[k], new_v[k] = [r.reshape(shp) for r in res]
    small_loc_shapes = [w_loc[k].shape for k in SMALL]
    res = adamw("adamw_small", _pack_flat([w_loc[k] for k in SMALL], F32, LANES),
                _pack_flat([m_loc[k] for k in SMALL], F32, LANES), _pack_flat([v_loc[k] for k in SMALL], F32, LANES),
                [_pack_flat([gsum[k] for k in SMALL], F32, LANES)])
    for dst, r in zip((grads, deltas, new_m, new_v), res):
        dst.update(dict(zip(SMALL, _unpack_flat(r, small_loc_shapes))))

    return (loss, grad_x[None], *[grads[k] for k in WEIGHTS], *[deltas[k] for k in WEIGHTS],
            *[new_m[k] for k in WEIGHTS], *[new_v[k] for k in WEIGHTS])
```

```python
import functools
import math
from typing import NamedTuple

import jax
import jax.numpy as jnp
from jax import lax
from jax.experimental import pallas as pl
from jax.experimental.pallas import tpu as pltpu

F32 = jnp.float32
CDT = jnp.bfloat16
NORM_EPS = 1e-6
ROPE_THETA = 10000.0
GRID_W = 64
D_MODEL = 1024
ATTN_HEADS, ATTN_KV, ATTN_HD = 8, 2, 64
SSD_HEADS, SSD_HD, SSD_STATE = 8, 64, 128
RET_HEADS, RET_DK = 4, 128
CHUNK = 128
ROW_TILE = 256
MM_ROWS = 768
MM_VMEM_BUDGET = 44 * 1024 * 1024
ATTN_TQ, ATTN_TK = 256, 256
ATTN_TK_BWD = 1024
ATTN_TK_FWD = 2048
LANES = 128
PAIR_BLOCK_BYTES = 2 * 1024 * 1024
COND_ROWS = 16
VMEM_LIMIT = 56 * 1024 * 1024

ADAM_LR, ADAM_B1, ADAM_B2, ADAM_EPS, ADAM_WD, ADAM_STEP = 0.001, 0.9, 0.999, 1e-08, 0.01, 10

IN_LAYOUT = {
    "gates": (0, 3072, 4368, 3072), "xbc": (3072, 1024, 1280, 1024), "q": (4096, 512, 0, 512),
    "z": (4608, 512, 768, 512), "rq": (5120, 512, 2320, 512), "rk": (5632, 512, 2832, 512),
    "rv": (6144, 512, 3344, 512), "rg": (6656, 512, 3856, 512), "k": (7168, 128, 512, 128),
    "v": (7296, 128, 640, 128), "dt": (7424, 128, 2304, 16),
}
IN_PAD = 7680
IN_ORIG_ORDER = ("q", "k", "v", "z", "xbc", "dt", "rq", "rk", "rv", "rg", "gates")
IN_NEW_ORDER = ("gates", "xbc", "q", "z", "rq", "rk", "rv", "rg", "k", "v", "dt")

BIG = ("w_mod", "w_in", "w_branch", "w_out", "w_mlp1", "w_mlp2")
BIG_KIND = {"w_mod": "cols", "w_in": "slices", "w_branch": "cols", "w_out": "rows", "w_mlp1": "cols", "w_mlp2": "rows"}
SMALL = ("c_ctx", "b_mod", "norm1_w", "norm2_w", "attn_q_norm", "attn_k_norm", "ssd_conv_w", "ssd_conv_b",
         "ssd_dt_bias", "ssd_a_log", "ssd_d", "ssd_norm_w", "ret_log_decay", "ret_gn_w", "final_norm_w")
WEIGHTS = ("c_ctx", "w_mod", "b_mod", "norm1_w", "norm2_w", "w_in", "attn_q_norm", "attn_k_norm", "ssd_conv_w",
           "ssd_conv_b", "ssd_dt_bias", "ssd_a_log", "ssd_d", "ssd_norm_w", "ret_log_decay", "ret_gn_w",
           "w_branch", "w_out", "w_mlp1", "w_mlp2", "final_norm_w")


def _cp(sem):
    return pltpu.CompilerParams(dimension_semantics=sem, vmem_limit_bytes=VMEM_LIMIT)


class Cols(NamedTuple):
    arr: jax.Array
    off: int
    width: int


def _width(item):
    return item.width if isinstance(item, Cols) else item.shape[1]


def _row_in(item, rows, imap=None):
    imap = imap or (lambda i: i)
    if isinstance(item, Cols):
        assert item.off % item.width == 0
        blk = item.off // item.width
        return item.arr, pl.BlockSpec((rows, item.width), lambda i, blk=blk: (imap(i), blk))
    return item, pl.BlockSpec((rows, item.shape[1]), lambda i: (imap(i), 0))


def _const_spec(shape):
    return pl.BlockSpec(shape, lambda *_: (0,) * len(shape))


def _mxu(a, b, dims=(((1,), (0,)), ((), ()))):
    return lax.dot_general(a.astype(CDT), b.astype(CDT), dims, preferred_element_type=F32)


_NT = (((1,), (1,)), ((), ()))
_TN = (((0,), (0,)), ((), ()))


@jax.custom_vjp
def _softplus(x):
    return jnp.maximum(x, 0.0) + jnp.log1p(jnp.exp(-jnp.abs(x)))


def _softplus_fwd(x):
    return _softplus(x), x


def _softplus_bwd(x, g):
    return (g * jax.nn.sigmoid(x),)


_softplus.defvjp(_softplus_fwd, _softplus_bwd)


def _group_mean_impl(x, gmat):
    hi = x.astype(CDT)
    lo = (x - hi.astype(F32)).astype(CDT)
    return (jnp.dot(hi, gmat, preferred_element_type=F32) + jnp.dot(lo, gmat, preferred_element_type=F32))


@jax.custom_vjp
def _group_mean(x, gmat):
    return _group_mean_impl(x, gmat)


def _group_mean_fwd(x, gmat):
    return _group_mean_impl(x, gmat), gmat


def _group_mean_bwd(gmat, g):
    return _group_mean_impl(g, gmat), jnp.zeros_like(gmat)


_group_mean.defvjp(_group_mean_fwd, _group_mean_bwd)


def _group_matrix(width, group):
    r = jnp.arange(width) // group
    return jnp.where(r[:, None] == r[None, :], 1.0 / group, 0.0).astype(CDT)


def _make_rope(half):
    def partner(x):
        w = x.shape[1]
        lane = lax.broadcasted_iota(jnp.int32, x.shape, 1)
        first = (lane % (2 * half)) < half
        return jnp.where(first, pltpu.roll(x, w - half, axis=1), pltpu.roll(x, half, axis=1))

    def impl(x, cos_full, sin_signed):
        return x * cos_full + partner(x) * sin_signed

    @jax.custom_vjp
    def rope(x, cos_full, sin_signed):
        return impl(x, cos_full, sin_signed)

    def fwd(x, cos_full, sin_signed):
        return impl(x, cos_full, sin_signed), (cos_full, sin_signed)

    def bwd(res, g):
        cos_full, sin_signed = res
        return impl(g, cos_full, -sin_signed), jnp.zeros_like(cos_full), jnp.zeros_like(sin_signed)

    rope.defvjp(fwd, bwd)
    return rope


_rope32 = _make_rope(32)
_rope64 = _make_rope(64)


def _rms(x, w):
    return x * lax.rsqrt(jnp.mean(x * x, axis=-1, keepdims=True) + NORM_EPS) * w


def _col(v, lane_index):
    lane = lax.broadcasted_iota(jnp.int32, v.shape, 1)
    return jnp.sum(jnp.where(lane == lane_index, v, 0.0), axis=1, keepdims=True)


def _typed_spec(width, nb_ctx):
    return pl.BlockSpec((None, 1, width), lambda i: (jnp.where(i >= nb_ctx, 1, 0), 0, 0))


def rowwise_fwd(name, f, rows, typed, shared, outs, n_rows, nb_ctx, tm=ROW_TILE):
    tm = min(tm, n_rows)
    nin = len(rows) + len(typed) + len(shared)

    def body(*refs):
        res = f(*[r[...] for r in refs[:nin]])
        for o_ref, o in zip(refs[nin:], res):
            o_ref[...] = o.astype(o_ref.dtype)

    arrs, specs = [], []
    for it in rows:
        a, s = _row_in(it, tm)
        arrs.append(a)
        specs.append(s)
    for t in typed:
        arrs.append(t)
        specs.append(_typed_spec(t.shape[-1], nb_ctx))
    for s_ in shared:
        arrs.append(s_)
        specs.append(_const_spec(s_.shape))
    res = pl.pallas_call(
        body, name=name, grid=(n_rows // tm,), in_specs=specs,
        out_specs=[pl.BlockSpec((tm, w), lambda i: (i, 0)) for w, _ in outs],
        out_shape=[jax.ShapeDtypeStruct((n_rows, w), dt) for w, dt in outs],
        compiler_params=_cp(("parallel",)))(*arrs)
    return res


def rowwise_bwd(name, f, rows, typed, shared, cots, row_diff, shared_diff, drow_dtypes, n_rows, nb_ctx, tm=ROW_TILE):
    tm = min(tm, n_rows)
    cot_groups = [c_ if isinstance(c_, tuple) else (c_,) for c_ in cots]
    cots = [a for grp in cot_groups for a in grp]
    nr, nt, ns, nc = len(rows), len(typed), len(shared), len(cots)
    nin = nr + nt + ns
    d_rows = [k for k in range(nr) if row_diff[k]]
    d_sh = [k for k in range(ns) if shared_diff[k]]

    def body(*refs):
        rvals = [r[...] for r in refs[:nr]]
        tvals = [r[...] for r in refs[nr:nr + nt]]
        svals = [r[...] for r in refs[nr + nt:nin]]
        cparts = [r[...].astype(F32) for r in refs[nin:nin + nc]]
        cvals = []
        for grp in cot_groups:
            cvals.append(sum(cparts[1:len(grp)], cparts[0]))
            cparts = cparts[len(grp):]
        out_refs = refs[nin + nc:]

        def g(*dv):
            dv = list(dv)
            rv = list(rvals)
            for k in d_rows:
                rv[k] = dv.pop(0)
            tv = [dv.pop(0) for _ in range(nt)]
            sv = list(svals)
            for k in d_sh:
                sv[k] = dv.pop(0)
            return tuple(o.astype(F32) for o in f(*rv, *tv, *sv))

        prim = [rvals[k].astype(F32) for k in d_rows] + tvals + [svals[k] for k in d_sh]
        _, vjp = jax.vjp(g, *prim)
        grads = list(vjp(tuple(cvals)))
        i = pl.program_id(0)
        for ref in out_refs[:len(d_rows)]:
            ref[...] = grads.pop(0).astype(ref.dtype)
        first_typed = (i == 0) | (i == nb_ctx)
        for ref in out_refs[len(d_rows):len(d_rows) + nt]:
            gr = grads.pop(0)

            @pl.when(first_typed)
            def _(ref=ref, gr=gr):
                ref[...] = gr

            @pl.when(jnp.logical_not(first_typed))
            def _(ref=ref, gr=gr):
                ref[...] += gr
        for ref in out_refs[len(d_rows) + nt:]:
            gr = grads.pop(0)

            @pl.when(i == 0)
            def _(ref=ref, gr=gr):
                ref[...] = gr

            @pl.when(i != 0)
            def _(ref=ref, gr=gr):
                ref[...] += gr

    arrs, specs = [], []
    for it in list(rows):
        a, s = _row_in(it, tm)
        arrs.append(a)
        specs.append(s)
    for t in typed:
        arrs.append(t)
        specs.append(_typed_spec(t.shape[-1], nb_ctx))
    for s_ in shared:
        arrs.append(s_)
        specs.append(_const_spec(s_.shape))
    for c_ in cots:
        a, s = _row_in(c_, tm)
        arrs.append(a)
        specs.append(s)
    out_specs, out_shape = [], []
    for k, dt in zip(d_rows, drow_dtypes):
        w = _width(rows[k])
        out_specs.append(pl.BlockSpec((tm, w), lambda i: (i, 0)))
        out_shape.append(jax.ShapeDtypeStruct((n_rows, w), dt))
    for t in typed:
        out_specs.append(_typed_spec(t.shape[-1], nb_ctx))
        out_shape.append(jax.ShapeDtypeStruct(t.shape, F32))
    for k in d_sh:
        out_specs.append(_const_spec(shared[k].shape))
        out_shape.append(jax.ShapeDtypeStruct(shared[k].shape, F32))
    res = pl.pallas_call(body, name=name, grid=(n_rows // tm,), in_specs=specs, out_specs=out_specs,
                         out_shape=out_shape, compiler_params=_cp(("arbitrary",)))(*arrs)
    n1, n2 = len(d_rows), len(d_rows) + nt
    return list(res[:n1]), list(res[n1:n2]), list(res[n2:])


def _pick(n, prefs):
    for p in prefs:
        if n % p == 0:
            return p
    return n


def mm(name, a, b, out_dtype, transpose_b=False):
    n, k = b.shape if transpose_b else b.shape[::-1]
    m = (a.arr if isinstance(a, Cols) else a).shape[0]
    assert _width(a) == k
    tm = _pick(m, (MM_ROWS, 256))
    osz = jnp.dtype(out_dtype).itemsize
    tn = next(c for c in (2560, 2048, 1536, 1024, 512, 256, 128, n)
              if n % c == 0 and 2 * (tm * k * 2 + k * c * 2 + tm * c * osz) <= MM_VMEM_BUDGET or c == n)
    dims = _NT if transpose_b else (((1,), (0,)), ((), ()))

    def body(a_ref, b_ref, o_ref):
        o_ref[...] = lax.dot_general(a_ref[...], b_ref[...], dims, preferred_element_type=F32).astype(o_ref.dtype)

    a_arr, a_spec = _row_in(a, tm)
    a_spec = pl.BlockSpec(a_spec.block_shape, lambda j, i, f=a_spec.index_map: f(i))
    b_spec = pl.BlockSpec((tn, k), lambda j, i: (j, 0)) if transpose_b else pl.BlockSpec((k, tn), lambda j, i: (0, j))
    return pl.pallas_call(
        body, name=name, grid=(n // tn, m // tm), in_specs=[a_spec, b_spec],
        out_specs=pl.BlockSpec((tm, tn), lambda j, i: (i, j)),
        out_shape=jax.ShapeDtypeStruct((m, n), out_dtype),
        compiler_params=_cp(("parallel", "parallel")))(a_arr, b)


def mm_tn(name, a, b, out_dtype=F32, pieces=None):
    t = (a.arr if isinstance(a, Cols) else a).shape[0]
    k, n = _width(a), _width(b)
    tt = _pick(t, (MM_ROWS, 256))
    k_unit = pieces[1] if pieces and pieces[0] == "rows" else k
    n_unit = pieces[1] if pieces and pieces[0] == "cols" else n
    tk = _pick(k_unit, (1024, 512, 256, 128))
    tn = _pick(n_unit, (1280, 1024, 512, 256, 128))
    n_t = t // tt

    def body(a_ref, b_ref, o_ref, acc):
        part = lax.dot_general(a_ref[...], b_ref[...], _TN, preferred_element_type=F32)
        ti = pl.program_id(2)

        @pl.when(ti == 0)
        def _():
            acc[...] = part

        @pl.when(ti != 0)
        def _():
            acc[...] += part

        @pl.when(ti == n_t - 1)
        def _():
            o_ref[...] = acc[...].astype(o_ref.dtype)

    def win(item, width):
        if isinstance(item, Cols):
            assert item.off % width == 0
            return item.arr, item.off // width
        return item, 0

    a_arr, a0 = win(a, tk)
    b_arr, b0 = win(b, tn)
    if pieces is None:
        out_spec = pl.BlockSpec((tk, tn), lambda ki, ni, ti: (ki, ni))
        out_shape = (k, n)
    elif pieces[0] == "cols":
        per = n_unit // tn
        out_spec = pl.BlockSpec((None, tk, tn), lambda ki, ni, ti: (ni // per, ki, ni % per))
        out_shape = (4, k, n_unit)
    else:
        per = k_unit // tk
        out_spec = pl.BlockSpec((None, tk, tn), lambda ki, ni, ti: (ki // per, ki % per, ni))
        out_shape = (4, k_unit, n)
    return pl.pallas_call(
        body, name=name, grid=(k // tk, n // tn, n_t),
        in_specs=[pl.BlockSpec((tt, tk), lambda ki, ni, ti: (ti, a0 + ki)),
                  pl.BlockSpec((tt, tn), lambda ki, ni, ti: (ti, b0 + ni))],
        out_specs=out_spec, out_shape=jax.ShapeDtypeStruct(out_shape, out_dtype),
        scratch_shapes=[pltpu.VMEM((tk, tn), F32)],
        compiler_params=_cp(("parallel", "parallel", "arbitrary")))(a_arr, b_arr)


def _heads_t(rows_blk):
    blk = rows_blk.astype(F32).T
    return jnp.concatenate([blk[hh * ATTN_HD:(hh + 1) * ATTN_HD, :] for hh in range(4)], axis=1)


def _heads_rows(t_blk):
    tq = t_blk.shape[1] // 4
    return jnp.concatenate([t_blk[:, hh * tq:(hh + 1) * tq] for hh in range(4)], axis=0).T


def attn_fwd(name, q, kk, vT, m_ctx, tq):
    t, hd = kk.shape[1], ATTN_HD
    nq, r = t // tq, 4 * tq
    tk = _pick(t - m_ctx, (ATTN_TK_FWD, ATTN_TK))
    nqc, n_lat_tiles = m_ctx // tq, (t - m_ctx) // tk

    def body(q_ref, k_ref, vT_ref, o_ref, qT_ref, oT_ref, lse_ref):
        i = pl.program_id(1)
        q_t = _heads_t(q_ref[...]).astype(CDT)
        qT_ref[...] = q_t

        def tile(off, size, carry):
            mi, li, acc = carry
            sub = min(size, ATTN_TK)
            offs = [off + u * sub for u in range(size // sub)]
            sts = [jnp.dot(k_ref[pl.ds(o, sub), :], q_t, preferred_element_type=F32) for o in offs]
            for o, st in zip(offs, sts):
                mn = jnp.maximum(mi, jnp.max(st, axis=0, keepdims=True))
                pt = jnp.exp(st - mn)
                al = jnp.exp(mi - mn)
                li = al * li + jnp.sum(pt, axis=0, keepdims=True)
                acc = al * acc + jnp.dot(vT_ref[:, pl.ds(o, sub)], pt.astype(CDT), preferred_element_type=F32)
                mi = mn
            return mi, li, acc

        init = (jnp.full((1, r), -1e30, F32), jnp.zeros((1, r), F32), jnp.zeros((hd, r), F32))
        carry = tile(0, m_ctx, init)
        mi, li, acc = lax.fori_loop(
            0, jnp.where(i < nqc, 0, n_lat_tiles),
            lambda j, cr: tile(pl.multiple_of(m_ctx + j * tk, ATTN_TK), tk, cr), carry)
        o_t = acc / li
        oT_ref[...] = o_t.astype(oT_ref.dtype)
        o_ref[...] = _heads_rows(o_t).astype(o_ref.dtype)
        lse_ref[...] = mi + jnp.log(li)

    blk_t = pl.BlockSpec((None, None, hd, r), lambda g, i: (g, i, 0, 0))
    rows = pl.BlockSpec((tq, 4 * hd), lambda g, i: (i, g))
    return pl.pallas_call(
        body, name=name, grid=(2, nq),
        in_specs=[rows, pl.BlockSpec((None, t, hd), lambda g, i: (g, 0, 0)),
                  pl.BlockSpec((None, hd, t), lambda g, i: (g, 0, 0))],
        out_specs=[rows, blk_t, blk_t, pl.BlockSpec((None, None, 1, r), lambda g, i: (g, i, 0, 0))],
        out_shape=[jax.ShapeDtypeStruct((t, 8 * hd), CDT), jax.ShapeDtypeStruct((2, nq, hd, r), CDT),
                   jax.ShapeDtypeStruct((2, nq, hd, r), CDT), jax.ShapeDtypeStruct((2, nq, 1, r), F32)],
        compiler_params=_cp(("parallel", "arbitrary")))(q, kk, vT)


def attn_bwd(name, qT, do, oT, lse, kk, kT, vv, m_ctx):
    _, nq, hd, r = qT.shape
    t = kk.shape[1]
    tq = r // 4
    tk = _pick(t - m_ctx, (ATTN_TK_BWD, ATTN_TK))
    nqc, n_lat_tiles = m_ctx // tq, (t - m_ctx) // tk

    def body(qT_ref, do_ref, oT_ref, lse_ref, k_ref, kT_ref, v_ref, dq_ref, dk_ref, dv_ref):
        i = pl.program_id(1)

        @pl.when(i == 0)
        def _():
            dk_ref[...] = jnp.zeros_like(dk_ref)
            dv_ref[...] = jnp.zeros_like(dv_ref)

        q_t = qT_ref[...]
        do_f = _heads_t(do_ref[...])
        do_t = do_f.astype(CDT)
        lse = lse_ref[...]
        delta = jnp.sum(do_f * oT_ref[...].astype(F32), axis=0, keepdims=True)

        def tile(off, size, dq):
            sub = min(size, ATTN_TK)
            offs = [off + u * sub for u in range(size // sub)]
            sts = [jnp.dot(k_ref[pl.ds(o, sub), :], q_t, preferred_element_type=F32) for o in offs]
            dpts = [jnp.dot(v_ref[pl.ds(o, sub), :], do_t, preferred_element_type=F32) for o in offs]
            for o, st, dpt in zip(offs, sts, dpts):
                pt = jnp.exp(st - lse)
                dv_ref[pl.ds(o, sub), :] += lax.dot_general(pt.astype(CDT), do_t, _NT, preferred_element_type=F32)
                dst = (pt * (dpt - delta)).astype(CDT)
                dk_ref[pl.ds(o, sub), :] += lax.dot_general(dst, q_t, _NT, preferred_element_type=F32)
                dq = dq + jnp.dot(kT_ref[:, pl.ds(o, sub)], dst, preferred_element_type=F32)
            return dq

        dq = tile(0, m_ctx, jnp.zeros((hd, r), F32))
        dq = lax.fori_loop(0, jnp.where(i < nqc, 0, n_lat_tiles),
                           lambda j, acc: tile(pl.multiple_of(m_ctx + j * tk, ATTN_TK), tk, acc), dq)
        dq_ref[...] = _heads_rows(dq)

    blk_t = pl.BlockSpec((None, None, hd, r), lambda g, i: (g, i, 0, 0))
    row = pl.BlockSpec((None, None, 1, r), lambda g, i: (g, i, 0, 0))
    kv = pl.BlockSpec((None, t, hd), lambda g, i: (g, 0, 0))
    rows = pl.BlockSpec((tq, 4 * hd), lambda g, i: (i, g))
    return pl.pallas_call(
        body, name=name, grid=(2, nq),
        in_specs=[blk_t, rows, blk_t, row, kv, pl.BlockSpec((None, hd, t), lambda g, i: (g, 0, 0)), kv],
        out_specs=[rows, kv, kv],
        out_shape=[jax.ShapeDtypeStruct((t, 8 * hd), F32), jax.ShapeDtypeStruct(kk.shape, F32),
                   jax.ShapeDtypeStruct(kk.shape, F32)],
        compiler_params=_cp(("parallel", "arbitrary")))(qT, do, oT, lse, kk, kT, vv)


def _split_kv(a):
    return a.reshape(a.shape[0], 2, ATTN_HD).transpose(1, 0, 2)


def _merge_kv(a):
    return a.transpose(1, 0, 2).reshape(a.shape[1], 2 * ATTN_HD)


def _chunk_order(rev, ncc, nct):
    if not rev:
        return lambda s: s
    return lambda s: jnp.where(s < ncc, ncc - 1 - s, nct - 1 - (s - ncc))


def scan_fwd(name, make_fn, rows, shared, n_state, y_width, n_rows, m_ctx):
    nct, ncc = n_rows // CHUNK, m_ctx // CHUNK
    orders = [_chunk_order(rev, ncc, nct) for rev in (False, True)]
    fns = [make_fn(0), make_fn(1)]
    nr, ns = len(rows), len(shared)

    def body(*refs):
        svals = [r[...] for r in refs[2 * nr:2 * nr + ns]]
        y_refs, sin_refs, st = refs[2 * nr + ns:2 * nr + ns + 2], refs[2 * nr + ns + 2:2 * nr + ns + 4], refs[-1]

        @pl.when(pl.program_id(0) == 0)
        def _():
            st[...] = jnp.zeros_like(st)

        for d in range(2):
            rvals = [r[...] for r in refs[d * nr:(d + 1) * nr]]
            prev = [st[d, k] for k in range(n_state)]
            sin_refs[d][...] = st[d]
            y, new = fns[d](rvals, svals, prev)
            y_refs[d][...] = y
            for k in range(n_state):
                st[d, k] = new[k]

    arrs, specs = [], []
    for order in orders:
        for it in rows:
            a, s = _row_in(it, CHUNK, order)
            arrs.append(a)
            specs.append(s)
    for s_ in shared:
        arrs.append(s_)
        specs.append(_const_spec(s_.shape))
    return pl.pallas_call(
        body, name=name, grid=(nct,), in_specs=specs,
        out_specs=[pl.BlockSpec((CHUNK, y_width), lambda s, o=o: (o(s), 0)) for o in orders]
        + [pl.BlockSpec((None, n_state, LANES, LANES), lambda s, o=o: (o(s), 0, 0, 0)) for o in orders],
        out_shape=[jax.ShapeDtypeStruct((n_rows, y_width), F32)] * 2
        + [jax.ShapeDtypeStruct((nct, n_state, LANES, LANES), F32)] * 2,
        scratch_shapes=[pltpu.VMEM((2, n_state, LANES, LANES), F32)],
        compiler_params=_cp(("arbitrary",)))(*arrs)


def scan_bwd(name, make_fn, rows, shared, states_in, dy, post, outs, n_state, n_rows, m_ctx, dirs=(0, 1)):
    nct, ncc = n_rows // CHUNK, m_ctx // CHUNK
    orders = [(lambda r, f=_chunk_order(d == 1, ncc, nct): f(nct - 1 - r)) for d in dirs]
    fns = [make_fn(d) for d in dirs]
    nd = len(dirs)
    nr, ns, no = len(rows), len(shared), len(outs)
    n_in = nd * nr + ns

    def body(*refs):
        svals = [r[...] for r in refs[nd * nr:n_in]]
        sin_refs, dy_refs = refs[n_in:n_in + nd], refs[n_in + nd:n_in + 2 * nd]
        out_refs = refs[n_in + 2 * nd:n_in + 2 * nd + nd * no]
        dsh_refs = refs[n_in + 2 * nd + nd * no:-1]
        dst = refs[-1]
        r = pl.program_id(0)

        @pl.when(r == 0)
        def _():
            dst[...] = jnp.zeros_like(dst)

        d_shared = None
        for d in range(nd):
            rvals = [x[...] for x in refs[d * nr:(d + 1) * nr]]
            prev = [sin_refs[d][k] for k in range(n_state)]
            _, vjp = jax.vjp(fns[d], rvals, svals, prev)
            d_rows, d_sh, d_prev = vjp((dy_refs[d][...], [dst[d, k] for k in range(n_state)]))
            for ref, val in zip(out_refs[d * no:(d + 1) * no], post(d_rows)):
                ref[...] = val.astype(ref.dtype)
            d_shared = d_sh if d_shared is None else [a + b for a, b in zip(d_shared, d_sh)]
            for k in range(n_state):
                dst[d, k] = d_prev[k]
        for ref, gr in zip(dsh_refs, d_shared):
            @pl.when(r == 0)
            def _(ref=ref, gr=gr):
                ref[...] = gr

            @pl.when(r != 0)
            def _(ref=ref, gr=gr):
                ref[...] += gr

    arrs, specs = [], []
    for order in orders:
        for it in rows:
            a, s = _row_in(it, CHUNK, order)
            arrs.append(a)
            specs.append(s)
    for s_ in shared:
        arrs.append(s_)
        specs.append(_const_spec(s_.shape))
    for sin, order in zip(states_in, orders):
        arrs.append(sin)
        specs.append(pl.BlockSpec((None, n_state, LANES, LANES), lambda r, o=order: (o(r), 0, 0, 0)))
    for order in orders:
        a, s = _row_in(dy, CHUNK, order)
        arrs.append(a)
        specs.append(s)
    out_specs = [pl.BlockSpec((CHUNK, w), lambda r, o=o: (o(r), 0)) for o in orders for w, _ in outs]
    out_shape = [jax.ShapeDtypeStruct((n_rows, w), dt) for _ in orders for w, dt in outs]
    for s_ in shared:
        out_specs.append(_const_spec(s_.shape))
        out_shape.append(jax.ShapeDtypeStruct(s_.shape, F32))
    res = pl.pallas_call(body, name=name, grid=(nct,), in_specs=specs, out_specs=out_specs, out_shape=out_shape,
                         scratch_shapes=[pltpu.VMEM((nd, n_state, LANES, LANES), F32)],
                         compiler_params=_cp(("arbitrary",)))(*arrs)
    return [list(res[d * no:(d + 1) * no]) for d in range(nd)] + [list(res[nd * no:])]


def _make_ssd_chunk(direction):
    rev = direction == 1
    base = 8 * direction

    def fn(rows, shared, prev):
        xs, bms, cms, dtraw = rows[0:4], rows[4:6], rows[6:8], rows[8]
        dt_bias, a_log = shared
        ln = dtraw.shape[0]
        dt_all = _softplus(dtraw + dt_bias)
        a_all = dt_all * (-jnp.exp(a_log))
        r_i = lax.broadcasted_iota(jnp.int32, (ln, ln), 0)
        c_i = lax.broadcasted_iota(jnp.int32, (ln, ln), 1)
        tri = (r_i <= c_i) if rev else (r_i >= c_i)
        a_cum_all = jnp.dot(tri.astype(F32), a_all, precision=lax.Precision.HIGHEST, preferred_element_type=F32)
        a_tot_all = jnp.sum(a_all, axis=0, keepdims=True)
        first = lax.broadcasted_iota(jnp.int32, (ln, LANES), 1) < SSD_HD
        first_row = lax.broadcasted_iota(jnp.int32, (LANES, 1), 0) < SSD_HD

        def lmat(acol):
            a_b = jnp.broadcast_to(acol, (ln, ln))
            seg = a_b - a_b.T
            return jnp.where(tri, jnp.exp(jnp.where(tri, seg, 0.0)), 0.0)

        ys, new = [], []
        for g in range(2):
            bm, cm = bms[g], cms[g]
            cb = _mxu(cm, bm, _NT)
            for jj in range(2):
                pr = 2 * g + jj
                h0, h1 = base + 2 * pr, base + 2 * pr + 1
                ac0, ac1 = _col(a_cum_all, h0), _col(a_cum_all, h1)
                at0, at1 = _col(a_tot_all, h0), _col(a_tot_all, h1)
                dt_pair = jnp.where(first, _col(dt_all, h0), _col(dt_all, h1))
                acum_pair = jnp.where(first, ac0, ac1)
                atot_pair = jnp.where(first[0:1], at0, at1)
                xd = xs[pr] * dt_pair
                st = _mxu(xd * jnp.exp(atot_pair - acum_pair), bm, _TN)
                new.append(prev[pr] * jnp.where(first_row, jnp.exp(at0), jnp.exp(at1)) + st)
                y0 = _mxu(cb * lmat(ac0), xd)
                y1 = _mxu(cb * lmat(ac1), xd)
                y_off = _mxu(cm, prev[pr], _NT) * jnp.exp(acum_pair)
                ys.append(jnp.where(first, y0, y1) + y_off)
        return jnp.concatenate(ys, axis=1), new

    return fn


def _make_ret_chunk(direction):
    rev = direction == 1
    base = 4 * direction

    def fn(rows, shared, prev):
        qs, ks, vs = rows[0:4], rows[4:8], rows[8:12]
        lg_all = -jnp.exp(shared[0])
        ln = qs[0].shape[0]
        pos = lax.broadcasted_iota(jnp.int32, (ln, 1), 0).astype(F32)
        r_i = lax.broadcasted_iota(jnp.int32, (ln, ln), 0)
        c_i = lax.broadcasted_iota(jnp.int32, (ln, ln), 1)
        diff = ((c_i - r_i) if rev else (r_i - c_i))
        mask = diff >= 0
        dpos = jnp.maximum(diff, 0).astype(F32)
        k_pow = pos if rev else (ln - 1.0 - pos)
        q_pow = (ln - pos) if rev else (pos + 1.0)
        ys, new = [], []
        for h in range(RET_HEADS):
            lg = _col(lg_all, base + h)
            dmat = jnp.where(mask, jnp.exp(dpos * lg), 0.0)
            st = _mxu(ks[h] * jnp.exp(k_pow * lg), vs[h], _TN)
            new.append(prev[h] * jnp.exp(ln * lg) + st)
            s = _mxu(qs[h], ks[h], _NT) * dmat
            ys.append(_mxu(s, vs[h]) + _mxu(qs[h], prev[h]) * jnp.exp(q_pow * lg))
        return jnp.concatenate(ys, axis=1), new

    return fn


def _conv_pre(x, w, b, t_idx, n_rows, m_ctx):
    is_start = (t_idx == 0) | (t_idx == m_ctx)
    is_end = (t_idx == m_ctx - 1) | (t_idx == n_rows - 1)
    xp = jnp.where(is_start, 0.0, pltpu.roll(x, 1, axis=0))
    xn = jnp.where(is_end, 0.0, pltpu.roll(x, n_rows - 1, axis=0))
    return w[0:1] * xp + w[1:2] * x + w[2:3] * xn + b, xp, xn, is_start, is_end


def conv_fwd(x, conv_w, conv_b, m_ctx):
    n_rows, width = x.arr.shape[0], x.width
    c0 = x.off // LANES

    def body(x_ref, w_ref, b_ref, o_ref):
        t_idx = lax.broadcasted_iota(jnp.int32, (n_rows, 1), 0)
        pre = _conv_pre(x_ref[...], w_ref[...], b_ref[...], t_idx, n_rows, m_ctx)[0]
        o_ref[...] = pre * jax.nn.sigmoid(pre)

    return pl.pallas_call(
        body, name="conv_fwd", grid=(width // LANES,),
        in_specs=[pl.BlockSpec((n_rows, LANES), lambda c: (0, c0 + c)),
                  pl.BlockSpec((3, LANES), lambda c: (0, c)), pl.BlockSpec((1, LANES), lambda c: (0, c))],
        out_specs=pl.BlockSpec((n_rows, LANES), lambda c: (0, c)),
        out_shape=jax.ShapeDtypeStruct((n_rows, width), F32),
        compiler_params=_cp(("parallel",)))(x.arr, conv_w, conv_b)


def conv_bwd(x, conv_w, conv_b, dy_a, dy_b, dxs_extra, m_ctx):
    n_rows, width = x.arr.shape[0], x.width
    c0 = x.off // LANES
    n_extra = dxs_extra.shape[1] // LANES

    def body(x_ref, w_ref, b_ref, dya_ref, dyb_ref, ex_ref, dx_ref, dw_ref, db_ref):
        c = pl.program_id(0)
        t_idx = lax.broadcasted_iota(jnp.int32, (n_rows, 1), 0)
        w = w_ref[...]
        pre, xp, xn, is_start, is_end = _conv_pre(x_ref[...], w, b_ref[...], t_idx, n_rows, m_ctx)
        sg = jax.nn.sigmoid(pre)
        dyv = dya_ref[...] + dyb_ref[...] + jnp.where(c < n_extra, ex_ref[...], 0.0)
        dpre = dyv * (sg * (1.0 + pre * (1.0 - sg)))
        d_next = jnp.where(is_end, 0.0, pltpu.roll(dpre, n_rows - 1, axis=0))
        d_prev = jnp.where(is_start, 0.0, pltpu.roll(dpre, 1, axis=0))
        dx_ref[...] = (w[1:2] * dpre + w[0:1] * d_next + w[2:3] * d_prev).astype(dx_ref.dtype)
        dw_ref[...] = jnp.concatenate([jnp.sum(dpre * xp, axis=0, keepdims=True),
                                       jnp.sum(dpre * x_ref[...], axis=0, keepdims=True),
                                       jnp.sum(dpre * xn, axis=0, keepdims=True)], axis=0)
        db_ref[...] = jnp.sum(dpre, axis=0, keepdims=True)

    return pl.pallas_call(
        body, name="conv_bwd", grid=(width // LANES,),
        in_specs=[pl.BlockSpec((n_rows, LANES), lambda c: (0, c0 + c)),
                  pl.BlockSpec((3, LANES), lambda c: (0, c)), pl.BlockSpec((1, LANES), lambda c: (0, c)),
                  pl.BlockSpec((n_rows, LANES), lambda c: (0, c)), pl.BlockSpec((n_rows, LANES), lambda c: (0, c)),
                  pl.BlockSpec((n_rows, LANES), lambda c: (0, jnp.minimum(c, n_extra - 1)))],
        out_specs=[pl.BlockSpec((n_rows, LANES), lambda c: (0, c)),
                   pl.BlockSpec((3, LANES), lambda c: (0, c)), pl.BlockSpec((1, LANES), lambda c: (0, c))],
        out_shape=[jax.ShapeDtypeStruct((n_rows, width), CDT), jax.ShapeDtypeStruct((3, width), F32),
                   jax.ShapeDtypeStruct((1, width), F32)],
        compiler_params=_cp(("parallel",)))(x.arr, conv_w, conv_b, dy_a, dy_b, dxs_extra)


def loss_head(h, target, final_w, m_ctx):
    n_rows, d = h.shape
    tm = min(ROW_TILE, n_rows)
    nb_ctx = m_ctx // tm

    def f(hb, w, tgt):
        err = _rms(hb, w) - tgt
        return 0.5 * jnp.sum(jnp.mean(err * err, axis=-1))

    def body(h_ref, t_ref, w_ref, loss_ref, dh_ref, dw_ref):
        i = pl.program_id(0)

        @pl.when(i < nb_ctx)
        def _():
            dh_ref[...] = jnp.zeros_like(dh_ref)

        @pl.when(i == 0)
        def _():
            loss_ref[...] = jnp.zeros_like(loss_ref)
            dw_ref[...] = jnp.zeros_like(dw_ref)

        @pl.when(i >= nb_ctx)
        def _():
            val, vjp = jax.vjp(lambda hb, w: f(hb, w, t_ref[...]), h_ref[...], w_ref[...])
            dh, dw = vjp(jnp.ones((), F32))
            dh_ref[...] = dh
            dw_ref[...] += dw
            loss_ref[...] += jnp.broadcast_to(val, loss_ref.shape)

    return pl.pallas_call(
        body, name="loss_head", grid=(n_rows // tm,),
        in_specs=[pl.BlockSpec((tm, d), lambda i: (i, 0)),
                  pl.BlockSpec((tm, d), lambda i: (jnp.maximum(i - nb_ctx, 0), 0)), _const_spec((1, d))],
        out_specs=[_const_spec((1, LANES)), pl.BlockSpec((tm, d), lambda i: (i, 0)), _const_spec((1, d))],
        out_shape=[jax.ShapeDtypeStruct((1, LANES), F32), jax.ShapeDtypeStruct((n_rows, d), F32),
                   jax.ShapeDtypeStruct((1, d), F32)],
        compiler_params=_cp(("arbitrary",)))(h, target, final_w)


def adamw(name, w, m, v, g_parts):
    rows, cols = w.shape
    tr = _pick(rows, (256, 128, 64, 32, 16, 8))
    npart = len(g_parts)
    c1 = 1.0 - ADAM_B1 ** ADAM_STEP
    c2 = 1.0 - ADAM_B2 ** ADAM_STEP

    def body(*refs):
        w_ref, m_ref, v_ref = refs[:3]
        g = refs[3][...].astype(F32)
        for r in refs[4:3 + npart]:
            g = g + r[...].astype(F32)
        g_ref, d_ref, nm_ref, nv_ref = refs[3 + npart:]
        nm = ADAM_B1 * m_ref[...] + (1.0 - ADAM_B1) * g
        nv = ADAM_B2 * v_ref[...] + (1.0 - ADAM_B2) * (g * g)
        g_ref[...] = g
        nm_ref[...] = nm
        nv_ref[...] = nv
        d_ref[...] = -ADAM_LR * ((nm / c1) / (jnp.sqrt(nv / c2) + ADAM_EPS) + ADAM_WD * w_ref[...])

    spec = pl.BlockSpec((tr, cols), lambda i: (i, 0))
    return pl.pallas_call(
        body, name=name, grid=(rows // tr,), in_specs=[spec] * (3 + npart), out_specs=[spec] * 4,
        out_shape=[jax.ShapeDtypeStruct((rows, cols), F32)] * 4, compiler_params=_cp(("parallel",)))(w, m, v, *g_parts)


MESH = pl.DeviceIdType.MESH
_HBM = pl.BlockSpec(memory_space=pl.ANY)


def _chip_peers():
    x, y, c = lax.axis_index("x"), lax.axis_index("y"), lax.axis_index("c")
    return x, y, c, [(1 - x, y), (x, 1 - y), (1 - x, 1 - y)]


def _window(ref, kind, chip, rows, cols):
    if kind == "cols":
        return ref.at[:, pl.ds(pl.multiple_of(chip * cols, LANES), cols)]
    if kind == "rows":
        return ref.at[pl.ds(pl.multiple_of(chip * rows, 8), rows), :]
    return ref.at[chip]


def _gathered_shape(kind, rows, cols):
    return {"cols": (rows, 4 * cols), "rows": (4 * rows, cols), "slices": (4, rows, cols)}[kind]


def gather_layers(name, shards, kinds):
    n = len(shards)

    def body(*refs):
        x_refs, o_refs = refs[:n], refs[n:2 * n]
        send_sems, recv_sems, local_sems = refs[2 * n:]
        x, y, c, peers = _chip_peers()
        me = 2 * x + y
        started = []
        for a in range(n):
            _, rows, cols = shards[a].shape
            src = x_refs[a].at[c]
            mine = pltpu.make_async_copy(src, _window(o_refs[a], kinds[a], me, rows, cols), local_sems.at[a])
            mine.start()
            started.append(mine.wait)
            for k, (px, py) in enumerate(peers):
                cp = pltpu.make_async_remote_copy(
                    src_ref=src, dst_ref=_window(o_refs[a], kinds[a], me, rows, cols), send_sem=send_sems.at[3 * a + k],
                    recv_sem=recv_sems.at[3 * a + k], device_id=(px, py, c), device_id_type=MESH)
                cp.start()
                started.append(cp.wait_send)
        for a in range(n):
            _, rows, cols = shards[a].shape
            for k, (px, py) in enumerate(peers):
                pltpu.make_async_remote_copy(
                    src_ref=x_refs[a].at[c], dst_ref=_window(o_refs[a], kinds[a], 2 * px + py, rows, cols),
                    send_sem=send_sems.at[3 * a + k], recv_sem=recv_sems.at[3 * a + k], device_id=(px, py, c),
                    device_id_type=MESH).wait_recv()
        for wait in started:
            wait()

    return pl.pallas_call(
        body, name=name, in_specs=[_HBM] * n, out_specs=[_HBM] * n,
        out_shape=[jax.ShapeDtypeStruct(_gathered_shape(kinds[a], *shards[a].shape[1:]), shards[a].dtype)
                   for a in range(n)],
        scratch_shapes=[pltpu.SemaphoreType.DMA((3 * n,)), pltpu.SemaphoreType.DMA((3 * n,)),
                        pltpu.SemaphoreType.DMA((n,))],
        )(*shards)


def scatter_pieces(name, pieces):
    n = len(pieces)

    def body(*refs):
        p_refs, o_refs = refs[:n], refs[n:2 * n]
        send_sems, recv_sems, local_sems = refs[2 * n:]
        x, y, c, peers = _chip_peers()
        me = 2 * x + y
        started = []
        for a in range(n):
            mine = pltpu.make_async_copy(p_refs[a].at[me], o_refs[a].at[me], local_sems.at[a])
            mine.start()
            started.append(mine.wait)
            for k, (px, py) in enumerate(peers):
                cp = pltpu.make_async_remote_copy(
                    src_ref=p_refs[a].at[2 * px + py], dst_ref=o_refs[a].at[me], send_sem=send_sems.at[3 * a + k],
                    recv_sem=recv_sems.at[3 * a + k], device_id=(px, py, c), device_id_type=MESH)
                cp.start()
                started.append(cp.wait_send)
        for a in range(n):
            for k, (px, py) in enumerate(peers):
                pltpu.make_async_remote_copy(
                    src_ref=p_refs[a].at[me], dst_ref=o_refs[a].at[2 * px + py], send_sem=send_sems.at[3 * a + k],
                    recv_sem=recv_sems.at[3 * a + k], device_id=(px, py, c), device_id_type=MESH).wait_recv()
        for wait in started:
            wait()

    return pl.pallas_call(
        body, name=name, in_specs=[_HBM] * n, out_specs=[_HBM] * n,
        out_shape=[jax.ShapeDtypeStruct(p.shape, p.dtype) for p in pieces],
        scratch_shapes=[pltpu.SemaphoreType.DMA((3 * n,)), pltpu.SemaphoreType.DMA((3 * n,)),
                        pltpu.SemaphoreType.DMA((n,))],
        )(*pieces)


def _pair_step(n_steps, x_ref, land, send_sems, recv_sems, credits, consume):
    x, y, c = lax.axis_index("x"), lax.axis_index("y"), lax.axis_index("c")
    sib = (x, y, 1 - c)
    i = pl.program_id(0)
    slot = i % 2

    @pl.when(i >= 2)
    def _():
        pl.semaphore_wait(credits.at[slot], 1)

    cp = pltpu.make_async_remote_copy(src_ref=x_ref, dst_ref=land.at[slot], send_sem=send_sems.at[slot],
                                      recv_sem=recv_sems.at[slot], device_id=sib, device_id_type=MESH)
    cp.start()
    cp.wait_recv()
    consume(land[slot])

    @pl.when(i < n_steps - 2)
    def _():
        pl.semaphore_signal(credits.at[slot], inc=1, device_id=sib, device_id_type=MESH)

    cp.wait_send()


def _pair_call(name, body, n_steps, in_specs, out_spec, out_shape, blk_shape, dtype, operands, extra_scratch=()):
    grid_spec = pltpu.PrefetchScalarGridSpec(
        num_scalar_prefetch=1, grid=(n_steps,), in_specs=in_specs, out_specs=out_spec,
        scratch_shapes=[pltpu.VMEM((2,) + blk_shape, dtype), pltpu.SemaphoreType.DMA((2,)),
                        pltpu.SemaphoreType.DMA((2,)), pltpu.SemaphoreType.REGULAR((2,)), *extra_scratch])
    return pl.pallas_call(body, name=name, grid_spec=grid_spec, out_shape=out_shape,
                          compiler_params=_cp(("arbitrary",)))(*operands)


def _place():
    return jnp.stack([lax.axis_index("x"), lax.axis_index("y"), lax.axis_index("c")]).astype(jnp.int32)


def _pair_rows(rows, row_bytes):
    for cand in (4096, 2048, 1024, 768, 512, 384, 256, 192, 128, 96, 64, 48, 32, 16):
        if rows % cand == 0 and cand * row_bytes <= PAIR_BLOCK_BYTES:
            return cand
    return _pick(rows, (16, 8))


def exchange_both(name, mine):
    rows, cols = mine.shape
    tr = _pair_rows(rows, cols * mine.dtype.itemsize)
    n_steps = rows // tr

    def body(s_ref, x_ref, o_ref, land, send_sems, recv_sems, credits):
        c = lax.axis_index("c")
        o_ref[c] = x_ref[...]

        def consume(v):
            o_ref[1 - c] = v
        _pair_step(n_steps, x_ref, land, send_sems, recv_sems, credits, consume)

    return _pair_call(name, body, n_steps, [pl.BlockSpec((tr, cols), lambda i, s: (i, 0))],
                      pl.BlockSpec((2, tr, cols), lambda i, s: (0, i, 0)),
                      jax.ShapeDtypeStruct((2, rows, cols), mine.dtype), (tr, cols), mine.dtype, (_place(), mine))


def exchange_add(name, both_layers):
    _, rows, cols = both_layers.shape
    tr = _pair_rows(rows, cols * both_layers.dtype.itemsize)
    nb = rows // tr
    flat = both_layers.reshape(2 * rows, cols)

    def body(s_ref, x_ref, m_ref, o_ref, land, send_sems, recv_sems, credits):
        def consume(v):
            o_ref[...] = (m_ref[...].astype(F32) + v.astype(F32)).astype(o_ref.dtype)
        _pair_step(nb, x_ref, land, send_sems, recv_sems, credits, consume)

    return _pair_call(name, body, nb,
                      [pl.BlockSpec((tr, cols), lambda i, s: ((1 - s[2]) * nb + i, 0)),
                       pl.BlockSpec((tr, cols), lambda i, s: (s[2] * nb + i, 0))],
                      pl.BlockSpec((tr, cols), lambda i, s: (i, 0)), jax.ShapeDtypeStruct((rows, cols), CDT),
                      (tr, cols), flat.dtype, (_place(), flat, flat))


def sum_exchange(name, parts):
    npart, rows, cols = parts.shape
    tr = _pair_rows(rows, cols * 4)
    n_steps = rows // tr

    def body(s_ref, x_ref, o_ref, land, send_sems, recv_sems, credits, mine):
        c = lax.axis_index("c")
        acc = x_ref[0].astype(F32)
        for k in range(1, npart):
            acc = acc + x_ref[k].astype(F32)
        mine[...] = acc
        o_ref[c] = acc

        def consume(v):
            o_ref[1 - c] = v
        _pair_step(n_steps, mine, land, send_sems, recv_sems, credits, consume)

    return _pair_call(name, body, n_steps, [pl.BlockSpec((npart, tr, cols), lambda i, s: (0, i, 0))],
                      pl.BlockSpec((2, tr, cols), lambda i, s: (0, i, 0)),
                      jax.ShapeDtypeStruct((2, rows, cols), F32), (tr, cols), F32, (_place(), parts),
                      extra_scratch=(pltpu.VMEM((tr, cols), F32),))


def allreduce_small(name, buf):
    rows = buf.shape[0]

    def body(x_ref, out_ref, gath, send_sems, recv_sems):
        x, y, c = lax.axis_index("x"), lax.axis_index("y"), lax.axis_index("c")
        me = 4 * x + 2 * y + c
        masks = [(k >> 2 & 1, k >> 1 & 1, k & 1) for k in range(1, 8)]

        def flip(v, bit):
            return 1 - v if bit else v

        sends = []
        for k, (bx, by, bc) in enumerate(masks):
            cp = pltpu.make_async_remote_copy(src_ref=x_ref, dst_ref=gath.at[me], send_sem=send_sems.at[k],
                                              recv_sem=recv_sems.at[k],
                                              device_id=(flip(x, bx), flip(y, by), flip(c, bc)), device_id_type=MESH)
            cp.start()
            sends.append(cp)
        gath[me] = x_ref[...]
        for k, (bx, by, bc) in enumerate(masks):
            px, py, pc = flip(x, bx), flip(y, by), flip(c, bc)
            pltpu.make_async_remote_copy(src_ref=x_ref, dst_ref=gath.at[4 * px + 2 * py + pc],
                                         send_sem=send_sems.at[k], recv_sem=recv_sems.at[k],
                                         device_id=(px, py, pc), device_id_type=MESH).wait_recv()
        for cp in sends:
            cp.wait_send()
        acc = gath[0]
        for d in range(1, 8):
            acc = acc + gath[d]
        out_ref[...] = acc

    return pl.pallas_call(
        body, name=name, in_specs=[pl.BlockSpec(memory_space=pltpu.VMEM)],
        out_specs=pl.BlockSpec(memory_space=pltpu.VMEM), out_shape=jax.ShapeDtypeStruct(buf.shape, F32),
        scratch_shapes=[pltpu.VMEM((8, rows, LANES), F32), pltpu.SemaphoreType.DMA((7,)),
                        pltpu.SemaphoreType.DMA((7,))],
        )(buf)


def _pack_flat(arrs, dtype, width, row_mult=8):
    flat = jnp.concatenate([a.reshape(-1).astype(dtype) for a in arrs])
    pad = (-flat.shape[0]) % (row_mult * width)
    if pad:
        flat = jnp.concatenate([flat, jnp.zeros((pad,), dtype)])
    return flat.reshape(-1, width)


def _unpack_flat(buf, shapes):
    flat = buf.reshape(-1)
    out, off = [], 0
    for s in shapes:
        n = math.prod(s)
        out.append(flat[off:off + n].reshape(s))
        off += n
    return out


def _in_to_padded(w):
    parts = []
    for name in IN_NEW_ORDER:
        _, width, o_off, o_w = IN_LAYOUT[name]
        parts.append(w[..., o_off:o_off + o_w])
        if o_w < width:
            parts.append(jnp.zeros(w.shape[:-1] + (width - o_w,), w.dtype))
    used = sum(IN_LAYOUT[n][1] for n in IN_NEW_ORDER)
    parts.append(jnp.zeros(w.shape[:-1] + (IN_PAD - used,), w.dtype))
    return jnp.concatenate(parts, axis=-1)


def _in_from_padded(g):
    parts = []
    for name in IN_ORIG_ORDER:
        off, _, _, o_w = IN_LAYOUT[name]
        parts.append(g[..., off:off + o_w])
    return jnp.concatenate(parts, axis=-1)


def _pcol(p, name):
    off, width, _, _ = IN_LAYOUT[name]
    return Cols(p, off, width)


def _lane_pad(v, width=LANES):
    v = v.reshape(-1)
    return jnp.concatenate([v, jnp.zeros((width - v.shape[0],), v.dtype)]).reshape(1, width)


def _f_norm_mod(h, sh, sc, w):
    return (_rms(h, w) * (1.0 + sc) + sh,)


def _f_norm_mod_thru(h, sh, sc, w):
    return h, _rms(h, w) * (1.0 + sc) + sh


def _f_attn_prep(qraw, kraw, vraw, cq, sq, ck, sk, qw, kw, gq, gk):
    q = qraw * lax.rsqrt(_group_mean(qraw * qraw, gq) + NORM_EPS) * qw
    q = _rope32(q, cq, sq) * (ATTN_HD ** -0.5)
    k = kraw * lax.rsqrt(_group_mean(kraw * kraw, gk) + NORM_EPS) * kw
    return q, _rope32(k, ck, sk), vraw


def _f_ssd_finish(yf, yb, xs, z, d_exp, nw):
    y = (yf + yb + d_exp * xs) * (z * jax.nn.sigmoid(z))
    return (_rms(y, nw),)


def _f_ret_prep(rq, rk, cos_full, sin_signed):
    return _rope64(rq, cos_full, sin_signed), _rope64(rk, cos_full, sin_signed) * (RET_DK ** -0.5)


def _f_ret_finish(yf, yb, g, gw):
    y = yf + yb
    outs = []
    for h in range(RET_HEADS):
        yh = y[:, h * RET_DK:(h + 1) * RET_DK]
        yc = yh - jnp.mean(yh, axis=-1, keepdims=True)
        outs.append(yc * lax.rsqrt(jnp.mean(yc * yc, axis=-1, keepdims=True) + NORM_EPS))
    return (jnp.concatenate(outs, axis=1) * gw * (g * jax.nn.sigmoid(g)),)


def _f_merge(p0, p1, p2, g0, g1, g2):
    return (jax.nn.sigmoid(g0) * p0 + jax.nn.sigmoid(g1) * p1 + jax.nn.sigmoid(g2) * p2,)


def _f_mid(h, mix, g1, sh2, sc2, w2):
    h_mid = h + g1 * mix
    return h_mid, _rms(h_mid, w2) * (1.0 + sc2) + sh2


def _f_sqrelu(a):
    r = jnp.maximum(a, 0.0)
    return (r * r,)


def _f_residual(h_mid, o, g2):
    return (h_mid + g2 * o,)


def _f_silu(x):
    return (x * jax.nn.sigmoid(x),)


def _f_bias(x, b):
    return (x + b,)


def _ssd_rows(xbc, p):
    rows = [Cols(xbc, LANES * k, LANES) for k in range(4)]
    rows += [Cols(xbc, 512 + LANES * g, LANES) for g in range(2)]
    rows += [Cols(xbc, 768 + LANES * g, LANES) for g in range(2)]
    return rows + [_pcol(p, "dt")]


def _ret_rows(rq, rk, p):
    off_v = IN_LAYOUT["rv"][0]
    return ([Cols(rq, LANES * h, LANES) for h in range(4)] + [Cols(rk, LANES * h, LANES) for h in range(4)]
            + [Cols(p, off_v + LANES * h, LANES) for h in range(4)])


def layer_fwd(li, h, mod, lw, tabs, m_ctx):
    t = h.shape[0]
    nb = m_ctx // min(ROW_TILE, t)
    sh1, sc1, g1, sh2, sc2, g2 = mod
    nm = lambda s: f"l{li}_{s}"
    sv = {}
    (u,) = rowwise_fwd(nm("norm1"), _f_norm_mod, [h], [sh1, sc1], [lw["norm1_w"]], [(D_MODEL, CDT)], t, nb)
    p = mm(nm("in_proj"), u, lw["w_in"], F32)
    q, k, v = rowwise_fwd(
        nm("attn_prep"), _f_attn_prep,
        [_pcol(p, "q"), _pcol(p, "k"), _pcol(p, "v"), tabs["cq"], tabs["sq"], tabs["ck"], tabs["sk"]], [],
        [lw["qw"], lw["kw"], tabs["gq"], tabs["gk"]], [(512, CDT), (128, CDT), (128, CDT)], t, nb)
    tq = min(ATTN_TQ, m_ctx)
    kk, vv = _split_kv(k), _split_kv(v)
    attn_o, qT, oT, lse = attn_fwd(nm("attn"), q, kk, vv.transpose(0, 2, 1), m_ctx, tq)

    xbc = conv_fwd(_pcol(p, "xbc"), lw["conv_w"], lw["conv_b"], m_ctx)
    ssd_sh = [lw["dt_bias"], lw["a_log"]]
    yf, yb, sf, sb = scan_fwd(nm("ssd"), _make_ssd_chunk, _ssd_rows(xbc, p), ssd_sh, 4, 512, t, m_ctx)
    (ssd_o,) = rowwise_fwd(nm("ssd_fin"), _f_ssd_finish, [yf, yb, Cols(xbc, 0, 512), _pcol(p, "z")], [],
                           [lw["d_exp"], lw["ssd_nw"]], [(512, CDT)], t, nb)

    rq, rk = rowwise_fwd(nm("ret_prep"), _f_ret_prep, [_pcol(p, "rq"), _pcol(p, "rk"), tabs["rc"], tabs["rs"]],
                         [], [], [(512, F32), (512, F32)], t, nb)
    rf, rb, rsf, rsb = scan_fwd(nm("ret"), _make_ret_chunk, _ret_rows(rq, rk, p), [lw["ret_lg"]], 4, 512, t, m_ctx)
    (ret_o,) = rowwise_fwd(nm("ret_fin"), _f_ret_finish, [rf, rb, _pcol(p, "rg")], [], [lw["ret_gw"]],
                           [(512, CDT)], t, nb)

    pbs = [mm(nm(f"branch{b}"), br, lw["w_branch"][b], CDT) for b, br in enumerate((attn_o, ssd_o, ret_o))]
    gl = [Cols(p, 1024 * b, 1024) for b in range(3)]
    (merged,) = rowwise_fwd(nm("merge"), _f_merge, pbs + gl, [], [], [(D_MODEL, CDT)], t, nb)
    mix = mm(nm("out_proj"), merged, lw["w_out"], F32)
    h_mid, vv2 = rowwise_fwd(nm("mid"), _f_mid, [h, mix], [g1, sh2, sc2], [lw["norm2_w"]],
                             [(D_MODEL, F32), (D_MODEL, CDT)], t, nb)
    a = mm(nm("mlp1"), vv2, lw["w_mlp1"], CDT)
    (hh,) = rowwise_fwd(nm("sqrelu"), _f_sqrelu, [a], [], [], [(a.shape[1], CDT)], t, nb)
    o = mm(nm("mlp2"), hh, lw["w_mlp2"], F32)
    (h_out,) = rowwise_fwd(nm("resid"), _f_residual, [h_mid, o], [g2], [], [(D_MODEL, F32)], t, nb)
    sv.update(h=h, u=u, p=p, qT=qT, kk=kk, vv=vv, oT=oT, lse=lse, attn_o=attn_o, xbc=xbc, yf=yf, yb=yb,
              sf=sf, sb=sb, ssd_o=ssd_o, rq=rq, rk=rk, rf=rf, rb=rb, rsf=rsf, rsb=rsb, ret_o=ret_o, pbs=pbs,
              merged=merged, mix=mix, h_mid=h_mid, v=vv2, a=a, hh=hh, o=o)
    return h_out, sv


def layer_bwd(li, dh_out, sv, mod, lw, tabs, m_ctx):
    t = dh_out.shape[0]
    nb = m_ctx // min(ROW_TILE, t)
    sh1, sc1, g1, sh2, sc2, g2 = mod
    nm = lambda s: f"l{li}_{s}_bwd"
    gw = {}
    p = sv["p"]
    (do,), (dg2,), _ = rowwise_bwd(nm("resid"), _f_residual, [sv["h_mid"], sv["o"]], [g2], [], [dh_out],
                                   [False, True], [], [CDT], t, nb)
    dhh = mm(nm("mlp2_dx"), do, lw["w_mlp2"], CDT, transpose_b=True)
    gw["w_mlp2"] = mm_tn(nm("mlp2_dw"), sv["hh"], do, CDT, ("rows", lw["w_mlp2"].shape[0] // 4))
    (da,), _, _ = rowwise_bwd(nm("sqrelu"), _f_sqrelu, [sv["a"]], [], [], [dhh], [True], [], [CDT], t, nb)
    dv = mm(nm("mlp1_dx"), da, lw["w_mlp1"], F32, transpose_b=True)
    gw["w_mlp1"] = mm_tn(nm("mlp1_dw"), sv["v"], da, CDT, ("cols", lw["w_mlp1"].shape[1] // 4))
    (dh_a, dmix), (dg1, dsh2, dsc2), (gw["norm2_w"],) = rowwise_bwd(
        nm("mid"), _f_mid, [sv["h"], sv["mix"]], [g1, sh2, sc2], [lw["norm2_w"]], [dh_out, dv],
        [True, True], [True], [F32, CDT], t, nb)
    dmerged = mm(nm("out_dx"), dmix, lw["w_out"], CDT, transpose_b=True)
    gw["w_out"] = mm_tn(nm("out_dw"), sv["merged"], dmix, CDT, ("rows", lw["w_out"].shape[0] // 4))
    gl = [Cols(p, 1024 * b, 1024) for b in range(3)]
    dmg, _, _ = rowwise_bwd(nm("merge"), _f_merge, sv["pbs"] + gl, [], [], [dmerged], [True] * 6, [], [CDT] * 6,
                            t, nb)
    dpb, dgl = dmg[:3], dmg[3:]
    brs = (sv["attn_o"], sv["ssd_o"], sv["ret_o"])
    d_attn_o = mm(nm("branch0_dx"), dpb[0], lw["w_branch"][0], CDT, transpose_b=True)
    d_ssd_o = mm(nm("branch1_dx"), dpb[1], lw["w_branch"][1], F32, transpose_b=True)
    d_ret_o = mm(nm("branch2_dx"), dpb[2], lw["w_branch"][2], F32, transpose_b=True)
    n_loc = lw["w_branch"].shape[2] // 4
    gw["w_branch"] = jnp.stack([mm_tn(nm(f"branch{b}_dw"), brs[b], dpb[b], CDT, ("cols", n_loc)) for b in range(3)],
                               axis=1).reshape(4, -1, n_loc)
    tq = min(ATTN_TQ, m_ctx)
    dq_rows, dk_s, dv_s = attn_bwd(nm("attn"), sv["qT"], d_attn_o, sv["oT"], sv["lse"], sv["kk"],
                                   sv["kk"].transpose(0, 2, 1), sv["vv"], m_ctx)
    (dq_raw, dk_raw, dv_raw), _, (gw["qw"], gw["kw"]) = rowwise_bwd(
        nm("attn_prep"), _f_attn_prep,
        [_pcol(p, "q"), _pcol(p, "k"), _pcol(p, "v"), tabs["cq"], tabs["sq"], tabs["ck"], tabs["sk"]], [],
        [lw["qw"], lw["kw"], tabs["gq"], tabs["gk"]],
        [dq_rows, _merge_kv(dk_s), _merge_kv(dv_s)],
        [True, True, True, False, False, False, False], [True, True, False, False], [CDT] * 3, t, nb)
    (dy_ssd, dxs_fin, dz), _, (gw["d_exp"], gw["ssd_nw"]) = rowwise_bwd(
        nm("ssd_fin"), _f_ssd_finish, [sv["yf"], sv["yb"], Cols(sv["xbc"], 0, 512), _pcol(p, "z")], [],
        [lw["d_exp"], lw["ssd_nw"]], [d_ssd_o], [True, False, True, True], [True, True], [F32, F32, CDT], t, nb)
    ssd_sh = [lw["dt_bias"], lw["a_log"]]
    post_ssd = lambda d: [jnp.concatenate(d[0:8], axis=1), d[8]]
    (dxbc_f, ddt_f), dsh_f = scan_bwd(nm("ssd_f"), _make_ssd_chunk, _ssd_rows(sv["xbc"], p), ssd_sh, (sv["sf"],),
                                      dy_ssd, post_ssd, [(1024, F32), (LANES, F32)], 4, t, m_ctx, dirs=(0,))
    (dxbc_b, ddt_b), dsh_b = scan_bwd(nm("ssd_b"), _make_ssd_chunk, _ssd_rows(sv["xbc"], p), ssd_sh, (sv["sb"],),
                                      dy_ssd, post_ssd, [(1024, F32), (LANES, F32)], 4, t, m_ctx, dirs=(1,))
    gw["dt_bias"], gw["a_log"] = dsh_f[0] + dsh_b[0], dsh_f[1] + dsh_b[1]
    ddt = (ddt_f + ddt_b).astype(CDT)
    dxbc_raw, gw["conv_w"], gw["conv_b"] = conv_bwd(_pcol(p, "xbc"), lw["conv_w"], lw["conv_b"], dxbc_f, dxbc_b,
                                                    dxs_fin, m_ctx)
    (dy_ret, drg), _, (gw["ret_gw"],) = rowwise_bwd(
        nm("ret_fin"), _f_ret_finish, [sv["rf"], sv["rb"], _pcol(p, "rg")], [], [lw["ret_gw"]], [d_ret_o],
        [True, False, True], [True], [F32, CDT], t, nb)
    post_ret = lambda d: [jnp.concatenate(d[0:4], axis=1), jnp.concatenate(d[4:8], axis=1),
                          jnp.concatenate(d[8:12], axis=1)]
    rrows = _ret_rows(sv["rq"], sv["rk"], p)
    (dq_f, dk_f, dv_f), (dq_b, dk_b, dv_b), (gw["ret_lg"],) = scan_bwd(
        nm("ret"), _make_ret_chunk, rrows, [lw["ret_lg"]], (sv["rsf"], sv["rsb"]), dy_ret, post_ret,
        [(512, F32)] * 3, 4, t, m_ctx)
    drv = (dv_f + dv_b).astype(CDT)
    (drq, drk), _, _ = rowwise_bwd(nm("ret_prep"), _f_ret_prep,
                                   [_pcol(p, "rq"), _pcol(p, "rk"), tabs["rc"], tabs["rs"]], [], [],
                                   [(dq_f, dq_b), (dk_f, dk_b)], [True, True, False, False], [], [CDT, CDT], t, nb)
    pieces = {"gates": None, "xbc": dxbc_raw, "q": dq_raw, "z": dz, "rq": drq, "rk": drk, "rv": drv, "rg": drg,
              "k": dk_raw, "v": dv_raw, "dt": ddt}
    cols = list(dgl) + [pieces[n] for n in IN_NEW_ORDER[1:]]
    used = sum(c.shape[1] for c in cols)
    cols.append(jnp.zeros((t, IN_PAD - used), CDT))
    dp = jnp.concatenate(cols, axis=1)
    du = mm(nm("in_dx"), dp, lw["w_in"], F32, transpose_b=True)
    gw["w_in"] = mm_tn(nm("in_dw"), sv["u"], dp, CDT)
    (dh_in,), (dsh1, dsc1), (gw["norm1_w"],) = rowwise_bwd(
        nm("norm1"), _f_norm_mod_thru, [sv["h"]], [sh1, sc1], [lw["norm1_w"]], [dh_a, du], [True], [True], [F32],
        t, nb)
    return dh_in, [dsh1, dsc1, dg1, dsh2, dsc2, dg2], gw


def _rope_tables(n_lat, m_ctx):
    rows = n_lat // GRID_W
    row = jnp.repeat(jnp.arange(rows, dtype=F32), GRID_W)
    col = jnp.tile(jnp.arange(GRID_W, dtype=F32), rows)
    nfreq = ATTN_HD // 4
    inv = ROPE_THETA ** (-jnp.arange(nfreq, dtype=F32) / nfreq)
    ang = jnp.concatenate([row[:, None] * inv, col[:, None] * inv], axis=-1)
    cos = jnp.concatenate([jnp.ones((m_ctx, ATTN_HD // 2), F32), jnp.cos(ang)], axis=0)
    sin = jnp.concatenate([jnp.zeros((m_ctx, ATTN_HD // 2), F32), jnp.sin(ang)], axis=0)
    c64 = jnp.concatenate([cos, cos], axis=1)
    s64 = jnp.concatenate([-sin, sin], axis=1)
    pos = jnp.arange(m_ctx + n_lat, dtype=F32)
    inv_r = ROPE_THETA ** (-jnp.linspace(0.0, 1.0, RET_DK // 2, dtype=F32))
    ang_r = pos[:, None] * inv_r
    rc = jnp.concatenate([jnp.cos(ang_r)] * 2, axis=1)
    rs = jnp.concatenate([-jnp.sin(ang_r), jnp.sin(ang_r)], axis=1)
    return dict(cq=jnp.tile(c64, (1, 8)), sq=jnp.tile(s64, (1, 8)), ck=jnp.tile(c64, (1, 2)), sk=jnp.tile(s64, (1, 2)),
                rc=jnp.tile(rc, (1, 4)), rs=jnp.tile(rs, (1, 4)), gq=_group_matrix(512, ATTN_HD),
                gk=_group_matrix(128, ATTN_HD))


def _layer_weights(full, small, layer):
    return dict(
        w_in=full["w_in"][layer], w_branch=full["w_branch"][layer], w_out=full["w_out"][layer],
        w_mlp1=full["w_mlp1"][layer], w_mlp2=full["w_mlp2"][layer],
        norm1_w=small["norm1_w"][layer][None], norm2_w=small["norm2_w"][layer][None],
        qw=jnp.tile(small["attn_q_norm"][layer], 8)[None], kw=jnp.tile(small["attn_k_norm"][layer], 2)[None],
        conv_w=small["ssd_conv_w"][layer], conv_b=small["ssd_conv_b"][layer][None],
        dt_bias=_lane_pad(small["ssd_dt_bias"][layer]), a_log=_lane_pad(small["ssd_a_log"][layer]),
        d_exp=jnp.repeat(small["ssd_d"][layer], SSD_HD)[None], ssd_nw=small["ssd_norm_w"][layer][None],
        ret_lg=_lane_pad(small["ret_log_decay"][layer]), ret_gw=small["ret_gn_w"][layer][None])


def local_step(x, c, ctx, full, small, loss_target):
    n_lat, d = x.shape
    m_ctx = ctx.shape[0]
    t = n_lat + m_ctx
    depth = small["norm1_w"].shape[0]
    tabs = _rope_tables(n_lat, m_ctx)
    h = jnp.concatenate([ctx, x], axis=0)
    cc = jnp.concatenate([small["c_ctx"][None], c, jnp.zeros((COND_ROWS - 2, d), F32)], axis=0)
    (scc,) = rowwise_fwd("cond_silu", _f_silu, [cc], [], [], [(d, CDT)], COND_ROWS, 0)
    mods, saved, lws = [], [], []
    for layer in range(depth):
        lw = _layer_weights(full, small, layer)
        mod_raw = mm(f"l{layer}_mod", scc, full["w_mod"][layer], F32)
        (mod8,) = rowwise_fwd(f"l{layer}_mod_bias", _f_bias, [mod_raw], [], [small["b_mod"][layer][None]],
                              [(6 * d, F32)], COND_ROWS, 0)
        mod = [mod8[0:2, k * d:(k + 1) * d].reshape(2, 1, d) for k in range(6)]
        h, sv = layer_fwd(layer, h, mod, lw, tabs, m_ctx)
        mods.append(mod)
        saved.append(sv)
        lws.append(lw)
    loss, dh, d_final = loss_head(h, loss_target, small["final_norm_w"][None], m_ctx)

    gbig = {k: [None] * depth for k in BIG}
    gs = {k: [None] * depth for k in SMALL if k not in ("c_ctx", "final_norm_w")}
    d_scc = None
    for layer in reversed(range(depth)):
        lw = lws[layer]
        dh, dmod, gw = layer_bwd(layer, dh, saved[layer], mods[layer], lw, tabs, m_ctx)
        dmod8 = jnp.concatenate([jnp.concatenate([g_.reshape(2, d) for g_ in dmod], axis=1),
                                 jnp.zeros((COND_ROWS - 2, 6 * d), F32)], axis=0)
        (dmod_c,), _, (db_mod,) = rowwise_bwd(f"l{layer}_mod_bias_bwd", _f_bias, [dmod8], [],
                                              [small["b_mod"][layer][None]], [dmod8], [True], [True], [CDT], COND_ROWS, 0)
        gbig["w_mod"][layer] = mm_tn(f"l{layer}_mod_dw", scc, dmod_c, CDT, ("cols", 6 * d // 4))
        part = mm(f"l{layer}_mod_dx", dmod_c, full["w_mod"][layer], F32, transpose_b=True)
        d_scc = part if d_scc is None else d_scc + part
        g_in = _in_from_padded(gw["w_in"])
        gbig["w_in"][layer] = g_in.reshape(d, 4, g_in.shape[1] // 4).transpose(1, 0, 2)
        for k in ("w_branch", "w_out", "w_mlp1", "w_mlp2"):
            gbig[k][layer] = gw[k]
        gs["b_mod"][layer] = db_mod.reshape(-1)
        gs["norm1_w"][layer] = gw["norm1_w"].reshape(-1)
        gs["norm2_w"][layer] = gw["norm2_w"].reshape(-1)
        gs["attn_q_norm"][layer] = gw["qw"].reshape(8, ATTN_HD).sum(0)
        gs["attn_k_norm"][layer] = gw["kw"].reshape(2, ATTN_HD).sum(0)
        gs["ssd_conv_w"][layer] = gw["conv_w"]
        gs["ssd_conv_b"][layer] = gw["conv_b"].reshape(-1)
        gs["ssd_dt_bias"][layer] = gw["dt_bias"][0, :16].reshape(2, 8)
        gs["ssd_a_log"][layer] = gw["a_log"][0, :16].reshape(2, 8)
        gs["ssd_d"][layer] = gw["d_exp"].reshape(SSD_HEADS, SSD_HD).sum(1)
        gs["ssd_norm_w"][layer] = gw["ssd_nw"].reshape(-1)
        gs["ret_log_decay"][layer] = gw["ret_lg"][0, :8].reshape(2, 4)
        gs["ret_gn_w"][layer] = gw["ret_gw"].reshape(-1)
    (d_cc,), _, _ = rowwise_bwd("cond_silu_bwd", _f_silu, [cc], [], [], [d_scc], [True], [], [F32], COND_ROWS, 0)
    g_small = {k: jnp.stack(v) for k, v in gs.items()}
    g_small["c_ctx"] = d_cc[0]
    g_small["final_norm_w"] = d_final.reshape(-1)
    g_big = {k: jnp.stack(v) for k, v in gbig.items()}
    return loss, dh[m_ctx:], g_big, g_small


def kernel(x, c, ctx, c_ctx, w_mod, b_mod, norm1_w, norm2_w, w_in, attn_q_norm, attn_k_norm, ssd_conv_w, ssd_conv_b, ssd_dt_bias, ssd_a_log, ssd_d, ssd_norm_w, ret_log_decay, ret_gn_w, w_branch, w_out, w_mlp1, w_mlp2, final_norm_w, loss_target, m_c_ctx, m_w_mod, m_b_mod, m_norm1_w, m_norm2_w, m_w_in, m_attn_q_norm, m_attn_k_norm, m_ssd_conv_w, m_ssd_conv_b, m_ssd_dt_bias, m_ssd_a_log, m_ssd_d, m_ssd_norm_w, m_ret_log_decay, m_ret_gn_w, m_w_branch, m_w_out, m_w_mlp1, m_w_mlp2, m_final_norm_w, v_c_ctx, v_w_mod, v_b_mod, v_norm1_w, v_norm2_w, v_w_in, v_attn_q_norm, v_attn_k_norm, v_ssd_conv_w, v_ssd_conv_b, v_ssd_dt_bias, v_ssd_a_log, v_ssd_d, v_ssd_norm_w, v_ret_log_decay, v_ret_gn_w, v_w_branch, v_w_out, v_w_mlp1, v_w_mlp2, v_final_norm_w):
    env = dict(locals())
    w_loc = {k: env[k] for k in WEIGHTS}
    m_loc = {k: env["m_" + k] for k in WEIGHTS}
    v_loc = {k: env["v_" + k] for k in WEIGHTS}
    chip = 2 * lax.axis_index("x") + lax.axis_index("y")
    core = lax.axis_index("c")

    depth = w_loc["w_mod"].shape[0]
    assert depth == 2, "the exchanges split the layers between a chip's two cores"
    shards = [w_loc[k].astype(CDT).reshape(depth, -1, w_loc[k].shape[-1]) for k in BIG]
    mine = gather_layers("gather_weights", shards, [BIG_KIND[k] for k in BIG])
    full = {}
    for k, arr in zip(BIG, mine):
        both = exchange_both("share_" + k, arr.reshape(-1, arr.shape[-1]))
        if k == "w_in":
            both = both.reshape(depth, 4, -1, both.shape[-1]).transpose(0, 2, 1, 3)
            both = _in_to_padded(both.reshape(depth, both.shape[1], -1))
        full[k] = both.reshape((depth,) + w_loc[k].shape[1:-1] + (-1,)) if BIG_KIND[k] == "cols" else \
            both.reshape((depth,) + w_loc[k].shape[1:-2] + (-1, w_loc[k].shape[-1])) if BIG_KIND[k] == "rows" else both

    cw = w_loc["ssd_conv_w"]
    cw_w = cw.shape[-1]
    placed = lax.dynamic_update_slice(jnp.zeros(cw.shape[:-1] + (4 * cw_w,), F32),
                                      cw * (core == 0).astype(F32), (0, 0, chip * cw_w))
    conv_full = _unpack_flat(allreduce_small("gather_conv_w", _pack_flat([placed], F32, LANES)), [placed.shape])[0]
    small = {k: w_loc[k] for k in SMALL}
    small["ssd_conv_w"] = conv_full

    loss_l, grad_x, g_big, g_small = local_step(x[0], c, ctx[0], full, small, loss_target[0])

    small_shapes = [g_small[k].shape for k in SMALL] + [(LANES,)]
    summed = _unpack_flat(allreduce_small("reduce_small", _pack_flat([g_small[k] for k in SMALL] + [loss_l], F32, LANES)),
                          small_shapes)
    gsum = dict(zip(SMALL, summed[:-1]))
    loss = summed[-1][0]
    gsum["ssd_conv_w"] = lax.dynamic_slice(gsum["ssd_conv_w"], (0, 0, chip * cw_w), cw.shape)

    pair = []
    for k in BIG:
        _, _, rows, cols = g_big[k].shape
        pair.append(exchange_add("pair_" + k, g_big[k].reshape(depth, 4 * rows, cols)).reshape(4, rows, cols))
    landed = scatter_pieces("scatter_grads", pair)
    g_sum = [sum_exchange("sum_" + k, parts) for k, parts in zip(BIG, landed)]

    grads, deltas, new_m, new_v = {}, {}, {}, {}
    for i, k in enumerate(BIG):
        shp = w_loc[k].shape
        two_d = lambda a, shp=shp: a.reshape(-1, shp[-1])
        res = adamw("adamw_" + k, two_d(w_loc[k]), two_d(m_loc[k]), two_d(v_loc[k]), [two_d(g_sum[i])])
        grads[k], deltas[k], new_m[k], new_v[k] = [r.reshape(shp) for r in res]
    small_loc_shapes = [w_loc[k].shape for k in SMALL]
    res = adamw("adamw_small", _pack_flat([w_loc[k] for k in SMALL], F32, LANES),
                _pack_flat([m_loc[k] for k in SMALL], F32, LANES), _pack_flat([v_loc[k] for k in SMALL], F32, LANES),
                [_pack_flat([gsum[k] for k in SMALL], F32, LANES)])
    for dst, r in zip((grads, deltas, new_m, new_v), res):
        dst.update(dict(zip(SMALL, _unpack_flat(r, small_loc_shapes))))

    return (loss, grad_x[None], *[grads[k] for k in WEIGHTS], *[deltas[k] for k in WEIGHTS],
            *[new_m[k] for k in WEIGHTS], *[new_v[k] for k in WEIGHTS])
```

```python
import functools
import math
from typing import NamedTuple

import jax
import jax.numpy as jnp
from jax import lax
from jax.experimental import pallas as pl
from jax.experimental.pallas import tpu as pltpu

F32 = jnp.float32
CDT = jnp.bfloat16
NORM_EPS = 1e-6
ROPE_THETA = 10000.0
GRID_W = 64
D_MODEL = 1024
ATTN_HEADS, ATTN_KV, ATTN_HD = 8, 2, 64
SSD_HEADS, SSD_HD, SSD_STATE = 8, 64, 128
RET_HEADS, RET_DK = 4, 128
CHUNK = 128
ROW_TILE = 256
MM_ROWS = 768
MM_VMEM_BUDGET = 44 * 1024 * 1024
ATTN_TQ, ATTN_TK = 256, 256
ATTN_TK_BWD = 2048
LOG2E, LN2 = 1.4426950408889634, 0.6931471805599453
ATTN_TK_FWD = 2048
LANES = 128
PAIR_BLOCK_BYTES = 2 * 1024 * 1024
COND_ROWS = 16
VMEM_LIMIT = 56 * 1024 * 1024

ADAM_LR, ADAM_B1, ADAM_B2, ADAM_EPS, ADAM_WD, ADAM_STEP = 0.001, 0.9, 0.999, 1e-08, 0.01, 10

IN_LAYOUT = {
    "gates": (0, 3072, 4368, 3072), "xbc": (3072, 1024, 1280, 1024), "q": (4096, 512, 0, 512),
    "z": (4608, 512, 768, 512), "rq": (5120, 512, 2320, 512), "rk": (5632, 512, 2832, 512),
    "rv": (6144, 512, 3344, 512), "rg": (6656, 512, 3856, 512), "k": (7168, 128, 512, 128),
    "v": (7296, 128, 640, 128), "dt": (7424, 128, 2304, 16),
}
IN_PAD = 7680
IN_ORIG_ORDER = ("q", "k", "v", "z", "xbc", "dt", "rq", "rk", "rv", "rg", "gates")
IN_NEW_ORDER = ("gates", "xbc", "q", "z", "rq", "rk", "rv", "rg", "k", "v", "dt")

BIG = ("w_mod", "w_in", "w_branch", "w_out", "w_mlp1", "w_mlp2")
BIG_KIND = {"w_mod": "cols", "w_in": "slices", "w_branch": "cols", "w_out": "rows", "w_mlp1": "cols", "w_mlp2": "rows"}
SMALL = ("c_ctx", "b_mod", "norm1_w", "norm2_w", "attn_q_norm", "attn_k_norm", "ssd_conv_w", "ssd_conv_b",
         "ssd_dt_bias", "ssd_a_log", "ssd_d", "ssd_norm_w", "ret_log_decay", "ret_gn_w", "final_norm_w")
WEIGHTS = ("c_ctx", "w_mod", "b_mod", "norm1_w", "norm2_w", "w_in", "attn_q_norm", "attn_k_norm", "ssd_conv_w",
           "ssd_conv_b", "ssd_dt_bias", "ssd_a_log", "ssd_d", "ssd_norm_w", "ret_log_decay", "ret_gn_w",
           "w_branch", "w_out", "w_mlp1", "w_mlp2", "final_norm_w")


def _cp(sem):
    return pltpu.CompilerParams(dimension_semantics=sem, vmem_limit_bytes=VMEM_LIMIT)


class Cols(NamedTuple):
    arr: jax.Array
    off: int
    width: int


def _width(item):
    return item.width if isinstance(item, Cols) else item.shape[1]


def _row_in(item, rows, imap=None):
    imap = imap or (lambda i: i)
    if isinstance(item, Cols):
        assert item.off % item.width == 0
        blk = item.off // item.width
        return item.arr, pl.BlockSpec((rows, item.width), lambda i, blk=blk: (imap(i), blk))
    return item, pl.BlockSpec((rows, item.shape[1]), lambda i: (imap(i), 0))


def _const_spec(shape):
    return pl.BlockSpec(shape, lambda *_: (0,) * len(shape))


def _mxu(a, b, dims=(((1,), (0,)), ((), ()))):
    return lax.dot_general(a.astype(CDT), b.astype(CDT), dims, preferred_element_type=F32)


_NT = (((1,), (1,)), ((), ()))
_TN = (((0,), (0,)), ((), ()))


@jax.custom_vjp
def _softplus(x):
    return jnp.maximum(x, 0.0) + jnp.log1p(jnp.exp(-jnp.abs(x)))


def _softplus_fwd(x):
    return _softplus(x), x


def _softplus_bwd(x, g):
    return (g * jax.nn.sigmoid(x),)


_softplus.defvjp(_softplus_fwd, _softplus_bwd)


def _group_mean_impl(x, gmat):
    hi = x.astype(CDT)
    lo = (x - hi.astype(F32)).astype(CDT)
    return (jnp.dot(hi, gmat, preferred_element_type=F32) + jnp.dot(lo, gmat, preferred_element_type=F32))


@jax.custom_vjp
def _group_mean(x, gmat):
    return _group_mean_impl(x, gmat)


def _group_mean_fwd(x, gmat):
    return _group_mean_impl(x, gmat), gmat


def _group_mean_bwd(gmat, g):
    return _group_mean_impl(g, gmat), jnp.zeros_like(gmat)


_group_mean.defvjp(_group_mean_fwd, _group_mean_bwd)


def _group_matrix(width, group):
    r = jnp.arange(width) // group
    return jnp.where(r[:, None] == r[None, :], 1.0 / group, 0.0).astype(CDT)


def _make_rope(half):
    def partner(x):
        w = x.shape[1]
        lane = lax.broadcasted_iota(jnp.int32, x.shape, 1)
        first = (lane % (2 * half)) < half
        return jnp.where(first, pltpu.roll(x, w - half, axis=1), pltpu.roll(x, half, axis=1))

    def impl(x, cos_full, sin_signed):
        return x * cos_full + partner(x) * sin_signed

    @jax.custom_vjp
    def rope(x, cos_full, sin_signed):
        return impl(x, cos_full, sin_signed)

    def fwd(x, cos_full, sin_signed):
        return impl(x, cos_full, sin_signed), (cos_full, sin_signed)

    def bwd(res, g):
        cos_full, sin_signed = res
        return impl(g, cos_full, -sin_signed), jnp.zeros_like(cos_full), jnp.zeros_like(sin_signed)

    rope.defvjp(fwd, bwd)
    return rope


_rope32 = _make_rope(32)
_rope64 = _make_rope(64)


def _rms(x, w):
    return x * lax.rsqrt(jnp.mean(x * x, axis=-1, keepdims=True) + NORM_EPS) * w


def _col(v, lane_index):
    lane = lax.broadcasted_iota(jnp.int32, v.shape, 1)
    return jnp.sum(jnp.where(lane == lane_index, v, 0.0), axis=1, keepdims=True)


def _typed_spec(width, nb_ctx):
    return pl.BlockSpec((None, 1, width), lambda i: (jnp.where(i >= nb_ctx, 1, 0), 0, 0))


def rowwise_fwd(name, f, rows, typed, shared, outs, n_rows, nb_ctx, tm=ROW_TILE):
    tm = min(tm, n_rows)
    nin = len(rows) + len(typed) + len(shared)

    def body(*refs):
        res = f(*[r[...] for r in refs[:nin]])
        for o_ref, o in zip(refs[nin:], res):
            o_ref[...] = o.astype(o_ref.dtype)

    arrs, specs = [], []
    for it in rows:
        a, s = _row_in(it, tm)
        arrs.append(a)
        specs.append(s)
    for t in typed:
        arrs.append(t)
        specs.append(_typed_spec(t.shape[-1], nb_ctx))
    for s_ in shared:
        arrs.append(s_)
        specs.append(_const_spec(s_.shape))
    res = pl.pallas_call(
        body, name=name, grid=(n_rows // tm,), in_specs=specs,
        out_specs=[pl.BlockSpec((tm, w), lambda i: (i, 0)) for w, _ in outs],
        out_shape=[jax.ShapeDtypeStruct((n_rows, w), dt) for w, dt in outs],
        compiler_params=_cp(("parallel",)))(*arrs)
    return res


def rowwise_bwd(name, f, rows, typed, shared, cots, row_diff, shared_diff, drow_dtypes, n_rows, nb_ctx, tm=ROW_TILE):
    tm = min(tm, n_rows)
    cot_groups = [c_ if isinstance(c_, tuple) else (c_,) for c_ in cots]
    cots = [a for grp in cot_groups for a in grp]
    nr, nt, ns, nc = len(rows), len(typed), len(shared), len(cots)
    nin = nr + nt + ns
    d_rows = [k for k in range(nr) if row_diff[k]]
    d_sh = [k for k in range(ns) if shared_diff[k]]

    def body(*refs):
        rvals = [r[...] for r in refs[:nr]]
        tvals = [r[...] for r in refs[nr:nr + nt]]
        svals = [r[...] for r in refs[nr + nt:nin]]
        cparts = [r[...].astype(F32) for r in refs[nin:nin + nc]]
        cvals = []
        for grp in cot_groups:
            cvals.append(sum(cparts[1:len(grp)], cparts[0]))
            cparts = cparts[len(grp):]
        out_refs = refs[nin + nc:]

        def g(*dv):
            dv = list(dv)
            rv = list(rvals)
            for k in d_rows:
                rv[k] = dv.pop(0)
            tv = [dv.pop(0) for _ in range(nt)]
            sv = list(svals)
            for k in d_sh:
                sv[k] = dv.pop(0)
            return tuple(o.astype(F32) for o in f(*rv, *tv, *sv))

        prim = [rvals[k].astype(F32) for k in d_rows] + tvals + [svals[k] for k in d_sh]
        _, vjp = jax.vjp(g, *prim)
        grads = list(vjp(tuple(cvals)))
        i = pl.program_id(0)
        for ref in out_refs[:len(d_rows)]:
            ref[...] = grads.pop(0).astype(ref.dtype)
        first_typed = (i == 0) | (i == nb_ctx)
        for ref in out_refs[len(d_rows):len(d_rows) + nt]:
            gr = grads.pop(0)

            @pl.when(first_typed)
            def _(ref=ref, gr=gr):
                ref[...] = gr

            @pl.when(jnp.logical_not(first_typed))
            def _(ref=ref, gr=gr):
                ref[...] += gr
        for ref in out_refs[len(d_rows) + nt:]:
            gr = grads.pop(0)

            @pl.when(i == 0)
            def _(ref=ref, gr=gr):
                ref[...] = gr

            @pl.when(i != 0)
            def _(ref=ref, gr=gr):
                ref[...] += gr

    arrs, specs = [], []
    for it in list(rows):
        a, s = _row_in(it, tm)
        arrs.append(a)
        specs.append(s)
    for t in typed:
        arrs.append(t)
        specs.append(_typed_spec(t.shape[-1], nb_ctx))
    for s_ in shared:
        arrs.append(s_)
        specs.append(_const_spec(s_.shape))
    for c_ in cots:
        a, s = _row_in(c_, tm)
        arrs.append(a)
        specs.append(s)
    out_specs, out_shape = [], []
    for k, dt in zip(d_rows, drow_dtypes):
        w = _width(rows[k])
        out_specs.append(pl.BlockSpec((tm, w), lambda i: (i, 0)))
        out_shape.append(jax.ShapeDtypeStruct((n_rows, w), dt))
    for t in typed:
        out_specs.append(_typed_spec(t.shape[-1], nb_ctx))
        out_shape.append(jax.ShapeDtypeStruct(t.shape, F32))
    for k in d_sh:
        out_specs.append(_const_spec(shared[k].shape))
        out_shape.append(jax.ShapeDtypeStruct(shared[k].shape, F32))
    res = pl.pallas_call(body, name=name, grid=(n_rows // tm,), in_specs=specs, out_specs=out_specs,
                         out_shape=out_shape, compiler_params=_cp(("arbitrary",)))(*arrs)
    n1, n2 = len(d_rows), len(d_rows) + nt
    return list(res[:n1]), list(res[n1:n2]), list(res[n2:])


def _pick(n, prefs):
    for p in prefs:
        if n % p == 0:
            return p
    return n


def mm(name, a, b, out_dtype, transpose_b=False):
    n, k = b.shape if transpose_b else b.shape[::-1]
    m = (a.arr if isinstance(a, Cols) else a).shape[0]
    assert _width(a) == k
    tm = _pick(m, (MM_ROWS, 256))
    osz = jnp.dtype(out_dtype).itemsize
    tn = next(c for c in (2560, 2048, 1536, 1024, 512, 256, 128, n)
              if n % c == 0 and 2 * (tm * k * 2 + k * c * 2 + tm * c * osz) <= MM_VMEM_BUDGET or c == n)
    dims = _NT if transpose_b else (((1,), (0,)), ((), ()))

    def body(a_ref, b_ref, o_ref):
        o_ref[...] = lax.dot_general(a_ref[...], b_ref[...], dims, preferred_element_type=F32).astype(o_ref.dtype)

    a_arr, a_spec = _row_in(a, tm)
    a_spec = pl.BlockSpec(a_spec.block_shape, lambda j, i, f=a_spec.index_map: f(i))
    b_spec = pl.BlockSpec((tn, k), lambda j, i: (j, 0)) if transpose_b else pl.BlockSpec((k, tn), lambda j, i: (0, j))
    return pl.pallas_call(
        body, name=name, grid=(n // tn, m // tm), in_specs=[a_spec, b_spec],
        out_specs=pl.BlockSpec((tm, tn), lambda j, i: (i, j)),
        out_shape=jax.ShapeDtypeStruct((m, n), out_dtype),
        compiler_params=_cp(("parallel", "parallel")))(a_arr, b)


def mm_tn(name, a, b, out_dtype=F32, pieces=None):
    t = (a.arr if isinstance(a, Cols) else a).shape[0]
    k, n = _width(a), _width(b)
    tt = _pick(t, (MM_ROWS, 256))
    k_unit = pieces[1] if pieces and pieces[0] == "rows" else k
    n_unit = pieces[1] if pieces and pieces[0] == "cols" else n
    tk = _pick(k_unit, (1024, 512, 256, 128))
    tn = _pick(n_unit, (1280, 1024, 512, 256, 128))
    n_t = t // tt

    def body(a_ref, b_ref, o_ref, acc):
        part = lax.dot_general(a_ref[...], b_ref[...], _TN, preferred_element_type=F32)
        ti = pl.program_id(2)

        @pl.when(ti == 0)
        def _():
            acc[...] = part

        @pl.when(ti != 0)
        def _():
            acc[...] += part

        @pl.when(ti == n_t - 1)
        def _():
            o_ref[...] = acc[...].astype(o_ref.dtype)

    def win(item, width):
        if isinstance(item, Cols):
            assert item.off % width == 0
            return item.arr, item.off // width
        return item, 0

    a_arr, a0 = win(a, tk)
    b_arr, b0 = win(b, tn)
    if pieces is None:
        out_spec = pl.BlockSpec((tk, tn), lambda ki, ni, ti: (ki, ni))
        out_shape = (k, n)
    elif pieces[0] == "cols":
        per = n_unit // tn
        out_spec = pl.BlockSpec((None, tk, tn), lambda ki, ni, ti: (ni // per, ki, ni % per))
        out_shape = (4, k, n_unit)
    else:
        per = k_unit // tk
        out_spec = pl.BlockSpec((None, tk, tn), lambda ki, ni, ti: (ki // per, ki % per, ni))
        out_shape = (4, k_unit, n)
    return pl.pallas_call(
        body, name=name, grid=(k // tk, n // tn, n_t),
        in_specs=[pl.BlockSpec((tt, tk), lambda ki, ni, ti: (ti, a0 + ki)),
                  pl.BlockSpec((tt, tn), lambda ki, ni, ti: (ti, b0 + ni))],
        out_specs=out_spec, out_shape=jax.ShapeDtypeStruct(out_shape, out_dtype),
        scratch_shapes=[pltpu.VMEM((tk, tn), F32)],
        compiler_params=_cp(("parallel", "parallel", "arbitrary")))(a_arr, b_arr)


def _heads_t(rows_blk):
    blk = rows_blk.astype(F32).T
    return jnp.concatenate([blk[hh * ATTN_HD:(hh + 1) * ATTN_HD, :] for hh in range(4)], axis=1)


def _heads_rows(t_blk):
    tq = t_blk.shape[1] // 4
    return jnp.concatenate([t_blk[:, hh * tq:(hh + 1) * tq] for hh in range(4)], axis=0).T


def attn_fwd(name, q, kk, vT, m_ctx, tq):
    t, hd = kk.shape[1], ATTN_HD
    nq, r = t // tq, 4 * tq
    tk = _pick(t - m_ctx, (ATTN_TK_FWD, ATTN_TK))
    nqc, n_lat_tiles = m_ctx // tq, (t - m_ctx) // tk

    def body(q_ref, k_ref, vT_ref, o_ref, qT_ref, oT_ref, lse_ref):
        i = pl.program_id(1)
        q_t = _heads_t(q_ref[...]).astype(CDT)
        qT_ref[...] = q_t

        def tile(off, size, carry):
            mi, li, acc = carry
            sub = min(size, ATTN_TK)
            offs = [off + u * sub for u in range(size // sub)]
            sts = [jnp.dot(k_ref[pl.ds(o, sub), :], q_t, preferred_element_type=F32) for o in offs]
            for o, st in zip(offs, sts):
                mn = jnp.maximum(mi, jnp.max(st, axis=0, keepdims=True))
                pt = jnp.exp2(st - mn)
                al = jnp.exp2(mi - mn)
                li = al * li + jnp.sum(pt, axis=0, keepdims=True)
                acc = al * acc + jnp.dot(vT_ref[:, pl.ds(o, sub)], pt.astype(CDT), preferred_element_type=F32)
                mi = mn
            return mi, li, acc

        init = (jnp.full((1, r), -1e30, F32), jnp.zeros((1, r), F32), jnp.zeros((hd, r), F32))
        carry = tile(0, m_ctx, init)
        mi, li, acc = lax.fori_loop(
            0, jnp.where(i < nqc, 0, n_lat_tiles),
            lambda j, cr: tile(pl.multiple_of(m_ctx + j * tk, ATTN_TK), tk, cr), carry)
        o_t = acc / li
        oT_ref[...] = o_t.astype(oT_ref.dtype)
        o_ref[...] = _heads_rows(o_t).astype(o_ref.dtype)
        lse_ref[...] = mi + jnp.log2(li)

    blk_t = pl.BlockSpec((None, None, hd, r), lambda g, i: (g, i, 0, 0))
    rows = pl.BlockSpec((tq, 4 * hd), lambda g, i: (i, g))
    return pl.pallas_call(
        body, name=name, grid=(2, nq),
        in_specs=[rows, pl.BlockSpec((None, t, hd), lambda g, i: (g, 0, 0)),
                  pl.BlockSpec((None, hd, t), lambda g, i: (g, 0, 0))],
        out_specs=[rows, blk_t, blk_t, pl.BlockSpec((None, None, 1, r), lambda g, i: (g, i, 0, 0))],
        out_shape=[jax.ShapeDtypeStruct((t, 8 * hd), CDT), jax.ShapeDtypeStruct((2, nq, hd, r), CDT),
                   jax.ShapeDtypeStruct((2, nq, hd, r), CDT), jax.ShapeDtypeStruct((2, nq, 1, r), F32)],
        compiler_params=_cp(("parallel", "arbitrary")))(q, kk, vT)


def attn_bwd(name, qT, do, oT, lse, kk, kT, vv, m_ctx):
    _, nq, hd, r = qT.shape
    t = kk.shape[1]
    tq = r // 4
    tk = _pick(t - m_ctx, (ATTN_TK_BWD, ATTN_TK))
    nqc, n_lat_tiles = m_ctx // tq, (t - m_ctx) // tk

    def body(qT_ref, do_ref, oT_ref, lse_ref, k_ref, kT_ref, v_ref, dq_ref, dk_ref, dv_ref):
        i = pl.program_id(1)

        @pl.when(i == 0)
        def _():
            dk_ref[...] = jnp.zeros_like(dk_ref)
            dv_ref[...] = jnp.zeros_like(dv_ref)

        q_t = qT_ref[...]
        do_f = _heads_t(do_ref[...])
        do_t = do_f.astype(CDT)
        lse = lse_ref[...]
        delta = jnp.sum(do_f * oT_ref[...].astype(F32), axis=0, keepdims=True)

        def tile(off, size, dq):
            sub = min(size, ATTN_TK)
            offs = [off + u * sub for u in range(size // sub)]
            sts = [jnp.dot(k_ref[pl.ds(o, sub), :], q_t, preferred_element_type=F32) for o in offs]
            dpts = [jnp.dot(v_ref[pl.ds(o, sub), :], do_t, preferred_element_type=F32) for o in offs]
            for o, st, dpt in zip(offs, sts, dpts):
                pt = jnp.exp2(st - lse)
                dv_ref[pl.ds(o, sub), :] += lax.dot_general(pt.astype(CDT), do_t, _NT, preferred_element_type=F32)
                dst = (pt * (dpt - delta)).astype(CDT)
                dk_ref[pl.ds(o, sub), :] += lax.dot_general(dst, q_t, _NT, preferred_element_type=F32)
                dq = dq + jnp.dot(kT_ref[:, pl.ds(o, sub)], dst, preferred_element_type=F32)
            return dq

        dq = tile(0, m_ctx, jnp.zeros((hd, r), F32))
        dq = lax.fori_loop(0, jnp.where(i < nqc, 0, n_lat_tiles),
                           lambda j, acc: tile(pl.multiple_of(m_ctx + j * tk, ATTN_TK), tk, acc), dq)
        dq_ref[...] = _heads_rows(dq * LN2)

    blk_t = pl.BlockSpec((None, None, hd, r), lambda g, i: (g, i, 0, 0))
    row = pl.BlockSpec((None, None, 1, r), lambda g, i: (g, i, 0, 0))
    kv = pl.BlockSpec((None, t, hd), lambda g, i: (g, 0, 0))
    rows = pl.BlockSpec((tq, 4 * hd), lambda g, i: (i, g))
    return pl.pallas_call(
        body, name=name, grid=(2, nq),
        in_specs=[blk_t, rows, blk_t, row, kv, pl.BlockSpec((None, hd, t), lambda g, i: (g, 0, 0)), kv],
        out_specs=[rows, kv, kv],
        out_shape=[jax.ShapeDtypeStruct((t, 8 * hd), F32), jax.ShapeDtypeStruct(kk.shape, F32),
                   jax.ShapeDtypeStruct(kk.shape, F32)],
        compiler_params=_cp(("parallel", "arbitrary")))(qT, do, oT, lse, kk, kT, vv)


def _split_kv(a):
    return a.reshape(a.shape[0], 2, ATTN_HD).transpose(1, 0, 2)


def _merge_kv(a):
    return a.transpose(1, 0, 2).reshape(a.shape[1], 2 * ATTN_HD)


def _chunk_order(rev, ncc, nct):
    if not rev:
        return lambda s: s
    return lambda s: jnp.where(s < ncc, ncc - 1 - s, nct - 1 - (s - ncc))


def scan_fwd(name, make_fn, rows, shared, n_state, y_width, n_rows, m_ctx):
    nct, ncc = n_rows // CHUNK, m_ctx // CHUNK
    orders = [_chunk_order(rev, ncc, nct) for rev in (False, True)]
    fns = [make_fn(0), make_fn(1)]
    nr, ns = len(rows), len(shared)

    def body(*refs):
        svals = [r[...] for r in refs[2 * nr:2 * nr + ns]]
        y_refs, sin_refs, st = refs[2 * nr + ns:2 * nr + ns + 2], refs[2 * nr + ns + 2:2 * nr + ns + 4], refs[-1]

        @pl.when(pl.program_id(0) == 0)
        def _():
            st[...] = jnp.zeros_like(st)

        for d in range(2):
            rvals = [r[...] for r in refs[d * nr:(d + 1) * nr]]
            prev = [st[d, k] for k in range(n_state)]
            sin_refs[d][...] = st[d]
            y, new = fns[d](rvals, svals, prev)
            y_refs[d][...] = y
            for k in range(n_state):
                st[d, k] = new[k]

    arrs, specs = [], []
    for order in orders:
        for it in rows:
            a, s = _row_in(it, CHUNK, order)
            arrs.append(a)
            specs.append(s)
    for s_ in shared:
        arrs.append(s_)
        specs.append(_const_spec(s_.shape))
    return pl.pallas_call(
        body, name=name, grid=(nct,), in_specs=specs,
        out_specs=[pl.BlockSpec((CHUNK, y_width), lambda s, o=o: (o(s), 0)) for o in orders]
        + [pl.BlockSpec((None, n_state, LANES, LANES), lambda s, o=o: (o(s), 0, 0, 0)) for o in orders],
        out_shape=[jax.ShapeDtypeStruct((n_rows, y_width), F32)] * 2
        + [jax.ShapeDtypeStruct((nct, n_state, LANES, LANES), F32)] * 2,
        scratch_shapes=[pltpu.VMEM((2, n_state, LANES, LANES), F32)],
        compiler_params=_cp(("arbitrary",)))(*arrs)


def scan_bwd(name, make_fn, rows, shared, states_in, dy, post, outs, n_state, n_rows, m_ctx, dirs=(0, 1)):
    nct, ncc = n_rows // CHUNK, m_ctx // CHUNK
    orders = [(lambda r, f=_chunk_order(d == 1, ncc, nct): f(nct - 1 - r)) for d in dirs]
    fns = [make_fn(d) for d in dirs]
    nd = len(dirs)
    nr, ns, no = len(rows), len(shared), len(outs)
    n_in = nd * nr + ns

    def body(*refs):
        svals = [r[...] for r in refs[nd * nr:n_in]]
        sin_refs, dy_refs = refs[n_in:n_in + nd], refs[n_in + nd:n_in + 2 * nd]
        out_refs = refs[n_in + 2 * nd:n_in + 2 * nd + nd * no]
        dsh_refs = refs[n_in + 2 * nd + nd * no:-1]
        dst = refs[-1]
        r = pl.program_id(0)

        @pl.when(r == 0)
        def _():
            dst[...] = jnp.zeros_like(dst)

        d_shared = None
        for d in range(nd):
            rvals = [x[...] for x in refs[d * nr:(d + 1) * nr]]
            prev = [sin_refs[d][k] for k in range(n_state)]
            _, vjp = jax.vjp(fns[d], rvals, svals, prev)
            d_rows, d_sh, d_prev = vjp((dy_refs[d][...], [dst[d, k] for k in range(n_state)]))
            for ref, val in zip(out_refs[d * no:(d + 1) * no], post(d_rows)):
                ref[...] = val.astype(ref.dtype)
            d_shared = d_sh if d_shared is None else [a + b for a, b in zip(d_shared, d_sh)]
            for k in range(n_state):
                dst[d, k] = d_prev[k]
        for ref, gr in zip(dsh_refs, d_shared):
            @pl.when(r == 0)
            def _(ref=ref, gr=gr):
                ref[...] = gr

            @pl.when(r != 0)
            def _(ref=ref, gr=gr):
                ref[...] += gr

    arrs, specs = [], []
    for order in orders:
        for it in rows:
            a, s = _row_in(it, CHUNK, order)
            arrs.append(a)
            specs.append(s)
    for s_ in shared:
        arrs.append(s_)
        specs.append(_const_spec(s_.shape))
    for sin, order in zip(states_in, orders):
        arrs.append(sin)
        specs.append(pl.BlockSpec((None, n_state, LANES, LANES), lambda r, o=order: (o(r), 0, 0, 0)))
    for order in orders:
        a, s = _row_in(dy, CHUNK, order)
        arrs.append(a)
        specs.append(s)
    out_specs = [pl.BlockSpec((CHUNK, w), lambda r, o=o: (o(r), 0)) for o in orders for w, _ in outs]
    out_shape = [jax.ShapeDtypeStruct((n_rows, w), dt) for _ in orders for w, dt in outs]
    for s_ in shared:
        out_specs.append(_const_spec(s_.shape))
        out_shape.append(jax.ShapeDtypeStruct(s_.shape, F32))
    res = pl.pallas_call(body, name=name, grid=(nct,), in_specs=specs, out_specs=out_specs, out_shape=out_shape,
                         scratch_shapes=[pltpu.VMEM((nd, n_state, LANES, LANES), F32)],
                         compiler_params=_cp(("arbitrary",)))(*arrs)
    return [list(res[d * no:(d + 1) * no]) for d in range(nd)] + [list(res[nd * no:])]


def _make_ssd_chunk(direction):
    rev = direction == 1
    base = 8 * direction

    def fn(rows, shared, prev):
        xs, bms, cms, dtraw = rows[0:4], rows[4:6], rows[6:8], rows[8]
        dt_bias, a_log = shared
        ln = dtraw.shape[0]
        dt_all = _softplus(dtraw + dt_bias)
        a_all = dt_all * (-jnp.exp(a_log))
        r_i = lax.broadcasted_iota(jnp.int32, (ln, ln), 0)
        c_i = lax.broadcasted_iota(jnp.int32, (ln, ln), 1)
        tri = (r_i <= c_i) if rev else (r_i >= c_i)
        a_cum_all = jnp.dot(tri.astype(F32), a_all, precision=lax.Precision.HIGHEST, preferred_element_type=F32)
        a_tot_all = jnp.sum(a_all, axis=0, keepdims=True)
        first = lax.broadcasted_iota(jnp.int32, (ln, LANES), 1) < SSD_HD
        first_row = lax.broadcasted_iota(jnp.int32, (LANES, 1), 0) < SSD_HD

        def lmat(acol):
            a_b = jnp.broadcast_to(acol, (ln, ln))
            seg = a_b - a_b.T
            return jnp.where(tri, jnp.exp(jnp.where(tri, seg, 0.0)), 0.0)

        ys, new = [], []
        for g in range(2):
            bm, cm = bms[g], cms[g]
            cb = _mxu(cm, bm, _NT)
            for jj in range(2):
                pr = 2 * g + jj
                h0, h1 = base + 2 * pr, base + 2 * pr + 1
                ac0, ac1 = _col(a_cum_all, h0), _col(a_cum_all, h1)
                at0, at1 = _col(a_tot_all, h0), _col(a_tot_all, h1)
                dt_pair = jnp.where(first, _col(dt_all, h0), _col(dt_all, h1))
                acum_pair = jnp.where(first, ac0, ac1)
                atot_pair = jnp.where(first[0:1], at0, at1)
                xd = xs[pr] * dt_pair
                st = _mxu(xd * jnp.exp(atot_pair - acum_pair), bm, _TN)
                new.append(prev[pr] * jnp.where(first_row, jnp.exp(at0), jnp.exp(at1)) + st)
                y0 = _mxu(cb * lmat(ac0), xd)
                y1 = _mxu(cb * lmat(ac1), xd)
                y_off = _mxu(cm, prev[pr], _NT) * jnp.exp(acum_pair)
                ys.append(jnp.where(first, y0, y1) + y_off)
        return jnp.concatenate(ys, axis=1), new

    return fn


def _make_ret_chunk(direction):
    rev = direction == 1
    base = 4 * direction

    def fn(rows, shared, prev):
        qs, ks, vs = rows[0:4], rows[4:8], rows[8:12]
        lg_all = -jnp.exp(shared[0])
        ln = qs[0].shape[0]
        pos = lax.broadcasted_iota(jnp.int32, (ln, 1), 0).astype(F32)
        r_i = lax.broadcasted_iota(jnp.int32, (ln, ln), 0)
        c_i = lax.broadcasted_iota(jnp.int32, (ln, ln), 1)
        diff = ((c_i - r_i) if rev else (r_i - c_i))
        mask = diff >= 0
        dpos = jnp.maximum(diff, 0).astype(F32)
        k_pow = pos if rev else (ln - 1.0 - pos)
        q_pow = (ln - pos) if rev else (pos + 1.0)
        ys, new = [], []
        for h in range(RET_HEADS):
            lg = _col(lg_all, base + h)
            dmat = jnp.where(mask, jnp.exp(dpos * lg), 0.0)
            st = _mxu(ks[h] * jnp.exp(k_pow * lg), vs[h], _TN)
            new.append(prev[h] * jnp.exp(ln * lg) + st)
            s = _mxu(qs[h], ks[h], _NT) * dmat
            ys.append(_mxu(s, vs[h]) + _mxu(qs[h], prev[h]) * jnp.exp(q_pow * lg))
        return jnp.concatenate(ys, axis=1), new

    return fn


def _conv_pre(x, w, b, t_idx, n_rows, m_ctx):
    is_start = (t_idx == 0) | (t_idx == m_ctx)
    is_end = (t_idx == m_ctx - 1) | (t_idx == n_rows - 1)
    xp = jnp.where(is_start, 0.0, pltpu.roll(x, 1, axis=0))
    xn = jnp.where(is_end, 0.0, pltpu.roll(x, n_rows - 1, axis=0))
    return w[0:1] * xp + w[1:2] * x + w[2:3] * xn + b, xp, xn, is_start, is_end


def conv_fwd(x, conv_w, conv_b, m_ctx):
    n_rows, width = x.arr.shape[0], x.width
    c0 = x.off // LANES

    def body(x_ref, w_ref, b_ref, o_ref):
        t_idx = lax.broadcasted_iota(jnp.int32, (n_rows, 1), 0)
        pre = _conv_pre(x_ref[...], w_ref[...], b_ref[...], t_idx, n_rows, m_ctx)[0]
        o_ref[...] = pre * jax.nn.sigmoid(pre)

    return pl.pallas_call(
        body, name="conv_fwd", grid=(width // LANES,),
        in_specs=[pl.BlockSpec((n_rows, LANES), lambda c: (0, c0 + c)),
                  pl.BlockSpec((3, LANES), lambda c: (0, c)), pl.BlockSpec((1, LANES), lambda c: (0, c))],
        out_specs=pl.BlockSpec((n_rows, LANES), lambda c: (0, c)),
        out_shape=jax.ShapeDtypeStruct((n_rows, width), F32),
        compiler_params=_cp(("parallel",)))(x.arr, conv_w, conv_b)


def conv_bwd(x, conv_w, conv_b, dy_a, dy_b, dxs_extra, m_ctx):
    n_rows, width = x.arr.shape[0], x.width
    c0 = x.off // LANES
    n_extra = dxs_extra.shape[1] // LANES

    def body(x_ref, w_ref, b_ref, dya_ref, dyb_ref, ex_ref, dx_ref, dw_ref, db_ref):
        c = pl.program_id(0)
        t_idx = lax.broadcasted_iota(jnp.int32, (n_rows, 1), 0)
        w = w_ref[...]
        pre, xp, xn, is_start, is_end = _conv_pre(x_ref[...], w, b_ref[...], t_idx, n_rows, m_ctx)
        sg = jax.nn.sigmoid(pre)
        dyv = dya_ref[...] + dyb_ref[...] + jnp.where(c < n_extra, ex_ref[...], 0.0)
        dpre = dyv * (sg * (1.0 + pre * (1.0 - sg)))
        d_next = jnp.where(is_end, 0.0, pltpu.roll(dpre, n_rows - 1, axis=0))
        d_prev = jnp.where(is_start, 0.0, pltpu.roll(dpre, 1, axis=0))
        dx_ref[...] = (w[1:2] * dpre + w[0:1] * d_next + w[2:3] * d_prev).astype(dx_ref.dtype)
        dw_ref[...] = jnp.concatenate([jnp.sum(dpre * xp, axis=0, keepdims=True),
                                       jnp.sum(dpre * x_ref[...], axis=0, keepdims=True),
                                       jnp.sum(dpre * xn, axis=0, keepdims=True)], axis=0)
        db_ref[...] = jnp.sum(dpre, axis=0, keepdims=True)

    return pl.pallas_call(
        body, name="conv_bwd", grid=(width // LANES,),
        in_specs=[pl.BlockSpec((n_rows, LANES), lambda c: (0, c0 + c)),
                  pl.BlockSpec((3, LANES), lambda c: (0, c)), pl.BlockSpec((1, LANES), lambda c: (0, c)),
                  pl.BlockSpec((n_rows, LANES), lambda c: (0, c)), pl.BlockSpec((n_rows, LANES), lambda c: (0, c)),
                  pl.BlockSpec((n_rows, LANES), lambda c: (0, jnp.minimum(c, n_extra - 1)))],
        out_specs=[pl.BlockSpec((n_rows, LANES), lambda c: (0, c)),
                   pl.BlockSpec((3, LANES), lambda c: (0, c)), pl.BlockSpec((1, LANES), lambda c: (0, c))],
        out_shape=[jax.ShapeDtypeStruct((n_rows, width), CDT), jax.ShapeDtypeStruct((3, width), F32),
                   jax.ShapeDtypeStruct((1, width), F32)],
        compiler_params=_cp(("parallel",)))(x.arr, conv_w, conv_b, dy_a, dy_b, dxs_extra)


def loss_head(h, target, final_w, m_ctx):
    n_rows, d = h.shape
    tm = min(ROW_TILE, n_rows)
    nb_ctx = m_ctx // tm

    def f(hb, w, tgt):
        err = _rms(hb, w) - tgt
        return 0.5 * jnp.sum(jnp.mean(err * err, axis=-1))

    def body(h_ref, t_ref, w_ref, loss_ref, dh_ref, dw_ref):
        i = pl.program_id(0)

        @pl.when(i < nb_ctx)
        def _():
            dh_ref[...] = jnp.zeros_like(dh_ref)

        @pl.when(i == 0)
        def _():
            loss_ref[...] = jnp.zeros_like(loss_ref)
            dw_ref[...] = jnp.zeros_like(dw_ref)

        @pl.when(i >= nb_ctx)
        def _():
            val, vjp = jax.vjp(lambda hb, w: f(hb, w, t_ref[...]), h_ref[...], w_ref[...])
            dh, dw = vjp(jnp.ones((), F32))
            dh_ref[...] = dh
            dw_ref[...] += dw
            loss_ref[...] += jnp.broadcast_to(val, loss_ref.shape)

    return pl.pallas_call(
        body, name="loss_head", grid=(n_rows // tm,),
        in_specs=[pl.BlockSpec((tm, d), lambda i: (i, 0)),
                  pl.BlockSpec((tm, d), lambda i: (jnp.maximum(i - nb_ctx, 0), 0)), _const_spec((1, d))],
        out_specs=[_const_spec((1, LANES)), pl.BlockSpec((tm, d), lambda i: (i, 0)), _const_spec((1, d))],
        out_shape=[jax.ShapeDtypeStruct((1, LANES), F32), jax.ShapeDtypeStruct((n_rows, d), F32),
                   jax.ShapeDtypeStruct((1, d), F32)],
        compiler_params=_cp(("arbitrary",)))(h, target, final_w)


def adamw(name, w, m, v, g_parts):
    lead, rows, cols = w.shape
    tr = _pick(rows, (256, 128, 64, 32, 16, 8))
    npart = len(g_parts)
    c1 = 1.0 - ADAM_B1 ** ADAM_STEP
    c2 = 1.0 - ADAM_B2 ** ADAM_STEP

    def body(*refs):
        w_ref, m_ref, v_ref = refs[:3]
        g = refs[3][...].astype(F32)
        for r in refs[4:3 + npart]:
            g = g + r[...].astype(F32)
        g_ref, d_ref, nm_ref, nv_ref = refs[3 + npart:]
        nm = ADAM_B1 * m_ref[...] + (1.0 - ADAM_B1) * g
        nv = ADAM_B2 * v_ref[...] + (1.0 - ADAM_B2) * (g * g)
        g_ref[...] = g
        nm_ref[...] = nm
        nv_ref[...] = nv
        d_ref[...] = -ADAM_LR * ((nm / c1) / (jnp.sqrt(nv / c2) + ADAM_EPS) + ADAM_WD * w_ref[...])

    spec = pl.BlockSpec((None, tr, cols), lambda l, i: (l, i, 0))
    return pl.pallas_call(
        body, name=name, grid=(lead, rows // tr), in_specs=[spec] * (3 + npart), out_specs=[spec] * 4,
        out_shape=[jax.ShapeDtypeStruct(w.shape, F32)] * 4,
        compiler_params=_cp(("parallel", "parallel")))(w, m, v, *g_parts)


MESH = pl.DeviceIdType.MESH
_HBM = pl.BlockSpec(memory_space=pl.ANY)


def _chip_peers():
    x, y, c = lax.axis_index("x"), lax.axis_index("y"), lax.axis_index("c")
    return x, y, c, [(1 - x, y), (x, 1 - y), (1 - x, 1 - y)]


def _window(ref, kind, chip, rows, cols):
    if kind == "cols":
        return ref.at[:, pl.ds(pl.multiple_of(chip * cols, LANES), cols)]
    if kind == "rows":
        return ref.at[pl.ds(pl.multiple_of(chip * rows, 8), rows), :]
    return ref.at[chip]


def _gathered_shape(kind, rows, cols):
    return {"cols": (rows, 4 * cols), "rows": (4 * rows, cols), "slices": (4, rows, cols)}[kind]


def gather_layers(name, shards, kinds):
    n = len(shards)

    def body(*refs):
        x_refs, o_refs = refs[:n], refs[n:2 * n]
        send_sems, recv_sems, local_sems = refs[2 * n:]
        x, y, c, peers = _chip_peers()
        me = 2 * x + y
        started = []
        for a in range(n):
            _, rows, cols = shards[a].shape
            src = x_refs[a].at[c]
            mine = pltpu.make_async_copy(src, _window(o_refs[a], kinds[a], me, rows, cols), local_sems.at[a])
            mine.start()
            started.append(mine.wait)
            for k, (px, py) in enumerate(peers):
                cp = pltpu.make_async_remote_copy(
                    src_ref=src, dst_ref=_window(o_refs[a], kinds[a], me, rows, cols), send_sem=send_sems.at[3 * a + k],
                    recv_sem=recv_sems.at[3 * a + k], device_id=(px, py, c), device_id_type=MESH)
                cp.start()
                started.append(cp.wait_send)
        for a in range(n):
            _, rows, cols = shards[a].shape
            for k, (px, py) in enumerate(peers):
                pltpu.make_async_remote_copy(
                    src_ref=x_refs[a].at[c], dst_ref=_window(o_refs[a], kinds[a], 2 * px + py, rows, cols),
                    send_sem=send_sems.at[3 * a + k], recv_sem=recv_sems.at[3 * a + k], device_id=(px, py, c),
                    device_id_type=MESH).wait_recv()
        for wait in started:
            wait()

    return pl.pallas_call(
        body, name=name, in_specs=[_HBM] * n, out_specs=[_HBM] * n,
        out_shape=[jax.ShapeDtypeStruct(_gathered_shape(kinds[a], *shards[a].shape[1:]), shards[a].dtype)
                   for a in range(n)],
        scratch_shapes=[pltpu.SemaphoreType.DMA((3 * n,)), pltpu.SemaphoreType.DMA((3 * n,)),
                        pltpu.SemaphoreType.DMA((n,))],
        )(*shards)


def scatter_pieces(name, pieces):
    n = len(pieces)

    def body(*refs):
        p_refs, o_refs = refs[:n], refs[n:2 * n]
        send_sems, recv_sems, local_sems = refs[2 * n:]
        x, y, c, peers = _chip_peers()
        me = 2 * x + y
        started = []
        for a in range(n):
            mine = pltpu.make_async_copy(p_refs[a].at[me], o_refs[a].at[me], local_sems.at[a])
            mine.start()
            started.append(mine.wait)
            for k, (px, py) in enumerate(peers):
                cp = pltpu.make_async_remote_copy(
                    src_ref=p_refs[a].at[2 * px + py], dst_ref=o_refs[a].at[me], send_sem=send_sems.at[3 * a + k],
                    recv_sem=recv_sems.at[3 * a + k], device_id=(px, py, c), device_id_type=MESH)
                cp.start()
                started.append(cp.wait_send)
        for a in range(n):
            for k, (px, py) in enumerate(peers):
                pltpu.make_async_remote_copy(
                    src_ref=p_refs[a].at[me], dst_ref=o_refs[a].at[2 * px + py], send_sem=send_sems.at[3 * a + k],
                    recv_sem=recv_sems.at[3 * a + k], device_id=(px, py, c), device_id_type=MESH).wait_recv()
        for wait in started:
            wait()

    return pl.pallas_call(
        body, name=name, in_specs=[_HBM] * n, out_specs=[_HBM] * n,
        out_shape=[jax.ShapeDtypeStruct(p.shape, p.dtype) for p in pieces],
        scratch_shapes=[pltpu.SemaphoreType.DMA((3 * n,)), pltpu.SemaphoreType.DMA((3 * n,)),
                        pltpu.SemaphoreType.DMA((n,))],
        )(*pieces)


def _pair_step(n_steps, x_ref, land, send_sems, recv_sems, credits, consume):
    x, y, c = lax.axis_index("x"), lax.axis_index("y"), lax.axis_index("c")
    sib = (x, y, 1 - c)
    i = pl.program_id(0)
    slot = i % 2

    @pl.when(i >= 2)
    def _():
        pl.semaphore_wait(credits.at[slot], 1)

    cp = pltpu.make_async_remote_copy(src_ref=x_ref, dst_ref=land.at[slot], send_sem=send_sems.at[slot],
                                      recv_sem=recv_sems.at[slot], device_id=sib, device_id_type=MESH)
    cp.start()
    cp.wait_recv()
    consume(land[slot])

    @pl.when(i < n_steps - 2)
    def _():
        pl.semaphore_signal(credits.at[slot], inc=1, device_id=sib, device_id_type=MESH)

    cp.wait_send()


def _pair_call(name, body, n_steps, in_specs, out_spec, out_shape, blk_shape, dtype, operands, extra_scratch=()):
    grid_spec = pltpu.PrefetchScalarGridSpec(
        num_scalar_prefetch=1, grid=(n_steps,), in_specs=in_specs, out_specs=out_spec,
        scratch_shapes=[pltpu.VMEM((2,) + blk_shape, dtype), pltpu.SemaphoreType.DMA((2,)),
                        pltpu.SemaphoreType.DMA((2,)), pltpu.SemaphoreType.REGULAR((2,)), *extra_scratch])
    return pl.pallas_call(body, name=name, grid_spec=grid_spec, out_shape=out_shape,
                          compiler_params=_cp(("arbitrary",)))(*operands)


def _place():
    return jnp.stack([lax.axis_index("x"), lax.axis_index("y"), lax.axis_index("c")]).astype(jnp.int32)


def _pair_rows(rows, row_bytes):
    for cand in (4096, 2048, 1024, 768, 512, 384, 256, 192, 128, 96, 64, 48, 32, 16):
        if rows % cand == 0 and cand * row_bytes <= PAIR_BLOCK_BYTES:
            return cand
    return _pick(rows, (16, 8))


def exchange_both(name, mine):
    rows, cols = mine.shape
    tr = _pair_rows(rows, cols * mine.dtype.itemsize)
    n_steps = rows // tr

    def body(s_ref, x_ref, o_ref, land, send_sems, recv_sems, credits):
        c = lax.axis_index("c")
        o_ref[c] = x_ref[...]

        def consume(v):
            o_ref[1 - c] = v
        _pair_step(n_steps, x_ref, land, send_sems, recv_sems, credits, consume)

    return _pair_call(name, body, n_steps, [pl.BlockSpec((tr, cols), lambda i, s: (i, 0))],
                      pl.BlockSpec((2, tr, cols), lambda i, s: (0, i, 0)),
                      jax.ShapeDtypeStruct((2, rows, cols), mine.dtype), (tr, cols), mine.dtype, (_place(), mine))


def exchange_add(name, both_layers):
    _, rows, cols = both_layers.shape
    tr = _pair_rows(rows, cols * both_layers.dtype.itemsize)
    nb = rows // tr
    flat = both_layers.reshape(2 * rows, cols)

    def body(s_ref, x_ref, m_ref, o_ref, land, send_sems, recv_sems, credits):
        def consume(v):
            o_ref[...] = (m_ref[...].astype(F32) + v.astype(F32)).astype(o_ref.dtype)
        _pair_step(nb, x_ref, land, send_sems, recv_sems, credits, consume)

    return _pair_call(name, body, nb,
                      [pl.BlockSpec((tr, cols), lambda i, s: ((1 - s[2]) * nb + i, 0)),
                       pl.BlockSpec((tr, cols), lambda i, s: (s[2] * nb + i, 0))],
                      pl.BlockSpec((tr, cols), lambda i, s: (i, 0)), jax.ShapeDtypeStruct((rows, cols), CDT),
                      (tr, cols), flat.dtype, (_place(), flat, flat))


def sum_exchange(name, parts):
    npart, rows, cols = parts.shape
    tr = _pair_rows(rows, cols * 4)
    n_steps = rows // tr

    def body(s_ref, x_ref, o_ref, land, send_sems, recv_sems, credits, mine):
        c = lax.axis_index("c")
        acc = x_ref[0].astype(F32)
        for k in range(1, npart):
            acc = acc + x_ref[k].astype(F32)
        mine[...] = acc
        o_ref[c] = acc

        def consume(v):
            o_ref[1 - c] = v
        _pair_step(n_steps, mine, land, send_sems, recv_sems, credits, consume)

    return _pair_call(name, body, n_steps, [pl.BlockSpec((npart, tr, cols), lambda i, s: (0, i, 0))],
                      pl.BlockSpec((2, tr, cols), lambda i, s: (0, i, 0)),
                      jax.ShapeDtypeStruct((2, rows, cols), F32), (tr, cols), F32, (_place(), parts),
                      extra_scratch=(pltpu.VMEM((tr, cols), F32),))


def allreduce_small(name, buf):
    rows = buf.shape[0]

    def body(x_ref, out_ref, gath, send_sems, recv_sems):
        x, y, c = lax.axis_index("x"), lax.axis_index("y"), lax.axis_index("c")
        me = 4 * x + 2 * y + c
        masks = [(k >> 2 & 1, k >> 1 & 1, k & 1) for k in range(1, 8)]

        def flip(v, bit):
            return 1 - v if bit else v

        sends = []
        for k, (bx, by, bc) in enumerate(masks):
            cp = pltpu.make_async_remote_copy(src_ref=x_ref, dst_ref=gath.at[me], send_sem=send_sems.at[k],
                                              recv_sem=recv_sems.at[k],
                                              device_id=(flip(x, bx), flip(y, by), flip(c, bc)), device_id_type=MESH)
            cp.start()
            sends.append(cp)
        gath[me] = x_ref[...]
        for k, (bx, by, bc) in enumerate(masks):
            px, py, pc = flip(x, bx), flip(y, by), flip(c, bc)
            pltpu.make_async_remote_copy(src_ref=x_ref, dst_ref=gath.at[4 * px + 2 * py + pc],
                                         send_sem=send_sems.at[k], recv_sem=recv_sems.at[k],
                                         device_id=(px, py, pc), device_id_type=MESH).wait_recv()
        for cp in sends:
            cp.wait_send()
        acc = gath[0]
        for d in range(1, 8):
            acc = acc + gath[d]
        out_ref[...] = acc

    return pl.pallas_call(
        body, name=name, in_specs=[pl.BlockSpec(memory_space=pltpu.VMEM)],
        out_specs=pl.BlockSpec(memory_space=pltpu.VMEM), out_shape=jax.ShapeDtypeStruct(buf.shape, F32),
        scratch_shapes=[pltpu.VMEM((8, rows, LANES), F32), pltpu.SemaphoreType.DMA((7,)),
                        pltpu.SemaphoreType.DMA((7,))],
        )(buf)


def _pack_flat(arrs, dtype, width, row_mult=8):
    flat = jnp.concatenate([a.reshape(-1).astype(dtype) for a in arrs])
    pad = (-flat.shape[0]) % (row_mult * width)
    if pad:
        flat = jnp.concatenate([flat, jnp.zeros((pad,), dtype)])
    return flat.reshape(-1, width)


def _unpack_flat(buf, shapes):
    flat = buf.reshape(-1)
    out, off = [], 0
    for s in shapes:
        n = math.prod(s)
        out.append(flat[off:off + n].reshape(s))
        off += n
    return out


def _in_to_padded(w):
    parts = []
    for name in IN_NEW_ORDER:
        _, width, o_off, o_w = IN_LAYOUT[name]
        parts.append(w[..., o_off:o_off + o_w])
        if o_w < width:
            parts.append(jnp.zeros(w.shape[:-1] + (width - o_w,), w.dtype))
    used = sum(IN_LAYOUT[n][1] for n in IN_NEW_ORDER)
    parts.append(jnp.zeros(w.shape[:-1] + (IN_PAD - used,), w.dtype))
    return jnp.concatenate(parts, axis=-1)


def _in_from_padded(g):
    parts = []
    for name in IN_ORIG_ORDER:
        off, _, _, o_w = IN_LAYOUT[name]
        parts.append(g[..., off:off + o_w])
    return jnp.concatenate(parts, axis=-1)


def _pcol(p, name):
    off, width, _, _ = IN_LAYOUT[name]
    return Cols(p, off, width)


def _lane_pad(v, width=LANES):
    v = v.reshape(-1)
    return jnp.concatenate([v, jnp.zeros((width - v.shape[0],), v.dtype)]).reshape(1, width)


def _f_norm_mod(h, sh, sc, w):
    return (_rms(h, w) * (1.0 + sc) + sh,)


def _f_norm_mod_thru(h, sh, sc, w):
    return h, _rms(h, w) * (1.0 + sc) + sh


def _f_attn_prep(qraw, kraw, vraw, cos2, sin2, qw, kw, gq, gk):
    q = qraw * lax.rsqrt(_group_mean(qraw * qraw, gq) + NORM_EPS) * qw
    q = _rope32(q, jnp.tile(cos2, (1, 4)), jnp.tile(sin2, (1, 4))) * (ATTN_HD ** -0.5 * LOG2E)
    k = kraw * lax.rsqrt(_group_mean(kraw * kraw, gk) + NORM_EPS) * kw
    return q, _rope32(k, cos2, sin2), vraw


def _f_ssd_finish(yf, yb, xs, z, d_exp, nw):
    y = (yf + yb + d_exp * xs) * (z * jax.nn.sigmoid(z))
    return (_rms(y, nw),)


def _f_ret_prep(rq, rk, cos1, sin1):
    cos_full, sin_signed = jnp.tile(cos1, (1, 4)), jnp.tile(sin1, (1, 4))
    return _rope64(rq, cos_full, sin_signed), _rope64(rk, cos_full, sin_signed) * (RET_DK ** -0.5)


def _f_ret_finish(yf, yb, g, gw):
    y = yf + yb
    outs = []
    for h in range(RET_HEADS):
        yh = y[:, h * RET_DK:(h + 1) * RET_DK]
        yc = yh - jnp.mean(yh, axis=-1, keepdims=True)
        outs.append(yc * lax.rsqrt(jnp.mean(yc * yc, axis=-1, keepdims=True) + NORM_EPS))
    return (jnp.concatenate(outs, axis=1) * gw * (g * jax.nn.sigmoid(g)),)


def _f_merge(p0, p1, p2, g0, g1, g2):
    return (jax.nn.sigmoid(g0) * p0 + jax.nn.sigmoid(g1) * p1 + jax.nn.sigmoid(g2) * p2,)


def _f_mid(h, mix, g1, sh2, sc2, w2):
    h_mid = h + g1 * mix
    return h_mid, _rms(h_mid, w2) * (1.0 + sc2) + sh2


def _f_sqrelu(a):
    r = jnp.maximum(a, 0.0)
    return (r * r,)


def _f_residual(h_mid, o, g2):
    return (h_mid + g2 * o,)


def _f_silu(x):
    return (x * jax.nn.sigmoid(x),)


def _f_bias(x, b):
    return (x + b,)


def _ssd_rows(xbc, p):
    rows = [Cols(xbc, LANES * k, LANES) for k in range(4)]
    rows += [Cols(xbc, 512 + LANES * g, LANES) for g in range(2)]
    rows += [Cols(xbc, 768 + LANES * g, LANES) for g in range(2)]
    return rows + [_pcol(p, "dt")]


def _ret_rows(rq, rk, p):
    off_v = IN_LAYOUT["rv"][0]
    return ([Cols(rq, LANES * h, LANES) for h in range(4)] + [Cols(rk, LANES * h, LANES) for h in range(4)]
            + [Cols(p, off_v + LANES * h, LANES) for h in range(4)])


def layer_fwd(li, h, mod, lw, tabs, m_ctx):
    t = h.shape[0]
    nb = m_ctx // min(ROW_TILE, t)
    sh1, sc1, g1, sh2, sc2, g2 = mod
    nm = lambda s: f"l{li}_{s}"
    sv = {}
    (u,) = rowwise_fwd(nm("norm1"), _f_norm_mod, [h], [sh1, sc1], [lw["norm1_w"]], [(D_MODEL, CDT)], t, nb)
    p = mm(nm("in_proj"), u, lw["w_in"], F32)
    q, k, v = rowwise_fwd(
        nm("attn_prep"), _f_attn_prep,
        [_pcol(p, "q"), _pcol(p, "k"), _pcol(p, "v"), tabs["ca"], tabs["sa"]], [],
        [lw["qw"], lw["kw"], tabs["gq"], tabs["gk"]], [(512, CDT), (128, CDT), (128, CDT)], t, nb)
    tq = min(ATTN_TQ, m_ctx)
    kk, vv = _split_kv(k), _split_kv(v)
    attn_o, qT, oT, lse = attn_fwd(nm("attn"), q, kk, vv.transpose(0, 2, 1), m_ctx, tq)

    xbc = conv_fwd(_pcol(p, "xbc"), lw["conv_w"], lw["conv_b"], m_ctx)
    ssd_sh = [lw["dt_bias"], lw["a_log"]]
    yf, yb, sf, sb = scan_fwd(nm("ssd"), _make_ssd_chunk, _ssd_rows(xbc, p), ssd_sh, 4, 512, t, m_ctx)
    (ssd_o,) = rowwise_fwd(nm("ssd_fin"), _f_ssd_finish, [yf, yb, Cols(xbc, 0, 512), _pcol(p, "z")], [],
                           [lw["d_exp"], lw["ssd_nw"]], [(512, CDT)], t, nb)

    rq, rk = rowwise_fwd(nm("ret_prep"), _f_ret_prep, [_pcol(p, "rq"), _pcol(p, "rk"), tabs["rc"], tabs["rs"]],
                         [], [], [(512, F32), (512, F32)], t, nb)
    rf, rb, rsf, rsb = scan_fwd(nm("ret"), _make_ret_chunk, _ret_rows(rq, rk, p), [lw["ret_lg"]], 4, 512, t, m_ctx)
    (ret_o,) = rowwise_fwd(nm("ret_fin"), _f_ret_finish, [rf, rb, _pcol(p, "rg")], [], [lw["ret_gw"]],
                           [(512, CDT)], t, nb)

    pbs = [mm(nm(f"branch{b}"), br, lw["w_branch"][b], CDT) for b, br in enumerate((attn_o, ssd_o, ret_o))]
    gl = [Cols(p, 1024 * b, 1024) for b in range(3)]
    (merged,) = rowwise_fwd(nm("merge"), _f_merge, pbs + gl, [], [], [(D_MODEL, CDT)], t, nb)
    mix = mm(nm("out_proj"), merged, lw["w_out"], F32)
    h_mid, vv2 = rowwise_fwd(nm("mid"), _f_mid, [h, mix], [g1, sh2, sc2], [lw["norm2_w"]],
                             [(D_MODEL, F32), (D_MODEL, CDT)], t, nb)
    a = mm(nm("mlp1"), vv2, lw["w_mlp1"], CDT)
    (hh,) = rowwise_fwd(nm("sqrelu"), _f_sqrelu, [a], [], [], [(a.shape[1], CDT)], t, nb)
    o = mm(nm("mlp2"), hh, lw["w_mlp2"], F32)
    (h_out,) = rowwise_fwd(nm("resid"), _f_residual, [h_mid, o], [g2], [], [(D_MODEL, F32)], t, nb)
    sv.update(h=h, u=u, p=p, qT=qT, kk=kk, vv=vv, oT=oT, lse=lse, attn_o=attn_o, xbc=xbc, yf=yf, yb=yb,
              sf=sf, sb=sb, ssd_o=ssd_o, rq=rq, rk=rk, rf=rf, rb=rb, rsf=rsf, rsb=rsb, ret_o=ret_o, pbs=pbs,
              merged=merged, mix=mix, h_mid=h_mid, v=vv2, a=a, hh=hh, o=o)
    return h_out, sv


def layer_bwd(li, dh_out, sv, mod, lw, tabs, m_ctx):
    t = dh_out.shape[0]
    nb = m_ctx // min(ROW_TILE, t)
    sh1, sc1, g1, sh2, sc2, g2 = mod
    nm = lambda s: f"l{li}_{s}_bwd"
    gw = {}
    p = sv["p"]
    (do,), (dg2,), _ = rowwise_bwd(nm("resid"), _f_residual, [sv["h_mid"], sv["o"]], [g2], [], [dh_out],
                                   [False, True], [], [CDT], t, nb)
    dhh = mm(nm("mlp2_dx"), do, lw["w_mlp2"], CDT, transpose_b=True)
    gw["w_mlp2"] = mm_tn(nm("mlp2_dw"), sv["hh"], do, CDT, ("rows", lw["w_mlp2"].shape[0] // 4))
    (da,), _, _ = rowwise_bwd(nm("sqrelu"), _f_sqrelu, [sv["a"]], [], [], [dhh], [True], [], [CDT], t, nb)
    dv = mm(nm("mlp1_dx"), da, lw["w_mlp1"], F32, transpose_b=True)
    gw["w_mlp1"] = mm_tn(nm("mlp1_dw"), sv["v"], da, CDT, ("cols", lw["w_mlp1"].shape[1] // 4))
    (dh_a, dmix), (dg1, dsh2, dsc2), (gw["norm2_w"],) = rowwise_bwd(
        nm("mid"), _f_mid, [sv["h"], sv["mix"]], [g1, sh2, sc2], [lw["norm2_w"]], [dh_out, dv],
        [True, True], [True], [F32, CDT], t, nb)
    dmerged = mm(nm("out_dx"), dmix, lw["w_out"], CDT, transpose_b=True)
    gw["w_out"] = mm_tn(nm("out_dw"), sv["merged"], dmix, CDT, ("rows", lw["w_out"].shape[0] // 4))
    gl = [Cols(p, 1024 * b, 1024) for b in range(3)]
    dmg, _, _ = rowwise_bwd(nm("merge"), _f_merge, sv["pbs"] + gl, [], [], [dmerged], [True] * 6, [], [CDT] * 6,
                            t, nb)
    dpb, dgl = dmg[:3], dmg[3:]
    brs = (sv["attn_o"], sv["ssd_o"], sv["ret_o"])
    d_attn_o = mm(nm("branch0_dx"), dpb[0], lw["w_branch"][0], CDT, transpose_b=True)
    d_ssd_o = mm(nm("branch1_dx"), dpb[1], lw["w_branch"][1], F32, transpose_b=True)
    d_ret_o = mm(nm("branch2_dx"), dpb[2], lw["w_branch"][2], F32, transpose_b=True)
    n_loc = lw["w_branch"].shape[2] // 4
    gw["w_branch"] = jnp.stack([mm_tn(nm(f"branch{b}_dw"), brs[b], dpb[b], CDT, ("cols", n_loc)) for b in range(3)],
                               axis=1).reshape(4, -1, n_loc)
    tq = min(ATTN_TQ, m_ctx)
    dq_rows, dk_s, dv_s = attn_bwd(nm("attn"), sv["qT"], d_attn_o, sv["oT"], sv["lse"], sv["kk"],
                                   sv["kk"].transpose(0, 2, 1), sv["vv"], m_ctx)
    (dq_raw, dk_raw, dv_raw), _, (gw["qw"], gw["kw"]) = rowwise_bwd(
        nm("attn_prep"), _f_attn_prep,
        [_pcol(p, "q"), _pcol(p, "k"), _pcol(p, "v"), tabs["ca"], tabs["sa"]], [],
        [lw["qw"], lw["kw"], tabs["gq"], tabs["gk"]],
        [dq_rows, _merge_kv(dk_s) * LN2, _merge_kv(dv_s)],
        [True, True, True, False, False], [True, True, False, False], [CDT] * 3, t, nb)
    (dy_ssd, dxs_fin, dz), _, (gw["d_exp"], gw["ssd_nw"]) = rowwise_bwd(
        nm("ssd_fin"), _f_ssd_finish, [sv["yf"], sv["yb"], Cols(sv["xbc"], 0, 512), _pcol(p, "z")], [],
        [lw["d_exp"], lw["ssd_nw"]], [d_ssd_o], [True, False, True, True], [True, True], [F32, F32, CDT], t, nb)
    ssd_sh = [lw["dt_bias"], lw["a_log"]]
    post_ssd = lambda d: [jnp.concatenate(d[0:8], axis=1), d[8]]
    (dxbc_f, ddt_f), dsh_f = scan_bwd(nm("ssd_f"), _make_ssd_chunk, _ssd_rows(sv["xbc"], p), ssd_sh, (sv["sf"],),
                                      dy_ssd, post_ssd, [(1024, F32), (LANES, F32)], 4, t, m_ctx, dirs=(0,))
    (dxbc_b, ddt_b), dsh_b = scan_bwd(nm("ssd_b"), _make_ssd_chunk, _ssd_rows(sv["xbc"], p), ssd_sh, (sv["sb"],),
                                      dy_ssd, post_ssd, [(1024, F32), (LANES, F32)], 4, t, m_ctx, dirs=(1,))
    gw["dt_bias"], gw["a_log"] = dsh_f[0] + dsh_b[0], dsh_f[1] + dsh_b[1]
    ddt = (ddt_f + ddt_b).astype(CDT)
    dxbc_raw, gw["conv_w"], gw["conv_b"] = conv_bwd(_pcol(p, "xbc"), lw["conv_w"], lw["conv_b"], dxbc_f, dxbc_b,
                                                    dxs_fin, m_ctx)
    (dy_ret, drg), _, (gw["ret_gw"],) = rowwise_bwd(
        nm("ret_fin"), _f_ret_finish, [sv["rf"], sv["rb"], _pcol(p, "rg")], [], [lw["ret_gw"]], [d_ret_o],
        [True, False, True], [True], [F32, CDT], t, nb)
    post_ret = lambda d: [jnp.concatenate(d[0:4], axis=1), jnp.concatenate(d[4:8], axis=1),
                          jnp.concatenate(d[8:12], axis=1)]
    rrows = _ret_rows(sv["rq"], sv["rk"], p)
    (dq_f, dk_f, dv_f), (dq_b, dk_b, dv_b), (gw["ret_lg"],) = scan_bwd(
        nm("ret"), _make_ret_chunk, rrows, [lw["ret_lg"]], (sv["rsf"], sv["rsb"]), dy_ret, post_ret,
        [(512, F32)] * 3, 4, t, m_ctx)
    drv = (dv_f + dv_b).astype(CDT)
    (drq, drk), _, _ = rowwise_bwd(nm("ret_prep"), _f_ret_prep,
                                   [_pcol(p, "rq"), _pcol(p, "rk"), tabs["rc"], tabs["rs"]], [], [],
                                   [(dq_f, dq_b), (dk_f, dk_b)], [True, True, False, False], [], [CDT, CDT], t, nb)
    pieces = {"gates": None, "xbc": dxbc_raw, "q": dq_raw, "z": dz, "rq": drq, "rk": drk, "rv": drv, "rg": drg,
              "k": dk_raw, "v": dv_raw, "dt": ddt}
    cols = list(dgl) + [pieces[n] for n in IN_NEW_ORDER[1:]]
    used = sum(c.shape[1] for c in cols)
    cols.append(jnp.zeros((t, IN_PAD - used), CDT))
    dp = jnp.concatenate(cols, axis=1)
    du = mm(nm("in_dx"), dp, lw["w_in"], F32, transpose_b=True)
    gw["w_in"] = mm_tn(nm("in_dw"), sv["u"], dp, CDT)
    (dh_in,), (dsh1, dsc1), (gw["norm1_w"],) = rowwise_bwd(
        nm("norm1"), _f_norm_mod_thru, [sv["h"]], [sh1, sc1], [lw["norm1_w"]], [dh_a, du], [True], [True], [F32],
        t, nb)
    return dh_in, [dsh1, dsc1, dg1, dsh2, dsc2, dg2], gw


def _rope_tables(n_lat, m_ctx):
    rows = n_lat // GRID_W
    row = jnp.repeat(jnp.arange(rows, dtype=F32), GRID_W)
    col = jnp.tile(jnp.arange(GRID_W, dtype=F32), rows)
    nfreq = ATTN_HD // 4
    inv = ROPE_THETA ** (-jnp.arange(nfreq, dtype=F32) / nfreq)
    ang = jnp.concatenate([row[:, None] * inv, col[:, None] * inv], axis=-1)
    cos = jnp.concatenate([jnp.ones((m_ctx, ATTN_HD // 2), F32), jnp.cos(ang)], axis=0)
    sin = jnp.concatenate([jnp.zeros((m_ctx, ATTN_HD // 2), F32), jnp.sin(ang)], axis=0)
    c64 = jnp.concatenate([cos, cos], axis=1)
    s64 = jnp.concatenate([-sin, sin], axis=1)
    pos = jnp.arange(m_ctx + n_lat, dtype=F32)
    inv_r = ROPE_THETA ** (-jnp.linspace(0.0, 1.0, RET_DK // 2, dtype=F32))
    ang_r = pos[:, None] * inv_r
    rc = jnp.concatenate([jnp.cos(ang_r)] * 2, axis=1)
    rs = jnp.concatenate([-jnp.sin(ang_r), jnp.sin(ang_r)], axis=1)
    return dict(ca=jnp.tile(c64, (1, 2)), sa=jnp.tile(s64, (1, 2)), rc=rc, rs=rs, gq=_group_matrix(512, ATTN_HD),
                gk=_group_matrix(128, ATTN_HD))


def _layer_weights(full, small, layer):
    return dict(
        w_in=full["w_in"][layer], w_branch=full["w_branch"][layer], w_out=full["w_out"][layer],
        w_mlp1=full["w_mlp1"][layer], w_mlp2=full["w_mlp2"][layer],
        norm1_w=small["norm1_w"][layer][None], norm2_w=small["norm2_w"][layer][None],
        qw=jnp.tile(small["attn_q_norm"][layer], 8)[None], kw=jnp.tile(small["attn_k_norm"][layer], 2)[None],
        conv_w=small["ssd_conv_w"][layer], conv_b=small["ssd_conv_b"][layer][None],
        dt_bias=_lane_pad(small["ssd_dt_bias"][layer]), a_log=_lane_pad(small["ssd_a_log"][layer]),
        d_exp=jnp.repeat(small["ssd_d"][layer], SSD_HD)[None], ssd_nw=small["ssd_norm_w"][layer][None],
        ret_lg=_lane_pad(small["ret_log_decay"][layer]), ret_gw=small["ret_gn_w"][layer][None])


def local_step(x, c, ctx, full, small, loss_target):
    n_lat, d = x.shape
    m_ctx = ctx.shape[0]
    t = n_lat + m_ctx
    depth = small["norm1_w"].shape[0]
    tabs = _rope_tables(n_lat, m_ctx)
    h = jnp.concatenate([ctx, x], axis=0)
    cc = jnp.concatenate([small["c_ctx"][None], c, jnp.zeros((COND_ROWS - 2, d), F32)], axis=0)
    (scc,) = rowwise_fwd("cond_silu", _f_silu, [cc], [], [], [(d, CDT)], COND_ROWS, 0)
    mods, saved, lws = [], [], []
    for layer in range(depth):
        lw = _layer_weights(full, small, layer)
        mod_raw = mm(f"l{layer}_mod", scc, full["w_mod"][layer], F32)
        (mod8,) = rowwise_fwd(f"l{layer}_mod_bias", _f_bias, [mod_raw], [], [small["b_mod"][layer][None]],
                              [(6 * d, F32)], COND_ROWS, 0)
        mod = [mod8[0:2, k * d:(k + 1) * d].reshape(2, 1, d) for k in range(6)]
        h, sv = layer_fwd(layer, h, mod, lw, tabs, m_ctx)
        mods.append(mod)
        saved.append(sv)
        lws.append(lw)
    loss, dh, d_final = loss_head(h, loss_target, small["final_norm_w"][None], m_ctx)

    gbig = {k: [None] * depth for k in BIG}
    gs = {k: [None] * depth for k in SMALL if k not in ("c_ctx", "final_norm_w")}
    d_scc = None
    for layer in reversed(range(depth)):
        lw = lws[layer]
        dh, dmod, gw = layer_bwd(layer, dh, saved[layer], mods[layer], lw, tabs, m_ctx)
        dmod8 = jnp.concatenate([jnp.concatenate([g_.reshape(2, d) for g_ in dmod], axis=1),
                                 jnp.zeros((COND_ROWS - 2, 6 * d), F32)], axis=0)
        (dmod_c,), _, (db_mod,) = rowwise_bwd(f"l{layer}_mod_bias_bwd", _f_bias, [dmod8], [],
                                              [small["b_mod"][layer][None]], [dmod8], [True], [True], [CDT], COND_ROWS, 0)
        gbig["w_mod"][layer] = mm_tn(f"l{layer}_mod_dw", scc, dmod_c, CDT, ("cols", 6 * d // 4))
        part = mm(f"l{layer}_mod_dx", dmod_c, full["w_mod"][layer], F32, transpose_b=True)
        d_scc = part if d_scc is None else d_scc + part
        g_in = _in_from_padded(gw["w_in"])
        gbig["w_in"][layer] = g_in.reshape(d, 4, g_in.shape[1] // 4).transpose(1, 0, 2)
        for k in ("w_branch", "w_out", "w_mlp1", "w_mlp2"):
            gbig[k][layer] = gw[k]
        gs["b_mod"][layer] = db_mod.reshape(-1)
        gs["norm1_w"][layer] = gw["norm1_w"].reshape(-1)
        gs["norm2_w"][layer] = gw["norm2_w"].reshape(-1)
        gs["attn_q_norm"][layer] = gw["qw"].reshape(8, ATTN_HD).sum(0)
        gs["attn_k_norm"][layer] = gw["kw"].reshape(2, ATTN_HD).sum(0)
        gs["ssd_conv_w"][layer] = gw["conv_w"]
        gs["ssd_conv_b"][layer] = gw["conv_b"].reshape(-1)
        gs["ssd_dt_bias"][layer] = gw["dt_bias"][0, :16].reshape(2, 8)
        gs["ssd_a_log"][layer] = gw["a_log"][0, :16].reshape(2, 8)
        gs["ssd_d"][layer] = gw["d_exp"].reshape(SSD_HEADS, SSD_HD).sum(1)
        gs["ssd_norm_w"][layer] = gw["ssd_nw"].reshape(-1)
        gs["ret_log_decay"][layer] = gw["ret_lg"][0, :8].reshape(2, 4)
        gs["ret_gn_w"][layer] = gw["ret_gw"].reshape(-1)
    (d_cc,), _, _ = rowwise_bwd("cond_silu_bwd", _f_silu, [cc], [], [], [d_scc], [True], [], [F32], COND_ROWS, 0)
    g_small = {k: jnp.stack(v) for k, v in gs.items()}
    g_small["c_ctx"] = d_cc[0]
    g_small["final_norm_w"] = d_final.reshape(-1)
    g_big = {k: jnp.stack(v) for k, v in gbig.items()}
    return loss, dh[m_ctx:], g_big, g_small


def kernel(x, c, ctx, c_ctx, w_mod, b_mod, norm1_w, norm2_w, w_in, attn_q_norm, attn_k_norm, ssd_conv_w, ssd_conv_b, ssd_dt_bias, ssd_a_log, ssd_d, ssd_norm_w, ret_log_decay, ret_gn_w, w_branch, w_out, w_mlp1, w_mlp2, final_norm_w, loss_target, m_c_ctx, m_w_mod, m_b_mod, m_norm1_w, m_norm2_w, m_w_in, m_attn_q_norm, m_attn_k_norm, m_ssd_conv_w, m_ssd_conv_b, m_ssd_dt_bias, m_ssd_a_log, m_ssd_d, m_ssd_norm_w, m_ret_log_decay, m_ret_gn_w, m_w_branch, m_w_out, m_w_mlp1, m_w_mlp2, m_final_norm_w, v_c_ctx, v_w_mod, v_b_mod, v_norm1_w, v_norm2_w, v_w_in, v_attn_q_norm, v_attn_k_norm, v_ssd_conv_w, v_ssd_conv_b, v_ssd_dt_bias, v_ssd_a_log, v_ssd_d, v_ssd_norm_w, v_ret_log_decay, v_ret_gn_w, v_w_branch, v_w_out, v_w_mlp1, v_w_mlp2, v_final_norm_w):
    env = dict(locals())
    w_loc = {k: env[k] for k in WEIGHTS}
    m_loc = {k: env["m_" + k] for k in WEIGHTS}
    v_loc = {k: env["v_" + k] for k in WEIGHTS}
    chip = 2 * lax.axis_index("x") + lax.axis_index("y")
    core = lax.axis_index("c")

    depth = w_loc["w_mod"].shape[0]
    assert depth == 2, "the exchanges split the layers between a chip's two cores"
    shards = [w_loc[k].astype(CDT).reshape(depth, -1, w_loc[k].shape[-1]) for k in BIG]
    mine = gather_layers("gather_weights", shards, [BIG_KIND[k] for k in BIG])
    full = {}
    for k, arr in zip(BIG, mine):
        both = exchange_both("share_" + k, arr.reshape(-1, arr.shape[-1]))
        if k == "w_in":
            both = both.reshape(depth, 4, -1, both.shape[-1]).transpose(0, 2, 1, 3)
            both = _in_to_padded(both.reshape(depth, both.shape[1], -1))
        full[k] = both.reshape((depth,) + w_loc[k].shape[1:-1] + (-1,)) if BIG_KIND[k] == "cols" else \
            both.reshape((depth,) + w_loc[k].shape[1:-2] + (-1, w_loc[k].shape[-1])) if BIG_KIND[k] == "rows" else both

    cw = w_loc["ssd_conv_w"]
    cw_w = cw.shape[-1]
    placed = lax.dynamic_update_slice(jnp.zeros(cw.shape[:-1] + (4 * cw_w,), F32),
                                      cw * (core == 0).astype(F32), (0, 0, chip * cw_w))
    conv_full = _unpack_flat(allreduce_small("gather_conv_w", _pack_flat([placed], F32, LANES)), [placed.shape])[0]
    small = {k: w_loc[k] for k in SMALL}
    small["ssd_conv_w"] = conv_full

    loss_l, grad_x, g_big, g_small = local_step(x[0], c, ctx[0], full, small, loss_target[0])

    small_shapes = [g_small[k].shape for k in SMALL] + [(LANES,)]
    summed = _unpack_flat(allreduce_small("reduce_small", _pack_flat([g_small[k] for k in SMALL] + [loss_l], F32, LANES)),
                          small_shapes)
    gsum = dict(zip(SMALL, summed[:-1]))
    loss = summed[-1][0]
    gsum["ssd_conv_w"] = lax.dynamic_slice(gsum["ssd_conv_w"], (0, 0, chip * cw_w), cw.shape)

    pair = []
    for k in BIG:
        _, _, rows, cols = g_big[k].shape
        pair.append(exchange_add("pair_" + k, g_big[k].reshape(depth, 4 * rows, cols)).reshape(4, rows, cols))
    landed = scatter_pieces("scatter_grads", pair)
    g_sum = [sum_exchange("sum_" + k, parts) for k, parts in zip(BIG, landed)]

    grads, deltas, new_m, new_v = {}, {}, {}, {}
    for i, k in enumerate(BIG):
        shp = w_loc[k].shape
        three_d = lambda a, shp=shp: a.reshape((-1,) + shp[-2:])
        res = adamw("adamw_" + k, three_d(w_loc[k]), three_d(m_loc[k]), three_d(v_loc[k]), [three_d(g_sum[i])])
        grads[k], deltas[k], new_m[k], new_v[k] = [r.reshape(shp) for r in res]
    small_loc_shapes = [w_loc[k].shape for k in SMALL]
    res = adamw("adamw_small", _pack_flat([w_loc[k] for k in SMALL], F32, LANES)[None],
                _pack_flat([m_loc[k] for k in SMALL], F32, LANES)[None],
                _pack_flat([v_loc[k] for k in SMALL], F32, LANES)[None],
                [_pack_flat([gsum[k] for k in SMALL], F32, LANES)[None]])
    for dst, r in zip((grads, deltas, new_m, new_v), res):
        dst.update(dict(zip(SMALL, _unpack_flat(r, small_loc_shapes))))

    return (loss, grad_x[None], *[grads[k] for k in WEIGHTS], *[deltas[k] for k in WEIGHTS],
            *[new_m[k] for k in WEIGHTS], *[new_v[k] for k in WEIGHTS])
```

```python
import functools
import math
from typing import NamedTuple

import jax
import jax.numpy as jnp
from jax import lax
from jax.experimental import pallas as pl
from jax.experimental.pallas import tpu as pltpu

F32 = jnp.float32
CDT = jnp.bfloat16
NORM_EPS = 1e-6
ROPE_THETA = 10000.0
GRID_W = 64
D_MODEL = 1024
ATTN_HEADS, ATTN_KV, ATTN_HD = 8, 2, 64
SSD_HEADS, SSD_HD, SSD_STATE = 8, 64, 128
RET_HEADS, RET_DK = 4, 128
CHUNK = 128
ROW_TILE = 256
MM_ROWS = 768
MM_VMEM_BUDGET = 44 * 1024 * 1024
ATTN_TQ, ATTN_TK = 256, 256
ATTN_TK_BWD = 2048
LOG2E, LN2 = 1.4426950408889634, 0.6931471805599453
ATTN_TK_FWD = 2048
LANES = 128
PAIR_BLOCK_BYTES = 2 * 1024 * 1024
COND_ROWS = 16
VMEM_LIMIT = 56 * 1024 * 1024

ADAM_LR, ADAM_B1, ADAM_B2, ADAM_EPS, ADAM_WD, ADAM_STEP = 0.001, 0.9, 0.999, 1e-08, 0.01, 10

IN_LAYOUT = {
    "gates": (0, 3072, 4368, 3072), "xbc": (3072, 1024, 1280, 1024), "q": (4096, 512, 0, 512),
    "z": (4608, 512, 768, 512), "rq": (5120, 512, 2320, 512), "rk": (5632, 512, 2832, 512),
    "rv": (6144, 512, 3344, 512), "rg": (6656, 512, 3856, 512), "k": (7168, 128, 512, 128),
    "v": (7296, 128, 640, 128), "dt": (7424, 128, 2304, 16),
}
IN_PAD = 7680
IN_ORIG_ORDER = ("q", "k", "v", "z", "xbc", "dt", "rq", "rk", "rv", "rg", "gates")
IN_NEW_ORDER = ("gates", "xbc", "q", "z", "rq", "rk", "rv", "rg", "k", "v", "dt")

BIG = ("w_mod", "w_in", "w_branch", "w_out", "w_mlp1", "w_mlp2")
BIG_KIND = {"w_mod": "cols", "w_in": "slices", "w_branch": "cols", "w_out": "rows", "w_mlp1": "cols", "w_mlp2": "rows"}
SMALL = ("c_ctx", "b_mod", "norm1_w", "norm2_w", "attn_q_norm", "attn_k_norm", "ssd_conv_w", "ssd_conv_b",
         "ssd_dt_bias", "ssd_a_log", "ssd_d", "ssd_norm_w", "ret_log_decay", "ret_gn_w", "final_norm_w")
WEIGHTS = ("c_ctx", "w_mod", "b_mod", "norm1_w", "norm2_w", "w_in", "attn_q_norm", "attn_k_norm", "ssd_conv_w",
           "ssd_conv_b", "ssd_dt_bias", "ssd_a_log", "ssd_d", "ssd_norm_w", "ret_log_decay", "ret_gn_w",
           "w_branch", "w_out", "w_mlp1", "w_mlp2", "final_norm_w")


def _cp(sem):
    return pltpu.CompilerParams(dimension_semantics=sem, vmem_limit_bytes=VMEM_LIMIT)


class Cols(NamedTuple):
    arr: jax.Array
    off: int
    width: int


def _width(item):
    return item.width if isinstance(item, Cols) else item.shape[1]


def _row_in(item, rows, imap=None):
    imap = imap or (lambda i: i)
    if isinstance(item, Cols):
        assert item.off % item.width == 0
        blk = item.off // item.width
        return item.arr, pl.BlockSpec((rows, item.width), lambda i, blk=blk: (imap(i), blk))
    return item, pl.BlockSpec((rows, item.shape[1]), lambda i: (imap(i), 0))


def _const_spec(shape):
    return pl.BlockSpec(shape, lambda *_: (0,) * len(shape))


def _mxu(a, b, dims=(((1,), (0,)), ((), ()))):
    return lax.dot_general(a.astype(CDT), b.astype(CDT), dims, preferred_element_type=F32)


_NT = (((1,), (1,)), ((), ()))
_TN = (((0,), (0,)), ((), ()))


@jax.custom_vjp
def _softplus(x):
    return jnp.maximum(x, 0.0) + jnp.log1p(jnp.exp(-jnp.abs(x)))


def _softplus_fwd(x):
    return _softplus(x), x


def _softplus_bwd(x, g):
    return (g * jax.nn.sigmoid(x),)


_softplus.defvjp(_softplus_fwd, _softplus_bwd)


def _group_mean_impl(x, gmat):
    hi = x.astype(CDT)
    lo = (x - hi.astype(F32)).astype(CDT)
    return (jnp.dot(hi, gmat, preferred_element_type=F32) + jnp.dot(lo, gmat, preferred_element_type=F32))


@jax.custom_vjp
def _group_mean(x, gmat):
    return _group_mean_impl(x, gmat)


def _group_mean_fwd(x, gmat):
    return _group_mean_impl(x, gmat), gmat


def _group_mean_bwd(gmat, g):
    return _group_mean_impl(g, gmat), jnp.zeros_like(gmat)


_group_mean.defvjp(_group_mean_fwd, _group_mean_bwd)


def _group_matrix(width, group):
    r = jnp.arange(width) // group
    return jnp.where(r[:, None] == r[None, :], 1.0 / group, 0.0).astype(CDT)


def _make_rope(half):
    def partner(x):
        w = x.shape[1]
        lane = lax.broadcasted_iota(jnp.int32, x.shape, 1)
        first = (lane % (2 * half)) < half
        return jnp.where(first, pltpu.roll(x, w - half, axis=1), pltpu.roll(x, half, axis=1))

    def impl(x, cos_full, sin_signed):
        return x * cos_full + partner(x) * sin_signed

    @jax.custom_vjp
    def rope(x, cos_full, sin_signed):
        return impl(x, cos_full, sin_signed)

    def fwd(x, cos_full, sin_signed):
        return impl(x, cos_full, sin_signed), (cos_full, sin_signed)

    def bwd(res, g):
        cos_full, sin_signed = res
        return impl(g, cos_full, -sin_signed), jnp.zeros_like(cos_full), jnp.zeros_like(sin_signed)

    rope.defvjp(fwd, bwd)
    return rope


_rope32 = _make_rope(32)
_rope64 = _make_rope(64)


def _rms(x, w):
    return x * lax.rsqrt(jnp.mean(x * x, axis=-1, keepdims=True) + NORM_EPS) * w


def _col(v, lane_index):
    lane = lax.broadcasted_iota(jnp.int32, v.shape, 1)
    return jnp.sum(jnp.where(lane == lane_index, v, 0.0), axis=1, keepdims=True)


def _typed_spec(width, nb_ctx):
    return pl.BlockSpec((None, 1, width), lambda i: (jnp.where(i >= nb_ctx, 1, 0), 0, 0))


def rowwise_fwd(name, f, rows, typed, shared, outs, n_rows, nb_ctx, tm=ROW_TILE):
    tm = min(tm, n_rows)
    nin = len(rows) + len(typed) + len(shared)

    def body(*refs):
        res = f(*[r[...] for r in refs[:nin]])
        for o_ref, o in zip(refs[nin:], res):
            o_ref[...] = o.astype(o_ref.dtype)

    arrs, specs = [], []
    for it in rows:
        a, s = _row_in(it, tm)
        arrs.append(a)
        specs.append(s)
    for t in typed:
        arrs.append(t)
        specs.append(_typed_spec(t.shape[-1], nb_ctx))
    for s_ in shared:
        arrs.append(s_)
        specs.append(_const_spec(s_.shape))
    res = pl.pallas_call(
        body, name=name, grid=(n_rows // tm,), in_specs=specs,
        out_specs=[pl.BlockSpec((tm, w), lambda i: (i, 0)) for w, _ in outs],
        out_shape=[jax.ShapeDtypeStruct((n_rows, w), dt) for w, dt in outs],
        compiler_params=_cp(("parallel",)))(*arrs)
    return res


def rowwise_bwd(name, f, rows, typed, shared, cots, row_diff, shared_diff, drow_dtypes, n_rows, nb_ctx, tm=ROW_TILE):
    tm = min(tm, n_rows)
    cot_groups = [c_ if isinstance(c_, tuple) else (c_,) for c_ in cots]
    cots = [a for grp in cot_groups for a in grp]
    nr, nt, ns, nc = len(rows), len(typed), len(shared), len(cots)
    nin = nr + nt + ns
    d_rows = [k for k in range(nr) if row_diff[k]]
    d_sh = [k for k in range(ns) if shared_diff[k]]

    def body(*refs):
        rvals = [r[...] for r in refs[:nr]]
        tvals = [r[...] for r in refs[nr:nr + nt]]
        svals = [r[...] for r in refs[nr + nt:nin]]
        cparts = [r[...].astype(F32) for r in refs[nin:nin + nc]]
        cvals = []
        for grp in cot_groups:
            cvals.append(sum(cparts[1:len(grp)], cparts[0]))
            cparts = cparts[len(grp):]
        out_refs = refs[nin + nc:]

        def g(*dv):
            dv = list(dv)
            rv = list(rvals)
            for k in d_rows:
                rv[k] = dv.pop(0)
            tv = [dv.pop(0) for _ in range(nt)]
            sv = list(svals)
            for k in d_sh:
                sv[k] = dv.pop(0)
            return tuple(o.astype(F32) for o in f(*rv, *tv, *sv))

        prim = [rvals[k].astype(F32) for k in d_rows] + tvals + [svals[k] for k in d_sh]
        _, vjp = jax.vjp(g, *prim)
        grads = list(vjp(tuple(cvals)))
        i = pl.program_id(0)
        for ref in out_refs[:len(d_rows)]:
            ref[...] = grads.pop(0).astype(ref.dtype)
        first_typed = (i == 0) | (i == nb_ctx)
        for ref in out_refs[len(d_rows):len(d_rows) + nt]:
            gr = grads.pop(0)

            @pl.when(first_typed)
            def _(ref=ref, gr=gr):
                ref[...] = gr

            @pl.when(jnp.logical_not(first_typed))
            def _(ref=ref, gr=gr):
                ref[...] += gr
        for ref in out_refs[len(d_rows) + nt:]:
            gr = grads.pop(0)

            @pl.when(i == 0)
            def _(ref=ref, gr=gr):
                ref[...] = gr

            @pl.when(i != 0)
            def _(ref=ref, gr=gr):
                ref[...] += gr

    arrs, specs = [], []
    for it in list(rows):
        a, s = _row_in(it, tm)
        arrs.append(a)
        specs.append(s)
    for t in typed:
        arrs.append(t)
        specs.append(_typed_spec(t.shape[-1], nb_ctx))
    for s_ in shared:
        arrs.append(s_)
        specs.append(_const_spec(s_.shape))
    for c_ in cots:
        a, s = _row_in(c_, tm)
        arrs.append(a)
        specs.append(s)
    out_specs, out_shape = [], []
    for k, dt in zip(d_rows, drow_dtypes):
        w = _width(rows[k])
        out_specs.append(pl.BlockSpec((tm, w), lambda i: (i, 0)))
        out_shape.append(jax.ShapeDtypeStruct((n_rows, w), dt))
    for t in typed:
        out_specs.append(_typed_spec(t.shape[-1], nb_ctx))
        out_shape.append(jax.ShapeDtypeStruct(t.shape, F32))
    for k in d_sh:
        out_specs.append(_const_spec(shared[k].shape))
        out_shape.append(jax.ShapeDtypeStruct(shared[k].shape, F32))
    res = pl.pallas_call(body, name=name, grid=(n_rows // tm,), in_specs=specs, out_specs=out_specs,
                         out_shape=out_shape, compiler_params=_cp(("arbitrary",)))(*arrs)
    n1, n2 = len(d_rows), len(d_rows) + nt
    return list(res[:n1]), list(res[n1:n2]), list(res[n2:])


def _pick(n, prefs):
    for p in prefs:
        if n % p == 0:
            return p
    return n


def mm(name, a, b, out_dtype, transpose_b=False):
    n, k = b.shape if transpose_b else b.shape[::-1]
    m = (a.arr if isinstance(a, Cols) else a).shape[0]
    assert _width(a) == k
    tm = _pick(m, (MM_ROWS, 256))
    osz = jnp.dtype(out_dtype).itemsize
    tn = next(c for c in (2560, 2048, 1536, 1024, 512, 256, 128, n)
              if n % c == 0 and 2 * (tm * k * 2 + k * c * 2 + tm * c * osz) <= MM_VMEM_BUDGET or c == n)
    dims = _NT if transpose_b else (((1,), (0,)), ((), ()))

    def body(a_ref, b_ref, o_ref):
        o_ref[...] = lax.dot_general(a_ref[...], b_ref[...], dims, preferred_element_type=F32).astype(o_ref.dtype)

    a_arr, a_spec = _row_in(a, tm)
    a_spec = pl.BlockSpec(a_spec.block_shape, lambda j, i, f=a_spec.index_map: f(i))
    b_spec = pl.BlockSpec((tn, k), lambda j, i: (j, 0)) if transpose_b else pl.BlockSpec((k, tn), lambda j, i: (0, j))
    return pl.pallas_call(
        body, name=name, grid=(n // tn, m // tm), in_specs=[a_spec, b_spec],
        out_specs=pl.BlockSpec((tm, tn), lambda j, i: (i, j)),
        out_shape=jax.ShapeDtypeStruct((m, n), out_dtype),
        compiler_params=_cp(("parallel", "parallel")))(a_arr, b)


def mm_tn(name, a, b, out_dtype=F32, pieces=None):
    t = (a.arr if isinstance(a, Cols) else a).shape[0]
    k, n = _width(a), _width(b)
    tt = _pick(t, (MM_ROWS, 256))
    k_unit = pieces[1] if pieces and pieces[0] == "rows" else k
    n_unit = pieces[1] if pieces and pieces[0] == "cols" else n
    tk = _pick(k_unit, (1024, 512, 256, 128))
    tn = _pick(n_unit, (1280, 1024, 512, 256, 128))
    n_t = t // tt

    def body(a_ref, b_ref, o_ref, acc):
        part = lax.dot_general(a_ref[...], b_ref[...], _TN, preferred_element_type=F32)
        ti = pl.program_id(2)

        @pl.when(ti == 0)
        def _():
            acc[...] = part

        @pl.when(ti != 0)
        def _():
            acc[...] += part

        @pl.when(ti == n_t - 1)
        def _():
            o_ref[...] = acc[...].astype(o_ref.dtype)

    def win(item, width):
        if isinstance(item, Cols):
            assert item.off % width == 0
            return item.arr, item.off // width
        return item, 0

    a_arr, a0 = win(a, tk)
    b_arr, b0 = win(b, tn)
    if pieces is None:
        out_spec = pl.BlockSpec((tk, tn), lambda ki, ni, ti: (ki, ni))
        out_shape = (k, n)
    elif pieces[0] == "cols":
        per = n_unit // tn
        out_spec = pl.BlockSpec((None, tk, tn), lambda ki, ni, ti: (ni // per, ki, ni % per))
        out_shape = (4, k, n_unit)
    else:
        per = k_unit // tk
        out_spec = pl.BlockSpec((None, tk, tn), lambda ki, ni, ti: (ki // per, ki % per, ni))
        out_shape = (4, k_unit, n)
    return pl.pallas_call(
        body, name=name, grid=(k // tk, n // tn, n_t),
        in_specs=[pl.BlockSpec((tt, tk), lambda ki, ni, ti: (ti, a0 + ki)),
                  pl.BlockSpec((tt, tn), lambda ki, ni, ti: (ti, b0 + ni))],
        out_specs=out_spec, out_shape=jax.ShapeDtypeStruct(out_shape, out_dtype),
        scratch_shapes=[pltpu.VMEM((tk, tn), F32)],
        compiler_params=_cp(("parallel", "parallel", "arbitrary")))(a_arr, b_arr)


def _heads_t(rows_blk):
    blk = rows_blk.astype(F32).T
    return jnp.concatenate([blk[hh * ATTN_HD:(hh + 1) * ATTN_HD, :] for hh in range(4)], axis=1)


def _heads_rows(t_blk):
    tq = t_blk.shape[1] // 4
    return jnp.concatenate([t_blk[:, hh * tq:(hh + 1) * tq] for hh in range(4)], axis=0).T


def attn_fwd(name, q, kk, vT, m_ctx, tq):
    t, hd = kk.shape[1], ATTN_HD
    nq, r = t // tq, 4 * tq
    tk = _pick(t - m_ctx, (ATTN_TK_FWD, ATTN_TK))
    nqc, n_lat_tiles = m_ctx // tq, (t - m_ctx) // tk

    def body(q_ref, k_ref, vT_ref, o_ref, qT_ref, oT_ref, lse_ref):
        i = pl.program_id(1)
        q_t = _heads_t(q_ref[...]).astype(CDT)
        qT_ref[...] = q_t

        def tile(off, size, carry):
            mi, li, acc = carry
            sub = min(size, ATTN_TK)
            offs = [off + u * sub for u in range(size // sub)]
            sts = [jnp.dot(k_ref[pl.ds(o, sub), :], q_t, preferred_element_type=F32) for o in offs]
            for o, st in zip(offs, sts):
                mn = jnp.maximum(mi, jnp.max(st, axis=0, keepdims=True))
                pt = jnp.exp2(st - mn)
                al = jnp.exp2(mi - mn)
                li = al * li + jnp.sum(pt, axis=0, keepdims=True)
                acc = al * acc + jnp.dot(vT_ref[:, pl.ds(o, sub)], pt.astype(CDT), preferred_element_type=F32)
                mi = mn
            return mi, li, acc

        init = (jnp.full((1, r), -1e30, F32), jnp.zeros((1, r), F32), jnp.zeros((hd, r), F32))
        carry = tile(0, m_ctx, init)
        mi, li, acc = lax.fori_loop(
            0, jnp.where(i < nqc, 0, n_lat_tiles),
            lambda j, cr: tile(pl.multiple_of(m_ctx + j * tk, ATTN_TK), tk, cr), carry)
        o_t = acc / li
        oT_ref[...] = o_t.astype(oT_ref.dtype)
        o_ref[...] = _heads_rows(o_t).astype(o_ref.dtype)
        lse_ref[...] = mi + jnp.log2(li)

    blk_t = pl.BlockSpec((None, None, hd, r), lambda g, i: (g, i, 0, 0))
    rows = pl.BlockSpec((tq, 4 * hd), lambda g, i: (i, g))
    return pl.pallas_call(
        body, name=name, grid=(2, nq),
        in_specs=[rows, pl.BlockSpec((None, t, hd), lambda g, i: (g, 0, 0)),
                  pl.BlockSpec((None, hd, t), lambda g, i: (g, 0, 0))],
        out_specs=[rows, blk_t, blk_t, pl.BlockSpec((None, None, 1, r), lambda g, i: (g, i, 0, 0))],
        out_shape=[jax.ShapeDtypeStruct((t, 8 * hd), CDT), jax.ShapeDtypeStruct((2, nq, hd, r), CDT),
                   jax.ShapeDtypeStruct((2, nq, hd, r), CDT), jax.ShapeDtypeStruct((2, nq, 1, r), F32)],
        compiler_params=_cp(("parallel", "arbitrary")))(q, kk, vT)


def attn_bwd(name, qT, do, oT, lse, kk, kT, vv, m_ctx):
    _, nq, hd, r = qT.shape
    t = kk.shape[1]
    tq = r // 4
    tk = _pick(t - m_ctx, (ATTN_TK_BWD, ATTN_TK))
    nqc, n_lat_tiles = m_ctx // tq, (t - m_ctx) // tk

    def body(qT_ref, do_ref, oT_ref, lse_ref, k_ref, kT_ref, v_ref, dq_ref, dk_ref, dv_ref):
        i = pl.program_id(1)

        @pl.when(i == 0)
        def _():
            dk_ref[...] = jnp.zeros_like(dk_ref)
            dv_ref[...] = jnp.zeros_like(dv_ref)

        q_t = qT_ref[...]
        do_f = _heads_t(do_ref[...])
        do_t = do_f.astype(CDT)
        lse = lse_ref[...]
        delta = jnp.sum(do_f * oT_ref[...].astype(F32), axis=0, keepdims=True)

        def tile(off, size, dq):
            sub = min(size, ATTN_TK)
            offs = [off + u * sub for u in range(size // sub)]
            sts = [jnp.dot(k_ref[pl.ds(o, sub), :], q_t, preferred_element_type=F32) for o in offs]
            dpts = [jnp.dot(v_ref[pl.ds(o, sub), :], do_t, preferred_element_type=F32) for o in offs]
            dvs, dks = [], []
            for o, st, dpt in zip(offs, sts, dpts):
                pt = jnp.exp2(st - lse)
                dvs.append(lax.dot_general(do_t, pt.astype(CDT), _NT, preferred_element_type=F32))
                dst = (pt * (dpt - delta)).astype(CDT)
                dks.append(lax.dot_general(q_t, dst, _NT, preferred_element_type=F32))
                dq = dq + jnp.dot(kT_ref[:, pl.ds(o, sub)], dst, preferred_element_type=F32)
            dv_ref[:, pl.ds(off, size)] += jnp.concatenate(dvs, axis=1)
            dk_ref[:, pl.ds(off, size)] += jnp.concatenate(dks, axis=1)
            return dq

        dq = tile(0, m_ctx, jnp.zeros((hd, r), F32))
        dq = lax.fori_loop(0, jnp.where(i < nqc, 0, n_lat_tiles),
                           lambda j, acc: tile(pl.multiple_of(m_ctx + j * tk, ATTN_TK), tk, acc), dq)
        dq_ref[...] = _heads_rows(dq * LN2)

    blk_t = pl.BlockSpec((None, None, hd, r), lambda g, i: (g, i, 0, 0))
    row = pl.BlockSpec((None, None, 1, r), lambda g, i: (g, i, 0, 0))
    kv = pl.BlockSpec((None, t, hd), lambda g, i: (g, 0, 0))
    rows = pl.BlockSpec((tq, 4 * hd), lambda g, i: (i, g))
    kv_t = pl.BlockSpec((None, hd, t), lambda g, i: (g, 0, 0))
    return pl.pallas_call(
        body, name=name, grid=(2, nq),
        in_specs=[blk_t, rows, blk_t, row, kv, kv_t, kv],
        out_specs=[rows, kv_t, kv_t],
        out_shape=[jax.ShapeDtypeStruct((t, 8 * hd), F32), jax.ShapeDtypeStruct((2, hd, t), F32),
                   jax.ShapeDtypeStruct((2, hd, t), F32)],
        compiler_params=_cp(("parallel", "arbitrary")))(qT, do, oT, lse, kk, kT, vv)


def _split_kv(a):
    return a.reshape(a.shape[0], 2, ATTN_HD).transpose(1, 0, 2)


def _merge_kv_t(a):
    return a.transpose(2, 0, 1).reshape(a.shape[2], 2 * ATTN_HD)


def _chunk_order(rev, ncc, nct):
    if not rev:
        return lambda s: s
    return lambda s: jnp.where(s < ncc, ncc - 1 - s, nct - 1 - (s - ncc))


def scan_fwd(name, make_fn, rows, shared, n_state, y_width, n_rows, m_ctx):
    nct, ncc = n_rows // CHUNK, m_ctx // CHUNK
    orders = [_chunk_order(rev, ncc, nct) for rev in (False, True)]
    fns = [make_fn(0), make_fn(1)]
    nr, ns = len(rows), len(shared)

    def body(*refs):
        svals = [r[...] for r in refs[2 * nr:2 * nr + ns]]
        y_refs, sin_refs, st = refs[2 * nr + ns:2 * nr + ns + 2], refs[2 * nr + ns + 2:2 * nr + ns + 4], refs[-1]

        @pl.when(pl.program_id(0) == 0)
        def _():
            st[...] = jnp.zeros_like(st)

        for d in range(2):
            rvals = [r[...] for r in refs[d * nr:(d + 1) * nr]]
            prev = [st[d, k] for k in range(n_state)]
            sin_refs[d][...] = st[d]
            y, new = fns[d](rvals, svals, prev)
            y_refs[d][...] = y
            for k in range(n_state):
                st[d, k] = new[k]

    arrs, specs = [], []
    for order in orders:
        for it in rows:
            a, s = _row_in(it, CHUNK, order)
            arrs.append(a)
            specs.append(s)
    for s_ in shared:
        arrs.append(s_)
        specs.append(_const_spec(s_.shape))
    return pl.pallas_call(
        body, name=name, grid=(nct,), in_specs=specs,
        out_specs=[pl.BlockSpec((CHUNK, y_width), lambda s, o=o: (o(s), 0)) for o in orders]
        + [pl.BlockSpec((None, n_state, LANES, LANES), lambda s, o=o: (o(s), 0, 0, 0)) for o in orders],
        out_shape=[jax.ShapeDtypeStruct((n_rows, y_width), F32)] * 2
        + [jax.ShapeDtypeStruct((nct, n_state, LANES, LANES), F32)] * 2,
        scratch_shapes=[pltpu.VMEM((2, n_state, LANES, LANES), F32)],
        compiler_params=_cp(("arbitrary",)))(*arrs)


def scan_bwd(name, make_fn, rows, shared, states_in, dy, post, outs, n_state, n_rows, m_ctx, dirs=(0, 1)):
    nct, ncc = n_rows // CHUNK, m_ctx // CHUNK
    orders = [(lambda r, f=_chunk_order(d == 1, ncc, nct): f(nct - 1 - r)) for d in dirs]
    fns = [make_fn(d) for d in dirs]
    nd = len(dirs)
    nr, ns, no = len(rows), len(shared), len(outs)
    n_in = nd * nr + ns

    def body(*refs):
        svals = [r[...] for r in refs[nd * nr:n_in]]
        sin_refs, dy_refs = refs[n_in:n_in + nd], refs[n_in + nd:n_in + 2 * nd]
        out_refs = refs[n_in + 2 * nd:n_in + 2 * nd + nd * no]
        dsh_refs = refs[n_in + 2 * nd + nd * no:-1]
        dst = refs[-1]
        r = pl.program_id(0)

        @pl.when(r == 0)
        def _():
            dst[...] = jnp.zeros_like(dst)

        d_shared = None
        for d in range(nd):
            rvals = [x[...] for x in refs[d * nr:(d + 1) * nr]]
            prev = [sin_refs[d][k] for k in range(n_state)]
            _, vjp = jax.vjp(fns[d], rvals, svals, prev)
            d_rows, d_sh, d_prev = vjp((dy_refs[d][...], [dst[d, k] for k in range(n_state)]))
            for ref, val in zip(out_refs[d * no:(d + 1) * no], post(d_rows)):
                ref[...] = val.astype(ref.dtype)
            d_shared = d_sh if d_shared is None else [a + b for a, b in zip(d_shared, d_sh)]
            for k in range(n_state):
                dst[d, k] = d_prev[k]
        for ref, gr in zip(dsh_refs, d_shared):
            @pl.when(r == 0)
            def _(ref=ref, gr=gr):
                ref[...] = gr

            @pl.when(r != 0)
            def _(ref=ref, gr=gr):
                ref[...] += gr

    arrs, specs = [], []
    for order in orders:
        for it in rows:
            a, s = _row_in(it, CHUNK, order)
            arrs.append(a)
            specs.append(s)
    for s_ in shared:
        arrs.append(s_)
        specs.append(_const_spec(s_.shape))
    for sin, order in zip(states_in, orders):
        arrs.append(sin)
        specs.append(pl.BlockSpec((None, n_state, LANES, LANES), lambda r, o=order: (o(r), 0, 0, 0)))
    for order in orders:
        a, s = _row_in(dy, CHUNK, order)
        arrs.append(a)
        specs.append(s)
    out_specs = [pl.BlockSpec((CHUNK, w), lambda r, o=o: (o(r), 0)) for o in orders for w, _ in outs]
    out_shape = [jax.ShapeDtypeStruct((n_rows, w), dt) for _ in orders for w, dt in outs]
    for s_ in shared:
        out_specs.append(_const_spec(s_.shape))
        out_shape.append(jax.ShapeDtypeStruct(s_.shape, F32))
    res = pl.pallas_call(body, name=name, grid=(nct,), in_specs=specs, out_specs=out_specs, out_shape=out_shape,
                         scratch_shapes=[pltpu.VMEM((nd, n_state, LANES, LANES), F32)],
                         compiler_params=_cp(("arbitrary",)))(*arrs)
    return [list(res[d * no:(d + 1) * no]) for d in range(nd)] + [list(res[nd * no:])]


def _make_ssd_chunk(direction):
    rev = direction == 1
    base = 8 * direction

    def fn(rows, shared, prev):
        xs, bms, cms, dtraw = rows[0:4], rows[4:6], rows[6:8], rows[8]
        dt_bias, a_log = shared
        ln = dtraw.shape[0]
        dt_all = _softplus(dtraw + dt_bias)
        a_all = dt_all * (-jnp.exp(a_log))
        r_i = lax.broadcasted_iota(jnp.int32, (ln, ln), 0)
        c_i = lax.broadcasted_iota(jnp.int32, (ln, ln), 1)
        tri = (r_i <= c_i) if rev else (r_i >= c_i)
        a_cum_all = jnp.dot(tri.astype(F32), a_all, precision=lax.Precision.HIGHEST, preferred_element_type=F32)
        a_tot_all = jnp.sum(a_all, axis=0, keepdims=True)
        first = lax.broadcasted_iota(jnp.int32, (ln, LANES), 1) < SSD_HD
        first_row = lax.broadcasted_iota(jnp.int32, (LANES, 1), 0) < SSD_HD

        def lmat(acol):
            a_b = jnp.broadcast_to(acol, (ln, ln))
            seg = a_b - a_b.T
            return jnp.where(tri, jnp.exp(jnp.where(tri, seg, 0.0)), 0.0)

        ys, new = [], []
        for g in range(2):
            bm, cm = bms[g], cms[g]
            cb = _mxu(cm, bm, _NT)
            for jj in range(2):
                pr = 2 * g + jj
                h0, h1 = base + 2 * pr, base + 2 * pr + 1
                ac0, ac1 = _col(a_cum_all, h0), _col(a_cum_all, h1)
                at0, at1 = _col(a_tot_all, h0), _col(a_tot_all, h1)
                dt_pair = jnp.where(first, _col(dt_all, h0), _col(dt_all, h1))
                acum_pair = jnp.where(first, ac0, ac1)
                atot_pair = jnp.where(first[0:1], at0, at1)
                xd = xs[pr] * dt_pair
                st = _mxu(xd * jnp.exp(atot_pair - acum_pair), bm, _TN)
                new.append(prev[pr] * jnp.where(first_row, jnp.exp(at0), jnp.exp(at1)) + st)
                y0 = _mxu(cb * lmat(ac0), xd)
                y1 = _mxu(cb * lmat(ac1), xd)
                y_off = _mxu(cm, prev[pr], _NT) * jnp.exp(acum_pair)
                ys.append(jnp.where(first, y0, y1) + y_off)
        return jnp.concatenate(ys, axis=1), new

    return fn


def _make_ret_chunk(direction):
    rev = direction == 1
    base = 4 * direction

    def fn(rows, shared, prev):
        qs, ks, vs = rows[0:4], rows[4:8], rows[8:12]
        lg_all = -jnp.exp(shared[0])
        ln = qs[0].shape[0]
        pos = lax.broadcasted_iota(jnp.int32, (ln, 1), 0).astype(F32)
        r_i = lax.broadcasted_iota(jnp.int32, (ln, ln), 0)
        c_i = lax.broadcasted_iota(jnp.int32, (ln, ln), 1)
        diff = ((c_i - r_i) if rev else (r_i - c_i))
        mask = diff >= 0
        dpos = jnp.maximum(diff, 0).astype(F32)
        k_pow = pos if rev else (ln - 1.0 - pos)
        q_pow = (ln - pos) if rev else (pos + 1.0)
        ys, new = [], []
        for h in range(RET_HEADS):
            lg = _col(lg_all, base + h)
            dmat = jnp.where(mask, jnp.exp(dpos * lg), 0.0)
            st = _mxu(ks[h] * jnp.exp(k_pow * lg), vs[h], _TN)
            new.append(prev[h] * jnp.exp(ln * lg) + st)
            s = _mxu(qs[h], ks[h], _NT) * dmat
            ys.append(_mxu(s, vs[h]) + _mxu(qs[h], prev[h]) * jnp.exp(q_pow * lg))
        return jnp.concatenate(ys, axis=1), new

    return fn


def _conv_pre(x, w, b, t_idx, n_rows, m_ctx):
    is_start = (t_idx == 0) | (t_idx == m_ctx)
    is_end = (t_idx == m_ctx - 1) | (t_idx == n_rows - 1)
    xp = jnp.where(is_start, 0.0, pltpu.roll(x, 1, axis=0))
    xn = jnp.where(is_end, 0.0, pltpu.roll(x, n_rows - 1, axis=0))
    return w[0:1] * xp + w[1:2] * x + w[2:3] * xn + b, xp, xn, is_start, is_end


def conv_fwd(x, conv_w, conv_b, m_ctx):
    n_rows, width = x.arr.shape[0], x.width
    c0 = x.off // LANES

    def body(x_ref, w_ref, b_ref, o_ref):
        t_idx = lax.broadcasted_iota(jnp.int32, (n_rows, 1), 0)
        pre = _conv_pre(x_ref[...], w_ref[...], b_ref[...], t_idx, n_rows, m_ctx)[0]
        o_ref[...] = pre * jax.nn.sigmoid(pre)

    return pl.pallas_call(
        body, name="conv_fwd", grid=(width // LANES,),
        in_specs=[pl.BlockSpec((n_rows, LANES), lambda c: (0, c0 + c)),
                  pl.BlockSpec((3, LANES), lambda c: (0, c)), pl.BlockSpec((1, LANES), lambda c: (0, c))],
        out_specs=pl.BlockSpec((n_rows, LANES), lambda c: (0, c)),
        out_shape=jax.ShapeDtypeStruct((n_rows, width), F32),
        compiler_params=_cp(("parallel",)))(x.arr, conv_w, conv_b)


def conv_bwd(x, conv_w, conv_b, dy_a, dy_b, dxs_extra, m_ctx):
    n_rows, width = x.arr.shape[0], x.width
    c0 = x.off // LANES
    n_extra = dxs_extra.shape[1] // LANES

    def body(x_ref, w_ref, b_ref, dya_ref, dyb_ref, ex_ref, dx_ref, dw_ref, db_ref):
        c = pl.program_id(0)
        t_idx = lax.broadcasted_iota(jnp.int32, (n_rows, 1), 0)
        w = w_ref[...]
        pre, xp, xn, is_start, is_end = _conv_pre(x_ref[...], w, b_ref[...], t_idx, n_rows, m_ctx)
        sg = jax.nn.sigmoid(pre)
        dyv = dya_ref[...] + dyb_ref[...] + jnp.where(c < n_extra, ex_ref[...], 0.0)
        dpre = dyv * (sg * (1.0 + pre * (1.0 - sg)))
        d_next = jnp.where(is_end, 0.0, pltpu.roll(dpre, n_rows - 1, axis=0))
        d_prev = jnp.where(is_start, 0.0, pltpu.roll(dpre, 1, axis=0))
        dx_ref[...] = (w[1:2] * dpre + w[0:1] * d_next + w[2:3] * d_prev).astype(dx_ref.dtype)
        dw_ref[...] = jnp.concatenate([jnp.sum(dpre * xp, axis=0, keepdims=True),
                                       jnp.sum(dpre * x_ref[...], axis=0, keepdims=True),
                                       jnp.sum(dpre * xn, axis=0, keepdims=True)], axis=0)
        db_ref[...] = jnp.sum(dpre, axis=0, keepdims=True)

    return pl.pallas_call(
        body, name="conv_bwd", grid=(width // LANES,),
        in_specs=[pl.BlockSpec((n_rows, LANES), lambda c: (0, c0 + c)),
                  pl.BlockSpec((3, LANES), lambda c: (0, c)), pl.BlockSpec((1, LANES), lambda c: (0, c)),
                  pl.BlockSpec((n_rows, LANES), lambda c: (0, c)), pl.BlockSpec((n_rows, LANES), lambda c: (0, c)),
                  pl.BlockSpec((n_rows, LANES), lambda c: (0, jnp.minimum(c, n_extra - 1)))],
        out_specs=[pl.BlockSpec((n_rows, LANES), lambda c: (0, c)),
                   pl.BlockSpec((3, LANES), lambda c: (0, c)), pl.BlockSpec((1, LANES), lambda c: (0, c))],
        out_shape=[jax.ShapeDtypeStruct((n_rows, width), CDT), jax.ShapeDtypeStruct((3, width), F32),
                   jax.ShapeDtypeStruct((1, width), F32)],
        compiler_params=_cp(("parallel",)))(x.arr, conv_w, conv_b, dy_a, dy_b, dxs_extra)


def loss_head(h, target, final_w, m_ctx):
    n_rows, d = h.shape
    tm = min(ROW_TILE, n_rows)
    nb_ctx = m_ctx // tm

    def f(hb, w, tgt):
        err = _rms(hb, w) - tgt
        return 0.5 * jnp.sum(jnp.mean(err * err, axis=-1))

    def body(h_ref, t_ref, w_ref, loss_ref, dh_ref, dw_ref):
        i = pl.program_id(0)

        @pl.when(i < nb_ctx)
        def _():
            dh_ref[...] = jnp.zeros_like(dh_ref)

        @pl.when(i == 0)
        def _():
            loss_ref[...] = jnp.zeros_like(loss_ref)
            dw_ref[...] = jnp.zeros_like(dw_ref)

        @pl.when(i >= nb_ctx)
        def _():
            val, vjp = jax.vjp(lambda hb, w: f(hb, w, t_ref[...]), h_ref[...], w_ref[...])
            dh, dw = vjp(jnp.ones((), F32))
            dh_ref[...] = dh
            dw_ref[...] += dw
            loss_ref[...] += jnp.broadcast_to(val, loss_ref.shape)

    return pl.pallas_call(
        body, name="loss_head", grid=(n_rows // tm,),
        in_specs=[pl.BlockSpec((tm, d), lambda i: (i, 0)),
                  pl.BlockSpec((tm, d), lambda i: (jnp.maximum(i - nb_ctx, 0), 0)), _const_spec((1, d))],
        out_specs=[_const_spec((1, LANES)), pl.BlockSpec((tm, d), lambda i: (i, 0)), _const_spec((1, d))],
        out_shape=[jax.ShapeDtypeStruct((1, LANES), F32), jax.ShapeDtypeStruct((n_rows, d), F32),
                   jax.ShapeDtypeStruct((1, d), F32)],
        compiler_params=_cp(("arbitrary",)))(h, target, final_w)


def adamw(name, w, m, v, g_parts):
    lead, rows, cols = w.shape
    tr = _pick(rows, (256, 128, 64, 32, 16, 8))
    npart = len(g_parts)
    c1 = 1.0 - ADAM_B1 ** ADAM_STEP
    c2 = 1.0 - ADAM_B2 ** ADAM_STEP

    def body(*refs):
        w_ref, m_ref, v_ref = refs[:3]
        g = refs[3][...].astype(F32)
        for r in refs[4:3 + npart]:
            g = g + r[...].astype(F32)
        g_ref, d_ref, nm_ref, nv_ref = refs[3 + npart:]
        nm = ADAM_B1 * m_ref[...] + (1.0 - ADAM_B1) * g
        nv = ADAM_B2 * v_ref[...] + (1.0 - ADAM_B2) * (g * g)
        g_ref[...] = g
        nm_ref[...] = nm
        nv_ref[...] = nv
        d_ref[...] = -ADAM_LR * ((nm / c1) / (jnp.sqrt(nv / c2) + ADAM_EPS) + ADAM_WD * w_ref[...])

    spec = pl.BlockSpec((None, tr, cols), lambda l, i: (l, i, 0))
    return pl.pallas_call(
        body, name=name, grid=(lead, rows // tr), in_specs=[spec] * (3 + npart), out_specs=[spec] * 4,
        out_shape=[jax.ShapeDtypeStruct(w.shape, F32)] * 4,
        compiler_params=_cp(("parallel", "parallel")))(w, m, v, *g_parts)


MESH = pl.DeviceIdType.MESH
_HBM = pl.BlockSpec(memory_space=pl.ANY)


def _chip_peers():
    x, y, c = lax.axis_index("x"), lax.axis_index("y"), lax.axis_index("c")
    return x, y, c, [(1 - x, y), (x, 1 - y), (1 - x, 1 - y)]


def _window(ref, kind, chip, rows, cols):
    if kind == "cols":
        return ref.at[:, pl.ds(pl.multiple_of(chip * cols, LANES), cols)]
    if kind == "rows":
        return ref.at[pl.ds(pl.multiple_of(chip * rows, 8), rows), :]
    return ref.at[chip]


def _gathered_shape(kind, rows, cols):
    return {"cols": (rows, 4 * cols), "rows": (4 * rows, cols), "slices": (4, rows, cols)}[kind]


def gather_layers(name, shards, kinds):
    n = len(shards)

    def body(*refs):
        x_refs, o_refs = refs[:n], refs[n:2 * n]
        send_sems, recv_sems, local_sems = refs[2 * n:]
        x, y, c, peers = _chip_peers()
        me = 2 * x + y
        started = []
        for a in range(n):
            _, rows, cols = shards[a].shape
            src = x_refs[a].at[c]
            mine = pltpu.make_async_copy(src, _window(o_refs[a], kinds[a], me, rows, cols), local_sems.at[a])
            mine.start()
            started.append(mine.wait)
            for k, (px, py) in enumerate(peers):
                cp = pltpu.make_async_remote_copy(
                    src_ref=src, dst_ref=_window(o_refs[a], kinds[a], me, rows, cols), send_sem=send_sems.at[3 * a + k],
                    recv_sem=recv_sems.at[3 * a + k], device_id=(px, py, c), device_id_type=MESH)
                cp.start()
                started.append(cp.wait_send)
        for a in range(n):
            _, rows, cols = shards[a].shape
            for k, (px, py) in enumerate(peers):
                pltpu.make_async_remote_copy(
                    src_ref=x_refs[a].at[c], dst_ref=_window(o_refs[a], kinds[a], 2 * px + py, rows, cols),
                    send_sem=send_sems.at[3 * a + k], recv_sem=recv_sems.at[3 * a + k], device_id=(px, py, c),
                    device_id_type=MESH).wait_recv()
        for wait in started:
            wait()

    return pl.pallas_call(
        body, name=name, in_specs=[_HBM] * n, out_specs=[_HBM] * n,
        out_shape=[jax.ShapeDtypeStruct(_gathered_shape(kinds[a], *shards[a].shape[1:]), shards[a].dtype)
                   for a in range(n)],
        scratch_shapes=[pltpu.SemaphoreType.DMA((3 * n,)), pltpu.SemaphoreType.DMA((3 * n,)),
                        pltpu.SemaphoreType.DMA((n,))],
        )(*shards)


def scatter_pieces(name, pieces):
    n = len(pieces)

    def body(*refs):
        p_refs, o_refs = refs[:n], refs[n:2 * n]
        send_sems, recv_sems, local_sems = refs[2 * n:]
        x, y, c, peers = _chip_peers()
        me = 2 * x + y
        started = []
        for a in range(n):
            mine = pltpu.make_async_copy(p_refs[a].at[me], o_refs[a].at[me], local_sems.at[a])
            mine.start()
            started.append(mine.wait)
            for k, (px, py) in enumerate(peers):
                cp = pltpu.make_async_remote_copy(
                    src_ref=p_refs[a].at[2 * px + py], dst_ref=o_refs[a].at[me], send_sem=send_sems.at[3 * a + k],
                    recv_sem=recv_sems.at[3 * a + k], device_id=(px, py, c), device_id_type=MESH)
                cp.start()
                started.append(cp.wait_send)
        for a in range(n):
            for k, (px, py) in enumerate(peers):
                pltpu.make_async_remote_copy(
                    src_ref=p_refs[a].at[me], dst_ref=o_refs[a].at[2 * px + py], send_sem=send_sems.at[3 * a + k],
                    recv_sem=recv_sems.at[3 * a + k], device_id=(px, py, c), device_id_type=MESH).wait_recv()
        for wait in started:
            wait()

    return pl.pallas_call(
        body, name=name, in_specs=[_HBM] * n, out_specs=[_HBM] * n,
        out_shape=[jax.ShapeDtypeStruct(p.shape, p.dtype) for p in pieces],
        scratch_shapes=[pltpu.SemaphoreType.DMA((3 * n,)), pltpu.SemaphoreType.DMA((3 * n,)),
                        pltpu.SemaphoreType.DMA((n,))],
        )(*pieces)


def _pair_step(n_steps, x_ref, land, send_sems, recv_sems, credits, consume):
    x, y, c = lax.axis_index("x"), lax.axis_index("y"), lax.axis_index("c")
    sib = (x, y, 1 - c)
    i = pl.program_id(0)
    slot = i % 2

    @pl.when(i >= 2)
    def _():
        pl.semaphore_wait(credits.at[slot], 1)

    cp = pltpu.make_async_remote_copy(src_ref=x_ref, dst_ref=land.at[slot], send_sem=send_sems.at[slot],
                                      recv_sem=recv_sems.at[slot], device_id=sib, device_id_type=MESH)
    cp.start()
    cp.wait_recv()
    consume(land[slot])

    @pl.when(i < n_steps - 2)
    def _():
        pl.semaphore_signal(credits.at[slot], inc=1, device_id=sib, device_id_type=MESH)

    cp.wait_send()


def _pair_call(name, body, n_steps, in_specs, out_spec, out_shape, blk_shape, dtype, operands, extra_scratch=()):
    grid_spec = pltpu.PrefetchScalarGridSpec(
        num_scalar_prefetch=1, grid=(n_steps,), in_specs=in_specs, out_specs=out_spec,
        scratch_shapes=[pltpu.VMEM((2,) + blk_shape, dtype), pltpu.SemaphoreType.DMA((2,)),
                        pltpu.SemaphoreType.DMA((2,)), pltpu.SemaphoreType.REGULAR((2,)), *extra_scratch])
    return pl.pallas_call(body, name=name, grid_spec=grid_spec, out_shape=out_shape,
                          compiler_params=_cp(("arbitrary",)))(*operands)


def _place():
    return jnp.stack([lax.axis_index("x"), lax.axis_index("y"), lax.axis_index("c")]).astype(jnp.int32)


def _pair_rows(rows, row_bytes):
    for cand in (4096, 2048, 1024, 768, 512, 384, 256, 192, 128, 96, 64, 48, 32, 16):
        if rows % cand == 0 and cand * row_bytes <= PAIR_BLOCK_BYTES:
            return cand
    return _pick(rows, (16, 8))


def exchange_both(name, mine):
    rows, cols = mine.shape
    tr = _pair_rows(rows, cols * mine.dtype.itemsize)
    n_steps = rows // tr

    def body(s_ref, x_ref, o_ref, land, send_sems, recv_sems, credits):
        c = lax.axis_index("c")
        o_ref[c] = x_ref[...]

        def consume(v):
            o_ref[1 - c] = v
        _pair_step(n_steps, x_ref, land, send_sems, recv_sems, credits, consume)

    return _pair_call(name, body, n_steps, [pl.BlockSpec((tr, cols), lambda i, s: (i, 0))],
                      pl.BlockSpec((2, tr, cols), lambda i, s: (0, i, 0)),
                      jax.ShapeDtypeStruct((2, rows, cols), mine.dtype), (tr, cols), mine.dtype, (_place(), mine))


def exchange_add(name, both_layers):
    _, rows, cols = both_layers.shape
    tr = _pair_rows(rows, cols * both_layers.dtype.itemsize)
    nb = rows // tr
    flat = both_layers.reshape(2 * rows, cols)

    def body(s_ref, x_ref, m_ref, o_ref, land, send_sems, recv_sems, credits):
        def consume(v):
            o_ref[...] = (m_ref[...].astype(F32) + v.astype(F32)).astype(o_ref.dtype)
        _pair_step(nb, x_ref, land, send_sems, recv_sems, credits, consume)

    return _pair_call(name, body, nb,
                      [pl.BlockSpec((tr, cols), lambda i, s: ((1 - s[2]) * nb + i, 0)),
                       pl.BlockSpec((tr, cols), lambda i, s: (s[2] * nb + i, 0))],
                      pl.BlockSpec((tr, cols), lambda i, s: (i, 0)), jax.ShapeDtypeStruct((rows, cols), CDT),
                      (tr, cols), flat.dtype, (_place(), flat, flat))


def sum_exchange(name, parts):
    npart, rows, cols = parts.shape
    tr = _pair_rows(rows, cols * 4)
    n_steps = rows // tr

    def body(s_ref, x_ref, o_ref, land, send_sems, recv_sems, credits, mine):
        c = lax.axis_index("c")
        acc = x_ref[0].astype(F32)
        for k in range(1, npart):
            acc = acc + x_ref[k].astype(F32)
        mine[...] = acc
        o_ref[c] = acc

        def consume(v):
            o_ref[1 - c] = v
        _pair_step(n_steps, mine, land, send_sems, recv_sems, credits, consume)

    return _pair_call(name, body, n_steps, [pl.BlockSpec((npart, tr, cols), lambda i, s: (0, i, 0))],
                      pl.BlockSpec((2, tr, cols), lambda i, s: (0, i, 0)),
                      jax.ShapeDtypeStruct((2, rows, cols), F32), (tr, cols), F32, (_place(), parts),
                      extra_scratch=(pltpu.VMEM((tr, cols), F32),))


def allreduce_small(name, buf):
    rows = buf.shape[0]

    def body(x_ref, out_ref, gath, send_sems, recv_sems):
        x, y, c = lax.axis_index("x"), lax.axis_index("y"), lax.axis_index("c")
        me = 4 * x + 2 * y + c
        masks = [(k >> 2 & 1, k >> 1 & 1, k & 1) for k in range(1, 8)]

        def flip(v, bit):
            return 1 - v if bit else v

        sends = []
        for k, (bx, by, bc) in enumerate(masks):
            cp = pltpu.make_async_remote_copy(src_ref=x_ref, dst_ref=gath.at[me], send_sem=send_sems.at[k],
                                              recv_sem=recv_sems.at[k],
                                              device_id=(flip(x, bx), flip(y, by), flip(c, bc)), device_id_type=MESH)
            cp.start()
            sends.append(cp)
        gath[me] = x_ref[...]
        for k, (bx, by, bc) in enumerate(masks):
            px, py, pc = flip(x, bx), flip(y, by), flip(c, bc)
            pltpu.make_async_remote_copy(src_ref=x_ref, dst_ref=gath.at[4 * px + 2 * py + pc],
                                         send_sem=send_sems.at[k], recv_sem=recv_sems.at[k],
                                         device_id=(px, py, pc), device_id_type=MESH).wait_recv()
        for cp in sends:
            cp.wait_send()
        acc = gath[0]
        for d in range(1, 8):
            acc = acc + gath[d]
        out_ref[...] = acc

    return pl.pallas_call(
        body, name=name, in_specs=[pl.BlockSpec(memory_space=pltpu.VMEM)],
        out_specs=pl.BlockSpec(memory_space=pltpu.VMEM), out_shape=jax.ShapeDtypeStruct(buf.shape, F32),
        scratch_shapes=[pltpu.VMEM((8, rows, LANES), F32), pltpu.SemaphoreType.DMA((7,)),
                        pltpu.SemaphoreType.DMA((7,))],
        )(buf)


def _pack_flat(arrs, dtype, width, row_mult=8):
    flat = jnp.concatenate([a.reshape(-1).astype(dtype) for a in arrs])
    pad = (-flat.shape[0]) % (row_mult * width)
    if pad:
        flat = jnp.concatenate([flat, jnp.zeros((pad,), dtype)])
    return flat.reshape(-1, width)


def _unpack_flat(buf, shapes):
    flat = buf.reshape(-1)
    out, off = [], 0
    for s in shapes:
        n = math.prod(s)
        out.append(flat[off:off + n].reshape(s))
        off += n
    return out


def _in_to_padded(w):
    parts = []
    for name in IN_NEW_ORDER:
        _, width, o_off, o_w = IN_LAYOUT[name]
        parts.append(w[..., o_off:o_off + o_w])
        if o_w < width:
            parts.append(jnp.zeros(w.shape[:-1] + (width - o_w,), w.dtype))
    used = sum(IN_LAYOUT[n][1] for n in IN_NEW_ORDER)
    parts.append(jnp.zeros(w.shape[:-1] + (IN_PAD - used,), w.dtype))
    return jnp.concatenate(parts, axis=-1)


def _in_from_padded(g):
    parts = []
    for name in IN_ORIG_ORDER:
        off, _, _, o_w = IN_LAYOUT[name]
        parts.append(g[..., off:off + o_w])
    return jnp.concatenate(parts, axis=-1)


def _pcol(p, name):
    off, width, _, _ = IN_LAYOUT[name]
    return Cols(p, off, width)


def _lane_pad(v, width=LANES):
    v = v.reshape(-1)
    return jnp.concatenate([v, jnp.zeros((width - v.shape[0],), v.dtype)]).reshape(1, width)


def _f_norm_mod(h, sh, sc, w):
    return (_rms(h, w) * (1.0 + sc) + sh,)


def _f_norm_mod_thru(h, sh, sc, w):
    return h, _rms(h, w) * (1.0 + sc) + sh


def _f_attn_prep(qraw, kraw, vraw, cos2, sin2, qw, kw, gq, gk):
    q = qraw * lax.rsqrt(_group_mean(qraw * qraw, gq) + NORM_EPS) * qw
    q = _rope32(q, jnp.tile(cos2, (1, 4)), jnp.tile(sin2, (1, 4))) * (ATTN_HD ** -0.5 * LOG2E)
    k = kraw * lax.rsqrt(_group_mean(kraw * kraw, gk) + NORM_EPS) * kw
    return q, _rope32(k, cos2, sin2), vraw


def _f_ssd_finish(yf, yb, xs, z, d_exp, nw):
    y = (yf + yb + d_exp * xs) * (z * jax.nn.sigmoid(z))
    return (_rms(y, nw),)


def _f_ret_prep(rq, rk, cos1, sin1):
    cos_full, sin_signed = jnp.tile(cos1, (1, 4)), jnp.tile(sin1, (1, 4))
    return _rope64(rq, cos_full, sin_signed), _rope64(rk, cos_full, sin_signed) * (RET_DK ** -0.5)


def _f_ret_finish(yf, yb, g, gw):
    y = yf + yb
    outs = []
    for h in range(RET_HEADS):
        yh = y[:, h * RET_DK:(h + 1) * RET_DK]
        yc = yh - jnp.mean(yh, axis=-1, keepdims=True)
        outs.append(yc * lax.rsqrt(jnp.mean(yc * yc, axis=-1, keepdims=True) + NORM_EPS))
    return (jnp.concatenate(outs, axis=1) * gw * (g * jax.nn.sigmoid(g)),)


def _f_merge(p0, p1, p2, g0, g1, g2):
    return (jax.nn.sigmoid(g0) * p0 + jax.nn.sigmoid(g1) * p1 + jax.nn.sigmoid(g2) * p2,)


def _f_mid(h, mix, g1, sh2, sc2, w2):
    h_mid = h + g1 * mix
    return h_mid, _rms(h_mid, w2) * (1.0 + sc2) + sh2


def _f_sqrelu(a):
    r = jnp.maximum(a, 0.0)
    return (r * r,)


def _f_residual(h_mid, o, g2):
    return (h_mid + g2 * o,)


def _f_silu(x):
    return (x * jax.nn.sigmoid(x),)


def _f_bias(x, b):
    return (x + b,)


def _ssd_rows(xbc, p):
    rows = [Cols(xbc, LANES * k, LANES) for k in range(4)]
    rows += [Cols(xbc, 512 + LANES * g, LANES) for g in range(2)]
    rows += [Cols(xbc, 768 + LANES * g, LANES) for g in range(2)]
    return rows + [_pcol(p, "dt")]


def _ret_rows(rq, rk, p):
    off_v = IN_LAYOUT["rv"][0]
    return ([Cols(rq, LANES * h, LANES) for h in range(4)] + [Cols(rk, LANES * h, LANES) for h in range(4)]
            + [Cols(p, off_v + LANES * h, LANES) for h in range(4)])


def layer_fwd(li, h, mod, lw, tabs, m_ctx):
    t = h.shape[0]
    nb = m_ctx // min(ROW_TILE, t)
    sh1, sc1, g1, sh2, sc2, g2 = mod
    nm = lambda s: f"l{li}_{s}"
    sv = {}
    (u,) = rowwise_fwd(nm("norm1"), _f_norm_mod, [h], [sh1, sc1], [lw["norm1_w"]], [(D_MODEL, CDT)], t, nb)
    p = mm(nm("in_proj"), u, lw["w_in"], F32)
    q, k, v = rowwise_fwd(
        nm("attn_prep"), _f_attn_prep,
        [_pcol(p, "q"), _pcol(p, "k"), _pcol(p, "v"), tabs["ca"], tabs["sa"]], [],
        [lw["qw"], lw["kw"], tabs["gq"], tabs["gk"]], [(512, CDT), (128, CDT), (128, CDT)], t, nb)
    tq = min(ATTN_TQ, m_ctx)
    kk, vv = _split_kv(k), _split_kv(v)
    attn_o, qT, oT, lse = attn_fwd(nm("attn"), q, kk, vv.transpose(0, 2, 1), m_ctx, tq)

    xbc = conv_fwd(_pcol(p, "xbc"), lw["conv_w"], lw["conv_b"], m_ctx)
    ssd_sh = [lw["dt_bias"], lw["a_log"]]
    yf, yb, sf, sb = scan_fwd(nm("ssd"), _make_ssd_chunk, _ssd_rows(xbc, p), ssd_sh, 4, 512, t, m_ctx)
    (ssd_o,) = rowwise_fwd(nm("ssd_fin"), _f_ssd_finish, [yf, yb, Cols(xbc, 0, 512), _pcol(p, "z")], [],
                           [lw["d_exp"], lw["ssd_nw"]], [(512, CDT)], t, nb)

    rq, rk = rowwise_fwd(nm("ret_prep"), _f_ret_prep, [_pcol(p, "rq"), _pcol(p, "rk"), tabs["rc"], tabs["rs"]],
                         [], [], [(512, F32), (512, F32)], t, nb)
    rf, rb, rsf, rsb = scan_fwd(nm("ret"), _make_ret_chunk, _ret_rows(rq, rk, p), [lw["ret_lg"]], 4, 512, t, m_ctx)
    (ret_o,) = rowwise_fwd(nm("ret_fin"), _f_ret_finish, [rf, rb, _pcol(p, "rg")], [], [lw["ret_gw"]],
                           [(512, CDT)], t, nb)

    pbs = [mm(nm(f"branch{b}"), br, lw["w_branch"][b], CDT) for b, br in enumerate((attn_o, ssd_o, ret_o))]
    gl = [Cols(p, 1024 * b, 1024) for b in range(3)]
    (merged,) = rowwise_fwd(nm("merge"), _f_merge, pbs + gl, [], [], [(D_MODEL, CDT)], t, nb)
    mix = mm(nm("out_proj"), merged, lw["w_out"], F32)
    h_mid, vv2 = rowwise_fwd(nm("mid"), _f_mid, [h, mix], [g1, sh2, sc2], [lw["norm2_w"]],
                             [(D_MODEL, F32), (D_MODEL, CDT)], t, nb)
    a = mm(nm("mlp1"), vv2, lw["w_mlp1"], CDT)
    (hh,) = rowwise_fwd(nm("sqrelu"), _f_sqrelu, [a], [], [], [(a.shape[1], CDT)], t, nb)
    o = mm(nm("mlp2"), hh, lw["w_mlp2"], F32)
    (h_out,) = rowwise_fwd(nm("resid"), _f_residual, [h_mid, o], [g2], [], [(D_MODEL, F32)], t, nb)
    sv.update(h=h, u=u, p=p, qT=qT, kk=kk, vv=vv, oT=oT, lse=lse, attn_o=attn_o, xbc=xbc, yf=yf, yb=yb,
              sf=sf, sb=sb, ssd_o=ssd_o, rq=rq, rk=rk, rf=rf, rb=rb, rsf=rsf, rsb=rsb, ret_o=ret_o, pbs=pbs,
              merged=merged, mix=mix, h_mid=h_mid, v=vv2, a=a, hh=hh, o=o)
    return h_out, sv


def layer_bwd(li, dh_out, sv, mod, lw, tabs, m_ctx):
    t = dh_out.shape[0]
    nb = m_ctx // min(ROW_TILE, t)
    sh1, sc1, g1, sh2, sc2, g2 = mod
    nm = lambda s: f"l{li}_{s}_bwd"
    gw = {}
    p = sv["p"]
    (do,), (dg2,), _ = rowwise_bwd(nm("resid"), _f_residual, [sv["h_mid"], sv["o"]], [g2], [], [dh_out],
                                   [False, True], [], [CDT], t, nb)
    dhh = mm(nm("mlp2_dx"), do, lw["w_mlp2"], CDT, transpose_b=True)
    gw["w_mlp2"] = mm_tn(nm("mlp2_dw"), sv["hh"], do, CDT, ("rows", lw["w_mlp2"].shape[0] // 4))
    (da,), _, _ = rowwise_bwd(nm("sqrelu"), _f_sqrelu, [sv["a"]], [], [], [dhh], [True], [], [CDT], t, nb)
    dv = mm(nm("mlp1_dx"), da, lw["w_mlp1"], F32, transpose_b=True)
    gw["w_mlp1"] = mm_tn(nm("mlp1_dw"), sv["v"], da, CDT, ("cols", lw["w_mlp1"].shape[1] // 4))
    (dh_a, dmix), (dg1, dsh2, dsc2), (gw["norm2_w"],) = rowwise_bwd(
        nm("mid"), _f_mid, [sv["h"], sv["mix"]], [g1, sh2, sc2], [lw["norm2_w"]], [dh_out, dv],
        [True, True], [True], [F32, CDT], t, nb)
    dmerged = mm(nm("out_dx"), dmix, lw["w_out"], CDT, transpose_b=True)
    gw["w_out"] = mm_tn(nm("out_dw"), sv["merged"], dmix, CDT, ("rows", lw["w_out"].shape[0] // 4))
    gl = [Cols(p, 1024 * b, 1024) for b in range(3)]
    dmg, _, _ = rowwise_bwd(nm("merge"), _f_merge, sv["pbs"] + gl, [], [], [dmerged], [True] * 6, [], [CDT] * 6,
                            t, nb)
    dpb, dgl = dmg[:3], dmg[3:]
    brs = (sv["attn_o"], sv["ssd_o"], sv["ret_o"])
    d_attn_o = mm(nm("branch0_dx"), dpb[0], lw["w_branch"][0], CDT, transpose_b=True)
    d_ssd_o = mm(nm("branch1_dx"), dpb[1], lw["w_branch"][1], F32, transpose_b=True)
    d_ret_o = mm(nm("branch2_dx"), dpb[2], lw["w_branch"][2], F32, transpose_b=True)
    n_loc = lw["w_branch"].shape[2] // 4
    gw["w_branch"] = jnp.stack([mm_tn(nm(f"branch{b}_dw"), brs[b], dpb[b], CDT, ("cols", n_loc)) for b in range(3)],
                               axis=1).reshape(4, -1, n_loc)
    tq = min(ATTN_TQ, m_ctx)
    dq_rows, dk_s, dv_s = attn_bwd(nm("attn"), sv["qT"], d_attn_o, sv["oT"], sv["lse"], sv["kk"],
                                   sv["kk"].transpose(0, 2, 1), sv["vv"], m_ctx)
    (dq_raw, dk_raw, dv_raw), _, (gw["qw"], gw["kw"]) = rowwise_bwd(
        nm("attn_prep"), _f_attn_prep,
        [_pcol(p, "q"), _pcol(p, "k"), _pcol(p, "v"), tabs["ca"], tabs["sa"]], [],
        [lw["qw"], lw["kw"], tabs["gq"], tabs["gk"]],
        [dq_rows, _merge_kv_t(dk_s) * LN2, _merge_kv_t(dv_s)],
        [True, True, True, False, False], [True, True, False, False], [CDT] * 3, t, nb)
    (dy_ssd, dxs_fin, dz), _, (gw["d_exp"], gw["ssd_nw"]) = rowwise_bwd(
        nm("ssd_fin"), _f_ssd_finish, [sv["yf"], sv["yb"], Cols(sv["xbc"], 0, 512), _pcol(p, "z")], [],
        [lw["d_exp"], lw["ssd_nw"]], [d_ssd_o], [True, False, True, True], [True, True], [F32, F32, CDT], t, nb)
    ssd_sh = [lw["dt_bias"], lw["a_log"]]
    post_ssd = lambda d: [jnp.concatenate(d[0:8], axis=1), d[8]]
    (dxbc_f, ddt_f), dsh_f = scan_bwd(nm("ssd_f"), _make_ssd_chunk, _ssd_rows(sv["xbc"], p), ssd_sh, (sv["sf"],),
                                      dy_ssd, post_ssd, [(1024, F32), (LANES, F32)], 4, t, m_ctx, dirs=(0,))
    (dxbc_b, ddt_b), dsh_b = scan_bwd(nm("ssd_b"), _make_ssd_chunk, _ssd_rows(sv["xbc"], p), ssd_sh, (sv["sb"],),
                                      dy_ssd, post_ssd, [(1024, F32), (LANES, F32)], 4, t, m_ctx, dirs=(1,))
    gw["dt_bias"], gw["a_log"] = dsh_f[0] + dsh_b[0], dsh_f[1] + dsh_b[1]
    ddt = (ddt_f + ddt_b).astype(CDT)
    dxbc_raw, gw["conv_w"], gw["conv_b"] = conv_bwd(_pcol(p, "xbc"), lw["conv_w"], lw["conv_b"], dxbc_f, dxbc_b,
                                                    dxs_fin, m_ctx)
    (dy_ret, drg), _, (gw["ret_gw"],) = rowwise_bwd(
        nm("ret_fin"), _f_ret_finish, [sv["rf"], sv["rb"], _pcol(p, "rg")], [], [lw["ret_gw"]], [d_ret_o],
        [True, False, True], [True], [F32, CDT], t, nb)
    post_ret = lambda d: [jnp.concatenate(d[0:4], axis=1), jnp.concatenate(d[4:8], axis=1),
                          jnp.concatenate(d[8:12], axis=1)]
    rrows = _ret_rows(sv["rq"], sv["rk"], p)
    (dq_f, dk_f, dv_f), (dq_b, dk_b, dv_b), (gw["ret_lg"],) = scan_bwd(
        nm("ret"), _make_ret_chunk, rrows, [lw["ret_lg"]], (sv["rsf"], sv["rsb"]), dy_ret, post_ret,
        [(512, F32)] * 3, 4, t, m_ctx)
    drv = (dv_f + dv_b).astype(CDT)
    (drq, drk), _, _ = rowwise_bwd(nm("ret_prep"), _f_ret_prep,
                                   [_pcol(p, "rq"), _pcol(p, "rk"), tabs["rc"], tabs["rs"]], [], [],
                                   [(dq_f, dq_b), (dk_f, dk_b)], [True, True, False, False], [], [CDT, CDT], t, nb)
    pieces = {"gates": None, "xbc": dxbc_raw, "q": dq_raw, "z": dz, "rq": drq, "rk": drk, "rv": drv, "rg": drg,
              "k": dk_raw, "v": dv_raw, "dt": ddt}
    cols = list(dgl) + [pieces[n] for n in IN_NEW_ORDER[1:]]
    used = sum(c.shape[1] for c in cols)
    cols.append(jnp.zeros((t, IN_PAD - used), CDT))
    dp = jnp.concatenate(cols, axis=1)
    du = mm(nm("in_dx"), dp, lw["w_in"], F32, transpose_b=True)
    gw["w_in"] = mm_tn(nm("in_dw"), sv["u"], dp, CDT)
    (dh_in,), (dsh1, dsc1), (gw["norm1_w"],) = rowwise_bwd(
        nm("norm1"), _f_norm_mod_thru, [sv["h"]], [sh1, sc1], [lw["norm1_w"]], [dh_a, du], [True], [True], [F32],
        t, nb)
    return dh_in, [dsh1, dsc1, dg1, dsh2, dsc2, dg2], gw


def _rope_tables(n_lat, m_ctx):
    rows = n_lat // GRID_W
    row = jnp.repeat(jnp.arange(rows, dtype=F32), GRID_W)
    col = jnp.tile(jnp.arange(GRID_W, dtype=F32), rows)
    nfreq = ATTN_HD // 4
    inv = ROPE_THETA ** (-jnp.arange(nfreq, dtype=F32) / nfreq)
    ang = jnp.concatenate([row[:, None] * inv, col[:, None] * inv], axis=-1)
    cos = jnp.concatenate([jnp.ones((m_ctx, ATTN_HD // 2), F32), jnp.cos(ang)], axis=0)
    sin = jnp.concatenate([jnp.zeros((m_ctx, ATTN_HD // 2), F32), jnp.sin(ang)], axis=0)
    c64 = jnp.concatenate([cos, cos], axis=1)
    s64 = jnp.concatenate([-sin, sin], axis=1)
    pos = jnp.arange(m_ctx + n_lat, dtype=F32)
    inv_r = ROPE_THETA ** (-jnp.linspace(0.0, 1.0, RET_DK // 2, dtype=F32))
    ang_r = pos[:, None] * inv_r
    rc = jnp.concatenate([jnp.cos(ang_r)] * 2, axis=1)
    rs = jnp.concatenate([-jnp.sin(ang_r), jnp.sin(ang_r)], axis=1)
    return dict(ca=jnp.tile(c64, (1, 2)), sa=jnp.tile(s64, (1, 2)), rc=rc, rs=rs, gq=_group_matrix(512, ATTN_HD),
                gk=_group_matrix(128, ATTN_HD))


def _layer_weights(full, small, layer):
    return dict(
        w_in=full["w_in"][layer], w_branch=full["w_branch"][layer], w_out=full["w_out"][layer],
        w_mlp1=full["w_mlp1"][layer], w_mlp2=full["w_mlp2"][layer],
        norm1_w=small["norm1_w"][layer][None], norm2_w=small["norm2_w"][layer][None],
        qw=jnp.tile(small["attn_q_norm"][layer], 8)[None], kw=jnp.tile(small["attn_k_norm"][layer], 2)[None],
        conv_w=small["ssd_conv_w"][layer], conv_b=small["ssd_conv_b"][layer][None],
        dt_bias=_lane_pad(small["ssd_dt_bias"][layer]), a_log=_lane_pad(small["ssd_a_log"][layer]),
        d_exp=jnp.repeat(small["ssd_d"][layer], SSD_HD)[None], ssd_nw=small["ssd_norm_w"][layer][None],
        ret_lg=_lane_pad(small["ret_log_decay"][layer]), ret_gw=small["ret_gn_w"][layer][None])


def local_step(x, c, ctx, full, small, loss_target):
    n_lat, d = x.shape
    m_ctx = ctx.shape[0]
    t = n_lat + m_ctx
    depth = small["norm1_w"].shape[0]
    tabs = _rope_tables(n_lat, m_ctx)
    h = jnp.concatenate([ctx, x], axis=0)
    cc = jnp.concatenate([small["c_ctx"][None], c, jnp.zeros((COND_ROWS - 2, d), F32)], axis=0)
    (scc,) = rowwise_fwd("cond_silu", _f_silu, [cc], [], [], [(d, CDT)], COND_ROWS, 0)
    mods, saved, lws = [], [], []
    for layer in range(depth):
        lw = _layer_weights(full, small, layer)
        mod_raw = mm(f"l{layer}_mod", scc, full["w_mod"][layer], F32)
        (mod8,) = rowwise_fwd(f"l{layer}_mod_bias", _f_bias, [mod_raw], [], [small["b_mod"][layer][None]],
                              [(6 * d, F32)], COND_ROWS, 0)
        mod = [mod8[0:2, k * d:(k + 1) * d].reshape(2, 1, d) for k in range(6)]
        h, sv = layer_fwd(layer, h, mod, lw, tabs, m_ctx)
        mods.append(mod)
        saved.append(sv)
        lws.append(lw)
    loss, dh, d_final = loss_head(h, loss_target, small["final_norm_w"][None], m_ctx)

    gbig = {k: [None] * depth for k in BIG}
    gs = {k: [None] * depth for k in SMALL if k not in ("c_ctx", "final_norm_w")}
    d_scc = None
    for layer in reversed(range(depth)):
        lw = lws[layer]
        dh, dmod, gw = layer_bwd(layer, dh, saved[layer], mods[layer], lw, tabs, m_ctx)
        dmod8 = jnp.concatenate([jnp.concatenate([g_.reshape(2, d) for g_ in dmod], axis=1),
                                 jnp.zeros((COND_ROWS - 2, 6 * d), F32)], axis=0)
        (dmod_c,), _, (db_mod,) = rowwise_bwd(f"l{layer}_mod_bias_bwd", _f_bias, [dmod8], [],
                                              [small["b_mod"][layer][None]], [dmod8], [True], [True], [CDT], COND_ROWS, 0)
        gbig["w_mod"][layer] = mm_tn(f"l{layer}_mod_dw", scc, dmod_c, CDT, ("cols", 6 * d // 4))
        part = mm(f"l{layer}_mod_dx", dmod_c, full["w_mod"][layer], F32, transpose_b=True)
        d_scc = part if d_scc is None else d_scc + part
        g_in = _in_from_padded(gw["w_in"])
        gbig["w_in"][layer] = g_in.reshape(d, 4, g_in.shape[1] // 4).transpose(1, 0, 2)
        for k in ("w_branch", "w_out", "w_mlp1", "w_mlp2"):
            gbig[k][layer] = gw[k]
        gs["b_mod"][layer] = db_mod.reshape(-1)
        gs["norm1_w"][layer] = gw["norm1_w"].reshape(-1)
        gs["norm2_w"][layer] = gw["norm2_w"].reshape(-1)
        gs["attn_q_norm"][layer] = gw["qw"].reshape(8, ATTN_HD).sum(0)
        gs["attn_k_norm"][layer] = gw["kw"].reshape(2, ATTN_HD).sum(0)
        gs["ssd_conv_w"][layer] = gw["conv_w"]
        gs["ssd_conv_b"][layer] = gw["conv_b"].reshape(-1)
        gs["ssd_dt_bias"][layer] = gw["dt_bias"][0, :16].reshape(2, 8)
        gs["ssd_a_log"][layer] = gw["a_log"][0, :16].reshape(2, 8)
        gs["ssd_d"][layer] = gw["d_exp"].reshape(SSD_HEADS, SSD_HD).sum(1)
        gs["ssd_norm_w"][layer] = gw["ssd_nw"].reshape(-1)
        gs["ret_log_decay"][layer] = gw["ret_lg"][0, :8].reshape(2, 4)
        gs["ret_gn_w"][layer] = gw["ret_gw"].reshape(-1)
    (d_cc,), _, _ = rowwise_bwd("cond_silu_bwd", _f_silu, [cc], [], [], [d_scc], [True], [], [F32], COND_ROWS, 0)
    g_small = {k: jnp.stack(v) for k, v in gs.items()}
    g_small["c_ctx"] = d_cc[0]
    g_small["final_norm_w"] = d_final.reshape(-1)
    g_big = {k: jnp.stack(v) for k, v in gbig.items()}
    return loss, dh[m_ctx:], g_big, g_small


def kernel(x, c, ctx, c_ctx, w_mod, b_mod, norm1_w, norm2_w, w_in, attn_q_norm, attn_k_norm, ssd_conv_w, ssd_conv_b, ssd_dt_bias, ssd_a_log, ssd_d, ssd_norm_w, ret_log_decay, ret_gn_w, w_branch, w_out, w_mlp1, w_mlp2, final_norm_w, loss_target, m_c_ctx, m_w_mod, m_b_mod, m_norm1_w, m_norm2_w, m_w_in, m_attn_q_norm, m_attn_k_norm, m_ssd_conv_w, m_ssd_conv_b, m_ssd_dt_bias, m_ssd_a_log, m_ssd_d, m_ssd_norm_w, m_ret_log_decay, m_ret_gn_w, m_w_branch, m_w_out, m_w_mlp1, m_w_mlp2, m_final_norm_w, v_c_ctx, v_w_mod, v_b_mod, v_norm1_w, v_norm2_w, v_w_in, v_attn_q_norm, v_attn_k_norm, v_ssd_conv_w, v_ssd_conv_b, v_ssd_dt_bias, v_ssd_a_log, v_ssd_d, v_ssd_norm_w, v_ret_log_decay, v_ret_gn_w, v_w_branch, v_w_out, v_w_mlp1, v_w_mlp2, v_final_norm_w):
    env = dict(locals())
    w_loc = {k: env[k] for k in WEIGHTS}
    m_loc = {k: env["m_" + k] for k in WEIGHTS}
    v_loc = {k: env["v_" + k] for k in WEIGHTS}
    chip = 2 * lax.axis_index("x") + lax.axis_index("y")
    core = lax.axis_index("c")

    depth = w_loc["w_mod"].shape[0]
    assert depth == 2, "the exchanges split the layers between a chip's two cores"
    shards = [w_loc[k].astype(CDT).reshape(depth, -1, w_loc[k].shape[-1]) for k in BIG]
    mine = gather_layers("gather_weights", shards, [BIG_KIND[k] for k in BIG])
    full = {}
    for k, arr in zip(BIG, mine):
        both = exchange_both("share_" + k, arr.reshape(-1, arr.shape[-1]))
        if k == "w_in":
            both = both.reshape(depth, 4, -1, both.shape[-1]).transpose(0, 2, 1, 3)
            both = _in_to_padded(both.reshape(depth, both.shape[1], -1))
        full[k] = both.reshape((depth,) + w_loc[k].shape[1:-1] + (-1,)) if BIG_KIND[k] == "cols" else \
            both.reshape((depth,) + w_loc[k].shape[1:-2] + (-1, w_loc[k].shape[-1])) if BIG_KIND[k] == "rows" else both

    cw = w_loc["ssd_conv_w"]
    cw_w = cw.shape[-1]
    placed = lax.dynamic_update_slice(jnp.zeros(cw.shape[:-1] + (4 * cw_w,), F32),
                                      cw * (core == 0).astype(F32), (0, 0, chip * cw_w))
    conv_full = _unpack_flat(allreduce_small("gather_conv_w", _pack_flat([placed], F32, LANES)), [placed.shape])[0]
    small = {k: w_loc[k] for k in SMALL}
    small["ssd_conv_w"] = conv_full

    loss_l, grad_x, g_big, g_small = local_step(x[0], c, ctx[0], full, small, loss_target[0])

    small_shapes = [g_small[k].shape for k in SMALL] + [(LANES,)]
    summed = _unpack_flat(allreduce_small("reduce_small", _pack_flat([g_small[k] for k in SMALL] + [loss_l], F32, LANES)),
                          small_shapes)
    gsum = dict(zip(SMALL, summed[:-1]))
    loss = summed[-1][0]
    gsum["ssd_conv_w"] = lax.dynamic_slice(gsum["ssd_conv_w"], (0, 0, chip * cw_w), cw.shape)

    pair = []
    for k in BIG:
        _, _, rows, cols = g_big[k].shape
        pair.append(exchange_add("pair_" + k, g_big[k].reshape(depth, 4 * rows, cols)).reshape(4, rows, cols))
    landed = scatter_pieces("scatter_grads", pair)
    g_sum = [sum_exchange("sum_" + k, parts) for k, parts in zip(BIG, landed)]

    grads, deltas, new_m, new_v = {}, {}, {}, {}
    for i, k in enumerate(BIG):
        shp = w_loc[k].shape
        three_d = lambda a, shp=shp: a.reshape((-1,) + shp[-2:])
        res = adamw("adamw_" + k, three_d(w_loc[k]), three_d(m_loc[k]), three_d(v_loc[k]), [three_d(g_sum[i])])
        grads[k], deltas[k], new_m[k], new_v[k] = [r.reshape(shp) for r in res]
    small_loc_shapes = [w_loc[k].shape for k in SMALL]
    res = adamw("adamw_small", _pack_flat([w_loc[k] for k in SMALL], F32, LANES)[None],
                _pack_flat([m_loc[k] for k in SMALL], F32, LANES)[None],
                _pack_flat([v_loc[k] for k in SMALL], F32, LANES)[None],
                [_pack_flat([gsum[k] for k in SMALL], F32, LANES)[None]])
    for dst, r in zip((grads, deltas, new_m, new_v), res):
        dst.update(dict(zip(SMALL, _unpack_flat(r, small_loc_shapes))))

    return (loss, grad_x[None], *[grads[k] for k in WEIGHTS], *[deltas[k] for k in WEIGHTS],
            *[new_m[k] for k in WEIGHTS], *[new_v[k] for k in WEIGHTS])
```

```python
import functools
import math
from typing import NamedTuple

import jax
import jax.numpy as jnp
from jax import lax
from jax.experimental import pallas as pl
from jax.experimental.pallas import tpu as pltpu

F32 = jnp.float32
CDT = jnp.bfloat16
NORM_EPS = 1e-6
ROPE_THETA = 10000.0
GRID_W = 64
D_MODEL = 1024
ATTN_HEADS, ATTN_KV, ATTN_HD = 8, 2, 64
SSD_HEADS, SSD_HD, SSD_STATE = 8, 64, 128
RET_HEADS, RET_DK = 4, 128
CHUNK = 128
ROW_TILE = 256
MM_ROWS = 768
MM_TN_ROWS = 2816
MM_VMEM_BUDGET = 44 * 1024 * 1024
ATTN_TQ, ATTN_TK = 256, 256
ATTN_ONES_ROWS = 16
ATTN_TK_BWD = 2048
LOG2E, LN2 = 1.4426950408889634, 0.6931471805599453
ATTN_TK_FWD = 2048
LANES = 128
PAIR_BLOCK_BYTES = 2 * 1024 * 1024
COND_ROWS = 16
VMEM_LIMIT = 56 * 1024 * 1024

ADAM_LR, ADAM_B1, ADAM_B2, ADAM_EPS, ADAM_WD, ADAM_STEP = 0.001, 0.9, 0.999, 1e-08, 0.01, 10

IN_LAYOUT = {
    "gates": (0, 3072, 4368, 3072), "xbc": (3072, 1024, 1280, 1024), "q": (4096, 512, 0, 512),
    "z": (4608, 512, 768, 512), "rq": (5120, 512, 2320, 512), "rk": (5632, 512, 2832, 512),
    "rv": (6144, 512, 3344, 512), "rg": (6656, 512, 3856, 512), "k": (7168, 128, 512, 128),
    "v": (7296, 128, 640, 128), "dt": (7424, 128, 2304, 16),
}
IN_PAD = 7680
IN_ORIG_ORDER = ("q", "k", "v", "z", "xbc", "dt", "rq", "rk", "rv", "rg", "gates")
IN_NEW_ORDER = ("gates", "xbc", "q", "z", "rq", "rk", "rv", "rg", "k", "v", "dt")

BIG = ("w_mod", "w_in", "w_branch", "w_out", "w_mlp1", "w_mlp2")
BIG_KIND = {"w_mod": "cols", "w_in": "slices", "w_branch": "cols", "w_out": "rows", "w_mlp1": "cols", "w_mlp2": "rows"}
SMALL = ("c_ctx", "b_mod", "norm1_w", "norm2_w", "attn_q_norm", "attn_k_norm", "ssd_conv_w", "ssd_conv_b",
         "ssd_dt_bias", "ssd_a_log", "ssd_d", "ssd_norm_w", "ret_log_decay", "ret_gn_w", "final_norm_w")
WEIGHTS = ("c_ctx", "w_mod", "b_mod", "norm1_w", "norm2_w", "w_in", "attn_q_norm", "attn_k_norm", "ssd_conv_w",
           "ssd_conv_b", "ssd_dt_bias", "ssd_a_log", "ssd_d", "ssd_norm_w", "ret_log_decay", "ret_gn_w",
           "w_branch", "w_out", "w_mlp1", "w_mlp2", "final_norm_w")


def _cp(sem):
    return pltpu.CompilerParams(dimension_semantics=sem, vmem_limit_bytes=VMEM_LIMIT)


class Cols(NamedTuple):
    arr: jax.Array
    off: int
    width: int


def _width(item):
    return item.width if isinstance(item, Cols) else item.shape[1]


def _row_in(item, rows, imap=None):
    imap = imap or (lambda i: i)
    if isinstance(item, Cols):
        assert item.off % item.width == 0
        blk = item.off // item.width
        return item.arr, pl.BlockSpec((rows, item.width), lambda i, blk=blk: (imap(i), blk))
    return item, pl.BlockSpec((rows, item.shape[1]), lambda i: (imap(i), 0))


def _const_spec(shape):
    return pl.BlockSpec(shape, lambda *_: (0,) * len(shape))


def _mxu(a, b, dims=(((1,), (0,)), ((), ()))):
    return lax.dot_general(a.astype(CDT), b.astype(CDT), dims, preferred_element_type=F32)


_NT = (((1,), (1,)), ((), ()))
_TN = (((0,), (0,)), ((), ()))


@jax.custom_vjp
def _softplus(x):
    return jnp.maximum(x, 0.0) + jnp.log1p(jnp.exp(-jnp.abs(x)))


def _softplus_fwd(x):
    return _softplus(x), x


def _softplus_bwd(x, g):
    return (g * jax.nn.sigmoid(x),)


_softplus.defvjp(_softplus_fwd, _softplus_bwd)


def _group_mean_impl(x, gmat):
    hi = x.astype(CDT)
    lo = (x - hi.astype(F32)).astype(CDT)
    return (jnp.dot(hi, gmat, preferred_element_type=F32) + jnp.dot(lo, gmat, preferred_element_type=F32))


@jax.custom_vjp
def _group_mean(x, gmat):
    return _group_mean_impl(x, gmat)


def _group_mean_fwd(x, gmat):
    return _group_mean_impl(x, gmat), gmat


def _group_mean_bwd(gmat, g):
    return _group_mean_impl(g, gmat), jnp.zeros_like(gmat)


_group_mean.defvjp(_group_mean_fwd, _group_mean_bwd)


def _group_matrix(width, group):
    r = jnp.arange(width) // group
    return jnp.where(r[:, None] == r[None, :], 1.0 / group, 0.0).astype(CDT)


def _make_rope(half):
    def partner(x):
        w = x.shape[1]
        lane = lax.broadcasted_iota(jnp.int32, x.shape, 1)
        first = (lane % (2 * half)) < half
        return jnp.where(first, pltpu.roll(x, w - half, axis=1), pltpu.roll(x, half, axis=1))

    def impl(x, cos_full, sin_signed):
        return x * cos_full + partner(x) * sin_signed

    @jax.custom_vjp
    def rope(x, cos_full, sin_signed):
        return impl(x, cos_full, sin_signed)

    def fwd(x, cos_full, sin_signed):
        return impl(x, cos_full, sin_signed), (cos_full, sin_signed)

    def bwd(res, g):
        cos_full, sin_signed = res
        return impl(g, cos_full, -sin_signed), jnp.zeros_like(cos_full), jnp.zeros_like(sin_signed)

    rope.defvjp(fwd, bwd)
    return rope


_rope32 = _make_rope(32)
_rope64 = _make_rope(64)


def _rms(x, w):
    return x * lax.rsqrt(jnp.mean(x * x, axis=-1, keepdims=True) + NORM_EPS) * w


def _col(v, lane_index):
    lane = lax.broadcasted_iota(jnp.int32, v.shape, 1)
    return jnp.sum(jnp.where(lane == lane_index, v, 0.0), axis=1, keepdims=True)


def _typed_spec(width, nb_ctx):
    return pl.BlockSpec((None, 1, width), lambda i: (jnp.where(i >= nb_ctx, 1, 0), 0, 0))


def rowwise_fwd(name, f, rows, typed, shared, outs, n_rows, nb_ctx, tm=ROW_TILE):
    tm = min(tm, n_rows)
    nin = len(rows) + len(typed) + len(shared)

    def body(*refs):
        res = f(*[r[...] for r in refs[:nin]])
        for o_ref, o in zip(refs[nin:], res):
            o_ref[...] = o.astype(o_ref.dtype)

    arrs, specs = [], []
    for it in rows:
        a, s = _row_in(it, tm)
        arrs.append(a)
        specs.append(s)
    for t in typed:
        arrs.append(t)
        specs.append(_typed_spec(t.shape[-1], nb_ctx))
    for s_ in shared:
        arrs.append(s_)
        specs.append(_const_spec(s_.shape))
    res = pl.pallas_call(
        body, name=name, grid=(n_rows // tm,), in_specs=specs,
        out_specs=[pl.BlockSpec((tm, w), lambda i: (i, 0)) for w, _ in outs],
        out_shape=[jax.ShapeDtypeStruct((n_rows, w), dt) for w, dt in outs],
        compiler_params=_cp(("parallel",)))(*arrs)
    return res


def rowwise_bwd(name, f, rows, typed, shared, cots, row_diff, shared_diff, drow_dtypes, n_rows, nb_ctx, tm=ROW_TILE):
    tm = min(tm, n_rows)
    cot_groups = [c_ if isinstance(c_, tuple) else (c_,) for c_ in cots]
    cots = [a for grp in cot_groups for a in grp]
    nr, nt, ns, nc = len(rows), len(typed), len(shared), len(cots)
    nin = nr + nt + ns
    d_rows = [k for k in range(nr) if row_diff[k]]
    d_sh = [k for k in range(ns) if shared_diff[k]]

    def body(*refs):
        rvals = [r[...] for r in refs[:nr]]
        tvals = [r[...] for r in refs[nr:nr + nt]]
        svals = [r[...] for r in refs[nr + nt:nin]]
        cparts = [r[...].astype(F32) for r in refs[nin:nin + nc]]
        cvals = []
        for grp in cot_groups:
            cvals.append(sum(cparts[1:len(grp)], cparts[0]))
            cparts = cparts[len(grp):]
        out_refs = refs[nin + nc:]

        def g(*dv):
            dv = list(dv)
            rv = list(rvals)
            for k in d_rows:
                rv[k] = dv.pop(0)
            tv = [dv.pop(0) for _ in range(nt)]
            sv = list(svals)
            for k in d_sh:
                sv[k] = dv.pop(0)
            return tuple(o.astype(F32) for o in f(*rv, *tv, *sv))

        prim = [rvals[k].astype(F32) for k in d_rows] + tvals + [svals[k] for k in d_sh]
        _, vjp = jax.vjp(g, *prim)
        grads = list(vjp(tuple(cvals)))
        i = pl.program_id(0)
        for ref in out_refs[:len(d_rows)]:
            ref[...] = grads.pop(0).astype(ref.dtype)
        first_typed = (i == 0) | (i == nb_ctx)
        for ref in out_refs[len(d_rows):len(d_rows) + nt]:
            gr = grads.pop(0)

            @pl.when(first_typed)
            def _(ref=ref, gr=gr):
                ref[...] = gr

            @pl.when(jnp.logical_not(first_typed))
            def _(ref=ref, gr=gr):
                ref[...] += gr
        for ref in out_refs[len(d_rows) + nt:]:
            gr = grads.pop(0)

            @pl.when(i == 0)
            def _(ref=ref, gr=gr):
                ref[...] = gr

            @pl.when(i != 0)
            def _(ref=ref, gr=gr):
                ref[...] += gr

    arrs, specs = [], []
    for it in list(rows):
        a, s = _row_in(it, tm)
        arrs.append(a)
        specs.append(s)
    for t in typed:
        arrs.append(t)
        specs.append(_typed_spec(t.shape[-1], nb_ctx))
    for s_ in shared:
        arrs.append(s_)
        specs.append(_const_spec(s_.shape))
    for c_ in cots:
        a, s = _row_in(c_, tm)
        arrs.append(a)
        specs.append(s)
    out_specs, out_shape = [], []
    for k, dt in zip(d_rows, drow_dtypes):
        w = _width(rows[k])
        out_specs.append(pl.BlockSpec((tm, w), lambda i: (i, 0)))
        out_shape.append(jax.ShapeDtypeStruct((n_rows, w), dt))
    for t in typed:
        out_specs.append(_typed_spec(t.shape[-1], nb_ctx))
        out_shape.append(jax.ShapeDtypeStruct(t.shape, F32))
    for k in d_sh:
        out_specs.append(_const_spec(shared[k].shape))
        out_shape.append(jax.ShapeDtypeStruct(shared[k].shape, F32))
    res = pl.pallas_call(body, name=name, grid=(n_rows // tm,), in_specs=specs, out_specs=out_specs,
                         out_shape=out_shape, compiler_params=_cp(("arbitrary",)))(*arrs)
    n1, n2 = len(d_rows), len(d_rows) + nt
    return list(res[:n1]), list(res[n1:n2]), list(res[n2:])


def _pick(n, prefs):
    for p in prefs:
        if n % p == 0:
            return p
    return n


def mm(name, a, b, out_dtype, transpose_b=False):
    n, k = b.shape if transpose_b else b.shape[::-1]
    m = (a.arr if isinstance(a, Cols) else a).shape[0]
    assert _width(a) == k
    tm = _pick(m, (MM_ROWS, 256))
    osz = jnp.dtype(out_dtype).itemsize
    tn = next(c for c in (2560, 2048, 1536, 1024, 512, 256, 128, n)
              if n % c == 0 and 2 * (tm * k * 2 + k * c * 2 + tm * c * osz) <= MM_VMEM_BUDGET or c == n)
    dims = _NT if transpose_b else (((1,), (0,)), ((), ()))

    def body(a_ref, b_ref, o_ref):
        o_ref[...] = lax.dot_general(a_ref[...], b_ref[...], dims, preferred_element_type=F32).astype(o_ref.dtype)

    a_arr, a_spec = _row_in(a, tm)
    a_spec = pl.BlockSpec(a_spec.block_shape, lambda j, i, f=a_spec.index_map: f(i))
    b_spec = pl.BlockSpec((tn, k), lambda j, i: (j, 0)) if transpose_b else pl.BlockSpec((k, tn), lambda j, i: (0, j))
    return pl.pallas_call(
        body, name=name, grid=(n // tn, m // tm), in_specs=[a_spec, b_spec],
        out_specs=pl.BlockSpec((tm, tn), lambda j, i: (i, j)),
        out_shape=jax.ShapeDtypeStruct((m, n), out_dtype),
        compiler_params=_cp(("parallel", "parallel")))(a_arr, b)


def mm_tn(name, a, b, out_dtype=F32, pieces=None):
    t = (a.arr if isinstance(a, Cols) else a).shape[0]
    k, n = _width(a), _width(b)
    tt = _pick(t, (MM_TN_ROWS, MM_ROWS, 256))
    k_unit = pieces[1] if pieces and pieces[0] == "rows" else k
    n_unit = pieces[1] if pieces and pieces[0] == "cols" else n
    tk = _pick(k_unit, (1024, 512, 256, 128))
    tn = _pick(n_unit, (1280, 1024, 512, 256, 128))
    n_t = t // tt

    def body(a_ref, b_ref, o_ref, acc):
        part = lax.dot_general(a_ref[...], b_ref[...], _TN, preferred_element_type=F32)
        ti = pl.program_id(2)

        @pl.when(ti == 0)
        def _():
            acc[...] = part

        @pl.when(ti != 0)
        def _():
            acc[...] += part

        @pl.when(ti == n_t - 1)
        def _():
            o_ref[...] = acc[...].astype(o_ref.dtype)

    def win(item, width):
        if isinstance(item, Cols):
            assert item.off % width == 0
            return item.arr, item.off // width
        return item, 0

    a_arr, a0 = win(a, tk)
    b_arr, b0 = win(b, tn)
    if pieces is None:
        out_spec = pl.BlockSpec((tk, tn), lambda ki, ni, ti: (ki, ni))
        out_shape = (k, n)
    elif pieces[0] == "cols":
        per = n_unit // tn
        out_spec = pl.BlockSpec((None, tk, tn), lambda ki, ni, ti: (ni // per, ki, ni % per))
        out_shape = (4, k, n_unit)
    else:
        per = k_unit // tk
        out_spec = pl.BlockSpec((None, tk, tn), lambda ki, ni, ti: (ki // per, ki % per, ni))
        out_shape = (4, k_unit, n)
    return pl.pallas_call(
        body, name=name, grid=(k // tk, n // tn, n_t),
        in_specs=[pl.BlockSpec((tt, tk), lambda ki, ni, ti: (ti, a0 + ki)),
                  pl.BlockSpec((tt, tn), lambda ki, ni, ti: (ti, b0 + ni))],
        out_specs=out_spec, out_shape=jax.ShapeDtypeStruct(out_shape, out_dtype),
        scratch_shapes=[pltpu.VMEM((tk, tn), F32)],
        compiler_params=_cp(("parallel", "parallel", "arbitrary")))(a_arr, b_arr)


def _heads_t(rows_blk):
    blk = rows_blk.astype(F32).T
    return jnp.concatenate([blk[hh * ATTN_HD:(hh + 1) * ATTN_HD, :] for hh in range(4)], axis=1)


def _heads_rows(t_blk):
    tq = t_blk.shape[1] // 4
    return jnp.concatenate([t_blk[:, hh * tq:(hh + 1) * tq] for hh in range(4)], axis=0).T


def attn_fwd(name, q, kk, vT_ones, m_ctx, tq):
    t, hd, hd_ext = kk.shape[1], ATTN_HD, vT_ones.shape[1]
    nq, r = t // tq, 4 * tq
    tk = _pick(t - m_ctx, (ATTN_TK_FWD, ATTN_TK))
    nqc, n_lat_tiles = m_ctx // tq, (t - m_ctx) // tk

    def body(q_ref, k_ref, vT_ref, o_ref, qT_ref, oT_ref, lse_ref):
        i = pl.program_id(1)
        q_t = _heads_t(q_ref[...]).astype(CDT)
        qT_ref[...] = q_t

        def tile(off, size, carry):
            mi, acc = carry
            sub = min(size, ATTN_TK)
            offs = [off + u * sub for u in range(size // sub)]
            sts = [jnp.dot(k_ref[pl.ds(o, sub), :], q_t, preferred_element_type=F32) for o in offs]
            for o, st in zip(offs, sts):
                mn = jnp.maximum(mi, jnp.max(st, axis=0, keepdims=True))
                pt = jnp.exp2(st - mn)
                acc = jnp.exp2(mi - mn) * acc + jnp.dot(vT_ref[:, pl.ds(o, sub)], pt.astype(CDT),
                                                        preferred_element_type=F32)
                mi = mn
            return mi, acc

        carry = tile(0, m_ctx, (jnp.full((1, r), -1e30, F32), jnp.zeros((hd_ext, r), F32)))
        mi, acc = lax.fori_loop(
            0, jnp.where(i < nqc, 0, n_lat_tiles),
            lambda j, cr: tile(pl.multiple_of(m_ctx + j * tk, ATTN_TK), tk, cr), carry)
        li = acc[hd:hd + 1]
        o_t = acc[:hd] / li
        oT_ref[...] = o_t.astype(oT_ref.dtype)
        o_ref[...] = _heads_rows(o_t).astype(o_ref.dtype)
        lse_ref[...] = mi + jnp.log2(li)

    blk_t = pl.BlockSpec((None, None, hd, r), lambda g, i: (g, i, 0, 0))
    rows = pl.BlockSpec((tq, 4 * hd), lambda g, i: (i, g))
    return pl.pallas_call(
        body, name=name, grid=(2, nq),
        in_specs=[rows, pl.BlockSpec((None, t, hd), lambda g, i: (g, 0, 0)),
                  pl.BlockSpec((None, hd_ext, t), lambda g, i: (g, 0, 0))],
        out_specs=[rows, blk_t, blk_t, pl.BlockSpec((None, None, 1, r), lambda g, i: (g, i, 0, 0))],
        out_shape=[jax.ShapeDtypeStruct((t, 8 * hd), CDT), jax.ShapeDtypeStruct((2, nq, hd, r), CDT),
                   jax.ShapeDtypeStruct((2, nq, hd, r), CDT), jax.ShapeDtypeStruct((2, nq, 1, r), F32)],
        compiler_params=_cp(("parallel", "arbitrary")))(q, kk, vT_ones)


def attn_bwd(name, qT, do, oT, lse, kk, kT, vv, m_ctx):
    _, nq, hd, r = qT.shape
    t = kk.shape[1]
    tq = r // 4
    tk = _pick(t - m_ctx, (ATTN_TK_BWD, ATTN_TK))
    nqc, n_lat_tiles = m_ctx // tq, (t - m_ctx) // tk

    def body(qT_ref, do_ref, oT_ref, lse_ref, k_ref, kT_ref, v_ref, dq_ref, dk_ref, dv_ref):
        i = pl.program_id(1)

        @pl.when(i == 0)
        def _():
            dk_ref[...] = jnp.zeros_like(dk_ref)
            dv_ref[...] = jnp.zeros_like(dv_ref)

        q_t = qT_ref[...]
        do_f = _heads_t(do_ref[...])
        do_t = do_f.astype(CDT)
        lse = lse_ref[...]
        delta = jnp.sum(do_f * oT_ref[...].astype(F32), axis=0, keepdims=True)

        def tile(off, size, dq):
            sub = min(size, ATTN_TK)
            offs = [off + u * sub for u in range(size // sub)]
            sts = [jnp.dot(k_ref[pl.ds(o, sub), :], q_t, preferred_element_type=F32) for o in offs]
            dpts = [jnp.dot(v_ref[pl.ds(o, sub), :], do_t, preferred_element_type=F32) for o in offs]
            for o, st, dpt in zip(offs, sts, dpts):
                pt = jnp.exp2(st - lse)
                dv_ref[pl.ds(o, sub), :] += lax.dot_general(pt.astype(CDT), do_t, _NT, preferred_element_type=F32)
                dst = (pt * (dpt - delta)).astype(CDT)
                dk_ref[pl.ds(o, sub), :] += lax.dot_general(dst, q_t, _NT, preferred_element_type=F32)
                dq = dq + jnp.dot(kT_ref[:, pl.ds(o, sub)], dst, preferred_element_type=F32)
            return dq

        dq = tile(0, m_ctx, jnp.zeros((hd, r), F32))
        dq = lax.fori_loop(0, jnp.where(i < nqc, 0, n_lat_tiles),
                           lambda j, acc: tile(pl.multiple_of(m_ctx + j * tk, ATTN_TK), tk, acc), dq)
        dq_ref[...] = _heads_rows(dq * LN2)

    blk_t = pl.BlockSpec((None, None, hd, r), lambda g, i: (g, i, 0, 0))
    row = pl.BlockSpec((None, None, 1, r), lambda g, i: (g, i, 0, 0))
    kv = pl.BlockSpec((None, t, hd), lambda g, i: (g, 0, 0))
    rows = pl.BlockSpec((tq, 4 * hd), lambda g, i: (i, g))
    return pl.pallas_call(
        body, name=name, grid=(2, nq),
        in_specs=[blk_t, rows, blk_t, row, kv, pl.BlockSpec((None, hd, t), lambda g, i: (g, 0, 0)), kv],
        out_specs=[rows, kv, kv],
        out_shape=[jax.ShapeDtypeStruct((t, 8 * hd), F32), jax.ShapeDtypeStruct(kk.shape, F32),
                   jax.ShapeDtypeStruct(kk.shape, F32)],
        compiler_params=_cp(("parallel", "arbitrary")))(qT, do, oT, lse, kk, kT, vv)


def _split_kv(a):
    return a.reshape(a.shape[0], 2, ATTN_HD).transpose(1, 0, 2)


def _merge_kv(a):
    return a.transpose(1, 0, 2).reshape(a.shape[1], 2 * ATTN_HD)


def _chunk_order(rev, ncc, nct):
    if not rev:
        return lambda s: s
    return lambda s: jnp.where(s < ncc, ncc - 1 - s, nct - 1 - (s - ncc))


def scan_fwd(name, make_fn, rows, shared, n_state, y_width, n_rows, m_ctx):
    nct, ncc = n_rows // CHUNK, m_ctx // CHUNK
    orders = [_chunk_order(rev, ncc, nct) for rev in (False, True)]
    fns = [make_fn(0), make_fn(1)]
    nr, ns = len(rows), len(shared)

    def body(*refs):
        svals = [r[...] for r in refs[2 * nr:2 * nr + ns]]
        y_refs, sin_refs, st = refs[2 * nr + ns:2 * nr + ns + 2], refs[2 * nr + ns + 2:2 * nr + ns + 4], refs[-1]

        @pl.when(pl.program_id(0) == 0)
        def _():
            st[...] = jnp.zeros_like(st)

        for d in range(2):
            rvals = [r[...] for r in refs[d * nr:(d + 1) * nr]]
            prev = [st[d, k] for k in range(n_state)]
            sin_refs[d][...] = st[d]
            y, new = fns[d](rvals, svals, prev)
            y_refs[d][...] = y
            for k in range(n_state):
                st[d, k] = new[k]

    arrs, specs = [], []
    for order in orders:
        for it in rows:
            a, s = _row_in(it, CHUNK, order)
            arrs.append(a)
            specs.append(s)
    for s_ in shared:
        arrs.append(s_)
        specs.append(_const_spec(s_.shape))
    return pl.pallas_call(
        body, name=name, grid=(nct,), in_specs=specs,
        out_specs=[pl.BlockSpec((CHUNK, y_width), lambda s, o=o: (o(s), 0)) for o in orders]
        + [pl.BlockSpec((None, n_state, LANES, LANES), lambda s, o=o: (o(s), 0, 0, 0)) for o in orders],
        out_shape=[jax.ShapeDtypeStruct((n_rows, y_width), F32)] * 2
        + [jax.ShapeDtypeStruct((nct, n_state, LANES, LANES), F32)] * 2,
        scratch_shapes=[pltpu.VMEM((2, n_state, LANES, LANES), F32)],
        compiler_params=_cp(("arbitrary",)))(*arrs)


def scan_bwd(name, make_fn, rows, shared, states_in, dy, post, outs, n_state, n_rows, m_ctx, dirs=(0, 1)):
    nct, ncc = n_rows // CHUNK, m_ctx // CHUNK
    orders = [(lambda r, f=_chunk_order(d == 1, ncc, nct): f(nct - 1 - r)) for d in dirs]
    fns = [make_fn(d) for d in dirs]
    nd = len(dirs)
    nr, ns, no = len(rows), len(shared), len(outs)
    n_in = nd * nr + ns

    def body(*refs):
        svals = [r[...] for r in refs[nd * nr:n_in]]
        sin_refs, dy_refs = refs[n_in:n_in + nd], refs[n_in + nd:n_in + 2 * nd]
        out_refs = refs[n_in + 2 * nd:n_in + 2 * nd + nd * no]
        dsh_refs = refs[n_in + 2 * nd + nd * no:-1]
        dst = refs[-1]
        r = pl.program_id(0)

        @pl.when(r == 0)
        def _():
            dst[...] = jnp.zeros_like(dst)

        d_shared = None
        for d in range(nd):
            rvals = [x[...] for x in refs[d * nr:(d + 1) * nr]]
            prev = [sin_refs[d][k] for k in range(n_state)]
            _, vjp = jax.vjp(fns[d], rvals, svals, prev)
            d_rows, d_sh, d_prev = vjp((dy_refs[d][...], [dst[d, k] for k in range(n_state)]))
            for ref, val in zip(out_refs[d * no:(d + 1) * no], post(d_rows)):
                ref[...] = val.astype(ref.dtype)
            d_shared = d_sh if d_shared is None else [a + b for a, b in zip(d_shared, d_sh)]
            for k in range(n_state):
                dst[d, k] = d_prev[k]
        for ref, gr in zip(dsh_refs, d_shared):
            @pl.when(r == 0)
            def _(ref=ref, gr=gr):
                ref[...] = gr

            @pl.when(r != 0)
            def _(ref=ref, gr=gr):
                ref[...] += gr

    arrs, specs = [], []
    for order in orders:
        for it in rows:
            a, s = _row_in(it, CHUNK, order)
            arrs.append(a)
            specs.append(s)
    for s_ in shared:
        arrs.append(s_)
        specs.append(_const_spec(s_.shape))
    for sin, order in zip(states_in, orders):
        arrs.append(sin)
        specs.append(pl.BlockSpec((None, n_state, LANES, LANES), lambda r, o=order: (o(r), 0, 0, 0)))
    for order in orders:
        a, s = _row_in(dy, CHUNK, order)
        arrs.append(a)
        specs.append(s)
    out_specs = [pl.BlockSpec((CHUNK, w), lambda r, o=o: (o(r), 0)) for o in orders for w, _ in outs]
    out_shape = [jax.ShapeDtypeStruct((n_rows, w), dt) for _ in orders for w, dt in outs]
    for s_ in shared:
        out_specs.append(_const_spec(s_.shape))
        out_shape.append(jax.ShapeDtypeStruct(s_.shape, F32))
    res = pl.pallas_call(body, name=name, grid=(nct,), in_specs=specs, out_specs=out_specs, out_shape=out_shape,
                         scratch_shapes=[pltpu.VMEM((nd, n_state, LANES, LANES), F32)],
                         compiler_params=_cp(("arbitrary",)))(*arrs)
    return [list(res[d * no:(d + 1) * no]) for d in range(nd)] + [list(res[nd * no:])]


def _make_ssd_chunk(direction):
    rev = direction == 1
    base = 8 * direction

    def fn(rows, shared, prev):
        xs, bms, cms, dtraw = rows[0:4], rows[4:6], rows[6:8], rows[8]
        dt_bias, a_log = shared
        ln = dtraw.shape[0]
        dt_all = _softplus(dtraw + dt_bias)
        a_all = dt_all * (-jnp.exp(a_log))
        r_i = lax.broadcasted_iota(jnp.int32, (ln, ln), 0)
        c_i = lax.broadcasted_iota(jnp.int32, (ln, ln), 1)
        tri = (r_i <= c_i) if rev else (r_i >= c_i)
        a_cum_all = jnp.dot(tri.astype(F32), a_all, precision=lax.Precision.HIGHEST, preferred_element_type=F32)
        a_tot_all = jnp.sum(a_all, axis=0, keepdims=True)
        first = lax.broadcasted_iota(jnp.int32, (ln, LANES), 1) < SSD_HD
        first_row = lax.broadcasted_iota(jnp.int32, (LANES, 1), 0) < SSD_HD

        def lmat(acol):
            a_b = jnp.broadcast_to(acol, (ln, ln))
            seg = a_b - a_b.T
            return jnp.where(tri, jnp.exp(jnp.where(tri, seg, 0.0)), 0.0)

        ys, new = [], []
        for g in range(2):
            bm, cm = bms[g], cms[g]
            cb = _mxu(cm, bm, _NT)
            for jj in range(2):
                pr = 2 * g + jj
                h0, h1 = base + 2 * pr, base + 2 * pr + 1
                ac0, ac1 = _col(a_cum_all, h0), _col(a_cum_all, h1)
                at0, at1 = _col(a_tot_all, h0), _col(a_tot_all, h1)
                dt_pair = jnp.where(first, _col(dt_all, h0), _col(dt_all, h1))
                acum_pair = jnp.where(first, ac0, ac1)
                atot_pair = jnp.where(first[0:1], at0, at1)
                xd = xs[pr] * dt_pair
                st = _mxu(xd * jnp.exp(atot_pair - acum_pair), bm, _TN)
                new.append(prev[pr] * jnp.where(first_row, jnp.exp(at0), jnp.exp(at1)) + st)
                y0 = _mxu(cb * lmat(ac0), xd)
                y1 = _mxu(cb * lmat(ac1), xd)
                y_off = _mxu(cm, prev[pr], _NT) * jnp.exp(acum_pair)
                ys.append(jnp.where(first, y0, y1) + y_off)
        return jnp.concatenate(ys, axis=1), new

    return fn


def _make_ret_chunk(direction):
    rev = direction == 1
    base = 4 * direction

    def fn(rows, shared, prev):
        qs, ks, vs = rows[0:4], rows[4:8], rows[8:12]
        lg_all = -jnp.exp(shared[0])
        ln = qs[0].shape[0]
        pos = lax.broadcasted_iota(jnp.int32, (ln, 1), 0).astype(F32)
        r_i = lax.broadcasted_iota(jnp.int32, (ln, ln), 0)
        c_i = lax.broadcasted_iota(jnp.int32, (ln, ln), 1)
        diff = ((c_i - r_i) if rev else (r_i - c_i))
        mask = diff >= 0
        dpos = jnp.maximum(diff, 0).astype(F32)
        k_pow = pos if rev else (ln - 1.0 - pos)
        q_pow = (ln - pos) if rev else (pos + 1.0)
        ys, new = [], []
        for h in range(RET_HEADS):
            lg = _col(lg_all, base + h)
            dmat = jnp.where(mask, jnp.exp(dpos * lg), 0.0)
            st = _mxu(ks[h] * jnp.exp(k_pow * lg), vs[h], _TN)
            new.append(prev[h] * jnp.exp(ln * lg) + st)
            s = _mxu(qs[h], ks[h], _NT) * dmat
            ys.append(_mxu(s, vs[h]) + _mxu(qs[h], prev[h]) * jnp.exp(q_pow * lg))
        return jnp.concatenate(ys, axis=1), new

    return fn


def _conv_pre(x, w, b, t_idx, n_rows, m_ctx):
    is_start = (t_idx == 0) | (t_idx == m_ctx)
    is_end = (t_idx == m_ctx - 1) | (t_idx == n_rows - 1)
    xp = jnp.where(is_start, 0.0, pltpu.roll(x, 1, axis=0))
    xn = jnp.where(is_end, 0.0, pltpu.roll(x, n_rows - 1, axis=0))
    return w[0:1] * xp + w[1:2] * x + w[2:3] * xn + b, xp, xn, is_start, is_end


def conv_fwd(x, conv_w, conv_b, m_ctx):
    n_rows, width = x.arr.shape[0], x.width
    c0 = x.off // LANES

    def body(x_ref, w_ref, b_ref, o_ref):
        t_idx = lax.broadcasted_iota(jnp.int32, (n_rows, 1), 0)
        pre = _conv_pre(x_ref[...], w_ref[...], b_ref[...], t_idx, n_rows, m_ctx)[0]
        o_ref[...] = pre * jax.nn.sigmoid(pre)

    return pl.pallas_call(
        body, name="conv_fwd", grid=(width // LANES,),
        in_specs=[pl.BlockSpec((n_rows, LANES), lambda c: (0, c0 + c)),
                  pl.BlockSpec((3, LANES), lambda c: (0, c)), pl.BlockSpec((1, LANES), lambda c: (0, c))],
        out_specs=pl.BlockSpec((n_rows, LANES), lambda c: (0, c)),
        out_shape=jax.ShapeDtypeStruct((n_rows, width), F32),
        compiler_params=_cp(("parallel",)))(x.arr, conv_w, conv_b)


def conv_bwd(x, conv_w, conv_b, dy_a, dy_b, dxs_extra, m_ctx):
    n_rows, width = x.arr.shape[0], x.width
    c0 = x.off // LANES
    n_extra = dxs_extra.shape[1] // LANES

    def body(x_ref, w_ref, b_ref, dya_ref, dyb_ref, ex_ref, dx_ref, dw_ref, db_ref):
        c = pl.program_id(0)
        t_idx = lax.broadcasted_iota(jnp.int32, (n_rows, 1), 0)
        w = w_ref[...]
        pre, xp, xn, is_start, is_end = _conv_pre(x_ref[...], w, b_ref[...], t_idx, n_rows, m_ctx)
        sg = jax.nn.sigmoid(pre)
        dyv = dya_ref[...] + dyb_ref[...] + jnp.where(c < n_extra, ex_ref[...], 0.0)
        dpre = dyv * (sg * (1.0 + pre * (1.0 - sg)))
        d_next = jnp.where(is_end, 0.0, pltpu.roll(dpre, n_rows - 1, axis=0))
        d_prev = jnp.where(is_start, 0.0, pltpu.roll(dpre, 1, axis=0))
        dx_ref[...] = (w[1:2] * dpre + w[0:1] * d_next + w[2:3] * d_prev).astype(dx_ref.dtype)
        dw_ref[...] = jnp.concatenate([jnp.sum(dpre * xp, axis=0, keepdims=True),
                                       jnp.sum(dpre * x_ref[...], axis=0, keepdims=True),
                                       jnp.sum(dpre * xn, axis=0, keepdims=True)], axis=0)
        db_ref[...] = jnp.sum(dpre, axis=0, keepdims=True)

    return pl.pallas_call(
        body, name="conv_bwd", grid=(width // LANES,),
        in_specs=[pl.BlockSpec((n_rows, LANES), lambda c: (0, c0 + c)),
                  pl.BlockSpec((3, LANES), lambda c: (0, c)), pl.BlockSpec((1, LANES), lambda c: (0, c)),
                  pl.BlockSpec((n_rows, LANES), lambda c: (0, c)), pl.BlockSpec((n_rows, LANES), lambda c: (0, c)),
                  pl.BlockSpec((n_rows, LANES), lambda c: (0, jnp.minimum(c, n_extra - 1)))],
        out_specs=[pl.BlockSpec((n_rows, LANES), lambda c: (0, c)),
                   pl.BlockSpec((3, LANES), lambda c: (0, c)), pl.BlockSpec((1, LANES), lambda c: (0, c))],
        out_shape=[jax.ShapeDtypeStruct((n_rows, width), CDT), jax.ShapeDtypeStruct((3, width), F32),
                   jax.ShapeDtypeStruct((1, width), F32)],
        compiler_params=_cp(("parallel",)))(x.arr, conv_w, conv_b, dy_a, dy_b, dxs_extra)


def loss_head(h, target, final_w, m_ctx):
    n_rows, d = h.shape
    tm = min(ROW_TILE, n_rows)
    nb_ctx = m_ctx // tm

    def f(hb, w, tgt):
        err = _rms(hb, w) - tgt
        return 0.5 * jnp.sum(jnp.mean(err * err, axis=-1))

    def body(h_ref, t_ref, w_ref, loss_ref, dh_ref, dw_ref):
        i = pl.program_id(0)

        @pl.when(i < nb_ctx)
        def _():
            dh_ref[...] = jnp.zeros_like(dh_ref)

        @pl.when(i == 0)
        def _():
            loss_ref[...] = jnp.zeros_like(loss_ref)
            dw_ref[...] = jnp.zeros_like(dw_ref)

        @pl.when(i >= nb_ctx)
        def _():
            val, vjp = jax.vjp(lambda hb, w: f(hb, w, t_ref[...]), h_ref[...], w_ref[...])
            dh, dw = vjp(jnp.ones((), F32))
            dh_ref[...] = dh
            dw_ref[...] += dw
            loss_ref[...] += jnp.broadcast_to(val, loss_ref.shape)

    return pl.pallas_call(
        body, name="loss_head", grid=(n_rows // tm,),
        in_specs=[pl.BlockSpec((tm, d), lambda i: (i, 0)),
                  pl.BlockSpec((tm, d), lambda i: (jnp.maximum(i - nb_ctx, 0), 0)), _const_spec((1, d))],
        out_specs=[_const_spec((1, LANES)), pl.BlockSpec((tm, d), lambda i: (i, 0)), _const_spec((1, d))],
        out_shape=[jax.ShapeDtypeStruct((1, LANES), F32), jax.ShapeDtypeStruct((n_rows, d), F32),
                   jax.ShapeDtypeStruct((1, d), F32)],
        compiler_params=_cp(("arbitrary",)))(h, target, final_w)


def adamw(name, w, m, v, g_parts):
    lead, rows, cols = w.shape
    tr = _pick(rows, (256, 128, 64, 32, 16, 8))
    npart = len(g_parts)
    c1 = 1.0 - ADAM_B1 ** ADAM_STEP
    c2 = 1.0 - ADAM_B2 ** ADAM_STEP

    def body(*refs):
        w_ref, m_ref, v_ref = refs[:3]
        g = refs[3][...].astype(F32)
        for r in refs[4:3 + npart]:
            g = g + r[...].astype(F32)
        g_ref, d_ref, nm_ref, nv_ref = refs[3 + npart:]
        nm = ADAM_B1 * m_ref[...] + (1.0 - ADAM_B1) * g
        nv = ADAM_B2 * v_ref[...] + (1.0 - ADAM_B2) * (g * g)
        g_ref[...] = g
        nm_ref[...] = nm
        nv_ref[...] = nv
        d_ref[...] = -ADAM_LR * ((nm / c1) / (jnp.sqrt(nv / c2) + ADAM_EPS) + ADAM_WD * w_ref[...])

    spec = pl.BlockSpec((None, tr, cols), lambda l, i: (l, i, 0))
    return pl.pallas_call(
        body, name=name, grid=(lead, rows // tr), in_specs=[spec] * (3 + npart), out_specs=[spec] * 4,
        out_shape=[jax.ShapeDtypeStruct(w.shape, F32)] * 4,
        compiler_params=_cp(("parallel", "parallel")))(w, m, v, *g_parts)


MESH = pl.DeviceIdType.MESH
_HBM = pl.BlockSpec(memory_space=pl.ANY)


def _chip_peers():
    x, y, c = lax.axis_index("x"), lax.axis_index("y"), lax.axis_index("c")
    return x, y, c, [(1 - x, y), (x, 1 - y), (1 - x, 1 - y)]


def _window(ref, kind, chip, rows, cols):
    if kind == "cols":
        return ref.at[:, pl.ds(pl.multiple_of(chip * cols, LANES), cols)]
    if kind == "rows":
        return ref.at[pl.ds(pl.multiple_of(chip * rows, 8), rows), :]
    return ref.at[chip]


def _gathered_shape(kind, rows, cols):
    return {"cols": (rows, 4 * cols), "rows": (4 * rows, cols), "slices": (4, rows, cols)}[kind]


def gather_layers(name, shards, kinds):
    n = len(shards)

    def body(*refs):
        x_refs, o_refs = refs[:n], refs[n:2 * n]
        send_sems, recv_sems, local_sems = refs[2 * n:]
        x, y, c, peers = _chip_peers()
        me = 2 * x + y
        started = []
        for a in range(n):
            _, rows, cols = shards[a].shape
            src = x_refs[a].at[c]
            mine = pltpu.make_async_copy(src, _window(o_refs[a], kinds[a], me, rows, cols), local_sems.at[a])
            mine.start()
            started.append(mine.wait)
            for k, (px, py) in enumerate(peers):
                cp = pltpu.make_async_remote_copy(
                    src_ref=src, dst_ref=_window(o_refs[a], kinds[a], me, rows, cols), send_sem=send_sems.at[3 * a + k],
                    recv_sem=recv_sems.at[3 * a + k], device_id=(px, py, c), device_id_type=MESH)
                cp.start()
                started.append(cp.wait_send)
        for a in range(n):
            _, rows, cols = shards[a].shape
            for k, (px, py) in enumerate(peers):
                pltpu.make_async_remote_copy(
                    src_ref=x_refs[a].at[c], dst_ref=_window(o_refs[a], kinds[a], 2 * px + py, rows, cols),
                    send_sem=send_sems.at[3 * a + k], recv_sem=recv_sems.at[3 * a + k], device_id=(px, py, c),
                    device_id_type=MESH).wait_recv()
        for wait in started:
            wait()

    return pl.pallas_call(
        body, name=name, in_specs=[_HBM] * n, out_specs=[_HBM] * n,
        out_shape=[jax.ShapeDtypeStruct(_gathered_shape(kinds[a], *shards[a].shape[1:]), shards[a].dtype)
                   for a in range(n)],
        scratch_shapes=[pltpu.SemaphoreType.DMA((3 * n,)), pltpu.SemaphoreType.DMA((3 * n,)),
                        pltpu.SemaphoreType.DMA((n,))],
        )(*shards)


def scatter_pieces(name, pieces):
    n = len(pieces)

    def body(*refs):
        p_refs, o_refs = refs[:n], refs[n:2 * n]
        send_sems, recv_sems, local_sems = refs[2 * n:]
        x, y, c, peers = _chip_peers()
        me = 2 * x + y
        started = []
        for a in range(n):
            mine = pltpu.make_async_copy(p_refs[a].at[me], o_refs[a].at[me], local_sems.at[a])
            mine.start()
            started.append(mine.wait)
            for k, (px, py) in enumerate(peers):
                cp = pltpu.make_async_remote_copy(
                    src_ref=p_refs[a].at[2 * px + py], dst_ref=o_refs[a].at[me], send_sem=send_sems.at[3 * a + k],
                    recv_sem=recv_sems.at[3 * a + k], device_id=(px, py, c), device_id_type=MESH)
                cp.start()
                started.append(cp.wait_send)
        for a in range(n):
            for k, (px, py) in enumerate(peers):
                pltpu.make_async_remote_copy(
                    src_ref=p_refs[a].at[me], dst_ref=o_refs[a].at[2 * px + py], send_sem=send_sems.at[3 * a + k],
                    recv_sem=recv_sems.at[3 * a + k], device_id=(px, py, c), device_id_type=MESH).wait_recv()
        for wait in started:
            wait()

    return pl.pallas_call(
        body, name=name, in_specs=[_HBM] * n, out_specs=[_HBM] * n,
        out_shape=[jax.ShapeDtypeStruct(p.shape, p.dtype) for p in pieces],
        scratch_shapes=[pltpu.SemaphoreType.DMA((3 * n,)), pltpu.SemaphoreType.DMA((3 * n,)),
                        pltpu.SemaphoreType.DMA((n,))],
        )(*pieces)


def _pair_step(n_steps, x_ref, land, send_sems, recv_sems, credits, consume):
    x, y, c = lax.axis_index("x"), lax.axis_index("y"), lax.axis_index("c")
    sib = (x, y, 1 - c)
    i = pl.program_id(0)
    slot = i % 2

    @pl.when(i >= 2)
    def _():
        pl.semaphore_wait(credits.at[slot], 1)

    cp = pltpu.make_async_remote_copy(src_ref=x_ref, dst_ref=land.at[slot], send_sem=send_sems.at[slot],
                                      recv_sem=recv_sems.at[slot], device_id=sib, device_id_type=MESH)
    cp.start()
    cp.wait_recv()
    consume(land[slot])

    @pl.when(i < n_steps - 2)
    def _():
        pl.semaphore_signal(credits.at[slot], inc=1, device_id=sib, device_id_type=MESH)

    cp.wait_send()


def _pair_call(name, body, n_steps, in_specs, out_spec, out_shape, blk_shape, dtype, operands, extra_scratch=()):
    grid_spec = pltpu.PrefetchScalarGridSpec(
        num_scalar_prefetch=1, grid=(n_steps,), in_specs=in_specs, out_specs=out_spec,
        scratch_shapes=[pltpu.VMEM((2,) + blk_shape, dtype), pltpu.SemaphoreType.DMA((2,)),
                        pltpu.SemaphoreType.DMA((2,)), pltpu.SemaphoreType.REGULAR((2,)), *extra_scratch])
    return pl.pallas_call(body, name=name, grid_spec=grid_spec, out_shape=out_shape,
                          compiler_params=_cp(("arbitrary",)))(*operands)


def _place():
    return jnp.stack([lax.axis_index("x"), lax.axis_index("y"), lax.axis_index("c")]).astype(jnp.int32)


def _pair_rows(rows, row_bytes):
    for cand in (4096, 2048, 1024, 768, 512, 384, 256, 192, 128, 96, 64, 48, 32, 16):
        if rows % cand == 0 and cand * row_bytes <= PAIR_BLOCK_BYTES:
            return cand
    return _pick(rows, (16, 8))


def exchange_both(name, mine):
    rows, cols = mine.shape
    tr = _pair_rows(rows, cols * mine.dtype.itemsize)
    n_steps = rows // tr

    def body(s_ref, x_ref, o_ref, land, send_sems, recv_sems, credits):
        c = lax.axis_index("c")
        o_ref[c] = x_ref[...]

        def consume(v):
            o_ref[1 - c] = v
        _pair_step(n_steps, x_ref, land, send_sems, recv_sems, credits, consume)

    return _pair_call(name, body, n_steps, [pl.BlockSpec((tr, cols), lambda i, s: (i, 0))],
                      pl.BlockSpec((2, tr, cols), lambda i, s: (0, i, 0)),
                      jax.ShapeDtypeStruct((2, rows, cols), mine.dtype), (tr, cols), mine.dtype, (_place(), mine))


def exchange_add(name, both_layers):
    _, rows, cols = both_layers.shape
    tr = _pair_rows(rows, cols * both_layers.dtype.itemsize)
    nb = rows // tr
    flat = both_layers.reshape(2 * rows, cols)

    def body(s_ref, x_ref, m_ref, o_ref, land, send_sems, recv_sems, credits):
        def consume(v):
            o_ref[...] = (m_ref[...].astype(F32) + v.astype(F32)).astype(o_ref.dtype)
        _pair_step(nb, x_ref, land, send_sems, recv_sems, credits, consume)

    return _pair_call(name, body, nb,
                      [pl.BlockSpec((tr, cols), lambda i, s: ((1 - s[2]) * nb + i, 0)),
                       pl.BlockSpec((tr, cols), lambda i, s: (s[2] * nb + i, 0))],
                      pl.BlockSpec((tr, cols), lambda i, s: (i, 0)), jax.ShapeDtypeStruct((rows, cols), CDT),
                      (tr, cols), flat.dtype, (_place(), flat, flat))


def sum_exchange(name, parts):
    npart, rows, cols = parts.shape
    tr = _pair_rows(rows, cols * 4)
    n_steps = rows // tr

    def body(s_ref, x_ref, o_ref, land, send_sems, recv_sems, credits, mine):
        c = lax.axis_index("c")
        acc = x_ref[0].astype(F32)
        for k in range(1, npart):
            acc = acc + x_ref[k].astype(F32)
        mine[...] = acc
        o_ref[c] = acc

        def consume(v):
            o_ref[1 - c] = v
        _pair_step(n_steps, mine, land, send_sems, recv_sems, credits, consume)

    return _pair_call(name, body, n_steps, [pl.BlockSpec((npart, tr, cols), lambda i, s: (0, i, 0))],
                      pl.BlockSpec((2, tr, cols), lambda i, s: (0, i, 0)),
                      jax.ShapeDtypeStruct((2, rows, cols), F32), (tr, cols), F32, (_place(), parts),
                      extra_scratch=(pltpu.VMEM((tr, cols), F32),))


def allreduce_small(name, buf):
    rows = buf.shape[0]

    def body(x_ref, out_ref, gath, send_sems, recv_sems):
        x, y, c = lax.axis_index("x"), lax.axis_index("y"), lax.axis_index("c")
        me = 4 * x + 2 * y + c
        masks = [(k >> 2 & 1, k >> 1 & 1, k & 1) for k in range(1, 8)]

        def flip(v, bit):
            return 1 - v if bit else v

        sends = []
        for k, (bx, by, bc) in enumerate(masks):
            cp = pltpu.make_async_remote_copy(src_ref=x_ref, dst_ref=gath.at[me], send_sem=send_sems.at[k],
                                              recv_sem=recv_sems.at[k],
                                              device_id=(flip(x, bx), flip(y, by), flip(c, bc)), device_id_type=MESH)
            cp.start()
            sends.append(cp)
        gath[me] = x_ref[...]
        for k, (bx, by, bc) in enumerate(masks):
            px, py, pc = flip(x, bx), flip(y, by), flip(c, bc)
            pltpu.make_async_remote_copy(src_ref=x_ref, dst_ref=gath.at[4 * px + 2 * py + pc],
                                         send_sem=send_sems.at[k], recv_sem=recv_sems.at[k],
                                         device_id=(px, py, pc), device_id_type=MESH).wait_recv()
        for cp in sends:
            cp.wait_send()
        acc = gath[0]
        for d in range(1, 8):
            acc = acc + gath[d]
        out_ref[...] = acc

    return pl.pallas_call(
        body, name=name, in_specs=[pl.BlockSpec(memory_space=pltpu.VMEM)],
        out_specs=pl.BlockSpec(memory_space=pltpu.VMEM), out_shape=jax.ShapeDtypeStruct(buf.shape, F32),
        scratch_shapes=[pltpu.VMEM((8, rows, LANES), F32), pltpu.SemaphoreType.DMA((7,)),
                        pltpu.SemaphoreType.DMA((7,))],
        )(buf)


def _pack_flat(arrs, dtype, width, row_mult=8):
    flat = jnp.concatenate([a.reshape(-1).astype(dtype) for a in arrs])
    pad = (-flat.shape[0]) % (row_mult * width)
    if pad:
        flat = jnp.concatenate([flat, jnp.zeros((pad,), dtype)])
    return flat.reshape(-1, width)


def _unpack_flat(buf, shapes):
    flat = buf.reshape(-1)
    out, off = [], 0
    for s in shapes:
        n = math.prod(s)
        out.append(flat[off:off + n].reshape(s))
        off += n
    return out


def _in_to_padded(w):
    parts = []
    for name in IN_NEW_ORDER:
        _, width, o_off, o_w = IN_LAYOUT[name]
        parts.append(w[..., o_off:o_off + o_w])
        if o_w < width:
            parts.append(jnp.zeros(w.shape[:-1] + (width - o_w,), w.dtype))
    used = sum(IN_LAYOUT[n][1] for n in IN_NEW_ORDER)
    parts.append(jnp.zeros(w.shape[:-1] + (IN_PAD - used,), w.dtype))
    return jnp.concatenate(parts, axis=-1)


def _in_from_padded(g):
    parts = []
    for name in IN_ORIG_ORDER:
        off, _, _, o_w = IN_LAYOUT[name]
        parts.append(g[..., off:off + o_w])
    return jnp.concatenate(parts, axis=-1)


def _pcol(p, name):
    off, width, _, _ = IN_LAYOUT[name]
    return Cols(p, off, width)


def _lane_pad(v, width=LANES):
    v = v.reshape(-1)
    return jnp.concatenate([v, jnp.zeros((width - v.shape[0],), v.dtype)]).reshape(1, width)


def _f_norm_mod(h, sh, sc, w):
    return (_rms(h, w) * (1.0 + sc) + sh,)


def _f_norm_mod_thru(h, sh, sc, w):
    return h, _rms(h, w) * (1.0 + sc) + sh


def _f_attn_prep(qraw, kraw, vraw, cos2, sin2, qw, kw, gq, gk):
    q = qraw * lax.rsqrt(_group_mean(qraw * qraw, gq) + NORM_EPS) * qw
    q = _rope32(q, jnp.tile(cos2, (1, 4)), jnp.tile(sin2, (1, 4))) * (ATTN_HD ** -0.5 * LOG2E)
    k = kraw * lax.rsqrt(_group_mean(kraw * kraw, gk) + NORM_EPS) * kw
    return q, _rope32(k, cos2, sin2), vraw


def _f_ssd_finish(yf, yb, xs, z, d_exp, nw):
    y = (yf + yb + d_exp * xs) * (z * jax.nn.sigmoid(z))
    return (_rms(y, nw),)


def _f_ret_prep(rq, rk, cos1, sin1):
    cos_full, sin_signed = jnp.tile(cos1, (1, 4)), jnp.tile(sin1, (1, 4))
    return _rope64(rq, cos_full, sin_signed), _rope64(rk, cos_full, sin_signed) * (RET_DK ** -0.5)


def _f_ret_finish(yf, yb, g, gw):
    y = yf + yb
    outs = []
    for h in range(RET_HEADS):
        yh = y[:, h * RET_DK:(h + 1) * RET_DK]
        yc = yh - jnp.mean(yh, axis=-1, keepdims=True)
        outs.append(yc * lax.rsqrt(jnp.mean(yc * yc, axis=-1, keepdims=True) + NORM_EPS))
    return (jnp.concatenate(outs, axis=1) * gw * (g * jax.nn.sigmoid(g)),)


def _f_merge(p0, p1, p2, g0, g1, g2):
    return (jax.nn.sigmoid(g0) * p0 + jax.nn.sigmoid(g1) * p1 + jax.nn.sigmoid(g2) * p2,)


def _f_mid(h, mix, g1, sh2, sc2, w2):
    h_mid = h + g1 * mix
    return h_mid, _rms(h_mid, w2) * (1.0 + sc2) + sh2


def _f_sqrelu(a):
    r = jnp.maximum(a, 0.0)
    return (r * r,)


def _f_residual(h_mid, o, g2):
    return (h_mid + g2 * o,)


def _f_silu(x):
    return (x * jax.nn.sigmoid(x),)


def _f_bias(x, b):
    return (x + b,)


def _ssd_rows(xbc, p):
    rows = [Cols(xbc, LANES * k, LANES) for k in range(4)]
    rows += [Cols(xbc, 512 + LANES * g, LANES) for g in range(2)]
    rows += [Cols(xbc, 768 + LANES * g, LANES) for g in range(2)]
    return rows + [_pcol(p, "dt")]


def _ret_rows(rq, rk, p):
    off_v = IN_LAYOUT["rv"][0]
    return ([Cols(rq, LANES * h, LANES) for h in range(4)] + [Cols(rk, LANES * h, LANES) for h in range(4)]
            + [Cols(p, off_v + LANES * h, LANES) for h in range(4)])


def layer_fwd(li, h, mod, lw, tabs, m_ctx):
    t = h.shape[0]
    nb = m_ctx // min(ROW_TILE, t)
    sh1, sc1, g1, sh2, sc2, g2 = mod
    nm = lambda s: f"l{li}_{s}"
    sv = {}
    (u,) = rowwise_fwd(nm("norm1"), _f_norm_mod, [h], [sh1, sc1], [lw["norm1_w"]], [(D_MODEL, CDT)], t, nb)
    p = mm(nm("in_proj"), u, lw["w_in"], F32)
    q, k, v = rowwise_fwd(
        nm("attn_prep"), _f_attn_prep,
        [_pcol(p, "q"), _pcol(p, "k"), _pcol(p, "v"), tabs["ca"], tabs["sa"]], [],
        [lw["qw"], lw["kw"], tabs["gq"], tabs["gk"]], [(512, CDT), (128, CDT), (128, CDT)], t, nb)
    tq = min(ATTN_TQ, m_ctx)
    kk, vv = _split_kv(k), _split_kv(v)
    ones_rows = jnp.concatenate([jnp.ones((2, 1, t), CDT), jnp.zeros((2, ATTN_ONES_ROWS - 1, t), CDT)], axis=1)
    attn_o, qT, oT, lse = attn_fwd(nm("attn"), q, kk, jnp.concatenate([vv.transpose(0, 2, 1), ones_rows], axis=1),
                                   m_ctx, tq)

    xbc = conv_fwd(_pcol(p, "xbc"), lw["conv_w"], lw["conv_b"], m_ctx)
    ssd_sh = [lw["dt_bias"], lw["a_log"]]
    yf, yb, sf, sb = scan_fwd(nm("ssd"), _make_ssd_chunk, _ssd_rows(xbc, p), ssd_sh, 4, 512, t, m_ctx)
    (ssd_o,) = rowwise_fwd(nm("ssd_fin"), _f_ssd_finish, [yf, yb, Cols(xbc, 0, 512), _pcol(p, "z")], [],
                           [lw["d_exp"], lw["ssd_nw"]], [(512, CDT)], t, nb)

    rq, rk = rowwise_fwd(nm("ret_prep"), _f_ret_prep, [_pcol(p, "rq"), _pcol(p, "rk"), tabs["rc"], tabs["rs"]],
                         [], [], [(512, F32), (512, F32)], t, nb)
    rf, rb, rsf, rsb = scan_fwd(nm("ret"), _make_ret_chunk, _ret_rows(rq, rk, p), [lw["ret_lg"]], 4, 512, t, m_ctx)
    (ret_o,) = rowwise_fwd(nm("ret_fin"), _f_ret_finish, [rf, rb, _pcol(p, "rg")], [], [lw["ret_gw"]],
                           [(512, CDT)], t, nb)

    pbs = [mm(nm(f"branch{b}"), br, lw["w_branch"][b], CDT) for b, br in enumerate((attn_o, ssd_o, ret_o))]
    gl = [Cols(p, 1024 * b, 1024) for b in range(3)]
    (merged,) = rowwise_fwd(nm("merge"), _f_merge, pbs + gl, [], [], [(D_MODEL, CDT)], t, nb)
    mix = mm(nm("out_proj"), merged, lw["w_out"], F32)
    h_mid, vv2 = rowwise_fwd(nm("mid"), _f_mid, [h, mix], [g1, sh2, sc2], [lw["norm2_w"]],
                             [(D_MODEL, F32), (D_MODEL, CDT)], t, nb)
    a = mm(nm("mlp1"), vv2, lw["w_mlp1"], CDT)
    (hh,) = rowwise_fwd(nm("sqrelu"), _f_sqrelu, [a], [], [], [(a.shape[1], CDT)], t, nb)
    o = mm(nm("mlp2"), hh, lw["w_mlp2"], F32)
    (h_out,) = rowwise_fwd(nm("resid"), _f_residual, [h_mid, o], [g2], [], [(D_MODEL, F32)], t, nb)
    sv.update(h=h, u=u, p=p, qT=qT, kk=kk, vv=vv, oT=oT, lse=lse, attn_o=attn_o, xbc=xbc, yf=yf, yb=yb,
              sf=sf, sb=sb, ssd_o=ssd_o, rq=rq, rk=rk, rf=rf, rb=rb, rsf=rsf, rsb=rsb, ret_o=ret_o, pbs=pbs,
              merged=merged, mix=mix, h_mid=h_mid, v=vv2, a=a, hh=hh, o=o)
    return h_out, sv


def layer_bwd(li, dh_out, sv, mod, lw, tabs, m_ctx):
    t = dh_out.shape[0]
    nb = m_ctx // min(ROW_TILE, t)
    sh1, sc1, g1, sh2, sc2, g2 = mod
    nm = lambda s: f"l{li}_{s}_bwd"
    gw = {}
    p = sv["p"]
    (do,), (dg2,), _ = rowwise_bwd(nm("resid"), _f_residual, [sv["h_mid"], sv["o"]], [g2], [], [dh_out],
                                   [False, True], [], [CDT], t, nb)
    dhh = mm(nm("mlp2_dx"), do, lw["w_mlp2"], CDT, transpose_b=True)
    gw["w_mlp2"] = mm_tn(nm("mlp2_dw"), sv["hh"], do, CDT, ("rows", lw["w_mlp2"].shape[0] // 4))
    (da,), _, _ = rowwise_bwd(nm("sqrelu"), _f_sqrelu, [sv["a"]], [], [], [dhh], [True], [], [CDT], t, nb)
    dv = mm(nm("mlp1_dx"), da, lw["w_mlp1"], F32, transpose_b=True)
    gw["w_mlp1"] = mm_tn(nm("mlp1_dw"), sv["v"], da, CDT, ("cols", lw["w_mlp1"].shape[1] // 4))
    (dh_a, dmix), (dg1, dsh2, dsc2), (gw["norm2_w"],) = rowwise_bwd(
        nm("mid"), _f_mid, [sv["h"], sv["mix"]], [g1, sh2, sc2], [lw["norm2_w"]], [dh_out, dv],
        [True, True], [True], [F32, CDT], t, nb)
    dmerged = mm(nm("out_dx"), dmix, lw["w_out"], CDT, transpose_b=True)
    gw["w_out"] = mm_tn(nm("out_dw"), sv["merged"], dmix, CDT, ("rows", lw["w_out"].shape[0] // 4))
    gl = [Cols(p, 1024 * b, 1024) for b in range(3)]
    dmg, _, _ = rowwise_bwd(nm("merge"), _f_merge, sv["pbs"] + gl, [], [], [dmerged], [True] * 6, [], [CDT] * 6,
                            t, nb)
    dpb, dgl = dmg[:3], dmg[3:]
    brs = (sv["attn_o"], sv["ssd_o"], sv["ret_o"])
    d_attn_o = mm(nm("branch0_dx"), dpb[0], lw["w_branch"][0], CDT, transpose_b=True)
    d_ssd_o = mm(nm("branch1_dx"), dpb[1], lw["w_branch"][1], F32, transpose_b=True)
    d_ret_o = mm(nm("branch2_dx"), dpb[2], lw["w_branch"][2], F32, transpose_b=True)
    n_loc = lw["w_branch"].shape[2] // 4
    gw["w_branch"] = jnp.stack([mm_tn(nm(f"branch{b}_dw"), brs[b], dpb[b], CDT, ("cols", n_loc)) for b in range(3)],
                               axis=1).reshape(4, -1, n_loc)
    tq = min(ATTN_TQ, m_ctx)
    dq_rows, dk_s, dv_s = attn_bwd(nm("attn"), sv["qT"], d_attn_o, sv["oT"], sv["lse"], sv["kk"],
                                   sv["kk"].transpose(0, 2, 1), sv["vv"], m_ctx)
    (dq_raw, dk_raw, dv_raw), _, (gw["qw"], gw["kw"]) = rowwise_bwd(
        nm("attn_prep"), _f_attn_prep,
        [_pcol(p, "q"), _pcol(p, "k"), _pcol(p, "v"), tabs["ca"], tabs["sa"]], [],
        [lw["qw"], lw["kw"], tabs["gq"], tabs["gk"]],
        [dq_rows, _merge_kv(dk_s) * LN2, _merge_kv(dv_s)],
        [True, True, True, False, False], [True, True, False, False], [CDT] * 3, t, nb)
    (dy_ssd, dxs_fin, dz), _, (gw["d_exp"], gw["ssd_nw"]) = rowwise_bwd(
        nm("ssd_fin"), _f_ssd_finish, [sv["yf"], sv["yb"], Cols(sv["xbc"], 0, 512), _pcol(p, "z")], [],
        [lw["d_exp"], lw["ssd_nw"]], [d_ssd_o], [True, False, True, True], [True, True], [F32, F32, CDT], t, nb)
    ssd_sh = [lw["dt_bias"], lw["a_log"]]
    post_ssd = lambda d: [jnp.concatenate(d[0:8], axis=1), d[8]]
    (dxbc_f, ddt_f), dsh_f = scan_bwd(nm("ssd_f"), _make_ssd_chunk, _ssd_rows(sv["xbc"], p), ssd_sh, (sv["sf"],),
                                      dy_ssd, post_ssd, [(1024, F32), (LANES, F32)], 4, t, m_ctx, dirs=(0,))
    (dxbc_b, ddt_b), dsh_b = scan_bwd(nm("ssd_b"), _make_ssd_chunk, _ssd_rows(sv["xbc"], p), ssd_sh, (sv["sb"],),
                                      dy_ssd, post_ssd, [(1024, F32), (LANES, F32)], 4, t, m_ctx, dirs=(1,))
    gw["dt_bias"], gw["a_log"] = dsh_f[0] + dsh_b[0], dsh_f[1] + dsh_b[1]
    ddt = (ddt_f + ddt_b).astype(CDT)
    dxbc_raw, gw["conv_w"], gw["conv_b"] = conv_bwd(_pcol(p, "xbc"), lw["conv_w"], lw["conv_b"], dxbc_f, dxbc_b,
                                                    dxs_fin, m_ctx)
    (dy_ret, drg), _, (gw["ret_gw"],) = rowwise_bwd(
        nm("ret_fin"), _f_ret_finish, [sv["rf"], sv["rb"], _pcol(p, "rg")], [], [lw["ret_gw"]], [d_ret_o],
        [True, False, True], [True], [F32, CDT], t, nb)
    post_ret = lambda d: [jnp.concatenate(d[0:4], axis=1), jnp.concatenate(d[4:8], axis=1),
                          jnp.concatenate(d[8:12], axis=1)]
    rrows = _ret_rows(sv["rq"], sv["rk"], p)
    (dq_f, dk_f, dv_f), (dq_b, dk_b, dv_b), (gw["ret_lg"],) = scan_bwd(
        nm("ret"), _make_ret_chunk, rrows, [lw["ret_lg"]], (sv["rsf"], sv["rsb"]), dy_ret, post_ret,
        [(512, F32)] * 3, 4, t, m_ctx)
    drv = (dv_f + dv_b).astype(CDT)
    (drq, drk), _, _ = rowwise_bwd(nm("ret_prep"), _f_ret_prep,
                                   [_pcol(p, "rq"), _pcol(p, "rk"), tabs["rc"], tabs["rs"]], [], [],
                                   [(dq_f, dq_b), (dk_f, dk_b)], [True, True, False, False], [], [CDT, CDT], t, nb)
    pieces = {"gates": None, "xbc": dxbc_raw, "q": dq_raw, "z": dz, "rq": drq, "rk": drk, "rv": drv, "rg": drg,
              "k": dk_raw, "v": dv_raw, "dt": ddt}
    cols = list(dgl) + [pieces[n] for n in IN_NEW_ORDER[1:]]
    used = sum(c.shape[1] for c in cols)
    cols.append(jnp.zeros((t, IN_PAD - used), CDT))
    dp = jnp.concatenate(cols, axis=1)
    du = mm(nm("in_dx"), dp, lw["w_in"], F32, transpose_b=True)
    gw["w_in"] = mm_tn(nm("in_dw"), sv["u"], dp, CDT)
    (dh_in,), (dsh1, dsc1), (gw["norm1_w"],) = rowwise_bwd(
        nm("norm1"), _f_norm_mod_thru, [sv["h"]], [sh1, sc1], [lw["norm1_w"]], [dh_a, du], [True], [True], [F32],
        t, nb)
    return dh_in, [dsh1, dsc1, dg1, dsh2, dsc2, dg2], gw


def _rope_tables(n_lat, m_ctx):
    rows = n_lat // GRID_W
    row = jnp.repeat(jnp.arange(rows, dtype=F32), GRID_W)
    col = jnp.tile(jnp.arange(GRID_W, dtype=F32), rows)
    nfreq = ATTN_HD // 4
    inv = ROPE_THETA ** (-jnp.arange(nfreq, dtype=F32) / nfreq)
    ang = jnp.concatenate([row[:, None] * inv, col[:, None] * inv], axis=-1)
    cos = jnp.concatenate([jnp.ones((m_ctx, ATTN_HD // 2), F32), jnp.cos(ang)], axis=0)
    sin = jnp.concatenate([jnp.zeros((m_ctx, ATTN_HD // 2), F32), jnp.sin(ang)], axis=0)
    c64 = jnp.concatenate([cos, cos], axis=1)
    s64 = jnp.concatenate([-sin, sin], axis=1)
    pos = jnp.arange(m_ctx + n_lat, dtype=F32)
    inv_r = ROPE_THETA ** (-jnp.linspace(0.0, 1.0, RET_DK // 2, dtype=F32))
    ang_r = pos[:, None] * inv_r
    rc = jnp.concatenate([jnp.cos(ang_r)] * 2, axis=1)
    rs = jnp.concatenate([-jnp.sin(ang_r), jnp.sin(ang_r)], axis=1)
    return dict(ca=jnp.tile(c64, (1, 2)), sa=jnp.tile(s64, (1, 2)), rc=rc, rs=rs, gq=_group_matrix(512, ATTN_HD),
                gk=_group_matrix(128, ATTN_HD))


def _layer_weights(full, small, layer):
    return dict(
        w_in=full["w_in"][layer], w_branch=full["w_branch"][layer], w_out=full["w_out"][layer],
        w_mlp1=full["w_mlp1"][layer], w_mlp2=full["w_mlp2"][layer],
        norm1_w=small["norm1_w"][layer][None], norm2_w=small["norm2_w"][layer][None],
        qw=jnp.tile(small["attn_q_norm"][layer], 8)[None], kw=jnp.tile(small["attn_k_norm"][layer], 2)[None],
        conv_w=small["ssd_conv_w"][layer], conv_b=small["ssd_conv_b"][layer][None],
        dt_bias=_lane_pad(small["ssd_dt_bias"][layer]), a_log=_lane_pad(small["ssd_a_log"][layer]),
        d_exp=jnp.repeat(small["ssd_d"][layer], SSD_HD)[None], ssd_nw=small["ssd_norm_w"][layer][None],
        ret_lg=_lane_pad(small["ret_log_decay"][layer]), ret_gw=small["ret_gn_w"][layer][None])


def local_step(x, c, ctx, full, small, loss_target):
    n_lat, d = x.shape
    m_ctx = ctx.shape[0]
    t = n_lat + m_ctx
    depth = small["norm1_w"].shape[0]
    tabs = _rope_tables(n_lat, m_ctx)
    h = jnp.concatenate([ctx, x], axis=0)
    cc = jnp.concatenate([small["c_ctx"][None], c, jnp.zeros((COND_ROWS - 2, d), F32)], axis=0)
    (scc,) = rowwise_fwd("cond_silu", _f_silu, [cc], [], [], [(d, CDT)], COND_ROWS, 0)
    mods, saved, lws = [], [], []
    for layer in range(depth):
        lw = _layer_weights(full, small, layer)
        mod_raw = mm(f"l{layer}_mod", scc, full["w_mod"][layer], F32)
        (mod8,) = rowwise_fwd(f"l{layer}_mod_bias", _f_bias, [mod_raw], [], [small["b_mod"][layer][None]],
                              [(6 * d, F32)], COND_ROWS, 0)
        mod = [mod8[0:2, k * d:(k + 1) * d].reshape(2, 1, d) for k in range(6)]
        h, sv = layer_fwd(layer, h, mod, lw, tabs, m_ctx)
        mods.append(mod)
        saved.append(sv)
        lws.append(lw)
    loss, dh, d_final = loss_head(h, loss_target, small["final_norm_w"][None], m_ctx)

    gbig = {k: [None] * depth for k in BIG}
    gs = {k: [None] * depth for k in SMALL if k not in ("c_ctx", "final_norm_w")}
    d_scc = None
    for layer in reversed(range(depth)):
        lw = lws[layer]
        dh, dmod, gw = layer_bwd(layer, dh, saved[layer], mods[layer], lw, tabs, m_ctx)
        dmod8 = jnp.concatenate([jnp.concatenate([g_.reshape(2, d) for g_ in dmod], axis=1),
                                 jnp.zeros((COND_ROWS - 2, 6 * d), F32)], axis=0)
        (dmod_c,), _, (db_mod,) = rowwise_bwd(f"l{layer}_mod_bias_bwd", _f_bias, [dmod8], [],
                                              [small["b_mod"][layer][None]], [dmod8], [True], [True], [CDT], COND_ROWS, 0)
        gbig["w_mod"][layer] = mm_tn(f"l{layer}_mod_dw", scc, dmod_c, CDT, ("cols", 6 * d // 4))
        part = mm(f"l{layer}_mod_dx", dmod_c, full["w_mod"][layer], F32, transpose_b=True)
        d_scc = part if d_scc is None else d_scc + part
        g_in = _in_from_padded(gw["w_in"])
        gbig["w_in"][layer] = g_in.reshape(d, 4, g_in.shape[1] // 4).transpose(1, 0, 2)
        for k in ("w_branch", "w_out", "w_mlp1", "w_mlp2"):
            gbig[k][layer] = gw[k]
        gs["b_mod"][layer] = db_mod.reshape(-1)
        gs["norm1_w"][layer] = gw["norm1_w"].reshape(-1)
        gs["norm2_w"][layer] = gw["norm2_w"].reshape(-1)
        gs["attn_q_norm"][layer] = gw["qw"].reshape(8, ATTN_HD).sum(0)
        gs["attn_k_norm"][layer] = gw["kw"].reshape(2, ATTN_HD).sum(0)
        gs["ssd_conv_w"][layer] = gw["conv_w"]
        gs["ssd_conv_b"][layer] = gw["conv_b"].reshape(-1)
        gs["ssd_dt_bias"][layer] = gw["dt_bias"][0, :16].reshape(2, 8)
        gs["ssd_a_log"][layer] = gw["a_log"][0, :16].reshape(2, 8)
        gs["ssd_d"][layer] = gw["d_exp"].reshape(SSD_HEADS, SSD_HD).sum(1)
        gs["ssd_norm_w"][layer] = gw["ssd_nw"].reshape(-1)
        gs["ret_log_decay"][layer] = gw["ret_lg"][0, :8].reshape(2, 4)
        gs["ret_gn_w"][layer] = gw["ret_gw"].reshape(-1)
    (d_cc,), _, _ = rowwise_bwd("cond_silu_bwd", _f_silu, [cc], [], [], [d_scc], [True], [], [F32], COND_ROWS, 0)
    g_small = {k: jnp.stack(v) for k, v in gs.items()}
    g_small["c_ctx"] = d_cc[0]
    g_small["final_norm_w"] = d_final.reshape(-1)
    g_big = {k: jnp.stack(v) for k, v in gbig.items()}
    return loss, dh[m_ctx:], g_big, g_small


def kernel(x, c, ctx, c_ctx, w_mod, b_mod, norm1_w, norm2_w, w_in, attn_q_norm, attn_k_norm, ssd_conv_w, ssd_conv_b, ssd_dt_bias, ssd_a_log, ssd_d, ssd_norm_w, ret_log_decay, ret_gn_w, w_branch, w_out, w_mlp1, w_mlp2, final_norm_w, loss_target, m_c_ctx, m_w_mod, m_b_mod, m_norm1_w, m_norm2_w, m_w_in, m_attn_q_norm, m_attn_k_norm, m_ssd_conv_w, m_ssd_conv_b, m_ssd_dt_bias, m_ssd_a_log, m_ssd_d, m_ssd_norm_w, m_ret_log_decay, m_ret_gn_w, m_w_branch, m_w_out, m_w_mlp1, m_w_mlp2, m_final_norm_w, v_c_ctx, v_w_mod, v_b_mod, v_norm1_w, v_norm2_w, v_w_in, v_attn_q_norm, v_attn_k_norm, v_ssd_conv_w, v_ssd_conv_b, v_ssd_dt_bias, v_ssd_a_log, v_ssd_d, v_ssd_norm_w, v_ret_log_decay, v_ret_gn_w, v_w_branch, v_w_out, v_w_mlp1, v_w_mlp2, v_final_norm_w):
    env = dict(locals())
    w_loc = {k: env[k] for k in WEIGHTS}
    m_loc = {k: env["m_" + k] for k in WEIGHTS}
    v_loc = {k: env["v_" + k] for k in WEIGHTS}
    chip = 2 * lax.axis_index("x") + lax.axis_index("y")
    core = lax.axis_index("c")

    depth = w_loc["w_mod"].shape[0]
    assert depth == 2, "the exchanges split the layers between a chip's two cores"
    shards = [w_loc[k].astype(CDT).reshape(depth, -1, w_loc[k].shape[-1]) for k in BIG]
    mine = gather_layers("gather_weights", shards, [BIG_KIND[k] for k in BIG])
    full = {}
    for k, arr in zip(BIG, mine):
        both = exchange_both("share_" + k, arr.reshape(-1, arr.shape[-1]))
        if k == "w_in":
            both = both.reshape(depth, 4, -1, both.shape[-1]).transpose(0, 2, 1, 3)
            both = _in_to_padded(both.reshape(depth, both.shape[1], -1))
        full[k] = both.reshape((depth,) + w_loc[k].shape[1:-1] + (-1,)) if BIG_KIND[k] == "cols" else \
            both.reshape((depth,) + w_loc[k].shape[1:-2] + (-1, w_loc[k].shape[-1])) if BIG_KIND[k] == "rows" else both

    cw = w_loc["ssd_conv_w"]
    cw_w = cw.shape[-1]
    placed = lax.dynamic_update_slice(jnp.zeros(cw.shape[:-1] + (4 * cw_w,), F32),
                                      cw * (core == 0).astype(F32), (0, 0, chip * cw_w))
    conv_full = _unpack_flat(allreduce_small("gather_conv_w", _pack_flat([placed], F32, LANES)), [placed.shape])[0]
    small = {k: w_loc[k] for k in SMALL}
    small["ssd_conv_w"] = conv_full

    loss_l, grad_x, g_big, g_small = local_step(x[0], c, ctx[0], full, small, loss_target[0])

    small_shapes = [g_small[k].shape for k in SMALL] + [(LANES,)]
    summed = _unpack_flat(allreduce_small("reduce_small", _pack_flat([g_small[k] for k in SMALL] + [loss_l], F32, LANES)),
                          small_shapes)
    gsum = dict(zip(SMALL, summed[:-1]))
    loss = summed[-1][0]
    gsum["ssd_conv_w"] = lax.dynamic_slice(gsum["ssd_conv_w"], (0, 0, chip * cw_w), cw.shape)

    pair = []
    for k in BIG:
        _, _, rows, cols = g_big[k].shape
        pair.append(exchange_add("pair_" + k, g_big[k].reshape(depth, 4 * rows, cols)).reshape(4, rows, cols))
    landed = scatter_pieces("scatter_grads", pair)
    g_sum = [sum_exchange("sum_" + k, parts) for k, parts in zip(BIG, landed)]

    grads, deltas, new_m, new_v = {}, {}, {}, {}
    for i, k in enumerate(BIG):
        shp = w_loc[k].shape
        three_d = lambda a, shp=shp: a.reshape((-1,) + shp[-2:])
        res = adamw("adamw_" + k, three_d(w_loc[k]), three_d(m_loc[k]), three_d(v_loc[k]), [three_d(g_sum[i])])
        grads[k], deltas[k], new_m[k], new_v[k] = [r.reshape(shp) for r in res]
    small_loc_shapes = [w_loc[k].shape for k in SMALL]
    res = adamw("adamw_small", _pack_flat([w_loc[k] for k in SMALL], F32, LANES)[None],
                _pack_flat([m_loc[k] for k in SMALL], F32, LANES)[None],
                _pack_flat([v_loc[k] for k in SMALL], F32, LANES)[None],
                [_pack_flat([gsum[k] for k in SMALL], F32, LANES)[None]])
    for dst, r in zip((grads, deltas, new_m, new_v), res):
        dst.update(dict(zip(SMALL, _unpack_flat(r, small_loc_shapes))))

    return (loss, grad_x[None], *[grads[k] for k in WEIGHTS], *[deltas[k] for k in WEIGHTS],
            *[new_m[k] for k in WEIGHTS], *[new_v[k] for k in WEIGHTS])
```

```python
import functools
import math
from typing import NamedTuple

import jax
import jax.numpy as jnp
from jax import lax
from jax.experimental import pallas as pl
from jax.experimental.pallas import tpu as pltpu

F32 = jnp.float32
CDT = jnp.bfloat16
NORM_EPS = 1e-6
ROPE_THETA = 10000.0
GRID_W = 64
D_MODEL = 1024
ATTN_HEADS, ATTN_KV, ATTN_HD = 8, 2, 64
SSD_HEADS, SSD_HD, SSD_STATE = 8, 64, 128
RET_HEADS, RET_DK = 4, 128
CHUNK = 256
ROW_TILE = 256
MM_ROWS = 768
MM_TN_ROWS = 2816
MM_VMEM_BUDGET = 44 * 1024 * 1024
ATTN_TQ, ATTN_TK = 256, 256
ATTN_ONES_ROWS = 16
ATTN_TK_BWD = 2048
LOG2E, LN2 = 1.4426950408889634, 0.6931471805599453
ATTN_TK_FWD = 2048
LANES = 128
PAIR_BLOCK_BYTES = 2 * 1024 * 1024
COND_ROWS = 16
VMEM_LIMIT = 56 * 1024 * 1024

ADAM_LR, ADAM_B1, ADAM_B2, ADAM_EPS, ADAM_WD, ADAM_STEP = 0.001, 0.9, 0.999, 1e-08, 0.01, 10

IN_LAYOUT = {
    "gates": (0, 3072, 4368, 3072), "xbc": (3072, 1024, 1280, 1024), "q": (4096, 512, 0, 512),
    "z": (4608, 512, 768, 512), "rq": (5120, 512, 2320, 512), "rk": (5632, 512, 2832, 512),
    "rv": (6144, 512, 3344, 512), "rg": (6656, 512, 3856, 512), "k": (7168, 128, 512, 128),
    "v": (7296, 128, 640, 128), "dt": (7424, 128, 2304, 16),
}
IN_PAD = 7680
IN_ORIG_ORDER = ("q", "k", "v", "z", "xbc", "dt", "rq", "rk", "rv", "rg", "gates")
IN_NEW_ORDER = ("gates", "xbc", "q", "z", "rq", "rk", "rv", "rg", "k", "v", "dt")

BIG = ("w_mod", "w_in", "w_branch", "w_out", "w_mlp1", "w_mlp2")
BIG_KIND = {"w_mod": "cols", "w_in": "slices", "w_branch": "cols", "w_out": "rows", "w_mlp1": "cols", "w_mlp2": "rows"}
SMALL = ("c_ctx", "b_mod", "norm1_w", "norm2_w", "attn_q_norm", "attn_k_norm", "ssd_conv_w", "ssd_conv_b",
         "ssd_dt_bias", "ssd_a_log", "ssd_d", "ssd_norm_w", "ret_log_decay", "ret_gn_w", "final_norm_w")
WEIGHTS = ("c_ctx", "w_mod", "b_mod", "norm1_w", "norm2_w", "w_in", "attn_q_norm", "attn_k_norm", "ssd_conv_w",
           "ssd_conv_b", "ssd_dt_bias", "ssd_a_log", "ssd_d", "ssd_norm_w", "ret_log_decay", "ret_gn_w",
           "w_branch", "w_out", "w_mlp1", "w_mlp2", "final_norm_w")


def _cp(sem):
    return pltpu.CompilerParams(dimension_semantics=sem, vmem_limit_bytes=VMEM_LIMIT)


class Cols(NamedTuple):
    arr: jax.Array
    off: int
    width: int


def _width(item):
    return item.width if isinstance(item, Cols) else item.shape[1]


def _row_in(item, rows, imap=None):
    imap = imap or (lambda i: i)
    if isinstance(item, Cols):
        assert item.off % item.width == 0
        blk = item.off // item.width
        return item.arr, pl.BlockSpec((rows, item.width), lambda i, blk=blk: (imap(i), blk))
    return item, pl.BlockSpec((rows, item.shape[1]), lambda i: (imap(i), 0))


def _const_spec(shape):
    return pl.BlockSpec(shape, lambda *_: (0,) * len(shape))


def _mxu(a, b, dims=(((1,), (0,)), ((), ()))):
    return lax.dot_general(a.astype(CDT), b.astype(CDT), dims, preferred_element_type=F32)


_NT = (((1,), (1,)), ((), ()))
_TN = (((0,), (0,)), ((), ()))


@jax.custom_vjp
def _softplus(x):
    return jnp.maximum(x, 0.0) + jnp.log1p(jnp.exp(-jnp.abs(x)))


def _softplus_fwd(x):
    return _softplus(x), x


def _softplus_bwd(x, g):
    return (g * jax.nn.sigmoid(x),)


_softplus.defvjp(_softplus_fwd, _softplus_bwd)


def _group_mean_impl(x, gmat):
    hi = x.astype(CDT)
    lo = (x - hi.astype(F32)).astype(CDT)
    return (jnp.dot(hi, gmat, preferred_element_type=F32) + jnp.dot(lo, gmat, preferred_element_type=F32))


@jax.custom_vjp
def _group_mean(x, gmat):
    return _group_mean_impl(x, gmat)


def _group_mean_fwd(x, gmat):
    return _group_mean_impl(x, gmat), gmat


def _group_mean_bwd(gmat, g):
    return _group_mean_impl(g, gmat), jnp.zeros_like(gmat)


_group_mean.defvjp(_group_mean_fwd, _group_mean_bwd)


def _group_matrix(width, group):
    r = jnp.arange(width) // group
    return jnp.where(r[:, None] == r[None, :], 1.0 / group, 0.0).astype(CDT)


def _make_rope(half):
    def partner(x):
        w = x.shape[1]
        lane = lax.broadcasted_iota(jnp.int32, x.shape, 1)
        first = (lane % (2 * half)) < half
        return jnp.where(first, pltpu.roll(x, w - half, axis=1), pltpu.roll(x, half, axis=1))

    def impl(x, cos_full, sin_signed):
        return x * cos_full + partner(x) * sin_signed

    @jax.custom_vjp
    def rope(x, cos_full, sin_signed):
        return impl(x, cos_full, sin_signed)

    def fwd(x, cos_full, sin_signed):
        return impl(x, cos_full, sin_signed), (cos_full, sin_signed)

    def bwd(res, g):
        cos_full, sin_signed = res
        return impl(g, cos_full, -sin_signed), jnp.zeros_like(cos_full), jnp.zeros_like(sin_signed)

    rope.defvjp(fwd, bwd)
    return rope


_rope32 = _make_rope(32)
_rope64 = _make_rope(64)


def _rms(x, w):
    return x * lax.rsqrt(jnp.mean(x * x, axis=-1, keepdims=True) + NORM_EPS) * w


def _col(v, lane_index):
    lane = lax.broadcasted_iota(jnp.int32, v.shape, 1)
    return jnp.sum(jnp.where(lane == lane_index, v, 0.0), axis=1, keepdims=True)


def _typed_spec(width, nb_ctx):
    return pl.BlockSpec((None, 1, width), lambda i: (jnp.where(i >= nb_ctx, 1, 0), 0, 0))


def rowwise_fwd(name, f, rows, typed, shared, outs, n_rows, nb_ctx, tm=ROW_TILE):
    tm = min(tm, n_rows)
    nin = len(rows) + len(typed) + len(shared)

    def body(*refs):
        res = f(*[r[...] for r in refs[:nin]])
        for o_ref, o in zip(refs[nin:], res):
            o_ref[...] = o.astype(o_ref.dtype)

    arrs, specs = [], []
    for it in rows:
        a, s = _row_in(it, tm)
        arrs.append(a)
        specs.append(s)
    for t in typed:
        arrs.append(t)
        specs.append(_typed_spec(t.shape[-1], nb_ctx))
    for s_ in shared:
        arrs.append(s_)
        specs.append(_const_spec(s_.shape))
    res = pl.pallas_call(
        body, name=name, grid=(n_rows // tm,), in_specs=specs,
        out_specs=[pl.BlockSpec((tm, w), lambda i: (i, 0)) for w, _ in outs],
        out_shape=[jax.ShapeDtypeStruct((n_rows, w), dt) for w, dt in outs],
        compiler_params=_cp(("parallel",)))(*arrs)
    return res


def rowwise_bwd(name, f, rows, typed, shared, cots, row_diff, shared_diff, drow_dtypes, n_rows, nb_ctx, tm=ROW_TILE):
    tm = min(tm, n_rows)
    cot_groups = [c_ if isinstance(c_, tuple) else (c_,) for c_ in cots]
    cots = [a for grp in cot_groups for a in grp]
    nr, nt, ns, nc = len(rows), len(typed), len(shared), len(cots)
    nin = nr + nt + ns
    d_rows = [k for k in range(nr) if row_diff[k]]
    d_sh = [k for k in range(ns) if shared_diff[k]]

    def body(*refs):
        rvals = [r[...] for r in refs[:nr]]
        tvals = [r[...] for r in refs[nr:nr + nt]]
        svals = [r[...] for r in refs[nr + nt:nin]]
        cparts = [r[...].astype(F32) for r in refs[nin:nin + nc]]
        cvals = []
        for grp in cot_groups:
            cvals.append(sum(cparts[1:len(grp)], cparts[0]))
            cparts = cparts[len(grp):]
        out_refs = refs[nin + nc:]

        def g(*dv):
            dv = list(dv)
            rv = list(rvals)
            for k in d_rows:
                rv[k] = dv.pop(0)
            tv = [dv.pop(0) for _ in range(nt)]
            sv = list(svals)
            for k in d_sh:
                sv[k] = dv.pop(0)
            return tuple(o.astype(F32) for o in f(*rv, *tv, *sv))

        prim = [rvals[k].astype(F32) for k in d_rows] + tvals + [svals[k] for k in d_sh]
        _, vjp = jax.vjp(g, *prim)
        grads = list(vjp(tuple(cvals)))
        i = pl.program_id(0)
        for ref in out_refs[:len(d_rows)]:
            ref[...] = grads.pop(0).astype(ref.dtype)
        first_typed = (i == 0) | (i == nb_ctx)
        for ref in out_refs[len(d_rows):len(d_rows) + nt]:
            gr = grads.pop(0)

            @pl.when(first_typed)
            def _(ref=ref, gr=gr):
                ref[...] = gr

            @pl.when(jnp.logical_not(first_typed))
            def _(ref=ref, gr=gr):
                ref[...] += gr
        for ref in out_refs[len(d_rows) + nt:]:
            gr = grads.pop(0)

            @pl.when(i == 0)
            def _(ref=ref, gr=gr):
                ref[...] = gr

            @pl.when(i != 0)
            def _(ref=ref, gr=gr):
                ref[...] += gr

    arrs, specs = [], []
    for it in list(rows):
        a, s = _row_in(it, tm)
        arrs.append(a)
        specs.append(s)
    for t in typed:
        arrs.append(t)
        specs.append(_typed_spec(t.shape[-1], nb_ctx))
    for s_ in shared:
        arrs.append(s_)
        specs.append(_const_spec(s_.shape))
    for c_ in cots:
        a, s = _row_in(c_, tm)
        arrs.append(a)
        specs.append(s)
    out_specs, out_shape = [], []
    for k, dt in zip(d_rows, drow_dtypes):
        w = _width(rows[k])
        out_specs.append(pl.BlockSpec((tm, w), lambda i: (i, 0)))
        out_shape.append(jax.ShapeDtypeStruct((n_rows, w), dt))
    for t in typed:
        out_specs.append(_typed_spec(t.shape[-1], nb_ctx))
        out_shape.append(jax.ShapeDtypeStruct(t.shape, F32))
    for k in d_sh:
        out_specs.append(_const_spec(shared[k].shape))
        out_shape.append(jax.ShapeDtypeStruct(shared[k].shape, F32))
    res = pl.pallas_call(body, name=name, grid=(n_rows // tm,), in_specs=specs, out_specs=out_specs,
                         out_shape=out_shape, compiler_params=_cp(("arbitrary",)))(*arrs)
    n1, n2 = len(d_rows), len(d_rows) + nt
    return list(res[:n1]), list(res[n1:n2]), list(res[n2:])


def _pick(n, prefs):
    for p in prefs:
        if n % p == 0:
            return p
    return n


def mm(name, a, b, out_dtype, transpose_b=False):
    n, k = b.shape if transpose_b else b.shape[::-1]
    m = (a.arr if isinstance(a, Cols) else a).shape[0]
    assert _width(a) == k
    tm = _pick(m, (MM_ROWS, 256))
    osz = jnp.dtype(out_dtype).itemsize
    tn = next(c for c in (2560, 2048, 1536, 1024, 512, 256, 128, n)
              if n % c == 0 and 2 * (tm * k * 2 + k * c * 2 + tm * c * osz) <= MM_VMEM_BUDGET or c == n)
    dims = _NT if transpose_b else (((1,), (0,)), ((), ()))

    def body(a_ref, b_ref, o_ref):
        o_ref[...] = lax.dot_general(a_ref[...], b_ref[...], dims, preferred_element_type=F32).astype(o_ref.dtype)

    a_arr, a_spec = _row_in(a, tm)
    a_spec = pl.BlockSpec(a_spec.block_shape, lambda j, i, f=a_spec.index_map: f(i))
    b_spec = pl.BlockSpec((tn, k), lambda j, i: (j, 0)) if transpose_b else pl.BlockSpec((k, tn), lambda j, i: (0, j))
    return pl.pallas_call(
        body, name=name, grid=(n // tn, m // tm), in_specs=[a_spec, b_spec],
        out_specs=pl.BlockSpec((tm, tn), lambda j, i: (i, j)),
        out_shape=jax.ShapeDtypeStruct((m, n), out_dtype),
        compiler_params=_cp(("parallel", "parallel")))(a_arr, b)


def mm_tn(name, a, b, out_dtype=F32, pieces=None):
    t = (a.arr if isinstance(a, Cols) else a).shape[0]
    k, n = _width(a), _width(b)
    tt = _pick(t, (MM_TN_ROWS, MM_ROWS, 256))
    k_unit = pieces[1] if pieces and pieces[0] == "rows" else k
    n_unit = pieces[1] if pieces and pieces[0] == "cols" else n
    tk = _pick(k_unit, (1024, 512, 256, 128))
    tn = _pick(n_unit, (1280, 1024, 512, 256, 128))
    n_t = t // tt

    def body(a_ref, b_ref, o_ref, acc):
        part = lax.dot_general(a_ref[...], b_ref[...], _TN, preferred_element_type=F32)
        ti = pl.program_id(2)

        @pl.when(ti == 0)
        def _():
            acc[...] = part

        @pl.when(ti != 0)
        def _():
            acc[...] += part

        @pl.when(ti == n_t - 1)
        def _():
            o_ref[...] = acc[...].astype(o_ref.dtype)

    def win(item, width):
        if isinstance(item, Cols):
            assert item.off % width == 0
            return item.arr, item.off // width
        return item, 0

    a_arr, a0 = win(a, tk)
    b_arr, b0 = win(b, tn)
    if pieces is None:
        out_spec = pl.BlockSpec((tk, tn), lambda ki, ni, ti: (ki, ni))
        out_shape = (k, n)
    elif pieces[0] == "cols":
        per = n_unit // tn
        out_spec = pl.BlockSpec((None, tk, tn), lambda ki, ni, ti: (ni // per, ki, ni % per))
        out_shape = (4, k, n_unit)
    else:
        per = k_unit // tk
        out_spec = pl.BlockSpec((None, tk, tn), lambda ki, ni, ti: (ki // per, ki % per, ni))
        out_shape = (4, k_unit, n)
    return pl.pallas_call(
        body, name=name, grid=(k // tk, n // tn, n_t),
        in_specs=[pl.BlockSpec((tt, tk), lambda ki, ni, ti: (ti, a0 + ki)),
                  pl.BlockSpec((tt, tn), lambda ki, ni, ti: (ti, b0 + ni))],
        out_specs=out_spec, out_shape=jax.ShapeDtypeStruct(out_shape, out_dtype),
        scratch_shapes=[pltpu.VMEM((tk, tn), F32)],
        compiler_params=_cp(("parallel", "parallel", "arbitrary")))(a_arr, b_arr)


def _heads_t(rows_blk):
    blk = rows_blk.astype(F32).T
    return jnp.concatenate([blk[hh * ATTN_HD:(hh + 1) * ATTN_HD, :] for hh in range(4)], axis=1)


def _heads_rows(t_blk):
    tq = t_blk.shape[1] // 4
    return jnp.concatenate([t_blk[:, hh * tq:(hh + 1) * tq] for hh in range(4)], axis=0).T


def attn_fwd(name, q, kk, vT_ones, m_ctx, tq):
    t, hd, hd_ext = kk.shape[1], ATTN_HD, vT_ones.shape[1]
    nq, r = t // tq, 4 * tq
    tk = _pick(t - m_ctx, (ATTN_TK_FWD, ATTN_TK))
    nqc, n_lat_tiles = m_ctx // tq, (t - m_ctx) // tk

    def body(q_ref, k_ref, vT_ref, o_ref, qT_ref, oT_ref, lse_ref):
        i = pl.program_id(1)
        q_t = _heads_t(q_ref[...]).astype(CDT)
        qT_ref[...] = q_t

        def tile(off, size, carry):
            mi, acc = carry
            sub = min(size, ATTN_TK)
            offs = [off + u * sub for u in range(size // sub)]
            sts = [jnp.dot(k_ref[pl.ds(o, sub), :], q_t, preferred_element_type=F32) for o in offs]
            for o, st in zip(offs, sts):
                mn = jnp.maximum(mi, jnp.max(st, axis=0, keepdims=True))
                pt = jnp.exp2(st - mn)
                acc = jnp.exp2(mi - mn) * acc + jnp.dot(vT_ref[:, pl.ds(o, sub)], pt.astype(CDT),
                                                        preferred_element_type=F32)
                mi = mn
            return mi, acc

        carry = tile(0, m_ctx, (jnp.full((1, r), -1e30, F32), jnp.zeros((hd_ext, r), F32)))
        mi, acc = lax.fori_loop(
            0, jnp.where(i < nqc, 0, n_lat_tiles),
            lambda j, cr: tile(pl.multiple_of(m_ctx + j * tk, ATTN_TK), tk, cr), carry)
        li = acc[hd:hd + 1]
        o_t = acc[:hd] / li
        oT_ref[...] = o_t.astype(oT_ref.dtype)
        o_ref[...] = _heads_rows(o_t).astype(o_ref.dtype)
        lse_ref[...] = mi + jnp.log2(li)

    blk_t = pl.BlockSpec((None, None, hd, r), lambda g, i: (g, i, 0, 0))
    rows = pl.BlockSpec((tq, 4 * hd), lambda g, i: (i, g))
    return pl.pallas_call(
        body, name=name, grid=(2, nq),
        in_specs=[rows, pl.BlockSpec((None, t, hd), lambda g, i: (g, 0, 0)),
                  pl.BlockSpec((None, hd_ext, t), lambda g, i: (g, 0, 0))],
        out_specs=[rows, blk_t, blk_t, pl.BlockSpec((None, None, 1, r), lambda g, i: (g, i, 0, 0))],
        out_shape=[jax.ShapeDtypeStruct((t, 8 * hd), CDT), jax.ShapeDtypeStruct((2, nq, hd, r), CDT),
                   jax.ShapeDtypeStruct((2, nq, hd, r), CDT), jax.ShapeDtypeStruct((2, nq, 1, r), F32)],
        compiler_params=_cp(("parallel", "arbitrary")))(q, kk, vT_ones)


def attn_bwd(name, qT, do, oT, lse, kk, kT, vv, m_ctx):
    _, nq, hd, r = qT.shape
    t = kk.shape[1]
    tq = r // 4
    tk = _pick(t - m_ctx, (ATTN_TK_BWD, ATTN_TK))
    nqc, n_lat_tiles = m_ctx // tq, (t - m_ctx) // tk

    def body(qT_ref, do_ref, oT_ref, lse_ref, k_ref, kT_ref, v_ref, dq_ref, dk_ref, dv_ref):
        i = pl.program_id(1)

        @pl.when(i == 0)
        def _():
            dk_ref[...] = jnp.zeros_like(dk_ref)
            dv_ref[...] = jnp.zeros_like(dv_ref)

        q_t = qT_ref[...]
        do_f = _heads_t(do_ref[...])
        do_t = do_f.astype(CDT)
        lse = lse_ref[...]
        delta = jnp.sum(do_f * oT_ref[...].astype(F32), axis=0, keepdims=True)

        def tile(off, size, dq):
            sub = min(size, ATTN_TK)
            offs = [off + u * sub for u in range(size // sub)]
            sts = [jnp.dot(k_ref[pl.ds(o, sub), :], q_t, preferred_element_type=F32) for o in offs]
            dpts = [jnp.dot(v_ref[pl.ds(o, sub), :], do_t, preferred_element_type=F32) for o in offs]
            for o, st, dpt in zip(offs, sts, dpts):
                pt = jnp.exp2(st - lse)
                dv_ref[pl.ds(o, sub), :] += lax.dot_general(pt.astype(CDT), do_t, _NT, preferred_element_type=F32)
                dst = (pt * (dpt - delta)).astype(CDT)
                dk_ref[pl.ds(o, sub), :] += lax.dot_general(dst, q_t, _NT, preferred_element_type=F32)
                dq = dq + jnp.dot(kT_ref[:, pl.ds(o, sub)], dst, preferred_element_type=F32)
            return dq

        dq = tile(0, m_ctx, jnp.zeros((hd, r), F32))
        dq = lax.fori_loop(0, jnp.where(i < nqc, 0, n_lat_tiles),
                           lambda j, acc: tile(pl.multiple_of(m_ctx + j * tk, ATTN_TK), tk, acc), dq)
        dq_ref[...] = _heads_rows(dq * LN2)

    blk_t = pl.BlockSpec((None, None, hd, r), lambda g, i: (g, i, 0, 0))
    row = pl.BlockSpec((None, None, 1, r), lambda g, i: (g, i, 0, 0))
    kv = pl.BlockSpec((None, t, hd), lambda g, i: (g, 0, 0))
    rows = pl.BlockSpec((tq, 4 * hd), lambda g, i: (i, g))
    return pl.pallas_call(
        body, name=name, grid=(2, nq),
        in_specs=[blk_t, rows, blk_t, row, kv, pl.BlockSpec((None, hd, t), lambda g, i: (g, 0, 0)), kv],
        out_specs=[rows, kv, kv],
        out_shape=[jax.ShapeDtypeStruct((t, 8 * hd), F32), jax.ShapeDtypeStruct(kk.shape, F32),
                   jax.ShapeDtypeStruct(kk.shape, F32)],
        compiler_params=_cp(("parallel", "arbitrary")))(qT, do, oT, lse, kk, kT, vv)


def _split_kv(a):
    return a.reshape(a.shape[0], 2, ATTN_HD).transpose(1, 0, 2)


def _merge_kv(a):
    return a.transpose(1, 0, 2).reshape(a.shape[1], 2 * ATTN_HD)


def _chunk_order(rev, ncc, nct):
    if not rev:
        return lambda s: s
    return lambda s: jnp.where(s < ncc, ncc - 1 - s, nct - 1 - (s - ncc))


def scan_fwd(name, make_fn, rows, shared, n_state, y_width, n_rows, m_ctx):
    nct, ncc = n_rows // CHUNK, m_ctx // CHUNK
    orders = [_chunk_order(rev, ncc, nct) for rev in (False, True)]
    fns = [make_fn(0), make_fn(1)]
    nr, ns = len(rows), len(shared)

    def body(*refs):
        svals = [r[...] for r in refs[2 * nr:2 * nr + ns]]
        y_refs, sin_refs, st = refs[2 * nr + ns:2 * nr + ns + 2], refs[2 * nr + ns + 2:2 * nr + ns + 4], refs[-1]

        @pl.when(pl.program_id(0) == 0)
        def _():
            st[...] = jnp.zeros_like(st)

        for d in range(2):
            rvals = [r[...] for r in refs[d * nr:(d + 1) * nr]]
            prev = [st[d, k] for k in range(n_state)]
            sin_refs[d][...] = st[d]
            y, new = fns[d](rvals, svals, prev)
            y_refs[d][...] = y
            for k in range(n_state):
                st[d, k] = new[k]

    arrs, specs = [], []
    for order in orders:
        for it in rows:
            a, s = _row_in(it, CHUNK, order)
            arrs.append(a)
            specs.append(s)
    for s_ in shared:
        arrs.append(s_)
        specs.append(_const_spec(s_.shape))
    return pl.pallas_call(
        body, name=name, grid=(nct,), in_specs=specs,
        out_specs=[pl.BlockSpec((CHUNK, y_width), lambda s, o=o: (o(s), 0)) for o in orders]
        + [pl.BlockSpec((None, n_state, LANES, LANES), lambda s, o=o: (o(s), 0, 0, 0)) for o in orders],
        out_shape=[jax.ShapeDtypeStruct((n_rows, y_width), F32)] * 2
        + [jax.ShapeDtypeStruct((nct, n_state, LANES, LANES), F32)] * 2,
        scratch_shapes=[pltpu.VMEM((2, n_state, LANES, LANES), F32)],
        compiler_params=_cp(("arbitrary",)))(*arrs)


def scan_bwd(name, make_fn, rows, shared, states_in, dy, post, outs, n_state, n_rows, m_ctx, dirs=(0, 1)):
    nct, ncc = n_rows // CHUNK, m_ctx // CHUNK
    orders = [(lambda r, f=_chunk_order(d == 1, ncc, nct): f(nct - 1 - r)) for d in dirs]
    fns = [make_fn(d) for d in dirs]
    nd = len(dirs)
    nr, ns, no = len(rows), len(shared), len(outs)
    n_in = nd * nr + ns

    def body(*refs):
        svals = [r[...] for r in refs[nd * nr:n_in]]
        sin_refs, dy_refs = refs[n_in:n_in + nd], refs[n_in + nd:n_in + 2 * nd]
        out_refs = refs[n_in + 2 * nd:n_in + 2 * nd + nd * no]
        dsh_refs = refs[n_in + 2 * nd + nd * no:-1]
        dst = refs[-1]
        r = pl.program_id(0)

        @pl.when(r == 0)
        def _():
            dst[...] = jnp.zeros_like(dst)

        d_shared = None
        for d in range(nd):
            rvals = [x[...] for x in refs[d * nr:(d + 1) * nr]]
            prev = [sin_refs[d][k] for k in range(n_state)]
            _, vjp = jax.vjp(fns[d], rvals, svals, prev)
            d_rows, d_sh, d_prev = vjp((dy_refs[d][...], [dst[d, k] for k in range(n_state)]))
            for ref, val in zip(out_refs[d * no:(d + 1) * no], post(d_rows)):
                ref[...] = val.astype(ref.dtype)
            d_shared = d_sh if d_shared is None else [a + b for a, b in zip(d_shared, d_sh)]
            for k in range(n_state):
                dst[d, k] = d_prev[k]
        for ref, gr in zip(dsh_refs, d_shared):
            @pl.when(r == 0)
            def _(ref=ref, gr=gr):
                ref[...] = gr

            @pl.when(r != 0)
            def _(ref=ref, gr=gr):
                ref[...] += gr

    arrs, specs = [], []
    for order in orders:
        for it in rows:
            a, s = _row_in(it, CHUNK, order)
            arrs.append(a)
            specs.append(s)
    for s_ in shared:
        arrs.append(s_)
        specs.append(_const_spec(s_.shape))
    for sin, order in zip(states_in, orders):
        arrs.append(sin)
        specs.append(pl.BlockSpec((None, n_state, LANES, LANES), lambda r, o=order: (o(r), 0, 0, 0)))
    for order in orders:
        a, s = _row_in(dy, CHUNK, order)
        arrs.append(a)
        specs.append(s)
    out_specs = [pl.BlockSpec((CHUNK, w), lambda r, o=o: (o(r), 0)) for o in orders for w, _ in outs]
    out_shape = [jax.ShapeDtypeStruct((n_rows, w), dt) for _ in orders for w, dt in outs]
    for s_ in shared:
        out_specs.append(_const_spec(s_.shape))
        out_shape.append(jax.ShapeDtypeStruct(s_.shape, F32))
    res = pl.pallas_call(body, name=name, grid=(nct,), in_specs=specs, out_specs=out_specs, out_shape=out_shape,
                         scratch_shapes=[pltpu.VMEM((nd, n_state, LANES, LANES), F32)],
                         compiler_params=_cp(("arbitrary",)))(*arrs)
    return [list(res[d * no:(d + 1) * no]) for d in range(nd)] + [list(res[nd * no:])]


def _make_ssd_chunk(direction):
    rev = direction == 1
    base = 8 * direction

    def fn(rows, shared, prev):
        xs, bms, cms, dtraw = rows[0:4], rows[4:6], rows[6:8], rows[8]
        dt_bias, a_log = shared
        ln = dtraw.shape[0]
        dt_all = _softplus(dtraw + dt_bias)
        a_all = dt_all * (-jnp.exp(a_log))
        r_i = lax.broadcasted_iota(jnp.int32, (ln, ln), 0)
        c_i = lax.broadcasted_iota(jnp.int32, (ln, ln), 1)
        tri = (r_i <= c_i) if rev else (r_i >= c_i)
        a_cum_all = jnp.dot(tri.astype(F32), a_all, precision=lax.Precision.HIGHEST, preferred_element_type=F32)
        a_tot_all = jnp.sum(a_all, axis=0, keepdims=True)
        first = lax.broadcasted_iota(jnp.int32, (ln, LANES), 1) < SSD_HD
        first_row = lax.broadcasted_iota(jnp.int32, (LANES, 1), 0) < SSD_HD

        def lmat(acol):
            a_b = jnp.broadcast_to(acol, (ln, ln))
            seg = a_b - a_b.T
            return jnp.where(tri, jnp.exp(jnp.where(tri, seg, 0.0)), 0.0)

        ys, new = [], []
        for g in range(2):
            bm, cm = bms[g], cms[g]
            cb = _mxu(cm, bm, _NT)
            for jj in range(2):
                pr = 2 * g + jj
                h0, h1 = base + 2 * pr, base + 2 * pr + 1
                ac0, ac1 = _col(a_cum_all, h0), _col(a_cum_all, h1)
                at0, at1 = _col(a_tot_all, h0), _col(a_tot_all, h1)
                dt_pair = jnp.where(first, _col(dt_all, h0), _col(dt_all, h1))
                acum_pair = jnp.where(first, ac0, ac1)
                atot_pair = jnp.where(first[0:1], at0, at1)
                xd = xs[pr] * dt_pair
                st = _mxu(xd * jnp.exp(atot_pair - acum_pair), bm, _TN)
                new.append(prev[pr] * jnp.where(first_row, jnp.exp(at0), jnp.exp(at1)) + st)
                y0 = _mxu(cb * lmat(ac0), xd)
                y1 = _mxu(cb * lmat(ac1), xd)
                y_off = _mxu(cm, prev[pr], _NT) * jnp.exp(acum_pair)
                ys.append(jnp.where(first, y0, y1) + y_off)
        return jnp.concatenate(ys, axis=1), new

    return fn


def _make_ret_chunk(direction):
    rev = direction == 1
    base = 4 * direction

    def fn(rows, shared, prev):
        qs, ks, vs = rows[0:4], rows[4:8], rows[8:12]
        lg_all = -jnp.exp(shared[0])
        ln = qs[0].shape[0]
        pos = lax.broadcasted_iota(jnp.int32, (ln, 1), 0).astype(F32)
        r_i = lax.broadcasted_iota(jnp.int32, (ln, ln), 0)
        c_i = lax.broadcasted_iota(jnp.int32, (ln, ln), 1)
        diff = ((c_i - r_i) if rev else (r_i - c_i))
        mask = diff >= 0
        dpos = jnp.maximum(diff, 0).astype(F32)
        k_pow = pos if rev else (ln - 1.0 - pos)
        q_pow = (ln - pos) if rev else (pos + 1.0)
        ys, new = [], []
        for h in range(RET_HEADS):
            lg = _col(lg_all, base + h)
            dmat = jnp.where(mask, jnp.exp(dpos * lg), 0.0)
            st = _mxu(ks[h] * jnp.exp(k_pow * lg), vs[h], _TN)
            new.append(prev[h] * jnp.exp(ln * lg) + st)
            s = _mxu(qs[h], ks[h], _NT) * dmat
            ys.append(_mxu(s, vs[h]) + _mxu(qs[h], prev[h]) * jnp.exp(q_pow * lg))
        return jnp.concatenate(ys, axis=1), new

    return fn


def _conv_pre(x, w, b, t_idx, n_rows, m_ctx):
    is_start = (t_idx == 0) | (t_idx == m_ctx)
    is_end = (t_idx == m_ctx - 1) | (t_idx == n_rows - 1)
    xp = jnp.where(is_start, 0.0, pltpu.roll(x, 1, axis=0))
    xn = jnp.where(is_end, 0.0, pltpu.roll(x, n_rows - 1, axis=0))
    return w[0:1] * xp + w[1:2] * x + w[2:3] * xn + b, xp, xn, is_start, is_end


def conv_fwd(x, conv_w, conv_b, m_ctx):
    n_rows, width = x.arr.shape[0], x.width
    c0 = x.off // LANES

    def body(x_ref, w_ref, b_ref, o_ref):
        t_idx = lax.broadcasted_iota(jnp.int32, (n_rows, 1), 0)
        pre = _conv_pre(x_ref[...], w_ref[...], b_ref[...], t_idx, n_rows, m_ctx)[0]
        o_ref[...] = pre * jax.nn.sigmoid(pre)

    return pl.pallas_call(
        body, name="conv_fwd", grid=(width // LANES,),
        in_specs=[pl.BlockSpec((n_rows, LANES), lambda c: (0, c0 + c)),
                  pl.BlockSpec((3, LANES), lambda c: (0, c)), pl.BlockSpec((1, LANES), lambda c: (0, c))],
        out_specs=pl.BlockSpec((n_rows, LANES), lambda c: (0, c)),
        out_shape=jax.ShapeDtypeStruct((n_rows, width), F32),
        compiler_params=_cp(("parallel",)))(x.arr, conv_w, conv_b)


def conv_bwd(x, conv_w, conv_b, dy_a, dy_b, dxs_extra, m_ctx):
    n_rows, width = x.arr.shape[0], x.width
    c0 = x.off // LANES
    n_extra = dxs_extra.shape[1] // LANES

    def body(x_ref, w_ref, b_ref, dya_ref, dyb_ref, ex_ref, dx_ref, dw_ref, db_ref):
        c = pl.program_id(0)
        t_idx = lax.broadcasted_iota(jnp.int32, (n_rows, 1), 0)
        w = w_ref[...]
        pre, xp, xn, is_start, is_end = _conv_pre(x_ref[...], w, b_ref[...], t_idx, n_rows, m_ctx)
        sg = jax.nn.sigmoid(pre)
        dyv = dya_ref[...] + dyb_ref[...] + jnp.where(c < n_extra, ex_ref[...], 0.0)
        dpre = dyv * (sg * (1.0 + pre * (1.0 - sg)))
        d_next = jnp.where(is_end, 0.0, pltpu.roll(dpre, n_rows - 1, axis=0))
        d_prev = jnp.where(is_start, 0.0, pltpu.roll(dpre, 1, axis=0))
        dx_ref[...] = (w[1:2] * dpre + w[0:1] * d_next + w[2:3] * d_prev).astype(dx_ref.dtype)
        dw_ref[...] = jnp.concatenate([jnp.sum(dpre * xp, axis=0, keepdims=True),
                                       jnp.sum(dpre * x_ref[...], axis=0, keepdims=True),
                                       jnp.sum(dpre * xn, axis=0, keepdims=True)], axis=0)
        db_ref[...] = jnp.sum(dpre, axis=0, keepdims=True)

    return pl.pallas_call(
        body, name="conv_bwd", grid=(width // LANES,),
        in_specs=[pl.BlockSpec((n_rows, LANES), lambda c: (0, c0 + c)),
                  pl.BlockSpec((3, LANES), lambda c: (0, c)), pl.BlockSpec((1, LANES), lambda c: (0, c)),
                  pl.BlockSpec((n_rows, LANES), lambda c: (0, c)), pl.BlockSpec((n_rows, LANES), lambda c: (0, c)),
                  pl.BlockSpec((n_rows, LANES), lambda c: (0, jnp.minimum(c, n_extra - 1)))],
        out_specs=[pl.BlockSpec((n_rows, LANES), lambda c: (0, c)),
                   pl.BlockSpec((3, LANES), lambda c: (0, c)), pl.BlockSpec((1, LANES), lambda c: (0, c))],
        out_shape=[jax.ShapeDtypeStruct((n_rows, width), CDT), jax.ShapeDtypeStruct((3, width), F32),
                   jax.ShapeDtypeStruct((1, width), F32)],
        compiler_params=_cp(("parallel",)))(x.arr, conv_w, conv_b, dy_a, dy_b, dxs_extra)


def loss_head(h, target, final_w, m_ctx):
    n_rows, d = h.shape
    tm = min(ROW_TILE, n_rows)
    nb_ctx = m_ctx // tm

    def f(hb, w, tgt):
        err = _rms(hb, w) - tgt
        return 0.5 * jnp.sum(jnp.mean(err * err, axis=-1))

    def body(h_ref, t_ref, w_ref, loss_ref, dh_ref, dw_ref):
        i = pl.program_id(0)

        @pl.when(i < nb_ctx)
        def _():
            dh_ref[...] = jnp.zeros_like(dh_ref)

        @pl.when(i == 0)
        def _():
            loss_ref[...] = jnp.zeros_like(loss_ref)
            dw_ref[...] = jnp.zeros_like(dw_ref)

        @pl.when(i >= nb_ctx)
        def _():
            val, vjp = jax.vjp(lambda hb, w: f(hb, w, t_ref[...]), h_ref[...], w_ref[...])
            dh, dw = vjp(jnp.ones((), F32))
            dh_ref[...] = dh
            dw_ref[...] += dw
            loss_ref[...] += jnp.broadcast_to(val, loss_ref.shape)

    return pl.pallas_call(
        body, name="loss_head", grid=(n_rows // tm,),
        in_specs=[pl.BlockSpec((tm, d), lambda i: (i, 0)),
                  pl.BlockSpec((tm, d), lambda i: (jnp.maximum(i - nb_ctx, 0), 0)), _const_spec((1, d))],
        out_specs=[_const_spec((1, LANES)), pl.BlockSpec((tm, d), lambda i: (i, 0)), _const_spec((1, d))],
        out_shape=[jax.ShapeDtypeStruct((1, LANES), F32), jax.ShapeDtypeStruct((n_rows, d), F32),
                   jax.ShapeDtypeStruct((1, d), F32)],
        compiler_params=_cp(("arbitrary",)))(h, target, final_w)


def adamw(name, w, m, v, g_parts):
    lead, rows, cols = w.shape
    tr = _pick(rows, (256, 128, 64, 32, 16, 8))
    npart = len(g_parts)
    c1 = 1.0 - ADAM_B1 ** ADAM_STEP
    c2 = 1.0 - ADAM_B2 ** ADAM_STEP

    def body(*refs):
        w_ref, m_ref, v_ref = refs[:3]
        g = refs[3][...].astype(F32)
        for r in refs[4:3 + npart]:
            g = g + r[...].astype(F32)
        g_ref, d_ref, nm_ref, nv_ref = refs[3 + npart:]
        nm = ADAM_B1 * m_ref[...] + (1.0 - ADAM_B1) * g
        nv = ADAM_B2 * v_ref[...] + (1.0 - ADAM_B2) * (g * g)
        g_ref[...] = g
        nm_ref[...] = nm
        nv_ref[...] = nv
        d_ref[...] = -ADAM_LR * ((nm / c1) / (jnp.sqrt(nv / c2) + ADAM_EPS) + ADAM_WD * w_ref[...])

    spec = pl.BlockSpec((None, tr, cols), lambda l, i: (l, i, 0))
    return pl.pallas_call(
        body, name=name, grid=(lead, rows // tr), in_specs=[spec] * (3 + npart), out_specs=[spec] * 4,
        out_shape=[jax.ShapeDtypeStruct(w.shape, F32)] * 4,
        compiler_params=_cp(("parallel", "parallel")))(w, m, v, *g_parts)


MESH = pl.DeviceIdType.MESH
_HBM = pl.BlockSpec(memory_space=pl.ANY)


def _chip_peers():
    x, y, c = lax.axis_index("x"), lax.axis_index("y"), lax.axis_index("c")
    return x, y, c, [(1 - x, y), (x, 1 - y), (1 - x, 1 - y)]


def _window(ref, kind, chip, rows, cols):
    if kind == "cols":
        return ref.at[:, pl.ds(pl.multiple_of(chip * cols, LANES), cols)]
    if kind == "rows":
        return ref.at[pl.ds(pl.multiple_of(chip * rows, 8), rows), :]
    return ref.at[chip]


def _gathered_shape(kind, rows, cols):
    return {"cols": (rows, 4 * cols), "rows": (4 * rows, cols), "slices": (4, rows, cols)}[kind]


def gather_layers(name, shards, kinds):
    n = len(shards)

    def body(*refs):
        x_refs, o_refs = refs[:n], refs[n:2 * n]
        send_sems, recv_sems, local_sems = refs[2 * n:]
        x, y, c, peers = _chip_peers()
        me = 2 * x + y
        started = []
        for a in range(n):
            _, rows, cols = shards[a].shape
            src = x_refs[a].at[c]
            mine = pltpu.make_async_copy(src, _window(o_refs[a], kinds[a], me, rows, cols), local_sems.at[a])
            mine.start()
            started.append(mine.wait)
            for k, (px, py) in enumerate(peers):
                cp = pltpu.make_async_remote_copy(
                    src_ref=src, dst_ref=_window(o_refs[a], kinds[a], me, rows, cols), send_sem=send_sems.at[3 * a + k],
                    recv_sem=recv_sems.at[3 * a + k], device_id=(px, py, c), device_id_type=MESH)
                cp.start()
                started.append(cp.wait_send)
        for a in range(n):
            _, rows, cols = shards[a].shape
            for k, (px, py) in enumerate(peers):
                pltpu.make_async_remote_copy(
                    src_ref=x_refs[a].at[c], dst_ref=_window(o_refs[a], kinds[a], 2 * px + py, rows, cols),
                    send_sem=send_sems.at[3 * a + k], recv_sem=recv_sems.at[3 * a + k], device_id=(px, py, c),
                    device_id_type=MESH).wait_recv()
        for wait in started:
            wait()

    return pl.pallas_call(
        body, name=name, in_specs=[_HBM] * n, out_specs=[_HBM] * n,
        out_shape=[jax.ShapeDtypeStruct(_gathered_shape(kinds[a], *shards[a].shape[1:]), shards[a].dtype)
                   for a in range(n)],
        scratch_shapes=[pltpu.SemaphoreType.DMA((3 * n,)), pltpu.SemaphoreType.DMA((3 * n,)),
                        pltpu.SemaphoreType.DMA((n,))],
        )(*shards)


def scatter_pieces(name, pieces):
    n = len(pieces)

    def body(*refs):
        p_refs, o_refs = refs[:n], refs[n:2 * n]
        send_sems, recv_sems, local_sems = refs[2 * n:]
        x, y, c, peers = _chip_peers()
        me = 2 * x + y
        started = []
        for a in range(n):
            mine = pltpu.make_async_copy(p_refs[a].at[me], o_refs[a].at[me], local_sems.at[a])
            mine.start()
            started.append(mine.wait)
            for k, (px, py) in enumerate(peers):
                cp = pltpu.make_async_remote_copy(
                    src_ref=p_refs[a].at[2 * px + py], dst_ref=o_refs[a].at[me], send_sem=send_sems.at[3 * a + k],
                    recv_sem=recv_sems.at[3 * a + k], device_id=(px, py, c), device_id_type=MESH)
                cp.start()
                started.append(cp.wait_send)
        for a in range(n):
            for k, (px, py) in enumerate(peers):
                pltpu.make_async_remote_copy(
                    src_ref=p_refs[a].at[me], dst_ref=o_refs[a].at[2 * px + py], send_sem=send_sems.at[3 * a + k],
                    recv_sem=recv_sems.at[3 * a + k], device_id=(px, py, c), device_id_type=MESH).wait_recv()
        for wait in started:
            wait()

    return pl.pallas_call(
        body, name=name, in_specs=[_HBM] * n, out_specs=[_HBM] * n,
        out_shape=[jax.ShapeDtypeStruct(p.shape, p.dtype) for p in pieces],
        scratch_shapes=[pltpu.SemaphoreType.DMA((3 * n,)), pltpu.SemaphoreType.DMA((3 * n,)),
                        pltpu.SemaphoreType.DMA((n,))],
        )(*pieces)


def _pair_step(n_steps, x_ref, land, send_sems, recv_sems, credits, consume):
    x, y, c = lax.axis_index("x"), lax.axis_index("y"), lax.axis_index("c")
    sib = (x, y, 1 - c)
    i = pl.program_id(0)
    slot = i % 2

    @pl.when(i >= 2)
    def _():
        pl.semaphore_wait(credits.at[slot], 1)

    cp = pltpu.make_async_remote_copy(src_ref=x_ref, dst_ref=land.at[slot], send_sem=send_sems.at[slot],
                                      recv_sem=recv_sems.at[slot], device_id=sib, device_id_type=MESH)
    cp.start()
    cp.wait_recv()
    consume(land[slot])

    @pl.when(i < n_steps - 2)
    def _():
        pl.semaphore_signal(credits.at[slot], inc=1, device_id=sib, device_id_type=MESH)

    cp.wait_send()


def _pair_call(name, body, n_steps, in_specs, out_spec, out_shape, blk_shape, dtype, operands, extra_scratch=()):
    grid_spec = pltpu.PrefetchScalarGridSpec(
        num_scalar_prefetch=1, grid=(n_steps,), in_specs=in_specs, out_specs=out_spec,
        scratch_shapes=[pltpu.VMEM((2,) + blk_shape, dtype), pltpu.SemaphoreType.DMA((2,)),
                        pltpu.SemaphoreType.DMA((2,)), pltpu.SemaphoreType.REGULAR((2,)), *extra_scratch])
    return pl.pallas_call(body, name=name, grid_spec=grid_spec, out_shape=out_shape,
                          compiler_params=_cp(("arbitrary",)))(*operands)


def _place():
    return jnp.stack([lax.axis_index("x"), lax.axis_index("y"), lax.axis_index("c")]).astype(jnp.int32)


def _pair_rows(rows, row_bytes):
    for cand in (4096, 2048, 1024, 768, 512, 384, 256, 192, 128, 96, 64, 48, 32, 16):
        if rows % cand == 0 and cand * row_bytes <= PAIR_BLOCK_BYTES:
            return cand
    return _pick(rows, (16, 8))


def exchange_both(name, mine):
    rows, cols = mine.shape
    tr = _pair_rows(rows, cols * mine.dtype.itemsize)
    n_steps = rows // tr

    def body(s_ref, x_ref, o_ref, land, send_sems, recv_sems, credits):
        c = lax.axis_index("c")
        o_ref[c] = x_ref[...]

        def consume(v):
            o_ref[1 - c] = v
        _pair_step(n_steps, x_ref, land, send_sems, recv_sems, credits, consume)

    return _pair_call(name, body, n_steps, [pl.BlockSpec((tr, cols), lambda i, s: (i, 0))],
                      pl.BlockSpec((2, tr, cols), lambda i, s: (0, i, 0)),
                      jax.ShapeDtypeStruct((2, rows, cols), mine.dtype), (tr, cols), mine.dtype, (_place(), mine))


def exchange_add(name, both_layers):
    _, rows, cols = both_layers.shape
    tr = _pair_rows(rows, cols * both_layers.dtype.itemsize)
    nb = rows // tr
    flat = both_layers.reshape(2 * rows, cols)

    def body(s_ref, x_ref, m_ref, o_ref, land, send_sems, recv_sems, credits):
        def consume(v):
            o_ref[...] = (m_ref[...].astype(F32) + v.astype(F32)).astype(o_ref.dtype)
        _pair_step(nb, x_ref, land, send_sems, recv_sems, credits, consume)

    return _pair_call(name, body, nb,
                      [pl.BlockSpec((tr, cols), lambda i, s: ((1 - s[2]) * nb + i, 0)),
                       pl.BlockSpec((tr, cols), lambda i, s: (s[2] * nb + i, 0))],
                      pl.BlockSpec((tr, cols), lambda i, s: (i, 0)), jax.ShapeDtypeStruct((rows, cols), CDT),
                      (tr, cols), flat.dtype, (_place(), flat, flat))


def sum_exchange(name, parts):
    npart, rows, cols = parts.shape
    tr = _pair_rows(rows, cols * 4)
    n_steps = rows // tr

    def body(s_ref, x_ref, o_ref, land, send_sems, recv_sems, credits, mine):
        c = lax.axis_index("c")
        acc = x_ref[0].astype(F32)
        for k in range(1, npart):
            acc = acc + x_ref[k].astype(F32)
        mine[...] = acc
        o_ref[c] = acc

        def consume(v):
            o_ref[1 - c] = v
        _pair_step(n_steps, mine, land, send_sems, recv_sems, credits, consume)

    return _pair_call(name, body, n_steps, [pl.BlockSpec((npart, tr, cols), lambda i, s: (0, i, 0))],
                      pl.BlockSpec((2, tr, cols), lambda i, s: (0, i, 0)),
                      jax.ShapeDtypeStruct((2, rows, cols), F32), (tr, cols), F32, (_place(), parts),
                      extra_scratch=(pltpu.VMEM((tr, cols), F32),))


def allreduce_small(name, buf):
    rows = buf.shape[0]

    def body(x_ref, out_ref, gath, send_sems, recv_sems):
        x, y, c = lax.axis_index("x"), lax.axis_index("y"), lax.axis_index("c")
        me = 4 * x + 2 * y + c
        masks = [(k >> 2 & 1, k >> 1 & 1, k & 1) for k in range(1, 8)]

        def flip(v, bit):
            return 1 - v if bit else v

        sends = []
        for k, (bx, by, bc) in enumerate(masks):
            cp = pltpu.make_async_remote_copy(src_ref=x_ref, dst_ref=gath.at[me], send_sem=send_sems.at[k],
                                              recv_sem=recv_sems.at[k],
                                              device_id=(flip(x, bx), flip(y, by), flip(c, bc)), device_id_type=MESH)
            cp.start()
            sends.append(cp)
        gath[me] = x_ref[...]
        for k, (bx, by, bc) in enumerate(masks):
            px, py, pc = flip(x, bx), flip(y, by), flip(c, bc)
            pltpu.make_async_remote_copy(src_ref=x_ref, dst_ref=gath.at[4 * px + 2 * py + pc],
                                         send_sem=send_sems.at[k], recv_sem=recv_sems.at[k],
                                         device_id=(px, py, pc), device_id_type=MESH).wait_recv()
        for cp in sends:
            cp.wait_send()
        acc = gath[0]
        for d in range(1, 8):
            acc = acc + gath[d]
        out_ref[...] = acc

    return pl.pallas_call(
        body, name=name, in_specs=[pl.BlockSpec(memory_space=pltpu.VMEM)],
        out_specs=pl.BlockSpec(memory_space=pltpu.VMEM), out_shape=jax.ShapeDtypeStruct(buf.shape, F32),
        scratch_shapes=[pltpu.VMEM((8, rows, LANES), F32), pltpu.SemaphoreType.DMA((7,)),
                        pltpu.SemaphoreType.DMA((7,))],
        )(buf)


def _pack_flat(arrs, dtype, width, row_mult=8):
    flat = jnp.concatenate([a.reshape(-1).astype(dtype) for a in arrs])
    pad = (-flat.shape[0]) % (row_mult * width)
    if pad:
        flat = jnp.concatenate([flat, jnp.zeros((pad,), dtype)])
    return flat.reshape(-1, width)


def _unpack_flat(buf, shapes):
    flat = buf.reshape(-1)
    out, off = [], 0
    for s in shapes:
        n = math.prod(s)
        out.append(flat[off:off + n].reshape(s))
        off += n
    return out


def _in_to_padded(w):
    parts = []
    for name in IN_NEW_ORDER:
        _, width, o_off, o_w = IN_LAYOUT[name]
        parts.append(w[..., o_off:o_off + o_w])
        if o_w < width:
            parts.append(jnp.zeros(w.shape[:-1] + (width - o_w,), w.dtype))
    used = sum(IN_LAYOUT[n][1] for n in IN_NEW_ORDER)
    parts.append(jnp.zeros(w.shape[:-1] + (IN_PAD - used,), w.dtype))
    return jnp.concatenate(parts, axis=-1)


def _in_from_padded(g):
    parts = []
    for name in IN_ORIG_ORDER:
        off, _, _, o_w = IN_LAYOUT[name]
        parts.append(g[..., off:off + o_w])
    return jnp.concatenate(parts, axis=-1)


def _pcol(p, name):
    off, width, _, _ = IN_LAYOUT[name]
    return Cols(p, off, width)


def _lane_pad(v, width=LANES):
    v = v.reshape(-1)
    return jnp.concatenate([v, jnp.zeros((width - v.shape[0],), v.dtype)]).reshape(1, width)


def _f_norm_mod(h, sh, sc, w):
    return (_rms(h, w) * (1.0 + sc) + sh,)


def _f_norm_mod_thru(h, sh, sc, w):
    return h, _rms(h, w) * (1.0 + sc) + sh


def _f_attn_prep(qraw, kraw, vraw, cos2, sin2, qw, kw, gq, gk):
    q = qraw * lax.rsqrt(_group_mean(qraw * qraw, gq) + NORM_EPS) * qw
    q = _rope32(q, jnp.tile(cos2, (1, 4)), jnp.tile(sin2, (1, 4))) * (ATTN_HD ** -0.5 * LOG2E)
    k = kraw * lax.rsqrt(_group_mean(kraw * kraw, gk) + NORM_EPS) * kw
    return q, _rope32(k, cos2, sin2), vraw


def _f_ssd_finish(yf, yb, xs, z, d_exp, nw):
    y = (yf + yb + d_exp * xs) * (z * jax.nn.sigmoid(z))
    return (_rms(y, nw),)


def _f_ret_prep(rq, rk, cos1, sin1):
    cos_full, sin_signed = jnp.tile(cos1, (1, 4)), jnp.tile(sin1, (1, 4))
    return _rope64(rq, cos_full, sin_signed), _rope64(rk, cos_full, sin_signed) * (RET_DK ** -0.5)


def _f_ret_finish(yf, yb, g, gw):
    y = yf + yb
    outs = []
    for h in range(RET_HEADS):
        yh = y[:, h * RET_DK:(h + 1) * RET_DK]
        yc = yh - jnp.mean(yh, axis=-1, keepdims=True)
        outs.append(yc * lax.rsqrt(jnp.mean(yc * yc, axis=-1, keepdims=True) + NORM_EPS))
    return (jnp.concatenate(outs, axis=1) * gw * (g * jax.nn.sigmoid(g)),)


def _f_merge(p0, p1, p2, g0, g1, g2):
    return (jax.nn.sigmoid(g0) * p0 + jax.nn.sigmoid(g1) * p1 + jax.nn.sigmoid(g2) * p2,)


def _f_mid(h, mix, g1, sh2, sc2, w2):
    h_mid = h + g1 * mix
    return h_mid, _rms(h_mid, w2) * (1.0 + sc2) + sh2


def _f_sqrelu(a):
    r = jnp.maximum(a, 0.0)
    return (r * r,)


def _f_residual(h_mid, o, g2):
    return (h_mid + g2 * o,)


def _f_silu(x):
    return (x * jax.nn.sigmoid(x),)


def _f_bias(x, b):
    return (x + b,)


def _ssd_rows(xbc, p):
    rows = [Cols(xbc, LANES * k, LANES) for k in range(4)]
    rows += [Cols(xbc, 512 + LANES * g, LANES) for g in range(2)]
    rows += [Cols(xbc, 768 + LANES * g, LANES) for g in range(2)]
    return rows + [_pcol(p, "dt")]


def _ret_rows(rq, rk, p):
    off_v = IN_LAYOUT["rv"][0]
    return ([Cols(rq, LANES * h, LANES) for h in range(4)] + [Cols(rk, LANES * h, LANES) for h in range(4)]
            + [Cols(p, off_v + LANES * h, LANES) for h in range(4)])


def layer_fwd(li, h, mod, lw, tabs, m_ctx):
    t = h.shape[0]
    nb = m_ctx // min(ROW_TILE, t)
    sh1, sc1, g1, sh2, sc2, g2 = mod
    nm = lambda s: f"l{li}_{s}"
    sv = {}
    (u,) = rowwise_fwd(nm("norm1"), _f_norm_mod, [h], [sh1, sc1], [lw["norm1_w"]], [(D_MODEL, CDT)], t, nb)
    p = mm(nm("in_proj"), u, lw["w_in"], F32)
    q, k, v = rowwise_fwd(
        nm("attn_prep"), _f_attn_prep,
        [_pcol(p, "q"), _pcol(p, "k"), _pcol(p, "v"), tabs["ca"], tabs["sa"]], [],
        [lw["qw"], lw["kw"], tabs["gq"], tabs["gk"]], [(512, CDT), (128, CDT), (128, CDT)], t, nb)
    tq = min(ATTN_TQ, m_ctx)
    kk, vv = _split_kv(k), _split_kv(v)
    ones_rows = jnp.concatenate([jnp.ones((2, 1, t), CDT), jnp.zeros((2, ATTN_ONES_ROWS - 1, t), CDT)], axis=1)
    attn_o, qT, oT, lse = attn_fwd(nm("attn"), q, kk, jnp.concatenate([vv.transpose(0, 2, 1), ones_rows], axis=1),
                                   m_ctx, tq)

    xbc = conv_fwd(_pcol(p, "xbc"), lw["conv_w"], lw["conv_b"], m_ctx)
    ssd_sh = [lw["dt_bias"], lw["a_log"]]
    yf, yb, sf, sb = scan_fwd(nm("ssd"), _make_ssd_chunk, _ssd_rows(xbc, p), ssd_sh, 4, 512, t, m_ctx)
    (ssd_o,) = rowwise_fwd(nm("ssd_fin"), _f_ssd_finish, [yf, yb, Cols(xbc, 0, 512), _pcol(p, "z")], [],
                           [lw["d_exp"], lw["ssd_nw"]], [(512, CDT)], t, nb)

    rq, rk = rowwise_fwd(nm("ret_prep"), _f_ret_prep, [_pcol(p, "rq"), _pcol(p, "rk"), tabs["rc"], tabs["rs"]],
                         [], [], [(512, F32), (512, F32)], t, nb)
    rf, rb, rsf, rsb = scan_fwd(nm("ret"), _make_ret_chunk, _ret_rows(rq, rk, p), [lw["ret_lg"]], 4, 512, t, m_ctx)
    (ret_o,) = rowwise_fwd(nm("ret_fin"), _f_ret_finish, [rf, rb, _pcol(p, "rg")], [], [lw["ret_gw"]],
                           [(512, CDT)], t, nb)

    pbs = [mm(nm(f"branch{b}"), br, lw["w_branch"][b], CDT) for b, br in enumerate((attn_o, ssd_o, ret_o))]
    gl = [Cols(p, 1024 * b, 1024) for b in range(3)]
    (merged,) = rowwise_fwd(nm("merge"), _f_merge, pbs + gl, [], [], [(D_MODEL, CDT)], t, nb)
    mix = mm(nm("out_proj"), merged, lw["w_out"], F32)
    h_mid, vv2 = rowwise_fwd(nm("mid"), _f_mid, [h, mix], [g1, sh2, sc2], [lw["norm2_w"]],
                             [(D_MODEL, F32), (D_MODEL, CDT)], t, nb)
    a = mm(nm("mlp1"), vv2, lw["w_mlp1"], CDT)
    (hh,) = rowwise_fwd(nm("sqrelu"), _f_sqrelu, [a], [], [], [(a.shape[1], CDT)], t, nb)
    o = mm(nm("mlp2"), hh, lw["w_mlp2"], F32)
    (h_out,) = rowwise_fwd(nm("resid"), _f_residual, [h_mid, o], [g2], [], [(D_MODEL, F32)], t, nb)
    sv.update(h=h, u=u, p=p, qT=qT, kk=kk, vv=vv, oT=oT, lse=lse, attn_o=attn_o, xbc=xbc, yf=yf, yb=yb,
              sf=sf, sb=sb, ssd_o=ssd_o, rq=rq, rk=rk, rf=rf, rb=rb, rsf=rsf, rsb=rsb, ret_o=ret_o, pbs=pbs,
              merged=merged, mix=mix, h_mid=h_mid, v=vv2, a=a, hh=hh, o=o)
    return h_out, sv


def layer_bwd(li, dh_out, sv, mod, lw, tabs, m_ctx):
    t = dh_out.shape[0]
    nb = m_ctx // min(ROW_TILE, t)
    sh1, sc1, g1, sh2, sc2, g2 = mod
    nm = lambda s: f"l{li}_{s}_bwd"
    gw = {}
    p = sv["p"]
    (do,), (dg2,), _ = rowwise_bwd(nm("resid"), _f_residual, [sv["h_mid"], sv["o"]], [g2], [], [dh_out],
                                   [False, True], [], [CDT], t, nb)
    dhh = mm(nm("mlp2_dx"), do, lw["w_mlp2"], CDT, transpose_b=True)
    gw["w_mlp2"] = mm_tn(nm("mlp2_dw"), sv["hh"], do, CDT, ("rows", lw["w_mlp2"].shape[0] // 4))
    (da,), _, _ = rowwise_bwd(nm("sqrelu"), _f_sqrelu, [sv["a"]], [], [], [dhh], [True], [], [CDT], t, nb)
    dv = mm(nm("mlp1_dx"), da, lw["w_mlp1"], F32, transpose_b=True)
    gw["w_mlp1"] = mm_tn(nm("mlp1_dw"), sv["v"], da, CDT, ("cols", lw["w_mlp1"].shape[1] // 4))
    (dh_a, dmix), (dg1, dsh2, dsc2), (gw["norm2_w"],) = rowwise_bwd(
        nm("mid"), _f_mid, [sv["h"], sv["mix"]], [g1, sh2, sc2], [lw["norm2_w"]], [dh_out, dv],
        [True, True], [True], [F32, CDT], t, nb)
    dmerged = mm(nm("out_dx"), dmix, lw["w_out"], CDT, transpose_b=True)
    gw["w_out"] = mm_tn(nm("out_dw"), sv["merged"], dmix, CDT, ("rows", lw["w_out"].shape[0] // 4))
    gl = [Cols(p, 1024 * b, 1024) for b in range(3)]
    dmg, _, _ = rowwise_bwd(nm("merge"), _f_merge, sv["pbs"] + gl, [], [], [dmerged], [True] * 6, [], [CDT] * 6,
                            t, nb)
    dpb, dgl = dmg[:3], dmg[3:]
    brs = (sv["attn_o"], sv["ssd_o"], sv["ret_o"])
    d_attn_o = mm(nm("branch0_dx"), dpb[0], lw["w_branch"][0], CDT, transpose_b=True)
    d_ssd_o = mm(nm("branch1_dx"), dpb[1], lw["w_branch"][1], F32, transpose_b=True)
    d_ret_o = mm(nm("branch2_dx"), dpb[2], lw["w_branch"][2], F32, transpose_b=True)
    n_loc = lw["w_branch"].shape[2] // 4
    gw["w_branch"] = jnp.stack([mm_tn(nm(f"branch{b}_dw"), brs[b], dpb[b], CDT, ("cols", n_loc)) for b in range(3)],
                               axis=1).reshape(4, -1, n_loc)
    tq = min(ATTN_TQ, m_ctx)
    dq_rows, dk_s, dv_s = attn_bwd(nm("attn"), sv["qT"], d_attn_o, sv["oT"], sv["lse"], sv["kk"],
                                   sv["kk"].transpose(0, 2, 1), sv["vv"], m_ctx)
    (dq_raw, dk_raw, dv_raw), _, (gw["qw"], gw["kw"]) = rowwise_bwd(
        nm("attn_prep"), _f_attn_prep,
        [_pcol(p, "q"), _pcol(p, "k"), _pcol(p, "v"), tabs["ca"], tabs["sa"]], [],
        [lw["qw"], lw["kw"], tabs["gq"], tabs["gk"]],
        [dq_rows, _merge_kv(dk_s) * LN2, _merge_kv(dv_s)],
        [True, True, True, False, False], [True, True, False, False], [CDT] * 3, t, nb)
    (dy_ssd, dxs_fin, dz), _, (gw["d_exp"], gw["ssd_nw"]) = rowwise_bwd(
        nm("ssd_fin"), _f_ssd_finish, [sv["yf"], sv["yb"], Cols(sv["xbc"], 0, 512), _pcol(p, "z")], [],
        [lw["d_exp"], lw["ssd_nw"]], [d_ssd_o], [True, False, True, True], [True, True], [F32, F32, CDT], t, nb)
    ssd_sh = [lw["dt_bias"], lw["a_log"]]
    post_ssd = lambda d: [jnp.concatenate(d[0:8], axis=1), d[8]]
    (dxbc_f, ddt_f), dsh_f = scan_bwd(nm("ssd_f"), _make_ssd_chunk, _ssd_rows(sv["xbc"], p), ssd_sh, (sv["sf"],),
                                      dy_ssd, post_ssd, [(1024, F32), (LANES, F32)], 4, t, m_ctx, dirs=(0,))
    (dxbc_b, ddt_b), dsh_b = scan_bwd(nm("ssd_b"), _make_ssd_chunk, _ssd_rows(sv["xbc"], p), ssd_sh, (sv["sb"],),
                                      dy_ssd, post_ssd, [(1024, F32), (LANES, F32)], 4, t, m_ctx, dirs=(1,))
    gw["dt_bias"], gw["a_log"] = dsh_f[0] + dsh_b[0], dsh_f[1] + dsh_b[1]
    ddt = (ddt_f + ddt_b).astype(CDT)
    dxbc_raw, gw["conv_w"], gw["conv_b"] = conv_bwd(_pcol(p, "xbc"), lw["conv_w"], lw["conv_b"], dxbc_f, dxbc_b,
                                                    dxs_fin, m_ctx)
    (dy_ret, drg), _, (gw["ret_gw"],) = rowwise_bwd(
        nm("ret_fin"), _f_ret_finish, [sv["rf"], sv["rb"], _pcol(p, "rg")], [], [lw["ret_gw"]], [d_ret_o],
        [True, False, True], [True], [F32, CDT], t, nb)
    post_ret = lambda d: [jnp.concatenate(d[0:4], axis=1), jnp.concatenate(d[4:8], axis=1),
                          jnp.concatenate(d[8:12], axis=1)]
    rrows = _ret_rows(sv["rq"], sv["rk"], p)
    (dq_f, dk_f, dv_f), (dq_b, dk_b, dv_b), (gw["ret_lg"],) = scan_bwd(
        nm("ret"), _make_ret_chunk, rrows, [lw["ret_lg"]], (sv["rsf"], sv["rsb"]), dy_ret, post_ret,
        [(512, F32)] * 3, 4, t, m_ctx)
    drv = (dv_f + dv_b).astype(CDT)
    (drq, drk), _, _ = rowwise_bwd(nm("ret_prep"), _f_ret_prep,
                                   [_pcol(p, "rq"), _pcol(p, "rk"), tabs["rc"], tabs["rs"]], [], [],
                                   [(dq_f, dq_b), (dk_f, dk_b)], [True, True, False, False], [], [CDT, CDT], t, nb)
    pieces = {"gates": None, "xbc": dxbc_raw, "q": dq_raw, "z": dz, "rq": drq, "rk": drk, "rv": drv, "rg": drg,
              "k": dk_raw, "v": dv_raw, "dt": ddt}
    cols = list(dgl) + [pieces[n] for n in IN_NEW_ORDER[1:]]
    used = sum(c.shape[1] for c in cols)
    cols.append(jnp.zeros((t, IN_PAD - used), CDT))
    dp = jnp.concatenate(cols, axis=1)
    du = mm(nm("in_dx"), dp, lw["w_in"], F32, transpose_b=True)
    gw["w_in"] = mm_tn(nm("in_dw"), sv["u"], dp, CDT)
    (dh_in,), (dsh1, dsc1), (gw["norm1_w"],) = rowwise_bwd(
        nm("norm1"), _f_norm_mod_thru, [sv["h"]], [sh1, sc1], [lw["norm1_w"]], [dh_a, du], [True], [True], [F32],
        t, nb)
    return dh_in, [dsh1, dsc1, dg1, dsh2, dsc2, dg2], gw


def _rope_tables(n_lat, m_ctx):
    rows = n_lat // GRID_W
    row = jnp.repeat(jnp.arange(rows, dtype=F32), GRID_W)
    col = jnp.tile(jnp.arange(GRID_W, dtype=F32), rows)
    nfreq = ATTN_HD // 4
    inv = ROPE_THETA ** (-jnp.arange(nfreq, dtype=F32) / nfreq)
    ang = jnp.concatenate([row[:, None] * inv, col[:, None] * inv], axis=-1)
    cos = jnp.concatenate([jnp.ones((m_ctx, ATTN_HD // 2), F32), jnp.cos(ang)], axis=0)
    sin = jnp.concatenate([jnp.zeros((m_ctx, ATTN_HD // 2), F32), jnp.sin(ang)], axis=0)
    c64 = jnp.concatenate([cos, cos], axis=1)
    s64 = jnp.concatenate([-sin, sin], axis=1)
    pos = jnp.arange(m_ctx + n_lat, dtype=F32)
    inv_r = ROPE_THETA ** (-jnp.linspace(0.0, 1.0, RET_DK // 2, dtype=F32))
    ang_r = pos[:, None] * inv_r
    rc = jnp.concatenate([jnp.cos(ang_r)] * 2, axis=1)
    rs = jnp.concatenate([-jnp.sin(ang_r), jnp.sin(ang_r)], axis=1)
    return dict(ca=jnp.tile(c64, (1, 2)), sa=jnp.tile(s64, (1, 2)), rc=rc, rs=rs, gq=_group_matrix(512, ATTN_HD),
                gk=_group_matrix(128, ATTN_HD))


def _layer_weights(full, small, layer):
    return dict(
        w_in=full["w_in"][layer], w_branch=full["w_branch"][layer], w_out=full["w_out"][layer],
        w_mlp1=full["w_mlp1"][layer], w_mlp2=full["w_mlp2"][layer],
        norm1_w=small["norm1_w"][layer][None], norm2_w=small["norm2_w"][layer][None],
        qw=jnp.tile(small["attn_q_norm"][layer], 8)[None], kw=jnp.tile(small["attn_k_norm"][layer], 2)[None],
        conv_w=small["ssd_conv_w"][layer], conv_b=small["ssd_conv_b"][layer][None],
        dt_bias=_lane_pad(small["ssd_dt_bias"][layer]), a_log=_lane_pad(small["ssd_a_log"][layer]),
        d_exp=jnp.repeat(small["ssd_d"][layer], SSD_HD)[None], ssd_nw=small["ssd_norm_w"][layer][None],
        ret_lg=_lane_pad(small["ret_log_decay"][layer]), ret_gw=small["ret_gn_w"][layer][None])


def local_step(x, c, ctx, full, small, loss_target):
    n_lat, d = x.shape
    m_ctx = ctx.shape[0]
    t = n_lat + m_ctx
    depth = small["norm1_w"].shape[0]
    tabs = _rope_tables(n_lat, m_ctx)
    h = jnp.concatenate([ctx, x], axis=0)
    cc = jnp.concatenate([small["c_ctx"][None], c, jnp.zeros((COND_ROWS - 2, d), F32)], axis=0)
    (scc,) = rowwise_fwd("cond_silu", _f_silu, [cc], [], [], [(d, CDT)], COND_ROWS, 0)
    mods, saved, lws = [], [], []
    for layer in range(depth):
        lw = _layer_weights(full, small, layer)
        mod_raw = mm(f"l{layer}_mod", scc, full["w_mod"][layer], F32)
        (mod8,) = rowwise_fwd(f"l{layer}_mod_bias", _f_bias, [mod_raw], [], [small["b_mod"][layer][None]],
                              [(6 * d, F32)], COND_ROWS, 0)
        mod = [mod8[0:2, k * d:(k + 1) * d].reshape(2, 1, d) for k in range(6)]
        h, sv = layer_fwd(layer, h, mod, lw, tabs, m_ctx)
        mods.append(mod)
        saved.append(sv)
        lws.append(lw)
    loss, dh, d_final = loss_head(h, loss_target, small["final_norm_w"][None], m_ctx)

    gbig = {k: [None] * depth for k in BIG}
    gs = {k: [None] * depth for k in SMALL if k not in ("c_ctx", "final_norm_w")}
    d_scc = None
    for layer in reversed(range(depth)):
        lw = lws[layer]
        dh, dmod, gw = layer_bwd(layer, dh, saved[layer], mods[layer], lw, tabs, m_ctx)
        dmod8 = jnp.concatenate([jnp.concatenate([g_.reshape(2, d) for g_ in dmod], axis=1),
                                 jnp.zeros((COND_ROWS - 2, 6 * d), F32)], axis=0)
        (dmod_c,), _, (db_mod,) = rowwise_bwd(f"l{layer}_mod_bias_bwd", _f_bias, [dmod8], [],
                                              [small["b_mod"][layer][None]], [dmod8], [True], [True], [CDT], COND_ROWS, 0)
        gbig["w_mod"][layer] = mm_tn(f"l{layer}_mod_dw", scc, dmod_c, CDT, ("cols", 6 * d // 4))
        part = mm(f"l{layer}_mod_dx", dmod_c, full["w_mod"][layer], F32, transpose_b=True)
        d_scc = part if d_scc is None else d_scc + part
        g_in = _in_from_padded(gw["w_in"])
        gbig["w_in"][layer] = g_in.reshape(d, 4, g_in.shape[1] // 4).transpose(1, 0, 2)
        for k in ("w_branch", "w_out", "w_mlp1", "w_mlp2"):
            gbig[k][layer] = gw[k]
        gs["b_mod"][layer] = db_mod.reshape(-1)
        gs["norm1_w"][layer] = gw["norm1_w"].reshape(-1)
        gs["norm2_w"][layer] = gw["norm2_w"].reshape(-1)
        gs["attn_q_norm"][layer] = gw["qw"].reshape(8, ATTN_HD).sum(0)
        gs["attn_k_norm"][layer] = gw["kw"].reshape(2, ATTN_HD).sum(0)
        gs["ssd_conv_w"][layer] = gw["conv_w"]
        gs["ssd_conv_b"][layer] = gw["conv_b"].reshape(-1)
        gs["ssd_dt_bias"][layer] = gw["dt_bias"][0, :16].reshape(2, 8)
        gs["ssd_a_log"][layer] = gw["a_log"][0, :16].reshape(2, 8)
        gs["ssd_d"][layer] = gw["d_exp"].reshape(SSD_HEADS, SSD_HD).sum(1)
        gs["ssd_norm_w"][layer] = gw["ssd_nw"].reshape(-1)
        gs["ret_log_decay"][layer] = gw["ret_lg"][0, :8].reshape(2, 4)
        gs["ret_gn_w"][layer] = gw["ret_gw"].reshape(-1)
    (d_cc,), _, _ = rowwise_bwd("cond_silu_bwd", _f_silu, [cc], [], [], [d_scc], [True], [], [F32], COND_ROWS, 0)
    g_small = {k: jnp.stack(v) for k, v in gs.items()}
    g_small["c_ctx"] = d_cc[0]
    g_small["final_norm_w"] = d_final.reshape(-1)
    g_big = {k: jnp.stack(v) for k, v in gbig.items()}
    return loss, dh[m_ctx:], g_big, g_small


def kernel(x, c, ctx, c_ctx, w_mod, b_mod, norm1_w, norm2_w, w_in, attn_q_norm, attn_k_norm, ssd_conv_w, ssd_conv_b, ssd_dt_bias, ssd_a_log, ssd_d, ssd_norm_w, ret_log_decay, ret_gn_w, w_branch, w_out, w_mlp1, w_mlp2, final_norm_w, loss_target, m_c_ctx, m_w_mod, m_b_mod, m_norm1_w, m_norm2_w, m_w_in, m_attn_q_norm, m_attn_k_norm, m_ssd_conv_w, m_ssd_conv_b, m_ssd_dt_bias, m_ssd_a_log, m_ssd_d, m_ssd_norm_w, m_ret_log_decay, m_ret_gn_w, m_w_branch, m_w_out, m_w_mlp1, m_w_mlp2, m_final_norm_w, v_c_ctx, v_w_mod, v_b_mod, v_norm1_w, v_norm2_w, v_w_in, v_attn_q_norm, v_attn_k_norm, v_ssd_conv_w, v_ssd_conv_b, v_ssd_dt_bias, v_ssd_a_log, v_ssd_d, v_ssd_norm_w, v_ret_log_decay, v_ret_gn_w, v_w_branch, v_w_out, v_w_mlp1, v_w_mlp2, v_final_norm_w):
    env = dict(locals())
    w_loc = {k: env[k] for k in WEIGHTS}
    m_loc = {k: env["m_" + k] for k in WEIGHTS}
    v_loc = {k: env["v_" + k] for k in WEIGHTS}
    chip = 2 * lax.axis_index("x") + lax.axis_index("y")
    core = lax.axis_index("c")

    depth = w_loc["w_mod"].shape[0]
    assert depth == 2, "the exchanges split the layers between a chip's two cores"
    shards = [w_loc[k].astype(CDT).reshape(depth, -1, w_loc[k].shape[-1]) for k in BIG]
    mine = gather_layers("gather_weights", shards, [BIG_KIND[k] for k in BIG])
    full = {}
    for k, arr in zip(BIG, mine):
        both = exchange_both("share_" + k, arr.reshape(-1, arr.shape[-1]))
        if k == "w_in":
            both = both.reshape(depth, 4, -1, both.shape[-1]).transpose(0, 2, 1, 3)
            both = _in_to_padded(both.reshape(depth, both.shape[1], -1))
        full[k] = both.reshape((depth,) + w_loc[k].shape[1:-1] + (-1,)) if BIG_KIND[k] == "cols" else \
            both.reshape((depth,) + w_loc[k].shape[1:-2] + (-1, w_loc[k].shape[-1])) if BIG_KIND[k] == "rows" else both

    cw = w_loc["ssd_conv_w"]
    cw_w = cw.shape[-1]
    placed = lax.dynamic_update_slice(jnp.zeros(cw.shape[:-1] + (4 * cw_w,), F32),
                                      cw * (core == 0).astype(F32), (0, 0, chip * cw_w))
    conv_full = _unpack_flat(allreduce_small("gather_conv_w", _pack_flat([placed], F32, LANES)), [placed.shape])[0]
    small = {k: w_loc[k] for k in SMALL}
    small["ssd_conv_w"] = conv_full

    loss_l, grad_x, g_big, g_small = local_step(x[0], c, ctx[0], full, small, loss_target[0])

    small_shapes = [g_small[k].shape for k in SMALL] + [(LANES,)]
    summed = _unpack_flat(allreduce_small("reduce_small", _pack_flat([g_small[k] for k in SMALL] + [loss_l], F32, LANES)),
                          small_shapes)
    gsum = dict(zip(SMALL, summed[:-1]))
    loss = summed[-1][0]
    gsum["ssd_conv_w"] = lax.dynamic_slice(gsum["ssd_conv_w"], (0, 0, chip * cw_w), cw.shape)

    pair = []
    for k in BIG:
        _, _, rows, cols = g_big[k].shape
        pair.append(exchange_add("pair_" + k, g_big[k].reshape(depth, 4 * rows, cols)).reshape(4, rows, cols))
    landed = scatter_pieces("scatter_grads", pair)
    g_sum = [sum_exchange("sum_" + k, parts) for k, parts in zip(BIG, landed)]

    grads, deltas, new_m, new_v = {}, {}, {}, {}
    for i, k in enumerate(BIG):
        shp = w_loc[k].shape
        three_d = lambda a, shp=shp: a.reshape((-1,) + shp[-2:])
        res = adamw("adamw_" + k, three_d(w_loc[k]), three_d(m_loc[k]), three_d(v_loc[k]), [three_d(g_sum[i])])
        grads[k], deltas[k], new_m[k], new_v[k] = [r.reshape(shp) for r in res]
    small_loc_shapes = [w_loc[k].shape for k in SMALL]
    res = adamw("adamw_small", _pack_flat([w_loc[k] for k in SMALL], F32, LANES)[None],
                _pack_flat([m_loc[k] for k in SMALL], F32, LANES)[None],
                _pack_flat([v_loc[k] for k in SMALL], F32, LANES)[None],
                [_pack_flat([gsum[k] for k in SMALL], F32, LANES)[None]])
    for dst, r in zip((grads, deltas, new_m, new_v), res):
        dst.update(dict(zip(SMALL, _unpack_flat(r, small_loc_shapes))))

    return (loss, grad_x[None], *[grads[k] for k in WEIGHTS], *[deltas[k] for k in WEIGHTS],
            *[new_m[k] for k in WEIGHTS], *[new_v[k] for k in WEIGHTS])
```

```python
import functools
import math
from typing import NamedTuple

import jax
import jax.numpy as jnp
from jax import lax
from jax.experimental import pallas as pl
from jax.experimental.pallas import tpu as pltpu

F32 = jnp.float32
CDT = jnp.bfloat16
NORM_EPS = 1e-6
ROPE_THETA = 10000.0
GRID_W = 64
D_MODEL = 1024
ATTN_HEADS, ATTN_KV, ATTN_HD = 8, 2, 64
SSD_HEADS, SSD_HD, SSD_STATE = 8, 64, 128
RET_HEADS, RET_DK = 4, 128
CHUNK = 256
ROW_TILE = 256
MM_ROWS = 768
MM_TN_ROWS = 2816
MM_VMEM_BUDGET = 44 * 1024 * 1024
ATTN_TQ, ATTN_TK = 256, 256
ATTN_ONES_ROWS = 16
ATTN_TK_BWD = 2048
LOG2E, LN2 = 1.4426950408889634, 0.6931471805599453
ATTN_TK_FWD = 2048
LANES = 128
PAIR_BLOCK_BYTES = 2 * 1024 * 1024
COND_ROWS = 16
VMEM_LIMIT = 56 * 1024 * 1024

ADAM_LR, ADAM_B1, ADAM_B2, ADAM_EPS, ADAM_WD, ADAM_STEP = 0.001, 0.9, 0.999, 1e-08, 0.01, 10

IN_LAYOUT = {
    "gates": (0, 3072, 4368, 3072), "xbc": (3072, 1024, 1280, 1024), "q": (4096, 512, 0, 512),
    "z": (4608, 512, 768, 512), "rq": (5120, 512, 2320, 512), "rk": (5632, 512, 2832, 512),
    "rv": (6144, 512, 3344, 512), "rg": (6656, 512, 3856, 512), "k": (7168, 128, 512, 128),
    "v": (7296, 128, 640, 128), "dt": (7424, 128, 2304, 16),
}
IN_PAD = 7680
IN_ORIG_ORDER = ("q", "k", "v", "z", "xbc", "dt", "rq", "rk", "rv", "rg", "gates")
IN_NEW_ORDER = ("gates", "xbc", "q", "z", "rq", "rk", "rv", "rg", "k", "v", "dt")

BIG = ("w_mod", "w_in", "w_branch", "w_out", "w_mlp1", "w_mlp2")
BIG_KIND = {"w_mod": "cols", "w_in": "slices", "w_branch": "cols", "w_out": "rows", "w_mlp1": "cols", "w_mlp2": "rows"}
SMALL = ("c_ctx", "b_mod", "norm1_w", "norm2_w", "attn_q_norm", "attn_k_norm", "ssd_conv_w", "ssd_conv_b",
         "ssd_dt_bias", "ssd_a_log", "ssd_d", "ssd_norm_w", "ret_log_decay", "ret_gn_w", "final_norm_w")
WEIGHTS = ("c_ctx", "w_mod", "b_mod", "norm1_w", "norm2_w", "w_in", "attn_q_norm", "attn_k_norm", "ssd_conv_w",
           "ssd_conv_b", "ssd_dt_bias", "ssd_a_log", "ssd_d", "ssd_norm_w", "ret_log_decay", "ret_gn_w",
           "w_branch", "w_out", "w_mlp1", "w_mlp2", "final_norm_w")


def _cp(sem):
    return pltpu.CompilerParams(dimension_semantics=sem, vmem_limit_bytes=VMEM_LIMIT)


class Cols(NamedTuple):
    arr: jax.Array
    off: int
    width: int


def _width(item):
    return item.width if isinstance(item, Cols) else item.shape[1]


def _row_in(item, rows, imap=None):
    imap = imap or (lambda i: i)
    if isinstance(item, Cols):
        assert item.off % item.width == 0
        blk = item.off // item.width
        return item.arr, pl.BlockSpec((rows, item.width), lambda i, blk=blk: (imap(i), blk))
    return item, pl.BlockSpec((rows, item.shape[1]), lambda i: (imap(i), 0))


def _const_spec(shape):
    return pl.BlockSpec(shape, lambda *_: (0,) * len(shape))


def _mxu(a, b, dims=(((1,), (0,)), ((), ()))):
    return lax.dot_general(a.astype(CDT), b.astype(CDT), dims, preferred_element_type=F32)


_NT = (((1,), (1,)), ((), ()))
_TN = (((0,), (0,)), ((), ()))


@jax.custom_vjp
def _softplus(x):
    return jnp.maximum(x, 0.0) + jnp.log1p(jnp.exp(-jnp.abs(x)))


def _softplus_fwd(x):
    return _softplus(x), x


def _softplus_bwd(x, g):
    return (g * jax.nn.sigmoid(x),)


_softplus.defvjp(_softplus_fwd, _softplus_bwd)


def _group_mean_impl(x, gmat):
    hi = x.astype(CDT)
    lo = (x - hi.astype(F32)).astype(CDT)
    return (jnp.dot(hi, gmat, preferred_element_type=F32) + jnp.dot(lo, gmat, preferred_element_type=F32))


@jax.custom_vjp
def _group_mean(x, gmat):
    return _group_mean_impl(x, gmat)


def _group_mean_fwd(x, gmat):
    return _group_mean_impl(x, gmat), gmat


def _group_mean_bwd(gmat, g):
    return _group_mean_impl(g, gmat), jnp.zeros_like(gmat)


_group_mean.defvjp(_group_mean_fwd, _group_mean_bwd)


def _group_matrix(width, group):
    r = jnp.arange(width) // group
    return jnp.where(r[:, None] == r[None, :], 1.0 / group, 0.0).astype(CDT)


def _make_rope(half):
    def partner(x):
        w = x.shape[1]
        lane = lax.broadcasted_iota(jnp.int32, x.shape, 1)
        first = (lane % (2 * half)) < half
        return jnp.where(first, pltpu.roll(x, w - half, axis=1), pltpu.roll(x, half, axis=1))

    def impl(x, cos_full, sin_signed):
        return x * cos_full + partner(x) * sin_signed

    @jax.custom_vjp
    def rope(x, cos_full, sin_signed):
        return impl(x, cos_full, sin_signed)

    def fwd(x, cos_full, sin_signed):
        return impl(x, cos_full, sin_signed), (cos_full, sin_signed)

    def bwd(res, g):
        cos_full, sin_signed = res
        return impl(g, cos_full, -sin_signed), jnp.zeros_like(cos_full), jnp.zeros_like(sin_signed)

    rope.defvjp(fwd, bwd)
    return rope


_rope32 = _make_rope(32)
_rope64 = _make_rope(64)


def _rms(x, w):
    return x * lax.rsqrt(jnp.mean(x * x, axis=-1, keepdims=True) + NORM_EPS) * w


def _col(v, lane_index):
    lane = lax.broadcasted_iota(jnp.int32, v.shape, 1)
    return jnp.sum(jnp.where(lane == lane_index, v, 0.0), axis=1, keepdims=True)


def _typed_spec(width, nb_ctx):
    return pl.BlockSpec((None, 1, width), lambda i: (jnp.where(i >= nb_ctx, 1, 0), 0, 0))


def rowwise_fwd(name, f, rows, typed, shared, outs, n_rows, nb_ctx, tm=ROW_TILE):
    tm = min(tm, n_rows)
    nin = len(rows) + len(typed) + len(shared)

    def body(*refs):
        res = f(*[r[...] for r in refs[:nin]])
        for o_ref, o in zip(refs[nin:], res):
            o_ref[...] = o.astype(o_ref.dtype)

    arrs, specs = [], []
    for it in rows:
        a, s = _row_in(it, tm)
        arrs.append(a)
        specs.append(s)
    for t in typed:
        arrs.append(t)
        specs.append(_typed_spec(t.shape[-1], nb_ctx))
    for s_ in shared:
        arrs.append(s_)
        specs.append(_const_spec(s_.shape))
    res = pl.pallas_call(
        body, name=name, grid=(n_rows // tm,), in_specs=specs,
        out_specs=[pl.BlockSpec((tm, w), lambda i: (i, 0)) for w, _ in outs],
        out_shape=[jax.ShapeDtypeStruct((n_rows, w), dt) for w, dt in outs],
        compiler_params=_cp(("parallel",)))(*arrs)
    return res


def rowwise_bwd(name, f, rows, typed, shared, cots, row_diff, shared_diff, drow_dtypes, n_rows, nb_ctx, tm=ROW_TILE):
    tm = min(tm, n_rows)
    cot_groups = [c_ if isinstance(c_, tuple) else (c_,) for c_ in cots]
    cots = [a for grp in cot_groups for a in grp]
    nr, nt, ns, nc = len(rows), len(typed), len(shared), len(cots)
    nin = nr + nt + ns
    d_rows = [k for k in range(nr) if row_diff[k]]
    d_sh = [k for k in range(ns) if shared_diff[k]]

    def body(*refs):
        rvals = [r[...] for r in refs[:nr]]
        tvals = [r[...] for r in refs[nr:nr + nt]]
        svals = [r[...] for r in refs[nr + nt:nin]]
        cparts = [r[...].astype(F32) for r in refs[nin:nin + nc]]
        cvals = []
        for grp in cot_groups:
            cvals.append(sum(cparts[1:len(grp)], cparts[0]))
            cparts = cparts[len(grp):]
        out_refs = refs[nin + nc:]

        def g(*dv):
            dv = list(dv)
            rv = list(rvals)
            for k in d_rows:
                rv[k] = dv.pop(0)
            tv = [dv.pop(0) for _ in range(nt)]
            sv = list(svals)
            for k in d_sh:
                sv[k] = dv.pop(0)
            return tuple(o.astype(F32) for o in f(*rv, *tv, *sv))

        prim = [rvals[k].astype(F32) for k in d_rows] + tvals + [svals[k] for k in d_sh]
        _, vjp = jax.vjp(g, *prim)
        grads = list(vjp(tuple(cvals)))
        i = pl.program_id(0)
        for ref in out_refs[:len(d_rows)]:
            ref[...] = grads.pop(0).astype(ref.dtype)
        first_typed = (i == 0) | (i == nb_ctx)
        for ref in out_refs[len(d_rows):len(d_rows) + nt]:
            gr = grads.pop(0)

            @pl.when(first_typed)
            def _(ref=ref, gr=gr):
                ref[...] = gr

            @pl.when(jnp.logical_not(first_typed))
            def _(ref=ref, gr=gr):
                ref[...] += gr
        for ref in out_refs[len(d_rows) + nt:]:
            gr = grads.pop(0)

            @pl.when(i == 0)
            def _(ref=ref, gr=gr):
                ref[...] = gr

            @pl.when(i != 0)
            def _(ref=ref, gr=gr):
                ref[...] += gr

    arrs, specs = [], []
    for it in list(rows):
        a, s = _row_in(it, tm)
        arrs.append(a)
        specs.append(s)
    for t in typed:
        arrs.append(t)
        specs.append(_typed_spec(t.shape[-1], nb_ctx))
    for s_ in shared:
        arrs.append(s_)
        specs.append(_const_spec(s_.shape))
    for c_ in cots:
        a, s = _row_in(c_, tm)
        arrs.append(a)
        specs.append(s)
    out_specs, out_shape = [], []
    for k, dt in zip(d_rows, drow_dtypes):
        w = _width(rows[k])
        out_specs.append(pl.BlockSpec((tm, w), lambda i: (i, 0)))
        out_shape.append(jax.ShapeDtypeStruct((n_rows, w), dt))
    for t in typed:
        out_specs.append(_typed_spec(t.shape[-1], nb_ctx))
        out_shape.append(jax.ShapeDtypeStruct(t.shape, F32))
    for k in d_sh:
        out_specs.append(_const_spec(shared[k].shape))
        out_shape.append(jax.ShapeDtypeStruct(shared[k].shape, F32))
    res = pl.pallas_call(body, name=name, grid=(n_rows // tm,), in_specs=specs, out_specs=out_specs,
                         out_shape=out_shape, compiler_params=_cp(("arbitrary",)))(*arrs)
    n1, n2 = len(d_rows), len(d_rows) + nt
    return list(res[:n1]), list(res[n1:n2]), list(res[n2:])


def _pick(n, prefs):
    for p in prefs:
        if n % p == 0:
            return p
    return n


def mm(name, a, b, out_dtype, transpose_b=False, epilogue=None, tile_ins=(), out_dtypes=None):
    n, k = b.shape if transpose_b else b.shape[::-1]
    m = (a.arr if isinstance(a, Cols) else a).shape[0]
    assert _width(a) == k
    tm = _pick(m, (MM_ROWS, 256))
    out_dtypes = out_dtypes or [out_dtype]
    tile_bytes = sum(jnp.dtype(d).itemsize for d in out_dtypes) + sum(x.dtype.itemsize for x in tile_ins)
    tn = next(c for c in (2560, 2048, 1536, 1024, 512, 256, 128, n)
              if n % c == 0 and 2 * (tm * k * 2 + k * c * 2 + tm * c * tile_bytes) <= MM_VMEM_BUDGET or c == n)
    dims = _NT if transpose_b else (((1,), (0,)), ((), ()))
    n_ex = len(tile_ins)

    def body(a_ref, b_ref, *refs):
        prod = lax.dot_general(a_ref[...], b_ref[...], dims, preferred_element_type=F32)
        outs = (prod,) if epilogue is None else epilogue(prod, *[r[...] for r in refs[:n_ex]])
        for o_ref, o in zip(refs[n_ex:], outs):
            o_ref[...] = o.astype(o_ref.dtype)

    a_arr, a_spec = _row_in(a, tm)
    a_spec = pl.BlockSpec(a_spec.block_shape, lambda j, i, f=a_spec.index_map: f(i))
    b_spec = pl.BlockSpec((tn, k), lambda j, i: (j, 0)) if transpose_b else pl.BlockSpec((k, tn), lambda j, i: (0, j))
    tile = pl.BlockSpec((tm, tn), lambda j, i: (i, j))
    res = pl.pallas_call(
        body, name=name, grid=(n // tn, m // tm), in_specs=[a_spec, b_spec] + [tile] * n_ex,
        out_specs=[tile] * len(out_dtypes),
        out_shape=[jax.ShapeDtypeStruct((m, n), d) for d in out_dtypes],
        compiler_params=_cp(("parallel", "parallel")))(a_arr, b, *tile_ins)
    return res[0] if epilogue is None else res


def mm_tn(name, a, b, out_dtype=F32, pieces=None):
    t = (a.arr if isinstance(a, Cols) else a).shape[0]
    k, n = _width(a), _width(b)
    tt = _pick(t, (MM_TN_ROWS, MM_ROWS, 256))
    k_unit = pieces[1] if pieces and pieces[0] == "rows" else k
    n_unit = pieces[1] if pieces and pieces[0] == "cols" else n
    tk = _pick(k_unit, (1024, 512, 256, 128))
    tn = _pick(n_unit, (1280, 1024, 512, 256, 128))
    n_t = t // tt

    def body(a_ref, b_ref, o_ref, acc):
        part = lax.dot_general(a_ref[...], b_ref[...], _TN, preferred_element_type=F32)
        ti = pl.program_id(2)

        @pl.when(ti == 0)
        def _():
            acc[...] = part

        @pl.when(ti != 0)
        def _():
            acc[...] += part

        @pl.when(ti == n_t - 1)
        def _():
            o_ref[...] = acc[...].astype(o_ref.dtype)

    def win(item, width):
        if isinstance(item, Cols):
            assert item.off % width == 0
            return item.arr, item.off // width
        return item, 0

    a_arr, a0 = win(a, tk)
    b_arr, b0 = win(b, tn)
    if pieces is None:
        out_spec = pl.BlockSpec((tk, tn), lambda ki, ni, ti: (ki, ni))
        out_shape = (k, n)
    elif pieces[0] == "cols":
        per = n_unit // tn
        out_spec = pl.BlockSpec((None, tk, tn), lambda ki, ni, ti: (ni // per, ki, ni % per))
        out_shape = (4, k, n_unit)
    else:
        per = k_unit // tk
        out_spec = pl.BlockSpec((None, tk, tn), lambda ki, ni, ti: (ki // per, ki % per, ni))
        out_shape = (4, k_unit, n)
    return pl.pallas_call(
        body, name=name, grid=(k // tk, n // tn, n_t),
        in_specs=[pl.BlockSpec((tt, tk), lambda ki, ni, ti: (ti, a0 + ki)),
                  pl.BlockSpec((tt, tn), lambda ki, ni, ti: (ti, b0 + ni))],
        out_specs=out_spec, out_shape=jax.ShapeDtypeStruct(out_shape, out_dtype),
        scratch_shapes=[pltpu.VMEM((tk, tn), F32)],
        compiler_params=_cp(("parallel", "parallel", "arbitrary")))(a_arr, b_arr)


def _heads_t(rows_blk):
    blk = rows_blk.astype(F32).T
    return jnp.concatenate([blk[hh * ATTN_HD:(hh + 1) * ATTN_HD, :] for hh in range(4)], axis=1)


def _heads_rows(t_blk):
    tq = t_blk.shape[1] // 4
    return jnp.concatenate([t_blk[:, hh * tq:(hh + 1) * tq] for hh in range(4)], axis=0).T


def attn_fwd(name, q, kk, vT_ones, m_ctx, tq):
    t, hd, hd_ext = kk.shape[1], ATTN_HD, vT_ones.shape[1]
    nq, r = t // tq, 4 * tq
    tk = _pick(t - m_ctx, (ATTN_TK_FWD, ATTN_TK))
    nqc, n_lat_tiles = m_ctx // tq, (t - m_ctx) // tk

    def body(q_ref, k_ref, vT_ref, o_ref, qT_ref, oT_ref, lse_ref):
        i = pl.program_id(1)
        q_t = _heads_t(q_ref[...]).astype(CDT)
        qT_ref[...] = q_t

        def tile(off, size, carry):
            mi, acc = carry
            sub = min(size, ATTN_TK)
            offs = [off + u * sub for u in range(size // sub)]
            sts = [jnp.dot(k_ref[pl.ds(o, sub), :], q_t, preferred_element_type=F32) for o in offs]
            for o, st in zip(offs, sts):
                mn = jnp.maximum(mi, jnp.max(st, axis=0, keepdims=True))
                pt = jnp.exp2(st - mn)
                acc = jnp.exp2(mi - mn) * acc + jnp.dot(vT_ref[:, pl.ds(o, sub)], pt.astype(CDT),
                                                        preferred_element_type=F32)
                mi = mn
            return mi, acc

        carry = tile(0, m_ctx, (jnp.full((1, r), -1e30, F32), jnp.zeros((hd_ext, r), F32)))
        mi, acc = lax.fori_loop(
            0, jnp.where(i < nqc, 0, n_lat_tiles),
            lambda j, cr: tile(pl.multiple_of(m_ctx + j * tk, ATTN_TK), tk, cr), carry)
        li = acc[hd:hd + 1]
        o_t = acc[:hd] / li
        oT_ref[...] = o_t.astype(oT_ref.dtype)
        o_ref[...] = _heads_rows(o_t).astype(o_ref.dtype)
        lse_ref[...] = mi + jnp.log2(li)

    blk_t = pl.BlockSpec((None, None, hd, r), lambda g, i: (g, i, 0, 0))
    rows = pl.BlockSpec((tq, 4 * hd), lambda g, i: (i, g))
    return pl.pallas_call(
        body, name=name, grid=(2, nq),
        in_specs=[rows, pl.BlockSpec((None, t, hd), lambda g, i: (g, 0, 0)),
                  pl.BlockSpec((None, hd_ext, t), lambda g, i: (g, 0, 0))],
        out_specs=[rows, blk_t, blk_t, pl.BlockSpec((None, None, 1, r), lambda g, i: (g, i, 0, 0))],
        out_shape=[jax.ShapeDtypeStruct((t, 8 * hd), CDT), jax.ShapeDtypeStruct((2, nq, hd, r), CDT),
                   jax.ShapeDtypeStruct((2, nq, hd, r), CDT), jax.ShapeDtypeStruct((2, nq, 1, r), F32)],
        compiler_params=_cp(("parallel", "arbitrary")))(q, kk, vT_ones)


def attn_bwd(name, qT, do, oT, lse, kk, kT, vv, m_ctx):
    _, nq, hd, r = qT.shape
    t = kk.shape[1]
    tq = r // 4
    tk = _pick(t - m_ctx, (ATTN_TK_BWD, ATTN_TK))
    nqc, n_lat_tiles = m_ctx // tq, (t - m_ctx) // tk

    def body(qT_ref, do_ref, oT_ref, lse_ref, k_ref, kT_ref, v_ref, dq_ref, dk_ref, dv_ref):
        i = pl.program_id(1)

        @pl.when(i == 0)
        def _():
            dk_ref[...] = jnp.zeros_like(dk_ref)
            dv_ref[...] = jnp.zeros_like(dv_ref)

        q_t = qT_ref[...]
        do_f = _heads_t(do_ref[...])
        do_t = do_f.astype(CDT)
        lse = lse_ref[...]
        delta = jnp.sum(do_f * oT_ref[...].astype(F32), axis=0, keepdims=True)

        def tile(off, size, dq):
            sub = min(size, ATTN_TK)
            offs = [off + u * sub for u in range(size // sub)]
            sts = [jnp.dot(k_ref[pl.ds(o, sub), :], q_t, preferred_element_type=F32) for o in offs]
            dpts = [jnp.dot(v_ref[pl.ds(o, sub), :], do_t, preferred_element_type=F32) for o in offs]
            for o, st, dpt in zip(offs, sts, dpts):
                pt = jnp.exp2(st - lse)
                dv_ref[pl.ds(o, sub), :] += lax.dot_general(pt.astype(CDT), do_t, _NT, preferred_element_type=F32)
                dst = (pt * (dpt - delta)).astype(CDT)
                dk_ref[pl.ds(o, sub), :] += lax.dot_general(dst, q_t, _NT, preferred_element_type=F32)
                dq = dq + jnp.dot(kT_ref[:, pl.ds(o, sub)], dst, preferred_element_type=F32)
            return dq

        dq = tile(0, m_ctx, jnp.zeros((hd, r), F32))
        dq = lax.fori_loop(0, jnp.where(i < nqc, 0, n_lat_tiles),
                           lambda j, acc: tile(pl.multiple_of(m_ctx + j * tk, ATTN_TK), tk, acc), dq)
        dq_ref[...] = _heads_rows(dq * LN2)

    blk_t = pl.BlockSpec((None, None, hd, r), lambda g, i: (g, i, 0, 0))
    row = pl.BlockSpec((None, None, 1, r), lambda g, i: (g, i, 0, 0))
    kv = pl.BlockSpec((None, t, hd), lambda g, i: (g, 0, 0))
    rows = pl.BlockSpec((tq, 4 * hd), lambda g, i: (i, g))
    return pl.pallas_call(
        body, name=name, grid=(2, nq),
        in_specs=[blk_t, rows, blk_t, row, kv, pl.BlockSpec((None, hd, t), lambda g, i: (g, 0, 0)), kv],
        out_specs=[rows, kv, kv],
        out_shape=[jax.ShapeDtypeStruct((t, 8 * hd), F32), jax.ShapeDtypeStruct(kk.shape, F32),
                   jax.ShapeDtypeStruct(kk.shape, F32)],
        compiler_params=_cp(("parallel", "arbitrary")))(qT, do, oT, lse, kk, kT, vv)


def _split_kv(a):
    return a.reshape(a.shape[0], 2, ATTN_HD).transpose(1, 0, 2)


def _merge_kv(a):
    return a.transpose(1, 0, 2).reshape(a.shape[1], 2 * ATTN_HD)


def _chunk_order(rev, ncc, nct):
    if not rev:
        return lambda s: s
    return lambda s: jnp.where(s < ncc, ncc - 1 - s, nct - 1 - (s - ncc))


def scan_fwd(name, make_fn, rows, shared, n_state, y_width, n_rows, m_ctx):
    nct, ncc = n_rows // CHUNK, m_ctx // CHUNK
    orders = [_chunk_order(rev, ncc, nct) for rev in (False, True)]
    fns = [make_fn(0), make_fn(1)]
    nr, ns = len(rows), len(shared)

    def body(*refs):
        svals = [r[...] for r in refs[2 * nr:2 * nr + ns]]
        y_refs, sin_refs, st = refs[2 * nr + ns:2 * nr + ns + 2], refs[2 * nr + ns + 2:2 * nr + ns + 4], refs[-1]

        @pl.when(pl.program_id(0) == 0)
        def _():
            st[...] = jnp.zeros_like(st)

        for d in range(2):
            rvals = [r[...] for r in refs[d * nr:(d + 1) * nr]]
            prev = [st[d, k] for k in range(n_state)]
            sin_refs[d][...] = st[d]
            y, new = fns[d](rvals, svals, prev)
            y_refs[d][...] = y
            for k in range(n_state):
                st[d, k] = new[k]

    arrs, specs = [], []
    for order in orders:
        for it in rows:
            a, s = _row_in(it, CHUNK, order)
            arrs.append(a)
            specs.append(s)
    for s_ in shared:
        arrs.append(s_)
        specs.append(_const_spec(s_.shape))
    return pl.pallas_call(
        body, name=name, grid=(nct,), in_specs=specs,
        out_specs=[pl.BlockSpec((CHUNK, y_width), lambda s, o=o: (o(s), 0)) for o in orders]
        + [pl.BlockSpec((None, n_state, LANES, LANES), lambda s, o=o: (o(s), 0, 0, 0)) for o in orders],
        out_shape=[jax.ShapeDtypeStruct((n_rows, y_width), F32)] * 2
        + [jax.ShapeDtypeStruct((nct, n_state, LANES, LANES), F32)] * 2,
        scratch_shapes=[pltpu.VMEM((2, n_state, LANES, LANES), F32)],
        compiler_params=_cp(("arbitrary",)))(*arrs)


def scan_bwd(name, make_fn, rows, shared, states_in, dy, post, outs, n_state, n_rows, m_ctx, dirs=(0, 1)):
    nct, ncc = n_rows // CHUNK, m_ctx // CHUNK
    orders = [(lambda r, f=_chunk_order(d == 1, ncc, nct): f(nct - 1 - r)) for d in dirs]
    fns = [make_fn(d) for d in dirs]
    nd = len(dirs)
    nr, ns, no = len(rows), len(shared), len(outs)
    n_in = nd * nr + ns

    def body(*refs):
        svals = [r[...] for r in refs[nd * nr:n_in]]
        sin_refs, dy_refs = refs[n_in:n_in + nd], refs[n_in + nd:n_in + 2 * nd]
        out_refs = refs[n_in + 2 * nd:n_in + 2 * nd + nd * no]
        dsh_refs = refs[n_in + 2 * nd + nd * no:-1]
        dst = refs[-1]
        r = pl.program_id(0)

        @pl.when(r == 0)
        def _():
            dst[...] = jnp.zeros_like(dst)

        d_shared = None
        for d in range(nd):
            rvals = [x[...] for x in refs[d * nr:(d + 1) * nr]]
            prev = [sin_refs[d][k] for k in range(n_state)]
            _, vjp = jax.vjp(fns[d], rvals, svals, prev)
            d_rows, d_sh, d_prev = vjp((dy_refs[d][...], [dst[d, k] for k in range(n_state)]))
            for ref, val in zip(out_refs[d * no:(d + 1) * no], post(d_rows)):
                ref[...] = val.astype(ref.dtype)
            d_shared = d_sh if d_shared is None else [a + b for a, b in zip(d_shared, d_sh)]
            for k in range(n_state):
                dst[d, k] = d_prev[k]
        for ref, gr in zip(dsh_refs, d_shared):
            @pl.when(r == 0)
            def _(ref=ref, gr=gr):
                ref[...] = gr

            @pl.when(r != 0)
            def _(ref=ref, gr=gr):
                ref[...] += gr

    arrs, specs = [], []
    for order in orders:
        for it in rows:
            a, s = _row_in(it, CHUNK, order)
            arrs.append(a)
            specs.append(s)
    for s_ in shared:
        arrs.append(s_)
        specs.append(_const_spec(s_.shape))
    for sin, order in zip(states_in, orders):
        arrs.append(sin)
        specs.append(pl.BlockSpec((None, n_state, LANES, LANES), lambda r, o=order: (o(r), 0, 0, 0)))
    for order in orders:
        a, s = _row_in(dy, CHUNK, order)
        arrs.append(a)
        specs.append(s)
    out_specs = [pl.BlockSpec((CHUNK, w), lambda r, o=o: (o(r), 0)) for o in orders for w, _ in outs]
    out_shape = [jax.ShapeDtypeStruct((n_rows, w), dt) for _ in orders for w, dt in outs]
    for s_ in shared:
        out_specs.append(_const_spec(s_.shape))
        out_shape.append(jax.ShapeDtypeStruct(s_.shape, F32))
    res = pl.pallas_call(body, name=name, grid=(nct,), in_specs=specs, out_specs=out_specs, out_shape=out_shape,
                         scratch_shapes=[pltpu.VMEM((nd, n_state, LANES, LANES), F32)],
                         compiler_params=_cp(("arbitrary",)))(*arrs)
    return [list(res[d * no:(d + 1) * no]) for d in range(nd)] + [list(res[nd * no:])]


def _make_ssd_chunk(direction):
    rev = direction == 1
    base = 8 * direction

    def fn(rows, shared, prev):
        xs, bms, cms, dtraw = rows[0:4], rows[4:6], rows[6:8], rows[8]
        dt_bias, a_log = shared
        ln = dtraw.shape[0]
        dt_all = _softplus(dtraw + dt_bias)
        a_all = dt_all * (-jnp.exp(a_log))
        r_i = lax.broadcasted_iota(jnp.int32, (ln, ln), 0)
        c_i = lax.broadcasted_iota(jnp.int32, (ln, ln), 1)
        tri = (r_i <= c_i) if rev else (r_i >= c_i)
        a_cum_all = jnp.dot(tri.astype(F32), a_all, precision=lax.Precision.HIGHEST, preferred_element_type=F32)
        a_tot_all = jnp.sum(a_all, axis=0, keepdims=True)
        first = lax.broadcasted_iota(jnp.int32, (ln, LANES), 1) < SSD_HD
        first_row = lax.broadcasted_iota(jnp.int32, (LANES, 1), 0) < SSD_HD

        def lmat(acol):
            a_b = jnp.broadcast_to(acol, (ln, ln))
            seg = a_b - a_b.T
            return jnp.where(tri, jnp.exp(jnp.where(tri, seg, 0.0)), 0.0)

        ys, new = [], []
        for g in range(2):
            bm, cm = bms[g], cms[g]
            cb = _mxu(cm, bm, _NT)
            for jj in range(2):
                pr = 2 * g + jj
                h0, h1 = base + 2 * pr, base + 2 * pr + 1
                ac0, ac1 = _col(a_cum_all, h0), _col(a_cum_all, h1)
                at0, at1 = _col(a_tot_all, h0), _col(a_tot_all, h1)
                dt_pair = jnp.where(first, _col(dt_all, h0), _col(dt_all, h1))
                acum_pair = jnp.where(first, ac0, ac1)
                atot_pair = jnp.where(first[0:1], at0, at1)
                xd = xs[pr] * dt_pair
                st = _mxu(xd * jnp.exp(atot_pair - acum_pair), bm, _TN)
                new.append(prev[pr] * jnp.where(first_row, jnp.exp(at0), jnp.exp(at1)) + st)
                y0 = _mxu(cb * lmat(ac0), xd)
                y1 = _mxu(cb * lmat(ac1), xd)
                y_off = _mxu(cm, prev[pr], _NT) * jnp.exp(acum_pair)
                ys.append(jnp.where(first, y0, y1) + y_off)
        return jnp.concatenate(ys, axis=1), new

    return fn


def _make_ret_chunk(direction):
    rev = direction == 1
    base = 4 * direction

    def fn(rows, shared, prev):
        qs, ks, vs = rows[0:4], rows[4:8], rows[8:12]
        lg_all = -jnp.exp(shared[0])
        ln = qs[0].shape[0]
        pos = lax.broadcasted_iota(jnp.int32, (ln, 1), 0).astype(F32)
        r_i = lax.broadcasted_iota(jnp.int32, (ln, ln), 0)
        c_i = lax.broadcasted_iota(jnp.int32, (ln, ln), 1)
        diff = ((c_i - r_i) if rev else (r_i - c_i))
        mask = diff >= 0
        dpos = jnp.maximum(diff, 0).astype(F32)
        k_pow = pos if rev else (ln - 1.0 - pos)
        q_pow = (ln - pos) if rev else (pos + 1.0)
        ys, new = [], []
        for h in range(RET_HEADS):
            lg = _col(lg_all, base + h)
            dmat = jnp.where(mask, jnp.exp(dpos * lg), 0.0)
            st = _mxu(ks[h] * jnp.exp(k_pow * lg), vs[h], _TN)
            new.append(prev[h] * jnp.exp(ln * lg) + st)
            s = _mxu(qs[h], ks[h], _NT) * dmat
            ys.append(_mxu(s, vs[h]) + _mxu(qs[h], prev[h]) * jnp.exp(q_pow * lg))
        return jnp.concatenate(ys, axis=1), new

    return fn


def _conv_pre(x, w, b, t_idx, n_rows, m_ctx):
    is_start = (t_idx == 0) | (t_idx == m_ctx)
    is_end = (t_idx == m_ctx - 1) | (t_idx == n_rows - 1)
    xp = jnp.where(is_start, 0.0, pltpu.roll(x, 1, axis=0))
    xn = jnp.where(is_end, 0.0, pltpu.roll(x, n_rows - 1, axis=0))
    return w[0:1] * xp + w[1:2] * x + w[2:3] * xn + b, xp, xn, is_start, is_end


def conv_fwd(x, conv_w, conv_b, m_ctx):
    n_rows, width = x.arr.shape[0], x.width
    c0 = x.off // LANES

    def body(x_ref, w_ref, b_ref, o_ref):
        t_idx = lax.broadcasted_iota(jnp.int32, (n_rows, 1), 0)
        pre = _conv_pre(x_ref[...], w_ref[...], b_ref[...], t_idx, n_rows, m_ctx)[0]
        o_ref[...] = pre * jax.nn.sigmoid(pre)

    return pl.pallas_call(
        body, name="conv_fwd", grid=(width // LANES,),
        in_specs=[pl.BlockSpec((n_rows, LANES), lambda c: (0, c0 + c)),
                  pl.BlockSpec((3, LANES), lambda c: (0, c)), pl.BlockSpec((1, LANES), lambda c: (0, c))],
        out_specs=pl.BlockSpec((n_rows, LANES), lambda c: (0, c)),
        out_shape=jax.ShapeDtypeStruct((n_rows, width), F32),
        compiler_params=_cp(("parallel",)))(x.arr, conv_w, conv_b)


def conv_bwd(x, conv_w, conv_b, dy_a, dy_b, dxs_extra, m_ctx):
    n_rows, width = x.arr.shape[0], x.width
    c0 = x.off // LANES
    n_extra = dxs_extra.shape[1] // LANES

    def body(x_ref, w_ref, b_ref, dya_ref, dyb_ref, ex_ref, dx_ref, dw_ref, db_ref):
        c = pl.program_id(0)
        t_idx = lax.broadcasted_iota(jnp.int32, (n_rows, 1), 0)
        w = w_ref[...]
        pre, xp, xn, is_start, is_end = _conv_pre(x_ref[...], w, b_ref[...], t_idx, n_rows, m_ctx)
        sg = jax.nn.sigmoid(pre)
        dyv = dya_ref[...] + dyb_ref[...] + jnp.where(c < n_extra, ex_ref[...], 0.0)
        dpre = dyv * (sg * (1.0 + pre * (1.0 - sg)))
        d_next = jnp.where(is_end, 0.0, pltpu.roll(dpre, n_rows - 1, axis=0))
        d_prev = jnp.where(is_start, 0.0, pltpu.roll(dpre, 1, axis=0))
        dx_ref[...] = (w[1:2] * dpre + w[0:1] * d_next + w[2:3] * d_prev).astype(dx_ref.dtype)
        dw_ref[...] = jnp.concatenate([jnp.sum(dpre * xp, axis=0, keepdims=True),
                                       jnp.sum(dpre * x_ref[...], axis=0, keepdims=True),
                                       jnp.sum(dpre * xn, axis=0, keepdims=True)], axis=0)
        db_ref[...] = jnp.sum(dpre, axis=0, keepdims=True)

    return pl.pallas_call(
        body, name="conv_bwd", grid=(width // LANES,),
        in_specs=[pl.BlockSpec((n_rows, LANES), lambda c: (0, c0 + c)),
                  pl.BlockSpec((3, LANES), lambda c: (0, c)), pl.BlockSpec((1, LANES), lambda c: (0, c)),
                  pl.BlockSpec((n_rows, LANES), lambda c: (0, c)), pl.BlockSpec((n_rows, LANES), lambda c: (0, c)),
                  pl.BlockSpec((n_rows, LANES), lambda c: (0, jnp.minimum(c, n_extra - 1)))],
        out_specs=[pl.BlockSpec((n_rows, LANES), lambda c: (0, c)),
                   pl.BlockSpec((3, LANES), lambda c: (0, c)), pl.BlockSpec((1, LANES), lambda c: (0, c))],
        out_shape=[jax.ShapeDtypeStruct((n_rows, width), CDT), jax.ShapeDtypeStruct((3, width), F32),
                   jax.ShapeDtypeStruct((1, width), F32)],
        compiler_params=_cp(("parallel",)))(x.arr, conv_w, conv_b, dy_a, dy_b, dxs_extra)


def loss_head(h, target, final_w, m_ctx):
    n_rows, d = h.shape
    tm = min(ROW_TILE, n_rows)
    nb_ctx = m_ctx // tm

    def f(hb, w, tgt):
        err = _rms(hb, w) - tgt
        return 0.5 * jnp.sum(jnp.mean(err * err, axis=-1))

    def body(h_ref, t_ref, w_ref, loss_ref, dh_ref, dw_ref):
        i = pl.program_id(0)

        @pl.when(i < nb_ctx)
        def _():
            dh_ref[...] = jnp.zeros_like(dh_ref)

        @pl.when(i == 0)
        def _():
            loss_ref[...] = jnp.zeros_like(loss_ref)
            dw_ref[...] = jnp.zeros_like(dw_ref)

        @pl.when(i >= nb_ctx)
        def _():
            val, vjp = jax.vjp(lambda hb, w: f(hb, w, t_ref[...]), h_ref[...], w_ref[...])
            dh, dw = vjp(jnp.ones((), F32))
            dh_ref[...] = dh
            dw_ref[...] += dw
            loss_ref[...] += jnp.broadcast_to(val, loss_ref.shape)

    return pl.pallas_call(
        body, name="loss_head", grid=(n_rows // tm,),
        in_specs=[pl.BlockSpec((tm, d), lambda i: (i, 0)),
                  pl.BlockSpec((tm, d), lambda i: (jnp.maximum(i - nb_ctx, 0), 0)), _const_spec((1, d))],
        out_specs=[_const_spec((1, LANES)), pl.BlockSpec((tm, d), lambda i: (i, 0)), _const_spec((1, d))],
        out_shape=[jax.ShapeDtypeStruct((1, LANES), F32), jax.ShapeDtypeStruct((n_rows, d), F32),
                   jax.ShapeDtypeStruct((1, d), F32)],
        compiler_params=_cp(("arbitrary",)))(h, target, final_w)


def adamw(name, w, m, v, g_parts):
    lead, rows, cols = w.shape
    tr = _pick(rows, (256, 128, 64, 32, 16, 8))
    npart = len(g_parts)
    c1 = 1.0 - ADAM_B1 ** ADAM_STEP
    c2 = 1.0 - ADAM_B2 ** ADAM_STEP

    def body(*refs):
        w_ref, m_ref, v_ref = refs[:3]
        g = refs[3][...].astype(F32)
        for r in refs[4:3 + npart]:
            g = g + r[...].astype(F32)
        g_ref, d_ref, nm_ref, nv_ref = refs[3 + npart:]
        nm = ADAM_B1 * m_ref[...] + (1.0 - ADAM_B1) * g
        nv = ADAM_B2 * v_ref[...] + (1.0 - ADAM_B2) * (g * g)
        g_ref[...] = g
        nm_ref[...] = nm
        nv_ref[...] = nv
        d_ref[...] = -ADAM_LR * ((nm / c1) / (jnp.sqrt(nv / c2) + ADAM_EPS) + ADAM_WD * w_ref[...])

    spec = pl.BlockSpec((None, tr, cols), lambda l, i: (l, i, 0))
    return pl.pallas_call(
        body, name=name, grid=(lead, rows // tr), in_specs=[spec] * (3 + npart), out_specs=[spec] * 4,
        out_shape=[jax.ShapeDtypeStruct(w.shape, F32)] * 4,
        compiler_params=_cp(("parallel", "parallel")))(w, m, v, *g_parts)


MESH = pl.DeviceIdType.MESH
_HBM = pl.BlockSpec(memory_space=pl.ANY)


def _chip_peers():
    x, y, c = lax.axis_index("x"), lax.axis_index("y"), lax.axis_index("c")
    return x, y, c, [(1 - x, y), (x, 1 - y), (1 - x, 1 - y)]


def _window(ref, kind, chip, rows, cols):
    if kind == "cols":
        return ref.at[:, pl.ds(pl.multiple_of(chip * cols, LANES), cols)]
    if kind == "rows":
        return ref.at[pl.ds(pl.multiple_of(chip * rows, 8), rows), :]
    return ref.at[chip]


def _gathered_shape(kind, rows, cols):
    return {"cols": (rows, 4 * cols), "rows": (4 * rows, cols), "slices": (4, rows, cols)}[kind]


def gather_layers(name, shards, kinds):
    n = len(shards)

    def body(*refs):
        x_refs, o_refs = refs[:n], refs[n:2 * n]
        send_sems, recv_sems, local_sems = refs[2 * n:]
        x, y, c, peers = _chip_peers()
        me = 2 * x + y
        started = []
        for a in range(n):
            _, rows, cols = shards[a].shape
            src = x_refs[a].at[c]
            mine = pltpu.make_async_copy(src, _window(o_refs[a], kinds[a], me, rows, cols), local_sems.at[a])
            mine.start()
            started.append(mine.wait)
            for k, (px, py) in enumerate(peers):
                cp = pltpu.make_async_remote_copy(
                    src_ref=src, dst_ref=_window(o_refs[a], kinds[a], me, rows, cols), send_sem=send_sems.at[3 * a + k],
                    recv_sem=recv_sems.at[3 * a + k], device_id=(px, py, c), device_id_type=MESH)
                cp.start()
                started.append(cp.wait_send)
        for a in range(n):
            _, rows, cols = shards[a].shape
            for k, (px, py) in enumerate(peers):
                pltpu.make_async_remote_copy(
                    src_ref=x_refs[a].at[c], dst_ref=_window(o_refs[a], kinds[a], 2 * px + py, rows, cols),
                    send_sem=send_sems.at[3 * a + k], recv_sem=recv_sems.at[3 * a + k], device_id=(px, py, c),
                    device_id_type=MESH).wait_recv()
        for wait in started:
            wait()

    return pl.pallas_call(
        body, name=name, in_specs=[_HBM] * n, out_specs=[_HBM] * n,
        out_shape=[jax.ShapeDtypeStruct(_gathered_shape(kinds[a], *shards[a].shape[1:]), shards[a].dtype)
                   for a in range(n)],
        scratch_shapes=[pltpu.SemaphoreType.DMA((3 * n,)), pltpu.SemaphoreType.DMA((3 * n,)),
                        pltpu.SemaphoreType.DMA((n,))],
        )(*shards)


def scatter_pieces(name, pieces):
    n = len(pieces)

    def body(*refs):
        p_refs, o_refs = refs[:n], refs[n:2 * n]
        send_sems, recv_sems, local_sems = refs[2 * n:]
        x, y, c, peers = _chip_peers()
        me = 2 * x + y
        started = []
        for a in range(n):
            mine = pltpu.make_async_copy(p_refs[a].at[me], o_refs[a].at[me], local_sems.at[a])
            mine.start()
            started.append(mine.wait)
            for k, (px, py) in enumerate(peers):
                cp = pltpu.make_async_remote_copy(
                    src_ref=p_refs[a].at[2 * px + py], dst_ref=o_refs[a].at[me], send_sem=send_sems.at[3 * a + k],
                    recv_sem=recv_sems.at[3 * a + k], device_id=(px, py, c), device_id_type=MESH)
                cp.start()
                started.append(cp.wait_send)
        for a in range(n):
            for k, (px, py) in enumerate(peers):
                pltpu.make_async_remote_copy(
                    src_ref=p_refs[a].at[me], dst_ref=o_refs[a].at[2 * px + py], send_sem=send_sems.at[3 * a + k],
                    recv_sem=recv_sems.at[3 * a + k], device_id=(px, py, c), device_id_type=MESH).wait_recv()
        for wait in started:
            wait()

    return pl.pallas_call(
        body, name=name, in_specs=[_HBM] * n, out_specs=[_HBM] * n,
        out_shape=[jax.ShapeDtypeStruct(p.shape, p.dtype) for p in pieces],
        scratch_shapes=[pltpu.SemaphoreType.DMA((3 * n,)), pltpu.SemaphoreType.DMA((3 * n,)),
                        pltpu.SemaphoreType.DMA((n,))],
        )(*pieces)


def _pair_step(n_steps, x_ref, land, send_sems, recv_sems, credits, consume):
    x, y, c = lax.axis_index("x"), lax.axis_index("y"), lax.axis_index("c")
    sib = (x, y, 1 - c)
    i = pl.program_id(0)
    slot = i % 2

    @pl.when(i >= 2)
    def _():
        pl.semaphore_wait(credits.at[slot], 1)

    cp = pltpu.make_async_remote_copy(src_ref=x_ref, dst_ref=land.at[slot], send_sem=send_sems.at[slot],
                                      recv_sem=recv_sems.at[slot], device_id=sib, device_id_type=MESH)
    cp.start()
    cp.wait_recv()
    consume(land[slot])

    @pl.when(i < n_steps - 2)
    def _():
        pl.semaphore_signal(credits.at[slot], inc=1, device_id=sib, device_id_type=MESH)

    cp.wait_send()


def _pair_call(name, body, n_steps, in_specs, out_spec, out_shape, blk_shape, dtype, operands, extra_scratch=()):
    grid_spec = pltpu.PrefetchScalarGridSpec(
        num_scalar_prefetch=1, grid=(n_steps,), in_specs=in_specs, out_specs=out_spec,
        scratch_shapes=[pltpu.VMEM((2,) + blk_shape, dtype), pltpu.SemaphoreType.DMA((2,)),
                        pltpu.SemaphoreType.DMA((2,)), pltpu.SemaphoreType.REGULAR((2,)), *extra_scratch])
    return pl.pallas_call(body, name=name, grid_spec=grid_spec, out_shape=out_shape,
                          compiler_params=_cp(("arbitrary",)))(*operands)


def _place():
    return jnp.stack([lax.axis_index("x"), lax.axis_index("y"), lax.axis_index("c")]).astype(jnp.int32)


def _pair_rows(rows, row_bytes):
    for cand in (4096, 2048, 1024, 768, 512, 384, 256, 192, 128, 96, 64, 48, 32, 16):
        if rows % cand == 0 and cand * row_bytes <= PAIR_BLOCK_BYTES:
            return cand
    return _pick(rows, (16, 8))


def exchange_both(name, mine):
    rows, cols = mine.shape
    tr = _pair_rows(rows, cols * mine.dtype.itemsize)
    n_steps = rows // tr

    def body(s_ref, x_ref, o_ref, land, send_sems, recv_sems, credits):
        c = lax.axis_index("c")
        o_ref[c] = x_ref[...]

        def consume(v):
            o_ref[1 - c] = v
        _pair_step(n_steps, x_ref, land, send_sems, recv_sems, credits, consume)

    return _pair_call(name, body, n_steps, [pl.BlockSpec((tr, cols), lambda i, s: (i, 0))],
                      pl.BlockSpec((2, tr, cols), lambda i, s: (0, i, 0)),
                      jax.ShapeDtypeStruct((2, rows, cols), mine.dtype), (tr, cols), mine.dtype, (_place(), mine))


def exchange_add(name, layer0, layer1):
    rows, cols = layer0.shape
    tr = _pair_rows(rows, cols * layer0.dtype.itemsize)
    nb = rows // tr

    def body(s_ref, l0_ref, l1_ref, o_ref, land, send_sems, recv_sems, credits, send_buf):
        first = lax.axis_index("c") == 0
        send_buf[...] = jnp.where(first, l1_ref[...], l0_ref[...])
        mine = jnp.where(first, l0_ref[...], l1_ref[...]).astype(F32)

        def consume(v):
            o_ref[...] = (mine + v.astype(F32)).astype(o_ref.dtype)
        _pair_step(nb, send_buf, land, send_sems, recv_sems, credits, consume)

    spec = pl.BlockSpec((tr, cols), lambda i, s: (i, 0))
    return _pair_call(name, body, nb, [spec, spec], spec, jax.ShapeDtypeStruct((rows, cols), CDT),
                      (tr, cols), layer0.dtype, (_place(), layer0, layer1),
                      extra_scratch=(pltpu.VMEM((tr, cols), layer0.dtype),))


def sum_exchange(name, parts):
    npart, rows, cols = parts.shape
    tr = _pair_rows(rows, cols * 4)
    n_steps = rows // tr

    def body(s_ref, x_ref, o_ref, land, send_sems, recv_sems, credits, mine):
        c = lax.axis_index("c")
        acc = x_ref[0].astype(F32)
        for k in range(1, npart):
            acc = acc + x_ref[k].astype(F32)
        mine[...] = acc
        o_ref[c] = acc

        def consume(v):
            o_ref[1 - c] = v
        _pair_step(n_steps, mine, land, send_sems, recv_sems, credits, consume)

    return _pair_call(name, body, n_steps, [pl.BlockSpec((npart, tr, cols), lambda i, s: (0, i, 0))],
                      pl.BlockSpec((2, tr, cols), lambda i, s: (0, i, 0)),
                      jax.ShapeDtypeStruct((2, rows, cols), F32), (tr, cols), F32, (_place(), parts),
                      extra_scratch=(pltpu.VMEM((tr, cols), F32),))


def allreduce_small(name, buf):
    rows = buf.shape[0]

    def body(x_ref, out_ref, gath, send_sems, recv_sems):
        x, y, c = lax.axis_index("x"), lax.axis_index("y"), lax.axis_index("c")
        me = 4 * x + 2 * y + c
        masks = [(k >> 2 & 1, k >> 1 & 1, k & 1) for k in range(1, 8)]

        def flip(v, bit):
            return 1 - v if bit else v

        sends = []
        for k, (bx, by, bc) in enumerate(masks):
            cp = pltpu.make_async_remote_copy(src_ref=x_ref, dst_ref=gath.at[me], send_sem=send_sems.at[k],
                                              recv_sem=recv_sems.at[k],
                                              device_id=(flip(x, bx), flip(y, by), flip(c, bc)), device_id_type=MESH)
            cp.start()
            sends.append(cp)
        gath[me] = x_ref[...]
        for k, (bx, by, bc) in enumerate(masks):
            px, py, pc = flip(x, bx), flip(y, by), flip(c, bc)
            pltpu.make_async_remote_copy(src_ref=x_ref, dst_ref=gath.at[4 * px + 2 * py + pc],
                                         send_sem=send_sems.at[k], recv_sem=recv_sems.at[k],
                                         device_id=(px, py, pc), device_id_type=MESH).wait_recv()
        for cp in sends:
            cp.wait_send()
        acc = gath[0]
        for d in range(1, 8):
            acc = acc + gath[d]
        out_ref[...] = acc

    return pl.pallas_call(
        body, name=name, in_specs=[pl.BlockSpec(memory_space=pltpu.VMEM)],
        out_specs=pl.BlockSpec(memory_space=pltpu.VMEM), out_shape=jax.ShapeDtypeStruct(buf.shape, F32),
        scratch_shapes=[pltpu.VMEM((8, rows, LANES), F32), pltpu.SemaphoreType.DMA((7,)),
                        pltpu.SemaphoreType.DMA((7,))],
        )(buf)


def _pack_flat(arrs, dtype, width, row_mult=8):
    flat = jnp.concatenate([a.reshape(-1).astype(dtype) for a in arrs])
    pad = (-flat.shape[0]) % (row_mult * width)
    if pad:
        flat = jnp.concatenate([flat, jnp.zeros((pad,), dtype)])
    return flat.reshape(-1, width)


def _unpack_flat(buf, shapes):
    flat = buf.reshape(-1)
    out, off = [], 0
    for s in shapes:
        n = math.prod(s)
        out.append(flat[off:off + n].reshape(s))
        off += n
    return out


def _in_to_padded(w):
    parts = []
    for name in IN_NEW_ORDER:
        _, width, o_off, o_w = IN_LAYOUT[name]
        parts.append(w[..., o_off:o_off + o_w])
        if o_w < width:
            parts.append(jnp.zeros(w.shape[:-1] + (width - o_w,), w.dtype))
    used = sum(IN_LAYOUT[n][1] for n in IN_NEW_ORDER)
    parts.append(jnp.zeros(w.shape[:-1] + (IN_PAD - used,), w.dtype))
    return jnp.concatenate(parts, axis=-1)


def _in_from_padded(g):
    parts = []
    for name in IN_ORIG_ORDER:
        off, _, _, o_w = IN_LAYOUT[name]
        parts.append(g[..., off:off + o_w])
    return jnp.concatenate(parts, axis=-1)


def _pcol(p, name):
    off, width, _, _ = IN_LAYOUT[name]
    return Cols(p, off, width)


def _lane_pad(v, width=LANES):
    v = v.reshape(-1)
    return jnp.concatenate([v, jnp.zeros((width - v.shape[0],), v.dtype)]).reshape(1, width)


def _f_norm_mod(h, sh, sc, w):
    return (_rms(h, w) * (1.0 + sc) + sh,)


def _f_norm_mod_thru(h, sh, sc, w):
    return h, _rms(h, w) * (1.0 + sc) + sh


def _f_attn_prep(qraw, kraw, vraw, cos2, sin2, qw, kw, gq, gk):
    q = qraw * lax.rsqrt(_group_mean(qraw * qraw, gq) + NORM_EPS) * qw
    q = _rope32(q, jnp.tile(cos2, (1, 4)), jnp.tile(sin2, (1, 4))) * (ATTN_HD ** -0.5 * LOG2E)
    k = kraw * lax.rsqrt(_group_mean(kraw * kraw, gk) + NORM_EPS) * kw
    return q, _rope32(k, cos2, sin2), vraw


def _f_ssd_finish(yf, yb, xs, z, d_exp, nw):
    y = (yf + yb + d_exp * xs) * (z * jax.nn.sigmoid(z))
    return (_rms(y, nw),)


def _f_ret_prep(rq, rk, cos1, sin1):
    cos_full, sin_signed = jnp.tile(cos1, (1, 4)), jnp.tile(sin1, (1, 4))
    return _rope64(rq, cos_full, sin_signed), _rope64(rk, cos_full, sin_signed) * (RET_DK ** -0.5)


def _f_ret_finish(yf, yb, g, gw):
    y = yf + yb
    outs = []
    for h in range(RET_HEADS):
        yh = y[:, h * RET_DK:(h + 1) * RET_DK]
        yc = yh - jnp.mean(yh, axis=-1, keepdims=True)
        outs.append(yc * lax.rsqrt(jnp.mean(yc * yc, axis=-1, keepdims=True) + NORM_EPS))
    return (jnp.concatenate(outs, axis=1) * gw * (g * jax.nn.sigmoid(g)),)


def _f_merge(p0, p1, p2, g0, g1, g2):
    return (jax.nn.sigmoid(g0) * p0 + jax.nn.sigmoid(g1) * p1 + jax.nn.sigmoid(g2) * p2,)


def _f_mid(h, mix, g1, sh2, sc2, w2):
    h_mid = h + g1 * mix
    return h_mid, _rms(h_mid, w2) * (1.0 + sc2) + sh2


def _epi_sqrelu(a):
    r = jnp.maximum(a, 0.0)
    return a, r * r


def _epi_sqrelu_bwd(dhh, a):
    return (dhh * (2.0 * jnp.maximum(a.astype(F32), 0.0)),)


def _f_residual(h_mid, o, g2):
    return (h_mid + g2 * o,)


def _f_silu(x):
    return (x * jax.nn.sigmoid(x),)


def _f_bias(x, b):
    return (x + b,)


def _ssd_rows(xbc, p):
    rows = [Cols(xbc, LANES * k, LANES) for k in range(4)]
    rows += [Cols(xbc, 512 + LANES * g, LANES) for g in range(2)]
    rows += [Cols(xbc, 768 + LANES * g, LANES) for g in range(2)]
    return rows + [_pcol(p, "dt")]


def _ret_rows(rq, rk, p):
    off_v = IN_LAYOUT["rv"][0]
    return ([Cols(rq, LANES * h, LANES) for h in range(4)] + [Cols(rk, LANES * h, LANES) for h in range(4)]
            + [Cols(p, off_v + LANES * h, LANES) for h in range(4)])


def layer_fwd(li, h, mod, lw, tabs, m_ctx):
    t = h.shape[0]
    nb = m_ctx // min(ROW_TILE, t)
    sh1, sc1, g1, sh2, sc2, g2 = mod
    nm = lambda s: f"l{li}_{s}"
    sv = {}
    (u,) = rowwise_fwd(nm("norm1"), _f_norm_mod, [h], [sh1, sc1], [lw["norm1_w"]], [(D_MODEL, CDT)], t, nb)
    p = mm(nm("in_proj"), u, lw["w_in"], F32)
    q, k, v = rowwise_fwd(
        nm("attn_prep"), _f_attn_prep,
        [_pcol(p, "q"), _pcol(p, "k"), _pcol(p, "v"), tabs["ca"], tabs["sa"]], [],
        [lw["qw"], lw["kw"], tabs["gq"], tabs["gk"]], [(512, CDT), (128, CDT), (128, CDT)], t, nb)
    tq = min(ATTN_TQ, m_ctx)
    kk, vv = _split_kv(k), _split_kv(v)
    ones_rows = jnp.concatenate([jnp.ones((2, 1, t), CDT), jnp.zeros((2, ATTN_ONES_ROWS - 1, t), CDT)], axis=1)
    attn_o, qT, oT, lse = attn_fwd(nm("attn"), q, kk, jnp.concatenate([vv.transpose(0, 2, 1), ones_rows], axis=1),
                                   m_ctx, tq)

    xbc = conv_fwd(_pcol(p, "xbc"), lw["conv_w"], lw["conv_b"], m_ctx)
    ssd_sh = [lw["dt_bias"], lw["a_log"]]
    yf, yb, sf, sb = scan_fwd(nm("ssd"), _make_ssd_chunk, _ssd_rows(xbc, p), ssd_sh, 4, 512, t, m_ctx)
    (ssd_o,) = rowwise_fwd(nm("ssd_fin"), _f_ssd_finish, [yf, yb, Cols(xbc, 0, 512), _pcol(p, "z")], [],
                           [lw["d_exp"], lw["ssd_nw"]], [(512, CDT)], t, nb)

    rq, rk = rowwise_fwd(nm("ret_prep"), _f_ret_prep, [_pcol(p, "rq"), _pcol(p, "rk"), tabs["rc"], tabs["rs"]],
                         [], [], [(512, F32), (512, F32)], t, nb)
    rf, rb, rsf, rsb = scan_fwd(nm("ret"), _make_ret_chunk, _ret_rows(rq, rk, p), [lw["ret_lg"]], 4, 512, t, m_ctx)
    (ret_o,) = rowwise_fwd(nm("ret_fin"), _f_ret_finish, [rf, rb, _pcol(p, "rg")], [], [lw["ret_gw"]],
                           [(512, CDT)], t, nb)

    pbs = [mm(nm(f"branch{b}"), br, lw["w_branch"][b], CDT) for b, br in enumerate((attn_o, ssd_o, ret_o))]
    gl = [Cols(p, 1024 * b, 1024) for b in range(3)]
    (merged,) = rowwise_fwd(nm("merge"), _f_merge, pbs + gl, [], [], [(D_MODEL, CDT)], t, nb)
    mix = mm(nm("out_proj"), merged, lw["w_out"], F32)
    h_mid, vv2 = rowwise_fwd(nm("mid"), _f_mid, [h, mix], [g1, sh2, sc2], [lw["norm2_w"]],
                             [(D_MODEL, F32), (D_MODEL, CDT)], t, nb)
    a, hh = mm(nm("mlp1"), vv2, lw["w_mlp1"], None, epilogue=_epi_sqrelu, out_dtypes=[CDT, CDT])
    o = mm(nm("mlp2"), hh, lw["w_mlp2"], F32)
    (h_out,) = rowwise_fwd(nm("resid"), _f_residual, [h_mid, o], [g2], [], [(D_MODEL, F32)], t, nb)
    sv.update(h=h, u=u, p=p, qT=qT, kk=kk, vv=vv, oT=oT, lse=lse, attn_o=attn_o, xbc=xbc, yf=yf, yb=yb,
              sf=sf, sb=sb, ssd_o=ssd_o, rq=rq, rk=rk, rf=rf, rb=rb, rsf=rsf, rsb=rsb, ret_o=ret_o, pbs=pbs,
              merged=merged, mix=mix, h_mid=h_mid, v=vv2, a=a, hh=hh, o=o)
    return h_out, sv


def layer_bwd(li, dh_out, sv, mod, lw, tabs, m_ctx):
    t = dh_out.shape[0]
    nb = m_ctx // min(ROW_TILE, t)
    sh1, sc1, g1, sh2, sc2, g2 = mod
    nm = lambda s: f"l{li}_{s}_bwd"
    gw = {}
    p = sv["p"]
    (do,), (dg2,), _ = rowwise_bwd(nm("resid"), _f_residual, [sv["h_mid"], sv["o"]], [g2], [], [dh_out],
                                   [False, True], [], [CDT], t, nb)
    (da,) = mm(nm("mlp2_dx"), do, lw["w_mlp2"], None, transpose_b=True, epilogue=_epi_sqrelu_bwd,
               tile_ins=[sv["a"]], out_dtypes=[CDT])
    gw["w_mlp2"] = mm_tn(nm("mlp2_dw"), sv["hh"], do, CDT, ("rows", lw["w_mlp2"].shape[0] // 4))
    dv = mm(nm("mlp1_dx"), da, lw["w_mlp1"], F32, transpose_b=True)
    gw["w_mlp1"] = mm_tn(nm("mlp1_dw"), sv["v"], da, CDT, ("cols", lw["w_mlp1"].shape[1] // 4))
    (dh_a, dmix), (dg1, dsh2, dsc2), (gw["norm2_w"],) = rowwise_bwd(
        nm("mid"), _f_mid, [sv["h"], sv["mix"]], [g1, sh2, sc2], [lw["norm2_w"]], [dh_out, dv],
        [True, True], [True], [F32, CDT], t, nb)
    dmerged = mm(nm("out_dx"), dmix, lw["w_out"], CDT, transpose_b=True)
    gw["w_out"] = mm_tn(nm("out_dw"), sv["merged"], dmix, CDT, ("rows", lw["w_out"].shape[0] // 4))
    gl = [Cols(p, 1024 * b, 1024) for b in range(3)]
    dmg, _, _ = rowwise_bwd(nm("merge"), _f_merge, sv["pbs"] + gl, [], [], [dmerged], [True] * 6, [], [CDT] * 6,
                            t, nb)
    dpb, dgl = dmg[:3], dmg[3:]
    brs = (sv["attn_o"], sv["ssd_o"], sv["ret_o"])
    d_attn_o = mm(nm("branch0_dx"), dpb[0], lw["w_branch"][0], CDT, transpose_b=True)
    d_ssd_o = mm(nm("branch1_dx"), dpb[1], lw["w_branch"][1], F32, transpose_b=True)
    d_ret_o = mm(nm("branch2_dx"), dpb[2], lw["w_branch"][2], F32, transpose_b=True)
    n_loc = lw["w_branch"].shape[2] // 4
    gw["w_branch"] = jnp.stack([mm_tn(nm(f"branch{b}_dw"), brs[b], dpb[b], CDT, ("cols", n_loc)) for b in range(3)],
                               axis=1).reshape(4, -1, n_loc)
    tq = min(ATTN_TQ, m_ctx)
    dq_rows, dk_s, dv_s = attn_bwd(nm("attn"), sv["qT"], d_attn_o, sv["oT"], sv["lse"], sv["kk"],
                                   sv["kk"].transpose(0, 2, 1), sv["vv"], m_ctx)
    (dq_raw, dk_raw, dv_raw), _, (gw["qw"], gw["kw"]) = rowwise_bwd(
        nm("attn_prep"), _f_attn_prep,
        [_pcol(p, "q"), _pcol(p, "k"), _pcol(p, "v"), tabs["ca"], tabs["sa"]], [],
        [lw["qw"], lw["kw"], tabs["gq"], tabs["gk"]],
        [dq_rows, _merge_kv(dk_s) * LN2, _merge_kv(dv_s)],
        [True, True, True, False, False], [True, True, False, False], [CDT] * 3, t, nb)
    (dy_ssd, dxs_fin, dz), _, (gw["d_exp"], gw["ssd_nw"]) = rowwise_bwd(
        nm("ssd_fin"), _f_ssd_finish, [sv["yf"], sv["yb"], Cols(sv["xbc"], 0, 512), _pcol(p, "z")], [],
        [lw["d_exp"], lw["ssd_nw"]], [d_ssd_o], [True, False, True, True], [True, True], [F32, F32, CDT], t, nb)
    ssd_sh = [lw["dt_bias"], lw["a_log"]]
    post_ssd = lambda d: [jnp.concatenate(d[0:8], axis=1), d[8]]
    (dxbc_f, ddt_f), dsh_f = scan_bwd(nm("ssd_f"), _make_ssd_chunk, _ssd_rows(sv["xbc"], p), ssd_sh, (sv["sf"],),
                                      dy_ssd, post_ssd, [(1024, F32), (LANES, F32)], 4, t, m_ctx, dirs=(0,))
    (dxbc_b, ddt_b), dsh_b = scan_bwd(nm("ssd_b"), _make_ssd_chunk, _ssd_rows(sv["xbc"], p), ssd_sh, (sv["sb"],),
                                      dy_ssd, post_ssd, [(1024, F32), (LANES, F32)], 4, t, m_ctx, dirs=(1,))
    gw["dt_bias"], gw["a_log"] = dsh_f[0] + dsh_b[0], dsh_f[1] + dsh_b[1]
    ddt = (ddt_f + ddt_b).astype(CDT)
    dxbc_raw, gw["conv_w"], gw["conv_b"] = conv_bwd(_pcol(p, "xbc"), lw["conv_w"], lw["conv_b"], dxbc_f, dxbc_b,
                                                    dxs_fin, m_ctx)
    (dy_ret, drg), _, (gw["ret_gw"],) = rowwise_bwd(
        nm("ret_fin"), _f_ret_finish, [sv["rf"], sv["rb"], _pcol(p, "rg")], [], [lw["ret_gw"]], [d_ret_o],
        [True, False, True], [True], [F32, CDT], t, nb)
    post_ret = lambda d: [jnp.concatenate(d[0:4], axis=1), jnp.concatenate(d[4:8], axis=1),
                          jnp.concatenate(d[8:12], axis=1)]
    rrows = _ret_rows(sv["rq"], sv["rk"], p)
    (dq_f, dk_f, dv_f), (dq_b, dk_b, dv_b), (gw["ret_lg"],) = scan_bwd(
        nm("ret"), _make_ret_chunk, rrows, [lw["ret_lg"]], (sv["rsf"], sv["rsb"]), dy_ret, post_ret,
        [(512, F32)] * 3, 4, t, m_ctx)
    drv = (dv_f + dv_b).astype(CDT)
    (drq, drk), _, _ = rowwise_bwd(nm("ret_prep"), _f_ret_prep,
                                   [_pcol(p, "rq"), _pcol(p, "rk"), tabs["rc"], tabs["rs"]], [], [],
                                   [(dq_f, dq_b), (dk_f, dk_b)], [True, True, False, False], [], [CDT, CDT], t, nb)
    pieces = {"gates": None, "xbc": dxbc_raw, "q": dq_raw, "z": dz, "rq": drq, "rk": drk, "rv": drv, "rg": drg,
              "k": dk_raw, "v": dv_raw, "dt": ddt}
    cols = list(dgl) + [pieces[n] for n in IN_NEW_ORDER[1:]]
    used = sum(c.shape[1] for c in cols)
    cols.append(jnp.zeros((t, IN_PAD - used), CDT))
    dp = jnp.concatenate(cols, axis=1)
    du = mm(nm("in_dx"), dp, lw["w_in"], F32, transpose_b=True)
    gw["w_in"] = mm_tn(nm("in_dw"), sv["u"], dp, CDT)
    (dh_in,), (dsh1, dsc1), (gw["norm1_w"],) = rowwise_bwd(
        nm("norm1"), _f_norm_mod_thru, [sv["h"]], [sh1, sc1], [lw["norm1_w"]], [dh_a, du], [True], [True], [F32],
        t, nb)
    return dh_in, [dsh1, dsc1, dg1, dsh2, dsc2, dg2], gw


def _rope_tables(n_lat, m_ctx):
    rows = n_lat // GRID_W
    row = jnp.repeat(jnp.arange(rows, dtype=F32), GRID_W)
    col = jnp.tile(jnp.arange(GRID_W, dtype=F32), rows)
    nfreq = ATTN_HD // 4
    inv = ROPE_THETA ** (-jnp.arange(nfreq, dtype=F32) / nfreq)
    ang = jnp.concatenate([row[:, None] * inv, col[:, None] * inv], axis=-1)
    cos = jnp.concatenate([jnp.ones((m_ctx, ATTN_HD // 2), F32), jnp.cos(ang)], axis=0)
    sin = jnp.concatenate([jnp.zeros((m_ctx, ATTN_HD // 2), F32), jnp.sin(ang)], axis=0)
    c64 = jnp.concatenate([cos, cos], axis=1)
    s64 = jnp.concatenate([-sin, sin], axis=1)
    pos = jnp.arange(m_ctx + n_lat, dtype=F32)
    inv_r = ROPE_THETA ** (-jnp.linspace(0.0, 1.0, RET_DK // 2, dtype=F32))
    ang_r = pos[:, None] * inv_r
    rc = jnp.concatenate([jnp.cos(ang_r)] * 2, axis=1)
    rs = jnp.concatenate([-jnp.sin(ang_r), jnp.sin(ang_r)], axis=1)
    return dict(ca=jnp.tile(c64, (1, 2)), sa=jnp.tile(s64, (1, 2)), rc=rc, rs=rs, gq=_group_matrix(512, ATTN_HD),
                gk=_group_matrix(128, ATTN_HD))


def _layer_weights(full, small, layer):
    return dict(
        w_in=full["w_in"][layer], w_branch=full["w_branch"][layer], w_out=full["w_out"][layer],
        w_mlp1=full["w_mlp1"][layer], w_mlp2=full["w_mlp2"][layer],
        norm1_w=small["norm1_w"][layer][None], norm2_w=small["norm2_w"][layer][None],
        qw=jnp.tile(small["attn_q_norm"][layer], 8)[None], kw=jnp.tile(small["attn_k_norm"][layer], 2)[None],
        conv_w=small["ssd_conv_w"][layer], conv_b=small["ssd_conv_b"][layer][None],
        dt_bias=_lane_pad(small["ssd_dt_bias"][layer]), a_log=_lane_pad(small["ssd_a_log"][layer]),
        d_exp=jnp.repeat(small["ssd_d"][layer], SSD_HD)[None], ssd_nw=small["ssd_norm_w"][layer][None],
        ret_lg=_lane_pad(small["ret_log_decay"][layer]), ret_gw=small["ret_gn_w"][layer][None])


def local_step(x, c, ctx, full, small, loss_target):
    n_lat, d = x.shape
    m_ctx = ctx.shape[0]
    t = n_lat + m_ctx
    depth = small["norm1_w"].shape[0]
    tabs = _rope_tables(n_lat, m_ctx)
    h = jnp.concatenate([ctx, x], axis=0)
    cc = jnp.concatenate([small["c_ctx"][None], c, jnp.zeros((COND_ROWS - 2, d), F32)], axis=0)
    (scc,) = rowwise_fwd("cond_silu", _f_silu, [cc], [], [], [(d, CDT)], COND_ROWS, 0)
    mods, saved, lws = [], [], []
    for layer in range(depth):
        lw = _layer_weights(full, small, layer)
        mod_raw = mm(f"l{layer}_mod", scc, full["w_mod"][layer], F32)
        (mod8,) = rowwise_fwd(f"l{layer}_mod_bias", _f_bias, [mod_raw], [], [small["b_mod"][layer][None]],
                              [(6 * d, F32)], COND_ROWS, 0)
        mod = [mod8[0:2, k * d:(k + 1) * d].reshape(2, 1, d) for k in range(6)]
        h, sv = layer_fwd(layer, h, mod, lw, tabs, m_ctx)
        mods.append(mod)
        saved.append(sv)
        lws.append(lw)
    loss, dh, d_final = loss_head(h, loss_target, small["final_norm_w"][None], m_ctx)

    gbig = {k: [None] * depth for k in BIG}
    gs = {k: [None] * depth for k in SMALL if k not in ("c_ctx", "final_norm_w")}
    d_scc = None
    for layer in reversed(range(depth)):
        lw = lws[layer]
        dh, dmod, gw = layer_bwd(layer, dh, saved[layer], mods[layer], lw, tabs, m_ctx)
        dmod8 = jnp.concatenate([jnp.concatenate([g_.reshape(2, d) for g_ in dmod], axis=1),
                                 jnp.zeros((COND_ROWS - 2, 6 * d), F32)], axis=0)
        (dmod_c,), _, (db_mod,) = rowwise_bwd(f"l{layer}_mod_bias_bwd", _f_bias, [dmod8], [],
                                              [small["b_mod"][layer][None]], [dmod8], [True], [True], [CDT], COND_ROWS, 0)
        gbig["w_mod"][layer] = mm_tn(f"l{layer}_mod_dw", scc, dmod_c, CDT, ("cols", 6 * d // 4))
        part = mm(f"l{layer}_mod_dx", dmod_c, full["w_mod"][layer], F32, transpose_b=True)
        d_scc = part if d_scc is None else d_scc + part
        g_in = _in_from_padded(gw["w_in"])
        gbig["w_in"][layer] = g_in.reshape(d, 4, g_in.shape[1] // 4).transpose(1, 0, 2)
        for k in ("w_branch", "w_out", "w_mlp1", "w_mlp2"):
            gbig[k][layer] = gw[k]
        gs["b_mod"][layer] = db_mod.reshape(-1)
        gs["norm1_w"][layer] = gw["norm1_w"].reshape(-1)
        gs["norm2_w"][layer] = gw["norm2_w"].reshape(-1)
        gs["attn_q_norm"][layer] = gw["qw"].reshape(8, ATTN_HD).sum(0)
        gs["attn_k_norm"][layer] = gw["kw"].reshape(2, ATTN_HD).sum(0)
        gs["ssd_conv_w"][layer] = gw["conv_w"]
        gs["ssd_conv_b"][layer] = gw["conv_b"].reshape(-1)
        gs["ssd_dt_bias"][layer] = gw["dt_bias"][0, :16].reshape(2, 8)
        gs["ssd_a_log"][layer] = gw["a_log"][0, :16].reshape(2, 8)
        gs["ssd_d"][layer] = gw["d_exp"].reshape(SSD_HEADS, SSD_HD).sum(1)
        gs["ssd_norm_w"][layer] = gw["ssd_nw"].reshape(-1)
        gs["ret_log_decay"][layer] = gw["ret_lg"][0, :8].reshape(2, 4)
        gs["ret_gn_w"][layer] = gw["ret_gw"].reshape(-1)
    (d_cc,), _, _ = rowwise_bwd("cond_silu_bwd", _f_silu, [cc], [], [], [d_scc], [True], [], [F32], COND_ROWS, 0)
    g_small = {k: jnp.stack(v) for k, v in gs.items()}
    g_small["c_ctx"] = d_cc[0]
    g_small["final_norm_w"] = d_final.reshape(-1)
    return loss, dh[m_ctx:], gbig, g_small


def kernel(x, c, ctx, c_ctx, w_mod, b_mod, norm1_w, norm2_w, w_in, attn_q_norm, attn_k_norm, ssd_conv_w, ssd_conv_b, ssd_dt_bias, ssd_a_log, ssd_d, ssd_norm_w, ret_log_decay, ret_gn_w, w_branch, w_out, w_mlp1, w_mlp2, final_norm_w, loss_target, m_c_ctx, m_w_mod, m_b_mod, m_norm1_w, m_norm2_w, m_w_in, m_attn_q_norm, m_attn_k_norm, m_ssd_conv_w, m_ssd_conv_b, m_ssd_dt_bias, m_ssd_a_log, m_ssd_d, m_ssd_norm_w, m_ret_log_decay, m_ret_gn_w, m_w_branch, m_w_out, m_w_mlp1, m_w_mlp2, m_final_norm_w, v_c_ctx, v_w_mod, v_b_mod, v_norm1_w, v_norm2_w, v_w_in, v_attn_q_norm, v_attn_k_norm, v_ssd_conv_w, v_ssd_conv_b, v_ssd_dt_bias, v_ssd_a_log, v_ssd_d, v_ssd_norm_w, v_ret_log_decay, v_ret_gn_w, v_w_branch, v_w_out, v_w_mlp1, v_w_mlp2, v_final_norm_w):
    env = dict(locals())
    w_loc = {k: env[k] for k in WEIGHTS}
    m_loc = {k: env["m_" + k] for k in WEIGHTS}
    v_loc = {k: env["v_" + k] for k in WEIGHTS}
    chip = 2 * lax.axis_index("x") + lax.axis_index("y")
    core = lax.axis_index("c")

    depth = w_loc["w_mod"].shape[0]
    assert depth == 2, "the exchanges split the layers between a chip's two cores"
    shards = [w_loc[k].astype(CDT).reshape(depth, -1, w_loc[k].shape[-1]) for k in BIG]
    mine = gather_layers("gather_weights", shards, [BIG_KIND[k] for k in BIG])
    full = {}
    for k, arr in zip(BIG, mine):
        both = exchange_both("share_" + k, arr.reshape(-1, arr.shape[-1]))
        if k == "w_in":
            both = both.reshape(depth, 4, -1, both.shape[-1]).transpose(0, 2, 1, 3)
            both = _in_to_padded(both.reshape(depth, both.shape[1], -1))
        full[k] = both.reshape((depth,) + w_loc[k].shape[1:-1] + (-1,)) if BIG_KIND[k] == "cols" else \
            both.reshape((depth,) + w_loc[k].shape[1:-2] + (-1, w_loc[k].shape[-1])) if BIG_KIND[k] == "rows" else both

    cw = w_loc["ssd_conv_w"]
    cw_w = cw.shape[-1]
    placed = lax.dynamic_update_slice(jnp.zeros(cw.shape[:-1] + (4 * cw_w,), F32),
                                      cw * (core == 0).astype(F32), (0, 0, chip * cw_w))
    conv_full = _unpack_flat(allreduce_small("gather_conv_w", _pack_flat([placed], F32, LANES)), [placed.shape])[0]
    small = {k: w_loc[k] for k in SMALL}
    small["ssd_conv_w"] = conv_full

    loss_l, grad_x, g_big, g_small = local_step(x[0], c, ctx[0], full, small, loss_target[0])

    small_shapes = [g_small[k].shape for k in SMALL] + [(LANES,)]
    summed = _unpack_flat(allreduce_small("reduce_small", _pack_flat([g_small[k] for k in SMALL] + [loss_l], F32, LANES)),
                          small_shapes)
    gsum = dict(zip(SMALL, summed[:-1]))
    loss = summed[-1][0]
    gsum["ssd_conv_w"] = lax.dynamic_slice(gsum["ssd_conv_w"], (0, 0, chip * cw_w), cw.shape)

    pair = []
    for k in BIG:
        _, rows, cols = g_big[k][0].shape
        pair.append(exchange_add("pair_" + k, g_big[k][0].reshape(4 * rows, cols),
                                 g_big[k][1].reshape(4 * rows, cols)).reshape(4, rows, cols))
    landed = scatter_pieces("scatter_grads", pair)
    g_sum = [sum_exchange("sum_" + k, parts) for k, parts in zip(BIG, landed)]

    grads, deltas, new_m, new_v = {}, {}, {}, {}
    for i, k in enumerate(BIG):
        shp = w_loc[k].shape
        three_d = lambda a, shp=shp: a.reshape((-1,) + shp[-2:])
        res = adamw("adamw_" + k, three_d(w_loc[k]), three_d(m_loc[k]), three_d(v_loc[k]), [three_d(g_sum[i])])
        grads[k], deltas[k], new_m[k], new_v[k] = [r.reshape(shp) for r in res]
    small_loc_shapes = [w_loc[k].shape for k in SMALL]
    res = adamw("adamw_small", _pack_flat([w_loc[k] for k in SMALL], F32, LANES)[None],
                _pack_flat([m_loc[k] for k in SMALL], F32, LANES)[None],
                _pack_flat([v_loc[k] for k in SMALL], F32, LANES)[None],
                [_pack_flat([gsum[k] for k in SMALL], F32, LANES)[None]])
    for dst, r in zip((grads, deltas, new_m, new_v), res):
        dst.update(dict(zip(SMALL, _unpack_flat(r, small_loc_shapes))))

    return (loss, grad_x[None], *[grads[k] for k in WEIGHTS], *[deltas[k] for k in WEIGHTS],
            *[new_m[k] for k in WEIGHTS], *[new_v[k] for k in WEIGHTS])
```

```python
import functools
import math
from typing import NamedTuple

import jax
import jax.numpy as jnp
from jax import lax
from jax.experimental import pallas as pl
from jax.experimental.pallas import tpu as pltpu

F32 = jnp.float32
CDT = jnp.bfloat16
NORM_EPS = 1e-6
ROPE_THETA = 10000.0
GRID_W = 64
D_MODEL = 1024
ATTN_HEADS, ATTN_KV, ATTN_HD = 8, 2, 64
SSD_HEADS, SSD_HD, SSD_STATE = 8, 64, 128
RET_HEADS, RET_DK = 4, 128
CHUNK = 256
ROW_TILE = 256
MM_ROWS = 768
MM_TN_ROWS = 2816
MM_VMEM_BUDGET = 44 * 1024 * 1024
ATTN_TQ, ATTN_TK = 256, 256
ATTN_ONES_ROWS = 16
ATTN_TK_BWD = 2048
LOG2E, LN2 = 1.4426950408889634, 0.6931471805599453
ATTN_TK_FWD = 2048
LANES = 128
PAIR_BLOCK_BYTES = 2 * 1024 * 1024
COND_ROWS = 16
VMEM_LIMIT = 56 * 1024 * 1024

ADAM_LR, ADAM_B1, ADAM_B2, ADAM_EPS, ADAM_WD, ADAM_STEP = 0.001, 0.9, 0.999, 1e-08, 0.01, 10

IN_LAYOUT = {
    "gates": (0, 3072, 4368, 3072), "xbc": (3072, 1024, 1280, 1024), "q": (4096, 512, 0, 512),
    "z": (4608, 512, 768, 512), "rq": (5120, 512, 2320, 512), "rk": (5632, 512, 2832, 512),
    "rv": (6144, 512, 3344, 512), "rg": (6656, 512, 3856, 512), "k": (7168, 128, 512, 128),
    "v": (7296, 128, 640, 128), "dt": (7424, 128, 2304, 16),
}
IN_PAD = 7680
IN_ORIG_ORDER = ("q", "k", "v", "z", "xbc", "dt", "rq", "rk", "rv", "rg", "gates")
IN_NEW_ORDER = ("gates", "xbc", "q", "z", "rq", "rk", "rv", "rg", "k", "v", "dt")

BIG = ("w_mod", "w_in", "w_branch", "w_out", "w_mlp1", "w_mlp2")
BIG_KIND = {"w_mod": "cols", "w_in": "slices", "w_branch": "cols", "w_out": "rows", "w_mlp1": "cols", "w_mlp2": "rows"}
SMALL = ("c_ctx", "b_mod", "norm1_w", "norm2_w", "attn_q_norm", "attn_k_norm", "ssd_conv_w", "ssd_conv_b",
         "ssd_dt_bias", "ssd_a_log", "ssd_d", "ssd_norm_w", "ret_log_decay", "ret_gn_w", "final_norm_w")
WEIGHTS = ("c_ctx", "w_mod", "b_mod", "norm1_w", "norm2_w", "w_in", "attn_q_norm", "attn_k_norm", "ssd_conv_w",
           "ssd_conv_b", "ssd_dt_bias", "ssd_a_log", "ssd_d", "ssd_norm_w", "ret_log_decay", "ret_gn_w",
           "w_branch", "w_out", "w_mlp1", "w_mlp2", "final_norm_w")


def _cp(sem):
    return pltpu.CompilerParams(dimension_semantics=sem, vmem_limit_bytes=VMEM_LIMIT)


class Cols(NamedTuple):
    arr: jax.Array
    off: int
    width: int


def _width(item):
    return item.width if isinstance(item, Cols) else item.shape[1]


def _row_in(item, rows, imap=None):
    imap = imap or (lambda i: i)
    if isinstance(item, Cols):
        assert item.off % item.width == 0
        blk = item.off // item.width
        return item.arr, pl.BlockSpec((rows, item.width), lambda i, blk=blk: (imap(i), blk))
    return item, pl.BlockSpec((rows, item.shape[1]), lambda i: (imap(i), 0))


def _const_spec(shape):
    return pl.BlockSpec(shape, lambda *_: (0,) * len(shape))


def _mxu(a, b, dims=(((1,), (0,)), ((), ()))):
    return lax.dot_general(a.astype(CDT), b.astype(CDT), dims, preferred_element_type=F32)


_NT = (((1,), (1,)), ((), ()))
_TN = (((0,), (0,)), ((), ()))


@jax.custom_vjp
def _softplus(x):
    return jnp.maximum(x, 0.0) + jnp.log1p(jnp.exp(-jnp.abs(x)))


def _softplus_fwd(x):
    return _softplus(x), x


def _softplus_bwd(x, g):
    return (g * jax.nn.sigmoid(x),)


_softplus.defvjp(_softplus_fwd, _softplus_bwd)


def _group_mean_impl(x, gmat):
    hi = x.astype(CDT)
    lo = (x - hi.astype(F32)).astype(CDT)
    return (jnp.dot(hi, gmat, preferred_element_type=F32) + jnp.dot(lo, gmat, preferred_element_type=F32))


@jax.custom_vjp
def _group_mean(x, gmat):
    return _group_mean_impl(x, gmat)


def _group_mean_fwd(x, gmat):
    return _group_mean_impl(x, gmat), gmat


def _group_mean_bwd(gmat, g):
    return _group_mean_impl(g, gmat), jnp.zeros_like(gmat)


_group_mean.defvjp(_group_mean_fwd, _group_mean_bwd)


def _group_matrix(width, group):
    r = jnp.arange(width) // group
    return jnp.where(r[:, None] == r[None, :], 1.0 / group, 0.0).astype(CDT)


def _make_rope(half):
    def partner(x):
        w = x.shape[1]
        lane = lax.broadcasted_iota(jnp.int32, x.shape, 1)
        first = (lane % (2 * half)) < half
        return jnp.where(first, pltpu.roll(x, w - half, axis=1), pltpu.roll(x, half, axis=1))

    def impl(x, cos_full, sin_signed):
        return x * cos_full + partner(x) * sin_signed

    @jax.custom_vjp
    def rope(x, cos_full, sin_signed):
        return impl(x, cos_full, sin_signed)

    def fwd(x, cos_full, sin_signed):
        return impl(x, cos_full, sin_signed), (cos_full, sin_signed)

    def bwd(res, g):
        cos_full, sin_signed = res
        return impl(g, cos_full, -sin_signed), jnp.zeros_like(cos_full), jnp.zeros_like(sin_signed)

    rope.defvjp(fwd, bwd)
    return rope


_rope32 = _make_rope(32)
_rope64 = _make_rope(64)


def _rms(x, w):
    return x * lax.rsqrt(jnp.mean(x * x, axis=-1, keepdims=True) + NORM_EPS) * w


def _col(v, lane_index):
    lane = lax.broadcasted_iota(jnp.int32, v.shape, 1)
    return jnp.sum(jnp.where(lane == lane_index, v, 0.0), axis=1, keepdims=True)


def _typed_spec(width, nb_ctx):
    return pl.BlockSpec((None, 1, width), lambda i: (jnp.where(i >= nb_ctx, 1, 0), 0, 0))


def rowwise_fwd(name, f, rows, typed, shared, outs, n_rows, nb_ctx, tm=ROW_TILE):
    tm = min(tm, n_rows)
    nin = len(rows) + len(typed) + len(shared)

    def body(*refs):
        res = f(*[r[...] for r in refs[:nin]])
        for o_ref, o in zip(refs[nin:], res):
            o_ref[...] = o.astype(o_ref.dtype)

    arrs, specs = [], []
    for it in rows:
        a, s = _row_in(it, tm)
        arrs.append(a)
        specs.append(s)
    for t in typed:
        arrs.append(t)
        specs.append(_typed_spec(t.shape[-1], nb_ctx))
    for s_ in shared:
        arrs.append(s_)
        specs.append(_const_spec(s_.shape))
    res = pl.pallas_call(
        body, name=name, grid=(n_rows // tm,), in_specs=specs,
        out_specs=[pl.BlockSpec((tm, w), lambda i: (i, 0)) for w, _ in outs],
        out_shape=[jax.ShapeDtypeStruct((n_rows, w), dt) for w, dt in outs],
        compiler_params=_cp(("parallel",)))(*arrs)
    return res


def rowwise_bwd(name, f, rows, typed, shared, cots, row_diff, shared_diff, drow_dtypes, n_rows, nb_ctx, tm=ROW_TILE):
    tm = min(tm, n_rows)
    cot_groups = [c_ if isinstance(c_, tuple) else (c_,) for c_ in cots]
    cots = [a for grp in cot_groups for a in grp]
    nr, nt, ns, nc = len(rows), len(typed), len(shared), len(cots)
    nin = nr + nt + ns
    d_rows = [k for k in range(nr) if row_diff[k]]
    d_sh = [k for k in range(ns) if shared_diff[k]]

    def body(*refs):
        rvals = [r[...] for r in refs[:nr]]
        tvals = [r[...] for r in refs[nr:nr + nt]]
        svals = [r[...] for r in refs[nr + nt:nin]]
        cparts = [r[...].astype(F32) for r in refs[nin:nin + nc]]
        cvals = []
        for grp in cot_groups:
            cvals.append(sum(cparts[1:len(grp)], cparts[0]))
            cparts = cparts[len(grp):]
        out_refs = refs[nin + nc:]

        def g(*dv):
            dv = list(dv)
            rv = list(rvals)
            for k in d_rows:
                rv[k] = dv.pop(0)
            tv = [dv.pop(0) for _ in range(nt)]
            sv = list(svals)
            for k in d_sh:
                sv[k] = dv.pop(0)
            return tuple(o.astype(F32) for o in f(*rv, *tv, *sv))

        prim = [rvals[k].astype(F32) for k in d_rows] + tvals + [svals[k] for k in d_sh]
        _, vjp = jax.vjp(g, *prim)
        grads = list(vjp(tuple(cvals)))
        i = pl.program_id(0)
        for ref in out_refs[:len(d_rows)]:
            ref[...] = grads.pop(0).astype(ref.dtype)
        first_typed = (i == 0) | (i == nb_ctx)
        for ref in out_refs[len(d_rows):len(d_rows) + nt]:
            gr = grads.pop(0)

            @pl.when(first_typed)
            def _(ref=ref, gr=gr):
                ref[...] = gr

            @pl.when(jnp.logical_not(first_typed))
            def _(ref=ref, gr=gr):
                ref[...] += gr
        for ref in out_refs[len(d_rows) + nt:]:
            gr = grads.pop(0)

            @pl.when(i == 0)
            def _(ref=ref, gr=gr):
                ref[...] = gr

            @pl.when(i != 0)
            def _(ref=ref, gr=gr):
                ref[...] += gr

    arrs, specs = [], []
    for it in list(rows):
        a, s = _row_in(it, tm)
        arrs.append(a)
        specs.append(s)
    for t in typed:
        arrs.append(t)
        specs.append(_typed_spec(t.shape[-1], nb_ctx))
    for s_ in shared:
        arrs.append(s_)
        specs.append(_const_spec(s_.shape))
    for c_ in cots:
        a, s = _row_in(c_, tm)
        arrs.append(a)
        specs.append(s)
    out_specs, out_shape = [], []
    for k, dt in zip(d_rows, drow_dtypes):
        w = _width(rows[k])
        out_specs.append(pl.BlockSpec((tm, w), lambda i: (i, 0)))
        out_shape.append(jax.ShapeDtypeStruct((n_rows, w), dt))
    for t in typed:
        out_specs.append(_typed_spec(t.shape[-1], nb_ctx))
        out_shape.append(jax.ShapeDtypeStruct(t.shape, F32))
    for k in d_sh:
        out_specs.append(_const_spec(shared[k].shape))
        out_shape.append(jax.ShapeDtypeStruct(shared[k].shape, F32))
    res = pl.pallas_call(body, name=name, grid=(n_rows // tm,), in_specs=specs, out_specs=out_specs,
                         out_shape=out_shape, compiler_params=_cp(("arbitrary",)))(*arrs)
    n1, n2 = len(d_rows), len(d_rows) + nt
    return list(res[:n1]), list(res[n1:n2]), list(res[n2:])


def _pick(n, prefs):
    for p in prefs:
        if n % p == 0:
            return p
    return n


def mm(name, a, b, out_dtype, transpose_b=False, epilogue=None, tile_ins=(), typed_ins=(), shared_ins=(),
       out_dtypes=None, nb_ctx=None, whole_rows=False):
    n, k = b.shape if transpose_b else b.shape[::-1]
    m = (a.arr if isinstance(a, Cols) else a).shape[0]
    assert _width(a) == k
    tm = min(ROW_TILE, m) if typed_ins else _pick(m, (MM_ROWS, 256))
    out_dtypes = out_dtypes or [out_dtype]
    tile_bytes = sum(jnp.dtype(d).itemsize for d in out_dtypes) + sum(x.dtype.itemsize for x in tile_ins)
    fits = lambda c: 2 * (tm * k * 2 + k * c * 2 + tm * c * tile_bytes) <= MM_VMEM_BUDGET
    tn = next(c for c in (2560, 2048, 1536, 1024, 512, 256, 128, n) if n % c == 0 and fits(c) or c == n)
    if whole_rows:
        assert fits(n)
        tn = n
    dims = _NT if transpose_b else (((1,), (0,)), ((), ()))
    n_ex = len(tile_ins) + len(typed_ins) + len(shared_ins)

    def body(a_ref, b_ref, *refs):
        prod = lax.dot_general(a_ref[...], b_ref[...], dims, preferred_element_type=F32)
        outs = (prod,) if epilogue is None else epilogue(prod, *[r[...] for r in refs[:n_ex]])
        for o_ref, o in zip(refs[n_ex:], outs):
            o_ref[...] = o.astype(o_ref.dtype)

    a_arr, a_spec = _row_in(a, tm)
    a_spec = pl.BlockSpec(a_spec.block_shape, lambda j, i, f=a_spec.index_map: f(i))
    b_spec = pl.BlockSpec((tn, k), lambda j, i: (j, 0)) if transpose_b else pl.BlockSpec((k, tn), lambda j, i: (0, j))
    tile = pl.BlockSpec((tm, tn), lambda j, i: (i, j))
    ex_specs = [tile] * len(tile_ins)
    ex_specs += [pl.BlockSpec((None, 1, tn), lambda j, i: (jnp.where(i >= nb_ctx, 1, 0), 0, j))] * len(typed_ins)
    ex_specs += [pl.BlockSpec((1, tn), lambda j, i: (0, j))] * len(shared_ins)
    res = pl.pallas_call(
        body, name=name, grid=(n // tn, m // tm), in_specs=[a_spec, b_spec] + ex_specs,
        out_specs=[tile] * len(out_dtypes),
        out_shape=[jax.ShapeDtypeStruct((m, n), d) for d in out_dtypes],
        compiler_params=_cp(("parallel", "parallel")))(a_arr, b, *tile_ins, *typed_ins, *shared_ins)
    return res[0] if epilogue is None else res


def mm_tn(name, a, b, out_dtype=F32, pieces=None):
    t = (a.arr if isinstance(a, Cols) else a).shape[0]
    k, n = _width(a), _width(b)
    tt = _pick(t, (MM_TN_ROWS, MM_ROWS, 256))
    k_unit = pieces[1] if pieces and pieces[0] == "rows" else k
    n_unit = pieces[1] if pieces and pieces[0] == "cols" else n
    tk = _pick(k_unit, (1024, 512, 256, 128))
    tn = _pick(n_unit, (1280, 1024, 512, 256, 128))
    n_t = t // tt

    def body(a_ref, b_ref, o_ref, acc):
        part = lax.dot_general(a_ref[...], b_ref[...], _TN, preferred_element_type=F32)
        ti = pl.program_id(2)

        @pl.when(ti == 0)
        def _():
            acc[...] = part

        @pl.when(ti != 0)
        def _():
            acc[...] += part

        @pl.when(ti == n_t - 1)
        def _():
            o_ref[...] = acc[...].astype(o_ref.dtype)

    def win(item, width):
        if isinstance(item, Cols):
            assert item.off % width == 0
            return item.arr, item.off // width
        return item, 0

    a_arr, a0 = win(a, tk)
    b_arr, b0 = win(b, tn)
    if pieces is None:
        out_spec = pl.BlockSpec((tk, tn), lambda ki, ni, ti: (ki, ni))
        out_shape = (k, n)
    elif pieces[0] == "cols":
        per = n_unit // tn
        out_spec = pl.BlockSpec((None, tk, tn), lambda ki, ni, ti: (ni // per, ki, ni % per))
        out_shape = (4, k, n_unit)
    else:
        per = k_unit // tk
        out_spec = pl.BlockSpec((None, tk, tn), lambda ki, ni, ti: (ki // per, ki % per, ni))
        out_shape = (4, k_unit, n)
    return pl.pallas_call(
        body, name=name, grid=(k // tk, n // tn, n_t),
        in_specs=[pl.BlockSpec((tt, tk), lambda ki, ni, ti: (ti, a0 + ki)),
                  pl.BlockSpec((tt, tn), lambda ki, ni, ti: (ti, b0 + ni))],
        out_specs=out_spec, out_shape=jax.ShapeDtypeStruct(out_shape, out_dtype),
        scratch_shapes=[pltpu.VMEM((tk, tn), F32)],
        compiler_params=_cp(("parallel", "parallel", "arbitrary")))(a_arr, b_arr)


def _heads_t(rows_blk):
    blk = rows_blk.astype(F32).T
    return jnp.concatenate([blk[hh * ATTN_HD:(hh + 1) * ATTN_HD, :] for hh in range(4)], axis=1)


def _heads_rows(t_blk):
    tq = t_blk.shape[1] // 4
    return jnp.concatenate([t_blk[:, hh * tq:(hh + 1) * tq] for hh in range(4)], axis=0).T


def attn_fwd(name, q, kk, vT_ones, m_ctx, tq):
    t, hd, hd_ext = kk.shape[1], ATTN_HD, vT_ones.shape[1]
    nq, r = t // tq, 4 * tq
    tk = _pick(t - m_ctx, (ATTN_TK_FWD, ATTN_TK))
    nqc, n_lat_tiles = m_ctx // tq, (t - m_ctx) // tk

    def body(q_ref, k_ref, vT_ref, o_ref, qT_ref, oT_ref, lse_ref):
        i = pl.program_id(1)
        q_t = _heads_t(q_ref[...]).astype(CDT)
        qT_ref[...] = q_t

        def tile(off, size, carry):
            mi, acc = carry
            sub = min(size, ATTN_TK)
            offs = [off + u * sub for u in range(size // sub)]
            sts = [jnp.dot(k_ref[pl.ds(o, sub), :], q_t, preferred_element_type=F32) for o in offs]
            for o, st in zip(offs, sts):
                mn = jnp.maximum(mi, jnp.max(st, axis=0, keepdims=True))
                pt = jnp.exp2(st - mn)
                acc = jnp.exp2(mi - mn) * acc + jnp.dot(vT_ref[:, pl.ds(o, sub)], pt.astype(CDT),
                                                        preferred_element_type=F32)
                mi = mn
            return mi, acc

        carry = tile(0, m_ctx, (jnp.full((1, r), -1e30, F32), jnp.zeros((hd_ext, r), F32)))
        mi, acc = lax.fori_loop(
            0, jnp.where(i < nqc, 0, n_lat_tiles),
            lambda j, cr: tile(pl.multiple_of(m_ctx + j * tk, ATTN_TK), tk, cr), carry)
        li = acc[hd:hd + 1]
        o_t = acc[:hd] / li
        oT_ref[...] = o_t.astype(oT_ref.dtype)
        o_ref[...] = _heads_rows(o_t).astype(o_ref.dtype)
        lse_ref[...] = mi + jnp.log2(li)

    blk_t = pl.BlockSpec((None, None, hd, r), lambda g, i: (g, i, 0, 0))
    rows = pl.BlockSpec((tq, 4 * hd), lambda g, i: (i, g))
    return pl.pallas_call(
        body, name=name, grid=(2, nq),
        in_specs=[rows, pl.BlockSpec((None, t, hd), lambda g, i: (g, 0, 0)),
                  pl.BlockSpec((None, hd_ext, t), lambda g, i: (g, 0, 0))],
        out_specs=[rows, blk_t, blk_t, pl.BlockSpec((None, None, 1, r), lambda g, i: (g, i, 0, 0))],
        out_shape=[jax.ShapeDtypeStruct((t, 8 * hd), CDT), jax.ShapeDtypeStruct((2, nq, hd, r), CDT),
                   jax.ShapeDtypeStruct((2, nq, hd, r), CDT), jax.ShapeDtypeStruct((2, nq, 1, r), F32)],
        compiler_params=_cp(("parallel", "arbitrary")))(q, kk, vT_ones)


def attn_bwd(name, qT, do, oT, lse, kk, kT, vv, m_ctx):
    _, nq, hd, r = qT.shape
    t = kk.shape[1]
    tq = r // 4
    tk = _pick(t - m_ctx, (ATTN_TK_BWD, ATTN_TK))
    nqc, n_lat_tiles = m_ctx // tq, (t - m_ctx) // tk

    def body(qT_ref, do_ref, oT_ref, lse_ref, k_ref, kT_ref, v_ref, dq_ref, dk_ref, dv_ref):
        i = pl.program_id(1)

        @pl.when(i == 0)
        def _():
            dk_ref[...] = jnp.zeros_like(dk_ref)
            dv_ref[...] = jnp.zeros_like(dv_ref)

        q_t = qT_ref[...]
        do_f = _heads_t(do_ref[...])
        do_t = do_f.astype(CDT)
        lse = lse_ref[...]
        delta = jnp.sum(do_f * oT_ref[...].astype(F32), axis=0, keepdims=True)

        def tile(off, size, dq):
            sub = min(size, ATTN_TK)
            offs = [off + u * sub for u in range(size // sub)]
            sts = [jnp.dot(k_ref[pl.ds(o, sub), :], q_t, preferred_element_type=F32) for o in offs]
            dpts = [jnp.dot(v_ref[pl.ds(o, sub), :], do_t, preferred_element_type=F32) for o in offs]
            for o, st, dpt in zip(offs, sts, dpts):
                pt = jnp.exp2(st - lse)
                dv_ref[pl.ds(o, sub), :] += lax.dot_general(pt.astype(CDT), do_t, _NT, preferred_element_type=F32)
                dst = (pt * (dpt - delta)).astype(CDT)
                dk_ref[pl.ds(o, sub), :] += lax.dot_general(dst, q_t, _NT, preferred_element_type=F32)
                dq = dq + jnp.dot(kT_ref[:, pl.ds(o, sub)], dst, preferred_element_type=F32)
            return dq

        dq = tile(0, m_ctx, jnp.zeros((hd, r), F32))
        dq = lax.fori_loop(0, jnp.where(i < nqc, 0, n_lat_tiles),
                           lambda j, acc: tile(pl.multiple_of(m_ctx + j * tk, ATTN_TK), tk, acc), dq)
        dq_ref[...] = _heads_rows(dq * LN2)

    blk_t = pl.BlockSpec((None, None, hd, r), lambda g, i: (g, i, 0, 0))
    row = pl.BlockSpec((None, None, 1, r), lambda g, i: (g, i, 0, 0))
    kv = pl.BlockSpec((None, t, hd), lambda g, i: (g, 0, 0))
    rows = pl.BlockSpec((tq, 4 * hd), lambda g, i: (i, g))
    return pl.pallas_call(
        body, name=name, grid=(2, nq),
        in_specs=[blk_t, rows, blk_t, row, kv, pl.BlockSpec((None, hd, t), lambda g, i: (g, 0, 0)), kv],
        out_specs=[rows, kv, kv],
        out_shape=[jax.ShapeDtypeStruct((t, 8 * hd), F32), jax.ShapeDtypeStruct(kk.shape, F32),
                   jax.ShapeDtypeStruct(kk.shape, F32)],
        compiler_params=_cp(("parallel", "arbitrary")))(qT, do, oT, lse, kk, kT, vv)


def _split_kv(a):
    return a.reshape(a.shape[0], 2, ATTN_HD).transpose(1, 0, 2)


def _merge_kv(a):
    return a.transpose(1, 0, 2).reshape(a.shape[1], 2 * ATTN_HD)


def _chunk_order(rev, ncc, nct):
    if not rev:
        return lambda s: s
    return lambda s: jnp.where(s < ncc, ncc - 1 - s, nct - 1 - (s - ncc))


def scan_fwd(name, make_fn, rows, shared, n_state, y_width, n_rows, m_ctx):
    nct, ncc = n_rows // CHUNK, m_ctx // CHUNK
    orders = [_chunk_order(rev, ncc, nct) for rev in (False, True)]
    fns = [make_fn(0), make_fn(1)]
    nr, ns = len(rows), len(shared)

    def body(*refs):
        svals = [r[...] for r in refs[2 * nr:2 * nr + ns]]
        y_refs, sin_refs, st = refs[2 * nr + ns:2 * nr + ns + 2], refs[2 * nr + ns + 2:2 * nr + ns + 4], refs[-1]

        @pl.when(pl.program_id(0) == 0)
        def _():
            st[...] = jnp.zeros_like(st)

        for d in range(2):
            rvals = [r[...] for r in refs[d * nr:(d + 1) * nr]]
            prev = [st[d, k] for k in range(n_state)]
            sin_refs[d][...] = st[d]
            y, new = fns[d](rvals, svals, prev)
            y_refs[d][...] = y
            for k in range(n_state):
                st[d, k] = new[k]

    arrs, specs = [], []
    for order in orders:
        for it in rows:
            a, s = _row_in(it, CHUNK, order)
            arrs.append(a)
            specs.append(s)
    for s_ in shared:
        arrs.append(s_)
        specs.append(_const_spec(s_.shape))
    return pl.pallas_call(
        body, name=name, grid=(nct,), in_specs=specs,
        out_specs=[pl.BlockSpec((CHUNK, y_width), lambda s, o=o: (o(s), 0)) for o in orders]
        + [pl.BlockSpec((None, n_state, LANES, LANES), lambda s, o=o: (o(s), 0, 0, 0)) for o in orders],
        out_shape=[jax.ShapeDtypeStruct((n_rows, y_width), F32)] * 2
        + [jax.ShapeDtypeStruct((nct, n_state, LANES, LANES), F32)] * 2,
        scratch_shapes=[pltpu.VMEM((2, n_state, LANES, LANES), F32)],
        compiler_params=_cp(("arbitrary",)))(*arrs)


def scan_bwd(name, make_fn, rows, shared, states_in, dy, post, outs, n_state, n_rows, m_ctx, dirs=(0, 1)):
    nct, ncc = n_rows // CHUNK, m_ctx // CHUNK
    orders = [(lambda r, f=_chunk_order(d == 1, ncc, nct): f(nct - 1 - r)) for d in dirs]
    fns = [make_fn(d) for d in dirs]
    nd = len(dirs)
    nr, ns, no = len(rows), len(shared), len(outs)
    n_in = nd * nr + ns

    def body(*refs):
        svals = [r[...] for r in refs[nd * nr:n_in]]
        sin_refs, dy_refs = refs[n_in:n_in + nd], refs[n_in + nd:n_in + 2 * nd]
        out_refs = refs[n_in + 2 * nd:n_in + 2 * nd + nd * no]
        dsh_refs = refs[n_in + 2 * nd + nd * no:-1]
        dst = refs[-1]
        r = pl.program_id(0)

        @pl.when(r == 0)
        def _():
            dst[...] = jnp.zeros_like(dst)

        d_shared = None
        for d in range(nd):
            rvals = [x[...] for x in refs[d * nr:(d + 1) * nr]]
            prev = [sin_refs[d][k] for k in range(n_state)]
            _, vjp = jax.vjp(fns[d], rvals, svals, prev)
            d_rows, d_sh, d_prev = vjp((dy_refs[d][...], [dst[d, k] for k in range(n_state)]))
            for ref, val in zip(out_refs[d * no:(d + 1) * no], post(d_rows)):
                ref[...] = val.astype(ref.dtype)
            d_shared = d_sh if d_shared is None else [a + b for a, b in zip(d_shared, d_sh)]
            for k in range(n_state):
                dst[d, k] = d_prev[k]
        for ref, gr in zip(dsh_refs, d_shared):
            @pl.when(r == 0)
            def _(ref=ref, gr=gr):
                ref[...] = gr

            @pl.when(r != 0)
            def _(ref=ref, gr=gr):
                ref[...] += gr

    arrs, specs = [], []
    for order in orders:
        for it in rows:
            a, s = _row_in(it, CHUNK, order)
            arrs.append(a)
            specs.append(s)
    for s_ in shared:
        arrs.append(s_)
        specs.append(_const_spec(s_.shape))
    for sin, order in zip(states_in, orders):
        arrs.append(sin)
        specs.append(pl.BlockSpec((None, n_state, LANES, LANES), lambda r, o=order: (o(r), 0, 0, 0)))
    for order in orders:
        a, s = _row_in(dy, CHUNK, order)
        arrs.append(a)
        specs.append(s)
    out_specs = [pl.BlockSpec((CHUNK, w), lambda r, o=o: (o(r), 0)) for o in orders for w, _ in outs]
    out_shape = [jax.ShapeDtypeStruct((n_rows, w), dt) for _ in orders for w, dt in outs]
    for s_ in shared:
        out_specs.append(_const_spec(s_.shape))
        out_shape.append(jax.ShapeDtypeStruct(s_.shape, F32))
    res = pl.pallas_call(body, name=name, grid=(nct,), in_specs=specs, out_specs=out_specs, out_shape=out_shape,
                         scratch_shapes=[pltpu.VMEM((nd, n_state, LANES, LANES), F32)],
                         compiler_params=_cp(("arbitrary",)))(*arrs)
    return [list(res[d * no:(d + 1) * no]) for d in range(nd)] + [list(res[nd * no:])]


def _make_ssd_chunk(direction):
    rev = direction == 1
    base = 8 * direction

    def fn(rows, shared, prev):
        xs, bms, cms, dtraw = rows[0:4], rows[4:6], rows[6:8], rows[8]
        dt_bias, a_log = shared
        ln = dtraw.shape[0]
        dt_all = _softplus(dtraw + dt_bias)
        a_all = dt_all * (-jnp.exp(a_log))
        r_i = lax.broadcasted_iota(jnp.int32, (ln, ln), 0)
        c_i = lax.broadcasted_iota(jnp.int32, (ln, ln), 1)
        tri = (r_i <= c_i) if rev else (r_i >= c_i)
        a_cum_all = jnp.dot(tri.astype(F32), a_all, precision=lax.Precision.HIGHEST, preferred_element_type=F32)
        a_tot_all = jnp.sum(a_all, axis=0, keepdims=True)
        first = lax.broadcasted_iota(jnp.int32, (ln, LANES), 1) < SSD_HD
        first_row = lax.broadcasted_iota(jnp.int32, (LANES, 1), 0) < SSD_HD

        def lmat(acol):
            a_b = jnp.broadcast_to(acol, (ln, ln))
            seg = a_b - a_b.T
            return jnp.where(tri, jnp.exp(jnp.where(tri, seg, 0.0)), 0.0)

        ys, new = [], []
        for g in range(2):
            bm, cm = bms[g], cms[g]
            cb = _mxu(cm, bm, _NT)
            for jj in range(2):
                pr = 2 * g + jj
                h0, h1 = base + 2 * pr, base + 2 * pr + 1
                ac0, ac1 = _col(a_cum_all, h0), _col(a_cum_all, h1)
                at0, at1 = _col(a_tot_all, h0), _col(a_tot_all, h1)
                dt_pair = jnp.where(first, _col(dt_all, h0), _col(dt_all, h1))
                acum_pair = jnp.where(first, ac0, ac1)
                atot_pair = jnp.where(first[0:1], at0, at1)
                xd = xs[pr] * dt_pair
                st = _mxu(xd * jnp.exp(atot_pair - acum_pair), bm, _TN)
                new.append(prev[pr] * jnp.where(first_row, jnp.exp(at0), jnp.exp(at1)) + st)
                y0 = _mxu(cb * lmat(ac0), xd)
                y1 = _mxu(cb * lmat(ac1), xd)
                y_off = _mxu(cm, prev[pr], _NT) * jnp.exp(acum_pair)
                ys.append(jnp.where(first, y0, y1) + y_off)
        return jnp.concatenate(ys, axis=1), new

    return fn


def _make_ret_chunk(direction):
    rev = direction == 1
    base = 4 * direction

    def fn(rows, shared, prev):
        qs, ks, vs = rows[0:4], rows[4:8], rows[8:12]
        lg_all = -jnp.exp(shared[0])
        ln = qs[0].shape[0]
        pos = lax.broadcasted_iota(jnp.int32, (ln, 1), 0).astype(F32)
        r_i = lax.broadcasted_iota(jnp.int32, (ln, ln), 0)
        c_i = lax.broadcasted_iota(jnp.int32, (ln, ln), 1)
        diff = ((c_i - r_i) if rev else (r_i - c_i))
        mask = diff >= 0
        dpos = jnp.maximum(diff, 0).astype(F32)
        k_pow = pos if rev else (ln - 1.0 - pos)
        q_pow = (ln - pos) if rev else (pos + 1.0)
        ys, new = [], []
        for h in range(RET_HEADS):
            lg = _col(lg_all, base + h)
            dmat = jnp.where(mask, jnp.exp(dpos * lg), 0.0)
            st = _mxu(ks[h] * jnp.exp(k_pow * lg), vs[h], _TN)
            new.append(prev[h] * jnp.exp(ln * lg) + st)
            s = _mxu(qs[h], ks[h], _NT) * dmat
            ys.append(_mxu(s, vs[h]) + _mxu(qs[h], prev[h]) * jnp.exp(q_pow * lg))
        return jnp.concatenate(ys, axis=1), new

    return fn


def _conv_pre(x, w, b, t_idx, n_rows, m_ctx):
    is_start = (t_idx == 0) | (t_idx == m_ctx)
    is_end = (t_idx == m_ctx - 1) | (t_idx == n_rows - 1)
    xp = jnp.where(is_start, 0.0, pltpu.roll(x, 1, axis=0))
    xn = jnp.where(is_end, 0.0, pltpu.roll(x, n_rows - 1, axis=0))
    return w[0:1] * xp + w[1:2] * x + w[2:3] * xn + b, xp, xn, is_start, is_end


def conv_fwd(x, conv_w, conv_b, m_ctx):
    n_rows, width = x.arr.shape[0], x.width
    c0 = x.off // LANES

    def body(x_ref, w_ref, b_ref, o_ref):
        t_idx = lax.broadcasted_iota(jnp.int32, (n_rows, 1), 0)
        pre = _conv_pre(x_ref[...], w_ref[...], b_ref[...], t_idx, n_rows, m_ctx)[0]
        o_ref[...] = pre * jax.nn.sigmoid(pre)

    return pl.pallas_call(
        body, name="conv_fwd", grid=(width // LANES,),
        in_specs=[pl.BlockSpec((n_rows, LANES), lambda c: (0, c0 + c)),
                  pl.BlockSpec((3, LANES), lambda c: (0, c)), pl.BlockSpec((1, LANES), lambda c: (0, c))],
        out_specs=pl.BlockSpec((n_rows, LANES), lambda c: (0, c)),
        out_shape=jax.ShapeDtypeStruct((n_rows, width), F32),
        compiler_params=_cp(("parallel",)))(x.arr, conv_w, conv_b)


def conv_bwd(x, conv_w, conv_b, dy_a, dy_b, dxs_extra, m_ctx):
    n_rows, width = x.arr.shape[0], x.width
    c0 = x.off // LANES
    n_extra = dxs_extra.shape[1] // LANES

    def body(x_ref, w_ref, b_ref, dya_ref, dyb_ref, ex_ref, dx_ref, dw_ref, db_ref):
        c = pl.program_id(0)
        t_idx = lax.broadcasted_iota(jnp.int32, (n_rows, 1), 0)
        w = w_ref[...]
        pre, xp, xn, is_start, is_end = _conv_pre(x_ref[...], w, b_ref[...], t_idx, n_rows, m_ctx)
        sg = jax.nn.sigmoid(pre)
        dyv = dya_ref[...] + dyb_ref[...] + jnp.where(c < n_extra, ex_ref[...], 0.0)
        dpre = dyv * (sg * (1.0 + pre * (1.0 - sg)))
        d_next = jnp.where(is_end, 0.0, pltpu.roll(dpre, n_rows - 1, axis=0))
        d_prev = jnp.where(is_start, 0.0, pltpu.roll(dpre, 1, axis=0))
        dx_ref[...] = (w[1:2] * dpre + w[0:1] * d_next + w[2:3] * d_prev).astype(dx_ref.dtype)
        dw_ref[...] = jnp.concatenate([jnp.sum(dpre * xp, axis=0, keepdims=True),
                                       jnp.sum(dpre * x_ref[...], axis=0, keepdims=True),
                                       jnp.sum(dpre * xn, axis=0, keepdims=True)], axis=0)
        db_ref[...] = jnp.sum(dpre, axis=0, keepdims=True)

    return pl.pallas_call(
        body, name="conv_bwd", grid=(width // LANES,),
        in_specs=[pl.BlockSpec((n_rows, LANES), lambda c: (0, c0 + c)),
                  pl.BlockSpec((3, LANES), lambda c: (0, c)), pl.BlockSpec((1, LANES), lambda c: (0, c)),
                  pl.BlockSpec((n_rows, LANES), lambda c: (0, c)), pl.BlockSpec((n_rows, LANES), lambda c: (0, c)),
                  pl.BlockSpec((n_rows, LANES), lambda c: (0, jnp.minimum(c, n_extra - 1)))],
        out_specs=[pl.BlockSpec((n_rows, LANES), lambda c: (0, c)),
                   pl.BlockSpec((3, LANES), lambda c: (0, c)), pl.BlockSpec((1, LANES), lambda c: (0, c))],
        out_shape=[jax.ShapeDtypeStruct((n_rows, width), CDT), jax.ShapeDtypeStruct((3, width), F32),
                   jax.ShapeDtypeStruct((1, width), F32)],
        compiler_params=_cp(("parallel",)))(x.arr, conv_w, conv_b, dy_a, dy_b, dxs_extra)


def loss_head(h, target, final_w, m_ctx):
    n_rows, d = h.shape
    tm = min(ROW_TILE, n_rows)
    nb_ctx = m_ctx // tm

    def f(hb, w, tgt):
        err = _rms(hb, w) - tgt
        return 0.5 * jnp.sum(jnp.mean(err * err, axis=-1))

    def body(h_ref, t_ref, w_ref, loss_ref, dh_ref, dw_ref):
        i = pl.program_id(0)

        @pl.when(i < nb_ctx)
        def _():
            dh_ref[...] = jnp.zeros_like(dh_ref)

        @pl.when(i == 0)
        def _():
            loss_ref[...] = jnp.zeros_like(loss_ref)
            dw_ref[...] = jnp.zeros_like(dw_ref)

        @pl.when(i >= nb_ctx)
        def _():
            val, vjp = jax.vjp(lambda hb, w: f(hb, w, t_ref[...]), h_ref[...], w_ref[...])
            dh, dw = vjp(jnp.ones((), F32))
            dh_ref[...] = dh
            dw_ref[...] += dw
            loss_ref[...] += jnp.broadcast_to(val, loss_ref.shape)

    return pl.pallas_call(
        body, name="loss_head", grid=(n_rows // tm,),
        in_specs=[pl.BlockSpec((tm, d), lambda i: (i, 0)),
                  pl.BlockSpec((tm, d), lambda i: (jnp.maximum(i - nb_ctx, 0), 0)), _const_spec((1, d))],
        out_specs=[_const_spec((1, LANES)), pl.BlockSpec((tm, d), lambda i: (i, 0)), _const_spec((1, d))],
        out_shape=[jax.ShapeDtypeStruct((1, LANES), F32), jax.ShapeDtypeStruct((n_rows, d), F32),
                   jax.ShapeDtypeStruct((1, d), F32)],
        compiler_params=_cp(("arbitrary",)))(h, target, final_w)


def adamw(name, w, m, v, g_parts):
    lead, rows, cols = w.shape
    tr = _pick(rows, (256, 128, 64, 32, 16, 8))
    npart = len(g_parts)
    c1 = 1.0 - ADAM_B1 ** ADAM_STEP
    c2 = 1.0 - ADAM_B2 ** ADAM_STEP

    def body(*refs):
        w_ref, m_ref, v_ref = refs[:3]
        g = refs[3][...].astype(F32)
        for r in refs[4:3 + npart]:
            g = g + r[...].astype(F32)
        g_ref, d_ref, nm_ref, nv_ref = refs[3 + npart:]
        nm = ADAM_B1 * m_ref[...] + (1.0 - ADAM_B1) * g
        nv = ADAM_B2 * v_ref[...] + (1.0 - ADAM_B2) * (g * g)
        g_ref[...] = g
        nm_ref[...] = nm
        nv_ref[...] = nv
        d_ref[...] = -ADAM_LR * ((nm / c1) / (jnp.sqrt(nv / c2) + ADAM_EPS) + ADAM_WD * w_ref[...])

    spec = pl.BlockSpec((None, tr, cols), lambda l, i: (l, i, 0))
    return pl.pallas_call(
        body, name=name, grid=(lead, rows // tr), in_specs=[spec] * (3 + npart), out_specs=[spec] * 4,
        out_shape=[jax.ShapeDtypeStruct(w.shape, F32)] * 4,
        compiler_params=_cp(("parallel", "parallel")))(w, m, v, *g_parts)


MESH = pl.DeviceIdType.MESH
_HBM = pl.BlockSpec(memory_space=pl.ANY)


def _chip_peers():
    x, y, c = lax.axis_index("x"), lax.axis_index("y"), lax.axis_index("c")
    return x, y, c, [(1 - x, y), (x, 1 - y), (1 - x, 1 - y)]


def _window(ref, kind, chip, rows, cols):
    if kind == "cols":
        return ref.at[:, pl.ds(pl.multiple_of(chip * cols, LANES), cols)]
    if kind == "rows":
        return ref.at[pl.ds(pl.multiple_of(chip * rows, 8), rows), :]
    return ref.at[chip]


def _gathered_shape(kind, rows, cols):
    return {"cols": (rows, 4 * cols), "rows": (4 * rows, cols), "slices": (4, rows, cols)}[kind]


def gather_layers(name, shards, kinds):
    n = len(shards)

    def body(*refs):
        x_refs, o_refs = refs[:n], refs[n:2 * n]
        send_sems, recv_sems, local_sems = refs[2 * n:]
        x, y, c, peers = _chip_peers()
        me = 2 * x + y
        started = []
        for a in range(n):
            _, rows, cols = shards[a].shape
            src = x_refs[a].at[c]
            mine = pltpu.make_async_copy(src, _window(o_refs[a], kinds[a], me, rows, cols), local_sems.at[a])
            mine.start()
            started.append(mine.wait)
            for k, (px, py) in enumerate(peers):
                cp = pltpu.make_async_remote_copy(
                    src_ref=src, dst_ref=_window(o_refs[a], kinds[a], me, rows, cols), send_sem=send_sems.at[3 * a + k],
                    recv_sem=recv_sems.at[3 * a + k], device_id=(px, py, c), device_id_type=MESH)
                cp.start()
                started.append(cp.wait_send)
        for a in range(n):
            _, rows, cols = shards[a].shape
            for k, (px, py) in enumerate(peers):
                pltpu.make_async_remote_copy(
                    src_ref=x_refs[a].at[c], dst_ref=_window(o_refs[a], kinds[a], 2 * px + py, rows, cols),
                    send_sem=send_sems.at[3 * a + k], recv_sem=recv_sems.at[3 * a + k], device_id=(px, py, c),
                    device_id_type=MESH).wait_recv()
        for wait in started:
            wait()

    return pl.pallas_call(
        body, name=name, in_specs=[_HBM] * n, out_specs=[_HBM] * n,
        out_shape=[jax.ShapeDtypeStruct(_gathered_shape(kinds[a], *shards[a].shape[1:]), shards[a].dtype)
                   for a in range(n)],
        scratch_shapes=[pltpu.SemaphoreType.DMA((3 * n,)), pltpu.SemaphoreType.DMA((3 * n,)),
                        pltpu.SemaphoreType.DMA((n,))],
        )(*shards)


def scatter_pieces(name, pieces):
    n = len(pieces)

    def body(*refs):
        p_refs, o_refs = refs[:n], refs[n:2 * n]
        send_sems, recv_sems, local_sems = refs[2 * n:]
        x, y, c, peers = _chip_peers()
        me = 2 * x + y
        started = []
        for a in range(n):
            mine = pltpu.make_async_copy(p_refs[a].at[me], o_refs[a].at[me], local_sems.at[a])
            mine.start()
            started.append(mine.wait)
            for k, (px, py) in enumerate(peers):
                cp = pltpu.make_async_remote_copy(
                    src_ref=p_refs[a].at[2 * px + py], dst_ref=o_refs[a].at[me], send_sem=send_sems.at[3 * a + k],
                    recv_sem=recv_sems.at[3 * a + k], device_id=(px, py, c), device_id_type=MESH)
                cp.start()
                started.append(cp.wait_send)
        for a in range(n):
            for k, (px, py) in enumerate(peers):
                pltpu.make_async_remote_copy(
                    src_ref=p_refs[a].at[me], dst_ref=o_refs[a].at[2 * px + py], send_sem=send_sems.at[3 * a + k],
                    recv_sem=recv_sems.at[3 * a + k], device_id=(px, py, c), device_id_type=MESH).wait_recv()
        for wait in started:
            wait()

    return pl.pallas_call(
        body, name=name, in_specs=[_HBM] * n, out_specs=[_HBM] * n,
        out_shape=[jax.ShapeDtypeStruct(p.shape, p.dtype) for p in pieces],
        scratch_shapes=[pltpu.SemaphoreType.DMA((3 * n,)), pltpu.SemaphoreType.DMA((3 * n,)),
                        pltpu.SemaphoreType.DMA((n,))],
        )(*pieces)


def _pair_step(n_steps, x_ref, land, send_sems, recv_sems, credits, consume):
    x, y, c = lax.axis_index("x"), lax.axis_index("y"), lax.axis_index("c")
    sib = (x, y, 1 - c)
    i = pl.program_id(0)
    slot = i % 2

    @pl.when(i >= 2)
    def _():
        pl.semaphore_wait(credits.at[slot], 1)

    cp = pltpu.make_async_remote_copy(src_ref=x_ref, dst_ref=land.at[slot], send_sem=send_sems.at[slot],
                                      recv_sem=recv_sems.at[slot], device_id=sib, device_id_type=MESH)
    cp.start()
    cp.wait_recv()
    consume(land[slot])

    @pl.when(i < n_steps - 2)
    def _():
        pl.semaphore_signal(credits.at[slot], inc=1, device_id=sib, device_id_type=MESH)

    cp.wait_send()


def _pair_call(name, body, n_steps, in_specs, out_spec, out_shape, blk_shape, dtype, operands, extra_scratch=()):
    grid_spec = pltpu.PrefetchScalarGridSpec(
        num_scalar_prefetch=1, grid=(n_steps,), in_specs=in_specs, out_specs=out_spec,
        scratch_shapes=[pltpu.VMEM((2,) + blk_shape, dtype), pltpu.SemaphoreType.DMA((2,)),
                        pltpu.SemaphoreType.DMA((2,)), pltpu.SemaphoreType.REGULAR((2,)), *extra_scratch])
    return pl.pallas_call(body, name=name, grid_spec=grid_spec, out_shape=out_shape,
                          compiler_params=_cp(("arbitrary",)))(*operands)


def _place():
    return jnp.stack([lax.axis_index("x"), lax.axis_index("y"), lax.axis_index("c")]).astype(jnp.int32)


def _pair_rows(rows, row_bytes):
    for cand in (4096, 2048, 1024, 768, 512, 384, 256, 192, 128, 96, 64, 48, 32, 16):
        if rows % cand == 0 and cand * row_bytes <= PAIR_BLOCK_BYTES:
            return cand
    return _pick(rows, (16, 8))


def exchange_both(name, mine):
    rows, cols = mine.shape
    tr = _pair_rows(rows, cols * mine.dtype.itemsize)
    n_steps = rows // tr

    def body(s_ref, x_ref, o_ref, land, send_sems, recv_sems, credits):
        c = lax.axis_index("c")
        o_ref[c] = x_ref[...]

        def consume(v):
            o_ref[1 - c] = v
        _pair_step(n_steps, x_ref, land, send_sems, recv_sems, credits, consume)

    return _pair_call(name, body, n_steps, [pl.BlockSpec((tr, cols), lambda i, s: (i, 0))],
                      pl.BlockSpec((2, tr, cols), lambda i, s: (0, i, 0)),
                      jax.ShapeDtypeStruct((2, rows, cols), mine.dtype), (tr, cols), mine.dtype, (_place(), mine))


def exchange_add(name, layer0, layer1):
    rows, cols = layer0.shape
    tr = _pair_rows(rows, cols * layer0.dtype.itemsize)
    nb = rows // tr

    def body(s_ref, l0_ref, l1_ref, o_ref, land, send_sems, recv_sems, credits, send_buf):
        first = lax.axis_index("c") == 0
        send_buf[...] = jnp.where(first, l1_ref[...], l0_ref[...])
        mine = jnp.where(first, l0_ref[...], l1_ref[...]).astype(F32)

        def consume(v):
            o_ref[...] = (mine + v.astype(F32)).astype(o_ref.dtype)
        _pair_step(nb, send_buf, land, send_sems, recv_sems, credits, consume)

    spec = pl.BlockSpec((tr, cols), lambda i, s: (i, 0))
    return _pair_call(name, body, nb, [spec, spec], spec, jax.ShapeDtypeStruct((rows, cols), CDT),
                      (tr, cols), layer0.dtype, (_place(), layer0, layer1),
                      extra_scratch=(pltpu.VMEM((tr, cols), layer0.dtype),))


def sum_exchange(name, parts):
    npart, rows, cols = parts.shape
    tr = _pair_rows(rows, cols * 4)
    n_steps = rows // tr

    def body(s_ref, x_ref, o_ref, land, send_sems, recv_sems, credits, mine):
        c = lax.axis_index("c")
        acc = x_ref[0].astype(F32)
        for k in range(1, npart):
            acc = acc + x_ref[k].astype(F32)
        mine[...] = acc
        o_ref[c] = acc

        def consume(v):
            o_ref[1 - c] = v
        _pair_step(n_steps, mine, land, send_sems, recv_sems, credits, consume)

    return _pair_call(name, body, n_steps, [pl.BlockSpec((npart, tr, cols), lambda i, s: (0, i, 0))],
                      pl.BlockSpec((2, tr, cols), lambda i, s: (0, i, 0)),
                      jax.ShapeDtypeStruct((2, rows, cols), F32), (tr, cols), F32, (_place(), parts),
                      extra_scratch=(pltpu.VMEM((tr, cols), F32),))


def allreduce_small(name, buf):
    rows = buf.shape[0]

    def body(x_ref, out_ref, gath, send_sems, recv_sems):
        x, y, c = lax.axis_index("x"), lax.axis_index("y"), lax.axis_index("c")
        me = 4 * x + 2 * y + c
        masks = [(k >> 2 & 1, k >> 1 & 1, k & 1) for k in range(1, 8)]

        def flip(v, bit):
            return 1 - v if bit else v

        sends = []
        for k, (bx, by, bc) in enumerate(masks):
            cp = pltpu.make_async_remote_copy(src_ref=x_ref, dst_ref=gath.at[me], send_sem=send_sems.at[k],
                                              recv_sem=recv_sems.at[k],
                                              device_id=(flip(x, bx), flip(y, by), flip(c, bc)), device_id_type=MESH)
            cp.start()
            sends.append(cp)
        gath[me] = x_ref[...]
        for k, (bx, by, bc) in enumerate(masks):
            px, py, pc = flip(x, bx), flip(y, by), flip(c, bc)
            pltpu.make_async_remote_copy(src_ref=x_ref, dst_ref=gath.at[4 * px + 2 * py + pc],
                                         send_sem=send_sems.at[k], recv_sem=recv_sems.at[k],
                                         device_id=(px, py, pc), device_id_type=MESH).wait_recv()
        for cp in sends:
            cp.wait_send()
        acc = gath[0]
        for d in range(1, 8):
            acc = acc + gath[d]
        out_ref[...] = acc

    return pl.pallas_call(
        body, name=name, in_specs=[pl.BlockSpec(memory_space=pltpu.VMEM)],
        out_specs=pl.BlockSpec(memory_space=pltpu.VMEM), out_shape=jax.ShapeDtypeStruct(buf.shape, F32),
        scratch_shapes=[pltpu.VMEM((8, rows, LANES), F32), pltpu.SemaphoreType.DMA((7,)),
                        pltpu.SemaphoreType.DMA((7,))],
        )(buf)


def _pack_flat(arrs, dtype, width, row_mult=8):
    flat = jnp.concatenate([a.reshape(-1).astype(dtype) for a in arrs])
    pad = (-flat.shape[0]) % (row_mult * width)
    if pad:
        flat = jnp.concatenate([flat, jnp.zeros((pad,), dtype)])
    return flat.reshape(-1, width)


def _unpack_flat(buf, shapes):
    flat = buf.reshape(-1)
    out, off = [], 0
    for s in shapes:
        n = math.prod(s)
        out.append(flat[off:off + n].reshape(s))
        off += n
    return out


def _in_to_padded(w):
    parts = []
    for name in IN_NEW_ORDER:
        _, width, o_off, o_w = IN_LAYOUT[name]
        parts.append(w[..., o_off:o_off + o_w])
        if o_w < width:
            parts.append(jnp.zeros(w.shape[:-1] + (width - o_w,), w.dtype))
    used = sum(IN_LAYOUT[n][1] for n in IN_NEW_ORDER)
    parts.append(jnp.zeros(w.shape[:-1] + (IN_PAD - used,), w.dtype))
    return jnp.concatenate(parts, axis=-1)


def _in_from_padded(g):
    parts = []
    for name in IN_ORIG_ORDER:
        off, _, _, o_w = IN_LAYOUT[name]
        parts.append(g[..., off:off + o_w])
    return jnp.concatenate(parts, axis=-1)


def _pcol(p, name):
    off, width, _, _ = IN_LAYOUT[name]
    return Cols(p, off, width)


def _lane_pad(v, width=LANES):
    v = v.reshape(-1)
    return jnp.concatenate([v, jnp.zeros((width - v.shape[0],), v.dtype)]).reshape(1, width)


def _f_norm_mod(h, sh, sc, w):
    return (_rms(h, w) * (1.0 + sc) + sh,)


def _f_norm_mod_thru(h, sh, sc, w):
    return h, _rms(h, w) * (1.0 + sc) + sh


def _f_attn_prep(qraw, kraw, vraw, cos2, sin2, qw, kw, gq, gk):
    q = qraw * lax.rsqrt(_group_mean(qraw * qraw, gq) + NORM_EPS) * qw
    q = _rope32(q, jnp.tile(cos2, (1, 4)), jnp.tile(sin2, (1, 4))) * (ATTN_HD ** -0.5 * LOG2E)
    k = kraw * lax.rsqrt(_group_mean(kraw * kraw, gk) + NORM_EPS) * kw
    return q, _rope32(k, cos2, sin2), vraw


def _f_ssd_finish(yf, yb, xs, z, d_exp, nw):
    y = (yf + yb + d_exp * xs) * (z * jax.nn.sigmoid(z))
    return (_rms(y, nw),)


def _f_ret_prep(rq, rk, cos1, sin1):
    cos_full, sin_signed = jnp.tile(cos1, (1, 4)), jnp.tile(sin1, (1, 4))
    return _rope64(rq, cos_full, sin_signed), _rope64(rk, cos_full, sin_signed) * (RET_DK ** -0.5)


def _f_ret_finish(yf, yb, g, gw):
    y = yf + yb
    outs = []
    for h in range(RET_HEADS):
        yh = y[:, h * RET_DK:(h + 1) * RET_DK]
        yc = yh - jnp.mean(yh, axis=-1, keepdims=True)
        outs.append(yc * lax.rsqrt(jnp.mean(yc * yc, axis=-1, keepdims=True) + NORM_EPS))
    return (jnp.concatenate(outs, axis=1) * gw * (g * jax.nn.sigmoid(g)),)


def _f_merge(p0, p1, p2, g0, g1, g2):
    return (jax.nn.sigmoid(g0) * p0 + jax.nn.sigmoid(g1) * p1 + jax.nn.sigmoid(g2) * p2,)


def _f_mid(h, mix, g1, sh2, sc2, w2):
    h_mid = h + g1 * mix
    return h_mid, _rms(h_mid, w2) * (1.0 + sc2) + sh2


def _epi_mid(mix, h, g1, sh2, sc2, w2):
    return (mix,) + _f_mid(h, mix, g1, sh2, sc2, w2)


def _epi_resid(o, h_mid, g2):
    return (o,) + _f_residual(h_mid, o, g2)


def _epi_sqrelu(a):
    r = jnp.maximum(a, 0.0)
    return a, r * r


def _epi_sqrelu_bwd(dhh, a):
    return (dhh * (2.0 * jnp.maximum(a.astype(F32), 0.0)),)


def _f_residual(h_mid, o, g2):
    return (h_mid + g2 * o,)


def _f_silu(x):
    return (x * jax.nn.sigmoid(x),)


def _f_bias(x, b):
    return (x + b,)


def _ssd_rows(xbc, p):
    rows = [Cols(xbc, LANES * k, LANES) for k in range(4)]
    rows += [Cols(xbc, 512 + LANES * g, LANES) for g in range(2)]
    rows += [Cols(xbc, 768 + LANES * g, LANES) for g in range(2)]
    return rows + [_pcol(p, "dt")]


def _ret_rows(rq, rk, p):
    off_v = IN_LAYOUT["rv"][0]
    return ([Cols(rq, LANES * h, LANES) for h in range(4)] + [Cols(rk, LANES * h, LANES) for h in range(4)]
            + [Cols(p, off_v + LANES * h, LANES) for h in range(4)])


def layer_fwd(li, h, mod, lw, tabs, m_ctx):
    t = h.shape[0]
    nb = m_ctx // min(ROW_TILE, t)
    sh1, sc1, g1, sh2, sc2, g2 = mod
    nm = lambda s: f"l{li}_{s}"
    sv = {}
    (u,) = rowwise_fwd(nm("norm1"), _f_norm_mod, [h], [sh1, sc1], [lw["norm1_w"]], [(D_MODEL, CDT)], t, nb)
    p = mm(nm("in_proj"), u, lw["w_in"], F32)
    q, k, v = rowwise_fwd(
        nm("attn_prep"), _f_attn_prep,
        [_pcol(p, "q"), _pcol(p, "k"), _pcol(p, "v"), tabs["ca"], tabs["sa"]], [],
        [lw["qw"], lw["kw"], tabs["gq"], tabs["gk"]], [(512, CDT), (128, CDT), (128, CDT)], t, nb)
    tq = min(ATTN_TQ, m_ctx)
    kk, vv = _split_kv(k), _split_kv(v)
    ones_rows = jnp.concatenate([jnp.ones((2, 1, t), CDT), jnp.zeros((2, ATTN_ONES_ROWS - 1, t), CDT)], axis=1)
    attn_o, qT, oT, lse = attn_fwd(nm("attn"), q, kk, jnp.concatenate([vv.transpose(0, 2, 1), ones_rows], axis=1),
                                   m_ctx, tq)

    xbc = conv_fwd(_pcol(p, "xbc"), lw["conv_w"], lw["conv_b"], m_ctx)
    ssd_sh = [lw["dt_bias"], lw["a_log"]]
    yf, yb, sf, sb = scan_fwd(nm("ssd"), _make_ssd_chunk, _ssd_rows(xbc, p), ssd_sh, 4, 512, t, m_ctx)
    (ssd_o,) = rowwise_fwd(nm("ssd_fin"), _f_ssd_finish, [yf, yb, Cols(xbc, 0, 512), _pcol(p, "z")], [],
                           [lw["d_exp"], lw["ssd_nw"]], [(512, CDT)], t, nb)

    rq, rk = rowwise_fwd(nm("ret_prep"), _f_ret_prep, [_pcol(p, "rq"), _pcol(p, "rk"), tabs["rc"], tabs["rs"]],
                         [], [], [(512, F32), (512, F32)], t, nb)
    rf, rb, rsf, rsb = scan_fwd(nm("ret"), _make_ret_chunk, _ret_rows(rq, rk, p), [lw["ret_lg"]], 4, 512, t, m_ctx)
    (ret_o,) = rowwise_fwd(nm("ret_fin"), _f_ret_finish, [rf, rb, _pcol(p, "rg")], [], [lw["ret_gw"]],
                           [(512, CDT)], t, nb)

    pbs = [mm(nm(f"branch{b}"), br, lw["w_branch"][b], CDT) for b, br in enumerate((attn_o, ssd_o, ret_o))]
    gl = [Cols(p, 1024 * b, 1024) for b in range(3)]
    (merged,) = rowwise_fwd(nm("merge"), _f_merge, pbs + gl, [], [], [(D_MODEL, CDT)], t, nb)
    mix, h_mid, vv2 = mm(nm("out_proj"), merged, lw["w_out"], None, epilogue=_epi_mid, tile_ins=[h],
                         typed_ins=[g1, sh2, sc2], shared_ins=[lw["norm2_w"]], out_dtypes=[F32, F32, CDT], nb_ctx=nb,
                         whole_rows=True)
    a, hh = mm(nm("mlp1"), vv2, lw["w_mlp1"], None, epilogue=_epi_sqrelu, out_dtypes=[CDT, CDT])
    o, h_out = mm(nm("mlp2"), hh, lw["w_mlp2"], None, epilogue=_epi_resid, tile_ins=[h_mid], typed_ins=[g2],
                  out_dtypes=[F32, F32], nb_ctx=nb)
    sv.update(h=h, u=u, p=p, qT=qT, kk=kk, vv=vv, oT=oT, lse=lse, attn_o=attn_o, xbc=xbc, yf=yf, yb=yb,
              sf=sf, sb=sb, ssd_o=ssd_o, rq=rq, rk=rk, rf=rf, rb=rb, rsf=rsf, rsb=rsb, ret_o=ret_o, pbs=pbs,
              merged=merged, mix=mix, h_mid=h_mid, v=vv2, a=a, hh=hh, o=o)
    return h_out, sv


def layer_bwd(li, dh_out, sv, mod, lw, tabs, m_ctx):
    t = dh_out.shape[0]
    nb = m_ctx // min(ROW_TILE, t)
    sh1, sc1, g1, sh2, sc2, g2 = mod
    nm = lambda s: f"l{li}_{s}_bwd"
    gw = {}
    p = sv["p"]
    (do,), (dg2,), _ = rowwise_bwd(nm("resid"), _f_residual, [sv["h_mid"], sv["o"]], [g2], [], [dh_out],
                                   [False, True], [], [CDT], t, nb)
    (da,) = mm(nm("mlp2_dx"), do, lw["w_mlp2"], None, transpose_b=True, epilogue=_epi_sqrelu_bwd,
               tile_ins=[sv["a"]], out_dtypes=[CDT])
    gw["w_mlp2"] = mm_tn(nm("mlp2_dw"), sv["hh"], do, CDT, ("rows", lw["w_mlp2"].shape[0] // 4))
    dv = mm(nm("mlp1_dx"), da, lw["w_mlp1"], F32, transpose_b=True)
    gw["w_mlp1"] = mm_tn(nm("mlp1_dw"), sv["v"], da, CDT, ("cols", lw["w_mlp1"].shape[1] // 4))
    (dh_a, dmix), (dg1, dsh2, dsc2), (gw["norm2_w"],) = rowwise_bwd(
        nm("mid"), _f_mid, [sv["h"], sv["mix"]], [g1, sh2, sc2], [lw["norm2_w"]], [dh_out, dv],
        [True, True], [True], [F32, CDT], t, nb)
    dmerged = mm(nm("out_dx"), dmix, lw["w_out"], CDT, transpose_b=True)
    gw["w_out"] = mm_tn(nm("out_dw"), sv["merged"], dmix, CDT, ("rows", lw["w_out"].shape[0] // 4))
    gl = [Cols(p, 1024 * b, 1024) for b in range(3)]
    dmg, _, _ = rowwise_bwd(nm("merge"), _f_merge, sv["pbs"] + gl, [], [], [dmerged], [True] * 6, [], [CDT] * 6,
                            t, nb)
    dpb, dgl = dmg[:3], dmg[3:]
    brs = (sv["attn_o"], sv["ssd_o"], sv["ret_o"])
    d_attn_o = mm(nm("branch0_dx"), dpb[0], lw["w_branch"][0], CDT, transpose_b=True)
    d_ssd_o = mm(nm("branch1_dx"), dpb[1], lw["w_branch"][1], F32, transpose_b=True)
    d_ret_o = mm(nm("branch2_dx"), dpb[2], lw["w_branch"][2], F32, transpose_b=True)
    n_loc = lw["w_branch"].shape[2] // 4
    gw["w_branch"] = jnp.stack([mm_tn(nm(f"branch{b}_dw"), brs[b], dpb[b], CDT, ("cols", n_loc)) for b in range(3)],
                               axis=1).reshape(4, -1, n_loc)
    tq = min(ATTN_TQ, m_ctx)
    dq_rows, dk_s, dv_s = attn_bwd(nm("attn"), sv["qT"], d_attn_o, sv["oT"], sv["lse"], sv["kk"],
                                   sv["kk"].transpose(0, 2, 1), sv["vv"], m_ctx)
    (dq_raw, dk_raw, dv_raw), _, (gw["qw"], gw["kw"]) = rowwise_bwd(
        nm("attn_prep"), _f_attn_prep,
        [_pcol(p, "q"), _pcol(p, "k"), _pcol(p, "v"), tabs["ca"], tabs["sa"]], [],
        [lw["qw"], lw["kw"], tabs["gq"], tabs["gk"]],
        [dq_rows, _merge_kv(dk_s) * LN2, _merge_kv(dv_s)],
        [True, True, True, False, False], [True, True, False, False], [CDT] * 3, t, nb)
    (dy_ssd, dxs_fin, dz), _, (gw["d_exp"], gw["ssd_nw"]) = rowwise_bwd(
        nm("ssd_fin"), _f_ssd_finish, [sv["yf"], sv["yb"], Cols(sv["xbc"], 0, 512), _pcol(p, "z")], [],
        [lw["d_exp"], lw["ssd_nw"]], [d_ssd_o], [True, False, True, True], [True, True], [F32, F32, CDT], t, nb)
    ssd_sh = [lw["dt_bias"], lw["a_log"]]
    post_ssd = lambda d: [jnp.concatenate(d[0:8], axis=1), d[8]]
    (dxbc_f, ddt_f), dsh_f = scan_bwd(nm("ssd_f"), _make_ssd_chunk, _ssd_rows(sv["xbc"], p), ssd_sh, (sv["sf"],),
                                      dy_ssd, post_ssd, [(1024, F32), (LANES, F32)], 4, t, m_ctx, dirs=(0,))
    (dxbc_b, ddt_b), dsh_b = scan_bwd(nm("ssd_b"), _make_ssd_chunk, _ssd_rows(sv["xbc"], p), ssd_sh, (sv["sb"],),
                                      dy_ssd, post_ssd, [(1024, F32), (LANES, F32)], 4, t, m_ctx, dirs=(1,))
    gw["dt_bias"], gw["a_log"] = dsh_f[0] + dsh_b[0], dsh_f[1] + dsh_b[1]
    ddt = (ddt_f + ddt_b).astype(CDT)
    dxbc_raw, gw["conv_w"], gw["conv_b"] = conv_bwd(_pcol(p, "xbc"), lw["conv_w"], lw["conv_b"], dxbc_f, dxbc_b,
                                                    dxs_fin, m_ctx)
    (dy_ret, drg), _, (gw["ret_gw"],) = rowwise_bwd(
        nm("ret_fin"), _f_ret_finish, [sv["rf"], sv["rb"], _pcol(p, "rg")], [], [lw["ret_gw"]], [d_ret_o],
        [True, False, True], [True], [F32, CDT], t, nb)
    post_ret = lambda d: [jnp.concatenate(d[0:4], axis=1), jnp.concatenate(d[4:8], axis=1),
                          jnp.concatenate(d[8:12], axis=1)]
    rrows = _ret_rows(sv["rq"], sv["rk"], p)
    (dq_f, dk_f, dv_f), (dq_b, dk_b, dv_b), (gw["ret_lg"],) = scan_bwd(
        nm("ret"), _make_ret_chunk, rrows, [lw["ret_lg"]], (sv["rsf"], sv["rsb"]), dy_ret, post_ret,
        [(512, F32)] * 3, 4, t, m_ctx)
    drv = (dv_f + dv_b).astype(CDT)
    (drq, drk), _, _ = rowwise_bwd(nm("ret_prep"), _f_ret_prep,
                                   [_pcol(p, "rq"), _pcol(p, "rk"), tabs["rc"], tabs["rs"]], [], [],
                                   [(dq_f, dq_b), (dk_f, dk_b)], [True, True, False, False], [], [CDT, CDT], t, nb)
    pieces = {"gates": None, "xbc": dxbc_raw, "q": dq_raw, "z": dz, "rq": drq, "rk": drk, "rv": drv, "rg": drg,
              "k": dk_raw, "v": dv_raw, "dt": ddt}
    cols = list(dgl) + [pieces[n] for n in IN_NEW_ORDER[1:]]
    used = sum(c.shape[1] for c in cols)
    cols.append(jnp.zeros((t, IN_PAD - used), CDT))
    dp = jnp.concatenate(cols, axis=1)
    du = mm(nm("in_dx"), dp, lw["w_in"], F32, transpose_b=True)
    gw["w_in"] = mm_tn(nm("in_dw"), sv["u"], dp, CDT)
    (dh_in,), (dsh1, dsc1), (gw["norm1_w"],) = rowwise_bwd(
        nm("norm1"), _f_norm_mod_thru, [sv["h"]], [sh1, sc1], [lw["norm1_w"]], [dh_a, du], [True], [True], [F32],
        t, nb)
    return dh_in, [dsh1, dsc1, dg1, dsh2, dsc2, dg2], gw


def _rope_tables(n_lat, m_ctx):
    rows = n_lat // GRID_W
    row = jnp.repeat(jnp.arange(rows, dtype=F32), GRID_W)
    col = jnp.tile(jnp.arange(GRID_W, dtype=F32), rows)
    nfreq = ATTN_HD // 4
    inv = ROPE_THETA ** (-jnp.arange(nfreq, dtype=F32) / nfreq)
    ang = jnp.concatenate([row[:, None] * inv, col[:, None] * inv], axis=-1)
    cos = jnp.concatenate([jnp.ones((m_ctx, ATTN_HD // 2), F32), jnp.cos(ang)], axis=0)
    sin = jnp.concatenate([jnp.zeros((m_ctx, ATTN_HD // 2), F32), jnp.sin(ang)], axis=0)
    c64 = jnp.concatenate([cos, cos], axis=1)
    s64 = jnp.concatenate([-sin, sin], axis=1)
    pos = jnp.arange(m_ctx + n_lat, dtype=F32)
    inv_r = ROPE_THETA ** (-jnp.linspace(0.0, 1.0, RET_DK // 2, dtype=F32))
    ang_r = pos[:, None] * inv_r
    rc = jnp.concatenate([jnp.cos(ang_r)] * 2, axis=1)
    rs = jnp.concatenate([-jnp.sin(ang_r), jnp.sin(ang_r)], axis=1)
    return dict(ca=jnp.tile(c64, (1, 2)), sa=jnp.tile(s64, (1, 2)), rc=rc, rs=rs, gq=_group_matrix(512, ATTN_HD),
                gk=_group_matrix(128, ATTN_HD))


def _layer_weights(full, small, layer):
    return dict(
        w_in=full["w_in"][layer], w_branch=full["w_branch"][layer], w_out=full["w_out"][layer],
        w_mlp1=full["w_mlp1"][layer], w_mlp2=full["w_mlp2"][layer],
        norm1_w=small["norm1_w"][layer][None], norm2_w=small["norm2_w"][layer][None],
        qw=jnp.tile(small["attn_q_norm"][layer], 8)[None], kw=jnp.tile(small["attn_k_norm"][layer], 2)[None],
        conv_w=small["ssd_conv_w"][layer], conv_b=small["ssd_conv_b"][layer][None],
        dt_bias=_lane_pad(small["ssd_dt_bias"][layer]), a_log=_lane_pad(small["ssd_a_log"][layer]),
        d_exp=jnp.repeat(small["ssd_d"][layer], SSD_HD)[None], ssd_nw=small["ssd_norm_w"][layer][None],
        ret_lg=_lane_pad(small["ret_log_decay"][layer]), ret_gw=small["ret_gn_w"][layer][None])


def local_step(x, c, ctx, full, small, loss_target):
    n_lat, d = x.shape
    m_ctx = ctx.shape[0]
    t = n_lat + m_ctx
    depth = small["norm1_w"].shape[0]
    tabs = _rope_tables(n_lat, m_ctx)
    h = jnp.concatenate([ctx, x], axis=0)
    cc = jnp.concatenate([small["c_ctx"][None], c, jnp.zeros((COND_ROWS - 2, d), F32)], axis=0)
    (scc,) = rowwise_fwd("cond_silu", _f_silu, [cc], [], [], [(d, CDT)], COND_ROWS, 0)
    mods, saved, lws = [], [], []
    for layer in range(depth):
        lw = _layer_weights(full, small, layer)
        mod_raw = mm(f"l{layer}_mod", scc, full["w_mod"][layer], F32)
        (mod8,) = rowwise_fwd(f"l{layer}_mod_bias", _f_bias, [mod_raw], [], [small["b_mod"][layer][None]],
                              [(6 * d, F32)], COND_ROWS, 0)
        mod = [mod8[0:2, k * d:(k + 1) * d].reshape(2, 1, d) for k in range(6)]
        h, sv = layer_fwd(layer, h, mod, lw, tabs, m_ctx)
        mods.append(mod)
        saved.append(sv)
        lws.append(lw)
    loss, dh, d_final = loss_head(h, loss_target, small["final_norm_w"][None], m_ctx)

    gbig = {k: [None] * depth for k in BIG}
    gs = {k: [None] * depth for k in SMALL if k not in ("c_ctx", "final_norm_w")}
    d_scc = None
    for layer in reversed(range(depth)):
        lw = lws[layer]
        dh, dmod, gw = layer_bwd(layer, dh, saved[layer], mods[layer], lw, tabs, m_ctx)
        dmod8 = jnp.concatenate([jnp.concatenate([g_.reshape(2, d) for g_ in dmod], axis=1),
                                 jnp.zeros((COND_ROWS - 2, 6 * d), F32)], axis=0)
        (dmod_c,), _, (db_mod,) = rowwise_bwd(f"l{layer}_mod_bias_bwd", _f_bias, [dmod8], [],
                                              [small["b_mod"][layer][None]], [dmod8], [True], [True], [CDT], COND_ROWS, 0)
        gbig["w_mod"][layer] = mm_tn(f"l{layer}_mod_dw", scc, dmod_c, CDT, ("cols", 6 * d // 4))
        part = mm(f"l{layer}_mod_dx", dmod_c, full["w_mod"][layer], F32, transpose_b=True)
        d_scc = part if d_scc is None else d_scc + part
        g_in = _in_from_padded(gw["w_in"])
        gbig["w_in"][layer] = g_in.reshape(d, 4, g_in.shape[1] // 4).transpose(1, 0, 2)
        for k in ("w_branch", "w_out", "w_mlp1", "w_mlp2"):
            gbig[k][layer] = gw[k]
        gs["b_mod"][layer] = db_mod.reshape(-1)
        gs["norm1_w"][layer] = gw["norm1_w"].reshape(-1)
        gs["norm2_w"][layer] = gw["norm2_w"].reshape(-1)
        gs["attn_q_norm"][layer] = gw["qw"].reshape(8, ATTN_HD).sum(0)
        gs["attn_k_norm"][layer] = gw["kw"].reshape(2, ATTN_HD).sum(0)
        gs["ssd_conv_w"][layer] = gw["conv_w"]
        gs["ssd_conv_b"][layer] = gw["conv_b"].reshape(-1)
        gs["ssd_dt_bias"][layer] = gw["dt_bias"][0, :16].reshape(2, 8)
        gs["ssd_a_log"][layer] = gw["a_log"][0, :16].reshape(2, 8)
        gs["ssd_d"][layer] = gw["d_exp"].reshape(SSD_HEADS, SSD_HD).sum(1)
        gs["ssd_norm_w"][layer] = gw["ssd_nw"].reshape(-1)
        gs["ret_log_decay"][layer] = gw["ret_lg"][0, :8].reshape(2, 4)
        gs["ret_gn_w"][layer] = gw["ret_gw"].reshape(-1)
    (d_cc,), _, _ = rowwise_bwd("cond_silu_bwd", _f_silu, [cc], [], [], [d_scc], [True], [], [F32], COND_ROWS, 0)
    g_small = {k: jnp.stack(v) for k, v in gs.items()}
    g_small["c_ctx"] = d_cc[0]
    g_small["final_norm_w"] = d_final.reshape(-1)
    return loss, dh[m_ctx:], gbig, g_small


def kernel(x, c, ctx, c_ctx, w_mod, b_mod, norm1_w, norm2_w, w_in, attn_q_norm, attn_k_norm, ssd_conv_w, ssd_conv_b, ssd_dt_bias, ssd_a_log, ssd_d, ssd_norm_w, ret_log_decay, ret_gn_w, w_branch, w_out, w_mlp1, w_mlp2, final_norm_w, loss_target, m_c_ctx, m_w_mod, m_b_mod, m_norm1_w, m_norm2_w, m_w_in, m_attn_q_norm, m_attn_k_norm, m_ssd_conv_w, m_ssd_conv_b, m_ssd_dt_bias, m_ssd_a_log, m_ssd_d, m_ssd_norm_w, m_ret_log_decay, m_ret_gn_w, m_w_branch, m_w_out, m_w_mlp1, m_w_mlp2, m_final_norm_w, v_c_ctx, v_w_mod, v_b_mod, v_norm1_w, v_norm2_w, v_w_in, v_attn_q_norm, v_attn_k_norm, v_ssd_conv_w, v_ssd_conv_b, v_ssd_dt_bias, v_ssd_a_log, v_ssd_d, v_ssd_norm_w, v_ret_log_decay, v_ret_gn_w, v_w_branch, v_w_out, v_w_mlp1, v_w_mlp2, v_final_norm_w):
    env = dict(locals())
    w_loc = {k: env[k] for k in WEIGHTS}
    m_loc = {k: env["m_" + k] for k in WEIGHTS}
    v_loc = {k: env["v_" + k] for k in WEIGHTS}
    chip = 2 * lax.axis_index("x") + lax.axis_index("y")
    core = lax.axis_index("c")

    depth = w_loc["w_mod"].shape[0]
    assert depth == 2, "the exchanges split the layers between a chip's two cores"
    shards = [w_loc[k].astype(CDT).reshape(depth, -1, w_loc[k].shape[-1]) for k in BIG]
    mine = gather_layers("gather_weights", shards, [BIG_KIND[k] for k in BIG])
    full = {}
    for k, arr in zip(BIG, mine):
        both = exchange_both("share_" + k, arr.reshape(-1, arr.shape[-1]))
        if k == "w_in":
            both = both.reshape(depth, 4, -1, both.shape[-1]).transpose(0, 2, 1, 3)
            both = _in_to_padded(both.reshape(depth, both.shape[1], -1))
        full[k] = both.reshape((depth,) + w_loc[k].shape[1:-1] + (-1,)) if BIG_KIND[k] == "cols" else \
            both.reshape((depth,) + w_loc[k].shape[1:-2] + (-1, w_loc[k].shape[-1])) if BIG_KIND[k] == "rows" else both

    cw = w_loc["ssd_conv_w"]
    cw_w = cw.shape[-1]
    placed = lax.dynamic_update_slice(jnp.zeros(cw.shape[:-1] + (4 * cw_w,), F32),
                                      cw * (core == 0).astype(F32), (0, 0, chip * cw_w))
    conv_full = _unpack_flat(allreduce_small("gather_conv_w", _pack_flat([placed], F32, LANES)), [placed.shape])[0]
    small = {k: w_loc[k] for k in SMALL}
    small["ssd_conv_w"] = conv_full

    loss_l, grad_x, g_big, g_small = local_step(x[0], c, ctx[0], full, small, loss_target[0])

    small_shapes = [g_small[k].shape for k in SMALL] + [(LANES,)]
    summed = _unpack_flat(allreduce_small("reduce_small", _pack_flat([g_small[k] for k in SMALL] + [loss_l], F32, LANES)),
                          small_shapes)
    gsum = dict(zip(SMALL, summed[:-1]))
    loss = summed[-1][0]
    gsum["ssd_conv_w"] = lax.dynamic_slice(gsum["ssd_conv_w"], (0, 0, chip * cw_w), cw.shape)

    pair = []
    for k in BIG:
        _, rows, cols = g_big[k][0].shape
        pair.append(exchange_add("pair_" + k, g_big[k][0].reshape(4 * rows, cols),
                                 g_big[k][1].reshape(4 * rows, cols)).reshape(4, rows, cols))
    landed = scatter_pieces("scatter_grads", pair)
    g_sum = [sum_exchange("sum_" + k, parts) for k, parts in zip(BIG, landed)]

    grads, deltas, new_m, new_v = {}, {}, {}, {}
    for i, k in enumerate(BIG):
        shp = w_loc[k].shape
        three_d = lambda a, shp=shp: a.reshape((-1,) + shp[-2:])
        res = adamw("adamw_" + k, three_d(w_loc[k]), three_d(m_loc[k]), three_d(v_loc[k]), [three_d(g_sum[i])])
        grads[k], deltas[k], new_m[k], new_v[k] = [r.reshape(shp) for r in res]
    small_loc_shapes = [w_loc[k].shape for k in SMALL]
    res = adamw("adamw_small", _pack_flat([w_loc[k] for k in SMALL], F32, LANES)[None],
                _pack_flat([m_loc[k] for k in SMALL], F32, LANES)[None],
                _pack_flat([v_loc[k] for k in SMALL], F32, LANES)[None],
                [_pack_flat([gsum[k] for k in SMALL], F32, LANES)[None]])
    for dst, r in zip((grads, deltas, new_m, new_v), res):
        dst.update(dict(zip(SMALL, _unpack_flat(r, small_loc_shapes))))

    return (loss, grad_x[None], *[grads[k] for k in WEIGHTS], *[deltas[k] for k in WEIGHTS],
            *[new_m[k] for k in WEIGHTS], *[new_v[k] for k in WEIGHTS])
```

```python
import functools
import math
from typing import NamedTuple

import jax
import jax.numpy as jnp
from jax import lax
from jax.experimental import pallas as pl
from jax.experimental.pallas import tpu as pltpu

F32 = jnp.float32
CDT = jnp.bfloat16
NORM_EPS = 1e-6
ROPE_THETA = 10000.0
GRID_W = 64
D_MODEL = 1024
ATTN_HEADS, ATTN_KV, ATTN_HD = 8, 2, 64
SSD_HEADS, SSD_HD, SSD_STATE = 8, 64, 128
RET_HEADS, RET_DK = 4, 128
CHUNK = 256
ROW_TILE = 256
MM_ROWS = 768
MM_TN_ROWS = 2816
MM_VMEM_BUDGET = 44 * 1024 * 1024
ATTN_TQ, ATTN_TK = 256, 256
ATTN_ONES_ROWS = 16
ATTN_TK_BWD = 2048
LOG2E, LN2 = 1.4426950408889634, 0.6931471805599453
ATTN_TK_FWD = 2048
LANES = 128
PAIR_BLOCK_BYTES = 2 * 1024 * 1024
COND_ROWS = 16
VMEM_LIMIT = 56 * 1024 * 1024

ADAM_LR, ADAM_B1, ADAM_B2, ADAM_EPS, ADAM_WD, ADAM_STEP = 0.001, 0.9, 0.999, 1e-08, 0.01, 10

IN_LAYOUT = {
    "gates": (0, 3072, 4368, 3072), "xbc": (3072, 1024, 1280, 1024), "q": (4096, 512, 0, 512),
    "z": (4608, 512, 768, 512), "rq": (5120, 512, 2320, 512), "rk": (5632, 512, 2832, 512),
    "rv": (6144, 512, 3344, 512), "rg": (6656, 512, 3856, 512), "k": (7168, 128, 512, 128),
    "v": (7296, 128, 640, 128), "dt": (7424, 128, 2304, 16),
}
IN_PAD = 7680
IN_ORIG_ORDER = ("q", "k", "v", "z", "xbc", "dt", "rq", "rk", "rv", "rg", "gates")
IN_NEW_ORDER = ("gates", "xbc", "q", "z", "rq", "rk", "rv", "rg", "k", "v", "dt")

BIG = ("w_mod", "w_in", "w_branch", "w_out", "w_mlp1", "w_mlp2")
BIG_KIND = {"w_mod": "cols", "w_in": "slices", "w_branch": "cols", "w_out": "rows", "w_mlp1": "cols", "w_mlp2": "rows"}
SMALL = ("c_ctx", "b_mod", "norm1_w", "norm2_w", "attn_q_norm", "attn_k_norm", "ssd_conv_w", "ssd_conv_b",
         "ssd_dt_bias", "ssd_a_log", "ssd_d", "ssd_norm_w", "ret_log_decay", "ret_gn_w", "final_norm_w")
WEIGHTS = ("c_ctx", "w_mod", "b_mod", "norm1_w", "norm2_w", "w_in", "attn_q_norm", "attn_k_norm", "ssd_conv_w",
           "ssd_conv_b", "ssd_dt_bias", "ssd_a_log", "ssd_d", "ssd_norm_w", "ret_log_decay", "ret_gn_w",
           "w_branch", "w_out", "w_mlp1", "w_mlp2", "final_norm_w")


def _cp(sem):
    return pltpu.CompilerParams(dimension_semantics=sem, vmem_limit_bytes=VMEM_LIMIT)


class Cols(NamedTuple):
    arr: jax.Array
    off: int
    width: int


def _width(item):
    return item.width if isinstance(item, Cols) else item.shape[1]


def _row_in(item, rows, imap=None):
    imap = imap or (lambda i: i)
    if isinstance(item, Cols):
        assert item.off % item.width == 0
        blk = item.off // item.width
        return item.arr, pl.BlockSpec((rows, item.width), lambda i, blk=blk: (imap(i), blk))
    return item, pl.BlockSpec((rows, item.shape[1]), lambda i: (imap(i), 0))


def _const_spec(shape):
    return pl.BlockSpec(shape, lambda *_: (0,) * len(shape))


def _mxu(a, b, dims=(((1,), (0,)), ((), ()))):
    return lax.dot_general(a.astype(CDT), b.astype(CDT), dims, preferred_element_type=F32)


_NT = (((1,), (1,)), ((), ()))
_TN = (((0,), (0,)), ((), ()))


@jax.custom_vjp
def _softplus(x):
    return jnp.maximum(x, 0.0) + jnp.log1p(jnp.exp(-jnp.abs(x)))


def _softplus_fwd(x):
    return _softplus(x), x


def _softplus_bwd(x, g):
    return (g * jax.nn.sigmoid(x),)


_softplus.defvjp(_softplus_fwd, _softplus_bwd)


def _group_mean_impl(x, gmat):
    hi = x.astype(CDT)
    lo = (x - hi.astype(F32)).astype(CDT)
    return (jnp.dot(hi, gmat, preferred_element_type=F32) + jnp.dot(lo, gmat, preferred_element_type=F32))


@jax.custom_vjp
def _group_mean(x, gmat):
    return _group_mean_impl(x, gmat)


def _group_mean_fwd(x, gmat):
    return _group_mean_impl(x, gmat), gmat


def _group_mean_bwd(gmat, g):
    return _group_mean_impl(g, gmat), jnp.zeros_like(gmat)


_group_mean.defvjp(_group_mean_fwd, _group_mean_bwd)


def _group_matrix(width, group):
    r = jnp.arange(width) // group
    return jnp.where(r[:, None] == r[None, :], 1.0 / group, 0.0).astype(CDT)


def _make_rope(half):
    def partner(x):
        w = x.shape[1]
        lane = lax.broadcasted_iota(jnp.int32, x.shape, 1)
        first = (lane % (2 * half)) < half
        return jnp.where(first, pltpu.roll(x, w - half, axis=1), pltpu.roll(x, half, axis=1))

    def impl(x, cos_full, sin_signed):
        return x * cos_full + partner(x) * sin_signed

    @jax.custom_vjp
    def rope(x, cos_full, sin_signed):
        return impl(x, cos_full, sin_signed)

    def fwd(x, cos_full, sin_signed):
        return impl(x, cos_full, sin_signed), (cos_full, sin_signed)

    def bwd(res, g):
        cos_full, sin_signed = res
        return impl(g, cos_full, -sin_signed), jnp.zeros_like(cos_full), jnp.zeros_like(sin_signed)

    rope.defvjp(fwd, bwd)
    return rope


_rope32 = _make_rope(32)
_rope64 = _make_rope(64)


def _rms(x, w):
    return x * lax.rsqrt(jnp.mean(x * x, axis=-1, keepdims=True) + NORM_EPS) * w


def _col(v, lane_index):
    lane = lax.broadcasted_iota(jnp.int32, v.shape, 1)
    return jnp.sum(jnp.where(lane == lane_index, v, 0.0), axis=1, keepdims=True)


def _typed_spec(width, nb_ctx):
    return pl.BlockSpec((None, 1, width), lambda i: (jnp.where(i >= nb_ctx, 1, 0), 0, 0))


def rowwise_fwd(name, f, rows, typed, shared, outs, n_rows, nb_ctx, tm=ROW_TILE):
    tm = min(tm, n_rows)
    nin = len(rows) + len(typed) + len(shared)

    def body(*refs):
        res = f(*[r[...] for r in refs[:nin]])
        for o_ref, o in zip(refs[nin:], res):
            o_ref[...] = o.astype(o_ref.dtype)

    arrs, specs = [], []
    for it in rows:
        a, s = _row_in(it, tm)
        arrs.append(a)
        specs.append(s)
    for t in typed:
        arrs.append(t)
        specs.append(_typed_spec(t.shape[-1], nb_ctx))
    for s_ in shared:
        arrs.append(s_)
        specs.append(_const_spec(s_.shape))
    res = pl.pallas_call(
        body, name=name, grid=(n_rows // tm,), in_specs=specs,
        out_specs=[pl.BlockSpec((tm, w), lambda i: (i, 0)) for w, _ in outs],
        out_shape=[jax.ShapeDtypeStruct((n_rows, w), dt) for w, dt in outs],
        compiler_params=_cp(("parallel",)))(*arrs)
    return res


def rowwise_bwd(name, f, rows, typed, shared, cots, row_diff, shared_diff, drow_dtypes, n_rows, nb_ctx, tm=ROW_TILE):
    tm = min(tm, n_rows)
    cot_groups = [c_ if isinstance(c_, tuple) else (c_,) for c_ in cots]
    cots = [a for grp in cot_groups for a in grp]
    nr, nt, ns, nc = len(rows), len(typed), len(shared), len(cots)
    nin = nr + nt + ns
    d_rows = [k for k in range(nr) if row_diff[k]]
    d_sh = [k for k in range(ns) if shared_diff[k]]

    def body(*refs):
        rvals = [r[...] for r in refs[:nr]]
        tvals = [r[...] for r in refs[nr:nr + nt]]
        svals = [r[...] for r in refs[nr + nt:nin]]
        cparts = [r[...].astype(F32) for r in refs[nin:nin + nc]]
        cvals = []
        for grp in cot_groups:
            cvals.append(sum(cparts[1:len(grp)], cparts[0]))
            cparts = cparts[len(grp):]
        out_refs = refs[nin + nc:]

        def g(*dv):
            dv = list(dv)
            rv = list(rvals)
            for k in d_rows:
                rv[k] = dv.pop(0)
            tv = [dv.pop(0) for _ in range(nt)]
            sv = list(svals)
            for k in d_sh:
                sv[k] = dv.pop(0)
            return tuple(o.astype(F32) for o in f(*rv, *tv, *sv))

        prim = [rvals[k].astype(F32) for k in d_rows] + tvals + [svals[k] for k in d_sh]
        _, vjp = jax.vjp(g, *prim)
        grads = list(vjp(tuple(cvals)))
        i = pl.program_id(0)
        for ref in out_refs[:len(d_rows)]:
            ref[...] = grads.pop(0).astype(ref.dtype)
        first_typed = (i == 0) | (i == nb_ctx)
        for ref in out_refs[len(d_rows):len(d_rows) + nt]:
            gr = grads.pop(0)

            @pl.when(first_typed)
            def _(ref=ref, gr=gr):
                ref[...] = gr

            @pl.when(jnp.logical_not(first_typed))
            def _(ref=ref, gr=gr):
                ref[...] += gr
        for ref in out_refs[len(d_rows) + nt:]:
            gr = grads.pop(0)

            @pl.when(i == 0)
            def _(ref=ref, gr=gr):
                ref[...] = gr

            @pl.when(i != 0)
            def _(ref=ref, gr=gr):
                ref[...] += gr

    arrs, specs = [], []
    for it in list(rows):
        a, s = _row_in(it, tm)
        arrs.append(a)
        specs.append(s)
    for t in typed:
        arrs.append(t)
        specs.append(_typed_spec(t.shape[-1], nb_ctx))
    for s_ in shared:
        arrs.append(s_)
        specs.append(_const_spec(s_.shape))
    for c_ in cots:
        a, s = _row_in(c_, tm)
        arrs.append(a)
        specs.append(s)
    out_specs, out_shape = [], []
    for k, dt in zip(d_rows, drow_dtypes):
        w = _width(rows[k])
        out_specs.append(pl.BlockSpec((tm, w), lambda i: (i, 0)))
        out_shape.append(jax.ShapeDtypeStruct((n_rows, w), dt))
    for t in typed:
        out_specs.append(_typed_spec(t.shape[-1], nb_ctx))
        out_shape.append(jax.ShapeDtypeStruct(t.shape, F32))
    for k in d_sh:
        out_specs.append(_const_spec(shared[k].shape))
        out_shape.append(jax.ShapeDtypeStruct(shared[k].shape, F32))
    res = pl.pallas_call(body, name=name, grid=(n_rows // tm,), in_specs=specs, out_specs=out_specs,
                         out_shape=out_shape, compiler_params=_cp(("arbitrary",)))(*arrs)
    n1, n2 = len(d_rows), len(d_rows) + nt
    return list(res[:n1]), list(res[n1:n2]), list(res[n2:])


def _pick(n, prefs):
    for p in prefs:
        if n % p == 0:
            return p
    return n


def mm(name, a, b, out_dtype, transpose_b=False, epilogue=None, tile_ins=(), typed_ins=(), shared_ins=(),
       out_dtypes=None, nb_ctx=None, whole_rows=False):
    n, k = b.shape if transpose_b else b.shape[::-1]
    m = (a.arr if isinstance(a, Cols) else a).shape[0]
    assert _width(a) == k
    tm = min(ROW_TILE, m) if typed_ins else _pick(m, (MM_ROWS, 256))
    out_dtypes = out_dtypes or [out_dtype]
    tile_arrs = [x.arr if isinstance(x, Cols) else x for x in tile_ins]
    tile_bytes = sum(jnp.dtype(d).itemsize for d in out_dtypes) + sum(x.dtype.itemsize for x in tile_arrs)
    fits = lambda c: 2 * (tm * k * 2 + k * c * 2 + tm * c * tile_bytes) <= MM_VMEM_BUDGET
    tn = next(c for c in (2560, 2048, 1536, 1024, 512, 256, 128, n) if n % c == 0 and fits(c) or c == n)
    if whole_rows:
        assert fits(n)
        tn = n
    dims = _NT if transpose_b else (((1,), (0,)), ((), ()))
    n_ex = len(tile_ins) + len(typed_ins) + len(shared_ins)

    def body(a_ref, b_ref, *refs):
        prod = lax.dot_general(a_ref[...], b_ref[...], dims, preferred_element_type=F32)
        outs = (prod,) if epilogue is None else epilogue(prod, *[r[...] for r in refs[:n_ex]])
        for o_ref, o in zip(refs[n_ex:], outs):
            o_ref[...] = o.astype(o_ref.dtype)

    a_arr, a_spec = _row_in(a, tm)
    a_spec = pl.BlockSpec(a_spec.block_shape, lambda j, i, f=a_spec.index_map: f(i))
    b_spec = pl.BlockSpec((tn, k), lambda j, i: (j, 0)) if transpose_b else pl.BlockSpec((k, tn), lambda j, i: (0, j))
    tile = pl.BlockSpec((tm, tn), lambda j, i: (i, j))
    ex_specs = []
    for x in tile_ins:
        base = x.off // tn if isinstance(x, Cols) else 0
        assert not isinstance(x, Cols) or (x.off % tn == 0 and x.width == n)
        ex_specs.append(pl.BlockSpec((tm, tn), lambda j, i, base=base: (i, base + j)))
    ex_specs += [pl.BlockSpec((None, 1, tn), lambda j, i: (jnp.where(i >= nb_ctx, 1, 0), 0, j))] * len(typed_ins)
    ex_specs += [pl.BlockSpec((1, tn), lambda j, i: (0, j))] * len(shared_ins)
    res = pl.pallas_call(
        body, name=name, grid=(n // tn, m // tm), in_specs=[a_spec, b_spec] + ex_specs,
        out_specs=[tile] * len(out_dtypes),
        out_shape=[jax.ShapeDtypeStruct((m, n), d) for d in out_dtypes],
        compiler_params=_cp(("parallel", "parallel")))(a_arr, b, *tile_arrs, *typed_ins, *shared_ins)
    return res[0] if epilogue is None else res


def mm_tn(name, a, b, out_dtype=F32, pieces=None):
    t = (a.arr if isinstance(a, Cols) else a).shape[0]
    k, n = _width(a), _width(b)
    tt = _pick(t, (MM_TN_ROWS, MM_ROWS, 256))
    k_unit = pieces[1] if pieces and pieces[0] == "rows" else k
    n_unit = pieces[1] if pieces and pieces[0] == "cols" else n
    tk = _pick(k_unit, (1024, 512, 256, 128))
    tn = _pick(n_unit, (1280, 1024, 512, 256, 128))
    n_t = t // tt

    def body(a_ref, b_ref, o_ref, acc):
        part = lax.dot_general(a_ref[...], b_ref[...], _TN, preferred_element_type=F32)
        ti = pl.program_id(2)

        @pl.when(ti == 0)
        def _():
            acc[...] = part

        @pl.when(ti != 0)
        def _():
            acc[...] += part

        @pl.when(ti == n_t - 1)
        def _():
            o_ref[...] = acc[...].astype(o_ref.dtype)

    def win(item, width):
        if isinstance(item, Cols):
            assert item.off % width == 0
            return item.arr, item.off // width
        return item, 0

    a_arr, a0 = win(a, tk)
    b_arr, b0 = win(b, tn)
    if pieces is None:
        out_spec = pl.BlockSpec((tk, tn), lambda ki, ni, ti: (ki, ni))
        out_shape = (k, n)
    elif pieces[0] == "cols":
        per = n_unit // tn
        out_spec = pl.BlockSpec((None, tk, tn), lambda ki, ni, ti: (ni // per, ki, ni % per))
        out_shape = (4, k, n_unit)
    else:
        per = k_unit // tk
        out_spec = pl.BlockSpec((None, tk, tn), lambda ki, ni, ti: (ki // per, ki % per, ni))
        out_shape = (4, k_unit, n)
    return pl.pallas_call(
        body, name=name, grid=(k // tk, n // tn, n_t),
        in_specs=[pl.BlockSpec((tt, tk), lambda ki, ni, ti: (ti, a0 + ki)),
                  pl.BlockSpec((tt, tn), lambda ki, ni, ti: (ti, b0 + ni))],
        out_specs=out_spec, out_shape=jax.ShapeDtypeStruct(out_shape, out_dtype),
        scratch_shapes=[pltpu.VMEM((tk, tn), F32)],
        compiler_params=_cp(("parallel", "parallel", "arbitrary")))(a_arr, b_arr)


def _heads_t(rows_blk):
    blk = rows_blk.astype(F32).T
    return jnp.concatenate([blk[hh * ATTN_HD:(hh + 1) * ATTN_HD, :] for hh in range(4)], axis=1)


def _heads_rows(t_blk):
    tq = t_blk.shape[1] // 4
    return jnp.concatenate([t_blk[:, hh * tq:(hh + 1) * tq] for hh in range(4)], axis=0).T


def attn_fwd(name, q, kk, vT_ones, m_ctx, tq):
    t, hd, hd_ext = kk.shape[1], ATTN_HD, vT_ones.shape[1]
    nq, r = t // tq, 4 * tq
    tk = _pick(t - m_ctx, (ATTN_TK_FWD, ATTN_TK))
    nqc, n_lat_tiles = m_ctx // tq, (t - m_ctx) // tk

    def body(q_ref, k_ref, vT_ref, o_ref, qT_ref, oT_ref, lse_ref):
        i = pl.program_id(1)
        q_t = _heads_t(q_ref[...]).astype(CDT)
        qT_ref[...] = q_t

        def tile(off, size, carry):
            mi, acc = carry
            sub = min(size, ATTN_TK)
            offs = [off + u * sub for u in range(size // sub)]
            sts = [jnp.dot(k_ref[pl.ds(o, sub), :], q_t, preferred_element_type=F32) for o in offs]
            for o, st in zip(offs, sts):
                mn = jnp.maximum(mi, jnp.max(st, axis=0, keepdims=True))
                pt = jnp.exp2(st - mn)
                acc = jnp.exp2(mi - mn) * acc + jnp.dot(vT_ref[:, pl.ds(o, sub)], pt.astype(CDT),
                                                        preferred_element_type=F32)
                mi = mn
            return mi, acc

        carry = tile(0, m_ctx, (jnp.full((1, r), -1e30, F32), jnp.zeros((hd_ext, r), F32)))
        mi, acc = lax.fori_loop(
            0, jnp.where(i < nqc, 0, n_lat_tiles),
            lambda j, cr: tile(pl.multiple_of(m_ctx + j * tk, ATTN_TK), tk, cr), carry)
        li = acc[hd:hd + 1]
        o_t = acc[:hd] / li
        oT_ref[...] = o_t.astype(oT_ref.dtype)
        o_ref[...] = _heads_rows(o_t).astype(o_ref.dtype)
        lse_ref[...] = mi + jnp.log2(li)

    blk_t = pl.BlockSpec((None, None, hd, r), lambda g, i: (g, i, 0, 0))
    rows = pl.BlockSpec((tq, 4 * hd), lambda g, i: (i, g))
    return pl.pallas_call(
        body, name=name, grid=(2, nq),
        in_specs=[rows, pl.BlockSpec((None, t, hd), lambda g, i: (g, 0, 0)),
                  pl.BlockSpec((None, hd_ext, t), lambda g, i: (g, 0, 0))],
        out_specs=[rows, blk_t, blk_t, pl.BlockSpec((None, None, 1, r), lambda g, i: (g, i, 0, 0))],
        out_shape=[jax.ShapeDtypeStruct((t, 8 * hd), CDT), jax.ShapeDtypeStruct((2, nq, hd, r), CDT),
                   jax.ShapeDtypeStruct((2, nq, hd, r), CDT), jax.ShapeDtypeStruct((2, nq, 1, r), F32)],
        compiler_params=_cp(("parallel", "arbitrary")))(q, kk, vT_ones)


def attn_bwd(name, qT, do, oT, lse, kk, kT, vv, m_ctx):
    _, nq, hd, r = qT.shape
    t = kk.shape[1]
    tq = r // 4
    tk = _pick(t - m_ctx, (ATTN_TK_BWD, ATTN_TK))
    nqc, n_lat_tiles = m_ctx // tq, (t - m_ctx) // tk

    def body(qT_ref, do_ref, oT_ref, lse_ref, k_ref, kT_ref, v_ref, dq_ref, dk_ref, dv_ref):
        i = pl.program_id(1)

        @pl.when(i == 0)
        def _():
            dk_ref[...] = jnp.zeros_like(dk_ref)
            dv_ref[...] = jnp.zeros_like(dv_ref)

        q_t = qT_ref[...]
        do_f = _heads_t(do_ref[...])
        do_t = do_f.astype(CDT)
        lse = lse_ref[...]
        delta = jnp.sum(do_f * oT_ref[...].astype(F32), axis=0, keepdims=True)

        def tile(off, size, dq):
            sub = min(size, ATTN_TK)
            offs = [off + u * sub for u in range(size // sub)]
            sts = [jnp.dot(k_ref[pl.ds(o, sub), :], q_t, preferred_element_type=F32) for o in offs]
            dpts = [jnp.dot(v_ref[pl.ds(o, sub), :], do_t, preferred_element_type=F32) for o in offs]
            for o, st, dpt in zip(offs, sts, dpts):
                pt = jnp.exp2(st - lse)
                dv_ref[pl.ds(o, sub), :] += lax.dot_general(pt.astype(CDT), do_t, _NT, preferred_element_type=F32)
                dst = (pt * (dpt - delta)).astype(CDT)
                dk_ref[pl.ds(o, sub), :] += lax.dot_general(dst, q_t, _NT, preferred_element_type=F32)
                dq = dq + jnp.dot(kT_ref[:, pl.ds(o, sub)], dst, preferred_element_type=F32)
            return dq

        dq = tile(0, m_ctx, jnp.zeros((hd, r), F32))
        dq = lax.fori_loop(0, jnp.where(i < nqc, 0, n_lat_tiles),
                           lambda j, acc: tile(pl.multiple_of(m_ctx + j * tk, ATTN_TK), tk, acc), dq)
        dq_ref[...] = _heads_rows(dq * LN2)

    blk_t = pl.BlockSpec((None, None, hd, r), lambda g, i: (g, i, 0, 0))
    row = pl.BlockSpec((None, None, 1, r), lambda g, i: (g, i, 0, 0))
    kv = pl.BlockSpec((None, t, hd), lambda g, i: (g, 0, 0))
    rows = pl.BlockSpec((tq, 4 * hd), lambda g, i: (i, g))
    return pl.pallas_call(
        body, name=name, grid=(2, nq),
        in_specs=[blk_t, rows, blk_t, row, kv, pl.BlockSpec((None, hd, t), lambda g, i: (g, 0, 0)), kv],
        out_specs=[rows, kv, kv],
        out_shape=[jax.ShapeDtypeStruct((t, 8 * hd), F32), jax.ShapeDtypeStruct(kk.shape, F32),
                   jax.ShapeDtypeStruct(kk.shape, F32)],
        compiler_params=_cp(("parallel", "arbitrary")))(qT, do, oT, lse, kk, kT, vv)


def _split_kv(a):
    return a.reshape(a.shape[0], 2, ATTN_HD).transpose(1, 0, 2)


def _merge_kv(a):
    return a.transpose(1, 0, 2).reshape(a.shape[1], 2 * ATTN_HD)


def _chunk_order(rev, ncc, nct):
    if not rev:
        return lambda s: s
    return lambda s: jnp.where(s < ncc, ncc - 1 - s, nct - 1 - (s - ncc))


def scan_fwd(name, make_fn, rows, shared, n_state, y_width, n_rows, m_ctx):
    nct, ncc = n_rows // CHUNK, m_ctx // CHUNK
    orders = [_chunk_order(rev, ncc, nct) for rev in (False, True)]
    fns = [make_fn(0), make_fn(1)]
    nr, ns = len(rows), len(shared)

    def body(*refs):
        svals = [r[...] for r in refs[2 * nr:2 * nr + ns]]
        y_refs, sin_refs, st = refs[2 * nr + ns:2 * nr + ns + 2], refs[2 * nr + ns + 2:2 * nr + ns + 4], refs[-1]

        @pl.when(pl.program_id(0) == 0)
        def _():
            st[...] = jnp.zeros_like(st)

        for d in range(2):
            rvals = [r[...] for r in refs[d * nr:(d + 1) * nr]]
            prev = [st[d, k] for k in range(n_state)]
            sin_refs[d][...] = st[d]
            y, new = fns[d](rvals, svals, prev)
            y_refs[d][...] = y
            for k in range(n_state):
                st[d, k] = new[k]

    arrs, specs = [], []
    for order in orders:
        for it in rows:
            a, s = _row_in(it, CHUNK, order)
            arrs.append(a)
            specs.append(s)
    for s_ in shared:
        arrs.append(s_)
        specs.append(_const_spec(s_.shape))
    return pl.pallas_call(
        body, name=name, grid=(nct,), in_specs=specs,
        out_specs=[pl.BlockSpec((CHUNK, y_width), lambda s, o=o: (o(s), 0)) for o in orders]
        + [pl.BlockSpec((None, n_state, LANES, LANES), lambda s, o=o: (o(s), 0, 0, 0)) for o in orders],
        out_shape=[jax.ShapeDtypeStruct((n_rows, y_width), F32)] * 2
        + [jax.ShapeDtypeStruct((nct, n_state, LANES, LANES), F32)] * 2,
        scratch_shapes=[pltpu.VMEM((2, n_state, LANES, LANES), F32)],
        compiler_params=_cp(("arbitrary",)))(*arrs)


def scan_bwd(name, make_fn, rows, shared, states_in, dy, post, outs, n_state, n_rows, m_ctx, dirs=(0, 1)):
    nct, ncc = n_rows // CHUNK, m_ctx // CHUNK
    orders = [(lambda r, f=_chunk_order(d == 1, ncc, nct): f(nct - 1 - r)) for d in dirs]
    fns = [make_fn(d) for d in dirs]
    nd = len(dirs)
    nr, ns, no = len(rows), len(shared), len(outs)
    n_in = nd * nr + ns

    def body(*refs):
        svals = [r[...] for r in refs[nd * nr:n_in]]
        sin_refs, dy_refs = refs[n_in:n_in + nd], refs[n_in + nd:n_in + 2 * nd]
        out_refs = refs[n_in + 2 * nd:n_in + 2 * nd + nd * no]
        dsh_refs = refs[n_in + 2 * nd + nd * no:-1]
        dst = refs[-1]
        r = pl.program_id(0)

        @pl.when(r == 0)
        def _():
            dst[...] = jnp.zeros_like(dst)

        d_shared = None
        for d in range(nd):
            rvals = [x[...] for x in refs[d * nr:(d + 1) * nr]]
            prev = [sin_refs[d][k] for k in range(n_state)]
            _, vjp = jax.vjp(fns[d], rvals, svals, prev)
            d_rows, d_sh, d_prev = vjp((dy_refs[d][...], [dst[d, k] for k in range(n_state)]))
            for ref, val in zip(out_refs[d * no:(d + 1) * no], post(d_rows)):
                ref[...] = val.astype(ref.dtype)
            d_shared = d_sh if d_shared is None else [a + b for a, b in zip(d_shared, d_sh)]
            for k in range(n_state):
                dst[d, k] = d_prev[k]
        for ref, gr in zip(dsh_refs, d_shared):
            @pl.when(r == 0)
            def _(ref=ref, gr=gr):
                ref[...] = gr

            @pl.when(r != 0)
            def _(ref=ref, gr=gr):
                ref[...] += gr

    arrs, specs = [], []
    for order in orders:
        for it in rows:
            a, s = _row_in(it, CHUNK, order)
            arrs.append(a)
            specs.append(s)
    for s_ in shared:
        arrs.append(s_)
        specs.append(_const_spec(s_.shape))
    for sin, order in zip(states_in, orders):
        arrs.append(sin)
        specs.append(pl.BlockSpec((None, n_state, LANES, LANES), lambda r, o=order: (o(r), 0, 0, 0)))
    for order in orders:
        a, s = _row_in(dy, CHUNK, order)
        arrs.append(a)
        specs.append(s)
    out_specs = [pl.BlockSpec((CHUNK, w), lambda r, o=o: (o(r), 0)) for o in orders for w, _ in outs]
    out_shape = [jax.ShapeDtypeStruct((n_rows, w), dt) for _ in orders for w, dt in outs]
    for s_ in shared:
        out_specs.append(_const_spec(s_.shape))
        out_shape.append(jax.ShapeDtypeStruct(s_.shape, F32))
    res = pl.pallas_call(body, name=name, grid=(nct,), in_specs=specs, out_specs=out_specs, out_shape=out_shape,
                         scratch_shapes=[pltpu.VMEM((nd, n_state, LANES, LANES), F32)],
                         compiler_params=_cp(("arbitrary",)))(*arrs)
    return [list(res[d * no:(d + 1) * no]) for d in range(nd)] + [list(res[nd * no:])]


def _make_ssd_chunk(direction):
    rev = direction == 1
    base = 8 * direction

    def fn(rows, shared, prev):
        xs, bms, cms, dtraw = rows[0:4], rows[4:6], rows[6:8], rows[8]
        dt_bias, a_log = shared
        ln = dtraw.shape[0]
        dt_all = _softplus(dtraw + dt_bias)
        a_all = dt_all * (-jnp.exp(a_log))
        r_i = lax.broadcasted_iota(jnp.int32, (ln, ln), 0)
        c_i = lax.broadcasted_iota(jnp.int32, (ln, ln), 1)
        tri = (r_i <= c_i) if rev else (r_i >= c_i)
        a_cum_all = jnp.dot(tri.astype(F32), a_all, precision=lax.Precision.HIGHEST, preferred_element_type=F32)
        a_tot_all = jnp.sum(a_all, axis=0, keepdims=True)
        first = lax.broadcasted_iota(jnp.int32, (ln, LANES), 1) < SSD_HD
        first_row = lax.broadcasted_iota(jnp.int32, (LANES, 1), 0) < SSD_HD

        def lmat(acol):
            a_b = jnp.broadcast_to(acol, (ln, ln))
            seg = a_b - a_b.T
            return jnp.where(tri, jnp.exp(jnp.where(tri, seg, 0.0)), 0.0)

        ys, new = [], []
        for g in range(2):
            bm, cm = bms[g], cms[g]
            cb = _mxu(cm, bm, _NT)
            for jj in range(2):
                pr = 2 * g + jj
                h0, h1 = base + 2 * pr, base + 2 * pr + 1
                ac0, ac1 = _col(a_cum_all, h0), _col(a_cum_all, h1)
                at0, at1 = _col(a_tot_all, h0), _col(a_tot_all, h1)
                dt_pair = jnp.where(first, _col(dt_all, h0), _col(dt_all, h1))
                acum_pair = jnp.where(first, ac0, ac1)
                atot_pair = jnp.where(first[0:1], at0, at1)
                xd = xs[pr] * dt_pair
                st = _mxu(xd * jnp.exp(atot_pair - acum_pair), bm, _TN)
                new.append(prev[pr] * jnp.where(first_row, jnp.exp(at0), jnp.exp(at1)) + st)
                y0 = _mxu(cb * lmat(ac0), xd)
                y1 = _mxu(cb * lmat(ac1), xd)
                y_off = _mxu(cm, prev[pr], _NT) * jnp.exp(acum_pair)
                ys.append(jnp.where(first, y0, y1) + y_off)
        return jnp.concatenate(ys, axis=1), new

    return fn


def _make_ret_chunk(direction):
    rev = direction == 1
    base = 4 * direction

    def fn(rows, shared, prev):
        qs, ks, vs = rows[0:4], rows[4:8], rows[8:12]
        lg_all = -jnp.exp(shared[0])
        ln = qs[0].shape[0]
        pos = lax.broadcasted_iota(jnp.int32, (ln, 1), 0).astype(F32)
        r_i = lax.broadcasted_iota(jnp.int32, (ln, ln), 0)
        c_i = lax.broadcasted_iota(jnp.int32, (ln, ln), 1)
        diff = ((c_i - r_i) if rev else (r_i - c_i))
        mask = diff >= 0
        dpos = jnp.maximum(diff, 0).astype(F32)
        k_pow = pos if rev else (ln - 1.0 - pos)
        q_pow = (ln - pos) if rev else (pos + 1.0)
        ys, new = [], []
        for h in range(RET_HEADS):
            lg = _col(lg_all, base + h)
            dmat = jnp.where(mask, jnp.exp(dpos * lg), 0.0)
            st = _mxu(ks[h] * jnp.exp(k_pow * lg), vs[h], _TN)
            new.append(prev[h] * jnp.exp(ln * lg) + st)
            s = _mxu(qs[h], ks[h], _NT) * dmat
            ys.append(_mxu(s, vs[h]) + _mxu(qs[h], prev[h]) * jnp.exp(q_pow * lg))
        return jnp.concatenate(ys, axis=1), new

    return fn


def _conv_pre(x, w, b, t_idx, n_rows, m_ctx):
    is_start = (t_idx == 0) | (t_idx == m_ctx)
    is_end = (t_idx == m_ctx - 1) | (t_idx == n_rows - 1)
    xp = jnp.where(is_start, 0.0, pltpu.roll(x, 1, axis=0))
    xn = jnp.where(is_end, 0.0, pltpu.roll(x, n_rows - 1, axis=0))
    return w[0:1] * xp + w[1:2] * x + w[2:3] * xn + b, xp, xn, is_start, is_end


def conv_fwd(x, conv_w, conv_b, m_ctx):
    n_rows, width = x.arr.shape[0], x.width
    c0 = x.off // LANES

    def body(x_ref, w_ref, b_ref, o_ref):
        t_idx = lax.broadcasted_iota(jnp.int32, (n_rows, 1), 0)
        pre = _conv_pre(x_ref[...], w_ref[...], b_ref[...], t_idx, n_rows, m_ctx)[0]
        o_ref[...] = pre * jax.nn.sigmoid(pre)

    return pl.pallas_call(
        body, name="conv_fwd", grid=(width // LANES,),
        in_specs=[pl.BlockSpec((n_rows, LANES), lambda c: (0, c0 + c)),
                  pl.BlockSpec((3, LANES), lambda c: (0, c)), pl.BlockSpec((1, LANES), lambda c: (0, c))],
        out_specs=pl.BlockSpec((n_rows, LANES), lambda c: (0, c)),
        out_shape=jax.ShapeDtypeStruct((n_rows, width), F32),
        compiler_params=_cp(("parallel",)))(x.arr, conv_w, conv_b)


def conv_bwd(x, conv_w, conv_b, dy_a, dy_b, dxs_extra, m_ctx):
    n_rows, width = x.arr.shape[0], x.width
    c0 = x.off // LANES
    n_extra = dxs_extra.shape[1] // LANES

    def body(x_ref, w_ref, b_ref, dya_ref, dyb_ref, ex_ref, dx_ref, dw_ref, db_ref):
        c = pl.program_id(0)
        t_idx = lax.broadcasted_iota(jnp.int32, (n_rows, 1), 0)
        w = w_ref[...]
        pre, xp, xn, is_start, is_end = _conv_pre(x_ref[...], w, b_ref[...], t_idx, n_rows, m_ctx)
        sg = jax.nn.sigmoid(pre)
        dyv = dya_ref[...] + dyb_ref[...] + jnp.where(c < n_extra, ex_ref[...], 0.0)
        dpre = dyv * (sg * (1.0 + pre * (1.0 - sg)))
        d_next = jnp.where(is_end, 0.0, pltpu.roll(dpre, n_rows - 1, axis=0))
        d_prev = jnp.where(is_start, 0.0, pltpu.roll(dpre, 1, axis=0))
        dx_ref[...] = (w[1:2] * dpre + w[0:1] * d_next + w[2:3] * d_prev).astype(dx_ref.dtype)
        dw_ref[...] = jnp.concatenate([jnp.sum(dpre * xp, axis=0, keepdims=True),
                                       jnp.sum(dpre * x_ref[...], axis=0, keepdims=True),
                                       jnp.sum(dpre * xn, axis=0, keepdims=True)], axis=0)
        db_ref[...] = jnp.sum(dpre, axis=0, keepdims=True)

    return pl.pallas_call(
        body, name="conv_bwd", grid=(width // LANES,),
        in_specs=[pl.BlockSpec((n_rows, LANES), lambda c: (0, c0 + c)),
                  pl.BlockSpec((3, LANES), lambda c: (0, c)), pl.BlockSpec((1, LANES), lambda c: (0, c)),
                  pl.BlockSpec((n_rows, LANES), lambda c: (0, c)), pl.BlockSpec((n_rows, LANES), lambda c: (0, c)),
                  pl.BlockSpec((n_rows, LANES), lambda c: (0, jnp.minimum(c, n_extra - 1)))],
        out_specs=[pl.BlockSpec((n_rows, LANES), lambda c: (0, c)),
                   pl.BlockSpec((3, LANES), lambda c: (0, c)), pl.BlockSpec((1, LANES), lambda c: (0, c))],
        out_shape=[jax.ShapeDtypeStruct((n_rows, width), CDT), jax.ShapeDtypeStruct((3, width), F32),
                   jax.ShapeDtypeStruct((1, width), F32)],
        compiler_params=_cp(("parallel",)))(x.arr, conv_w, conv_b, dy_a, dy_b, dxs_extra)


def loss_head(h, target, final_w, m_ctx):
    n_rows, d = h.shape
    tm = min(ROW_TILE, n_rows)
    nb_ctx = m_ctx // tm

    def f(hb, w, tgt):
        err = _rms(hb, w) - tgt
        return 0.5 * jnp.sum(jnp.mean(err * err, axis=-1))

    def body(h_ref, t_ref, w_ref, loss_ref, dh_ref, dw_ref):
        i = pl.program_id(0)

        @pl.when(i < nb_ctx)
        def _():
            dh_ref[...] = jnp.zeros_like(dh_ref)

        @pl.when(i == 0)
        def _():
            loss_ref[...] = jnp.zeros_like(loss_ref)
            dw_ref[...] = jnp.zeros_like(dw_ref)

        @pl.when(i >= nb_ctx)
        def _():
            val, vjp = jax.vjp(lambda hb, w: f(hb, w, t_ref[...]), h_ref[...], w_ref[...])
            dh, dw = vjp(jnp.ones((), F32))
            dh_ref[...] = dh
            dw_ref[...] += dw
            loss_ref[...] += jnp.broadcast_to(val, loss_ref.shape)

    return pl.pallas_call(
        body, name="loss_head", grid=(n_rows // tm,),
        in_specs=[pl.BlockSpec((tm, d), lambda i: (i, 0)),
                  pl.BlockSpec((tm, d), lambda i: (jnp.maximum(i - nb_ctx, 0), 0)), _const_spec((1, d))],
        out_specs=[_const_spec((1, LANES)), pl.BlockSpec((tm, d), lambda i: (i, 0)), _const_spec((1, d))],
        out_shape=[jax.ShapeDtypeStruct((1, LANES), F32), jax.ShapeDtypeStruct((n_rows, d), F32),
                   jax.ShapeDtypeStruct((1, d), F32)],
        compiler_params=_cp(("arbitrary",)))(h, target, final_w)


def adamw(name, w, m, v, g_parts):
    lead, rows, cols = w.shape
    tr = _pick(rows, (256, 128, 64, 32, 16, 8))
    npart = len(g_parts)
    c1 = 1.0 - ADAM_B1 ** ADAM_STEP
    c2 = 1.0 - ADAM_B2 ** ADAM_STEP

    def body(*refs):
        w_ref, m_ref, v_ref = refs[:3]
        g = refs[3][...].astype(F32)
        for r in refs[4:3 + npart]:
            g = g + r[...].astype(F32)
        g_ref, d_ref, nm_ref, nv_ref = refs[3 + npart:]
        nm = ADAM_B1 * m_ref[...] + (1.0 - ADAM_B1) * g
        nv = ADAM_B2 * v_ref[...] + (1.0 - ADAM_B2) * (g * g)
        g_ref[...] = g
        nm_ref[...] = nm
        nv_ref[...] = nv
        d_ref[...] = -ADAM_LR * ((nm / c1) / (jnp.sqrt(nv / c2) + ADAM_EPS) + ADAM_WD * w_ref[...])

    spec = pl.BlockSpec((None, tr, cols), lambda l, i: (l, i, 0))
    return pl.pallas_call(
        body, name=name, grid=(lead, rows // tr), in_specs=[spec] * (3 + npart), out_specs=[spec] * 4,
        out_shape=[jax.ShapeDtypeStruct(w.shape, F32)] * 4,
        compiler_params=_cp(("parallel", "parallel")))(w, m, v, *g_parts)


MESH = pl.DeviceIdType.MESH
_HBM = pl.BlockSpec(memory_space=pl.ANY)


def _chip_peers():
    x, y, c = lax.axis_index("x"), lax.axis_index("y"), lax.axis_index("c")
    return x, y, c, [(1 - x, y), (x, 1 - y), (1 - x, 1 - y)]


def _window(ref, kind, chip, rows, cols):
    if kind == "cols":
        return ref.at[:, pl.ds(pl.multiple_of(chip * cols, LANES), cols)]
    if kind == "rows":
        return ref.at[pl.ds(pl.multiple_of(chip * rows, 8), rows), :]
    return ref.at[chip]


def _gathered_shape(kind, rows, cols):
    return {"cols": (rows, 4 * cols), "rows": (4 * rows, cols), "slices": (4, rows, cols)}[kind]


def gather_layers(name, shards, kinds):
    n = len(shards)

    def body(*refs):
        x_refs, o_refs = refs[:n], refs[n:2 * n]
        send_sems, recv_sems, local_sems = refs[2 * n:]
        x, y, c, peers = _chip_peers()
        me = 2 * x + y
        started = []
        for a in range(n):
            _, rows, cols = shards[a].shape
            src = x_refs[a].at[c]
            mine = pltpu.make_async_copy(src, _window(o_refs[a], kinds[a], me, rows, cols), local_sems.at[a])
            mine.start()
            started.append(mine.wait)
            for k, (px, py) in enumerate(peers):
                cp = pltpu.make_async_remote_copy(
                    src_ref=src, dst_ref=_window(o_refs[a], kinds[a], me, rows, cols), send_sem=send_sems.at[3 * a + k],
                    recv_sem=recv_sems.at[3 * a + k], device_id=(px, py, c), device_id_type=MESH)
                cp.start()
                started.append(cp.wait_send)
        for a in range(n):
            _, rows, cols = shards[a].shape
            for k, (px, py) in enumerate(peers):
                pltpu.make_async_remote_copy(
                    src_ref=x_refs[a].at[c], dst_ref=_window(o_refs[a], kinds[a], 2 * px + py, rows, cols),
                    send_sem=send_sems.at[3 * a + k], recv_sem=recv_sems.at[3 * a + k], device_id=(px, py, c),
                    device_id_type=MESH).wait_recv()
        for wait in started:
            wait()

    return pl.pallas_call(
        body, name=name, in_specs=[_HBM] * n, out_specs=[_HBM] * n,
        out_shape=[jax.ShapeDtypeStruct(_gathered_shape(kinds[a], *shards[a].shape[1:]), shards[a].dtype)
                   for a in range(n)],
        scratch_shapes=[pltpu.SemaphoreType.DMA((3 * n,)), pltpu.SemaphoreType.DMA((3 * n,)),
                        pltpu.SemaphoreType.DMA((n,))],
        )(*shards)


def scatter_pieces(name, pieces):
    n = len(pieces)

    def body(*refs):
        p_refs, o_refs = refs[:n], refs[n:2 * n]
        send_sems, recv_sems, local_sems = refs[2 * n:]
        x, y, c, peers = _chip_peers()
        me = 2 * x + y
        started = []
        for a in range(n):
            mine = pltpu.make_async_copy(p_refs[a].at[me], o_refs[a].at[me], local_sems.at[a])
            mine.start()
            started.append(mine.wait)
            for k, (px, py) in enumerate(peers):
                cp = pltpu.make_async_remote_copy(
                    src_ref=p_refs[a].at[2 * px + py], dst_ref=o_refs[a].at[me], send_sem=send_sems.at[3 * a + k],
                    recv_sem=recv_sems.at[3 * a + k], device_id=(px, py, c), device_id_type=MESH)
                cp.start()
                started.append(cp.wait_send)
        for a in range(n):
            for k, (px, py) in enumerate(peers):
                pltpu.make_async_remote_copy(
                    src_ref=p_refs[a].at[me], dst_ref=o_refs[a].at[2 * px + py], send_sem=send_sems.at[3 * a + k],
                    recv_sem=recv_sems.at[3 * a + k], device_id=(px, py, c), device_id_type=MESH).wait_recv()
        for wait in started:
            wait()

    return pl.pallas_call(
        body, name=name, in_specs=[_HBM] * n, out_specs=[_HBM] * n,
        out_shape=[jax.ShapeDtypeStruct(p.shape, p.dtype) for p in pieces],
        scratch_shapes=[pltpu.SemaphoreType.DMA((3 * n,)), pltpu.SemaphoreType.DMA((3 * n,)),
                        pltpu.SemaphoreType.DMA((n,))],
        )(*pieces)


def _pair_step(n_steps, x_ref, land, send_sems, recv_sems, credits, consume):
    x, y, c = lax.axis_index("x"), lax.axis_index("y"), lax.axis_index("c")
    sib = (x, y, 1 - c)
    i = pl.program_id(0)
    slot = i % 2

    @pl.when(i >= 2)
    def _():
        pl.semaphore_wait(credits.at[slot], 1)

    cp = pltpu.make_async_remote_copy(src_ref=x_ref, dst_ref=land.at[slot], send_sem=send_sems.at[slot],
                                      recv_sem=recv_sems.at[slot], device_id=sib, device_id_type=MESH)
    cp.start()
    cp.wait_recv()
    consume(land[slot])

    @pl.when(i < n_steps - 2)
    def _():
        pl.semaphore_signal(credits.at[slot], inc=1, device_id=sib, device_id_type=MESH)

    cp.wait_send()


def _pair_call(name, body, n_steps, in_specs, out_spec, out_shape, blk_shape, dtype, operands, extra_scratch=()):
    grid_spec = pltpu.PrefetchScalarGridSpec(
        num_scalar_prefetch=1, grid=(n_steps,), in_specs=in_specs, out_specs=out_spec,
        scratch_shapes=[pltpu.VMEM((2,) + blk_shape, dtype), pltpu.SemaphoreType.DMA((2,)),
                        pltpu.SemaphoreType.DMA((2,)), pltpu.SemaphoreType.REGULAR((2,)), *extra_scratch])
    return pl.pallas_call(body, name=name, grid_spec=grid_spec, out_shape=out_shape,
                          compiler_params=_cp(("arbitrary",)))(*operands)


def _place():
    return jnp.stack([lax.axis_index("x"), lax.axis_index("y"), lax.axis_index("c")]).astype(jnp.int32)


def _pair_rows(rows, row_bytes):
    for cand in (4096, 2048, 1024, 768, 512, 384, 256, 192, 128, 96, 64, 48, 32, 16):
        if rows % cand == 0 and cand * row_bytes <= PAIR_BLOCK_BYTES:
            return cand
    return _pick(rows, (16, 8))


def exchange_both(name, mine):
    rows, cols = mine.shape
    tr = _pair_rows(rows, cols * mine.dtype.itemsize)
    n_steps = rows // tr

    def body(s_ref, x_ref, o_ref, land, send_sems, recv_sems, credits):
        c = lax.axis_index("c")
        o_ref[c] = x_ref[...]

        def consume(v):
            o_ref[1 - c] = v
        _pair_step(n_steps, x_ref, land, send_sems, recv_sems, credits, consume)

    return _pair_call(name, body, n_steps, [pl.BlockSpec((tr, cols), lambda i, s: (i, 0))],
                      pl.BlockSpec((2, tr, cols), lambda i, s: (0, i, 0)),
                      jax.ShapeDtypeStruct((2, rows, cols), mine.dtype), (tr, cols), mine.dtype, (_place(), mine))


def exchange_add(name, layer0, layer1):
    rows, cols = layer0.shape
    tr = _pair_rows(rows, cols * layer0.dtype.itemsize)
    nb = rows // tr

    def body(s_ref, l0_ref, l1_ref, o_ref, land, send_sems, recv_sems, credits, send_buf):
        first = lax.axis_index("c") == 0
        send_buf[...] = jnp.where(first, l1_ref[...], l0_ref[...])
        mine = jnp.where(first, l0_ref[...], l1_ref[...]).astype(F32)

        def consume(v):
            o_ref[...] = (mine + v.astype(F32)).astype(o_ref.dtype)
        _pair_step(nb, send_buf, land, send_sems, recv_sems, credits, consume)

    spec = pl.BlockSpec((tr, cols), lambda i, s: (i, 0))
    return _pair_call(name, body, nb, [spec, spec], spec, jax.ShapeDtypeStruct((rows, cols), CDT),
                      (tr, cols), layer0.dtype, (_place(), layer0, layer1),
                      extra_scratch=(pltpu.VMEM((tr, cols), layer0.dtype),))


def sum_exchange(name, parts):
    npart, rows, cols = parts.shape
    tr = _pair_rows(rows, cols * 4)
    n_steps = rows // tr

    def body(s_ref, x_ref, o_ref, land, send_sems, recv_sems, credits, mine):
        c = lax.axis_index("c")
        acc = x_ref[0].astype(F32)
        for k in range(1, npart):
            acc = acc + x_ref[k].astype(F32)
        mine[...] = acc
        o_ref[c] = acc

        def consume(v):
            o_ref[1 - c] = v
        _pair_step(n_steps, mine, land, send_sems, recv_sems, credits, consume)

    return _pair_call(name, body, n_steps, [pl.BlockSpec((npart, tr, cols), lambda i, s: (0, i, 0))],
                      pl.BlockSpec((2, tr, cols), lambda i, s: (0, i, 0)),
                      jax.ShapeDtypeStruct((2, rows, cols), F32), (tr, cols), F32, (_place(), parts),
                      extra_scratch=(pltpu.VMEM((tr, cols), F32),))


def allreduce_small(name, buf):
    rows = buf.shape[0]

    def body(x_ref, out_ref, gath, send_sems, recv_sems):
        x, y, c = lax.axis_index("x"), lax.axis_index("y"), lax.axis_index("c")
        me = 4 * x + 2 * y + c
        masks = [(k >> 2 & 1, k >> 1 & 1, k & 1) for k in range(1, 8)]

        def flip(v, bit):
            return 1 - v if bit else v

        sends = []
        for k, (bx, by, bc) in enumerate(masks):
            cp = pltpu.make_async_remote_copy(src_ref=x_ref, dst_ref=gath.at[me], send_sem=send_sems.at[k],
                                              recv_sem=recv_sems.at[k],
                                              device_id=(flip(x, bx), flip(y, by), flip(c, bc)), device_id_type=MESH)
            cp.start()
            sends.append(cp)
        gath[me] = x_ref[...]
        for k, (bx, by, bc) in enumerate(masks):
            px, py, pc = flip(x, bx), flip(y, by), flip(c, bc)
            pltpu.make_async_remote_copy(src_ref=x_ref, dst_ref=gath.at[4 * px + 2 * py + pc],
                                         send_sem=send_sems.at[k], recv_sem=recv_sems.at[k],
                                         device_id=(px, py, pc), device_id_type=MESH).wait_recv()
        for cp in sends:
            cp.wait_send()
        acc = gath[0]
        for d in range(1, 8):
            acc = acc + gath[d]
        out_ref[...] = acc

    return pl.pallas_call(
        body, name=name, in_specs=[pl.BlockSpec(memory_space=pltpu.VMEM)],
        out_specs=pl.BlockSpec(memory_space=pltpu.VMEM), out_shape=jax.ShapeDtypeStruct(buf.shape, F32),
        scratch_shapes=[pltpu.VMEM((8, rows, LANES), F32), pltpu.SemaphoreType.DMA((7,)),
                        pltpu.SemaphoreType.DMA((7,))],
        )(buf)


def _pack_flat(arrs, dtype, width, row_mult=8):
    flat = jnp.concatenate([a.reshape(-1).astype(dtype) for a in arrs])
    pad = (-flat.shape[0]) % (row_mult * width)
    if pad:
        flat = jnp.concatenate([flat, jnp.zeros((pad,), dtype)])
    return flat.reshape(-1, width)


def _unpack_flat(buf, shapes):
    flat = buf.reshape(-1)
    out, off = [], 0
    for s in shapes:
        n = math.prod(s)
        out.append(flat[off:off + n].reshape(s))
        off += n
    return out


def _in_to_padded(w):
    parts = []
    for name in IN_NEW_ORDER:
        _, width, o_off, o_w = IN_LAYOUT[name]
        parts.append(w[..., o_off:o_off + o_w])
        if o_w < width:
            parts.append(jnp.zeros(w.shape[:-1] + (width - o_w,), w.dtype))
    used = sum(IN_LAYOUT[n][1] for n in IN_NEW_ORDER)
    parts.append(jnp.zeros(w.shape[:-1] + (IN_PAD - used,), w.dtype))
    return jnp.concatenate(parts, axis=-1)


def _in_from_padded(g):
    parts = []
    for name in IN_ORIG_ORDER:
        off, _, _, o_w = IN_LAYOUT[name]
        parts.append(g[..., off:off + o_w])
    return jnp.concatenate(parts, axis=-1)


def _pcol(p, name):
    off, width, _, _ = IN_LAYOUT[name]
    return Cols(p, off, width)


def _lane_pad(v, width=LANES):
    v = v.reshape(-1)
    return jnp.concatenate([v, jnp.zeros((width - v.shape[0],), v.dtype)]).reshape(1, width)


def _f_norm_mod(h, sh, sc, w):
    return (_rms(h, w) * (1.0 + sc) + sh,)


def _f_norm_mod_thru(h, sh, sc, w):
    return h, _rms(h, w) * (1.0 + sc) + sh


def _f_attn_prep(qraw, kraw, vraw, cos2, sin2, qw, kw, gq, gk):
    q = qraw * lax.rsqrt(_group_mean(qraw * qraw, gq) + NORM_EPS) * qw
    q = _rope32(q, jnp.tile(cos2, (1, 4)), jnp.tile(sin2, (1, 4))) * (ATTN_HD ** -0.5 * LOG2E)
    k = kraw * lax.rsqrt(_group_mean(kraw * kraw, gk) + NORM_EPS) * kw
    return q, _rope32(k, cos2, sin2), vraw


def _f_ssd_finish(yf, yb, xs, z, d_exp, nw):
    y = (yf + yb + d_exp * xs) * (z * jax.nn.sigmoid(z))
    return (_rms(y, nw),)


def _f_ret_prep(rq, rk, cos1, sin1):
    cos_full, sin_signed = jnp.tile(cos1, (1, 4)), jnp.tile(sin1, (1, 4))
    return _rope64(rq, cos_full, sin_signed), _rope64(rk, cos_full, sin_signed) * (RET_DK ** -0.5)


def _f_ret_finish(yf, yb, g, gw):
    y = yf + yb
    outs = []
    for h in range(RET_HEADS):
        yh = y[:, h * RET_DK:(h + 1) * RET_DK]
        yc = yh - jnp.mean(yh, axis=-1, keepdims=True)
        outs.append(yc * lax.rsqrt(jnp.mean(yc * yc, axis=-1, keepdims=True) + NORM_EPS))
    return (jnp.concatenate(outs, axis=1) * gw * (g * jax.nn.sigmoid(g)),)


def _f_merge(p0, p1, p2, g0, g1, g2):
    return (jax.nn.sigmoid(g0) * p0 + jax.nn.sigmoid(g1) * p1 + jax.nn.sigmoid(g2) * p2,)


def _f_mid(h, mix, g1, sh2, sc2, w2):
    h_mid = h + g1 * mix
    return h_mid, _rms(h_mid, w2) * (1.0 + sc2) + sh2


def _epi_merge_bwd(dmerged, p0, p1, p2, g0, g1, g2):
    dps, dgs = [], []
    for pb, gate in ((p0, g0), (p1, g1), (p2, g2)):
        s = jax.nn.sigmoid(gate)
        dps.append(dmerged * s)
        dgs.append(dmerged * pb.astype(F32) * (s * (1.0 - s)))
    return tuple(dps + dgs)


def _epi_mid(mix, h, g1, sh2, sc2, w2):
    return (mix,) + _f_mid(h, mix, g1, sh2, sc2, w2)


def _epi_resid(o, h_mid, g2):
    return (o,) + _f_residual(h_mid, o, g2)


def _epi_sqrelu(a):
    r = jnp.maximum(a, 0.0)
    return a, r * r


def _epi_sqrelu_bwd(dhh, a):
    return (dhh * (2.0 * jnp.maximum(a.astype(F32), 0.0)),)


def _f_residual(h_mid, o, g2):
    return (h_mid + g2 * o,)


def _f_silu(x):
    return (x * jax.nn.sigmoid(x),)


def _f_bias(x, b):
    return (x + b,)


def _ssd_rows(xbc, p):
    rows = [Cols(xbc, LANES * k, LANES) for k in range(4)]
    rows += [Cols(xbc, 512 + LANES * g, LANES) for g in range(2)]
    rows += [Cols(xbc, 768 + LANES * g, LANES) for g in range(2)]
    return rows + [_pcol(p, "dt")]


def _ret_rows(rq, rk, p):
    off_v = IN_LAYOUT["rv"][0]
    return ([Cols(rq, LANES * h, LANES) for h in range(4)] + [Cols(rk, LANES * h, LANES) for h in range(4)]
            + [Cols(p, off_v + LANES * h, LANES) for h in range(4)])


def layer_fwd(li, h, mod, lw, tabs, m_ctx):
    t = h.shape[0]
    nb = m_ctx // min(ROW_TILE, t)
    sh1, sc1, g1, sh2, sc2, g2 = mod
    nm = lambda s: f"l{li}_{s}"
    sv = {}
    (u,) = rowwise_fwd(nm("norm1"), _f_norm_mod, [h], [sh1, sc1], [lw["norm1_w"]], [(D_MODEL, CDT)], t, nb)
    p = mm(nm("in_proj"), u, lw["w_in"], F32)
    q, k, v = rowwise_fwd(
        nm("attn_prep"), _f_attn_prep,
        [_pcol(p, "q"), _pcol(p, "k"), _pcol(p, "v"), tabs["ca"], tabs["sa"]], [],
        [lw["qw"], lw["kw"], tabs["gq"], tabs["gk"]], [(512, CDT), (128, CDT), (128, CDT)], t, nb)
    tq = min(ATTN_TQ, m_ctx)
    kk, vv = _split_kv(k), _split_kv(v)
    ones_rows = jnp.concatenate([jnp.ones((2, 1, t), CDT), jnp.zeros((2, ATTN_ONES_ROWS - 1, t), CDT)], axis=1)
    attn_o, qT, oT, lse = attn_fwd(nm("attn"), q, kk, jnp.concatenate([vv.transpose(0, 2, 1), ones_rows], axis=1),
                                   m_ctx, tq)

    xbc = conv_fwd(_pcol(p, "xbc"), lw["conv_w"], lw["conv_b"], m_ctx)
    ssd_sh = [lw["dt_bias"], lw["a_log"]]
    yf, yb, sf, sb = scan_fwd(nm("ssd"), _make_ssd_chunk, _ssd_rows(xbc, p), ssd_sh, 4, 512, t, m_ctx)
    (ssd_o,) = rowwise_fwd(nm("ssd_fin"), _f_ssd_finish, [yf, yb, Cols(xbc, 0, 512), _pcol(p, "z")], [],
                           [lw["d_exp"], lw["ssd_nw"]], [(512, CDT)], t, nb)

    rq, rk = rowwise_fwd(nm("ret_prep"), _f_ret_prep, [_pcol(p, "rq"), _pcol(p, "rk"), tabs["rc"], tabs["rs"]],
                         [], [], [(512, F32), (512, F32)], t, nb)
    rf, rb, rsf, rsb = scan_fwd(nm("ret"), _make_ret_chunk, _ret_rows(rq, rk, p), [lw["ret_lg"]], 4, 512, t, m_ctx)
    (ret_o,) = rowwise_fwd(nm("ret_fin"), _f_ret_finish, [rf, rb, _pcol(p, "rg")], [], [lw["ret_gw"]],
                           [(512, CDT)], t, nb)

    pbs = [mm(nm(f"branch{b}"), br, lw["w_branch"][b], CDT) for b, br in enumerate((attn_o, ssd_o, ret_o))]
    gl = [Cols(p, 1024 * b, 1024) for b in range(3)]
    (merged,) = rowwise_fwd(nm("merge"), _f_merge, pbs + gl, [], [], [(D_MODEL, CDT)], t, nb)
    mix, h_mid, vv2 = mm(nm("out_proj"), merged, lw["w_out"], None, epilogue=_epi_mid, tile_ins=[h],
                         typed_ins=[g1, sh2, sc2], shared_ins=[lw["norm2_w"]], out_dtypes=[F32, F32, CDT], nb_ctx=nb,
                         whole_rows=True)
    a, hh = mm(nm("mlp1"), vv2, lw["w_mlp1"], None, epilogue=_epi_sqrelu, out_dtypes=[CDT, CDT])
    o, h_out = mm(nm("mlp2"), hh, lw["w_mlp2"], None, epilogue=_epi_resid, tile_ins=[h_mid], typed_ins=[g2],
                  out_dtypes=[F32, F32], nb_ctx=nb)
    sv.update(h=h, u=u, p=p, qT=qT, kk=kk, vv=vv, oT=oT, lse=lse, attn_o=attn_o, xbc=xbc, yf=yf, yb=yb,
              sf=sf, sb=sb, ssd_o=ssd_o, rq=rq, rk=rk, rf=rf, rb=rb, rsf=rsf, rsb=rsb, ret_o=ret_o, pbs=pbs,
              merged=merged, mix=mix, h_mid=h_mid, v=vv2, a=a, hh=hh, o=o)
    return h_out, sv


def layer_bwd(li, dh_out, sv, mod, lw, tabs, m_ctx):
    t = dh_out.shape[0]
    nb = m_ctx // min(ROW_TILE, t)
    sh1, sc1, g1, sh2, sc2, g2 = mod
    nm = lambda s: f"l{li}_{s}_bwd"
    gw = {}
    p = sv["p"]
    (do,), (dg2,), _ = rowwise_bwd(nm("resid"), _f_residual, [sv["h_mid"], sv["o"]], [g2], [], [dh_out],
                                   [False, True], [], [CDT], t, nb)
    (da,) = mm(nm("mlp2_dx"), do, lw["w_mlp2"], None, transpose_b=True, epilogue=_epi_sqrelu_bwd,
               tile_ins=[sv["a"]], out_dtypes=[CDT])
    gw["w_mlp2"] = mm_tn(nm("mlp2_dw"), sv["hh"], do, CDT, ("rows", lw["w_mlp2"].shape[0] // 4))
    dv = mm(nm("mlp1_dx"), da, lw["w_mlp1"], F32, transpose_b=True)
    gw["w_mlp1"] = mm_tn(nm("mlp1_dw"), sv["v"], da, CDT, ("cols", lw["w_mlp1"].shape[1] // 4))
    (dh_a, dmix), (dg1, dsh2, dsc2), (gw["norm2_w"],) = rowwise_bwd(
        nm("mid"), _f_mid, [sv["h"], sv["mix"]], [g1, sh2, sc2], [lw["norm2_w"]], [dh_out, dv],
        [True, True], [True], [F32, CDT], t, nb)
    gl = [Cols(p, 1024 * b, 1024) for b in range(3)]
    dmg = mm(nm("out_dx"), dmix, lw["w_out"], None, transpose_b=True, epilogue=_epi_merge_bwd,
             tile_ins=sv["pbs"] + gl, out_dtypes=[CDT] * 6)
    gw["w_out"] = mm_tn(nm("out_dw"), sv["merged"], dmix, CDT, ("rows", lw["w_out"].shape[0] // 4))
    dpb, dgl = dmg[:3], dmg[3:]
    brs = (sv["attn_o"], sv["ssd_o"], sv["ret_o"])
    d_attn_o = mm(nm("branch0_dx"), dpb[0], lw["w_branch"][0], CDT, transpose_b=True)
    d_ssd_o = mm(nm("branch1_dx"), dpb[1], lw["w_branch"][1], F32, transpose_b=True)
    d_ret_o = mm(nm("branch2_dx"), dpb[2], lw["w_branch"][2], F32, transpose_b=True)
    n_loc = lw["w_branch"].shape[2] // 4
    gw["w_branch"] = jnp.stack([mm_tn(nm(f"branch{b}_dw"), brs[b], dpb[b], CDT, ("cols", n_loc)) for b in range(3)],
                               axis=1).reshape(4, -1, n_loc)
    tq = min(ATTN_TQ, m_ctx)
    dq_rows, dk_s, dv_s = attn_bwd(nm("attn"), sv["qT"], d_attn_o, sv["oT"], sv["lse"], sv["kk"],
                                   sv["kk"].transpose(0, 2, 1), sv["vv"], m_ctx)
    (dq_raw, dk_raw, dv_raw), _, (gw["qw"], gw["kw"]) = rowwise_bwd(
        nm("attn_prep"), _f_attn_prep,
        [_pcol(p, "q"), _pcol(p, "k"), _pcol(p, "v"), tabs["ca"], tabs["sa"]], [],
        [lw["qw"], lw["kw"], tabs["gq"], tabs["gk"]],
        [dq_rows, _merge_kv(dk_s) * LN2, _merge_kv(dv_s)],
        [True, True, True, False, False], [True, True, False, False], [CDT] * 3, t, nb)
    (dy_ssd, dxs_fin, dz), _, (gw["d_exp"], gw["ssd_nw"]) = rowwise_bwd(
        nm("ssd_fin"), _f_ssd_finish, [sv["yf"], sv["yb"], Cols(sv["xbc"], 0, 512), _pcol(p, "z")], [],
        [lw["d_exp"], lw["ssd_nw"]], [d_ssd_o], [True, False, True, True], [True, True], [F32, F32, CDT], t, nb)
    ssd_sh = [lw["dt_bias"], lw["a_log"]]
    post_ssd = lambda d: [jnp.concatenate(d[0:8], axis=1), d[8]]
    (dxbc_f, ddt_f), dsh_f = scan_bwd(nm("ssd_f"), _make_ssd_chunk, _ssd_rows(sv["xbc"], p), ssd_sh, (sv["sf"],),
                                      dy_ssd, post_ssd, [(1024, F32), (LANES, F32)], 4, t, m_ctx, dirs=(0,))
    (dxbc_b, ddt_b), dsh_b = scan_bwd(nm("ssd_b"), _make_ssd_chunk, _ssd_rows(sv["xbc"], p), ssd_sh, (sv["sb"],),
                                      dy_ssd, post_ssd, [(1024, F32), (LANES, F32)], 4, t, m_ctx, dirs=(1,))
    gw["dt_bias"], gw["a_log"] = dsh_f[0] + dsh_b[0], dsh_f[1] + dsh_b[1]
    ddt = (ddt_f + ddt_b).astype(CDT)
    dxbc_raw, gw["conv_w"], gw["conv_b"] = conv_bwd(_pcol(p, "xbc"), lw["conv_w"], lw["conv_b"], dxbc_f, dxbc_b,
                                                    dxs_fin, m_ctx)
    (dy_ret, drg), _, (gw["ret_gw"],) = rowwise_bwd(
        nm("ret_fin"), _f_ret_finish, [sv["rf"], sv["rb"], _pcol(p, "rg")], [], [lw["ret_gw"]], [d_ret_o],
        [True, False, True], [True], [F32, CDT], t, nb)
    post_ret = lambda d: [jnp.concatenate(d[0:4], axis=1), jnp.concatenate(d[4:8], axis=1),
                          jnp.concatenate(d[8:12], axis=1)]
    rrows = _ret_rows(sv["rq"], sv["rk"], p)
    (dq_f, dk_f, dv_f), (dq_b, dk_b, dv_b), (gw["ret_lg"],) = scan_bwd(
        nm("ret"), _make_ret_chunk, rrows, [lw["ret_lg"]], (sv["rsf"], sv["rsb"]), dy_ret, post_ret,
        [(512, F32)] * 3, 4, t, m_ctx)
    drv = (dv_f + dv_b).astype(CDT)
    (drq, drk), _, _ = rowwise_bwd(nm("ret_prep"), _f_ret_prep,
                                   [_pcol(p, "rq"), _pcol(p, "rk"), tabs["rc"], tabs["rs"]], [], [],
                                   [(dq_f, dq_b), (dk_f, dk_b)], [True, True, False, False], [], [CDT, CDT], t, nb)
    pieces = {"gates": None, "xbc": dxbc_raw, "q": dq_raw, "z": dz, "rq": drq, "rk": drk, "rv": drv, "rg": drg,
              "k": dk_raw, "v": dv_raw, "dt": ddt}
    cols = list(dgl) + [pieces[n] for n in IN_NEW_ORDER[1:]]
    used = sum(c.shape[1] for c in cols)
    cols.append(jnp.zeros((t, IN_PAD - used), CDT))
    dp = jnp.concatenate(cols, axis=1)
    du = mm(nm("in_dx"), dp, lw["w_in"], F32, transpose_b=True)
    gw["w_in"] = mm_tn(nm("in_dw"), sv["u"], dp, CDT)
    (dh_in,), (dsh1, dsc1), (gw["norm1_w"],) = rowwise_bwd(
        nm("norm1"), _f_norm_mod_thru, [sv["h"]], [sh1, sc1], [lw["norm1_w"]], [dh_a, du], [True], [True], [F32],
        t, nb)
    return dh_in, [dsh1, dsc1, dg1, dsh2, dsc2, dg2], gw


def _rope_tables(n_lat, m_ctx):
    rows = n_lat // GRID_W
    row = jnp.repeat(jnp.arange(rows, dtype=F32), GRID_W)
    col = jnp.tile(jnp.arange(GRID_W, dtype=F32), rows)
    nfreq = ATTN_HD // 4
    inv = ROPE_THETA ** (-jnp.arange(nfreq, dtype=F32) / nfreq)
    ang = jnp.concatenate([row[:, None] * inv, col[:, None] * inv], axis=-1)
    cos = jnp.concatenate([jnp.ones((m_ctx, ATTN_HD // 2), F32), jnp.cos(ang)], axis=0)
    sin = jnp.concatenate([jnp.zeros((m_ctx, ATTN_HD // 2), F32), jnp.sin(ang)], axis=0)
    c64 = jnp.concatenate([cos, cos], axis=1)
    s64 = jnp.concatenate([-sin, sin], axis=1)
    pos = jnp.arange(m_ctx + n_lat, dtype=F32)
    inv_r = ROPE_THETA ** (-jnp.linspace(0.0, 1.0, RET_DK // 2, dtype=F32))
    ang_r = pos[:, None] * inv_r
    rc = jnp.concatenate([jnp.cos(ang_r)] * 2, axis=1)
    rs = jnp.concatenate([-jnp.sin(ang_r), jnp.sin(ang_r)], axis=1)
    return dict(ca=jnp.tile(c64, (1, 2)), sa=jnp.tile(s64, (1, 2)), rc=rc, rs=rs, gq=_group_matrix(512, ATTN_HD),
                gk=_group_matrix(128, ATTN_HD))


def _layer_weights(full, small, layer):
    return dict(
        w_in=full["w_in"][layer], w_branch=full["w_branch"][layer], w_out=full["w_out"][layer],
        w_mlp1=full["w_mlp1"][layer], w_mlp2=full["w_mlp2"][layer],
        norm1_w=small["norm1_w"][layer][None], norm2_w=small["norm2_w"][layer][None],
        qw=jnp.tile(small["attn_q_norm"][layer], 8)[None], kw=jnp.tile(small["attn_k_norm"][layer], 2)[None],
        conv_w=small["ssd_conv_w"][layer], conv_b=small["ssd_conv_b"][layer][None],
        dt_bias=_lane_pad(small["ssd_dt_bias"][layer]), a_log=_lane_pad(small["ssd_a_log"][layer]),
        d_exp=jnp.repeat(small["ssd_d"][layer], SSD_HD)[None], ssd_nw=small["ssd_norm_w"][layer][None],
        ret_lg=_lane_pad(small["ret_log_decay"][layer]), ret_gw=small["ret_gn_w"][layer][None])


def local_step(x, c, ctx, full, small, loss_target):
    n_lat, d = x.shape
    m_ctx = ctx.shape[0]
    t = n_lat + m_ctx
    depth = small["norm1_w"].shape[0]
    tabs = _rope_tables(n_lat, m_ctx)
    h = jnp.concatenate([ctx, x], axis=0)
    cc = jnp.concatenate([small["c_ctx"][None], c, jnp.zeros((COND_ROWS - 2, d), F32)], axis=0)
    (scc,) = rowwise_fwd("cond_silu", _f_silu, [cc], [], [], [(d, CDT)], COND_ROWS, 0)
    mods, saved, lws = [], [], []
    for layer in range(depth):
        lw = _layer_weights(full, small, layer)
        mod_raw = mm(f"l{layer}_mod", scc, full["w_mod"][layer], F32)
        (mod8,) = rowwise_fwd(f"l{layer}_mod_bias", _f_bias, [mod_raw], [], [small["b_mod"][layer][None]],
                              [(6 * d, F32)], COND_ROWS, 0)
        mod = [mod8[0:2, k * d:(k + 1) * d].reshape(2, 1, d) for k in range(6)]
        h, sv = layer_fwd(layer, h, mod, lw, tabs, m_ctx)
        mods.append(mod)
        saved.append(sv)
        lws.append(lw)
    loss, dh, d_final = loss_head(h, loss_target, small["final_norm_w"][None], m_ctx)

    gbig = {k: [None] * depth for k in BIG}
    gs = {k: [None] * depth for k in SMALL if k not in ("c_ctx", "final_norm_w")}
    d_scc = None
    for layer in reversed(range(depth)):
        lw = lws[layer]
        dh, dmod, gw = layer_bwd(layer, dh, saved[layer], mods[layer], lw, tabs, m_ctx)
        dmod8 = jnp.concatenate([jnp.concatenate([g_.reshape(2, d) for g_ in dmod], axis=1),
                                 jnp.zeros((COND_ROWS - 2, 6 * d), F32)], axis=0)
        (dmod_c,), _, (db_mod,) = rowwise_bwd(f"l{layer}_mod_bias_bwd", _f_bias, [dmod8], [],
                                              [small["b_mod"][layer][None]], [dmod8], [True], [True], [CDT], COND_ROWS, 0)
        gbig["w_mod"][layer] = mm_tn(f"l{layer}_mod_dw", scc, dmod_c, CDT, ("cols", 6 * d // 4))
        part = mm(f"l{layer}_mod_dx", dmod_c, full["w_mod"][layer], F32, transpose_b=True)
        d_scc = part if d_scc is None else d_scc + part
        g_in = _in_from_padded(gw["w_in"])
        gbig["w_in"][layer] = g_in.reshape(d, 4, g_in.shape[1] // 4).transpose(1, 0, 2)
        for k in ("w_branch", "w_out", "w_mlp1", "w_mlp2"):
            gbig[k][layer] = gw[k]
        gs["b_mod"][layer] = db_mod.reshape(-1)
        gs["norm1_w"][layer] = gw["norm1_w"].reshape(-1)
        gs["norm2_w"][layer] = gw["norm2_w"].reshape(-1)
        gs["attn_q_norm"][layer] = gw["qw"].reshape(8, ATTN_HD).sum(0)
        gs["attn_k_norm"][layer] = gw["kw"].reshape(2, ATTN_HD).sum(0)
        gs["ssd_conv_w"][layer] = gw["conv_w"]
        gs["ssd_conv_b"][layer] = gw["conv_b"].reshape(-1)
        gs["ssd_dt_bias"][layer] = gw["dt_bias"][0, :16].reshape(2, 8)
        gs["ssd_a_log"][layer] = gw["a_log"][0, :16].reshape(2, 8)
        gs["ssd_d"][layer] = gw["d_exp"].reshape(SSD_HEADS, SSD_HD).sum(1)
        gs["ssd_norm_w"][layer] = gw["ssd_nw"].reshape(-1)
        gs["ret_log_decay"][layer] = gw["ret_lg"][0, :8].reshape(2, 4)
        gs["ret_gn_w"][layer] = gw["ret_gw"].reshape(-1)
    (d_cc,), _, _ = rowwise_bwd("cond_silu_bwd", _f_silu, [cc], [], [], [d_scc], [True], [], [F32], COND_ROWS, 0)
    g_small = {k: jnp.stack(v) for k, v in gs.items()}
    g_small["c_ctx"] = d_cc[0]
    g_small["final_norm_w"] = d_final.reshape(-1)
    return loss, dh[m_ctx:], gbig, g_small


def kernel(x, c, ctx, c_ctx, w_mod, b_mod, norm1_w, norm2_w, w_in, attn_q_norm, attn_k_norm, ssd_conv_w, ssd_conv_b, ssd_dt_bias, ssd_a_log, ssd_d, ssd_norm_w, ret_log_decay, ret_gn_w, w_branch, w_out, w_mlp1, w_mlp2, final_norm_w, loss_target, m_c_ctx, m_w_mod, m_b_mod, m_norm1_w, m_norm2_w, m_w_in, m_attn_q_norm, m_attn_k_norm, m_ssd_conv_w, m_ssd_conv_b, m_ssd_dt_bias, m_ssd_a_log, m_ssd_d, m_ssd_norm_w, m_ret_log_decay, m_ret_gn_w, m_w_branch, m_w_out, m_w_mlp1, m_w_mlp2, m_final_norm_w, v_c_ctx, v_w_mod, v_b_mod, v_norm1_w, v_norm2_w, v_w_in, v_attn_q_norm, v_attn_k_norm, v_ssd_conv_w, v_ssd_conv_b, v_ssd_dt_bias, v_ssd_a_log, v_ssd_d, v_ssd_norm_w, v_ret_log_decay, v_ret_gn_w, v_w_branch, v_w_out, v_w_mlp1, v_w_mlp2, v_final_norm_w):
    env = dict(locals())
    w_loc = {k: env[k] for k in WEIGHTS}
    m_loc = {k: env["m_" + k] for k in WEIGHTS}
    v_loc = {k: env["v_" + k] for k in WEIGHTS}
    chip = 2 * lax.axis_index("x") + lax.axis_index("y")
    core = lax.axis_index("c")

    depth = w_loc["w_mod"].shape[0]
    assert depth == 2, "the exchanges split the layers between a chip's two cores"
    shards = [w_loc[k].astype(CDT).reshape(depth, -1, w_loc[k].shape[-1]) for k in BIG]
    mine = gather_layers("gather_weights", shards, [BIG_KIND[k] for k in BIG])
    full = {}
    for k, arr in zip(BIG, mine):
        both = exchange_both("share_" + k, arr.reshape(-1, arr.shape[-1]))
        if k == "w_in":
            both = both.reshape(depth, 4, -1, both.shape[-1]).transpose(0, 2, 1, 3)
            both = _in_to_padded(both.reshape(depth, both.shape[1], -1))
        full[k] = both.reshape((depth,) + w_loc[k].shape[1:-1] + (-1,)) if BIG_KIND[k] == "cols" else \
            both.reshape((depth,) + w_loc[k].shape[1:-2] + (-1, w_loc[k].shape[-1])) if BIG_KIND[k] == "rows" else both

    cw = w_loc["ssd_conv_w"]
    cw_w = cw.shape[-1]
    placed = lax.dynamic_update_slice(jnp.zeros(cw.shape[:-1] + (4 * cw_w,), F32),
                                      cw * (core == 0).astype(F32), (0, 0, chip * cw_w))
    conv_full = _unpack_flat(allreduce_small("gather_conv_w", _pack_flat([placed], F32, LANES)), [placed.shape])[0]
    small = {k: w_loc[k] for k in SMALL}
    small["ssd_conv_w"] = conv_full

    loss_l, grad_x, g_big, g_small = local_step(x[0], c, ctx[0], full, small, loss_target[0])

    small_shapes = [g_small[k].shape for k in SMALL] + [(LANES,)]
    summed = _unpack_flat(allreduce_small("reduce_small", _pack_flat([g_small[k] for k in SMALL] + [loss_l], F32, LANES)),
                          small_shapes)
    gsum = dict(zip(SMALL, summed[:-1]))
    loss = summed[-1][0]
    gsum["ssd_conv_w"] = lax.dynamic_slice(gsum["ssd_conv_w"], (0, 0, chip * cw_w), cw.shape)

    pair = []
    for k in BIG:
        _, rows, cols = g_big[k][0].shape
        pair.append(exchange_add("pair_" + k, g_big[k][0].reshape(4 * rows, cols),
                                 g_big[k][1].reshape(4 * rows, cols)).reshape(4, rows, cols))
    landed = scatter_pieces("scatter_grads", pair)
    g_sum = [sum_exchange("sum_" + k, parts) for k, parts in zip(BIG, landed)]

    grads, deltas, new_m, new_v = {}, {}, {}, {}
    for i, k in enumerate(BIG):
        shp = w_loc[k].shape
        three_d = lambda a, shp=shp: a.reshape((-1,) + shp[-2:])
        res = adamw("adamw_" + k, three_d(w_loc[k]), three_d(m_loc[k]), three_d(v_loc[k]), [three_d(g_sum[i])])
        grads[k], deltas[k], new_m[k], new_v[k] = [r.reshape(shp) for r in res]
    small_loc_shapes = [w_loc[k].shape for k in SMALL]
    res = adamw("adamw_small", _pack_flat([w_loc[k] for k in SMALL], F32, LANES)[None],
                _pack_flat([m_loc[k] for k in SMALL], F32, LANES)[None],
                _pack_flat([v_loc[k] for k in SMALL], F32, LANES)[None],
                [_pack_flat([gsum[k] for k in SMALL], F32, LANES)[None]])
    for dst, r in zip((grads, deltas, new_m, new_v), res):
        dst.update(dict(zip(SMALL, _unpack_flat(r, small_loc_shapes))))

    return (loss, grad_x[None], *[grads[k] for k in WEIGHTS], *[deltas[k] for k in WEIGHTS],
            *[new_m[k] for k in WEIGHTS], *[new_v[k] for k in WEIGHTS])
```

```python
import math
from typing import NamedTuple

import jax
import jax.numpy as jnp
from jax import lax
from jax.experimental import pallas as pl
from jax.experimental.pallas import tpu as pltpu

F32 = jnp.float32
CDT = jnp.bfloat16
NORM_EPS = 1e-6
ROPE_THETA = 10000.0
GRID_W = 64
D_MODEL = 1024
ATTN_HD = 64
SSD_HEADS, SSD_HD = 8, 64
RET_HEADS, RET_DK = 4, 128
CHUNK = 256
ROW_TILE = 256
ROW_TILE_TALL = 768
MM_ROWS = 768
MM_TN_ROWS = 2816
MM_VMEM_BUDGET = 44 * 1024 * 1024
ATTN_TQ, ATTN_TK = 256, 256
ATTN_ONES_ROWS = 16
ATTN_TK_BWD = 2048
LOG2E, LN2 = 1.4426950408889634, 0.6931471805599453
ATTN_TK_FWD = 2048
LANES = 128
PAIR_BLOCK_BYTES = 2 * 1024 * 1024
COND_ROWS = 16
VMEM_LIMIT = 56 * 1024 * 1024

ADAM_LR, ADAM_B1, ADAM_B2, ADAM_EPS, ADAM_WD, ADAM_STEP = 0.001, 0.9, 0.999, 1e-08, 0.01, 10

IN_LAYOUT = {
    "gates": (0, 3072, 4368, 3072), "xbc": (3072, 1024, 1280, 1024), "q": (4096, 512, 0, 512),
    "z": (4608, 512, 768, 512), "rq": (5120, 512, 2320, 512), "rk": (5632, 512, 2832, 512),
    "rv": (6144, 512, 3344, 512), "rg": (6656, 512, 3856, 512), "k": (7168, 128, 512, 128),
    "v": (7296, 128, 640, 128), "dt": (7424, 128, 2304, 16),
}
IN_PAD = 7680
IN_ORIG_ORDER = ("q", "k", "v", "z", "xbc", "dt", "rq", "rk", "rv", "rg", "gates")
IN_NEW_ORDER = ("gates", "xbc", "q", "z", "rq", "rk", "rv", "rg", "k", "v", "dt")

BIG = ("w_mod", "w_in", "w_branch", "w_out", "w_mlp1", "w_mlp2")
BIG_KIND = {"w_mod": "cols", "w_in": "slices", "w_branch": "cols", "w_out": "rows", "w_mlp1": "cols", "w_mlp2": "rows"}
SMALL = ("c_ctx", "b_mod", "norm1_w", "norm2_w", "attn_q_norm", "attn_k_norm", "ssd_conv_w", "ssd_conv_b",
         "ssd_dt_bias", "ssd_a_log", "ssd_d", "ssd_norm_w", "ret_log_decay", "ret_gn_w", "final_norm_w")
WEIGHTS = ("c_ctx", "w_mod", "b_mod", "norm1_w", "norm2_w", "w_in", "attn_q_norm", "attn_k_norm", "ssd_conv_w",
           "ssd_conv_b", "ssd_dt_bias", "ssd_a_log", "ssd_d", "ssd_norm_w", "ret_log_decay", "ret_gn_w",
           "w_branch", "w_out", "w_mlp1", "w_mlp2", "final_norm_w")


def _cp(sem):
    return pltpu.CompilerParams(dimension_semantics=sem, vmem_limit_bytes=VMEM_LIMIT)


class Cols(NamedTuple):
    arr: jax.Array
    off: int
    width: int


def _width(item):
    return item.width if isinstance(item, Cols) else item.shape[1]


def _row_in(item, rows, imap=None):
    imap = imap or (lambda i: i)
    if isinstance(item, Cols):
        assert item.off % item.width == 0
        blk = item.off // item.width
        return item.arr, pl.BlockSpec((rows, item.width), lambda i, blk=blk: (imap(i), blk))
    return item, pl.BlockSpec((rows, item.shape[1]), lambda i: (imap(i), 0))


def _const_spec(shape):
    return pl.BlockSpec(shape, lambda *_: (0,) * len(shape))


def _mxu(a, b, dims=(((1,), (0,)), ((), ()))):
    return lax.dot_general(a.astype(CDT), b.astype(CDT), dims, preferred_element_type=F32)


_NT = (((1,), (1,)), ((), ()))
_TN = (((0,), (0,)), ((), ()))


@jax.custom_vjp
def _softplus(x):
    return jnp.maximum(x, 0.0) + jnp.log1p(jnp.exp(-jnp.abs(x)))


def _softplus_fwd(x):
    return _softplus(x), x


def _softplus_bwd(x, g):
    return (g * jax.nn.sigmoid(x),)


_softplus.defvjp(_softplus_fwd, _softplus_bwd)


def _group_mean_impl(x, gmat):
    hi = x.astype(CDT)
    lo = (x - hi.astype(F32)).astype(CDT)
    return (jnp.dot(hi, gmat, preferred_element_type=F32) + jnp.dot(lo, gmat, preferred_element_type=F32))


@jax.custom_vjp
def _group_mean(x, gmat):
    return _group_mean_impl(x, gmat)


def _group_mean_fwd(x, gmat):
    return _group_mean_impl(x, gmat), gmat


def _group_mean_bwd(gmat, g):
    return _group_mean_impl(g, gmat), jnp.zeros_like(gmat)


_group_mean.defvjp(_group_mean_fwd, _group_mean_bwd)


def _group_matrix(width, group):
    r = jnp.arange(width) // group
    return jnp.where(r[:, None] == r[None, :], 1.0 / group, 0.0).astype(CDT)


def _make_rope(half):
    def partner(x):
        w = x.shape[1]
        lane = lax.broadcasted_iota(jnp.int32, x.shape, 1)
        first = (lane % (2 * half)) < half
        return jnp.where(first, pltpu.roll(x, w - half, axis=1), pltpu.roll(x, half, axis=1))

    def impl(x, cos_full, sin_signed):
        return x * cos_full + partner(x) * sin_signed

    @jax.custom_vjp
    def rope(x, cos_full, sin_signed):
        return impl(x, cos_full, sin_signed)

    def fwd(x, cos_full, sin_signed):
        return impl(x, cos_full, sin_signed), (cos_full, sin_signed)

    def bwd(res, g):
        cos_full, sin_signed = res
        return impl(g, cos_full, -sin_signed), jnp.zeros_like(cos_full), jnp.zeros_like(sin_signed)

    rope.defvjp(fwd, bwd)
    return rope


_rope32 = _make_rope(32)
_rope64 = _make_rope(64)


def _rms(x, w):
    return x * lax.rsqrt(jnp.mean(x * x, axis=-1, keepdims=True) + NORM_EPS) * w


def _col(v, lane_index):
    lane = lax.broadcasted_iota(jnp.int32, v.shape, 1)
    return jnp.sum(jnp.where(lane == lane_index, v, 0.0), axis=1, keepdims=True)


def _typed_spec(width, nb_ctx):
    return pl.BlockSpec((None, 1, width), lambda i: (jnp.where(i >= nb_ctx, 1, 0), 0, 0))


def _row_tile(n_rows, typed):
    return min(ROW_TILE, n_rows) if typed else _pick(n_rows, (ROW_TILE_TALL, ROW_TILE))


def rowwise_fwd(name, f, rows, typed, shared, outs, n_rows, nb_ctx):
    tm = _row_tile(n_rows, typed)
    nin = len(rows) + len(typed) + len(shared)

    def body(*refs):
        res = f(*[r[...] for r in refs[:nin]])
        for o_ref, o in zip(refs[nin:], res):
            o_ref[...] = o.astype(o_ref.dtype)

    arrs, specs = [], []
    for it in rows:
        a, s = _row_in(it, tm)
        arrs.append(a)
        specs.append(s)
    for t in typed:
        arrs.append(t)
        specs.append(_typed_spec(t.shape[-1], nb_ctx))
    for s_ in shared:
        arrs.append(s_)
        specs.append(_const_spec(s_.shape))
    res = pl.pallas_call(
        body, name=name, grid=(n_rows // tm,), in_specs=specs,
        out_specs=[pl.BlockSpec((tm, w), lambda i: (i, 0)) for w, _ in outs],
        out_shape=[jax.ShapeDtypeStruct((n_rows, w), dt) for w, dt in outs],
        compiler_params=_cp(("parallel",)))(*arrs)
    return res


def rowwise_bwd(name, f, rows, typed, shared, cots, row_diff, shared_diff, drow_dtypes, n_rows, nb_ctx):
    tm = _row_tile(n_rows, typed)
    cot_groups = [c_ if isinstance(c_, tuple) else (c_,) for c_ in cots]
    cots = [a for grp in cot_groups for a in grp]
    nr, nt, ns, nc = len(rows), len(typed), len(shared), len(cots)
    nin = nr + nt + ns
    d_rows = [k for k in range(nr) if row_diff[k]]
    d_sh = [k for k in range(ns) if shared_diff[k]]

    def body(*refs):
        rvals = [r[...] for r in refs[:nr]]
        tvals = [r[...] for r in refs[nr:nr + nt]]
        svals = [r[...] for r in refs[nr + nt:nin]]
        cparts = [r[...].astype(F32) for r in refs[nin:nin + nc]]
        cvals = []
        for grp in cot_groups:
            cvals.append(sum(cparts[1:len(grp)], cparts[0]))
            cparts = cparts[len(grp):]
        out_refs = refs[nin + nc:]

        def g(*dv):
            dv = list(dv)
            rv = list(rvals)
            for k in d_rows:
                rv[k] = dv.pop(0)
            tv = [dv.pop(0) for _ in range(nt)]
            sv = list(svals)
            for k in d_sh:
                sv[k] = dv.pop(0)
            return tuple(o.astype(F32) for o in f(*rv, *tv, *sv))

        prim = [rvals[k].astype(F32) for k in d_rows] + tvals + [svals[k] for k in d_sh]
        _, vjp = jax.vjp(g, *prim)
        grads = list(vjp(tuple(cvals)))
        i = pl.program_id(0)
        for ref in out_refs[:len(d_rows)]:
            ref[...] = grads.pop(0).astype(ref.dtype)
        first_typed = (i == 0) | (i == nb_ctx)
        for ref in out_refs[len(d_rows):len(d_rows) + nt]:
            gr = grads.pop(0)

            @pl.when(first_typed)
            def _(ref=ref, gr=gr):
                ref[...] = gr

            @pl.when(jnp.logical_not(first_typed))
            def _(ref=ref, gr=gr):
                ref[...] += gr
        for ref in out_refs[len(d_rows) + nt:]:
            gr = grads.pop(0)

            @pl.when(i == 0)
            def _(ref=ref, gr=gr):
                ref[...] = gr

            @pl.when(i != 0)
            def _(ref=ref, gr=gr):
                ref[...] += gr

    arrs, specs = [], []
    for it in list(rows):
        a, s = _row_in(it, tm)
        arrs.append(a)
        specs.append(s)
    for t in typed:
        arrs.append(t)
        specs.append(_typed_spec(t.shape[-1], nb_ctx))
    for s_ in shared:
        arrs.append(s_)
        specs.append(_const_spec(s_.shape))
    for c_ in cots:
        a, s = _row_in(c_, tm)
        arrs.append(a)
        specs.append(s)
    out_specs, out_shape = [], []
    for k, dt in zip(d_rows, drow_dtypes):
        w = _width(rows[k])
        out_specs.append(pl.BlockSpec((tm, w), lambda i: (i, 0)))
        out_shape.append(jax.ShapeDtypeStruct((n_rows, w), dt))
    for t in typed:
        out_specs.append(_typed_spec(t.shape[-1], nb_ctx))
        out_shape.append(jax.ShapeDtypeStruct(t.shape, F32))
    for k in d_sh:
        out_specs.append(_const_spec(shared[k].shape))
        out_shape.append(jax.ShapeDtypeStruct(shared[k].shape, F32))
    res = pl.pallas_call(body, name=name, grid=(n_rows // tm,), in_specs=specs, out_specs=out_specs,
                         out_shape=out_shape, compiler_params=_cp(("arbitrary",)))(*arrs)
    n1, n2 = len(d_rows), len(d_rows) + nt
    return list(res[:n1]), list(res[n1:n2]), list(res[n2:])


def _pick(n, prefs):
    for p in prefs:
        if n % p == 0:
            return p
    return n


def mm(name, a, b, out_dtype, transpose_b=False, epilogue=None, tile_ins=(), typed_ins=(), shared_ins=(),
       out_dtypes=None, nb_ctx=None, whole_rows=False):
    n, k = b.shape if transpose_b else b.shape[::-1]
    m = (a.arr if isinstance(a, Cols) else a).shape[0]
    assert _width(a) == k
    tm = min(ROW_TILE, m) if typed_ins else _pick(m, (MM_ROWS, 256))
    out_dtypes = out_dtypes or [out_dtype]
    tile_arrs = [x.arr if isinstance(x, Cols) else x for x in tile_ins]
    tile_bytes = sum(jnp.dtype(d).itemsize for d in out_dtypes) + sum(x.dtype.itemsize for x in tile_arrs)
    fits = lambda c: 2 * (tm * k * 2 + k * c * 2 + tm * c * tile_bytes) <= MM_VMEM_BUDGET
    tn = next(c for c in (2560, 2048, 1536, 1024, 512, 256, 128, n) if n % c == 0 and fits(c) or c == n)
    if whole_rows:
        assert fits(n)
        tn = n
    dims = _NT if transpose_b else (((1,), (0,)), ((), ()))
    n_ex = len(tile_ins) + len(typed_ins) + len(shared_ins)

    def body(a_ref, b_ref, *refs):
        prod = lax.dot_general(a_ref[...], b_ref[...], dims, preferred_element_type=F32)
        outs = (prod,) if epilogue is None else epilogue(prod, *[r[...] for r in refs[:n_ex]])
        for o_ref, o in zip(refs[n_ex:], outs):
            o_ref[...] = o.astype(o_ref.dtype)

    a_arr, a_spec = _row_in(a, tm)
    a_spec = pl.BlockSpec(a_spec.block_shape, lambda j, i, f=a_spec.index_map: f(i))
    b_spec = pl.BlockSpec((tn, k), lambda j, i: (j, 0)) if transpose_b else pl.BlockSpec((k, tn), lambda j, i: (0, j))
    tile = pl.BlockSpec((tm, tn), lambda j, i: (i, j))
    ex_specs = []
    for x in tile_ins:
        base = x.off // tn if isinstance(x, Cols) else 0
        assert not isinstance(x, Cols) or (x.off % tn == 0 and x.width == n)
        ex_specs.append(pl.BlockSpec((tm, tn), lambda j, i, base=base: (i, base + j)))
    ex_specs += [pl.BlockSpec((None, 1, tn), lambda j, i: (jnp.where(i >= nb_ctx, 1, 0), 0, j))] * len(typed_ins)
    ex_specs += [pl.BlockSpec((1, tn), lambda j, i: (0, j))] * len(shared_ins)
    res = pl.pallas_call(
        body, name=name, grid=(n // tn, m // tm), in_specs=[a_spec, b_spec] + ex_specs,
        out_specs=[tile] * len(out_dtypes),
        out_shape=[jax.ShapeDtypeStruct((m, n), d) for d in out_dtypes],
        compiler_params=_cp(("parallel", "parallel")))(a_arr, b, *tile_arrs, *typed_ins, *shared_ins)
    return res[0] if epilogue is None else res


def mm_tn(name, a, b, out_dtype=F32, pieces=None):
    t = (a.arr if isinstance(a, Cols) else a).shape[0]
    k, n = _width(a), _width(b)
    tt = _pick(t, (MM_TN_ROWS, MM_ROWS, 256))
    k_unit = pieces[1] if pieces and pieces[0] == "rows" else k
    n_unit = pieces[1] if pieces and pieces[0] == "cols" else n
    tk = _pick(k_unit, (1024, 512, 256, 128))
    tn = _pick(n_unit, (1280, 1024, 512, 256, 128))
    n_t = t // tt

    def body(a_ref, b_ref, o_ref, acc):
        part = lax.dot_general(a_ref[...], b_ref[...], _TN, preferred_element_type=F32)
        ti = pl.program_id(2)

        @pl.when(ti == 0)
        def _():
            acc[...] = part

        @pl.when(ti != 0)
        def _():
            acc[...] += part

        @pl.when(ti == n_t - 1)
        def _():
            o_ref[...] = acc[...].astype(o_ref.dtype)

    def win(item, width):
        if isinstance(item, Cols):
            assert item.off % width == 0
            return item.arr, item.off // width
        return item, 0

    a_arr, a0 = win(a, tk)
    b_arr, b0 = win(b, tn)
    if pieces is None:
        out_spec = pl.BlockSpec((tk, tn), lambda ki, ni, ti: (ki, ni))
        out_shape = (k, n)
    elif pieces[0] == "cols":
        per = n_unit // tn
        out_spec = pl.BlockSpec((None, tk, tn), lambda ki, ni, ti: (ni // per, ki, ni % per))
        out_shape = (4, k, n_unit)
    else:
        per = k_unit // tk
        out_spec = pl.BlockSpec((None, tk, tn), lambda ki, ni, ti: (ki // per, ki % per, ni))
        out_shape = (4, k_unit, n)
    return pl.pallas_call(
        body, name=name, grid=(k // tk, n // tn, n_t),
        in_specs=[pl.BlockSpec((tt, tk), lambda ki, ni, ti: (ti, a0 + ki)),
                  pl.BlockSpec((tt, tn), lambda ki, ni, ti: (ti, b0 + ni))],
        out_specs=out_spec, out_shape=jax.ShapeDtypeStruct(out_shape, out_dtype),
        scratch_shapes=[pltpu.VMEM((tk, tn), F32)],
        compiler_params=_cp(("parallel", "parallel", "arbitrary")))(a_arr, b_arr)


def _heads_t(rows_blk):
    blk = rows_blk.astype(F32).T
    return jnp.concatenate([blk[hh * ATTN_HD:(hh + 1) * ATTN_HD, :] for hh in range(4)], axis=1)


def _heads_rows(t_blk):
    tq = t_blk.shape[1] // 4
    return jnp.concatenate([t_blk[:, hh * tq:(hh + 1) * tq] for hh in range(4)], axis=0).T


def attn_fwd(name, q, kk, vT_ones, m_ctx, tq):
    t, hd, hd_ext = kk.shape[1], ATTN_HD, vT_ones.shape[1]
    nq, r = t // tq, 4 * tq
    tk = _pick(t - m_ctx, (ATTN_TK_FWD, ATTN_TK))
    nqc, n_lat_tiles = m_ctx // tq, (t - m_ctx) // tk

    def body(q_ref, k_ref, vT_ref, o_ref, qT_ref, oT_ref, lse_ref):
        i = pl.program_id(1)
        q_t = _heads_t(q_ref[...]).astype(CDT)
        qT_ref[...] = q_t

        def tile(off, size, carry):
            mi, acc = carry
            sub = min(size, ATTN_TK)
            offs = [off + u * sub for u in range(size // sub)]
            sts = [jnp.dot(k_ref[pl.ds(o, sub), :], q_t, preferred_element_type=F32) for o in offs]
            for o, st in zip(offs, sts):
                mn = jnp.maximum(mi, jnp.max(st, axis=0, keepdims=True))
                pt = jnp.exp2(st - mn)
                acc = jnp.exp2(mi - mn) * acc + jnp.dot(vT_ref[:, pl.ds(o, sub)], pt.astype(CDT),
                                                        preferred_element_type=F32)
                mi = mn
            return mi, acc

        carry = tile(0, m_ctx, (jnp.full((1, r), -1e30, F32), jnp.zeros((hd_ext, r), F32)))
        mi, acc = lax.fori_loop(
            0, jnp.where(i < nqc, 0, n_lat_tiles),
            lambda j, cr: tile(pl.multiple_of(m_ctx + j * tk, ATTN_TK), tk, cr), carry)
        li = acc[hd:hd + 1]
        o_t = acc[:hd] / li
        oT_ref[...] = o_t.astype(oT_ref.dtype)
        o_ref[...] = _heads_rows(o_t).astype(o_ref.dtype)
        lse_ref[...] = mi + jnp.log2(li)

    blk_t = pl.BlockSpec((None, None, hd, r), lambda g, i: (g, i, 0, 0))
    rows = pl.BlockSpec((tq, 4 * hd), lambda g, i: (i, g))
    return pl.pallas_call(
        body, name=name, grid=(2, nq),
        in_specs=[rows, pl.BlockSpec((None, t, hd), lambda g, i: (g, 0, 0)),
                  pl.BlockSpec((None, hd_ext, t), lambda g, i: (g, 0, 0))],
        out_specs=[rows, blk_t, blk_t, pl.BlockSpec((None, None, 1, r), lambda g, i: (g, i, 0, 0))],
        out_shape=[jax.ShapeDtypeStruct((t, 8 * hd), CDT), jax.ShapeDtypeStruct((2, nq, hd, r), CDT),
                   jax.ShapeDtypeStruct((2, nq, hd, r), CDT), jax.ShapeDtypeStruct((2, nq, 1, r), F32)],
        compiler_params=_cp(("parallel", "arbitrary")))(q, kk, vT_ones)


def attn_bwd(name, qT, do, oT, lse, kk, kT, vv, m_ctx):
    _, nq, hd, r = qT.shape
    t = kk.shape[1]
    tq = r // 4
    tk = _pick(t - m_ctx, (ATTN_TK_BWD, ATTN_TK))
    nqc, n_lat_tiles = m_ctx // tq, (t - m_ctx) // tk

    def body(qT_ref, do_ref, oT_ref, lse_ref, k_ref, kT_ref, v_ref, dq_ref, dk_ref, dv_ref):
        i = pl.program_id(1)

        @pl.when(i == 0)
        def _():
            dk_ref[...] = jnp.zeros_like(dk_ref)
            dv_ref[...] = jnp.zeros_like(dv_ref)

        q_t = qT_ref[...]
        do_f = _heads_t(do_ref[...])
        do_t = do_f.astype(CDT)
        lse = lse_ref[...]
        delta = jnp.sum(do_f * oT_ref[...].astype(F32), axis=0, keepdims=True)

        def tile(off, size, dq):
            sub = min(size, ATTN_TK)
            offs = [off + u * sub for u in range(size // sub)]
            sts = [jnp.dot(k_ref[pl.ds(o, sub), :], q_t, preferred_element_type=F32) for o in offs]
            dpts = [jnp.dot(v_ref[pl.ds(o, sub), :], do_t, preferred_element_type=F32) for o in offs]
            for o, st, dpt in zip(offs, sts, dpts):
                pt = jnp.exp2(st - lse)
                dv_ref[pl.ds(o, sub), :] += lax.dot_general(pt.astype(CDT), do_t, _NT, preferred_element_type=F32)
                dst = (pt * (dpt - delta)).astype(CDT)
                dk_ref[pl.ds(o, sub), :] += lax.dot_general(dst, q_t, _NT, preferred_element_type=F32)
                dq = dq + jnp.dot(kT_ref[:, pl.ds(o, sub)], dst, preferred_element_type=F32)
            return dq

        dq = tile(0, m_ctx, jnp.zeros((hd, r), F32))
        dq = lax.fori_loop(0, jnp.where(i < nqc, 0, n_lat_tiles),
                           lambda j, acc: tile(pl.multiple_of(m_ctx + j * tk, ATTN_TK), tk, acc), dq)
        dq_ref[...] = _heads_rows(dq * LN2)

    blk_t = pl.BlockSpec((None, None, hd, r), lambda g, i: (g, i, 0, 0))
    row = pl.BlockSpec((None, None, 1, r), lambda g, i: (g, i, 0, 0))
    kv = pl.BlockSpec((None, t, hd), lambda g, i: (g, 0, 0))
    rows = pl.BlockSpec((tq, 4 * hd), lambda g, i: (i, g))
    return pl.pallas_call(
        body, name=name, grid=(2, nq),
        in_specs=[blk_t, rows, blk_t, row, kv, pl.BlockSpec((None, hd, t), lambda g, i: (g, 0, 0)), kv],
        out_specs=[rows, kv, kv],
        out_shape=[jax.ShapeDtypeStruct((t, 8 * hd), F32), jax.ShapeDtypeStruct(kk.shape, F32),
                   jax.ShapeDtypeStruct(kk.shape, F32)],
        compiler_params=_cp(("parallel", "arbitrary")))(qT, do, oT, lse, kk, kT, vv)


def _split_kv(a):
    return a.reshape(a.shape[0], 2, ATTN_HD).transpose(1, 0, 2)


def _merge_kv(a):
    return a.transpose(1, 0, 2).reshape(a.shape[1], 2 * ATTN_HD)


def _chunk_order(rev, ncc, nct):
    if not rev:
        return lambda s: s
    return lambda s: jnp.where(s < ncc, ncc - 1 - s, nct - 1 - (s - ncc))


def scan_fwd(name, make_fn, rows, shared, n_state, y_width, n_rows, m_ctx):
    nct, ncc = n_rows // CHUNK, m_ctx // CHUNK
    orders = [_chunk_order(rev, ncc, nct) for rev in (False, True)]
    fns = [make_fn(0), make_fn(1)]
    nr, ns = len(rows), len(shared)

    def body(*refs):
        svals = [r[...] for r in refs[2 * nr:2 * nr + ns]]
        y_refs, sin_refs, st = refs[2 * nr + ns:2 * nr + ns + 2], refs[2 * nr + ns + 2:2 * nr + ns + 4], refs[-1]

        @pl.when(pl.program_id(0) == 0)
        def _():
            st[...] = jnp.zeros_like(st)

        for d in range(2):
            rvals = [r[...] for r in refs[d * nr:(d + 1) * nr]]
            prev = [st[d, k] for k in range(n_state)]
            sin_refs[d][...] = st[d]
            y, new = fns[d](rvals, svals, prev)
            y_refs[d][...] = y
            for k in range(n_state):
                st[d, k] = new[k]

    arrs, specs = [], []
    for order in orders:
        for it in rows:
            a, s = _row_in(it, CHUNK, order)
            arrs.append(a)
            specs.append(s)
    for s_ in shared:
        arrs.append(s_)
        specs.append(_const_spec(s_.shape))
    return pl.pallas_call(
        body, name=name, grid=(nct,), in_specs=specs,
        out_specs=[pl.BlockSpec((CHUNK, y_width), lambda s, o=o: (o(s), 0)) for o in orders]
        + [pl.BlockSpec((None, n_state, LANES, LANES), lambda s, o=o: (o(s), 0, 0, 0)) for o in orders],
        out_shape=[jax.ShapeDtypeStruct((n_rows, y_width), F32)] * 2
        + [jax.ShapeDtypeStruct((nct, n_state, LANES, LANES), F32)] * 2,
        scratch_shapes=[pltpu.VMEM((2, n_state, LANES, LANES), F32)],
        compiler_params=_cp(("arbitrary",)))(*arrs)


def scan_bwd(name, make_fn, rows, shared, states_in, dy, post, outs, n_state, n_rows, m_ctx, dirs=(0, 1)):
    nct, ncc = n_rows // CHUNK, m_ctx // CHUNK
    orders = [(lambda r, f=_chunk_order(d == 1, ncc, nct): f(nct - 1 - r)) for d in dirs]
    fns = [make_fn(d) for d in dirs]
    nd = len(dirs)
    nr, ns, no = len(rows), len(shared), len(outs)
    n_in = nd * nr + ns

    def body(*refs):
        svals = [r[...] for r in refs[nd * nr:n_in]]
        sin_refs, dy_refs = refs[n_in:n_in + nd], refs[n_in + nd:n_in + 2 * nd]
        out_refs = refs[n_in + 2 * nd:n_in + 2 * nd + nd * no]
        dsh_refs = refs[n_in + 2 * nd + nd * no:-1]
        dst = refs[-1]
        r = pl.program_id(0)

        @pl.when(r == 0)
        def _():
            dst[...] = jnp.zeros_like(dst)

        d_shared = None
        for d in range(nd):
            rvals = [x[...] for x in refs[d * nr:(d + 1) * nr]]
            prev = [sin_refs[d][k] for k in range(n_state)]
            _, vjp = jax.vjp(fns[d], rvals, svals, prev)
            d_rows, d_sh, d_prev = vjp((dy_refs[d][...], [dst[d, k] for k in range(n_state)]))
            for ref, val in zip(out_refs[d * no:(d + 1) * no], post(d_rows)):
                ref[...] = val.astype(ref.dtype)
            d_shared = d_sh if d_shared is None else [a + b for a, b in zip(d_shared, d_sh)]
            for k in range(n_state):
                dst[d, k] = d_prev[k]
        for ref, gr in zip(dsh_refs, d_shared):
            @pl.when(r == 0)
            def _(ref=ref, gr=gr):
                ref[...] = gr

            @pl.when(r != 0)
            def _(ref=ref, gr=gr):
                ref[...] += gr

    arrs, specs = [], []
    for order in orders:
        for it in rows:
            a, s = _row_in(it, CHUNK, order)
            arrs.append(a)
            specs.append(s)
    for s_ in shared:
        arrs.append(s_)
        specs.append(_const_spec(s_.shape))
    for sin, order in zip(states_in, orders):
        arrs.append(sin)
        specs.append(pl.BlockSpec((None, n_state, LANES, LANES), lambda r, o=order: (o(r), 0, 0, 0)))
    for order in orders:
        a, s = _row_in(dy, CHUNK, order)
        arrs.append(a)
        specs.append(s)
    out_specs = [pl.BlockSpec((CHUNK, w), lambda r, o=o: (o(r), 0)) for o in orders for w, _ in outs]
    out_shape = [jax.ShapeDtypeStruct((n_rows, w), dt) for _ in orders for w, dt in outs]
    for s_ in shared:
        out_specs.append(_const_spec(s_.shape))
        out_shape.append(jax.ShapeDtypeStruct(s_.shape, F32))
    res = pl.pallas_call(body, name=name, grid=(nct,), in_specs=specs, out_specs=out_specs, out_shape=out_shape,
                         scratch_shapes=[pltpu.VMEM((nd, n_state, LANES, LANES), F32)],
                         compiler_params=_cp(("arbitrary",)))(*arrs)
    return [list(res[d * no:(d + 1) * no]) for d in range(nd)] + [list(res[nd * no:])]


def _make_ssd_chunk(direction):
    rev = direction == 1
    base = 8 * direction

    def fn(rows, shared, prev):
        xs, bms, cms, dtraw = rows[0:4], rows[4:6], rows[6:8], rows[8]
        dt_bias, a_log = shared
        ln = dtraw.shape[0]
        dt_all = _softplus(dtraw + dt_bias)
        a_all = dt_all * (-jnp.exp(a_log))
        r_i = lax.broadcasted_iota(jnp.int32, (ln, ln), 0)
        c_i = lax.broadcasted_iota(jnp.int32, (ln, ln), 1)
        tri = (r_i <= c_i) if rev else (r_i >= c_i)
        a_cum_all = jnp.dot(tri.astype(F32), a_all, precision=lax.Precision.HIGHEST, preferred_element_type=F32)
        a_tot_all = jnp.sum(a_all, axis=0, keepdims=True)
        first = lax.broadcasted_iota(jnp.int32, (ln, LANES), 1) < SSD_HD
        first_row = lax.broadcasted_iota(jnp.int32, (LANES, 1), 0) < SSD_HD

        def lmat(acol):
            a_b = jnp.broadcast_to(acol, (ln, ln))
            seg = a_b - a_b.T
            return jnp.where(tri, jnp.exp(jnp.where(tri, seg, 0.0)), 0.0)

        ys, new = [], []
        for g in range(2):
            bm, cm = bms[g], cms[g]
            cb = _mxu(cm, bm, _NT)
            for jj in range(2):
                pr = 2 * g + jj
                h0, h1 = base + 2 * pr, base + 2 * pr + 1
                ac0, ac1 = _col(a_cum_all, h0), _col(a_cum_all, h1)
                at0, at1 = _col(a_tot_all, h0), _col(a_tot_all, h1)
                dt_pair = jnp.where(first, _col(dt_all, h0), _col(dt_all, h1))
                acum_pair = jnp.where(first, ac0, ac1)
                atot_pair = jnp.where(first[0:1], at0, at1)
                xd = xs[pr] * dt_pair
                st = _mxu(xd * jnp.exp(atot_pair - acum_pair), bm, _TN)
                new.append(prev[pr] * jnp.where(first_row, jnp.exp(at0), jnp.exp(at1)) + st)
                y0 = _mxu(cb * lmat(ac0), xd)
                y1 = _mxu(cb * lmat(ac1), xd)
                y_off = _mxu(cm, prev[pr], _NT) * jnp.exp(acum_pair)
                ys.append(jnp.where(first, y0, y1) + y_off)
        return jnp.concatenate(ys, axis=1), new

    return fn


def _make_ret_chunk(direction):
    rev = direction == 1
    base = 4 * direction

    def fn(rows, shared, prev):
        qs, ks, vs = rows[0:4], rows[4:8], rows[8:12]
        lg_all = -jnp.exp(shared[0])
        ln = qs[0].shape[0]
        pos = lax.broadcasted_iota(jnp.int32, (ln, 1), 0).astype(F32)
        r_i = lax.broadcasted_iota(jnp.int32, (ln, ln), 0)
        c_i = lax.broadcasted_iota(jnp.int32, (ln, ln), 1)
        diff = ((c_i - r_i) if rev else (r_i - c_i))
        mask = diff >= 0
        dpos = jnp.maximum(diff, 0).astype(F32)
        k_pow = pos if rev else (ln - 1.0 - pos)
        q_pow = (ln - pos) if rev else (pos + 1.0)
        ys, new = [], []
        for h in range(RET_HEADS):
            lg = _col(lg_all, base + h)
            dmat = jnp.where(mask, jnp.exp(dpos * lg), 0.0)
            st = _mxu(ks[h] * jnp.exp(k_pow * lg), vs[h], _TN)
            new.append(prev[h] * jnp.exp(ln * lg) + st)
            s = _mxu(qs[h], ks[h], _NT) * dmat
            ys.append(_mxu(s, vs[h]) + _mxu(qs[h], prev[h]) * jnp.exp(q_pow * lg))
        return jnp.concatenate(ys, axis=1), new

    return fn


def _conv_pre(x, w, b, t_idx, n_rows, m_ctx):
    is_start = (t_idx == 0) | (t_idx == m_ctx)
    is_end = (t_idx == m_ctx - 1) | (t_idx == n_rows - 1)
    xp = jnp.where(is_start, 0.0, pltpu.roll(x, 1, axis=0))
    xn = jnp.where(is_end, 0.0, pltpu.roll(x, n_rows - 1, axis=0))
    return w[0:1] * xp + w[1:2] * x + w[2:3] * xn + b, xp, xn, is_start, is_end


def conv_fwd(x, conv_w, conv_b, m_ctx):
    n_rows, width = x.arr.shape[0], x.width
    c0 = x.off // LANES

    def body(x_ref, w_ref, b_ref, o_ref):
        t_idx = lax.broadcasted_iota(jnp.int32, (n_rows, 1), 0)
        pre = _conv_pre(x_ref[...], w_ref[...], b_ref[...], t_idx, n_rows, m_ctx)[0]
        o_ref[...] = pre * jax.nn.sigmoid(pre)

    return pl.pallas_call(
        body, name="conv_fwd", grid=(width // LANES,),
        in_specs=[pl.BlockSpec((n_rows, LANES), lambda c: (0, c0 + c)),
                  pl.BlockSpec((3, LANES), lambda c: (0, c)), pl.BlockSpec((1, LANES), lambda c: (0, c))],
        out_specs=pl.BlockSpec((n_rows, LANES), lambda c: (0, c)),
        out_shape=jax.ShapeDtypeStruct((n_rows, width), F32),
        compiler_params=_cp(("parallel",)))(x.arr, conv_w, conv_b)


def conv_bwd(x, conv_w, conv_b, dy_a, dy_b, dxs_extra, m_ctx):
    n_rows, width = x.arr.shape[0], x.width
    c0 = x.off // LANES
    n_extra = dxs_extra.shape[1] // LANES

    def body(x_ref, w_ref, b_ref, dya_ref, dyb_ref, ex_ref, dx_ref, dw_ref, db_ref):
        c = pl.program_id(0)
        t_idx = lax.broadcasted_iota(jnp.int32, (n_rows, 1), 0)
        w = w_ref[...]
        pre, xp, xn, is_start, is_end = _conv_pre(x_ref[...], w, b_ref[...], t_idx, n_rows, m_ctx)
        sg = jax.nn.sigmoid(pre)
        dyv = dya_ref[...] + dyb_ref[...] + jnp.where(c < n_extra, ex_ref[...], 0.0)
        dpre = dyv * (sg * (1.0 + pre * (1.0 - sg)))
        d_next = jnp.where(is_end, 0.0, pltpu.roll(dpre, n_rows - 1, axis=0))
        d_prev = jnp.where(is_start, 0.0, pltpu.roll(dpre, 1, axis=0))
        dx_ref[...] = (w[1:2] * dpre + w[0:1] * d_next + w[2:3] * d_prev).astype(dx_ref.dtype)
        dw_ref[...] = jnp.concatenate([jnp.sum(dpre * xp, axis=0, keepdims=True),
                                       jnp.sum(dpre * x_ref[...], axis=0, keepdims=True),
                                       jnp.sum(dpre * xn, axis=0, keepdims=True)], axis=0)
        db_ref[...] = jnp.sum(dpre, axis=0, keepdims=True)

    return pl.pallas_call(
        body, name="conv_bwd", grid=(width // LANES,),
        in_specs=[pl.BlockSpec((n_rows, LANES), lambda c: (0, c0 + c)),
                  pl.BlockSpec((3, LANES), lambda c: (0, c)), pl.BlockSpec((1, LANES), lambda c: (0, c)),
                  pl.BlockSpec((n_rows, LANES), lambda c: (0, c)), pl.BlockSpec((n_rows, LANES), lambda c: (0, c)),
                  pl.BlockSpec((n_rows, LANES), lambda c: (0, jnp.minimum(c, n_extra - 1)))],
        out_specs=[pl.BlockSpec((n_rows, LANES), lambda c: (0, c)),
                   pl.BlockSpec((3, LANES), lambda c: (0, c)), pl.BlockSpec((1, LANES), lambda c: (0, c))],
        out_shape=[jax.ShapeDtypeStruct((n_rows, width), CDT), jax.ShapeDtypeStruct((3, width), F32),
                   jax.ShapeDtypeStruct((1, width), F32)],
        compiler_params=_cp(("parallel",)))(x.arr, conv_w, conv_b, dy_a, dy_b, dxs_extra)


def loss_head(h, target, final_w, m_ctx):
    n_rows, d = h.shape
    tm = min(ROW_TILE, n_rows)
    nb_ctx = m_ctx // tm

    def f(hb, w, tgt):
        err = _rms(hb, w) - tgt
        return 0.5 * jnp.sum(jnp.mean(err * err, axis=-1))

    def body(h_ref, t_ref, w_ref, loss_ref, dh_ref, dw_ref):
        i = pl.program_id(0)

        @pl.when(i < nb_ctx)
        def _():
            dh_ref[...] = jnp.zeros_like(dh_ref)

        @pl.when(i == 0)
        def _():
            loss_ref[...] = jnp.zeros_like(loss_ref)
            dw_ref[...] = jnp.zeros_like(dw_ref)

        @pl.when(i >= nb_ctx)
        def _():
            val, vjp = jax.vjp(lambda hb, w: f(hb, w, t_ref[...]), h_ref[...], w_ref[...])
            dh, dw = vjp(jnp.ones((), F32))
            dh_ref[...] = dh
            dw_ref[...] += dw
            loss_ref[...] += jnp.broadcast_to(val, loss_ref.shape)

    return pl.pallas_call(
        body, name="loss_head", grid=(n_rows // tm,),
        in_specs=[pl.BlockSpec((tm, d), lambda i: (i, 0)),
                  pl.BlockSpec((tm, d), lambda i: (jnp.maximum(i - nb_ctx, 0), 0)), _const_spec((1, d))],
        out_specs=[_const_spec((1, LANES)), pl.BlockSpec((tm, d), lambda i: (i, 0)), _const_spec((1, d))],
        out_shape=[jax.ShapeDtypeStruct((1, LANES), F32), jax.ShapeDtypeStruct((n_rows, d), F32),
                   jax.ShapeDtypeStruct((1, d), F32)],
        compiler_params=_cp(("arbitrary",)))(h, target, final_w)


def adamw(name, w, m, v, g_parts):
    lead, rows, cols = w.shape
    tr = _pick(rows, (256, 128, 64, 32, 16, 8))
    npart = len(g_parts)
    c1 = 1.0 - ADAM_B1 ** ADAM_STEP
    c2 = 1.0 - ADAM_B2 ** ADAM_STEP

    def body(*refs):
        w_ref, m_ref, v_ref = refs[:3]
        g = refs[3][...].astype(F32)
        for r in refs[4:3 + npart]:
            g = g + r[...].astype(F32)
        g_ref, d_ref, nm_ref, nv_ref = refs[3 + npart:]
        nm = ADAM_B1 * m_ref[...] + (1.0 - ADAM_B1) * g
        nv = ADAM_B2 * v_ref[...] + (1.0 - ADAM_B2) * (g * g)
        g_ref[...] = g
        nm_ref[...] = nm
        nv_ref[...] = nv
        d_ref[...] = -ADAM_LR * ((nm / c1) / (jnp.sqrt(nv / c2) + ADAM_EPS) + ADAM_WD * w_ref[...])

    spec = pl.BlockSpec((None, tr, cols), lambda l, i: (l, i, 0))
    return pl.pallas_call(
        body, name=name, grid=(lead, rows // tr), in_specs=[spec] * (3 + npart), out_specs=[spec] * 4,
        out_shape=[jax.ShapeDtypeStruct(w.shape, F32)] * 4,
        compiler_params=_cp(("parallel", "parallel")))(w, m, v, *g_parts)


MESH = pl.DeviceIdType.MESH
_HBM = pl.BlockSpec(memory_space=pl.ANY)


def _chip_peers():
    x, y, c = lax.axis_index("x"), lax.axis_index("y"), lax.axis_index("c")
    return x, y, c, [(1 - x, y), (x, 1 - y), (1 - x, 1 - y)]


def _window(ref, kind, chip, rows, cols):
    if kind == "cols":
        return ref.at[:, pl.ds(pl.multiple_of(chip * cols, LANES), cols)]
    if kind == "rows":
        return ref.at[pl.ds(pl.multiple_of(chip * rows, 8), rows), :]
    return ref.at[chip]


def _gathered_shape(kind, rows, cols):
    return {"cols": (rows, 4 * cols), "rows": (4 * rows, cols), "slices": (4, rows, cols)}[kind]


def gather_layers(name, shards, kinds):
    n = len(shards)

    def body(*refs):
        x_refs, o_refs = refs[:n], refs[n:2 * n]
        send_sems, recv_sems, local_sems = refs[2 * n:]
        x, y, c, peers = _chip_peers()
        me = 2 * x + y
        started = []
        for a in range(n):
            _, rows, cols = shards[a].shape
            src = x_refs[a].at[c]
            mine = pltpu.make_async_copy(src, _window(o_refs[a], kinds[a], me, rows, cols), local_sems.at[a])
            mine.start()
            started.append(mine.wait)
            for k, (px, py) in enumerate(peers):
                cp = pltpu.make_async_remote_copy(
                    src_ref=src, dst_ref=_window(o_refs[a], kinds[a], me, rows, cols), send_sem=send_sems.at[3 * a + k],
                    recv_sem=recv_sems.at[3 * a + k], device_id=(px, py, c), device_id_type=MESH)
                cp.start()
                started.append(cp.wait_send)
        for a in range(n):
            _, rows, cols = shards[a].shape
            for k, (px, py) in enumerate(peers):
                pltpu.make_async_remote_copy(
                    src_ref=x_refs[a].at[c], dst_ref=_window(o_refs[a], kinds[a], 2 * px + py, rows, cols),
                    send_sem=send_sems.at[3 * a + k], recv_sem=recv_sems.at[3 * a + k], device_id=(px, py, c),
                    device_id_type=MESH).wait_recv()
        for wait in started:
            wait()

    return pl.pallas_call(
        body, name=name, in_specs=[_HBM] * n, out_specs=[_HBM] * n,
        out_shape=[jax.ShapeDtypeStruct(_gathered_shape(kinds[a], *shards[a].shape[1:]), shards[a].dtype)
                   for a in range(n)],
        scratch_shapes=[pltpu.SemaphoreType.DMA((3 * n,)), pltpu.SemaphoreType.DMA((3 * n,)),
                        pltpu.SemaphoreType.DMA((n,))],
        )(*shards)


def scatter_pieces(name, pieces):
    n = len(pieces)

    def body(*refs):
        p_refs, o_refs = refs[:n], refs[n:2 * n]
        send_sems, recv_sems, local_sems = refs[2 * n:]
        x, y, c, peers = _chip_peers()
        me = 2 * x + y
        started = []
        for a in range(n):
            mine = pltpu.make_async_copy(p_refs[a].at[me], o_refs[a].at[me], local_sems.at[a])
            mine.start()
            started.append(mine.wait)
            for k, (px, py) in enumerate(peers):
                cp = pltpu.make_async_remote_copy(
                    src_ref=p_refs[a].at[2 * px + py], dst_ref=o_refs[a].at[me], send_sem=send_sems.at[3 * a + k],
                    recv_sem=recv_sems.at[3 * a + k], device_id=(px, py, c), device_id_type=MESH)
                cp.start()
                started.append(cp.wait_send)
        for a in range(n):
            for k, (px, py) in enumerate(peers):
                pltpu.make_async_remote_copy(
                    src_ref=p_refs[a].at[me], dst_ref=o_refs[a].at[2 * px + py], send_sem=send_sems.at[3 * a + k],
                    recv_sem=recv_sems.at[3 * a + k], device_id=(px, py, c), device_id_type=MESH).wait_recv()
        for wait in started:
            wait()

    return pl.pallas_call(
        body, name=name, in_specs=[_HBM] * n, out_specs=[_HBM] * n,
        out_shape=[jax.ShapeDtypeStruct(p.shape, p.dtype) for p in pieces],
        scratch_shapes=[pltpu.SemaphoreType.DMA((3 * n,)), pltpu.SemaphoreType.DMA((3 * n,)),
                        pltpu.SemaphoreType.DMA((n,))],
        )(*pieces)


def _pair_step(n_steps, x_ref, land, send_sems, recv_sems, credits, consume):
    x, y, c = lax.axis_index("x"), lax.axis_index("y"), lax.axis_index("c")
    sib = (x, y, 1 - c)
    i = pl.program_id(0)
    slot = i % 2

    @pl.when(i >= 2)
    def _():
        pl.semaphore_wait(credits.at[slot], 1)

    cp = pltpu.make_async_remote_copy(src_ref=x_ref, dst_ref=land.at[slot], send_sem=send_sems.at[slot],
                                      recv_sem=recv_sems.at[slot], device_id=sib, device_id_type=MESH)
    cp.start()
    cp.wait_recv()
    consume(land[slot])

    @pl.when(i < n_steps - 2)
    def _():
        pl.semaphore_signal(credits.at[slot], inc=1, device_id=sib, device_id_type=MESH)

    cp.wait_send()


def _pair_call(name, body, n_steps, in_specs, out_spec, out_shape, blk_shape, dtype, operands, extra_scratch=()):
    grid_spec = pltpu.PrefetchScalarGridSpec(
        num_scalar_prefetch=1, grid=(n_steps,), in_specs=in_specs, out_specs=out_spec,
        scratch_shapes=[pltpu.VMEM((2,) + blk_shape, dtype), pltpu.SemaphoreType.DMA((2,)),
                        pltpu.SemaphoreType.DMA((2,)), pltpu.SemaphoreType.REGULAR((2,)), *extra_scratch])
    return pl.pallas_call(body, name=name, grid_spec=grid_spec, out_shape=out_shape,
                          compiler_params=_cp(("arbitrary",)))(*operands)


def _place():
    return jnp.stack([lax.axis_index("x"), lax.axis_index("y"), lax.axis_index("c")]).astype(jnp.int32)


def _pair_rows(rows, row_bytes):
    for cand in (4096, 2048, 1024, 768, 512, 384, 256, 192, 128, 96, 64, 48, 32, 16):
        if rows % cand == 0 and cand * row_bytes <= PAIR_BLOCK_BYTES:
            return cand
    return _pick(rows, (16, 8))


def exchange_both(name, mine):
    rows, cols = mine.shape
    tr = _pair_rows(rows, cols * mine.dtype.itemsize)
    n_steps = rows // tr

    def body(s_ref, x_ref, o_ref, land, send_sems, recv_sems, credits):
        c = lax.axis_index("c")
        o_ref[c] = x_ref[...]

        def consume(v):
            o_ref[1 - c] = v
        _pair_step(n_steps, x_ref, land, send_sems, recv_sems, credits, consume)

    return _pair_call(name, body, n_steps, [pl.BlockSpec((tr, cols), lambda i, s: (i, 0))],
                      pl.BlockSpec((2, tr, cols), lambda i, s: (0, i, 0)),
                      jax.ShapeDtypeStruct((2, rows, cols), mine.dtype), (tr, cols), mine.dtype, (_place(), mine))


def exchange_add(name, layer0, layer1):
    rows, cols = layer0.shape
    tr = _pair_rows(rows, cols * layer0.dtype.itemsize)
    nb = rows // tr

    def body(s_ref, l0_ref, l1_ref, o_ref, land, send_sems, recv_sems, credits, send_buf):
        first = lax.axis_index("c") == 0
        send_buf[...] = jnp.where(first, l1_ref[...], l0_ref[...])
        mine = jnp.where(first, l0_ref[...], l1_ref[...]).astype(F32)

        def consume(v):
            o_ref[...] = (mine + v.astype(F32)).astype(o_ref.dtype)
        _pair_step(nb, send_buf, land, send_sems, recv_sems, credits, consume)

    spec = pl.BlockSpec((tr, cols), lambda i, s: (i, 0))
    return _pair_call(name, body, nb, [spec, spec], spec, jax.ShapeDtypeStruct((rows, cols), CDT),
                      (tr, cols), layer0.dtype, (_place(), layer0, layer1),
                      extra_scratch=(pltpu.VMEM((tr, cols), layer0.dtype),))


def sum_exchange(name, parts):
    npart, rows, cols = parts.shape
    tr = _pair_rows(rows, cols * 4)
    n_steps = rows // tr

    def body(s_ref, x_ref, o_ref, land, send_sems, recv_sems, credits, mine):
        c = lax.axis_index("c")
        acc = x_ref[0].astype(F32)
        for k in range(1, npart):
            acc = acc + x_ref[k].astype(F32)
        mine[...] = acc
        o_ref[c] = acc

        def consume(v):
            o_ref[1 - c] = v
        _pair_step(n_steps, mine, land, send_sems, recv_sems, credits, consume)

    return _pair_call(name, body, n_steps, [pl.BlockSpec((npart, tr, cols), lambda i, s: (0, i, 0))],
                      pl.BlockSpec((2, tr, cols), lambda i, s: (0, i, 0)),
                      jax.ShapeDtypeStruct((2, rows, cols), F32), (tr, cols), F32, (_place(), parts),
                      extra_scratch=(pltpu.VMEM((tr, cols), F32),))


def allreduce_small(name, buf):
    rows = buf.shape[0]

    def body(x_ref, out_ref, gath, send_sems, recv_sems):
        x, y, c = lax.axis_index("x"), lax.axis_index("y"), lax.axis_index("c")
        me = 4 * x + 2 * y + c
        masks = [(k >> 2 & 1, k >> 1 & 1, k & 1) for k in range(1, 8)]

        def flip(v, bit):
            return 1 - v if bit else v

        sends = []
        for k, (bx, by, bc) in enumerate(masks):
            cp = pltpu.make_async_remote_copy(src_ref=x_ref, dst_ref=gath.at[me], send_sem=send_sems.at[k],
                                              recv_sem=recv_sems.at[k],
                                              device_id=(flip(x, bx), flip(y, by), flip(c, bc)), device_id_type=MESH)
            cp.start()
            sends.append(cp)
        gath[me] = x_ref[...]
        for k, (bx, by, bc) in enumerate(masks):
            px, py, pc = flip(x, bx), flip(y, by), flip(c, bc)
            pltpu.make_async_remote_copy(src_ref=x_ref, dst_ref=gath.at[4 * px + 2 * py + pc],
                                         send_sem=send_sems.at[k], recv_sem=recv_sems.at[k],
                                         device_id=(px, py, pc), device_id_type=MESH).wait_recv()
        for cp in sends:
            cp.wait_send()
        acc = gath[0]
        for d in range(1, 8):
            acc = acc + gath[d]
        out_ref[...] = acc

    return pl.pallas_call(
        body, name=name, in_specs=[pl.BlockSpec(memory_space=pltpu.VMEM)],
        out_specs=pl.BlockSpec(memory_space=pltpu.VMEM), out_shape=jax.ShapeDtypeStruct(buf.shape, F32),
        scratch_shapes=[pltpu.VMEM((8, rows, LANES), F32), pltpu.SemaphoreType.DMA((7,)),
                        pltpu.SemaphoreType.DMA((7,))],
        )(buf)


def _pack_flat(arrs, dtype, width, row_mult=8):
    flat = jnp.concatenate([a.reshape(-1).astype(dtype) for a in arrs])
    pad = (-flat.shape[0]) % (row_mult * width)
    if pad:
        flat = jnp.concatenate([flat, jnp.zeros((pad,), dtype)])
    return flat.reshape(-1, width)


def _unpack_flat(buf, shapes):
    flat = buf.reshape(-1)
    out, off = [], 0
    for s in shapes:
        n = math.prod(s)
        out.append(flat[off:off + n].reshape(s))
        off += n
    return out


def _in_to_padded(w):
    parts = []
    for name in IN_NEW_ORDER:
        _, width, o_off, o_w = IN_LAYOUT[name]
        parts.append(w[..., o_off:o_off + o_w])
        if o_w < width:
            parts.append(jnp.zeros(w.shape[:-1] + (width - o_w,), w.dtype))
    used = sum(IN_LAYOUT[n][1] for n in IN_NEW_ORDER)
    parts.append(jnp.zeros(w.shape[:-1] + (IN_PAD - used,), w.dtype))
    return jnp.concatenate(parts, axis=-1)


def _in_from_padded(g):
    parts = []
    for name in IN_ORIG_ORDER:
        off, _, _, o_w = IN_LAYOUT[name]
        parts.append(g[..., off:off + o_w])
    return jnp.concatenate(parts, axis=-1)


def _pcol(p, name):
    off, width, _, _ = IN_LAYOUT[name]
    return Cols(p, off, width)


def _lane_pad(v, width=LANES):
    v = v.reshape(-1)
    return jnp.concatenate([v, jnp.zeros((width - v.shape[0],), v.dtype)]).reshape(1, width)


def _f_norm_mod(h, sh, sc, w):
    return (_rms(h, w) * (1.0 + sc) + sh,)


def _f_norm_mod_thru(h, sh, sc, w):
    return h, _rms(h, w) * (1.0 + sc) + sh


def _f_attn_prep(qraw, kraw, vraw, cos2, sin2, qw, kw, gq, gk):
    q = qraw * lax.rsqrt(_group_mean(qraw * qraw, gq) + NORM_EPS) * qw
    q = _rope32(q, jnp.tile(cos2, (1, 4)), jnp.tile(sin2, (1, 4))) * (ATTN_HD ** -0.5 * LOG2E)
    k = kraw * lax.rsqrt(_group_mean(kraw * kraw, gk) + NORM_EPS) * kw
    return q, _rope32(k, cos2, sin2), vraw


def _f_ssd_finish(yf, yb, xs, z, d_exp, nw):
    y = (yf + yb + d_exp * xs) * (z * jax.nn.sigmoid(z))
    return (_rms(y, nw),)


def _f_ret_prep(rq, rk, cos1, sin1):
    cos_full, sin_signed = jnp.tile(cos1, (1, 4)), jnp.tile(sin1, (1, 4))
    return _rope64(rq, cos_full, sin_signed), _rope64(rk, cos_full, sin_signed) * (RET_DK ** -0.5)


def _f_ret_finish(yf, yb, g, gw):
    y = yf + yb
    outs = []
    for h in range(RET_HEADS):
        yh = y[:, h * RET_DK:(h + 1) * RET_DK]
        yc = yh - jnp.mean(yh, axis=-1, keepdims=True)
        outs.append(yc * lax.rsqrt(jnp.mean(yc * yc, axis=-1, keepdims=True) + NORM_EPS))
    return (jnp.concatenate(outs, axis=1) * gw * (g * jax.nn.sigmoid(g)),)


def _f_merge(p0, p1, p2, g0, g1, g2):
    return (jax.nn.sigmoid(g0) * p0 + jax.nn.sigmoid(g1) * p1 + jax.nn.sigmoid(g2) * p2,)


def _f_mid(h, mix, g1, sh2, sc2, w2):
    h_mid = h + g1 * mix
    return h_mid, _rms(h_mid, w2) * (1.0 + sc2) + sh2


def _epi_merge_bwd(dmerged, p0, p1, p2, g0, g1, g2):
    dps, dgs = [], []
    for pb, gate in ((p0, g0), (p1, g1), (p2, g2)):
        s = jax.nn.sigmoid(gate)
        dps.append(dmerged * s)
        dgs.append(dmerged * pb.astype(F32) * (s * (1.0 - s)))
    return tuple(dps + dgs)


def _epi_mid(mix, h, g1, sh2, sc2, w2):
    return (mix,) + _f_mid(h, mix, g1, sh2, sc2, w2)


def _epi_resid(o, h_mid, g2):
    return (o,) + _f_residual(h_mid, o, g2)


def _epi_sqrelu(a):
    r = jnp.maximum(a, 0.0)
    return a, r * r


def _epi_sqrelu_bwd(dhh, a):
    return (dhh * (2.0 * jnp.maximum(a.astype(F32), 0.0)),)


def _f_residual(h_mid, o, g2):
    return (h_mid + g2 * o,)


def _f_silu(x):
    return (x * jax.nn.sigmoid(x),)


def _f_bias(x, b):
    return (x + b,)


def _ssd_rows(xbc, p):
    rows = [Cols(xbc, LANES * k, LANES) for k in range(4)]
    rows += [Cols(xbc, 512 + LANES * g, LANES) for g in range(2)]
    rows += [Cols(xbc, 768 + LANES * g, LANES) for g in range(2)]
    return rows + [_pcol(p, "dt")]


def _ret_rows(rq, rk, p):
    off_v = IN_LAYOUT["rv"][0]
    return ([Cols(rq, LANES * h, LANES) for h in range(4)] + [Cols(rk, LANES * h, LANES) for h in range(4)]
            + [Cols(p, off_v + LANES * h, LANES) for h in range(4)])


def layer_fwd(li, h, mod, lw, tabs, m_ctx):
    t = h.shape[0]
    nb = m_ctx // min(ROW_TILE, t)
    sh1, sc1, g1, sh2, sc2, g2 = mod
    nm = lambda s: f"l{li}_{s}"
    sv = {}
    (u,) = rowwise_fwd(nm("norm1"), _f_norm_mod, [h], [sh1, sc1], [lw["norm1_w"]], [(D_MODEL, CDT)], t, nb)
    p = mm(nm("in_proj"), u, lw["w_in"], F32)
    q, k, v = rowwise_fwd(
        nm("attn_prep"), _f_attn_prep,
        [_pcol(p, "q"), _pcol(p, "k"), _pcol(p, "v"), tabs["ca"], tabs["sa"]], [],
        [lw["qw"], lw["kw"], tabs["gq"], tabs["gk"]], [(512, CDT), (128, CDT), (128, CDT)], t, nb)
    tq = min(ATTN_TQ, m_ctx)
    kk, vv = _split_kv(k), _split_kv(v)
    ones_rows = jnp.concatenate([jnp.ones((2, 1, t), CDT), jnp.zeros((2, ATTN_ONES_ROWS - 1, t), CDT)], axis=1)
    attn_o, qT, oT, lse = attn_fwd(nm("attn"), q, kk, jnp.concatenate([vv.transpose(0, 2, 1), ones_rows], axis=1),
                                   m_ctx, tq)

    xbc = conv_fwd(_pcol(p, "xbc"), lw["conv_w"], lw["conv_b"], m_ctx)
    ssd_sh = [lw["dt_bias"], lw["a_log"]]
    yf, yb, sf, sb = scan_fwd(nm("ssd"), _make_ssd_chunk, _ssd_rows(xbc, p), ssd_sh, 4, 512, t, m_ctx)
    (ssd_o,) = rowwise_fwd(nm("ssd_fin"), _f_ssd_finish, [yf, yb, Cols(xbc, 0, 512), _pcol(p, "z")], [],
                           [lw["d_exp"], lw["ssd_nw"]], [(512, CDT)], t, nb)

    rq, rk = rowwise_fwd(nm("ret_prep"), _f_ret_prep, [_pcol(p, "rq"), _pcol(p, "rk"), tabs["rc"], tabs["rs"]],
                         [], [], [(512, F32), (512, F32)], t, nb)
    rf, rb, rsf, rsb = scan_fwd(nm("ret"), _make_ret_chunk, _ret_rows(rq, rk, p), [lw["ret_lg"]], 4, 512, t, m_ctx)
    (ret_o,) = rowwise_fwd(nm("ret_fin"), _f_ret_finish, [rf, rb, _pcol(p, "rg")], [], [lw["ret_gw"]],
                           [(512, CDT)], t, nb)

    pbs = [mm(nm(f"branch{b}"), br, lw["w_branch"][b], CDT) for b, br in enumerate((attn_o, ssd_o, ret_o))]
    gl = [Cols(p, 1024 * b, 1024) for b in range(3)]
    (merged,) = rowwise_fwd(nm("merge"), _f_merge, pbs + gl, [], [], [(D_MODEL, CDT)], t, nb)
    mix, h_mid, vv2 = mm(nm("out_proj"), merged, lw["w_out"], None, epilogue=_epi_mid, tile_ins=[h],
                         typed_ins=[g1, sh2, sc2], shared_ins=[lw["norm2_w"]], out_dtypes=[F32, F32, CDT], nb_ctx=nb,
                         whole_rows=True)
    a, hh = mm(nm("mlp1"), vv2, lw["w_mlp1"], None, epilogue=_epi_sqrelu, out_dtypes=[CDT, CDT])
    o, h_out = mm(nm("mlp2"), hh, lw["w_mlp2"], None, epilogue=_epi_resid, tile_ins=[h_mid], typed_ins=[g2],
                  out_dtypes=[F32, F32], nb_ctx=nb)
    sv.update(h=h, u=u, p=p, qT=qT, kk=kk, vv=vv, oT=oT, lse=lse, attn_o=attn_o, xbc=xbc, yf=yf, yb=yb,
              sf=sf, sb=sb, ssd_o=ssd_o, rq=rq, rk=rk, rf=rf, rb=rb, rsf=rsf, rsb=rsb, ret_o=ret_o, pbs=pbs,
              merged=merged, mix=mix, h_mid=h_mid, v=vv2, a=a, hh=hh, o=o)
    return h_out, sv


def layer_bwd(li, dh_out, sv, mod, lw, tabs, m_ctx):
    t = dh_out.shape[0]
    nb = m_ctx // min(ROW_TILE, t)
    sh1, sc1, g1, sh2, sc2, g2 = mod
    nm = lambda s: f"l{li}_{s}_bwd"
    gw = {}
    p = sv["p"]
    (do,), (dg2,), _ = rowwise_bwd(nm("resid"), _f_residual, [sv["h_mid"], sv["o"]], [g2], [], [dh_out],
                                   [False, True], [], [CDT], t, nb)
    (da,) = mm(nm("mlp2_dx"), do, lw["w_mlp2"], None, transpose_b=True, epilogue=_epi_sqrelu_bwd,
               tile_ins=[sv["a"]], out_dtypes=[CDT])
    gw["w_mlp2"] = mm_tn(nm("mlp2_dw"), sv["hh"], do, CDT, ("rows", lw["w_mlp2"].shape[0] // 4))
    dv = mm(nm("mlp1_dx"), da, lw["w_mlp1"], F32, transpose_b=True)
    gw["w_mlp1"] = mm_tn(nm("mlp1_dw"), sv["v"], da, CDT, ("cols", lw["w_mlp1"].shape[1] // 4))
    (dh_a, dmix), (dg1, dsh2, dsc2), (gw["norm2_w"],) = rowwise_bwd(
        nm("mid"), _f_mid, [sv["h"], sv["mix"]], [g1, sh2, sc2], [lw["norm2_w"]], [dh_out, dv],
        [True, True], [True], [F32, CDT], t, nb)
    gl = [Cols(p, 1024 * b, 1024) for b in range(3)]
    dmg = mm(nm("out_dx"), dmix, lw["w_out"], None, transpose_b=True, epilogue=_epi_merge_bwd,
             tile_ins=sv["pbs"] + gl, out_dtypes=[CDT] * 6)
    gw["w_out"] = mm_tn(nm("out_dw"), sv["merged"], dmix, CDT, ("rows", lw["w_out"].shape[0] // 4))
    dpb, dgl = dmg[:3], dmg[3:]
    brs = (sv["attn_o"], sv["ssd_o"], sv["ret_o"])
    d_attn_o = mm(nm("branch0_dx"), dpb[0], lw["w_branch"][0], CDT, transpose_b=True)
    d_ssd_o = mm(nm("branch1_dx"), dpb[1], lw["w_branch"][1], F32, transpose_b=True)
    d_ret_o = mm(nm("branch2_dx"), dpb[2], lw["w_branch"][2], F32, transpose_b=True)
    n_loc = lw["w_branch"].shape[2] // 4
    gw["w_branch"] = jnp.stack([mm_tn(nm(f"branch{b}_dw"), brs[b], dpb[b], CDT, ("cols", n_loc)) for b in range(3)],
                               axis=1).reshape(4, -1, n_loc)
    tq = min(ATTN_TQ, m_ctx)
    dq_rows, dk_s, dv_s = attn_bwd(nm("attn"), sv["qT"], d_attn_o, sv["oT"], sv["lse"], sv["kk"],
                                   sv["kk"].transpose(0, 2, 1), sv["vv"], m_ctx)
    (dq_raw, dk_raw, dv_raw), _, (gw["qw"], gw["kw"]) = rowwise_bwd(
        nm("attn_prep"), _f_attn_prep,
        [_pcol(p, "q"), _pcol(p, "k"), _pcol(p, "v"), tabs["ca"], tabs["sa"]], [],
        [lw["qw"], lw["kw"], tabs["gq"], tabs["gk"]],
        [dq_rows, _merge_kv(dk_s) * LN2, _merge_kv(dv_s)],
        [True, True, True, False, False], [True, True, False, False], [CDT] * 3, t, nb)
    (dy_ssd, dxs_fin, dz), _, (gw["d_exp"], gw["ssd_nw"]) = rowwise_bwd(
        nm("ssd_fin"), _f_ssd_finish, [sv["yf"], sv["yb"], Cols(sv["xbc"], 0, 512), _pcol(p, "z")], [],
        [lw["d_exp"], lw["ssd_nw"]], [d_ssd_o], [True, False, True, True], [True, True], [F32, F32, CDT], t, nb)
    ssd_sh = [lw["dt_bias"], lw["a_log"]]
    post_ssd = lambda d: [jnp.concatenate(d[0:8], axis=1), d[8]]
    (dxbc_f, ddt_f), dsh_f = scan_bwd(nm("ssd_f"), _make_ssd_chunk, _ssd_rows(sv["xbc"], p), ssd_sh, (sv["sf"],),
                                      dy_ssd, post_ssd, [(1024, F32), (LANES, F32)], 4, t, m_ctx, dirs=(0,))
    (dxbc_b, ddt_b), dsh_b = scan_bwd(nm("ssd_b"), _make_ssd_chunk, _ssd_rows(sv["xbc"], p), ssd_sh, (sv["sb"],),
                                      dy_ssd, post_ssd, [(1024, F32), (LANES, F32)], 4, t, m_ctx, dirs=(1,))
    gw["dt_bias"], gw["a_log"] = dsh_f[0] + dsh_b[0], dsh_f[1] + dsh_b[1]
    ddt = (ddt_f + ddt_b).astype(CDT)
    dxbc_raw, gw["conv_w"], gw["conv_b"] = conv_bwd(_pcol(p, "xbc"), lw["conv_w"], lw["conv_b"], dxbc_f, dxbc_b,
                                                    dxs_fin, m_ctx)
    (dy_ret, drg), _, (gw["ret_gw"],) = rowwise_bwd(
        nm("ret_fin"), _f_ret_finish, [sv["rf"], sv["rb"], _pcol(p, "rg")], [], [lw["ret_gw"]], [d_ret_o],
        [True, False, True], [True], [F32, CDT], t, nb)
    post_ret = lambda d: [jnp.concatenate(d[0:4], axis=1), jnp.concatenate(d[4:8], axis=1),
                          jnp.concatenate(d[8:12], axis=1)]
    rrows = _ret_rows(sv["rq"], sv["rk"], p)
    (dq_f, dk_f, dv_f), (dq_b, dk_b, dv_b), (gw["ret_lg"],) = scan_bwd(
        nm("ret"), _make_ret_chunk, rrows, [lw["ret_lg"]], (sv["rsf"], sv["rsb"]), dy_ret, post_ret,
        [(512, F32)] * 3, 4, t, m_ctx)
    drv = (dv_f + dv_b).astype(CDT)
    (drq, drk), _, _ = rowwise_bwd(nm("ret_prep"), _f_ret_prep,
                                   [_pcol(p, "rq"), _pcol(p, "rk"), tabs["rc"], tabs["rs"]], [], [],
                                   [(dq_f, dq_b), (dk_f, dk_b)], [True, True, False, False], [], [CDT, CDT], t, nb)
    pieces = {"gates": None, "xbc": dxbc_raw, "q": dq_raw, "z": dz, "rq": drq, "rk": drk, "rv": drv, "rg": drg,
              "k": dk_raw, "v": dv_raw, "dt": ddt}
    cols = list(dgl) + [pieces[n] for n in IN_NEW_ORDER[1:]]
    used = sum(c.shape[1] for c in cols)
    cols.append(jnp.zeros((t, IN_PAD - used), CDT))
    dp = jnp.concatenate(cols, axis=1)
    du = mm(nm("in_dx"), dp, lw["w_in"], F32, transpose_b=True)
    gw["w_in"] = mm_tn(nm("in_dw"), sv["u"], dp, CDT)
    (dh_in,), (dsh1, dsc1), (gw["norm1_w"],) = rowwise_bwd(
        nm("norm1"), _f_norm_mod_thru, [sv["h"]], [sh1, sc1], [lw["norm1_w"]], [dh_a, du], [True], [True], [F32],
        t, nb)
    return dh_in, [dsh1, dsc1, dg1, dsh2, dsc2, dg2], gw


def _rope_tables(n_lat, m_ctx):
    rows = n_lat // GRID_W
    row = jnp.repeat(jnp.arange(rows, dtype=F32), GRID_W)
    col = jnp.tile(jnp.arange(GRID_W, dtype=F32), rows)
    nfreq = ATTN_HD // 4
    inv = ROPE_THETA ** (-jnp.arange(nfreq, dtype=F32) / nfreq)
    ang = jnp.concatenate([row[:, None] * inv, col[:, None] * inv], axis=-1)
    cos = jnp.concatenate([jnp.ones((m_ctx, ATTN_HD // 2), F32), jnp.cos(ang)], axis=0)
    sin = jnp.concatenate([jnp.zeros((m_ctx, ATTN_HD // 2), F32), jnp.sin(ang)], axis=0)
    c64 = jnp.concatenate([cos, cos], axis=1)
    s64 = jnp.concatenate([-sin, sin], axis=1)
    pos = jnp.arange(m_ctx + n_lat, dtype=F32)
    inv_r = ROPE_THETA ** (-jnp.linspace(0.0, 1.0, RET_DK // 2, dtype=F32))
    ang_r = pos[:, None] * inv_r
    rc = jnp.concatenate([jnp.cos(ang_r)] * 2, axis=1)
    rs = jnp.concatenate([-jnp.sin(ang_r), jnp.sin(ang_r)], axis=1)
    return dict(ca=jnp.tile(c64, (1, 2)), sa=jnp.tile(s64, (1, 2)), rc=rc, rs=rs, gq=_group_matrix(512, ATTN_HD),
                gk=_group_matrix(128, ATTN_HD))


def _layer_weights(full, small, layer):
    return dict(
        w_in=full["w_in"][layer], w_branch=full["w_branch"][layer], w_out=full["w_out"][layer],
        w_mlp1=full["w_mlp1"][layer], w_mlp2=full["w_mlp2"][layer],
        norm1_w=small["norm1_w"][layer][None], norm2_w=small["norm2_w"][layer][None],
        qw=jnp.tile(small["attn_q_norm"][layer], 8)[None], kw=jnp.tile(small["attn_k_norm"][layer], 2)[None],
        conv_w=small["ssd_conv_w"][layer], conv_b=small["ssd_conv_b"][layer][None],
        dt_bias=_lane_pad(small["ssd_dt_bias"][layer]), a_log=_lane_pad(small["ssd_a_log"][layer]),
        d_exp=jnp.repeat(small["ssd_d"][layer], SSD_HD)[None], ssd_nw=small["ssd_norm_w"][layer][None],
        ret_lg=_lane_pad(small["ret_log_decay"][layer]), ret_gw=small["ret_gn_w"][layer][None])


def local_step(x, c, ctx, full, small, loss_target):
    n_lat, d = x.shape
    m_ctx = ctx.shape[0]
    t = n_lat + m_ctx
    depth = small["norm1_w"].shape[0]
    tabs = _rope_tables(n_lat, m_ctx)
    h = jnp.concatenate([ctx, x], axis=0)
    cc = jnp.concatenate([small["c_ctx"][None], c, jnp.zeros((COND_ROWS - 2, d), F32)], axis=0)
    (scc,) = rowwise_fwd("cond_silu", _f_silu, [cc], [], [], [(d, CDT)], COND_ROWS, 0)
    mods, saved, lws = [], [], []
    for layer in range(depth):
        lw = _layer_weights(full, small, layer)
        mod_raw = mm(f"l{layer}_mod", scc, full["w_mod"][layer], F32)
        (mod8,) = rowwise_fwd(f"l{layer}_mod_bias", _f_bias, [mod_raw], [], [small["b_mod"][layer][None]],
                              [(6 * d, F32)], COND_ROWS, 0)
        mod = [mod8[0:2, k * d:(k + 1) * d].reshape(2, 1, d) for k in range(6)]
        h, sv = layer_fwd(layer, h, mod, lw, tabs, m_ctx)
        mods.append(mod)
        saved.append(sv)
        lws.append(lw)
    loss, dh, d_final = loss_head(h, loss_target, small["final_norm_w"][None], m_ctx)

    gbig = {k: [None] * depth for k in BIG}
    gs = {k: [None] * depth for k in SMALL if k not in ("c_ctx", "final_norm_w")}
    d_scc = None
    for layer in reversed(range(depth)):
        lw = lws[layer]
        dh, dmod, gw = layer_bwd(layer, dh, saved[layer], mods[layer], lw, tabs, m_ctx)
        dmod8 = jnp.concatenate([jnp.concatenate([g_.reshape(2, d) for g_ in dmod], axis=1),
                                 jnp.zeros((COND_ROWS - 2, 6 * d), F32)], axis=0)
        (dmod_c,), _, (db_mod,) = rowwise_bwd(f"l{layer}_mod_bias_bwd", _f_bias, [dmod8], [],
                                              [small["b_mod"][layer][None]], [dmod8], [True], [True], [CDT], COND_ROWS, 0)
        gbig["w_mod"][layer] = mm_tn(f"l{layer}_mod_dw", scc, dmod_c, CDT, ("cols", 6 * d // 4))
        part = mm(f"l{layer}_mod_dx", dmod_c, full["w_mod"][layer], F32, transpose_b=True)
        d_scc = part if d_scc is None else d_scc + part
        g_in = _in_from_padded(gw["w_in"])
        gbig["w_in"][layer] = g_in.reshape(d, 4, g_in.shape[1] // 4).transpose(1, 0, 2)
        for k in ("w_branch", "w_out", "w_mlp1", "w_mlp2"):
            gbig[k][layer] = gw[k]
        gs["b_mod"][layer] = db_mod.reshape(-1)
        gs["norm1_w"][layer] = gw["norm1_w"].reshape(-1)
        gs["norm2_w"][layer] = gw["norm2_w"].reshape(-1)
        gs["attn_q_norm"][layer] = gw["qw"].reshape(8, ATTN_HD).sum(0)
        gs["attn_k_norm"][layer] = gw["kw"].reshape(2, ATTN_HD).sum(0)
        gs["ssd_conv_w"][layer] = gw["conv_w"]
        gs["ssd_conv_b"][layer] = gw["conv_b"].reshape(-1)
        gs["ssd_dt_bias"][layer] = gw["dt_bias"][0, :16].reshape(2, 8)
        gs["ssd_a_log"][layer] = gw["a_log"][0, :16].reshape(2, 8)
        gs["ssd_d"][layer] = gw["d_exp"].reshape(SSD_HEADS, SSD_HD).sum(1)
        gs["ssd_norm_w"][layer] = gw["ssd_nw"].reshape(-1)
        gs["ret_log_decay"][layer] = gw["ret_lg"][0, :8].reshape(2, 4)
        gs["ret_gn_w"][layer] = gw["ret_gw"].reshape(-1)
    (d_cc,), _, _ = rowwise_bwd("cond_silu_bwd", _f_silu, [cc], [], [], [d_scc], [True], [], [F32], COND_ROWS, 0)
    g_small = {k: jnp.stack(v) for k, v in gs.items()}
    g_small["c_ctx"] = d_cc[0]
    g_small["final_norm_w"] = d_final.reshape(-1)
    return loss, dh[m_ctx:], gbig, g_small


def kernel(x, c, ctx, c_ctx, w_mod, b_mod, norm1_w, norm2_w, w_in, attn_q_norm, attn_k_norm, ssd_conv_w, ssd_conv_b, ssd_dt_bias, ssd_a_log, ssd_d, ssd_norm_w, ret_log_decay, ret_gn_w, w_branch, w_out, w_mlp1, w_mlp2, final_norm_w, loss_target, m_c_ctx, m_w_mod, m_b_mod, m_norm1_w, m_norm2_w, m_w_in, m_attn_q_norm, m_attn_k_norm, m_ssd_conv_w, m_ssd_conv_b, m_ssd_dt_bias, m_ssd_a_log, m_ssd_d, m_ssd_norm_w, m_ret_log_decay, m_ret_gn_w, m_w_branch, m_w_out, m_w_mlp1, m_w_mlp2, m_final_norm_w, v_c_ctx, v_w_mod, v_b_mod, v_norm1_w, v_norm2_w, v_w_in, v_attn_q_norm, v_attn_k_norm, v_ssd_conv_w, v_ssd_conv_b, v_ssd_dt_bias, v_ssd_a_log, v_ssd_d, v_ssd_norm_w, v_ret_log_decay, v_ret_gn_w, v_w_branch, v_w_out, v_w_mlp1, v_w_mlp2, v_final_norm_w):
    env = dict(locals())
    w_loc = {k: env[k] for k in WEIGHTS}
    m_loc = {k: env["m_" + k] for k in WEIGHTS}
    v_loc = {k: env["v_" + k] for k in WEIGHTS}
    chip = 2 * lax.axis_index("x") + lax.axis_index("y")
    core = lax.axis_index("c")

    depth = w_loc["w_mod"].shape[0]
    assert depth == 2, "the exchanges split the layers between a chip's two cores"
    shards = [w_loc[k].astype(CDT).reshape(depth, -1, w_loc[k].shape[-1]) for k in BIG]
    mine = gather_layers("gather_weights", shards, [BIG_KIND[k] for k in BIG])
    full = {}
    for k, arr in zip(BIG, mine):
        both = exchange_both("share_" + k, arr.reshape(-1, arr.shape[-1]))
        if k == "w_in":
            both = both.reshape(depth, 4, -1, both.shape[-1]).transpose(0, 2, 1, 3)
            both = _in_to_padded(both.reshape(depth, both.shape[1], -1))
        full[k] = both.reshape((depth,) + w_loc[k].shape[1:-1] + (-1,)) if BIG_KIND[k] == "cols" else \
            both.reshape((depth,) + w_loc[k].shape[1:-2] + (-1, w_loc[k].shape[-1])) if BIG_KIND[k] == "rows" else both

    cw = w_loc["ssd_conv_w"]
    cw_w = cw.shape[-1]
    placed = lax.dynamic_update_slice(jnp.zeros(cw.shape[:-1] + (4 * cw_w,), F32),
                                      cw * (core == 0).astype(F32), (0, 0, chip * cw_w))
    conv_full = _unpack_flat(allreduce_small("gather_conv_w", _pack_flat([placed], F32, LANES)), [placed.shape])[0]
    small = {k: w_loc[k] for k in SMALL}
    small["ssd_conv_w"] = conv_full

    loss_l, grad_x, g_big, g_small = local_step(x[0], c, ctx[0], full, small, loss_target[0])

    small_shapes = [g_small[k].shape for k in SMALL] + [(LANES,)]
    summed = _unpack_flat(allreduce_small("reduce_small", _pack_flat([g_small[k] for k in SMALL] + [loss_l], F32, LANES)),
                          small_shapes)
    gsum = dict(zip(SMALL, summed[:-1]))
    loss = summed[-1][0]
    gsum["ssd_conv_w"] = lax.dynamic_slice(gsum["ssd_conv_w"], (0, 0, chip * cw_w), cw.shape)

    pair = []
    for k in BIG:
        _, rows, cols = g_big[k][0].shape
        pair.append(exchange_add("pair_" + k, g_big[k][0].reshape(4 * rows, cols),
                                 g_big[k][1].reshape(4 * rows, cols)).reshape(4, rows, cols))
    landed = scatter_pieces("scatter_grads", pair)
    g_sum = [sum_exchange("sum_" + k, parts) for k, parts in zip(BIG, landed)]

    grads, deltas, new_m, new_v = {}, {}, {}, {}
    for i, k in enumerate(BIG):
        shp = w_loc[k].shape
        three_d = lambda a, shp=shp: a.reshape((-1,) + shp[-2:])
        res = adamw("adamw_" + k, three_d(w_loc[k]), three_d(m_loc[k]), three_d(v_loc[k]), [three_d(g_sum[i])])
        grads[k], deltas[k], new_m[k], new_v[k] = [r.reshape(shp) for r in res]
    small_loc_shapes = [w_loc[k].shape for k in SMALL]
    res = adamw("adamw_small", _pack_flat([w_loc[k] for k in SMALL], F32, LANES)[None],
                _pack_flat([m_loc[k] for k in SMALL], F32, LANES)[None],
                _pack_flat([v_loc[k] for k in SMALL], F32, LANES)[None],
                [_pack_flat([gsum[k] for k in SMALL], F32, LANES)[None]])
    for dst, r in zip((grads, deltas, new_m, new_v), res):
        dst.update(dict(zip(SMALL, _unpack_flat(r, small_loc_shapes))))

    return (loss, grad_x[None], *[grads[k] for k in WEIGHTS], *[deltas[k] for k in WEIGHTS],
            *[new_m[k] for k in WEIGHTS], *[new_v[k] for k in WEIGHTS])
```

```python
import math
from typing import NamedTuple

import jax
import jax.numpy as jnp
from jax import lax
from jax.experimental import pallas as pl
from jax.experimental.pallas import tpu as pltpu

F32 = jnp.float32
CDT = jnp.bfloat16
NORM_EPS = 1e-6
ROPE_THETA = 10000.0
GRID_W = 64
D_MODEL = 1024
ATTN_HD = 64
SSD_HEADS, SSD_HD = 8, 64
RET_HEADS, RET_DK = 4, 128
CHUNK = 256
ROW_TILE = 256
ROW_TILE_TALL = 768
MM_ROWS = 768
MM_TN_ROWS = 2816
MM_VMEM_BUDGET = 44 * 1024 * 1024
ATTN_TQ, ATTN_TK = 256, 256
ATTN_ONES_ROWS = 16
ATTN_TK_BWD = 2048
LOG2E, LN2 = 1.4426950408889634, 0.6931471805599453
ATTN_TK_FWD = 2048
LANES = 128
PAIR_BLOCK_BYTES = 2 * 1024 * 1024
COND_ROWS = 16
VMEM_LIMIT = 56 * 1024 * 1024

ADAM_LR, ADAM_B1, ADAM_B2, ADAM_EPS, ADAM_WD, ADAM_STEP = 0.001, 0.9, 0.999, 1e-08, 0.01, 10

IN_LAYOUT = {
    "gates": (0, 3072, 4368, 3072), "xbc": (3072, 1024, 1280, 1024), "q": (4096, 512, 0, 512),
    "z": (4608, 512, 768, 512), "rq": (5120, 512, 2320, 512), "rk": (5632, 512, 2832, 512),
    "rv": (6144, 512, 3344, 512), "rg": (6656, 512, 3856, 512), "k": (7168, 128, 512, 128),
    "v": (7296, 128, 640, 128), "dt": (7424, 128, 2304, 16),
}
IN_PAD = 7680
IN_ORIG_ORDER = ("q", "k", "v", "z", "xbc", "dt", "rq", "rk", "rv", "rg", "gates")
IN_NEW_ORDER = ("gates", "xbc", "q", "z", "rq", "rk", "rv", "rg", "k", "v", "dt")

BIG = ("w_mod", "w_in", "w_branch", "w_out", "w_mlp1", "w_mlp2")
BIG_KIND = {"w_mod": "cols", "w_in": "slices", "w_branch": "cols", "w_out": "rows", "w_mlp1": "cols", "w_mlp2": "rows"}
SMALL = ("c_ctx", "b_mod", "norm1_w", "norm2_w", "attn_q_norm", "attn_k_norm", "ssd_conv_w", "ssd_conv_b",
         "ssd_dt_bias", "ssd_a_log", "ssd_d", "ssd_norm_w", "ret_log_decay", "ret_gn_w", "final_norm_w")
WEIGHTS = ("c_ctx", "w_mod", "b_mod", "norm1_w", "norm2_w", "w_in", "attn_q_norm", "attn_k_norm", "ssd_conv_w",
           "ssd_conv_b", "ssd_dt_bias", "ssd_a_log", "ssd_d", "ssd_norm_w", "ret_log_decay", "ret_gn_w",
           "w_branch", "w_out", "w_mlp1", "w_mlp2", "final_norm_w")


def _cp(sem):
    return pltpu.CompilerParams(dimension_semantics=sem, vmem_limit_bytes=VMEM_LIMIT)


class Cols(NamedTuple):
    arr: jax.Array
    off: int
    width: int


def _width(item):
    return item.width if isinstance(item, Cols) else item.shape[1]


def _row_in(item, rows, imap=None):
    imap = imap or (lambda i: i)
    if isinstance(item, Cols):
        assert item.off % item.width == 0
        blk = item.off // item.width
        return item.arr, pl.BlockSpec((rows, item.width), lambda i, blk=blk: (imap(i), blk))
    return item, pl.BlockSpec((rows, item.shape[1]), lambda i: (imap(i), 0))


def _const_spec(shape):
    return pl.BlockSpec(shape, lambda *_: (0,) * len(shape))


def _mxu(a, b, dims=(((1,), (0,)), ((), ()))):
    return lax.dot_general(a.astype(CDT), b.astype(CDT), dims, preferred_element_type=F32)


_NT = (((1,), (1,)), ((), ()))
_TN = (((0,), (0,)), ((), ()))


@jax.custom_vjp
def _softplus(x):
    return jnp.maximum(x, 0.0) + jnp.log1p(jnp.exp(-jnp.abs(x)))


def _softplus_fwd(x):
    return _softplus(x), x


def _softplus_bwd(x, g):
    return (g * jax.nn.sigmoid(x),)


_softplus.defvjp(_softplus_fwd, _softplus_bwd)


def _group_mean_impl(x, gmat):
    hi = x.astype(CDT)
    lo = (x - hi.astype(F32)).astype(CDT)
    return (jnp.dot(hi, gmat, preferred_element_type=F32) + jnp.dot(lo, gmat, preferred_element_type=F32))


@jax.custom_vjp
def _group_mean(x, gmat):
    return _group_mean_impl(x, gmat)


def _group_mean_fwd(x, gmat):
    return _group_mean_impl(x, gmat), gmat


def _group_mean_bwd(gmat, g):
    return _group_mean_impl(g, gmat), jnp.zeros_like(gmat)


_group_mean.defvjp(_group_mean_fwd, _group_mean_bwd)


def _group_matrix(width, group):
    r = jnp.arange(width) // group
    return jnp.where(r[:, None] == r[None, :], 1.0 / group, 0.0).astype(CDT)


def _make_rope(half):
    def partner(x):
        w = x.shape[1]
        lane = lax.broadcasted_iota(jnp.int32, x.shape, 1)
        first = (lane % (2 * half)) < half
        return jnp.where(first, pltpu.roll(x, w - half, axis=1), pltpu.roll(x, half, axis=1))

    def impl(x, cos_full, sin_signed):
        return x * cos_full + partner(x) * sin_signed

    @jax.custom_vjp
    def rope(x, cos_full, sin_signed):
        return impl(x, cos_full, sin_signed)

    def fwd(x, cos_full, sin_signed):
        return impl(x, cos_full, sin_signed), (cos_full, sin_signed)

    def bwd(res, g):
        cos_full, sin_signed = res
        return impl(g, cos_full, -sin_signed), jnp.zeros_like(cos_full), jnp.zeros_like(sin_signed)

    rope.defvjp(fwd, bwd)
    return rope


_rope32 = _make_rope(32)
_rope64 = _make_rope(64)


def _rms(x, w):
    return x * lax.rsqrt(jnp.mean(x * x, axis=-1, keepdims=True) + NORM_EPS) * w


def _col(v, lane_index):
    lane = lax.broadcasted_iota(jnp.int32, v.shape, 1)
    return jnp.sum(jnp.where(lane == lane_index, v, 0.0), axis=1, keepdims=True)


def _select_typed(tvals, step, tm, m_ctx):
    row = step * tm + lax.broadcasted_iota(jnp.int32, (tm, 1), 0)
    return [jnp.where(row >= m_ctx, tv[1], tv[0]) for tv in tvals]


def _row_tile(n_rows):
    return _pick(n_rows, (ROW_TILE_TALL, ROW_TILE))


def rowwise_fwd(name, f, rows, typed, shared, outs, n_rows, m_ctx):
    tm = _row_tile(n_rows)
    nr, nt = len(rows), len(typed)
    nin = nr + nt + len(shared)

    def body(*refs):
        vals = [r[...] for r in refs[:nin]]
        sel = _select_typed(vals[nr:nr + nt], pl.program_id(0), tm, m_ctx)
        res = f(*vals[:nr], *sel, *vals[nr + nt:])
        for o_ref, o in zip(refs[nin:], res):
            o_ref[...] = o.astype(o_ref.dtype)

    arrs, specs = [], []
    for it in rows:
        a, s = _row_in(it, tm)
        arrs.append(a)
        specs.append(s)
    for t in typed:
        arrs.append(t)
        specs.append(_const_spec(t.shape))
    for s_ in shared:
        arrs.append(s_)
        specs.append(_const_spec(s_.shape))
    res = pl.pallas_call(
        body, name=name, grid=(n_rows // tm,), in_specs=specs,
        out_specs=[pl.BlockSpec((tm, w), lambda i: (i, 0)) for w, _ in outs],
        out_shape=[jax.ShapeDtypeStruct((n_rows, w), dt) for w, dt in outs],
        compiler_params=_cp(("parallel",)))(*arrs)
    return res


def rowwise_bwd(name, f, rows, typed, shared, cots, row_diff, shared_diff, drow_dtypes, n_rows, m_ctx, tall=True):
    tm = _row_tile(n_rows) if tall else min(ROW_TILE, n_rows)
    cot_groups = [c_ if isinstance(c_, tuple) else (c_,) for c_ in cots]
    cots = [a for grp in cot_groups for a in grp]
    nr, nt, ns, nc = len(rows), len(typed), len(shared), len(cots)
    nin = nr + nt + ns
    d_rows = [k for k in range(nr) if row_diff[k]]
    d_sh = [k for k in range(ns) if shared_diff[k]]

    def body(*refs):
        rvals = [r[...] for r in refs[:nr]]
        tvals = [r[...] for r in refs[nr:nr + nt]]
        svals = [r[...] for r in refs[nr + nt:nin]]
        cparts = [r[...].astype(F32) for r in refs[nin:nin + nc]]
        cvals = []
        for grp in cot_groups:
            cvals.append(sum(cparts[1:len(grp)], cparts[0]))
            cparts = cparts[len(grp):]
        out_refs = refs[nin + nc:]
        i = pl.program_id(0)

        def g(*dv):
            dv = list(dv)
            rv = list(rvals)
            for k in d_rows:
                rv[k] = dv.pop(0)
            tv = _select_typed([dv.pop(0) for _ in range(nt)], i, tm, m_ctx)
            sv = list(svals)
            for k in d_sh:
                sv[k] = dv.pop(0)
            return tuple(o.astype(F32) for o in f(*rv, *tv, *sv))

        prim = [rvals[k].astype(F32) for k in d_rows] + tvals + [svals[k] for k in d_sh]
        _, vjp = jax.vjp(g, *prim)
        grads = list(vjp(tuple(cvals)))
        for ref in out_refs[:len(d_rows)]:
            ref[...] = grads.pop(0).astype(ref.dtype)
        for ref in out_refs[len(d_rows):]:
            gr = grads.pop(0)

            @pl.when(i == 0)
            def _(ref=ref, gr=gr):
                ref[...] = gr

            @pl.when(i != 0)
            def _(ref=ref, gr=gr):
                ref[...] += gr

    arrs, specs = [], []
    for it in list(rows):
        a, s = _row_in(it, tm)
        arrs.append(a)
        specs.append(s)
    for t in typed:
        arrs.append(t)
        specs.append(_const_spec(t.shape))
    for s_ in shared:
        arrs.append(s_)
        specs.append(_const_spec(s_.shape))
    for c_ in cots:
        a, s = _row_in(c_, tm)
        arrs.append(a)
        specs.append(s)
    out_specs, out_shape = [], []
    for k, dt in zip(d_rows, drow_dtypes):
        w = _width(rows[k])
        out_specs.append(pl.BlockSpec((tm, w), lambda i: (i, 0)))
        out_shape.append(jax.ShapeDtypeStruct((n_rows, w), dt))
    for t in typed:
        out_specs.append(_const_spec(t.shape))
        out_shape.append(jax.ShapeDtypeStruct(t.shape, F32))
    for k in d_sh:
        out_specs.append(_const_spec(shared[k].shape))
        out_shape.append(jax.ShapeDtypeStruct(shared[k].shape, F32))
    res = pl.pallas_call(body, name=name, grid=(n_rows // tm,), in_specs=specs, out_specs=out_specs,
                         out_shape=out_shape, compiler_params=_cp(("arbitrary",)))(*arrs)
    n1, n2 = len(d_rows), len(d_rows) + nt
    return list(res[:n1]), list(res[n1:n2]), list(res[n2:])


def _pick(n, prefs):
    for p in prefs:
        if n % p == 0:
            return p
    return n


def mm(name, a, b, out_dtype, transpose_b=False, epilogue=None, tile_ins=(), typed_ins=(), shared_ins=(),
       out_dtypes=None, nb_ctx=None, whole_rows=False):
    n, k = b.shape if transpose_b else b.shape[::-1]
    m = (a.arr if isinstance(a, Cols) else a).shape[0]
    assert _width(a) == k
    tm = _pick(m, (MM_ROWS, 256))
    out_dtypes = out_dtypes or [out_dtype]
    tile_arrs = [x.arr if isinstance(x, Cols) else x for x in tile_ins]
    tile_bytes = sum(jnp.dtype(d).itemsize for d in out_dtypes) + sum(x.dtype.itemsize for x in tile_arrs)
    fits = lambda c: 2 * (tm * k * 2 + k * c * 2 + tm * c * tile_bytes) <= MM_VMEM_BUDGET
    tn = next(c for c in (2560, 2048, 1536, 1024, 512, 256, 128, n) if n % c == 0 and fits(c) or c == n)
    if whole_rows:
        assert fits(n)
        tn = n
    dims = _NT if transpose_b else (((1,), (0,)), ((), ()))
    n_ex = len(tile_ins) + len(typed_ins) + len(shared_ins)

    n_tile, n_typed = len(tile_ins), len(typed_ins)

    def body(a_ref, b_ref, *refs):
        prod = lax.dot_general(a_ref[...], b_ref[...], dims, preferred_element_type=F32)
        ex = [r[...] for r in refs[:n_ex]]
        ex[n_tile:n_tile + n_typed] = _select_typed(ex[n_tile:n_tile + n_typed], pl.program_id(1), tm, nb_ctx)
        outs = (prod,) if epilogue is None else epilogue(prod, *ex)
        for o_ref, o in zip(refs[n_ex:], outs):
            o_ref[...] = o.astype(o_ref.dtype)

    a_arr, a_spec = _row_in(a, tm)
    a_spec = pl.BlockSpec(a_spec.block_shape, lambda j, i, f=a_spec.index_map: f(i))
    b_spec = pl.BlockSpec((tn, k), lambda j, i: (j, 0)) if transpose_b else pl.BlockSpec((k, tn), lambda j, i: (0, j))
    tile = pl.BlockSpec((tm, tn), lambda j, i: (i, j))
    ex_specs = []
    for x in tile_ins:
        base = x.off // tn if isinstance(x, Cols) else 0
        assert not isinstance(x, Cols) or (x.off % tn == 0 and x.width == n)
        ex_specs.append(pl.BlockSpec((tm, tn), lambda j, i, base=base: (i, base + j)))
    ex_specs += [pl.BlockSpec((2, 1, tn), lambda j, i: (0, 0, j))] * len(typed_ins)
    ex_specs += [pl.BlockSpec((1, tn), lambda j, i: (0, j))] * len(shared_ins)
    res = pl.pallas_call(
        body, name=name, grid=(n // tn, m // tm), in_specs=[a_spec, b_spec] + ex_specs,
        out_specs=[tile] * len(out_dtypes),
        out_shape=[jax.ShapeDtypeStruct((m, n), d) for d in out_dtypes],
        compiler_params=_cp(("parallel", "parallel")))(a_arr, b, *tile_arrs, *typed_ins, *shared_ins)
    return res[0] if epilogue is None else res


def mm_tn(name, a, b, out_dtype=F32, pieces=None):
    t = (a.arr if isinstance(a, Cols) else a).shape[0]
    k, n = _width(a), _width(b)
    tt = _pick(t, (MM_TN_ROWS, MM_ROWS, 256))
    k_unit = pieces[1] if pieces and pieces[0] == "rows" else k
    n_unit = pieces[1] if pieces and pieces[0] == "cols" else n
    tk = _pick(k_unit, (1024, 512, 256, 128))
    tn = _pick(n_unit, (1280, 1024, 512, 256, 128))
    n_t = t // tt

    def body(a_ref, b_ref, o_ref, acc):
        part = lax.dot_general(a_ref[...], b_ref[...], _TN, preferred_element_type=F32)
        ti = pl.program_id(2)

        @pl.when(ti == 0)
        def _():
            acc[...] = part

        @pl.when(ti != 0)
        def _():
            acc[...] += part

        @pl.when(ti == n_t - 1)
        def _():
            o_ref[...] = acc[...].astype(o_ref.dtype)

    def win(item, width):
        if isinstance(item, Cols):
            assert item.off % width == 0
            return item.arr, item.off // width
        return item, 0

    a_arr, a0 = win(a, tk)
    b_arr, b0 = win(b, tn)
    if pieces is None:
        out_spec = pl.BlockSpec((tk, tn), lambda ki, ni, ti: (ki, ni))
        out_shape = (k, n)
    elif pieces[0] == "cols":
        per = n_unit // tn
        out_spec = pl.BlockSpec((None, tk, tn), lambda ki, ni, ti: (ni // per, ki, ni % per))
        out_shape = (4, k, n_unit)
    else:
        per = k_unit // tk
        out_spec = pl.BlockSpec((None, tk, tn), lambda ki, ni, ti: (ki // per, ki % per, ni))
        out_shape = (4, k_unit, n)
    return pl.pallas_call(
        body, name=name, grid=(k // tk, n // tn, n_t),
        in_specs=[pl.BlockSpec((tt, tk), lambda ki, ni, ti: (ti, a0 + ki)),
                  pl.BlockSpec((tt, tn), lambda ki, ni, ti: (ti, b0 + ni))],
        out_specs=out_spec, out_shape=jax.ShapeDtypeStruct(out_shape, out_dtype),
        scratch_shapes=[pltpu.VMEM((tk, tn), F32)],
        compiler_params=_cp(("parallel", "parallel", "arbitrary")))(a_arr, b_arr)


def _heads_t(rows_blk):
    blk = rows_blk.astype(F32).T
    return jnp.concatenate([blk[hh * ATTN_HD:(hh + 1) * ATTN_HD, :] for hh in range(4)], axis=1)


def _heads_rows(t_blk):
    tq = t_blk.shape[1] // 4
    return jnp.concatenate([t_blk[:, hh * tq:(hh + 1) * tq] for hh in range(4)], axis=0).T


def attn_fwd(name, q, kk, vT_ones, m_ctx, tq):
    t, hd, hd_ext = kk.shape[1], ATTN_HD, vT_ones.shape[1]
    nq, r = t // tq, 4 * tq
    tk = _pick(t - m_ctx, (ATTN_TK_FWD, ATTN_TK))
    nqc, n_lat_tiles = m_ctx // tq, (t - m_ctx) // tk

    def body(q_ref, k_ref, vT_ref, o_ref, qT_ref, oT_ref, lse_ref):
        i = pl.program_id(1)
        q_t = _heads_t(q_ref[...]).astype(CDT)
        qT_ref[...] = q_t

        def tile(off, size, carry):
            mi, acc = carry
            sub = min(size, ATTN_TK)
            offs = [off + u * sub for u in range(size // sub)]
            sts = [jnp.dot(k_ref[pl.ds(o, sub), :], q_t, preferred_element_type=F32) for o in offs]
            for o, st in zip(offs, sts):
                mn = jnp.maximum(mi, jnp.max(st, axis=0, keepdims=True))
                pt = jnp.exp2(st - mn)
                acc = jnp.exp2(mi - mn) * acc + jnp.dot(vT_ref[:, pl.ds(o, sub)], pt.astype(CDT),
                                                        preferred_element_type=F32)
                mi = mn
            return mi, acc

        carry = tile(0, m_ctx, (jnp.full((1, r), -1e30, F32), jnp.zeros((hd_ext, r), F32)))
        mi, acc = lax.fori_loop(
            0, jnp.where(i < nqc, 0, n_lat_tiles),
            lambda j, cr: tile(pl.multiple_of(m_ctx + j * tk, ATTN_TK), tk, cr), carry)
        li = acc[hd:hd + 1]
        o_t = acc[:hd] / li
        oT_ref[...] = o_t.astype(oT_ref.dtype)
        o_ref[...] = _heads_rows(o_t).astype(o_ref.dtype)
        lse_ref[...] = mi + jnp.log2(li)

    blk_t = pl.BlockSpec((None, None, hd, r), lambda g, i: (g, i, 0, 0))
    rows = pl.BlockSpec((tq, 4 * hd), lambda g, i: (i, g))
    return pl.pallas_call(
        body, name=name, grid=(2, nq),
        in_specs=[rows, pl.BlockSpec((None, t, hd), lambda g, i: (g, 0, 0)),
                  pl.BlockSpec((None, hd_ext, t), lambda g, i: (g, 0, 0))],
        out_specs=[rows, blk_t, blk_t, pl.BlockSpec((None, None, 1, r), lambda g, i: (g, i, 0, 0))],
        out_shape=[jax.ShapeDtypeStruct((t, 8 * hd), CDT), jax.ShapeDtypeStruct((2, nq, hd, r), CDT),
                   jax.ShapeDtypeStruct((2, nq, hd, r), CDT), jax.ShapeDtypeStruct((2, nq, 1, r), F32)],
        compiler_params=_cp(("parallel", "arbitrary")))(q, kk, vT_ones)


def attn_bwd(name, qT, do, oT, lse, kk, kT, vv, m_ctx):
    _, nq, hd, r = qT.shape
    t = kk.shape[1]
    tq = r // 4
    tk = _pick(t - m_ctx, (ATTN_TK_BWD, ATTN_TK))
    nqc, n_lat_tiles = m_ctx // tq, (t - m_ctx) // tk

    def body(qT_ref, do_ref, oT_ref, lse_ref, k_ref, kT_ref, v_ref, dq_ref, dk_ref, dv_ref):
        i = pl.program_id(1)

        @pl.when(i == 0)
        def _():
            dk_ref[...] = jnp.zeros_like(dk_ref)
            dv_ref[...] = jnp.zeros_like(dv_ref)

        q_t = qT_ref[...]
        do_f = _heads_t(do_ref[...])
        do_t = do_f.astype(CDT)
        lse = lse_ref[...]
        delta = jnp.sum(do_f * oT_ref[...].astype(F32), axis=0, keepdims=True)

        def tile(off, size, dq):
            sub = min(size, ATTN_TK)
            offs = [off + u * sub for u in range(size // sub)]
            sts = [jnp.dot(k_ref[pl.ds(o, sub), :], q_t, preferred_element_type=F32) for o in offs]
            dpts = [jnp.dot(v_ref[pl.ds(o, sub), :], do_t, preferred_element_type=F32) for o in offs]
            for o, st, dpt in zip(offs, sts, dpts):
                pt = jnp.exp2(st - lse)
                dv_ref[pl.ds(o, sub), :] += lax.dot_general(pt.astype(CDT), do_t, _NT, preferred_element_type=F32)
                dst = (pt * (dpt - delta)).astype(CDT)
                dk_ref[pl.ds(o, sub), :] += lax.dot_general(dst, q_t, _NT, preferred_element_type=F32)
                dq = dq + jnp.dot(kT_ref[:, pl.ds(o, sub)], dst, preferred_element_type=F32)
            return dq

        dq = tile(0, m_ctx, jnp.zeros((hd, r), F32))
        dq = lax.fori_loop(0, jnp.where(i < nqc, 0, n_lat_tiles),
                           lambda j, acc: tile(pl.multiple_of(m_ctx + j * tk, ATTN_TK), tk, acc), dq)
        dq_ref[...] = _heads_rows(dq * LN2)

    blk_t = pl.BlockSpec((None, None, hd, r), lambda g, i: (g, i, 0, 0))
    row = pl.BlockSpec((None, None, 1, r), lambda g, i: (g, i, 0, 0))
    kv = pl.BlockSpec((None, t, hd), lambda g, i: (g, 0, 0))
    rows = pl.BlockSpec((tq, 4 * hd), lambda g, i: (i, g))
    return pl.pallas_call(
        body, name=name, grid=(2, nq),
        in_specs=[blk_t, rows, blk_t, row, kv, pl.BlockSpec((None, hd, t), lambda g, i: (g, 0, 0)), kv],
        out_specs=[rows, kv, kv],
        out_shape=[jax.ShapeDtypeStruct((t, 8 * hd), F32), jax.ShapeDtypeStruct(kk.shape, F32),
                   jax.ShapeDtypeStruct(kk.shape, F32)],
        compiler_params=_cp(("parallel", "arbitrary")))(qT, do, oT, lse, kk, kT, vv)


def _split_kv(a):
    return a.reshape(a.shape[0], 2, ATTN_HD).transpose(1, 0, 2)


def _merge_kv(a):
    return a.transpose(1, 0, 2).reshape(a.shape[1], 2 * ATTN_HD)


def _chunk_order(rev, ncc, nct):
    if not rev:
        return lambda s: s
    return lambda s: jnp.where(s < ncc, ncc - 1 - s, nct - 1 - (s - ncc))


def scan_fwd(name, make_fn, rows, shared, n_state, y_width, n_rows, m_ctx):
    nct, ncc = n_rows // CHUNK, m_ctx // CHUNK
    orders = [_chunk_order(rev, ncc, nct) for rev in (False, True)]
    fns = [make_fn(0), make_fn(1)]
    nr, ns = len(rows), len(shared)

    def body(*refs):
        svals = [r[...] for r in refs[2 * nr:2 * nr + ns]]
        y_refs, sin_refs, st = refs[2 * nr + ns:2 * nr + ns + 2], refs[2 * nr + ns + 2:2 * nr + ns + 4], refs[-1]

        @pl.when(pl.program_id(0) == 0)
        def _():
            st[...] = jnp.zeros_like(st)

        for d in range(2):
            rvals = [r[...] for r in refs[d * nr:(d + 1) * nr]]
            prev = [st[d, k] for k in range(n_state)]
            sin_refs[d][...] = st[d]
            y, new = fns[d](rvals, svals, prev)
            y_refs[d][...] = y
            for k in range(n_state):
                st[d, k] = new[k]

    arrs, specs = [], []
    for order in orders:
        for it in rows:
            a, s = _row_in(it, CHUNK, order)
            arrs.append(a)
            specs.append(s)
    for s_ in shared:
        arrs.append(s_)
        specs.append(_const_spec(s_.shape))
    return pl.pallas_call(
        body, name=name, grid=(nct,), in_specs=specs,
        out_specs=[pl.BlockSpec((CHUNK, y_width), lambda s, o=o: (o(s), 0)) for o in orders]
        + [pl.BlockSpec((None, n_state, LANES, LANES), lambda s, o=o: (o(s), 0, 0, 0)) for o in orders],
        out_shape=[jax.ShapeDtypeStruct((n_rows, y_width), F32)] * 2
        + [jax.ShapeDtypeStruct((nct, n_state, LANES, LANES), F32)] * 2,
        scratch_shapes=[pltpu.VMEM((2, n_state, LANES, LANES), F32)],
        compiler_params=_cp(("arbitrary",)))(*arrs)


def scan_bwd(name, make_fn, rows, shared, states_in, dy, post, outs, n_state, n_rows, m_ctx, dirs=(0, 1)):
    nct, ncc = n_rows // CHUNK, m_ctx // CHUNK
    orders = [(lambda r, f=_chunk_order(d == 1, ncc, nct): f(nct - 1 - r)) for d in dirs]
    fns = [make_fn(d) for d in dirs]
    nd = len(dirs)
    nr, ns, no = len(rows), len(shared), len(outs)
    n_in = nd * nr + ns

    def body(*refs):
        svals = [r[...] for r in refs[nd * nr:n_in]]
        sin_refs, dy_refs = refs[n_in:n_in + nd], refs[n_in + nd:n_in + 2 * nd]
        out_refs = refs[n_in + 2 * nd:n_in + 2 * nd + nd * no]
        dsh_refs = refs[n_in + 2 * nd + nd * no:-1]
        dst = refs[-1]
        r = pl.program_id(0)

        @pl.when(r == 0)
        def _():
            dst[...] = jnp.zeros_like(dst)

        d_shared = None
        for d in range(nd):
            rvals = [x[...] for x in refs[d * nr:(d + 1) * nr]]
            prev = [sin_refs[d][k] for k in range(n_state)]
            _, vjp = jax.vjp(fns[d], rvals, svals, prev)
            d_rows, d_sh, d_prev = vjp((dy_refs[d][...], [dst[d, k] for k in range(n_state)]))
            for ref, val in zip(out_refs[d * no:(d + 1) * no], post(d_rows)):
                ref[...] = val.astype(ref.dtype)
            d_shared = d_sh if d_shared is None else [a + b for a, b in zip(d_shared, d_sh)]
            for k in range(n_state):
                dst[d, k] = d_prev[k]
        for ref, gr in zip(dsh_refs, d_shared):
            @pl.when(r == 0)
            def _(ref=ref, gr=gr):
                ref[...] = gr

            @pl.when(r != 0)
            def _(ref=ref, gr=gr):
                ref[...] += gr

    arrs, specs = [], []
    for order in orders:
        for it in rows:
            a, s = _row_in(it, CHUNK, order)
            arrs.append(a)
            specs.append(s)
    for s_ in shared:
        arrs.append(s_)
        specs.append(_const_spec(s_.shape))
    for sin, order in zip(states_in, orders):
        arrs.append(sin)
        specs.append(pl.BlockSpec((None, n_state, LANES, LANES), lambda r, o=order: (o(r), 0, 0, 0)))
    for order in orders:
        a, s = _row_in(dy, CHUNK, order)
        arrs.append(a)
        specs.append(s)
    out_specs = [pl.BlockSpec((CHUNK, w), lambda r, o=o: (o(r), 0)) for o in orders for w, _ in outs]
    out_shape = [jax.ShapeDtypeStruct((n_rows, w), dt) for _ in orders for w, dt in outs]
    for s_ in shared:
        out_specs.append(_const_spec(s_.shape))
        out_shape.append(jax.ShapeDtypeStruct(s_.shape, F32))
    res = pl.pallas_call(body, name=name, grid=(nct,), in_specs=specs, out_specs=out_specs, out_shape=out_shape,
                         scratch_shapes=[pltpu.VMEM((nd, n_state, LANES, LANES), F32)],
                         compiler_params=_cp(("arbitrary",)))(*arrs)
    return [list(res[d * no:(d + 1) * no]) for d in range(nd)] + [list(res[nd * no:])]


def _make_ssd_chunk(direction):
    rev = direction == 1
    base = 8 * direction

    def fn(rows, shared, prev):
        xs, bms, cms, dtraw = rows[0:4], rows[4:6], rows[6:8], rows[8]
        dt_bias, a_log = shared
        ln = dtraw.shape[0]
        dt_all = _softplus(dtraw + dt_bias)
        a_all = dt_all * (-jnp.exp(a_log))
        r_i = lax.broadcasted_iota(jnp.int32, (ln, ln), 0)
        c_i = lax.broadcasted_iota(jnp.int32, (ln, ln), 1)
        tri = (r_i <= c_i) if rev else (r_i >= c_i)
        a_cum_all = jnp.dot(tri.astype(F32), a_all, precision=lax.Precision.HIGHEST, preferred_element_type=F32)
        a_tot_all = jnp.sum(a_all, axis=0, keepdims=True)
        first = lax.broadcasted_iota(jnp.int32, (ln, LANES), 1) < SSD_HD
        first_row = lax.broadcasted_iota(jnp.int32, (LANES, 1), 0) < SSD_HD

        def lmat(acol):
            a_b = jnp.broadcast_to(acol, (ln, ln))
            seg = a_b - a_b.T
            return jnp.where(tri, jnp.exp(jnp.where(tri, seg, 0.0)), 0.0)

        ys, new = [], []
        for g in range(2):
            bm, cm = bms[g], cms[g]
            cb = _mxu(cm, bm, _NT)
            for jj in range(2):
                pr = 2 * g + jj
                h0, h1 = base + 2 * pr, base + 2 * pr + 1
                ac0, ac1 = _col(a_cum_all, h0), _col(a_cum_all, h1)
                at0, at1 = _col(a_tot_all, h0), _col(a_tot_all, h1)
                dt_pair = jnp.where(first, _col(dt_all, h0), _col(dt_all, h1))
                acum_pair = jnp.where(first, ac0, ac1)
                atot_pair = jnp.where(first[0:1], at0, at1)
                xd = xs[pr] * dt_pair
                st = _mxu(xd * jnp.exp(atot_pair - acum_pair), bm, _TN)
                new.append(prev[pr] * jnp.where(first_row, jnp.exp(at0), jnp.exp(at1)) + st)
                y0 = _mxu(cb * lmat(ac0), xd)
                y1 = _mxu(cb * lmat(ac1), xd)
                y_off = _mxu(cm, prev[pr], _NT) * jnp.exp(acum_pair)
                ys.append(jnp.where(first, y0, y1) + y_off)
        return jnp.concatenate(ys, axis=1), new

    return fn


def _make_ret_chunk(direction):
    rev = direction == 1
    base = 4 * direction

    def fn(rows, shared, prev):
        qs, ks, vs = rows[0:4], rows[4:8], rows[8:12]
        lg_all = -jnp.exp(shared[0])
        ln = qs[0].shape[0]
        pos = lax.broadcasted_iota(jnp.int32, (ln, 1), 0).astype(F32)
        r_i = lax.broadcasted_iota(jnp.int32, (ln, ln), 0)
        c_i = lax.broadcasted_iota(jnp.int32, (ln, ln), 1)
        diff = ((c_i - r_i) if rev else (r_i - c_i))
        mask = diff >= 0
        dpos = jnp.maximum(diff, 0).astype(F32)
        k_pow = pos if rev else (ln - 1.0 - pos)
        q_pow = (ln - pos) if rev else (pos + 1.0)
        ys, new = [], []
        for h in range(RET_HEADS):
            lg = _col(lg_all, base + h)
            dmat = jnp.where(mask, jnp.exp(dpos * lg), 0.0)
            st = _mxu(ks[h] * jnp.exp(k_pow * lg), vs[h], _TN)
            new.append(prev[h] * jnp.exp(ln * lg) + st)
            s = _mxu(qs[h], ks[h], _NT) * dmat
            ys.append(_mxu(s, vs[h]) + _mxu(qs[h], prev[h]) * jnp.exp(q_pow * lg))
        return jnp.concatenate(ys, axis=1), new

    return fn


def _conv_pre(x, w, b, t_idx, n_rows, m_ctx):
    is_start = (t_idx == 0) | (t_idx == m_ctx)
    is_end = (t_idx == m_ctx - 1) | (t_idx == n_rows - 1)
    xp = jnp.where(is_start, 0.0, pltpu.roll(x, 1, axis=0))
    xn = jnp.where(is_end, 0.0, pltpu.roll(x, n_rows - 1, axis=0))
    return w[0:1] * xp + w[1:2] * x + w[2:3] * xn + b, xp, xn, is_start, is_end


def conv_fwd(x, conv_w, conv_b, m_ctx):
    n_rows, width = x.arr.shape[0], x.width
    c0 = x.off // LANES

    def body(x_ref, w_ref, b_ref, o_ref):
        t_idx = lax.broadcasted_iota(jnp.int32, (n_rows, 1), 0)
        pre = _conv_pre(x_ref[...], w_ref[...], b_ref[...], t_idx, n_rows, m_ctx)[0]
        o_ref[...] = pre * jax.nn.sigmoid(pre)

    return pl.pallas_call(
        body, name="conv_fwd", grid=(width // LANES,),
        in_specs=[pl.BlockSpec((n_rows, LANES), lambda c: (0, c0 + c)),
                  pl.BlockSpec((3, LANES), lambda c: (0, c)), pl.BlockSpec((1, LANES), lambda c: (0, c))],
        out_specs=pl.BlockSpec((n_rows, LANES), lambda c: (0, c)),
        out_shape=jax.ShapeDtypeStruct((n_rows, width), F32),
        compiler_params=_cp(("parallel",)))(x.arr, conv_w, conv_b)


def conv_bwd(x, conv_w, conv_b, dy_a, dy_b, dxs_extra, m_ctx):
    n_rows, width = x.arr.shape[0], x.width
    c0 = x.off // LANES
    n_extra = dxs_extra.shape[1] // LANES

    def body(x_ref, w_ref, b_ref, dya_ref, dyb_ref, ex_ref, dx_ref, dw_ref, db_ref):
        c = pl.program_id(0)
        t_idx = lax.broadcasted_iota(jnp.int32, (n_rows, 1), 0)
        w = w_ref[...]
        pre, xp, xn, is_start, is_end = _conv_pre(x_ref[...], w, b_ref[...], t_idx, n_rows, m_ctx)
        sg = jax.nn.sigmoid(pre)
        dyv = dya_ref[...] + dyb_ref[...] + jnp.where(c < n_extra, ex_ref[...], 0.0)
        dpre = dyv * (sg * (1.0 + pre * (1.0 - sg)))
        d_next = jnp.where(is_end, 0.0, pltpu.roll(dpre, n_rows - 1, axis=0))
        d_prev = jnp.where(is_start, 0.0, pltpu.roll(dpre, 1, axis=0))
        dx_ref[...] = (w[1:2] * dpre + w[0:1] * d_next + w[2:3] * d_prev).astype(dx_ref.dtype)
        dw_ref[...] = jnp.concatenate([jnp.sum(dpre * xp, axis=0, keepdims=True),
                                       jnp.sum(dpre * x_ref[...], axis=0, keepdims=True),
                                       jnp.sum(dpre * xn, axis=0, keepdims=True)], axis=0)
        db_ref[...] = jnp.sum(dpre, axis=0, keepdims=True)

    return pl.pallas_call(
        body, name="conv_bwd", grid=(width // LANES,),
        in_specs=[pl.BlockSpec((n_rows, LANES), lambda c: (0, c0 + c)),
                  pl.BlockSpec((3, LANES), lambda c: (0, c)), pl.BlockSpec((1, LANES), lambda c: (0, c)),
                  pl.BlockSpec((n_rows, LANES), lambda c: (0, c)), pl.BlockSpec((n_rows, LANES), lambda c: (0, c)),
                  pl.BlockSpec((n_rows, LANES), lambda c: (0, jnp.minimum(c, n_extra - 1)))],
        out_specs=[pl.BlockSpec((n_rows, LANES), lambda c: (0, c)),
                   pl.BlockSpec((3, LANES), lambda c: (0, c)), pl.BlockSpec((1, LANES), lambda c: (0, c))],
        out_shape=[jax.ShapeDtypeStruct((n_rows, width), CDT), jax.ShapeDtypeStruct((3, width), F32),
                   jax.ShapeDtypeStruct((1, width), F32)],
        compiler_params=_cp(("parallel",)))(x.arr, conv_w, conv_b, dy_a, dy_b, dxs_extra)


def loss_head(h, target, final_w, m_ctx):
    n_rows, d = h.shape
    tm = min(ROW_TILE, n_rows)
    nb_ctx = m_ctx // tm

    def f(hb, w, tgt):
        err = _rms(hb, w) - tgt
        return 0.5 * jnp.sum(jnp.mean(err * err, axis=-1))

    def body(h_ref, t_ref, w_ref, loss_ref, dh_ref, dw_ref):
        i = pl.program_id(0)

        @pl.when(i < nb_ctx)
        def _():
            dh_ref[...] = jnp.zeros_like(dh_ref)

        @pl.when(i == 0)
        def _():
            loss_ref[...] = jnp.zeros_like(loss_ref)
            dw_ref[...] = jnp.zeros_like(dw_ref)

        @pl.when(i >= nb_ctx)
        def _():
            val, vjp = jax.vjp(lambda hb, w: f(hb, w, t_ref[...]), h_ref[...], w_ref[...])
            dh, dw = vjp(jnp.ones((), F32))
            dh_ref[...] = dh
            dw_ref[...] += dw
            loss_ref[...] += jnp.broadcast_to(val, loss_ref.shape)

    return pl.pallas_call(
        body, name="loss_head", grid=(n_rows // tm,),
        in_specs=[pl.BlockSpec((tm, d), lambda i: (i, 0)),
                  pl.BlockSpec((tm, d), lambda i: (jnp.maximum(i - nb_ctx, 0), 0)), _const_spec((1, d))],
        out_specs=[_const_spec((1, LANES)), pl.BlockSpec((tm, d), lambda i: (i, 0)), _const_spec((1, d))],
        out_shape=[jax.ShapeDtypeStruct((1, LANES), F32), jax.ShapeDtypeStruct((n_rows, d), F32),
                   jax.ShapeDtypeStruct((1, d), F32)],
        compiler_params=_cp(("arbitrary",)))(h, target, final_w)


def adamw(name, w, m, v, g_parts):
    lead, rows, cols = w.shape
    tr = _pick(rows, (256, 128, 64, 32, 16, 8))
    npart = len(g_parts)
    c1 = 1.0 - ADAM_B1 ** ADAM_STEP
    c2 = 1.0 - ADAM_B2 ** ADAM_STEP

    def body(*refs):
        w_ref, m_ref, v_ref = refs[:3]
        g = refs[3][...].astype(F32)
        for r in refs[4:3 + npart]:
            g = g + r[...].astype(F32)
        g_ref, d_ref, nm_ref, nv_ref = refs[3 + npart:]
        nm = ADAM_B1 * m_ref[...] + (1.0 - ADAM_B1) * g
        nv = ADAM_B2 * v_ref[...] + (1.0 - ADAM_B2) * (g * g)
        g_ref[...] = g
        nm_ref[...] = nm
        nv_ref[...] = nv
        d_ref[...] = -ADAM_LR * ((nm / c1) / (jnp.sqrt(nv / c2) + ADAM_EPS) + ADAM_WD * w_ref[...])

    spec = pl.BlockSpec((None, tr, cols), lambda l, i: (l, i, 0))
    return pl.pallas_call(
        body, name=name, grid=(lead, rows // tr), in_specs=[spec] * (3 + npart), out_specs=[spec] * 4,
        out_shape=[jax.ShapeDtypeStruct(w.shape, F32)] * 4,
        compiler_params=_cp(("parallel", "parallel")))(w, m, v, *g_parts)


MESH = pl.DeviceIdType.MESH
_HBM = pl.BlockSpec(memory_space=pl.ANY)


def _chip_peers():
    x, y, c = lax.axis_index("x"), lax.axis_index("y"), lax.axis_index("c")
    return x, y, c, [(1 - x, y), (x, 1 - y), (1 - x, 1 - y)]


def _window(ref, kind, chip, rows, cols):
    if kind == "cols":
        return ref.at[:, pl.ds(pl.multiple_of(chip * cols, LANES), cols)]
    if kind == "rows":
        return ref.at[pl.ds(pl.multiple_of(chip * rows, 8), rows), :]
    return ref.at[chip]


def _gathered_shape(kind, rows, cols):
    return {"cols": (rows, 4 * cols), "rows": (4 * rows, cols), "slices": (4, rows, cols)}[kind]


def gather_layers(name, shards, kinds):
    n = len(shards)

    def body(*refs):
        x_refs, o_refs = refs[:n], refs[n:2 * n]
        send_sems, recv_sems, local_sems = refs[2 * n:]
        x, y, c, peers = _chip_peers()
        me = 2 * x + y
        started = []
        for a in range(n):
            _, rows, cols = shards[a].shape
            src = x_refs[a].at[c]
            mine = pltpu.make_async_copy(src, _window(o_refs[a], kinds[a], me, rows, cols), local_sems.at[a])
            mine.start()
            started.append(mine.wait)
            for k, (px, py) in enumerate(peers):
                cp = pltpu.make_async_remote_copy(
                    src_ref=src, dst_ref=_window(o_refs[a], kinds[a], me, rows, cols), send_sem=send_sems.at[3 * a + k],
                    recv_sem=recv_sems.at[3 * a + k], device_id=(px, py, c), device_id_type=MESH)
                cp.start()
                started.append(cp.wait_send)
        for a in range(n):
            _, rows, cols = shards[a].shape
            for k, (px, py) in enumerate(peers):
                pltpu.make_async_remote_copy(
                    src_ref=x_refs[a].at[c], dst_ref=_window(o_refs[a], kinds[a], 2 * px + py, rows, cols),
                    send_sem=send_sems.at[3 * a + k], recv_sem=recv_sems.at[3 * a + k], device_id=(px, py, c),
                    device_id_type=MESH).wait_recv()
        for wait in started:
            wait()

    return pl.pallas_call(
        body, name=name, in_specs=[_HBM] * n, out_specs=[_HBM] * n,
        out_shape=[jax.ShapeDtypeStruct(_gathered_shape(kinds[a], *shards[a].shape[1:]), shards[a].dtype)
                   for a in range(n)],
        scratch_shapes=[pltpu.SemaphoreType.DMA((3 * n,)), pltpu.SemaphoreType.DMA((3 * n,)),
                        pltpu.SemaphoreType.DMA((n,))],
        )(*shards)


def scatter_pieces(name, pieces):
    n = len(pieces)

    def body(*refs):
        p_refs, o_refs = refs[:n], refs[n:2 * n]
        send_sems, recv_sems, local_sems = refs[2 * n:]
        x, y, c, peers = _chip_peers()
        me = 2 * x + y
        started = []
        for a in range(n):
            mine = pltpu.make_async_copy(p_refs[a].at[me], o_refs[a].at[me], local_sems.at[a])
            mine.start()
            started.append(mine.wait)
            for k, (px, py) in enumerate(peers):
                cp = pltpu.make_async_remote_copy(
                    src_ref=p_refs[a].at[2 * px + py], dst_ref=o_refs[a].at[me], send_sem=send_sems.at[3 * a + k],
                    recv_sem=recv_sems.at[3 * a + k], device_id=(px, py, c), device_id_type=MESH)
                cp.start()
                started.append(cp.wait_send)
        for a in range(n):
            for k, (px, py) in enumerate(peers):
                pltpu.make_async_remote_copy(
                    src_ref=p_refs[a].at[me], dst_ref=o_refs[a].at[2 * px + py], send_sem=send_sems.at[3 * a + k],
                    recv_sem=recv_sems.at[3 * a + k], device_id=(px, py, c), device_id_type=MESH).wait_recv()
        for wait in started:
            wait()

    return pl.pallas_call(
        body, name=name, in_specs=[_HBM] * n, out_specs=[_HBM] * n,
        out_shape=[jax.ShapeDtypeStruct(p.shape, p.dtype) for p in pieces],
        scratch_shapes=[pltpu.SemaphoreType.DMA((3 * n,)), pltpu.SemaphoreType.DMA((3 * n,)),
                        pltpu.SemaphoreType.DMA((n,))],
        )(*pieces)


def _pair_step(n_steps, x_ref, land, send_sems, recv_sems, credits, consume):
    x, y, c = lax.axis_index("x"), lax.axis_index("y"), lax.axis_index("c")
    sib = (x, y, 1 - c)
    i = pl.program_id(0)
    slot = i % 2

    @pl.when(i >= 2)
    def _():
        pl.semaphore_wait(credits.at[slot], 1)

    cp = pltpu.make_async_remote_copy(src_ref=x_ref, dst_ref=land.at[slot], send_sem=send_sems.at[slot],
                                      recv_sem=recv_sems.at[slot], device_id=sib, device_id_type=MESH)
    cp.start()
    cp.wait_recv()
    consume(land[slot])

    @pl.when(i < n_steps - 2)
    def _():
        pl.semaphore_signal(credits.at[slot], inc=1, device_id=sib, device_id_type=MESH)

    cp.wait_send()


def _pair_call(name, body, n_steps, in_specs, out_spec, out_shape, blk_shape, dtype, operands, extra_scratch=()):
    grid_spec = pltpu.PrefetchScalarGridSpec(
        num_scalar_prefetch=1, grid=(n_steps,), in_specs=in_specs, out_specs=out_spec,
        scratch_shapes=[pltpu.VMEM((2,) + blk_shape, dtype), pltpu.SemaphoreType.DMA((2,)),
                        pltpu.SemaphoreType.DMA((2,)), pltpu.SemaphoreType.REGULAR((2,)), *extra_scratch])
    return pl.pallas_call(body, name=name, grid_spec=grid_spec, out_shape=out_shape,
                          compiler_params=_cp(("arbitrary",)))(*operands)


def _place():
    return jnp.stack([lax.axis_index("x"), lax.axis_index("y"), lax.axis_index("c")]).astype(jnp.int32)


def _pair_rows(rows, row_bytes):
    for cand in (4096, 2048, 1024, 768, 512, 384, 256, 192, 128, 96, 64, 48, 32, 16):
        if rows % cand == 0 and cand * row_bytes <= PAIR_BLOCK_BYTES:
            return cand
    return _pick(rows, (16, 8))


def exchange_both(name, mine):
    rows, cols = mine.shape
    tr = _pair_rows(rows, cols * mine.dtype.itemsize)
    n_steps = rows // tr

    def body(s_ref, x_ref, o_ref, land, send_sems, recv_sems, credits):
        c = lax.axis_index("c")
        o_ref[c] = x_ref[...]

        def consume(v):
            o_ref[1 - c] = v
        _pair_step(n_steps, x_ref, land, send_sems, recv_sems, credits, consume)

    return _pair_call(name, body, n_steps, [pl.BlockSpec((tr, cols), lambda i, s: (i, 0))],
                      pl.BlockSpec((2, tr, cols), lambda i, s: (0, i, 0)),
                      jax.ShapeDtypeStruct((2, rows, cols), mine.dtype), (tr, cols), mine.dtype, (_place(), mine))


def exchange_add(name, layer0, layer1):
    rows, cols = layer0.shape
    tr = _pair_rows(rows, cols * layer0.dtype.itemsize)
    nb = rows // tr

    def body(s_ref, l0_ref, l1_ref, o_ref, land, send_sems, recv_sems, credits, send_buf):
        first = lax.axis_index("c") == 0
        send_buf[...] = jnp.where(first, l1_ref[...], l0_ref[...])
        mine = jnp.where(first, l0_ref[...], l1_ref[...]).astype(F32)

        def consume(v):
            o_ref[...] = (mine + v.astype(F32)).astype(o_ref.dtype)
        _pair_step(nb, send_buf, land, send_sems, recv_sems, credits, consume)

    spec = pl.BlockSpec((tr, cols), lambda i, s: (i, 0))
    return _pair_call(name, body, nb, [spec, spec], spec, jax.ShapeDtypeStruct((rows, cols), CDT),
                      (tr, cols), layer0.dtype, (_place(), layer0, layer1),
                      extra_scratch=(pltpu.VMEM((tr, cols), layer0.dtype),))


def sum_exchange(name, parts):
    npart, rows, cols = parts.shape
    tr = _pair_rows(rows, cols * 4)
    n_steps = rows // tr

    def body(s_ref, x_ref, o_ref, land, send_sems, recv_sems, credits, mine):
        c = lax.axis_index("c")
        acc = x_ref[0].astype(F32)
        for k in range(1, npart):
            acc = acc + x_ref[k].astype(F32)
        mine[...] = acc
        o_ref[c] = acc

        def consume(v):
            o_ref[1 - c] = v
        _pair_step(n_steps, mine, land, send_sems, recv_sems, credits, consume)

    return _pair_call(name, body, n_steps, [pl.BlockSpec((npart, tr, cols), lambda i, s: (0, i, 0))],
                      pl.BlockSpec((2, tr, cols), lambda i, s: (0, i, 0)),
                      jax.ShapeDtypeStruct((2, rows, cols), F32), (tr, cols), F32, (_place(), parts),
                      extra_scratch=(pltpu.VMEM((tr, cols), F32),))


def allreduce_small(name, buf):
    rows = buf.shape[0]

    def body(x_ref, out_ref, gath, send_sems, recv_sems):
        x, y, c = lax.axis_index("x"), lax.axis_index("y"), lax.axis_index("c")
        me = 4 * x + 2 * y + c
        masks = [(k >> 2 & 1, k >> 1 & 1, k & 1) for k in range(1, 8)]

        def flip(v, bit):
            return 1 - v if bit else v

        sends = []
        for k, (bx, by, bc) in enumerate(masks):
            cp = pltpu.make_async_remote_copy(src_ref=x_ref, dst_ref=gath.at[me], send_sem=send_sems.at[k],
                                              recv_sem=recv_sems.at[k],
                                              device_id=(flip(x, bx), flip(y, by), flip(c, bc)), device_id_type=MESH)
            cp.start()
            sends.append(cp)
        gath[me] = x_ref[...]
        for k, (bx, by, bc) in enumerate(masks):
            px, py, pc = flip(x, bx), flip(y, by), flip(c, bc)
            pltpu.make_async_remote_copy(src_ref=x_ref, dst_ref=gath.at[4 * px + 2 * py + pc],
                                         send_sem=send_sems.at[k], recv_sem=recv_sems.at[k],
                                         device_id=(px, py, pc), device_id_type=MESH).wait_recv()
        for cp in sends:
            cp.wait_send()
        acc = gath[0]
        for d in range(1, 8):
            acc = acc + gath[d]
        out_ref[...] = acc

    return pl.pallas_call(
        body, name=name, in_specs=[pl.BlockSpec(memory_space=pltpu.VMEM)],
        out_specs=pl.BlockSpec(memory_space=pltpu.VMEM), out_shape=jax.ShapeDtypeStruct(buf.shape, F32),
        scratch_shapes=[pltpu.VMEM((8, rows, LANES), F32), pltpu.SemaphoreType.DMA((7,)),
                        pltpu.SemaphoreType.DMA((7,))],
        )(buf)


def _pack_flat(arrs, dtype, width, row_mult=8):
    flat = jnp.concatenate([a.reshape(-1).astype(dtype) for a in arrs])
    pad = (-flat.shape[0]) % (row_mult * width)
    if pad:
        flat = jnp.concatenate([flat, jnp.zeros((pad,), dtype)])
    return flat.reshape(-1, width)


def _unpack_flat(buf, shapes):
    flat = buf.reshape(-1)
    out, off = [], 0
    for s in shapes:
        n = math.prod(s)
        out.append(flat[off:off + n].reshape(s))
        off += n
    return out


def _in_to_padded(w):
    parts = []
    for name in IN_NEW_ORDER:
        _, width, o_off, o_w = IN_LAYOUT[name]
        parts.append(w[..., o_off:o_off + o_w])
        if o_w < width:
            parts.append(jnp.zeros(w.shape[:-1] + (width - o_w,), w.dtype))
    used = sum(IN_LAYOUT[n][1] for n in IN_NEW_ORDER)
    parts.append(jnp.zeros(w.shape[:-1] + (IN_PAD - used,), w.dtype))
    return jnp.concatenate(parts, axis=-1)


def _in_from_padded(g):
    parts = []
    for name in IN_ORIG_ORDER:
        off, _, _, o_w = IN_LAYOUT[name]
        parts.append(g[..., off:off + o_w])
    return jnp.concatenate(parts, axis=-1)


def _pcol(p, name):
    off, width, _, _ = IN_LAYOUT[name]
    return Cols(p, off, width)


def _lane_pad(v, width=LANES):
    v = v.reshape(-1)
    return jnp.concatenate([v, jnp.zeros((width - v.shape[0],), v.dtype)]).reshape(1, width)


def _f_norm_mod(h, sh, sc, w):
    return (_rms(h, w) * (1.0 + sc) + sh,)


def _f_norm_mod_thru(h, sh, sc, w):
    return h, _rms(h, w) * (1.0 + sc) + sh


def _f_attn_prep(qraw, kraw, vraw, cos2, sin2, qw, kw, gq, gk):
    q = qraw * lax.rsqrt(_group_mean(qraw * qraw, gq) + NORM_EPS) * qw
    q = _rope32(q, jnp.tile(cos2, (1, 4)), jnp.tile(sin2, (1, 4))) * (ATTN_HD ** -0.5 * LOG2E)
    k = kraw * lax.rsqrt(_group_mean(kraw * kraw, gk) + NORM_EPS) * kw
    return q, _rope32(k, cos2, sin2), vraw


def _f_ssd_finish(yf, yb, xs, z, d_exp, nw):
    y = (yf + yb + d_exp * xs) * (z * jax.nn.sigmoid(z))
    return (_rms(y, nw),)


def _f_ret_prep(rq, rk, cos1, sin1):
    cos_full, sin_signed = jnp.tile(cos1, (1, 4)), jnp.tile(sin1, (1, 4))
    return _rope64(rq, cos_full, sin_signed), _rope64(rk, cos_full, sin_signed) * (RET_DK ** -0.5)


def _f_ret_finish(yf, yb, g, gw):
    y = yf + yb
    outs = []
    for h in range(RET_HEADS):
        yh = y[:, h * RET_DK:(h + 1) * RET_DK]
        yc = yh - jnp.mean(yh, axis=-1, keepdims=True)
        outs.append(yc * lax.rsqrt(jnp.mean(yc * yc, axis=-1, keepdims=True) + NORM_EPS))
    return (jnp.concatenate(outs, axis=1) * gw * (g * jax.nn.sigmoid(g)),)


def _f_merge(p0, p1, p2, g0, g1, g2):
    return (jax.nn.sigmoid(g0) * p0 + jax.nn.sigmoid(g1) * p1 + jax.nn.sigmoid(g2) * p2,)


def _f_mid(h, mix, g1, sh2, sc2, w2):
    h_mid = h + g1 * mix
    return h_mid, _rms(h_mid, w2) * (1.0 + sc2) + sh2


def _epi_merge_bwd(dmerged, p0, p1, p2, g0, g1, g2):
    dps, dgs = [], []
    for pb, gate in ((p0, g0), (p1, g1), (p2, g2)):
        s = jax.nn.sigmoid(gate)
        dps.append(dmerged * s)
        dgs.append(dmerged * pb.astype(F32) * (s * (1.0 - s)))
    return tuple(dps + dgs)


def _epi_mid(mix, h, g1, sh2, sc2, w2):
    return (mix,) + _f_mid(h, mix, g1, sh2, sc2, w2)


def _epi_resid(o, h_mid, g2):
    return (o,) + _f_residual(h_mid, o, g2)


def _epi_sqrelu(a):
    r = jnp.maximum(a, 0.0)
    return a, r * r


def _epi_sqrelu_bwd(dhh, a):
    return (dhh * (2.0 * jnp.maximum(a.astype(F32), 0.0)),)


def _f_residual(h_mid, o, g2):
    return (h_mid + g2 * o,)


def _f_silu(x):
    return (x * jax.nn.sigmoid(x),)


def _f_bias(x, b):
    return (x + b,)


def _ssd_rows(xbc, p):
    rows = [Cols(xbc, LANES * k, LANES) for k in range(4)]
    rows += [Cols(xbc, 512 + LANES * g, LANES) for g in range(2)]
    rows += [Cols(xbc, 768 + LANES * g, LANES) for g in range(2)]
    return rows + [_pcol(p, "dt")]


def _ret_rows(rq, rk, p):
    off_v = IN_LAYOUT["rv"][0]
    return ([Cols(rq, LANES * h, LANES) for h in range(4)] + [Cols(rk, LANES * h, LANES) for h in range(4)]
            + [Cols(p, off_v + LANES * h, LANES) for h in range(4)])


def layer_fwd(li, h, mod, lw, tabs, m_ctx):
    t = h.shape[0]
    nb = m_ctx
    sh1, sc1, g1, sh2, sc2, g2 = mod
    nm = lambda s: f"l{li}_{s}"
    sv = {}
    (u,) = rowwise_fwd(nm("norm1"), _f_norm_mod, [h], [sh1, sc1], [lw["norm1_w"]], [(D_MODEL, CDT)], t, nb)
    p = mm(nm("in_proj"), u, lw["w_in"], F32)
    q, k, v = rowwise_fwd(
        nm("attn_prep"), _f_attn_prep,
        [_pcol(p, "q"), _pcol(p, "k"), _pcol(p, "v"), tabs["ca"], tabs["sa"]], [],
        [lw["qw"], lw["kw"], tabs["gq"], tabs["gk"]], [(512, CDT), (128, CDT), (128, CDT)], t, nb)
    tq = min(ATTN_TQ, m_ctx)
    kk, vv = _split_kv(k), _split_kv(v)
    ones_rows = jnp.concatenate([jnp.ones((2, 1, t), CDT), jnp.zeros((2, ATTN_ONES_ROWS - 1, t), CDT)], axis=1)
    attn_o, qT, oT, lse = attn_fwd(nm("attn"), q, kk, jnp.concatenate([vv.transpose(0, 2, 1), ones_rows], axis=1),
                                   m_ctx, tq)

    xbc = conv_fwd(_pcol(p, "xbc"), lw["conv_w"], lw["conv_b"], m_ctx)
    ssd_sh = [lw["dt_bias"], lw["a_log"]]
    yf, yb, sf, sb = scan_fwd(nm("ssd"), _make_ssd_chunk, _ssd_rows(xbc, p), ssd_sh, 4, 512, t, m_ctx)
    (ssd_o,) = rowwise_fwd(nm("ssd_fin"), _f_ssd_finish, [yf, yb, Cols(xbc, 0, 512), _pcol(p, "z")], [],
                           [lw["d_exp"], lw["ssd_nw"]], [(512, CDT)], t, nb)

    rq, rk = rowwise_fwd(nm("ret_prep"), _f_ret_prep, [_pcol(p, "rq"), _pcol(p, "rk"), tabs["rc"], tabs["rs"]],
                         [], [], [(512, F32), (512, F32)], t, nb)
    rf, rb, rsf, rsb = scan_fwd(nm("ret"), _make_ret_chunk, _ret_rows(rq, rk, p), [lw["ret_lg"]], 4, 512, t, m_ctx)
    (ret_o,) = rowwise_fwd(nm("ret_fin"), _f_ret_finish, [rf, rb, _pcol(p, "rg")], [], [lw["ret_gw"]],
                           [(512, CDT)], t, nb)

    pbs = [mm(nm(f"branch{b}"), br, lw["w_branch"][b], CDT) for b, br in enumerate((attn_o, ssd_o, ret_o))]
    gl = [Cols(p, 1024 * b, 1024) for b in range(3)]
    (merged,) = rowwise_fwd(nm("merge"), _f_merge, pbs + gl, [], [], [(D_MODEL, CDT)], t, nb)
    mix, h_mid, vv2 = mm(nm("out_proj"), merged, lw["w_out"], None, epilogue=_epi_mid, tile_ins=[h],
                         typed_ins=[g1, sh2, sc2], shared_ins=[lw["norm2_w"]], out_dtypes=[F32, F32, CDT], nb_ctx=nb,
                         whole_rows=True)
    a, hh = mm(nm("mlp1"), vv2, lw["w_mlp1"], None, epilogue=_epi_sqrelu, out_dtypes=[CDT, CDT])
    o, h_out = mm(nm("mlp2"), hh, lw["w_mlp2"], None, epilogue=_epi_resid, tile_ins=[h_mid], typed_ins=[g2],
                  out_dtypes=[F32, F32], nb_ctx=nb)
    sv.update(h=h, u=u, p=p, qT=qT, kk=kk, vv=vv, oT=oT, lse=lse, attn_o=attn_o, xbc=xbc, yf=yf, yb=yb,
              sf=sf, sb=sb, ssd_o=ssd_o, rq=rq, rk=rk, rf=rf, rb=rb, rsf=rsf, rsb=rsb, ret_o=ret_o, pbs=pbs,
              merged=merged, mix=mix, h_mid=h_mid, v=vv2, a=a, hh=hh, o=o)
    return h_out, sv


def layer_bwd(li, dh_out, sv, mod, lw, tabs, m_ctx):
    t = dh_out.shape[0]
    nb = m_ctx
    sh1, sc1, g1, sh2, sc2, g2 = mod
    nm = lambda s: f"l{li}_{s}_bwd"
    gw = {}
    p = sv["p"]
    (do,), (dg2,), _ = rowwise_bwd(nm("resid"), _f_residual, [sv["h_mid"], sv["o"]], [g2], [], [dh_out],
                                   [False, True], [], [CDT], t, nb)
    (da,) = mm(nm("mlp2_dx"), do, lw["w_mlp2"], None, transpose_b=True, epilogue=_epi_sqrelu_bwd,
               tile_ins=[sv["a"]], out_dtypes=[CDT])
    gw["w_mlp2"] = mm_tn(nm("mlp2_dw"), sv["hh"], do, CDT, ("rows", lw["w_mlp2"].shape[0] // 4))
    dv = mm(nm("mlp1_dx"), da, lw["w_mlp1"], F32, transpose_b=True)
    gw["w_mlp1"] = mm_tn(nm("mlp1_dw"), sv["v"], da, CDT, ("cols", lw["w_mlp1"].shape[1] // 4))
    (dh_a, dmix), (dg1, dsh2, dsc2), (gw["norm2_w"],) = rowwise_bwd(
        nm("mid"), _f_mid, [sv["h"], sv["mix"]], [g1, sh2, sc2], [lw["norm2_w"]], [dh_out, dv],
        [True, True], [True], [F32, CDT], t, nb, tall=False)
    gl = [Cols(p, 1024 * b, 1024) for b in range(3)]
    dmg = mm(nm("out_dx"), dmix, lw["w_out"], None, transpose_b=True, epilogue=_epi_merge_bwd,
             tile_ins=sv["pbs"] + gl, out_dtypes=[CDT] * 6)
    gw["w_out"] = mm_tn(nm("out_dw"), sv["merged"], dmix, CDT, ("rows", lw["w_out"].shape[0] // 4))
    dpb, dgl = dmg[:3], dmg[3:]
    brs = (sv["attn_o"], sv["ssd_o"], sv["ret_o"])
    d_attn_o = mm(nm("branch0_dx"), dpb[0], lw["w_branch"][0], CDT, transpose_b=True)
    d_ssd_o = mm(nm("branch1_dx"), dpb[1], lw["w_branch"][1], F32, transpose_b=True)
    d_ret_o = mm(nm("branch2_dx"), dpb[2], lw["w_branch"][2], F32, transpose_b=True)
    n_loc = lw["w_branch"].shape[2] // 4
    gw["w_branch"] = jnp.stack([mm_tn(nm(f"branch{b}_dw"), brs[b], dpb[b], CDT, ("cols", n_loc)) for b in range(3)],
                               axis=1).reshape(4, -1, n_loc)
    tq = min(ATTN_TQ, m_ctx)
    dq_rows, dk_s, dv_s = attn_bwd(nm("attn"), sv["qT"], d_attn_o, sv["oT"], sv["lse"], sv["kk"],
                                   sv["kk"].transpose(0, 2, 1), sv["vv"], m_ctx)
    (dq_raw, dk_raw, dv_raw), _, (gw["qw"], gw["kw"]) = rowwise_bwd(
        nm("attn_prep"), _f_attn_prep,
        [_pcol(p, "q"), _pcol(p, "k"), _pcol(p, "v"), tabs["ca"], tabs["sa"]], [],
        [lw["qw"], lw["kw"], tabs["gq"], tabs["gk"]],
        [dq_rows, _merge_kv(dk_s) * LN2, _merge_kv(dv_s)],
        [True, True, True, False, False], [True, True, False, False], [CDT] * 3, t, nb)
    (dy_ssd, dxs_fin, dz), _, (gw["d_exp"], gw["ssd_nw"]) = rowwise_bwd(
        nm("ssd_fin"), _f_ssd_finish, [sv["yf"], sv["yb"], Cols(sv["xbc"], 0, 512), _pcol(p, "z")], [],
        [lw["d_exp"], lw["ssd_nw"]], [d_ssd_o], [True, False, True, True], [True, True], [F32, F32, CDT], t, nb)
    ssd_sh = [lw["dt_bias"], lw["a_log"]]
    post_ssd = lambda d: [jnp.concatenate(d[0:8], axis=1), d[8]]
    (dxbc_f, ddt_f), dsh_f = scan_bwd(nm("ssd_f"), _make_ssd_chunk, _ssd_rows(sv["xbc"], p), ssd_sh, (sv["sf"],),
                                      dy_ssd, post_ssd, [(1024, F32), (LANES, F32)], 4, t, m_ctx, dirs=(0,))
    (dxbc_b, ddt_b), dsh_b = scan_bwd(nm("ssd_b"), _make_ssd_chunk, _ssd_rows(sv["xbc"], p), ssd_sh, (sv["sb"],),
                                      dy_ssd, post_ssd, [(1024, F32), (LANES, F32)], 4, t, m_ctx, dirs=(1,))
    gw["dt_bias"], gw["a_log"] = dsh_f[0] + dsh_b[0], dsh_f[1] + dsh_b[1]
    ddt = (ddt_f + ddt_b).astype(CDT)
    dxbc_raw, gw["conv_w"], gw["conv_b"] = conv_bwd(_pcol(p, "xbc"), lw["conv_w"], lw["conv_b"], dxbc_f, dxbc_b,
                                                    dxs_fin, m_ctx)
    (dy_ret, drg), _, (gw["ret_gw"],) = rowwise_bwd(
        nm("ret_fin"), _f_ret_finish, [sv["rf"], sv["rb"], _pcol(p, "rg")], [], [lw["ret_gw"]], [d_ret_o],
        [True, False, True], [True], [F32, CDT], t, nb)
    post_ret = lambda d: [jnp.concatenate(d[0:4], axis=1), jnp.concatenate(d[4:8], axis=1),
                          jnp.concatenate(d[8:12], axis=1)]
    rrows = _ret_rows(sv["rq"], sv["rk"], p)
    (dq_f, dk_f, dv_f), (dq_b, dk_b, dv_b), (gw["ret_lg"],) = scan_bwd(
        nm("ret"), _make_ret_chunk, rrows, [lw["ret_lg"]], (sv["rsf"], sv["rsb"]), dy_ret, post_ret,
        [(512, F32)] * 3, 4, t, m_ctx)
    drv = (dv_f + dv_b).astype(CDT)
    (drq, drk), _, _ = rowwise_bwd(nm("ret_prep"), _f_ret_prep,
                                   [_pcol(p, "rq"), _pcol(p, "rk"), tabs["rc"], tabs["rs"]], [], [],
                                   [(dq_f, dq_b), (dk_f, dk_b)], [True, True, False, False], [], [CDT, CDT], t, nb)
    pieces = {"gates": None, "xbc": dxbc_raw, "q": dq_raw, "z": dz, "rq": drq, "rk": drk, "rv": drv, "rg": drg,
              "k": dk_raw, "v": dv_raw, "dt": ddt}
    cols = list(dgl) + [pieces[n] for n in IN_NEW_ORDER[1:]]
    used = sum(c.shape[1] for c in cols)
    cols.append(jnp.zeros((t, IN_PAD - used), CDT))
    dp = jnp.concatenate(cols, axis=1)
    du = mm(nm("in_dx"), dp, lw["w_in"], F32, transpose_b=True)
    gw["w_in"] = mm_tn(nm("in_dw"), sv["u"], dp, CDT)
    (dh_in,), (dsh1, dsc1), (gw["norm1_w"],) = rowwise_bwd(
        nm("norm1"), _f_norm_mod_thru, [sv["h"]], [sh1, sc1], [lw["norm1_w"]], [dh_a, du], [True], [True], [F32],
        t, nb)
    return dh_in, [dsh1, dsc1, dg1, dsh2, dsc2, dg2], gw


def _rope_tables(n_lat, m_ctx):
    rows = n_lat // GRID_W
    row = jnp.repeat(jnp.arange(rows, dtype=F32), GRID_W)
    col = jnp.tile(jnp.arange(GRID_W, dtype=F32), rows)
    nfreq = ATTN_HD // 4
    inv = ROPE_THETA ** (-jnp.arange(nfreq, dtype=F32) / nfreq)
    ang = jnp.concatenate([row[:, None] * inv, col[:, None] * inv], axis=-1)
    cos = jnp.concatenate([jnp.ones((m_ctx, ATTN_HD // 2), F32), jnp.cos(ang)], axis=0)
    sin = jnp.concatenate([jnp.zeros((m_ctx, ATTN_HD // 2), F32), jnp.sin(ang)], axis=0)
    c64 = jnp.concatenate([cos, cos], axis=1)
    s64 = jnp.concatenate([-sin, sin], axis=1)
    pos = jnp.arange(m_ctx + n_lat, dtype=F32)
    inv_r = ROPE_THETA ** (-jnp.linspace(0.0, 1.0, RET_DK // 2, dtype=F32))
    ang_r = pos[:, None] * inv_r
    rc = jnp.concatenate([jnp.cos(ang_r)] * 2, axis=1)
    rs = jnp.concatenate([-jnp.sin(ang_r), jnp.sin(ang_r)], axis=1)
    return dict(ca=jnp.tile(c64, (1, 2)), sa=jnp.tile(s64, (1, 2)), rc=rc, rs=rs, gq=_group_matrix(512, ATTN_HD),
                gk=_group_matrix(128, ATTN_HD))


def _layer_weights(full, small, layer):
    return dict(
        w_in=full["w_in"][layer], w_branch=full["w_branch"][layer], w_out=full["w_out"][layer],
        w_mlp1=full["w_mlp1"][layer], w_mlp2=full["w_mlp2"][layer],
        norm1_w=small["norm1_w"][layer][None], norm2_w=small["norm2_w"][layer][None],
        qw=jnp.tile(small["attn_q_norm"][layer], 8)[None], kw=jnp.tile(small["attn_k_norm"][layer], 2)[None],
        conv_w=small["ssd_conv_w"][layer], conv_b=small["ssd_conv_b"][layer][None],
        dt_bias=_lane_pad(small["ssd_dt_bias"][layer]), a_log=_lane_pad(small["ssd_a_log"][layer]),
        d_exp=jnp.repeat(small["ssd_d"][layer], SSD_HD)[None], ssd_nw=small["ssd_norm_w"][layer][None],
        ret_lg=_lane_pad(small["ret_log_decay"][layer]), ret_gw=small["ret_gn_w"][layer][None])


def local_step(x, c, ctx, full, small, loss_target):
    n_lat, d = x.shape
    m_ctx = ctx.shape[0]
    t = n_lat + m_ctx
    depth = small["norm1_w"].shape[0]
    tabs = _rope_tables(n_lat, m_ctx)
    h = jnp.concatenate([ctx, x], axis=0)
    cc = jnp.concatenate([small["c_ctx"][None], c, jnp.zeros((COND_ROWS - 2, d), F32)], axis=0)
    (scc,) = rowwise_fwd("cond_silu", _f_silu, [cc], [], [], [(d, CDT)], COND_ROWS, 0)
    mods, saved, lws = [], [], []
    for layer in range(depth):
        lw = _layer_weights(full, small, layer)
        mod_raw = mm(f"l{layer}_mod", scc, full["w_mod"][layer], F32)
        (mod8,) = rowwise_fwd(f"l{layer}_mod_bias", _f_bias, [mod_raw], [], [small["b_mod"][layer][None]],
                              [(6 * d, F32)], COND_ROWS, 0)
        mod = [mod8[0:2, k * d:(k + 1) * d].reshape(2, 1, d) for k in range(6)]
        h, sv = layer_fwd(layer, h, mod, lw, tabs, m_ctx)
        mods.append(mod)
        saved.append(sv)
        lws.append(lw)
    loss, dh, d_final = loss_head(h, loss_target, small["final_norm_w"][None], m_ctx)

    gbig = {k: [None] * depth for k in BIG}
    gs = {k: [None] * depth for k in SMALL if k not in ("c_ctx", "final_norm_w")}
    d_scc = None
    for layer in reversed(range(depth)):
        lw = lws[layer]
        dh, dmod, gw = layer_bwd(layer, dh, saved[layer], mods[layer], lw, tabs, m_ctx)
        dmod8 = jnp.concatenate([jnp.concatenate([g_.reshape(2, d) for g_ in dmod], axis=1),
                                 jnp.zeros((COND_ROWS - 2, 6 * d), F32)], axis=0)
        (dmod_c,), _, (db_mod,) = rowwise_bwd(f"l{layer}_mod_bias_bwd", _f_bias, [dmod8], [],
                                              [small["b_mod"][layer][None]], [dmod8], [True], [True], [CDT], COND_ROWS, 0)
        gbig["w_mod"][layer] = mm_tn(f"l{layer}_mod_dw", scc, dmod_c, CDT, ("cols", 6 * d // 4))
        part = mm(f"l{layer}_mod_dx", dmod_c, full["w_mod"][layer], F32, transpose_b=True)
        d_scc = part if d_scc is None else d_scc + part
        g_in = _in_from_padded(gw["w_in"])
        gbig["w_in"][layer] = g_in.reshape(d, 4, g_in.shape[1] // 4).transpose(1, 0, 2)
        for k in ("w_branch", "w_out", "w_mlp1", "w_mlp2"):
            gbig[k][layer] = gw[k]
        gs["b_mod"][layer] = db_mod.reshape(-1)
        gs["norm1_w"][layer] = gw["norm1_w"].reshape(-1)
        gs["norm2_w"][layer] = gw["norm2_w"].reshape(-1)
        gs["attn_q_norm"][layer] = gw["qw"].reshape(8, ATTN_HD).sum(0)
        gs["attn_k_norm"][layer] = gw["kw"].reshape(2, ATTN_HD).sum(0)
        gs["ssd_conv_w"][layer] = gw["conv_w"]
        gs["ssd_conv_b"][layer] = gw["conv_b"].reshape(-1)
        gs["ssd_dt_bias"][layer] = gw["dt_bias"][0, :16].reshape(2, 8)
        gs["ssd_a_log"][layer] = gw["a_log"][0, :16].reshape(2, 8)
        gs["ssd_d"][layer] = gw["d_exp"].reshape(SSD_HEADS, SSD_HD).sum(1)
        gs["ssd_norm_w"][layer] = gw["ssd_nw"].reshape(-1)
        gs["ret_log_decay"][layer] = gw["ret_lg"][0, :8].reshape(2, 4)
        gs["ret_gn_w"][layer] = gw["ret_gw"].reshape(-1)
    (d_cc,), _, _ = rowwise_bwd("cond_silu_bwd", _f_silu, [cc], [], [], [d_scc], [True], [], [F32], COND_ROWS, 0)
    g_small = {k: jnp.stack(v) for k, v in gs.items()}
    g_small["c_ctx"] = d_cc[0]
    g_small["final_norm_w"] = d_final.reshape(-1)
    return loss, dh[m_ctx:], gbig, g_small


def kernel(x, c, ctx, c_ctx, w_mod, b_mod, norm1_w, norm2_w, w_in, attn_q_norm, attn_k_norm, ssd_conv_w, ssd_conv_b, ssd_dt_bias, ssd_a_log, ssd_d, ssd_norm_w, ret_log_decay, ret_gn_w, w_branch, w_out, w_mlp1, w_mlp2, final_norm_w, loss_target, m_c_ctx, m_w_mod, m_b_mod, m_norm1_w, m_norm2_w, m_w_in, m_attn_q_norm, m_attn_k_norm, m_ssd_conv_w, m_ssd_conv_b, m_ssd_dt_bias, m_ssd_a_log, m_ssd_d, m_ssd_norm_w, m_ret_log_decay, m_ret_gn_w, m_w_branch, m_w_out, m_w_mlp1, m_w_mlp2, m_final_norm_w, v_c_ctx, v_w_mod, v_b_mod, v_norm1_w, v_norm2_w, v_w_in, v_attn_q_norm, v_attn_k_norm, v_ssd_conv_w, v_ssd_conv_b, v_ssd_dt_bias, v_ssd_a_log, v_ssd_d, v_ssd_norm_w, v_ret_log_decay, v_ret_gn_w, v_w_branch, v_w_out, v_w_mlp1, v_w_mlp2, v_final_norm_w):
    env = dict(locals())
    w_loc = {k: env[k] for k in WEIGHTS}
    m_loc = {k: env["m_" + k] for k in WEIGHTS}
    v_loc = {k: env["v_" + k] for k in WEIGHTS}
    chip = 2 * lax.axis_index("x") + lax.axis_index("y")
    core = lax.axis_index("c")

    depth = w_loc["w_mod"].shape[0]
    assert depth == 2, "the exchanges split the layers between a chip's two cores"
    shards = [w_loc[k].astype(CDT).reshape(depth, -1, w_loc[k].shape[-1]) for k in BIG]
    mine = gather_layers("gather_weights", shards, [BIG_KIND[k] for k in BIG])
    full = {}
    for k, arr in zip(BIG, mine):
        both = exchange_both("share_" + k, arr.reshape(-1, arr.shape[-1]))
        if k == "w_in":
            both = both.reshape(depth, 4, -1, both.shape[-1]).transpose(0, 2, 1, 3)
            both = _in_to_padded(both.reshape(depth, both.shape[1], -1))
        full[k] = both.reshape((depth,) + w_loc[k].shape[1:-1] + (-1,)) if BIG_KIND[k] == "cols" else \
            both.reshape((depth,) + w_loc[k].shape[1:-2] + (-1, w_loc[k].shape[-1])) if BIG_KIND[k] == "rows" else both

    cw = w_loc["ssd_conv_w"]
    cw_w = cw.shape[-1]
    placed = lax.dynamic_update_slice(jnp.zeros(cw.shape[:-1] + (4 * cw_w,), F32),
                                      cw * (core == 0).astype(F32), (0, 0, chip * cw_w))
    conv_full = _unpack_flat(allreduce_small("gather_conv_w", _pack_flat([placed], F32, LANES)), [placed.shape])[0]
    small = {k: w_loc[k] for k in SMALL}
    small["ssd_conv_w"] = conv_full

    loss_l, grad_x, g_big, g_small = local_step(x[0], c, ctx[0], full, small, loss_target[0])

    small_shapes = [g_small[k].shape for k in SMALL] + [(LANES,)]
    summed = _unpack_flat(allreduce_small("reduce_small", _pack_flat([g_small[k] for k in SMALL] + [loss_l], F32, LANES)),
                          small_shapes)
    gsum = dict(zip(SMALL, summed[:-1]))
    loss = summed[-1][0]
    gsum["ssd_conv_w"] = lax.dynamic_slice(gsum["ssd_conv_w"], (0, 0, chip * cw_w), cw.shape)

    pair = []
    for k in BIG:
        _, rows, cols = g_big[k][0].shape
        pair.append(exchange_add("pair_" + k, g_big[k][0].reshape(4 * rows, cols),
                                 g_big[k][1].reshape(4 * rows, cols)).reshape(4, rows, cols))
    landed = scatter_pieces("scatter_grads", pair)
    g_sum = [sum_exchange("sum_" + k, parts) for k, parts in zip(BIG, landed)]

    grads, deltas, new_m, new_v = {}, {}, {}, {}
    for i, k in enumerate(BIG):
        shp = w_loc[k].shape
        three_d = lambda a, shp=shp: a.reshape((-1,) + shp[-2:])
        res = adamw("adamw_" + k, three_d(w_loc[k]), three_d(m_loc[k]), three_d(v_loc[k]), [three_d(g_sum[i])])
        grads[k], deltas[k], new_m[k], new_v[k] = [r.reshape(shp) for r in res]
    small_loc_shapes = [w_loc[k].shape for k in SMALL]
    res = adamw("adamw_small", _pack_flat([w_loc[k] for k in SMALL], F32, LANES)[None],
                _pack_flat([m_loc[k] for k in SMALL], F32, LANES)[None],
                _pack_flat([v_loc[k] for k in SMALL], F32, LANES)[None],
                [_pack_flat([gsum[k] for k in SMALL], F32, LANES)[None]])
    for dst, r in zip((grads, deltas, new_m, new_v), res):
        dst.update(dict(zip(SMALL, _unpack_flat(r, small_loc_shapes))))

    return (loss, grad_x[None], *[grads[k] for k in WEIGHTS], *[deltas[k] for k in WEIGHTS],
            *[new_m[k] for k in WEIGHTS], *[new_v[k] for k in WEIGHTS])
```

```python
import math
from typing import NamedTuple

import jax
import jax.numpy as jnp
from jax import lax
from jax.experimental import pallas as pl
from jax.experimental.pallas import tpu as pltpu

F32 = jnp.float32
CDT = jnp.bfloat16
NORM_EPS = 1e-6
ROPE_THETA = 10000.0
GRID_W = 64
D_MODEL = 1024
ATTN_HD = 64
SSD_HEADS, SSD_HD = 8, 64
RET_HEADS, RET_DK = 4, 128
CHUNK = 256
ROW_TILE = 256
ROW_TILE_TALL = 768
MM_ROWS = 768
MM_TN_ROWS = 2816
MM_VMEM_BUDGET = 44 * 1024 * 1024
ATTN_TQ, ATTN_TK = 256, 256
ATTN_ONES_ROWS = 16
ATTN_TK_BWD = 2048
LOG2E, LN2 = 1.4426950408889634, 0.6931471805599453
ATTN_TK_FWD = 2048
LANES = 128
PAIR_BLOCK_BYTES = 2 * 1024 * 1024
COND_ROWS = 16
VMEM_LIMIT = 56 * 1024 * 1024

ADAM_LR, ADAM_B1, ADAM_B2, ADAM_EPS, ADAM_WD, ADAM_STEP = 0.001, 0.9, 0.999, 1e-08, 0.01, 10

IN_LAYOUT = {
    "gates": (0, 3072, 4368, 3072), "xbc": (3072, 1024, 1280, 1024), "q": (4096, 512, 0, 512),
    "z": (4608, 512, 768, 512), "rq": (5120, 512, 2320, 512), "rk": (5632, 512, 2832, 512),
    "rv": (6144, 512, 3344, 512), "rg": (6656, 512, 3856, 512), "k": (7168, 128, 512, 128),
    "v": (7296, 128, 640, 128), "dt": (7424, 128, 2304, 16),
}
IN_PAD = 7680
IN_ORIG_ORDER = ("q", "k", "v", "z", "xbc", "dt", "rq", "rk", "rv", "rg", "gates")
IN_NEW_ORDER = ("gates", "xbc", "q", "z", "rq", "rk", "rv", "rg", "k", "v", "dt")

BIG = ("w_mod", "w_in", "w_branch", "w_out", "w_mlp1", "w_mlp2")
BIG_KIND = {"w_mod": "cols", "w_in": "slices", "w_branch": "cols", "w_out": "rows", "w_mlp1": "cols", "w_mlp2": "rows"}
SMALL = ("c_ctx", "b_mod", "norm1_w", "norm2_w", "attn_q_norm", "attn_k_norm", "ssd_conv_w", "ssd_conv_b",
         "ssd_dt_bias", "ssd_a_log", "ssd_d", "ssd_norm_w", "ret_log_decay", "ret_gn_w", "final_norm_w")
WEIGHTS = ("c_ctx", "w_mod", "b_mod", "norm1_w", "norm2_w", "w_in", "attn_q_norm", "attn_k_norm", "ssd_conv_w",
           "ssd_conv_b", "ssd_dt_bias", "ssd_a_log", "ssd_d", "ssd_norm_w", "ret_log_decay", "ret_gn_w",
           "w_branch", "w_out", "w_mlp1", "w_mlp2", "final_norm_w")


def _cp(sem):
    return pltpu.CompilerParams(dimension_semantics=sem, vmem_limit_bytes=VMEM_LIMIT)


class Cols(NamedTuple):
    arr: jax.Array
    off: int
    width: int


def _width(item):
    return item.width if isinstance(item, Cols) else item.shape[1]


def _row_in(item, rows, imap=None):
    imap = imap or (lambda i: i)
    if isinstance(item, Cols):
        assert item.off % item.width == 0
        blk = item.off // item.width
        return item.arr, pl.BlockSpec((rows, item.width), lambda i, blk=blk: (imap(i), blk))
    return item, pl.BlockSpec((rows, item.shape[1]), lambda i: (imap(i), 0))


def _const_spec(shape):
    return pl.BlockSpec(shape, lambda *_: (0,) * len(shape))


def _mxu(a, b, dims=(((1,), (0,)), ((), ()))):
    return lax.dot_general(a.astype(CDT), b.astype(CDT), dims, preferred_element_type=F32)


_NT = (((1,), (1,)), ((), ()))
_TN = (((0,), (0,)), ((), ()))


@jax.custom_vjp
def _softplus(x):
    return jnp.maximum(x, 0.0) + jnp.log1p(jnp.exp(-jnp.abs(x)))


def _softplus_fwd(x):
    return _softplus(x), x


def _softplus_bwd(x, g):
    return (g * jax.nn.sigmoid(x),)


_softplus.defvjp(_softplus_fwd, _softplus_bwd)


def _group_mean_impl(x, gmat):
    hi = x.astype(CDT)
    lo = (x - hi.astype(F32)).astype(CDT)
    return (jnp.dot(hi, gmat, preferred_element_type=F32) + jnp.dot(lo, gmat, preferred_element_type=F32))


@jax.custom_vjp
def _group_mean(x, gmat):
    return _group_mean_impl(x, gmat)


def _group_mean_fwd(x, gmat):
    return _group_mean_impl(x, gmat), gmat


def _group_mean_bwd(gmat, g):
    return _group_mean_impl(g, gmat), jnp.zeros_like(gmat)


_group_mean.defvjp(_group_mean_fwd, _group_mean_bwd)


def _group_matrix(width, group):
    r = jnp.arange(width) // group
    return jnp.where(r[:, None] == r[None, :], 1.0 / group, 0.0).astype(CDT)


def _make_rope(half):
    def partner(x):
        w = x.shape[1]
        lane = lax.broadcasted_iota(jnp.int32, x.shape, 1)
        first = (lane % (2 * half)) < half
        return jnp.where(first, pltpu.roll(x, w - half, axis=1), pltpu.roll(x, half, axis=1))

    def impl(x, cos_full, sin_signed):
        return x * cos_full + partner(x) * sin_signed

    @jax.custom_vjp
    def rope(x, cos_full, sin_signed):
        return impl(x, cos_full, sin_signed)

    def fwd(x, cos_full, sin_signed):
        return impl(x, cos_full, sin_signed), (cos_full, sin_signed)

    def bwd(res, g):
        cos_full, sin_signed = res
        return impl(g, cos_full, -sin_signed), jnp.zeros_like(cos_full), jnp.zeros_like(sin_signed)

    rope.defvjp(fwd, bwd)
    return rope


_rope32 = _make_rope(32)
_rope64 = _make_rope(64)


def _rms(x, w):
    return x * lax.rsqrt(jnp.mean(x * x, axis=-1, keepdims=True) + NORM_EPS) * w


def _col(v, lane_index):
    lane = lax.broadcasted_iota(jnp.int32, v.shape, 1)
    return jnp.sum(jnp.where(lane == lane_index, v, 0.0), axis=1, keepdims=True)


def _select_typed(tvals, step, tm, m_ctx):
    row = step * tm + lax.broadcasted_iota(jnp.int32, (tm, 1), 0)
    return [jnp.where(row >= m_ctx, tv[1], tv[0]) for tv in tvals]


def _row_tile(n_rows):
    return _pick(n_rows, (ROW_TILE_TALL, ROW_TILE))


def rowwise_fwd(name, f, rows, typed, shared, outs, n_rows, m_ctx):
    tm = _row_tile(n_rows)
    nr, nt = len(rows), len(typed)
    nin = nr + nt + len(shared)

    def body(*refs):
        vals = [r[...] for r in refs[:nin]]
        sel = _select_typed(vals[nr:nr + nt], pl.program_id(0), tm, m_ctx)
        res = f(*vals[:nr], *sel, *vals[nr + nt:])
        for o_ref, o in zip(refs[nin:], res):
            o_ref[...] = o.astype(o_ref.dtype)

    arrs, specs = [], []
    for it in rows:
        a, s = _row_in(it, tm)
        arrs.append(a)
        specs.append(s)
    for t in typed:
        arrs.append(t)
        specs.append(_const_spec(t.shape))
    for s_ in shared:
        arrs.append(s_)
        specs.append(_const_spec(s_.shape))
    res = pl.pallas_call(
        body, name=name, grid=(n_rows // tm,), in_specs=specs,
        out_specs=[pl.BlockSpec((tm, w), lambda i: (i, 0)) for w, _ in outs],
        out_shape=[jax.ShapeDtypeStruct((n_rows, w), dt) for w, dt in outs],
        compiler_params=_cp(("parallel",)))(*arrs)
    return res


def rowwise_bwd(name, f, rows, typed, shared, cots, row_diff, shared_diff, drow_dtypes, n_rows, m_ctx, tall=True):
    tm = _row_tile(n_rows) if tall else min(ROW_TILE, n_rows)
    cot_groups = [c_ if isinstance(c_, tuple) else (c_,) for c_ in cots]
    cots = [a for grp in cot_groups for a in grp]
    nr, nt, ns, nc = len(rows), len(typed), len(shared), len(cots)
    nin = nr + nt + ns
    d_rows = [k for k in range(nr) if row_diff[k]]
    d_sh = [k for k in range(ns) if shared_diff[k]]

    def body(*refs):
        rvals = [r[...] for r in refs[:nr]]
        tvals = [r[...] for r in refs[nr:nr + nt]]
        svals = [r[...] for r in refs[nr + nt:nin]]
        cparts = [r[...].astype(F32) for r in refs[nin:nin + nc]]
        cvals = []
        for grp in cot_groups:
            cvals.append(sum(cparts[1:len(grp)], cparts[0]))
            cparts = cparts[len(grp):]
        out_refs = refs[nin + nc:]
        i = pl.program_id(0)

        def g(*dv):
            dv = list(dv)
            rv = list(rvals)
            for k in d_rows:
                rv[k] = dv.pop(0)
            tv = _select_typed([dv.pop(0) for _ in range(nt)], i, tm, m_ctx)
            sv = list(svals)
            for k in d_sh:
                sv[k] = dv.pop(0)
            return tuple(o.astype(F32) for o in f(*rv, *tv, *sv))

        prim = [rvals[k].astype(F32) for k in d_rows] + tvals + [svals[k] for k in d_sh]
        _, vjp = jax.vjp(g, *prim)
        grads = list(vjp(tuple(cvals)))
        for ref in out_refs[:len(d_rows)]:
            ref[...] = grads.pop(0).astype(ref.dtype)
        for ref in out_refs[len(d_rows):]:
            gr = grads.pop(0)

            @pl.when(i == 0)
            def _(ref=ref, gr=gr):
                ref[...] = gr

            @pl.when(i != 0)
            def _(ref=ref, gr=gr):
                ref[...] += gr

    arrs, specs = [], []
    for it in list(rows):
        a, s = _row_in(it, tm)
        arrs.append(a)
        specs.append(s)
    for t in typed:
        arrs.append(t)
        specs.append(_const_spec(t.shape))
    for s_ in shared:
        arrs.append(s_)
        specs.append(_const_spec(s_.shape))
    for c_ in cots:
        a, s = _row_in(c_, tm)
        arrs.append(a)
        specs.append(s)
    out_specs, out_shape = [], []
    for k, dt in zip(d_rows, drow_dtypes):
        w = _width(rows[k])
        out_specs.append(pl.BlockSpec((tm, w), lambda i: (i, 0)))
        out_shape.append(jax.ShapeDtypeStruct((n_rows, w), dt))
    for t in typed:
        out_specs.append(_const_spec(t.shape))
        out_shape.append(jax.ShapeDtypeStruct(t.shape, F32))
    for k in d_sh:
        out_specs.append(_const_spec(shared[k].shape))
        out_shape.append(jax.ShapeDtypeStruct(shared[k].shape, F32))
    res = pl.pallas_call(body, name=name, grid=(n_rows // tm,), in_specs=specs, out_specs=out_specs,
                         out_shape=out_shape, compiler_params=_cp(("arbitrary",)))(*arrs)
    n1, n2 = len(d_rows), len(d_rows) + nt
    return list(res[:n1]), list(res[n1:n2]), list(res[n2:])


def _pick(n, prefs):
    for p in prefs:
        if n % p == 0:
            return p
    return n


def mm(name, a, b, out_dtype, transpose_b=False, epilogue=None, tile_ins=(), typed_ins=(), shared_ins=(),
       out_dtypes=None, nb_ctx=None, whole_rows=False):
    n, k = b.shape if transpose_b else b.shape[::-1]
    m = (a.arr if isinstance(a, Cols) else a).shape[0]
    assert _width(a) == k
    tm = min(ROW_TILE, m) if typed_ins else _pick(m, (MM_ROWS, 256))
    out_dtypes = out_dtypes or [out_dtype]
    tile_arrs = [x.arr if isinstance(x, Cols) else x for x in tile_ins]
    tile_bytes = sum(jnp.dtype(d).itemsize for d in out_dtypes) + sum(x.dtype.itemsize for x in tile_arrs)
    fits = lambda c: 2 * (tm * k * 2 + k * c * 2 + tm * c * tile_bytes) <= MM_VMEM_BUDGET
    tn = next(c for c in (2560, 2048, 1536, 1024, 512, 256, 128, n) if n % c == 0 and fits(c) or c == n)
    if whole_rows:
        assert fits(n)
        tn = n
    dims = _NT if transpose_b else (((1,), (0,)), ((), ()))
    n_ex = len(tile_ins) + len(typed_ins) + len(shared_ins)

    n_tile, n_typed = len(tile_ins), len(typed_ins)

    def body(a_ref, b_ref, *refs):
        prod = lax.dot_general(a_ref[...], b_ref[...], dims, preferred_element_type=F32)
        ex = [r[...] for r in refs[:n_ex]]
        ex[n_tile:n_tile + n_typed] = _select_typed(ex[n_tile:n_tile + n_typed], pl.program_id(1), tm, nb_ctx)
        outs = (prod,) if epilogue is None else epilogue(prod, *ex)
        for o_ref, o in zip(refs[n_ex:], outs):
            o_ref[...] = o.astype(o_ref.dtype)

    a_arr, a_spec = _row_in(a, tm)
    a_spec = pl.BlockSpec(a_spec.block_shape, lambda j, i, f=a_spec.index_map: f(i))
    b_spec = pl.BlockSpec((tn, k), lambda j, i: (j, 0)) if transpose_b else pl.BlockSpec((k, tn), lambda j, i: (0, j))
    tile = pl.BlockSpec((tm, tn), lambda j, i: (i, j))
    ex_specs = []
    for x in tile_ins:
        base = x.off // tn if isinstance(x, Cols) else 0
        assert not isinstance(x, Cols) or (x.off % tn == 0 and x.width == n)
        ex_specs.append(pl.BlockSpec((tm, tn), lambda j, i, base=base: (i, base + j)))
    ex_specs += [pl.BlockSpec((2, 1, tn), lambda j, i: (0, 0, j))] * len(typed_ins)
    ex_specs += [pl.BlockSpec((1, tn), lambda j, i: (0, j))] * len(shared_ins)
    res = pl.pallas_call(
        body, name=name, grid=(n // tn, m // tm), in_specs=[a_spec, b_spec] + ex_specs,
        out_specs=[tile] * len(out_dtypes),
        out_shape=[jax.ShapeDtypeStruct((m, n), d) for d in out_dtypes],
        compiler_params=_cp(("parallel", "parallel")))(a_arr, b, *tile_arrs, *typed_ins, *shared_ins)
    return res[0] if epilogue is None else res


def mm_tn(name, a, b, out_dtype=F32, pieces=None):
    t = (a.arr if isinstance(a, Cols) else a).shape[0]
    k, n = _width(a), _width(b)
    tt = _pick(t, (MM_TN_ROWS, MM_ROWS, 256))
    k_unit = pieces[1] if pieces and pieces[0] == "rows" else k
    n_unit = pieces[1] if pieces and pieces[0] == "cols" else n
    tk = _pick(k_unit, (1024, 512, 256, 128))
    tn = _pick(n_unit, (1280, 1024, 512, 256, 128))
    n_t = t // tt

    def body(a_ref, b_ref, o_ref, acc):
        part = lax.dot_general(a_ref[...], b_ref[...], _TN, preferred_element_type=F32)
        ti = pl.program_id(2)

        @pl.when(ti == 0)
        def _():
            acc[...] = part

        @pl.when(ti != 0)
        def _():
            acc[...] += part

        @pl.when(ti == n_t - 1)
        def _():
            o_ref[...] = acc[...].astype(o_ref.dtype)

    def win(item, width):
        if isinstance(item, Cols):
            assert item.off % width == 0
            return item.arr, item.off // width
        return item, 0

    a_arr, a0 = win(a, tk)
    b_arr, b0 = win(b, tn)
    if pieces is None:
        out_spec = pl.BlockSpec((tk, tn), lambda ki, ni, ti: (ki, ni))
        out_shape = (k, n)
    elif pieces[0] == "cols":
        per = n_unit // tn
        out_spec = pl.BlockSpec((None, tk, tn), lambda ki, ni, ti: (ni // per, ki, ni % per))
        out_shape = (4, k, n_unit)
    else:
        per = k_unit // tk
        out_spec = pl.BlockSpec((None, tk, tn), lambda ki, ni, ti: (ki // per, ki % per, ni))
        out_shape = (4, k_unit, n)
    return pl.pallas_call(
        body, name=name, grid=(k // tk, n // tn, n_t),
        in_specs=[pl.BlockSpec((tt, tk), lambda ki, ni, ti: (ti, a0 + ki)),
                  pl.BlockSpec((tt, tn), lambda ki, ni, ti: (ti, b0 + ni))],
        out_specs=out_spec, out_shape=jax.ShapeDtypeStruct(out_shape, out_dtype),
        scratch_shapes=[pltpu.VMEM((tk, tn), F32)],
        compiler_params=_cp(("parallel", "parallel", "arbitrary")))(a_arr, b_arr)


def _heads_t(rows_blk):
    blk = rows_blk.astype(F32).T
    return jnp.concatenate([blk[hh * ATTN_HD:(hh + 1) * ATTN_HD, :] for hh in range(4)], axis=1)


def _heads_rows(t_blk):
    tq = t_blk.shape[1] // 4
    return jnp.concatenate([t_blk[:, hh * tq:(hh + 1) * tq] for hh in range(4)], axis=0).T


def attn_fwd(name, q, kk, vT_ones, m_ctx, tq):
    t, hd, hd_ext = kk.shape[1], ATTN_HD, vT_ones.shape[1]
    nq, r = t // tq, 4 * tq
    tk = _pick(t - m_ctx, (ATTN_TK_FWD, ATTN_TK))
    nqc, n_lat_tiles = m_ctx // tq, (t - m_ctx) // tk

    def body(q_ref, k_ref, vT_ref, o_ref, qT_ref, oT_ref, lse_ref):
        i = pl.program_id(1)
        q_t = _heads_t(q_ref[...]).astype(CDT)
        qT_ref[...] = q_t

        def tile(off, size, carry):
            mi, acc = carry
            sub = min(size, ATTN_TK)
            offs = [off + u * sub for u in range(size // sub)]
            sts = [jnp.dot(k_ref[pl.ds(o, sub), :], q_t, preferred_element_type=F32) for o in offs]
            for o, st in zip(offs, sts):
                mn = jnp.maximum(mi, jnp.max(st, axis=0, keepdims=True))
                pt = jnp.exp2(st - mn)
                acc = jnp.exp2(mi - mn) * acc + jnp.dot(vT_ref[:, pl.ds(o, sub)], pt.astype(CDT),
                                                        preferred_element_type=F32)
                mi = mn
            return mi, acc

        carry = tile(0, m_ctx, (jnp.full((1, r), -1e30, F32), jnp.zeros((hd_ext, r), F32)))
        mi, acc = lax.fori_loop(
            0, jnp.where(i < nqc, 0, n_lat_tiles),
            lambda j, cr: tile(pl.multiple_of(m_ctx + j * tk, ATTN_TK), tk, cr), carry)
        li = acc[hd:hd + 1]
        o_t = acc[:hd] / li
        oT_ref[...] = o_t.astype(oT_ref.dtype)
        o_ref[...] = _heads_rows(o_t).astype(o_ref.dtype)
        lse_ref[...] = mi + jnp.log2(li)

    blk_t = pl.BlockSpec((None, None, hd, r), lambda g, i: (g, i, 0, 0))
    rows = pl.BlockSpec((tq, 4 * hd), lambda g, i: (i, g))
    return pl.pallas_call(
        body, name=name, grid=(2, nq),
        in_specs=[rows, pl.BlockSpec((None, t, hd), lambda g, i: (g, 0, 0)),
                  pl.BlockSpec((None, hd_ext, t), lambda g, i: (g, 0, 0))],
        out_specs=[rows, blk_t, blk_t, pl.BlockSpec((None, None, 1, r), lambda g, i: (g, i, 0, 0))],
        out_shape=[jax.ShapeDtypeStruct((t, 8 * hd), CDT), jax.ShapeDtypeStruct((2, nq, hd, r), CDT),
                   jax.ShapeDtypeStruct((2, nq, hd, r), CDT), jax.ShapeDtypeStruct((2, nq, 1, r), F32)],
        compiler_params=_cp(("parallel", "arbitrary")))(q, kk, vT_ones)


def attn_bwd(name, qT, do, oT, lse, kk, kT, vv, m_ctx):
    _, nq, hd, r = qT.shape
    t = kk.shape[1]
    tq = r // 4
    tk = _pick(t - m_ctx, (ATTN_TK_BWD, ATTN_TK))
    nqc, n_lat_tiles = m_ctx // tq, (t - m_ctx) // tk

    def body(qT_ref, do_ref, oT_ref, lse_ref, k_ref, kT_ref, v_ref, dq_ref, dk_ref, dv_ref):
        i = pl.program_id(1)

        @pl.when(i == 0)
        def _():
            dk_ref[...] = jnp.zeros_like(dk_ref)
            dv_ref[...] = jnp.zeros_like(dv_ref)

        q_t = qT_ref[...]
        do_f = _heads_t(do_ref[...])
        do_t = do_f.astype(CDT)
        lse = lse_ref[...]
        delta = jnp.sum(do_f * oT_ref[...].astype(F32), axis=0, keepdims=True)

        def tile(off, size, dq):
            sub = min(size, ATTN_TK)
            offs = [off + u * sub for u in range(size // sub)]
            sts = [jnp.dot(k_ref[pl.ds(o, sub), :], q_t, preferred_element_type=F32) for o in offs]
            dpts = [jnp.dot(v_ref[pl.ds(o, sub), :], do_t, preferred_element_type=F32) for o in offs]
            for o, st, dpt in zip(offs, sts, dpts):
                pt = jnp.exp2(st - lse)
                dv_ref[pl.ds(o, sub), :] += lax.dot_general(pt.astype(CDT), do_t, _NT, preferred_element_type=F32)
                dst = (pt * (dpt - delta)).astype(CDT)
                dk_ref[pl.ds(o, sub), :] += lax.dot_general(dst, q_t, _NT, preferred_element_type=F32)
                dq = dq + jnp.dot(kT_ref[:, pl.ds(o, sub)], dst, preferred_element_type=F32)
            return dq

        dq = tile(0, m_ctx, jnp.zeros((hd, r), F32))
        dq = lax.fori_loop(0, jnp.where(i < nqc, 0, n_lat_tiles),
                           lambda j, acc: tile(pl.multiple_of(m_ctx + j * tk, ATTN_TK), tk, acc), dq)
        dq_ref[...] = _heads_rows(dq * LN2)

    blk_t = pl.BlockSpec((None, None, hd, r), lambda g, i: (g, i, 0, 0))
    row = pl.BlockSpec((None, None, 1, r), lambda g, i: (g, i, 0, 0))
    kv = pl.BlockSpec((None, t, hd), lambda g, i: (g, 0, 0))
    rows = pl.BlockSpec((tq, 4 * hd), lambda g, i: (i, g))
    return pl.pallas_call(
        body, name=name, grid=(2, nq),
        in_specs=[blk_t, rows, blk_t, row, kv, pl.BlockSpec((None, hd, t), lambda g, i: (g, 0, 0)), kv],
        out_specs=[rows, kv, kv],
        out_shape=[jax.ShapeDtypeStruct((t, 8 * hd), F32), jax.ShapeDtypeStruct(kk.shape, F32),
                   jax.ShapeDtypeStruct(kk.shape, F32)],
        compiler_params=_cp(("parallel", "arbitrary")))(qT, do, oT, lse, kk, kT, vv)


def _split_kv(a):
    return a.reshape(a.shape[0], 2, ATTN_HD).transpose(1, 0, 2)


def _merge_kv(a):
    return a.transpose(1, 0, 2).reshape(a.shape[1], 2 * ATTN_HD)


def _chunk_order(rev, ncc, nct):
    if not rev:
        return lambda s: s
    return lambda s: jnp.where(s < ncc, ncc - 1 - s, nct - 1 - (s - ncc))


def scan_fwd(name, make_fn, rows, shared, n_state, y_width, n_rows, m_ctx):
    nct, ncc = n_rows // CHUNK, m_ctx // CHUNK
    orders = [_chunk_order(rev, ncc, nct) for rev in (False, True)]
    fns = [make_fn(0), make_fn(1)]
    nr, ns = len(rows), len(shared)

    def body(*refs):
        svals = [r[...] for r in refs[2 * nr:2 * nr + ns]]
        y_refs, sin_refs, st = refs[2 * nr + ns:2 * nr + ns + 2], refs[2 * nr + ns + 2:2 * nr + ns + 4], refs[-1]

        @pl.when(pl.program_id(0) == 0)
        def _():
            st[...] = jnp.zeros_like(st)

        for d in range(2):
            rvals = [r[...] for r in refs[d * nr:(d + 1) * nr]]
            prev = [st[d, k] for k in range(n_state)]
            sin_refs[d][...] = st[d]
            y, new = fns[d](rvals, svals, prev)
            y_refs[d][...] = y
            for k in range(n_state):
                st[d, k] = new[k]

    arrs, specs = [], []
    for order in orders:
        for it in rows:
            a, s = _row_in(it, CHUNK, order)
            arrs.append(a)
            specs.append(s)
    for s_ in shared:
        arrs.append(s_)
        specs.append(_const_spec(s_.shape))
    return pl.pallas_call(
        body, name=name, grid=(nct,), in_specs=specs,
        out_specs=[pl.BlockSpec((CHUNK, y_width), lambda s, o=o: (o(s), 0)) for o in orders]
        + [pl.BlockSpec((None, n_state, LANES, LANES), lambda s, o=o: (o(s), 0, 0, 0)) for o in orders],
        out_shape=[jax.ShapeDtypeStruct((n_rows, y_width), F32)] * 2
        + [jax.ShapeDtypeStruct((nct, n_state, LANES, LANES), F32)] * 2,
        scratch_shapes=[pltpu.VMEM((2, n_state, LANES, LANES), F32)],
        compiler_params=_cp(("arbitrary",)))(*arrs)


def scan_bwd(name, make_fn, rows, shared, states_in, dy, post, outs, n_state, n_rows, m_ctx, dirs=(0, 1)):
    nct, ncc = n_rows // CHUNK, m_ctx // CHUNK
    orders = [(lambda r, f=_chunk_order(d == 1, ncc, nct): f(nct - 1 - r)) for d in dirs]
    fns = [make_fn(d) for d in dirs]
    nd = len(dirs)
    nr, ns, no = len(rows), len(shared), len(outs)
    n_in = nd * nr + ns

    def body(*refs):
        svals = [r[...] for r in refs[nd * nr:n_in]]
        sin_refs, dy_refs = refs[n_in:n_in + nd], refs[n_in + nd:n_in + 2 * nd]
        out_refs = refs[n_in + 2 * nd:n_in + 2 * nd + nd * no]
        dsh_refs = refs[n_in + 2 * nd + nd * no:-1]
        dst = refs[-1]
        r = pl.program_id(0)

        @pl.when(r == 0)
        def _():
            dst[...] = jnp.zeros_like(dst)

        d_shared = None
        for d in range(nd):
            rvals = [x[...] for x in refs[d * nr:(d + 1) * nr]]
            prev = [sin_refs[d][k] for k in range(n_state)]
            _, vjp = jax.vjp(fns[d], rvals, svals, prev)
            d_rows, d_sh, d_prev = vjp((dy_refs[d][...], [dst[d, k] for k in range(n_state)]))
            for ref, val in zip(out_refs[d * no:(d + 1) * no], post(d_rows)):
                ref[...] = val.astype(ref.dtype)
            d_shared = d_sh if d_shared is None else [a + b for a, b in zip(d_shared, d_sh)]
            for k in range(n_state):
                dst[d, k] = d_prev[k]
        for ref, gr in zip(dsh_refs, d_shared):
            @pl.when(r == 0)
            def _(ref=ref, gr=gr):
                ref[...] = gr

            @pl.when(r != 0)
            def _(ref=ref, gr=gr):
                ref[...] += gr

    arrs, specs = [], []
    for order in orders:
        for it in rows:
            a, s = _row_in(it, CHUNK, order)
            arrs.append(a)
            specs.append(s)
    for s_ in shared:
        arrs.append(s_)
        specs.append(_const_spec(s_.shape))
    for sin, order in zip(states_in, orders):
        arrs.append(sin)
        specs.append(pl.BlockSpec((None, n_state, LANES, LANES), lambda r, o=order: (o(r), 0, 0, 0)))
    for order in orders:
        a, s = _row_in(dy, CHUNK, order)
        arrs.append(a)
        specs.append(s)
    out_specs = [pl.BlockSpec((CHUNK, w), lambda r, o=o: (o(r), 0)) for o in orders for w, _ in outs]
    out_shape = [jax.ShapeDtypeStruct((n_rows, w), dt) for _ in orders for w, dt in outs]
    for s_ in shared:
        out_specs.append(_const_spec(s_.shape))
        out_shape.append(jax.ShapeDtypeStruct(s_.shape, F32))
    res = pl.pallas_call(body, name=name, grid=(nct,), in_specs=specs, out_specs=out_specs, out_shape=out_shape,
                         scratch_shapes=[pltpu.VMEM((nd, n_state, LANES, LANES), F32)],
                         compiler_params=_cp(("arbitrary",)))(*arrs)
    return [list(res[d * no:(d + 1) * no]) for d in range(nd)] + [list(res[nd * no:])]


def _make_ssd_chunk(direction):
    rev = direction == 1
    base = 8 * direction

    def fn(rows, shared, prev):
        xs, bms, cms, dtraw = rows[0:4], rows[4:6], rows[6:8], rows[8]
        dt_bias, a_log = shared
        ln = dtraw.shape[0]
        dt_all = _softplus(dtraw + dt_bias)
        a_all = dt_all * (-jnp.exp(a_log))
        r_i = lax.broadcasted_iota(jnp.int32, (ln, ln), 0)
        c_i = lax.broadcasted_iota(jnp.int32, (ln, ln), 1)
        tri = (r_i <= c_i) if rev else (r_i >= c_i)
        a_cum_all = jnp.dot(tri.astype(F32), a_all, precision=lax.Precision.HIGHEST, preferred_element_type=F32)
        a_tot_all = jnp.sum(a_all, axis=0, keepdims=True)
        first = lax.broadcasted_iota(jnp.int32, (ln, LANES), 1) < SSD_HD
        first_row = lax.broadcasted_iota(jnp.int32, (LANES, 1), 0) < SSD_HD

        def lmat(acol):
            a_b = jnp.broadcast_to(acol, (ln, ln))
            seg = a_b - a_b.T
            return jnp.where(tri, jnp.exp(jnp.where(tri, seg, 0.0)), 0.0)

        ys, new = [], []
        for g in range(2):
            bm, cm = bms[g], cms[g]
            cb = _mxu(cm, bm, _NT)
            for jj in range(2):
                pr = 2 * g + jj
                h0, h1 = base + 2 * pr, base + 2 * pr + 1
                ac0, ac1 = _col(a_cum_all, h0), _col(a_cum_all, h1)
                at0, at1 = _col(a_tot_all, h0), _col(a_tot_all, h1)
                dt_pair = jnp.where(first, _col(dt_all, h0), _col(dt_all, h1))
                acum_pair = jnp.where(first, ac0, ac1)
                atot_pair = jnp.where(first[0:1], at0, at1)
                xd = xs[pr] * dt_pair
                st = _mxu(xd * jnp.exp(atot_pair - acum_pair), bm, _TN)
                new.append(prev[pr] * jnp.where(first_row, jnp.exp(at0), jnp.exp(at1)) + st)
                y0 = _mxu(cb * lmat(ac0), xd)
                y1 = _mxu(cb * lmat(ac1), xd)
                y_off = _mxu(cm, prev[pr], _NT) * jnp.exp(acum_pair)
                ys.append(jnp.where(first, y0, y1) + y_off)
        return jnp.concatenate(ys, axis=1), new

    return fn


def _make_ret_chunk(direction):
    rev = direction == 1
    base = 4 * direction

    def fn(rows, shared, prev):
        qs, ks, vs = rows[0:4], rows[4:8], rows[8:12]
        lg_all = -jnp.exp(shared[0])
        ln = qs[0].shape[0]
        pos = lax.broadcasted_iota(jnp.int32, (ln, 1), 0).astype(F32)
        r_i = lax.broadcasted_iota(jnp.int32, (ln, ln), 0)
        c_i = lax.broadcasted_iota(jnp.int32, (ln, ln), 1)
        diff = ((c_i - r_i) if rev else (r_i - c_i))
        mask = diff >= 0
        dpos = jnp.maximum(diff, 0).astype(F32)
        k_pow = pos if rev else (ln - 1.0 - pos)
        q_pow = (ln - pos) if rev else (pos + 1.0)
        ys, new = [], []
        for h in range(RET_HEADS):
            lg = _col(lg_all, base + h)
            dmat = jnp.where(mask, jnp.exp(dpos * lg), 0.0)
            st = _mxu(ks[h] * jnp.exp(k_pow * lg), vs[h], _TN)
            new.append(prev[h] * jnp.exp(ln * lg) + st)
            s = _mxu(qs[h], ks[h], _NT) * dmat
            ys.append(_mxu(s, vs[h]) + _mxu(qs[h], prev[h]) * jnp.exp(q_pow * lg))
        return jnp.concatenate(ys, axis=1), new

    return fn


def _conv_pre(x, w, b, t_idx, n_rows, m_ctx):
    is_start = (t_idx == 0) | (t_idx == m_ctx)
    is_end = (t_idx == m_ctx - 1) | (t_idx == n_rows - 1)
    xp = jnp.where(is_start, 0.0, pltpu.roll(x, 1, axis=0))
    xn = jnp.where(is_end, 0.0, pltpu.roll(x, n_rows - 1, axis=0))
    return w[0:1] * xp + w[1:2] * x + w[2:3] * xn + b, xp, xn, is_start, is_end


def conv_fwd(x, conv_w, conv_b, m_ctx):
    n_rows, width = x.arr.shape[0], x.width
    c0 = x.off // LANES

    def body(x_ref, w_ref, b_ref, o_ref):
        t_idx = lax.broadcasted_iota(jnp.int32, (n_rows, 1), 0)
        pre = _conv_pre(x_ref[...], w_ref[...], b_ref[...], t_idx, n_rows, m_ctx)[0]
        o_ref[...] = pre * jax.nn.sigmoid(pre)

    return pl.pallas_call(
        body, name="conv_fwd", grid=(width // LANES,),
        in_specs=[pl.BlockSpec((n_rows, LANES), lambda c: (0, c0 + c)),
                  pl.BlockSpec((3, LANES), lambda c: (0, c)), pl.BlockSpec((1, LANES), lambda c: (0, c))],
        out_specs=pl.BlockSpec((n_rows, LANES), lambda c: (0, c)),
        out_shape=jax.ShapeDtypeStruct((n_rows, width), F32),
        compiler_params=_cp(("parallel",)))(x.arr, conv_w, conv_b)


def conv_bwd(x, conv_w, conv_b, dy_a, dy_b, dxs_extra, m_ctx):
    n_rows, width = x.arr.shape[0], x.width
    c0 = x.off // LANES
    n_extra = dxs_extra.shape[1] // LANES

    def body(x_ref, w_ref, b_ref, dya_ref, dyb_ref, ex_ref, dx_ref, dw_ref, db_ref):
        c = pl.program_id(0)
        t_idx = lax.broadcasted_iota(jnp.int32, (n_rows, 1), 0)
        w = w_ref[...]
        pre, xp, xn, is_start, is_end = _conv_pre(x_ref[...], w, b_ref[...], t_idx, n_rows, m_ctx)
        sg = jax.nn.sigmoid(pre)
        dyv = dya_ref[...] + dyb_ref[...] + jnp.where(c < n_extra, ex_ref[...], 0.0)
        dpre = dyv * (sg * (1.0 + pre * (1.0 - sg)))
        d_next = jnp.where(is_end, 0.0, pltpu.roll(dpre, n_rows - 1, axis=0))
        d_prev = jnp.where(is_start, 0.0, pltpu.roll(dpre, 1, axis=0))
        dx_ref[...] = (w[1:2] * dpre + w[0:1] * d_next + w[2:3] * d_prev).astype(dx_ref.dtype)
        dw_ref[...] = jnp.concatenate([jnp.sum(dpre * xp, axis=0, keepdims=True),
                                       jnp.sum(dpre * x_ref[...], axis=0, keepdims=True),
                                       jnp.sum(dpre * xn, axis=0, keepdims=True)], axis=0)
        db_ref[...] = jnp.sum(dpre, axis=0, keepdims=True)

    return pl.pallas_call(
        body, name="conv_bwd", grid=(width // LANES,),
        in_specs=[pl.BlockSpec((n_rows, LANES), lambda c: (0, c0 + c)),
                  pl.BlockSpec((3, LANES), lambda c: (0, c)), pl.BlockSpec((1, LANES), lambda c: (0, c)),
                  pl.BlockSpec((n_rows, LANES), lambda c: (0, c)), pl.BlockSpec((n_rows, LANES), lambda c: (0, c)),
                  pl.BlockSpec((n_rows, LANES), lambda c: (0, jnp.minimum(c, n_extra - 1)))],
        out_specs=[pl.BlockSpec((n_rows, LANES), lambda c: (0, c)),
                   pl.BlockSpec((3, LANES), lambda c: (0, c)), pl.BlockSpec((1, LANES), lambda c: (0, c))],
        out_shape=[jax.ShapeDtypeStruct((n_rows, width), CDT), jax.ShapeDtypeStruct((3, width), F32),
                   jax.ShapeDtypeStruct((1, width), F32)],
        compiler_params=_cp(("parallel",)))(x.arr, conv_w, conv_b, dy_a, dy_b, dxs_extra)


def loss_head(h, target, final_w, m_ctx):
    n_rows, d = h.shape
    tm = min(ROW_TILE, n_rows)
    nb_ctx = m_ctx // tm

    def f(hb, w, tgt):
        err = _rms(hb, w) - tgt
        return 0.5 * jnp.sum(jnp.mean(err * err, axis=-1))

    def body(h_ref, t_ref, w_ref, loss_ref, dh_ref, dw_ref):
        i = pl.program_id(0)

        @pl.when(i < nb_ctx)
        def _():
            dh_ref[...] = jnp.zeros_like(dh_ref)

        @pl.when(i == 0)
        def _():
            loss_ref[...] = jnp.zeros_like(loss_ref)
            dw_ref[...] = jnp.zeros_like(dw_ref)

        @pl.when(i >= nb_ctx)
        def _():
            val, vjp = jax.vjp(lambda hb, w: f(hb, w, t_ref[...]), h_ref[...], w_ref[...])
            dh, dw = vjp(jnp.ones((), F32))
            dh_ref[...] = dh
            dw_ref[...] += dw
            loss_ref[...] += jnp.broadcast_to(val, loss_ref.shape)

    return pl.pallas_call(
        body, name="loss_head", grid=(n_rows // tm,),
        in_specs=[pl.BlockSpec((tm, d), lambda i: (i, 0)),
                  pl.BlockSpec((tm, d), lambda i: (jnp.maximum(i - nb_ctx, 0), 0)), _const_spec((1, d))],
        out_specs=[_const_spec((1, LANES)), pl.BlockSpec((tm, d), lambda i: (i, 0)), _const_spec((1, d))],
        out_shape=[jax.ShapeDtypeStruct((1, LANES), F32), jax.ShapeDtypeStruct((n_rows, d), F32),
                   jax.ShapeDtypeStruct((1, d), F32)],
        compiler_params=_cp(("arbitrary",)))(h, target, final_w)


def adamw(name, w, m, v, g_parts):
    lead, rows, cols = w.shape
    tr = _pick(rows, (256, 128, 64, 32, 16, 8))
    npart = len(g_parts)
    c1 = 1.0 - ADAM_B1 ** ADAM_STEP
    c2 = 1.0 - ADAM_B2 ** ADAM_STEP

    def body(*refs):
        w_ref, m_ref, v_ref = refs[:3]
        g = refs[3][...].astype(F32)
        for r in refs[4:3 + npart]:
            g = g + r[...].astype(F32)
        g_ref, d_ref, nm_ref, nv_ref = refs[3 + npart:]
        nm = ADAM_B1 * m_ref[...] + (1.0 - ADAM_B1) * g
        nv = ADAM_B2 * v_ref[...] + (1.0 - ADAM_B2) * (g * g)
        g_ref[...] = g
        nm_ref[...] = nm
        nv_ref[...] = nv
        d_ref[...] = -ADAM_LR * ((nm / c1) / (jnp.sqrt(nv / c2) + ADAM_EPS) + ADAM_WD * w_ref[...])

    spec = pl.BlockSpec((None, tr, cols), lambda l, i: (l, i, 0))
    return pl.pallas_call(
        body, name=name, grid=(lead, rows // tr), in_specs=[spec] * (3 + npart), out_specs=[spec] * 4,
        out_shape=[jax.ShapeDtypeStruct(w.shape, F32)] * 4,
        compiler_params=_cp(("parallel", "parallel")))(w, m, v, *g_parts)


MESH = pl.DeviceIdType.MESH
_HBM = pl.BlockSpec(memory_space=pl.ANY)


def _chip_peers():
    x, y, c = lax.axis_index("x"), lax.axis_index("y"), lax.axis_index("c")
    return x, y, c, [(1 - x, y), (x, 1 - y), (1 - x, 1 - y)]


def _window(ref, kind, chip, rows, cols):
    if kind == "cols":
        return ref.at[:, pl.ds(pl.multiple_of(chip * cols, LANES), cols)]
    if kind == "rows":
        return ref.at[pl.ds(pl.multiple_of(chip * rows, 8), rows), :]
    return ref.at[chip]


def _gathered_shape(kind, rows, cols):
    return {"cols": (rows, 4 * cols), "rows": (4 * rows, cols), "slices": (4, rows, cols)}[kind]


def gather_layers(name, shards, kinds):
    n = len(shards)

    def body(*refs):
        x_refs, o_refs = refs[:n], refs[n:2 * n]
        send_sems, recv_sems, local_sems = refs[2 * n:]
        x, y, c, peers = _chip_peers()
        me = 2 * x + y
        started = []
        for a in range(n):
            _, rows, cols = shards[a].shape
            src = x_refs[a].at[c]
            mine = pltpu.make_async_copy(src, _window(o_refs[a], kinds[a], me, rows, cols), local_sems.at[a])
            mine.start()
            started.append(mine.wait)
            for k, (px, py) in enumerate(peers):
                cp = pltpu.make_async_remote_copy(
                    src_ref=src, dst_ref=_window(o_refs[a], kinds[a], me, rows, cols), send_sem=send_sems.at[3 * a + k],
                    recv_sem=recv_sems.at[3 * a + k], device_id=(px, py, c), device_id_type=MESH)
                cp.start()
                started.append(cp.wait_send)
        for a in range(n):
            _, rows, cols = shards[a].shape
            for k, (px, py) in enumerate(peers):
                pltpu.make_async_remote_copy(
                    src_ref=x_refs[a].at[c], dst_ref=_window(o_refs[a], kinds[a], 2 * px + py, rows, cols),
                    send_sem=send_sems.at[3 * a + k], recv_sem=recv_sems.at[3 * a + k], device_id=(px, py, c),
                    device_id_type=MESH).wait_recv()
        for wait in started:
            wait()

    return pl.pallas_call(
        body, name=name, in_specs=[_HBM] * n, out_specs=[_HBM] * n,
        out_shape=[jax.ShapeDtypeStruct(_gathered_shape(kinds[a], *shards[a].shape[1:]), shards[a].dtype)
                   for a in range(n)],
        scratch_shapes=[pltpu.SemaphoreType.DMA((3 * n,)), pltpu.SemaphoreType.DMA((3 * n,)),
                        pltpu.SemaphoreType.DMA((n,))],
        )(*shards)


def scatter_pieces(name, pieces):
    n = len(pieces)

    def body(*refs):
        p_refs, o_refs = refs[:n], refs[n:2 * n]
        send_sems, recv_sems, local_sems = refs[2 * n:]
        x, y, c, peers = _chip_peers()
        me = 2 * x + y
        started = []
        for a in range(n):
            mine = pltpu.make_async_copy(p_refs[a].at[me], o_refs[a].at[me], local_sems.at[a])
            mine.start()
            started.append(mine.wait)
            for k, (px, py) in enumerate(peers):
                cp = pltpu.make_async_remote_copy(
                    src_ref=p_refs[a].at[2 * px + py], dst_ref=o_refs[a].at[me], send_sem=send_sems.at[3 * a + k],
                    recv_sem=recv_sems.at[3 * a + k], device_id=(px, py, c), device_id_type=MESH)
                cp.start()
                started.append(cp.wait_send)
        for a in range(n):
            for k, (px, py) in enumerate(peers):
                pltpu.make_async_remote_copy(
                    src_ref=p_refs[a].at[me], dst_ref=o_refs[a].at[2 * px + py], send_sem=send_sems.at[3 * a + k],
                    recv_sem=recv_sems.at[3 * a + k], device_id=(px, py, c), device_id_type=MESH).wait_recv()
        for wait in started:
            wait()

    return pl.pallas_call(
        body, name=name, in_specs=[_HBM] * n, out_specs=[_HBM] * n,
        out_shape=[jax.ShapeDtypeStruct(p.shape, p.dtype) for p in pieces],
        scratch_shapes=[pltpu.SemaphoreType.DMA((3 * n,)), pltpu.SemaphoreType.DMA((3 * n,)),
                        pltpu.SemaphoreType.DMA((n,))],
        )(*pieces)


def _pair_step(n_steps, x_ref, land, send_sems, recv_sems, credits, consume):
    x, y, c = lax.axis_index("x"), lax.axis_index("y"), lax.axis_index("c")
    sib = (x, y, 1 - c)
    i = pl.program_id(0)
    slot = i % 2

    @pl.when(i >= 2)
    def _():
        pl.semaphore_wait(credits.at[slot], 1)

    cp = pltpu.make_async_remote_copy(src_ref=x_ref, dst_ref=land.at[slot], send_sem=send_sems.at[slot],
                                      recv_sem=recv_sems.at[slot], device_id=sib, device_id_type=MESH)
    cp.start()
    cp.wait_recv()
    consume(land[slot])

    @pl.when(i < n_steps - 2)
    def _():
        pl.semaphore_signal(credits.at[slot], inc=1, device_id=sib, device_id_type=MESH)

    cp.wait_send()


def _pair_call(name, body, n_steps, in_specs, out_spec, out_shape, blk_shape, dtype, operands, extra_scratch=()):
    grid_spec = pltpu.PrefetchScalarGridSpec(
        num_scalar_prefetch=1, grid=(n_steps,), in_specs=in_specs, out_specs=out_spec,
        scratch_shapes=[pltpu.VMEM((2,) + blk_shape, dtype), pltpu.SemaphoreType.DMA((2,)),
                        pltpu.SemaphoreType.DMA((2,)), pltpu.SemaphoreType.REGULAR((2,)), *extra_scratch])
    return pl.pallas_call(body, name=name, grid_spec=grid_spec, out_shape=out_shape,
                          compiler_params=_cp(("arbitrary",)))(*operands)


def _place():
    return jnp.stack([lax.axis_index("x"), lax.axis_index("y"), lax.axis_index("c")]).astype(jnp.int32)


def _pair_rows(rows, row_bytes):
    for cand in (4096, 2048, 1024, 768, 512, 384, 256, 192, 128, 96, 64, 48, 32, 16):
        if rows % cand == 0 and cand * row_bytes <= PAIR_BLOCK_BYTES:
            return cand
    return _pick(rows, (16, 8))


def exchange_both(name, mine):
    rows, cols = mine.shape
    tr = _pair_rows(rows, cols * mine.dtype.itemsize)
    n_steps = rows // tr

    def body(s_ref, x_ref, o_ref, land, send_sems, recv_sems, credits):
        c = lax.axis_index("c")
        o_ref[c] = x_ref[...]

        def consume(v):
            o_ref[1 - c] = v
        _pair_step(n_steps, x_ref, land, send_sems, recv_sems, credits, consume)

    return _pair_call(name, body, n_steps, [pl.BlockSpec((tr, cols), lambda i, s: (i, 0))],
                      pl.BlockSpec((2, tr, cols), lambda i, s: (0, i, 0)),
                      jax.ShapeDtypeStruct((2, rows, cols), mine.dtype), (tr, cols), mine.dtype, (_place(), mine))


def exchange_add(name, layer0, layer1):
    rows, cols = layer0.shape
    tr = _pair_rows(rows, cols * layer0.dtype.itemsize)
    nb = rows // tr

    def body(s_ref, l0_ref, l1_ref, o_ref, land, send_sems, recv_sems, credits, send_buf):
        first = lax.axis_index("c") == 0
        send_buf[...] = jnp.where(first, l1_ref[...], l0_ref[...])
        mine = jnp.where(first, l0_ref[...], l1_ref[...]).astype(F32)

        def consume(v):
            o_ref[...] = (mine + v.astype(F32)).astype(o_ref.dtype)
        _pair_step(nb, send_buf, land, send_sems, recv_sems, credits, consume)

    spec = pl.BlockSpec((tr, cols), lambda i, s: (i, 0))
    return _pair_call(name, body, nb, [spec, spec], spec, jax.ShapeDtypeStruct((rows, cols), CDT),
                      (tr, cols), layer0.dtype, (_place(), layer0, layer1),
                      extra_scratch=(pltpu.VMEM((tr, cols), layer0.dtype),))


def sum_exchange(name, parts):
    npart, rows, cols = parts.shape
    tr = _pair_rows(rows, cols * 4)
    n_steps = rows // tr

    def body(s_ref, x_ref, o_ref, land, send_sems, recv_sems, credits, mine):
        c = lax.axis_index("c")
        acc = x_ref[0].astype(F32)
        for k in range(1, npart):
            acc = acc + x_ref[k].astype(F32)
        mine[...] = acc
        o_ref[c] = acc

        def consume(v):
            o_ref[1 - c] = v
        _pair_step(n_steps, mine, land, send_sems, recv_sems, credits, consume)

    return _pair_call(name, body, n_steps, [pl.BlockSpec((npart, tr, cols), lambda i, s: (0, i, 0))],
                      pl.BlockSpec((2, tr, cols), lambda i, s: (0, i, 0)),
                      jax.ShapeDtypeStruct((2, rows, cols), F32), (tr, cols), F32, (_place(), parts),
                      extra_scratch=(pltpu.VMEM((tr, cols), F32),))


def allreduce_small(name, buf):
    rows = buf.shape[0]

    def body(x_ref, out_ref, gath, send_sems, recv_sems):
        x, y, c = lax.axis_index("x"), lax.axis_index("y"), lax.axis_index("c")
        me = 4 * x + 2 * y + c
        masks = [(k >> 2 & 1, k >> 1 & 1, k & 1) for k in range(1, 8)]

        def flip(v, bit):
            return 1 - v if bit else v

        sends = []
        for k, (bx, by, bc) in enumerate(masks):
            cp = pltpu.make_async_remote_copy(src_ref=x_ref, dst_ref=gath.at[me], send_sem=send_sems.at[k],
                                              recv_sem=recv_sems.at[k],
                                              device_id=(flip(x, bx), flip(y, by), flip(c, bc)), device_id_type=MESH)
            cp.start()
            sends.append(cp)
        gath[me] = x_ref[...]
        for k, (bx, by, bc) in enumerate(masks):
            px, py, pc = flip(x, bx), flip(y, by), flip(c, bc)
            pltpu.make_async_remote_copy(src_ref=x_ref, dst_ref=gath.at[4 * px + 2 * py + pc],
                                         send_sem=send_sems.at[k], recv_sem=recv_sems.at[k],
                                         device_id=(px, py, pc), device_id_type=MESH).wait_recv()
        for cp in sends:
            cp.wait_send()
        acc = gath[0]
        for d in range(1, 8):
            acc = acc + gath[d]
        out_ref[...] = acc

    return pl.pallas_call(
        body, name=name, in_specs=[pl.BlockSpec(memory_space=pltpu.VMEM)],
        out_specs=pl.BlockSpec(memory_space=pltpu.VMEM), out_shape=jax.ShapeDtypeStruct(buf.shape, F32),
        scratch_shapes=[pltpu.VMEM((8, rows, LANES), F32), pltpu.SemaphoreType.DMA((7,)),
                        pltpu.SemaphoreType.DMA((7,))],
        )(buf)


def _pack_flat(arrs, dtype, width, row_mult=8):
    flat = jnp.concatenate([a.reshape(-1).astype(dtype) for a in arrs])
    pad = (-flat.shape[0]) % (row_mult * width)
    if pad:
        flat = jnp.concatenate([flat, jnp.zeros((pad,), dtype)])
    return flat.reshape(-1, width)


def _unpack_flat(buf, shapes):
    flat = buf.reshape(-1)
    out, off = [], 0
    for s in shapes:
        n = math.prod(s)
        out.append(flat[off:off + n].reshape(s))
        off += n
    return out


def _in_to_padded(w):
    parts = []
    for name in IN_NEW_ORDER:
        _, width, o_off, o_w = IN_LAYOUT[name]
        parts.append(w[..., o_off:o_off + o_w])
        if o_w < width:
            parts.append(jnp.zeros(w.shape[:-1] + (width - o_w,), w.dtype))
    used = sum(IN_LAYOUT[n][1] for n in IN_NEW_ORDER)
    parts.append(jnp.zeros(w.shape[:-1] + (IN_PAD - used,), w.dtype))
    return jnp.concatenate(parts, axis=-1)


def _in_from_padded(g):
    parts = []
    for name in IN_ORIG_ORDER:
        off, _, _, o_w = IN_LAYOUT[name]
        parts.append(g[..., off:off + o_w])
    return jnp.concatenate(parts, axis=-1)


def _pcol(p, name):
    off, width, _, _ = IN_LAYOUT[name]
    return Cols(p, off, width)


def _lane_pad(v, width=LANES):
    v = v.reshape(-1)
    return jnp.concatenate([v, jnp.zeros((width - v.shape[0],), v.dtype)]).reshape(1, width)


def _f_norm_mod(h, sh, sc, w):
    return (_rms(h, w) * (1.0 + sc) + sh,)


def _f_norm_mod_thru(h, sh, sc, w):
    return h, _rms(h, w) * (1.0 + sc) + sh


def _f_attn_prep(qraw, kraw, vraw, cos2, sin2, qw, kw, gq, gk):
    q = qraw * lax.rsqrt(_group_mean(qraw * qraw, gq) + NORM_EPS) * qw
    q = _rope32(q, jnp.tile(cos2, (1, 4)), jnp.tile(sin2, (1, 4))) * (ATTN_HD ** -0.5 * LOG2E)
    k = kraw * lax.rsqrt(_group_mean(kraw * kraw, gk) + NORM_EPS) * kw
    return q, _rope32(k, cos2, sin2), vraw


def _f_ssd_finish(yf, yb, xs, z, d_exp, nw):
    y = (yf + yb + d_exp * xs) * (z * jax.nn.sigmoid(z))
    return (_rms(y, nw),)


def _f_ret_prep(rq, rk, cos1, sin1):
    cos_full, sin_signed = jnp.tile(cos1, (1, 4)), jnp.tile(sin1, (1, 4))
    return _rope64(rq, cos_full, sin_signed), _rope64(rk, cos_full, sin_signed) * (RET_DK ** -0.5)


def _f_ret_finish(yf, yb, g, gw):
    y = yf + yb
    outs = []
    for h in range(RET_HEADS):
        yh = y[:, h * RET_DK:(h + 1) * RET_DK]
        yc = yh - jnp.mean(yh, axis=-1, keepdims=True)
        outs.append(yc * lax.rsqrt(jnp.mean(yc * yc, axis=-1, keepdims=True) + NORM_EPS))
    return (jnp.concatenate(outs, axis=1) * gw * (g * jax.nn.sigmoid(g)),)


def _f_merge(p0, p1, p2, g0, g1, g2):
    return (jax.nn.sigmoid(g0) * p0 + jax.nn.sigmoid(g1) * p1 + jax.nn.sigmoid(g2) * p2,)


def _f_mid(h, mix, g1, sh2, sc2, w2):
    h_mid = h + g1 * mix
    return h_mid, _rms(h_mid, w2) * (1.0 + sc2) + sh2


def _epi_merge_bwd(dmerged, p0, p1, p2, g0, g1, g2):
    dps, dgs = [], []
    for pb, gate in ((p0, g0), (p1, g1), (p2, g2)):
        s = jax.nn.sigmoid(gate)
        dps.append(dmerged * s)
        dgs.append(dmerged * pb.astype(F32) * (s * (1.0 - s)))
    return tuple(dps + dgs)


def _epi_mid(mix, h, g1, sh2, sc2, w2):
    return (mix,) + _f_mid(h, mix, g1, sh2, sc2, w2)


def _epi_resid(o, h_mid, g2):
    return (o,) + _f_residual(h_mid, o, g2)


def _epi_sqrelu(a):
    r = jnp.maximum(a, 0.0)
    return a, r * r


def _epi_sqrelu_bwd(dhh, a):
    return (dhh * (2.0 * jnp.maximum(a.astype(F32), 0.0)),)


def _f_residual(h_mid, o, g2):
    return (h_mid + g2 * o,)


def _f_silu(x):
    return (x * jax.nn.sigmoid(x),)


def _f_bias(x, b):
    return (x + b,)


def _ssd_rows(xbc, p):
    rows = [Cols(xbc, LANES * k, LANES) for k in range(4)]
    rows += [Cols(xbc, 512 + LANES * g, LANES) for g in range(2)]
    rows += [Cols(xbc, 768 + LANES * g, LANES) for g in range(2)]
    return rows + [_pcol(p, "dt")]


def _ret_rows(rq, rk, p):
    off_v = IN_LAYOUT["rv"][0]
    return ([Cols(rq, LANES * h, LANES) for h in range(4)] + [Cols(rk, LANES * h, LANES) for h in range(4)]
            + [Cols(p, off_v + LANES * h, LANES) for h in range(4)])


def layer_fwd(li, h, mod, lw, tabs, m_ctx):
    t = h.shape[0]
    nb = m_ctx
    sh1, sc1, g1, sh2, sc2, g2 = mod
    nm = lambda s: f"l{li}_{s}"
    sv = {}
    (u,) = rowwise_fwd(nm("norm1"), _f_norm_mod, [h], [sh1, sc1], [lw["norm1_w"]], [(D_MODEL, CDT)], t, nb)
    p = mm(nm("in_proj"), u, lw["w_in"], F32)
    q, k, v = rowwise_fwd(
        nm("attn_prep"), _f_attn_prep,
        [_pcol(p, "q"), _pcol(p, "k"), _pcol(p, "v"), tabs["ca"], tabs["sa"]], [],
        [lw["qw"], lw["kw"], tabs["gq"], tabs["gk"]], [(512, CDT), (128, CDT), (128, CDT)], t, nb)
    tq = min(ATTN_TQ, m_ctx)
    kk, vv = _split_kv(k), _split_kv(v)
    ones_rows = jnp.concatenate([jnp.ones((2, 1, t), CDT), jnp.zeros((2, ATTN_ONES_ROWS - 1, t), CDT)], axis=1)
    attn_o, qT, oT, lse = attn_fwd(nm("attn"), q, kk, jnp.concatenate([vv.transpose(0, 2, 1), ones_rows], axis=1),
                                   m_ctx, tq)

    xbc = conv_fwd(_pcol(p, "xbc"), lw["conv_w"], lw["conv_b"], m_ctx)
    ssd_sh = [lw["dt_bias"], lw["a_log"]]
    yf, yb, sf, sb = scan_fwd(nm("ssd"), _make_ssd_chunk, _ssd_rows(xbc, p), ssd_sh, 4, 512, t, m_ctx)
    (ssd_o,) = rowwise_fwd(nm("ssd_fin"), _f_ssd_finish, [yf, yb, Cols(xbc, 0, 512), _pcol(p, "z")], [],
                           [lw["d_exp"], lw["ssd_nw"]], [(512, CDT)], t, nb)

    rq, rk = rowwise_fwd(nm("ret_prep"), _f_ret_prep, [_pcol(p, "rq"), _pcol(p, "rk"), tabs["rc"], tabs["rs"]],
                         [], [], [(512, F32), (512, F32)], t, nb)
    rf, rb, rsf, rsb = scan_fwd(nm("ret"), _make_ret_chunk, _ret_rows(rq, rk, p), [lw["ret_lg"]], 4, 512, t, m_ctx)
    (ret_o,) = rowwise_fwd(nm("ret_fin"), _f_ret_finish, [rf, rb, _pcol(p, "rg")], [], [lw["ret_gw"]],
                           [(512, CDT)], t, nb)

    pbs = [mm(nm(f"branch{b}"), br, lw["w_branch"][b], CDT) for b, br in enumerate((attn_o, ssd_o, ret_o))]
    gl = [Cols(p, 1024 * b, 1024) for b in range(3)]
    (merged,) = rowwise_fwd(nm("merge"), _f_merge, pbs + gl, [], [], [(D_MODEL, CDT)], t, nb)
    mix, h_mid, vv2 = mm(nm("out_proj"), merged, lw["w_out"], None, epilogue=_epi_mid, tile_ins=[h],
                         typed_ins=[g1, sh2, sc2], shared_ins=[lw["norm2_w"]], out_dtypes=[F32, F32, CDT], nb_ctx=nb,
                         whole_rows=True)
    a, hh = mm(nm("mlp1"), vv2, lw["w_mlp1"], None, epilogue=_epi_sqrelu, out_dtypes=[CDT, CDT])
    o, h_out = mm(nm("mlp2"), hh, lw["w_mlp2"], None, epilogue=_epi_resid, tile_ins=[h_mid], typed_ins=[g2],
                  out_dtypes=[F32, F32], nb_ctx=nb)
    sv.update(h=h, u=u, p=p, qT=qT, kk=kk, vv=vv, oT=oT, lse=lse, attn_o=attn_o, xbc=xbc, yf=yf, yb=yb,
              sf=sf, sb=sb, ssd_o=ssd_o, rq=rq, rk=rk, rf=rf, rb=rb, rsf=rsf, rsb=rsb, ret_o=ret_o, pbs=pbs,
              merged=merged, mix=mix, h_mid=h_mid, v=vv2, a=a, hh=hh, o=o)
    return h_out, sv


def layer_bwd(li, dh_out, sv, mod, lw, tabs, m_ctx):
    t = dh_out.shape[0]
    nb = m_ctx
    sh1, sc1, g1, sh2, sc2, g2 = mod
    nm = lambda s: f"l{li}_{s}_bwd"
    gw = {}
    p = sv["p"]
    (do,), (dg2,), _ = rowwise_bwd(nm("resid"), _f_residual, [sv["h_mid"], sv["o"]], [g2], [], [dh_out],
                                   [False, True], [], [CDT], t, nb)
    (da,) = mm(nm("mlp2_dx"), do, lw["w_mlp2"], None, transpose_b=True, epilogue=_epi_sqrelu_bwd,
               tile_ins=[sv["a"]], out_dtypes=[CDT])
    gw["w_mlp2"] = mm_tn(nm("mlp2_dw"), sv["hh"], do, CDT, ("rows", lw["w_mlp2"].shape[0] // 4))
    dv = mm(nm("mlp1_dx"), da, lw["w_mlp1"], F32, transpose_b=True)
    gw["w_mlp1"] = mm_tn(nm("mlp1_dw"), sv["v"], da, CDT, ("cols", lw["w_mlp1"].shape[1] // 4))
    (dh_a, dmix), (dg1, dsh2, dsc2), (gw["norm2_w"],) = rowwise_bwd(
        nm("mid"), _f_mid, [sv["h"], sv["mix"]], [g1, sh2, sc2], [lw["norm2_w"]], [dh_out, dv],
        [True, True], [True], [F32, CDT], t, nb, tall=False)
    gl = [Cols(p, 1024 * b, 1024) for b in range(3)]
    dmg = mm(nm("out_dx"), dmix, lw["w_out"], None, transpose_b=True, epilogue=_epi_merge_bwd,
             tile_ins=sv["pbs"] + gl, out_dtypes=[CDT] * 6)
    gw["w_out"] = mm_tn(nm("out_dw"), sv["merged"], dmix, CDT, ("rows", lw["w_out"].shape[0] // 4))
    dpb, dgl = dmg[:3], dmg[3:]
    brs = (sv["attn_o"], sv["ssd_o"], sv["ret_o"])
    d_attn_o = mm(nm("branch0_dx"), dpb[0], lw["w_branch"][0], CDT, transpose_b=True)
    d_ssd_o = mm(nm("branch1_dx"), dpb[1], lw["w_branch"][1], F32, transpose_b=True)
    d_ret_o = mm(nm("branch2_dx"), dpb[2], lw["w_branch"][2], F32, transpose_b=True)
    n_loc = lw["w_branch"].shape[2] // 4
    gw["w_branch"] = jnp.stack([mm_tn(nm(f"branch{b}_dw"), brs[b], dpb[b], CDT, ("cols", n_loc)) for b in range(3)],
                               axis=1).reshape(4, -1, n_loc)
    tq = min(ATTN_TQ, m_ctx)
    dq_rows, dk_s, dv_s = attn_bwd(nm("attn"), sv["qT"], d_attn_o, sv["oT"], sv["lse"], sv["kk"],
                                   sv["kk"].transpose(0, 2, 1), sv["vv"], m_ctx)
    (dq_raw, dk_raw, dv_raw), _, (gw["qw"], gw["kw"]) = rowwise_bwd(
        nm("attn_prep"), _f_attn_prep,
        [_pcol(p, "q"), _pcol(p, "k"), _pcol(p, "v"), tabs["ca"], tabs["sa"]], [],
        [lw["qw"], lw["kw"], tabs["gq"], tabs["gk"]],
        [dq_rows, _merge_kv(dk_s) * LN2, _merge_kv(dv_s)],
        [True, True, True, False, False], [True, True, False, False], [CDT] * 3, t, nb)
    (dy_ssd, dxs_fin, dz), _, (gw["d_exp"], gw["ssd_nw"]) = rowwise_bwd(
        nm("ssd_fin"), _f_ssd_finish, [sv["yf"], sv["yb"], Cols(sv["xbc"], 0, 512), _pcol(p, "z")], [],
        [lw["d_exp"], lw["ssd_nw"]], [d_ssd_o], [True, False, True, True], [True, True], [F32, F32, CDT], t, nb)
    ssd_sh = [lw["dt_bias"], lw["a_log"]]
    post_ssd = lambda d: [jnp.concatenate(d[0:8], axis=1), d[8]]
    (dxbc_f, ddt_f), dsh_f = scan_bwd(nm("ssd_f"), _make_ssd_chunk, _ssd_rows(sv["xbc"], p), ssd_sh, (sv["sf"],),
                                      dy_ssd, post_ssd, [(1024, F32), (LANES, F32)], 4, t, m_ctx, dirs=(0,))
    (dxbc_b, ddt_b), dsh_b = scan_bwd(nm("ssd_b"), _make_ssd_chunk, _ssd_rows(sv["xbc"], p), ssd_sh, (sv["sb"],),
                                      dy_ssd, post_ssd, [(1024, F32), (LANES, F32)], 4, t, m_ctx, dirs=(1,))
    gw["dt_bias"], gw["a_log"] = dsh_f[0] + dsh_b[0], dsh_f[1] + dsh_b[1]
    ddt = (ddt_f + ddt_b).astype(CDT)
    dxbc_raw, gw["conv_w"], gw["conv_b"] = conv_bwd(_pcol(p, "xbc"), lw["conv_w"], lw["conv_b"], dxbc_f, dxbc_b,
                                                    dxs_fin, m_ctx)
    (dy_ret, drg), _, (gw["ret_gw"],) = rowwise_bwd(
        nm("ret_fin"), _f_ret_finish, [sv["rf"], sv["rb"], _pcol(p, "rg")], [], [lw["ret_gw"]], [d_ret_o],
        [True, False, True], [True], [F32, CDT], t, nb)
    post_ret = lambda d: [jnp.concatenate(d[0:4], axis=1), jnp.concatenate(d[4:8], axis=1),
                          jnp.concatenate(d[8:12], axis=1)]
    rrows = _ret_rows(sv["rq"], sv["rk"], p)
    (dq_f, dk_f, dv_f), (dq_b, dk_b, dv_b), (gw["ret_lg"],) = scan_bwd(
        nm("ret"), _make_ret_chunk, rrows, [lw["ret_lg"]], (sv["rsf"], sv["rsb"]), dy_ret, post_ret,
        [(512, F32)] * 3, 4, t, m_ctx)
    drv = (dv_f + dv_b).astype(CDT)
    (drq, drk), _, _ = rowwise_bwd(nm("ret_prep"), _f_ret_prep,
                                   [_pcol(p, "rq"), _pcol(p, "rk"), tabs["rc"], tabs["rs"]], [], [],
                                   [(dq_f, dq_b), (dk_f, dk_b)], [True, True, False, False], [], [CDT, CDT], t, nb)
    pieces = {"gates": None, "xbc": dxbc_raw, "q": dq_raw, "z": dz, "rq": drq, "rk": drk, "rv": drv, "rg": drg,
              "k": dk_raw, "v": dv_raw, "dt": ddt}
    cols = list(dgl) + [pieces[n] for n in IN_NEW_ORDER[1:]]
    used = sum(c.shape[1] for c in cols)
    cols.append(jnp.zeros((t, IN_PAD - used), CDT))
    dp = jnp.concatenate(cols, axis=1)
    du = mm(nm("in_dx"), dp, lw["w_in"], F32, transpose_b=True)
    gw["w_in"] = mm_tn(nm("in_dw"), sv["u"], dp, CDT)
    (dh_in,), (dsh1, dsc1), (gw["norm1_w"],) = rowwise_bwd(
        nm("norm1"), _f_norm_mod_thru, [sv["h"]], [sh1, sc1], [lw["norm1_w"]], [dh_a, du], [True], [True], [F32],
        t, nb)
    return dh_in, [dsh1, dsc1, dg1, dsh2, dsc2, dg2], gw


def _rope_tables(n_lat, m_ctx):
    rows = n_lat // GRID_W
    row = jnp.repeat(jnp.arange(rows, dtype=F32), GRID_W)
    col = jnp.tile(jnp.arange(GRID_W, dtype=F32), rows)
    nfreq = ATTN_HD // 4
    inv = ROPE_THETA ** (-jnp.arange(nfreq, dtype=F32) / nfreq)
    ang = jnp.concatenate([row[:, None] * inv, col[:, None] * inv], axis=-1)
    cos = jnp.concatenate([jnp.ones((m_ctx, ATTN_HD // 2), F32), jnp.cos(ang)], axis=0)
    sin = jnp.concatenate([jnp.zeros((m_ctx, ATTN_HD // 2), F32), jnp.sin(ang)], axis=0)
    c64 = jnp.concatenate([cos, cos], axis=1)
    s64 = jnp.concatenate([-sin, sin], axis=1)
    pos = jnp.arange(m_ctx + n_lat, dtype=F32)
    inv_r = ROPE_THETA ** (-jnp.linspace(0.0, 1.0, RET_DK // 2, dtype=F32))
    ang_r = pos[:, None] * inv_r
    rc = jnp.concatenate([jnp.cos(ang_r)] * 2, axis=1)
    rs = jnp.concatenate([-jnp.sin(ang_r), jnp.sin(ang_r)], axis=1)
    return dict(ca=jnp.tile(c64, (1, 2)), sa=jnp.tile(s64, (1, 2)), rc=rc, rs=rs, gq=_group_matrix(512, ATTN_HD),
                gk=_group_matrix(128, ATTN_HD))


def _layer_weights(full, small, layer):
    return dict(
        w_in=full["w_in"][layer], w_branch=full["w_branch"][layer], w_out=full["w_out"][layer],
        w_mlp1=full["w_mlp1"][layer], w_mlp2=full["w_mlp2"][layer],
        norm1_w=small["norm1_w"][layer][None], norm2_w=small["norm2_w"][layer][None],
        qw=jnp.tile(small["attn_q_norm"][layer], 8)[None], kw=jnp.tile(small["attn_k_norm"][layer], 2)[None],
        conv_w=small["ssd_conv_w"][layer], conv_b=small["ssd_conv_b"][layer][None],
        dt_bias=_lane_pad(small["ssd_dt_bias"][layer]), a_log=_lane_pad(small["ssd_a_log"][layer]),
        d_exp=jnp.repeat(small["ssd_d"][layer], SSD_HD)[None], ssd_nw=small["ssd_norm_w"][layer][None],
        ret_lg=_lane_pad(small["ret_log_decay"][layer]), ret_gw=small["ret_gn_w"][layer][None])


def local_step(x, c, ctx, full, small, loss_target):
    n_lat, d = x.shape
    m_ctx = ctx.shape[0]
    t = n_lat + m_ctx
    depth = small["norm1_w"].shape[0]
    tabs = _rope_tables(n_lat, m_ctx)
    h = jnp.concatenate([ctx, x], axis=0)
    cc = jnp.concatenate([small["c_ctx"][None], c, jnp.zeros((COND_ROWS - 2, d), F32)], axis=0)
    (scc,) = rowwise_fwd("cond_silu", _f_silu, [cc], [], [], [(d, CDT)], COND_ROWS, 0)
    mods, saved, lws = [], [], []
    for layer in range(depth):
        lw = _layer_weights(full, small, layer)
        mod_raw = mm(f"l{layer}_mod", scc, full["w_mod"][layer], F32)
        (mod8,) = rowwise_fwd(f"l{layer}_mod_bias", _f_bias, [mod_raw], [], [small["b_mod"][layer][None]],
                              [(6 * d, F32)], COND_ROWS, 0)
        mod = [mod8[0:2, k * d:(k + 1) * d].reshape(2, 1, d) for k in range(6)]
        h, sv = layer_fwd(layer, h, mod, lw, tabs, m_ctx)
        mods.append(mod)
        saved.append(sv)
        lws.append(lw)
    loss, dh, d_final = loss_head(h, loss_target, small["final_norm_w"][None], m_ctx)

    gbig = {k: [None] * depth for k in BIG}
    gs = {k: [None] * depth for k in SMALL if k not in ("c_ctx", "final_norm_w")}
    d_scc = None
    for layer in reversed(range(depth)):
        lw = lws[layer]
        dh, dmod, gw = layer_bwd(layer, dh, saved[layer], mods[layer], lw, tabs, m_ctx)
        dmod8 = jnp.concatenate([jnp.concatenate([g_.reshape(2, d) for g_ in dmod], axis=1),
                                 jnp.zeros((COND_ROWS - 2, 6 * d), F32)], axis=0)
        (dmod_c,), _, (db_mod,) = rowwise_bwd(f"l{layer}_mod_bias_bwd", _f_bias, [dmod8], [],
                                              [small["b_mod"][layer][None]], [dmod8], [True], [True], [CDT], COND_ROWS, 0)
        gbig["w_mod"][layer] = mm_tn(f"l{layer}_mod_dw", scc, dmod_c, CDT, ("cols", 6 * d // 4))
        part = mm(f"l{layer}_mod_dx", dmod_c, full["w_mod"][layer], F32, transpose_b=True)
        d_scc = part if d_scc is None else d_scc + part
        g_in = _in_from_padded(gw["w_in"])
        gbig["w_in"][layer] = g_in.reshape(d, 4, g_in.shape[1] // 4).transpose(1, 0, 2)
        for k in ("w_branch", "w_out", "w_mlp1", "w_mlp2"):
            gbig[k][layer] = gw[k]
        gs["b_mod"][layer] = db_mod.reshape(-1)
        gs["norm1_w"][layer] = gw["norm1_w"].reshape(-1)
        gs["norm2_w"][layer] = gw["norm2_w"].reshape(-1)
        gs["attn_q_norm"][layer] = gw["qw"].reshape(8, ATTN_HD).sum(0)
        gs["attn_k_norm"][layer] = gw["kw"].reshape(2, ATTN_HD).sum(0)
        gs["ssd_conv_w"][layer] = gw["conv_w"]
        gs["ssd_conv_b"][layer] = gw["conv_b"].reshape(-1)
        gs["ssd_dt_bias"][layer] = gw["dt_bias"][0, :16].reshape(2, 8)
        gs["ssd_a_log"][layer] = gw["a_log"][0, :16].reshape(2, 8)
        gs["ssd_d"][layer] = gw["d_exp"].reshape(SSD_HEADS, SSD_HD).sum(1)
        gs["ssd_norm_w"][layer] = gw["ssd_nw"].reshape(-1)
        gs["ret_log_decay"][layer] = gw["ret_lg"][0, :8].reshape(2, 4)
        gs["ret_gn_w"][layer] = gw["ret_gw"].reshape(-1)
    (d_cc,), _, _ = rowwise_bwd("cond_silu_bwd", _f_silu, [cc], [], [], [d_scc], [True], [], [F32], COND_ROWS, 0)
    g_small = {k: jnp.stack(v) for k, v in gs.items()}
    g_small["c_ctx"] = d_cc[0]
    g_small["final_norm_w"] = d_final.reshape(-1)
    return loss, dh[m_ctx:], gbig, g_small


def kernel(x, c, ctx, c_ctx, w_mod, b_mod, norm1_w, norm2_w, w_in, attn_q_norm, attn_k_norm, ssd_conv_w, ssd_conv_b, ssd_dt_bias, ssd_a_log, ssd_d, ssd_norm_w, ret_log_decay, ret_gn_w, w_branch, w_out, w_mlp1, w_mlp2, final_norm_w, loss_target, m_c_ctx, m_w_mod, m_b_mod, m_norm1_w, m_norm2_w, m_w_in, m_attn_q_norm, m_attn_k_norm, m_ssd_conv_w, m_ssd_conv_b, m_ssd_dt_bias, m_ssd_a_log, m_ssd_d, m_ssd_norm_w, m_ret_log_decay, m_ret_gn_w, m_w_branch, m_w_out, m_w_mlp1, m_w_mlp2, m_final_norm_w, v_c_ctx, v_w_mod, v_b_mod, v_norm1_w, v_norm2_w, v_w_in, v_attn_q_norm, v_attn_k_norm, v_ssd_conv_w, v_ssd_conv_b, v_ssd_dt_bias, v_ssd_a_log, v_ssd_d, v_ssd_norm_w, v_ret_log_decay, v_ret_gn_w, v_w_branch, v_w_out, v_w_mlp1, v_w_mlp2, v_final_norm_w):
    env = dict(locals())
    w_loc = {k: env[k] for k in WEIGHTS}
    m_loc = {k: env["m_" + k] for k in WEIGHTS}
    v_loc = {k: env["v_" + k] for k in WEIGHTS}
    chip = 2 * lax.axis_index("x") + lax.axis_index("y")
    core = lax.axis_index("c")

    depth = w_loc["w_mod"].shape[0]
    assert depth == 2, "the exchanges split the layers between a chip's two cores"
    shards = [w_loc[k].astype(CDT).reshape(depth, -1, w_loc[k].shape[-1]) for k in BIG]
    mine = gather_layers("gather_weights", shards, [BIG_KIND[k] for k in BIG])
    full = {}
    for k, arr in zip(BIG, mine):
        both = exchange_both("share_" + k, arr.reshape(-1, arr.shape[-1]))
        if k == "w_in":
            both = both.reshape(depth, 4, -1, both.shape[-1]).transpose(0, 2, 1, 3)
            both = _in_to_padded(both.reshape(depth, both.shape[1], -1))
        full[k] = both.reshape((depth,) + w_loc[k].shape[1:-1] + (-1,)) if BIG_KIND[k] == "cols" else \
            both.reshape((depth,) + w_loc[k].shape[1:-2] + (-1, w_loc[k].shape[-1])) if BIG_KIND[k] == "rows" else both

    cw = w_loc["ssd_conv_w"]
    cw_w = cw.shape[-1]
    placed = lax.dynamic_update_slice(jnp.zeros(cw.shape[:-1] + (4 * cw_w,), F32),
                                      cw * (core == 0).astype(F32), (0, 0, chip * cw_w))
    conv_full = _unpack_flat(allreduce_small("gather_conv_w", _pack_flat([placed], F32, LANES)), [placed.shape])[0]
    small = {k: w_loc[k] for k in SMALL}
    small["ssd_conv_w"] = conv_full

    loss_l, grad_x, g_big, g_small = local_step(x[0], c, ctx[0], full, small, loss_target[0])

    small_shapes = [g_small[k].shape for k in SMALL] + [(LANES,)]
    summed = _unpack_flat(allreduce_small("reduce_small", _pack_flat([g_small[k] for k in SMALL] + [loss_l], F32, LANES)),
                          small_shapes)
    gsum = dict(zip(SMALL, summed[:-1]))
    loss = summed[-1][0]
    gsum["ssd_conv_w"] = lax.dynamic_slice(gsum["ssd_conv_w"], (0, 0, chip * cw_w), cw.shape)

    pair = []
    for k in BIG:
        _, rows, cols = g_big[k][0].shape
        pair.append(exchange_add("pair_" + k, g_big[k][0].reshape(4 * rows, cols),
                                 g_big[k][1].reshape(4 * rows, cols)).reshape(4, rows, cols))
    landed = scatter_pieces("scatter_grads", pair)
    g_sum = [sum_exchange("sum_" + k, parts) for k, parts in zip(BIG, landed)]

    grads, deltas, new_m, new_v = {}, {}, {}, {}
    for i, k in enumerate(BIG):
        shp = w_loc[k].shape
        three_d = lambda a, shp=shp: a.reshape((-1,) + shp[-2:])
        res = adamw("adamw_" + k, three_d(w_loc[k]), three_d(m_loc[k]), three_d(v_loc[k]), [three_d(g_sum[i])])
        grads[k], deltas[k], new_m[k], new_v[k] = [r.reshape(shp) for r in res]
    small_loc_shapes = [w_loc[k].shape for k in SMALL]
    res = adamw("adamw_small", _pack_flat([w_loc[k] for k in SMALL], F32, LANES)[None],
                _pack_flat([m_loc[k] for k in SMALL], F32, LANES)[None],
                _pack_flat([v_loc[k] for k in SMALL], F32, LANES)[None],
                [_pack_flat([gsum[k] for k in SMALL], F32, LANES)[None]])
    for dst, r in zip((grads, deltas, new_m, new_v), res):
        dst.update(dict(zip(SMALL, _unpack_flat(r, small_loc_shapes))))

    return (loss, grad_x[None], *[grads[k] for k in WEIGHTS], *[deltas[k] for k in WEIGHTS],
            *[new_m[k] for k in WEIGHTS], *[new_v[k] for k in WEIGHTS])
```

```python
import math
from typing import NamedTuple

import jax
import jax.numpy as jnp
from jax import lax
from jax.experimental import pallas as pl
from jax.experimental.pallas import tpu as pltpu

F32 = jnp.float32
CDT = jnp.bfloat16
NORM_EPS = 1e-6
ROPE_THETA = 10000.0
GRID_W = 64
D_MODEL = 1024
ATTN_HD = 64
SSD_HEADS, SSD_HD = 8, 64
RET_HEADS, RET_DK = 4, 128
CHUNK = 256
ROW_TILE = 256
ROW_TILE_TALL = 768
MM_ROWS = 768
MM_TN_ROWS = 2816
MM_VMEM_BUDGET = 44 * 1024 * 1024
ATTN_TQ, ATTN_TK = 256, 256
ATTN_ONES_ROWS = 16
ATTN_TK_BWD = 2048
LOG2E, LN2 = 1.4426950408889634, 0.6931471805599453
ATTN_TK_FWD = 2048
LANES = 128
PAIR_BLOCK_BYTES = 4 * 1024 * 1024
COND_ROWS = 16
VMEM_LIMIT = 56 * 1024 * 1024

ADAM_LR, ADAM_B1, ADAM_B2, ADAM_EPS, ADAM_WD, ADAM_STEP = 0.001, 0.9, 0.999, 1e-08, 0.01, 10

IN_LAYOUT = {
    "gates": (0, 3072, 4368, 3072), "xbc": (3072, 1024, 1280, 1024), "q": (4096, 512, 0, 512),
    "z": (4608, 512, 768, 512), "rq": (5120, 512, 2320, 512), "rk": (5632, 512, 2832, 512),
    "rv": (6144, 512, 3344, 512), "rg": (6656, 512, 3856, 512), "k": (7168, 128, 512, 128),
    "v": (7296, 128, 640, 128), "dt": (7424, 128, 2304, 16),
}
IN_PAD = 7680
IN_ORIG_ORDER = ("q", "k", "v", "z", "xbc", "dt", "rq", "rk", "rv", "rg", "gates")
IN_NEW_ORDER = ("gates", "xbc", "q", "z", "rq", "rk", "rv", "rg", "k", "v", "dt")

BIG = ("w_mod", "w_in", "w_branch", "w_out", "w_mlp1", "w_mlp2")
BIG_KIND = {"w_mod": "cols", "w_in": "slices", "w_branch": "cols", "w_out": "rows", "w_mlp1": "cols", "w_mlp2": "rows"}
SMALL = ("c_ctx", "b_mod", "norm1_w", "norm2_w", "attn_q_norm", "attn_k_norm", "ssd_conv_w", "ssd_conv_b",
         "ssd_dt_bias", "ssd_a_log", "ssd_d", "ssd_norm_w", "ret_log_decay", "ret_gn_w", "final_norm_w")
WEIGHTS = ("c_ctx", "w_mod", "b_mod", "norm1_w", "norm2_w", "w_in", "attn_q_norm", "attn_k_norm", "ssd_conv_w",
           "ssd_conv_b", "ssd_dt_bias", "ssd_a_log", "ssd_d", "ssd_norm_w", "ret_log_decay", "ret_gn_w",
           "w_branch", "w_out", "w_mlp1", "w_mlp2", "final_norm_w")


def _cp(sem):
    return pltpu.CompilerParams(dimension_semantics=sem, vmem_limit_bytes=VMEM_LIMIT)


class Cols(NamedTuple):
    arr: jax.Array
    off: int
    width: int


def _width(item):
    return item.width if isinstance(item, Cols) else item.shape[1]


def _row_in(item, rows, imap=None):
    imap = imap or (lambda i: i)
    if isinstance(item, Cols):
        assert item.off % item.width == 0
        blk = item.off // item.width
        return item.arr, pl.BlockSpec((rows, item.width), lambda i, blk=blk: (imap(i), blk))
    return item, pl.BlockSpec((rows, item.shape[1]), lambda i: (imap(i), 0))


def _const_spec(shape):
    return pl.BlockSpec(shape, lambda *_: (0,) * len(shape))


def _mxu(a, b, dims=(((1,), (0,)), ((), ()))):
    return lax.dot_general(a.astype(CDT), b.astype(CDT), dims, preferred_element_type=F32)


_NT = (((1,), (1,)), ((), ()))
_TN = (((0,), (0,)), ((), ()))


@jax.custom_vjp
def _softplus(x):
    return jnp.maximum(x, 0.0) + jnp.log1p(jnp.exp(-jnp.abs(x)))


def _softplus_fwd(x):
    return _softplus(x), x


def _softplus_bwd(x, g):
    return (g * jax.nn.sigmoid(x),)


_softplus.defvjp(_softplus_fwd, _softplus_bwd)


def _group_mean_impl(x, gmat):
    hi = x.astype(CDT)
    lo = (x - hi.astype(F32)).astype(CDT)
    return (jnp.dot(hi, gmat, preferred_element_type=F32) + jnp.dot(lo, gmat, preferred_element_type=F32))


@jax.custom_vjp
def _group_mean(x, gmat):
    return _group_mean_impl(x, gmat)


def _group_mean_fwd(x, gmat):
    return _group_mean_impl(x, gmat), gmat


def _group_mean_bwd(gmat, g):
    return _group_mean_impl(g, gmat), jnp.zeros_like(gmat)


_group_mean.defvjp(_group_mean_fwd, _group_mean_bwd)


def _group_matrix(width, group):
    r = jnp.arange(width) // group
    return jnp.where(r[:, None] == r[None, :], 1.0 / group, 0.0).astype(CDT)


def _make_rope(half):
    def partner(x):
        w = x.shape[1]
        lane = lax.broadcasted_iota(jnp.int32, x.shape, 1)
        first = (lane % (2 * half)) < half
        return jnp.where(first, pltpu.roll(x, w - half, axis=1), pltpu.roll(x, half, axis=1))

    def impl(x, cos_full, sin_signed):
        return x * cos_full + partner(x) * sin_signed

    @jax.custom_vjp
    def rope(x, cos_full, sin_signed):
        return impl(x, cos_full, sin_signed)

    def fwd(x, cos_full, sin_signed):
        return impl(x, cos_full, sin_signed), (cos_full, sin_signed)

    def bwd(res, g):
        cos_full, sin_signed = res
        return impl(g, cos_full, -sin_signed), jnp.zeros_like(cos_full), jnp.zeros_like(sin_signed)

    rope.defvjp(fwd, bwd)
    return rope


_rope32 = _make_rope(32)
_rope64 = _make_rope(64)


def _rms(x, w):
    return x * lax.rsqrt(jnp.mean(x * x, axis=-1, keepdims=True) + NORM_EPS) * w


def _col(v, lane_index):
    lane = lax.broadcasted_iota(jnp.int32, v.shape, 1)
    return jnp.sum(jnp.where(lane == lane_index, v, 0.0), axis=1, keepdims=True)


def _typed_spec(width, nb_ctx):
    return pl.BlockSpec((None, 1, width), lambda i: (jnp.where(i >= nb_ctx, 1, 0), 0, 0))


def _row_tile(n_rows, typed):
    return min(ROW_TILE, n_rows) if typed else _pick(n_rows, (ROW_TILE_TALL, ROW_TILE))


def rowwise_fwd(name, f, rows, typed, shared, outs, n_rows, nb_ctx):
    tm = _row_tile(n_rows, typed)
    nin = len(rows) + len(typed) + len(shared)

    def body(*refs):
        res = f(*[r[...] for r in refs[:nin]])
        for o_ref, o in zip(refs[nin:], res):
            o_ref[...] = o.astype(o_ref.dtype)

    arrs, specs = [], []
    for it in rows:
        a, s = _row_in(it, tm)
        arrs.append(a)
        specs.append(s)
    for t in typed:
        arrs.append(t)
        specs.append(_typed_spec(t.shape[-1], nb_ctx))
    for s_ in shared:
        arrs.append(s_)
        specs.append(_const_spec(s_.shape))
    res = pl.pallas_call(
        body, name=name, grid=(n_rows // tm,), in_specs=specs,
        out_specs=[pl.BlockSpec((tm, w), lambda i: (i, 0)) for w, _ in outs],
        out_shape=[jax.ShapeDtypeStruct((n_rows, w), dt) for w, dt in outs],
        compiler_params=_cp(("parallel",)))(*arrs)
    return res


def rowwise_bwd(name, f, rows, typed, shared, cots, row_diff, shared_diff, drow_dtypes, n_rows, nb_ctx):
    tm = _row_tile(n_rows, typed)
    cot_groups = [c_ if isinstance(c_, tuple) else (c_,) for c_ in cots]
    cots = [a for grp in cot_groups for a in grp]
    nr, nt, ns, nc = len(rows), len(typed), len(shared), len(cots)
    nin = nr + nt + ns
    d_rows = [k for k in range(nr) if row_diff[k]]
    d_sh = [k for k in range(ns) if shared_diff[k]]

    def body(*refs):
        rvals = [r[...] for r in refs[:nr]]
        tvals = [r[...] for r in refs[nr:nr + nt]]
        svals = [r[...] for r in refs[nr + nt:nin]]
        cparts = [r[...].astype(F32) for r in refs[nin:nin + nc]]
        cvals = []
        for grp in cot_groups:
            cvals.append(sum(cparts[1:len(grp)], cparts[0]))
            cparts = cparts[len(grp):]
        out_refs = refs[nin + nc:]

        def g(*dv):
            dv = list(dv)
            rv = list(rvals)
            for k in d_rows:
                rv[k] = dv.pop(0)
            tv = [dv.pop(0) for _ in range(nt)]
            sv = list(svals)
            for k in d_sh:
                sv[k] = dv.pop(0)
            return tuple(o.astype(F32) for o in f(*rv, *tv, *sv))

        prim = [rvals[k].astype(F32) for k in d_rows] + tvals + [svals[k] for k in d_sh]
        _, vjp = jax.vjp(g, *prim)
        grads = list(vjp(tuple(cvals)))
        i = pl.program_id(0)
        for ref in out_refs[:len(d_rows)]:
            ref[...] = grads.pop(0).astype(ref.dtype)
        first_typed = (i == 0) | (i == nb_ctx)
        for ref in out_refs[len(d_rows):len(d_rows) + nt]:
            gr = grads.pop(0)

            @pl.when(first_typed)
            def _(ref=ref, gr=gr):
                ref[...] = gr

            @pl.when(jnp.logical_not(first_typed))
            def _(ref=ref, gr=gr):
                ref[...] += gr
        for ref in out_refs[len(d_rows) + nt:]:
            gr = grads.pop(0)

            @pl.when(i == 0)
            def _(ref=ref, gr=gr):
                ref[...] = gr

            @pl.when(i != 0)
            def _(ref=ref, gr=gr):
                ref[...] += gr

    arrs, specs = [], []
    for it in list(rows):
        a, s = _row_in(it, tm)
        arrs.append(a)
        specs.append(s)
    for t in typed:
        arrs.append(t)
        specs.append(_typed_spec(t.shape[-1], nb_ctx))
    for s_ in shared:
        arrs.append(s_)
        specs.append(_const_spec(s_.shape))
    for c_ in cots:
        a, s = _row_in(c_, tm)
        arrs.append(a)
        specs.append(s)
    out_specs, out_shape = [], []
    for k, dt in zip(d_rows, drow_dtypes):
        w = _width(rows[k])
        out_specs.append(pl.BlockSpec((tm, w), lambda i: (i, 0)))
        out_shape.append(jax.ShapeDtypeStruct((n_rows, w), dt))
    for t in typed:
        out_specs.append(_typed_spec(t.shape[-1], nb_ctx))
        out_shape.append(jax.ShapeDtypeStruct(t.shape, F32))
    for k in d_sh:
        out_specs.append(_const_spec(shared[k].shape))
        out_shape.append(jax.ShapeDtypeStruct(shared[k].shape, F32))
    res = pl.pallas_call(body, name=name, grid=(n_rows // tm,), in_specs=specs, out_specs=out_specs,
                         out_shape=out_shape, compiler_params=_cp(("arbitrary",)))(*arrs)
    n1, n2 = len(d_rows), len(d_rows) + nt
    return list(res[:n1]), list(res[n1:n2]), list(res[n2:])


def _pick(n, prefs):
    for p in prefs:
        if n % p == 0:
            return p
    return n


def mm(name, a, b, out_dtype, transpose_b=False, epilogue=None, tile_ins=(), typed_ins=(), shared_ins=(),
       out_dtypes=None, nb_ctx=None, whole_rows=False):
    n, k = b.shape if transpose_b else b.shape[::-1]
    m = (a.arr if isinstance(a, Cols) else a).shape[0]
    assert _width(a) == k
    tm = min(ROW_TILE, m) if typed_ins else _pick(m, (MM_ROWS, 256))
    out_dtypes = out_dtypes or [out_dtype]
    tile_arrs = [x.arr if isinstance(x, Cols) else x for x in tile_ins]
    tile_bytes = sum(jnp.dtype(d).itemsize for d in out_dtypes) + sum(x.dtype.itemsize for x in tile_arrs)
    fits = lambda c: 2 * (tm * k * 2 + k * c * 2 + tm * c * tile_bytes) <= MM_VMEM_BUDGET
    tn = next(c for c in (2560, 2048, 1536, 1024, 512, 256, 128, n) if n % c == 0 and fits(c) or c == n)
    if whole_rows:
        assert fits(n)
        tn = n
    dims = _NT if transpose_b else (((1,), (0,)), ((), ()))
    n_ex = len(tile_ins) + len(typed_ins) + len(shared_ins)

    def body(a_ref, b_ref, *refs):
        prod = lax.dot_general(a_ref[...], b_ref[...], dims, preferred_element_type=F32)
        outs = (prod,) if epilogue is None else epilogue(prod, *[r[...] for r in refs[:n_ex]])
        for o_ref, o in zip(refs[n_ex:], outs):
            o_ref[...] = o.astype(o_ref.dtype)

    a_arr, a_spec = _row_in(a, tm)
    a_spec = pl.BlockSpec(a_spec.block_shape, lambda j, i, f=a_spec.index_map: f(i))
    b_spec = pl.BlockSpec((tn, k), lambda j, i: (j, 0)) if transpose_b else pl.BlockSpec((k, tn), lambda j, i: (0, j))
    tile = pl.BlockSpec((tm, tn), lambda j, i: (i, j))
    ex_specs = []
    for x in tile_ins:
        base = x.off // tn if isinstance(x, Cols) else 0
        assert not isinstance(x, Cols) or (x.off % tn == 0 and x.width == n)
        ex_specs.append(pl.BlockSpec((tm, tn), lambda j, i, base=base: (i, base + j)))
    ex_specs += [pl.BlockSpec((None, 1, tn), lambda j, i: (jnp.where(i >= nb_ctx, 1, 0), 0, j))] * len(typed_ins)
    ex_specs += [pl.BlockSpec((1, tn), lambda j, i: (0, j))] * len(shared_ins)
    res = pl.pallas_call(
        body, name=name, grid=(n // tn, m // tm), in_specs=[a_spec, b_spec] + ex_specs,
        out_specs=[tile] * len(out_dtypes),
        out_shape=[jax.ShapeDtypeStruct((m, n), d) for d in out_dtypes],
        compiler_params=_cp(("parallel", "parallel")))(a_arr, b, *tile_arrs, *typed_ins, *shared_ins)
    return res[0] if epilogue is None else res


def mm_tn(name, a, b, out_dtype=F32, pieces=None):
    t = (a.arr if isinstance(a, Cols) else a).shape[0]
    k, n = _width(a), _width(b)
    tt = _pick(t, (MM_TN_ROWS, MM_ROWS, 256))
    k_unit = pieces[1] if pieces and pieces[0] == "rows" else k
    n_unit = pieces[1] if pieces and pieces[0] == "cols" else n
    tk = _pick(k_unit, (1024, 512, 256, 128))
    tn = _pick(n_unit, (1280, 1024, 512, 256, 128))
    n_t = t // tt

    def body(a_ref, b_ref, o_ref, acc):
        part = lax.dot_general(a_ref[...], b_ref[...], _TN, preferred_element_type=F32)
        ti = pl.program_id(2)

        @pl.when(ti == 0)
        def _():
            acc[...] = part

        @pl.when(ti != 0)
        def _():
            acc[...] += part

        @pl.when(ti == n_t - 1)
        def _():
            o_ref[...] = acc[...].astype(o_ref.dtype)

    def win(item, width):
        if isinstance(item, Cols):
            assert item.off % width == 0
            return item.arr, item.off // width
        return item, 0

    a_arr, a0 = win(a, tk)
    b_arr, b0 = win(b, tn)
    if pieces is None:
        out_spec = pl.BlockSpec((tk, tn), lambda ki, ni, ti: (ki, ni))
        out_shape = (k, n)
    elif pieces[0] == "cols":
        per = n_unit // tn
        out_spec = pl.BlockSpec((None, tk, tn), lambda ki, ni, ti: (ni // per, ki, ni % per))
        out_shape = (4, k, n_unit)
    else:
        per = k_unit // tk
        out_spec = pl.BlockSpec((None, tk, tn), lambda ki, ni, ti: (ki // per, ki % per, ni))
        out_shape = (4, k_unit, n)
    return pl.pallas_call(
        body, name=name, grid=(k // tk, n // tn, n_t),
        in_specs=[pl.BlockSpec((tt, tk), lambda ki, ni, ti: (ti, a0 + ki)),
                  pl.BlockSpec((tt, tn), lambda ki, ni, ti: (ti, b0 + ni))],
        out_specs=out_spec, out_shape=jax.ShapeDtypeStruct(out_shape, out_dtype),
        scratch_shapes=[pltpu.VMEM((tk, tn), F32)],
        compiler_params=_cp(("parallel", "parallel", "arbitrary")))(a_arr, b_arr)


def _heads_t(rows_blk):
    blk = rows_blk.astype(F32).T
    return jnp.concatenate([blk[hh * ATTN_HD:(hh + 1) * ATTN_HD, :] for hh in range(4)], axis=1)


def _heads_rows(t_blk):
    tq = t_blk.shape[1] // 4
    return jnp.concatenate([t_blk[:, hh * tq:(hh + 1) * tq] for hh in range(4)], axis=0).T


def attn_fwd(name, q, kk, vT_ones, m_ctx, tq):
    t, hd, hd_ext = kk.shape[1], ATTN_HD, vT_ones.shape[1]
    nq, r = t // tq, 4 * tq
    tk = _pick(t - m_ctx, (ATTN_TK_FWD, ATTN_TK))
    nqc, n_lat_tiles = m_ctx // tq, (t - m_ctx) // tk

    def body(q_ref, k_ref, vT_ref, o_ref, qT_ref, oT_ref, lse_ref):
        i = pl.program_id(1)
        q_t = _heads_t(q_ref[...]).astype(CDT)
        qT_ref[...] = q_t

        def tile(off, size, carry):
            mi, acc = carry
            sub = min(size, ATTN_TK)
            offs = [off + u * sub for u in range(size // sub)]
            sts = [jnp.dot(k_ref[pl.ds(o, sub), :], q_t, preferred_element_type=F32) for o in offs]
            for o, st in zip(offs, sts):
                mn = jnp.maximum(mi, jnp.max(st, axis=0, keepdims=True))
                pt = jnp.exp2(st - mn)
                acc = jnp.exp2(mi - mn) * acc + jnp.dot(vT_ref[:, pl.ds(o, sub)], pt.astype(CDT),
                                                        preferred_element_type=F32)
                mi = mn
            return mi, acc

        carry = tile(0, m_ctx, (jnp.full((1, r), -1e30, F32), jnp.zeros((hd_ext, r), F32)))
        mi, acc = lax.fori_loop(
            0, jnp.where(i < nqc, 0, n_lat_tiles),
            lambda j, cr: tile(pl.multiple_of(m_ctx + j * tk, ATTN_TK), tk, cr), carry)
        li = acc[hd:hd + 1]
        o_t = acc[:hd] / li
        oT_ref[...] = o_t.astype(oT_ref.dtype)
        o_ref[...] = _heads_rows(o_t).astype(o_ref.dtype)
        lse_ref[...] = mi + jnp.log2(li)

    blk_t = pl.BlockSpec((None, None, hd, r), lambda g, i: (g, i, 0, 0))
    rows = pl.BlockSpec((tq, 4 * hd), lambda g, i: (i, g))
    return pl.pallas_call(
        body, name=name, grid=(2, nq),
        in_specs=[rows, pl.BlockSpec((None, t, hd), lambda g, i: (g, 0, 0)),
                  pl.BlockSpec((None, hd_ext, t), lambda g, i: (g, 0, 0))],
        out_specs=[rows, blk_t, blk_t, pl.BlockSpec((None, None, 1, r), lambda g, i: (g, i, 0, 0))],
        out_shape=[jax.ShapeDtypeStruct((t, 8 * hd), CDT), jax.ShapeDtypeStruct((2, nq, hd, r), CDT),
                   jax.ShapeDtypeStruct((2, nq, hd, r), CDT), jax.ShapeDtypeStruct((2, nq, 1, r), F32)],
        compiler_params=_cp(("parallel", "arbitrary")))(q, kk, vT_ones)


def attn_bwd(name, qT, do, oT, lse, kk, kT, vv, m_ctx):
    _, nq, hd, r = qT.shape
    t = kk.shape[1]
    tq = r // 4
    tk = _pick(t - m_ctx, (ATTN_TK_BWD, ATTN_TK))
    nqc, n_lat_tiles = m_ctx // tq, (t - m_ctx) // tk

    def body(qT_ref, do_ref, oT_ref, lse_ref, k_ref, kT_ref, v_ref, dq_ref, dk_ref, dv_ref):
        i = pl.program_id(1)

        @pl.when(i == 0)
        def _():
            dk_ref[...] = jnp.zeros_like(dk_ref)
            dv_ref[...] = jnp.zeros_like(dv_ref)

        q_t = qT_ref[...]
        do_f = _heads_t(do_ref[...])
        do_t = do_f.astype(CDT)
        lse = lse_ref[...]
        delta = jnp.sum(do_f * oT_ref[...].astype(F32), axis=0, keepdims=True)

        def tile(off, size, dq):
            sub = min(size, ATTN_TK)
            offs = [off + u * sub for u in range(size // sub)]
            sts = [jnp.dot(k_ref[pl.ds(o, sub), :], q_t, preferred_element_type=F32) for o in offs]
            dpts = [jnp.dot(v_ref[pl.ds(o, sub), :], do_t, preferred_element_type=F32) for o in offs]
            for o, st, dpt in zip(offs, sts, dpts):
                pt = jnp.exp2(st - lse)
                dv_ref[pl.ds(o, sub), :] += lax.dot_general(pt.astype(CDT), do_t, _NT, preferred_element_type=F32)
                dst = (pt * (dpt - delta)).astype(CDT)
                dk_ref[pl.ds(o, sub), :] += lax.dot_general(dst, q_t, _NT, preferred_element_type=F32)
                dq = dq + jnp.dot(kT_ref[:, pl.ds(o, sub)], dst, preferred_element_type=F32)
            return dq

        dq = tile(0, m_ctx, jnp.zeros((hd, r), F32))
        dq = lax.fori_loop(0, jnp.where(i < nqc, 0, n_lat_tiles),
                           lambda j, acc: tile(pl.multiple_of(m_ctx + j * tk, ATTN_TK), tk, acc), dq)
        dq_ref[...] = _heads_rows(dq * LN2)

    blk_t = pl.BlockSpec((None, None, hd, r), lambda g, i: (g, i, 0, 0))
    row = pl.BlockSpec((None, None, 1, r), lambda g, i: (g, i, 0, 0))
    kv = pl.BlockSpec((None, t, hd), lambda g, i: (g, 0, 0))
    rows = pl.BlockSpec((tq, 4 * hd), lambda g, i: (i, g))
    return pl.pallas_call(
        body, name=name, grid=(2, nq),
        in_specs=[blk_t, rows, blk_t, row, kv, pl.BlockSpec((None, hd, t), lambda g, i: (g, 0, 0)), kv],
        out_specs=[rows, kv, kv],
        out_shape=[jax.ShapeDtypeStruct((t, 8 * hd), F32), jax.ShapeDtypeStruct(kk.shape, F32),
                   jax.ShapeDtypeStruct(kk.shape, F32)],
        compiler_params=_cp(("parallel", "arbitrary")))(qT, do, oT, lse, kk, kT, vv)


def _split_kv(a):
    return a.reshape(a.shape[0], 2, ATTN_HD).transpose(1, 0, 2)


def _merge_kv(a):
    return a.transpose(1, 0, 2).reshape(a.shape[1], 2 * ATTN_HD)


def _chunk_order(rev, ncc, nct):
    if not rev:
        return lambda s: s
    return lambda s: jnp.where(s < ncc, ncc - 1 - s, nct - 1 - (s - ncc))


def scan_fwd(name, make_fn, rows, shared, n_state, y_width, n_rows, m_ctx):
    nct, ncc = n_rows // CHUNK, m_ctx // CHUNK
    orders = [_chunk_order(rev, ncc, nct) for rev in (False, True)]
    fns = [make_fn(0), make_fn(1)]
    nr, ns = len(rows), len(shared)

    def body(*refs):
        svals = [r[...] for r in refs[2 * nr:2 * nr + ns]]
        y_refs, sin_refs, st = refs[2 * nr + ns:2 * nr + ns + 2], refs[2 * nr + ns + 2:2 * nr + ns + 4], refs[-1]

        @pl.when(pl.program_id(0) == 0)
        def _():
            st[...] = jnp.zeros_like(st)

        for d in range(2):
            rvals = [r[...] for r in refs[d * nr:(d + 1) * nr]]
            prev = [st[d, k] for k in range(n_state)]
            sin_refs[d][...] = st[d]
            y, new = fns[d](rvals, svals, prev)
            y_refs[d][...] = y
            for k in range(n_state):
                st[d, k] = new[k]

    arrs, specs = [], []
    for order in orders:
        for it in rows:
            a, s = _row_in(it, CHUNK, order)
            arrs.append(a)
            specs.append(s)
    for s_ in shared:
        arrs.append(s_)
        specs.append(_const_spec(s_.shape))
    return pl.pallas_call(
        body, name=name, grid=(nct,), in_specs=specs,
        out_specs=[pl.BlockSpec((CHUNK, y_width), lambda s, o=o: (o(s), 0)) for o in orders]
        + [pl.BlockSpec((None, n_state, LANES, LANES), lambda s, o=o: (o(s), 0, 0, 0)) for o in orders],
        out_shape=[jax.ShapeDtypeStruct((n_rows, y_width), F32)] * 2
        + [jax.ShapeDtypeStruct((nct, n_state, LANES, LANES), F32)] * 2,
        scratch_shapes=[pltpu.VMEM((2, n_state, LANES, LANES), F32)],
        compiler_params=_cp(("arbitrary",)))(*arrs)


def scan_bwd(name, make_fn, rows, shared, states_in, dy, post, outs, n_state, n_rows, m_ctx, dirs=(0, 1)):
    nct, ncc = n_rows // CHUNK, m_ctx // CHUNK
    orders = [(lambda r, f=_chunk_order(d == 1, ncc, nct): f(nct - 1 - r)) for d in dirs]
    fns = [make_fn(d) for d in dirs]
    nd = len(dirs)
    nr, ns, no = len(rows), len(shared), len(outs)
    n_in = nd * nr + ns

    def body(*refs):
        svals = [r[...] for r in refs[nd * nr:n_in]]
        sin_refs, dy_refs = refs[n_in:n_in + nd], refs[n_in + nd:n_in + 2 * nd]
        out_refs = refs[n_in + 2 * nd:n_in + 2 * nd + nd * no]
        dsh_refs = refs[n_in + 2 * nd + nd * no:-1]
        dst = refs[-1]
        r = pl.program_id(0)

        @pl.when(r == 0)
        def _():
            dst[...] = jnp.zeros_like(dst)

        d_shared = None
        for d in range(nd):
            rvals = [x[...] for x in refs[d * nr:(d + 1) * nr]]
            prev = [sin_refs[d][k] for k in range(n_state)]
            _, vjp = jax.vjp(fns[d], rvals, svals, prev)
            d_rows, d_sh, d_prev = vjp((dy_refs[d][...], [dst[d, k] for k in range(n_state)]))
            for ref, val in zip(out_refs[d * no:(d + 1) * no], post(d_rows)):
                ref[...] = val.astype(ref.dtype)
            d_shared = d_sh if d_shared is None else [a + b for a, b in zip(d_shared, d_sh)]
            for k in range(n_state):
                dst[d, k] = d_prev[k]
        for ref, gr in zip(dsh_refs, d_shared):
            @pl.when(r == 0)
            def _(ref=ref, gr=gr):
                ref[...] = gr

            @pl.when(r != 0)
            def _(ref=ref, gr=gr):
                ref[...] += gr

    arrs, specs = [], []
    for order in orders:
        for it in rows:
            a, s = _row_in(it, CHUNK, order)
            arrs.append(a)
            specs.append(s)
    for s_ in shared:
        arrs.append(s_)
        specs.append(_const_spec(s_.shape))
    for sin, order in zip(states_in, orders):
        arrs.append(sin)
        specs.append(pl.BlockSpec((None, n_state, LANES, LANES), lambda r, o=order: (o(r), 0, 0, 0)))
    for order in orders:
        a, s = _row_in(dy, CHUNK, order)
        arrs.append(a)
        specs.append(s)
    out_specs = [pl.BlockSpec((CHUNK, w), lambda r, o=o: (o(r), 0)) for o in orders for w, _ in outs]
    out_shape = [jax.ShapeDtypeStruct((n_rows, w), dt) for _ in orders for w, dt in outs]
    for s_ in shared:
        out_specs.append(_const_spec(s_.shape))
        out_shape.append(jax.ShapeDtypeStruct(s_.shape, F32))
    res = pl.pallas_call(body, name=name, grid=(nct,), in_specs=specs, out_specs=out_specs, out_shape=out_shape,
                         scratch_shapes=[pltpu.VMEM((nd, n_state, LANES, LANES), F32)],
                         compiler_params=_cp(("arbitrary",)))(*arrs)
    return [list(res[d * no:(d + 1) * no]) for d in range(nd)] + [list(res[nd * no:])]


def _make_ssd_chunk(direction):
    rev = direction == 1
    base = 8 * direction

    def fn(rows, shared, prev):
        xs, bms, cms, dtraw = rows[0:4], rows[4:6], rows[6:8], rows[8]
        dt_bias, a_log = shared
        ln = dtraw.shape[0]
        dt_all = _softplus(dtraw + dt_bias)
        a_all = dt_all * (-jnp.exp(a_log))
        r_i = lax.broadcasted_iota(jnp.int32, (ln, ln), 0)
        c_i = lax.broadcasted_iota(jnp.int32, (ln, ln), 1)
        tri = (r_i <= c_i) if rev else (r_i >= c_i)
        a_cum_all = jnp.dot(tri.astype(F32), a_all, precision=lax.Precision.HIGHEST, preferred_element_type=F32)
        a_tot_all = jnp.sum(a_all, axis=0, keepdims=True)
        first = lax.broadcasted_iota(jnp.int32, (ln, LANES), 1) < SSD_HD
        first_row = lax.broadcasted_iota(jnp.int32, (LANES, 1), 0) < SSD_HD

        def lmat(acol):
            a_b = jnp.broadcast_to(acol, (ln, ln))
            seg = a_b - a_b.T
            return jnp.where(tri, jnp.exp(jnp.where(tri, seg, 0.0)), 0.0)

        ys, new = [], []
        for g in range(2):
            bm, cm = bms[g], cms[g]
            cb = _mxu(cm, bm, _NT)
            for jj in range(2):
                pr = 2 * g + jj
                h0, h1 = base + 2 * pr, base + 2 * pr + 1
                ac0, ac1 = _col(a_cum_all, h0), _col(a_cum_all, h1)
                at0, at1 = _col(a_tot_all, h0), _col(a_tot_all, h1)
                dt_pair = jnp.where(first, _col(dt_all, h0), _col(dt_all, h1))
                acum_pair = jnp.where(first, ac0, ac1)
                atot_pair = jnp.where(first[0:1], at0, at1)
                xd = xs[pr] * dt_pair
                st = _mxu(xd * jnp.exp(atot_pair - acum_pair), bm, _TN)
                new.append(prev[pr] * jnp.where(first_row, jnp.exp(at0), jnp.exp(at1)) + st)
                y0 = _mxu(cb * lmat(ac0), xd)
                y1 = _mxu(cb * lmat(ac1), xd)
                y_off = _mxu(cm, prev[pr], _NT) * jnp.exp(acum_pair)
                ys.append(jnp.where(first, y0, y1) + y_off)
        return jnp.concatenate(ys, axis=1), new

    return fn


def _make_ret_chunk(direction):
    rev = direction == 1
    base = 4 * direction

    def fn(rows, shared, prev):
        qs, ks, vs = rows[0:4], rows[4:8], rows[8:12]
        lg_all = -jnp.exp(shared[0])
        ln = qs[0].shape[0]
        pos = lax.broadcasted_iota(jnp.int32, (ln, 1), 0).astype(F32)
        r_i = lax.broadcasted_iota(jnp.int32, (ln, ln), 0)
        c_i = lax.broadcasted_iota(jnp.int32, (ln, ln), 1)
        diff = ((c_i - r_i) if rev else (r_i - c_i))
        mask = diff >= 0
        dpos = jnp.maximum(diff, 0).astype(F32)
        k_pow = pos if rev else (ln - 1.0 - pos)
        q_pow = (ln - pos) if rev else (pos + 1.0)
        ys, new = [], []
        for h in range(RET_HEADS):
            lg = _col(lg_all, base + h)
            dmat = jnp.where(mask, jnp.exp(dpos * lg), 0.0)
            st = _mxu(ks[h] * jnp.exp(k_pow * lg), vs[h], _TN)
            new.append(prev[h] * jnp.exp(ln * lg) + st)
            s = _mxu(qs[h], ks[h], _NT) * dmat
            ys.append(_mxu(s, vs[h]) + _mxu(qs[h], prev[h]) * jnp.exp(q_pow * lg))
        return jnp.concatenate(ys, axis=1), new

    return fn


def _conv_pre(x, w, b, t_idx, n_rows, m_ctx):
    is_start = (t_idx == 0) | (t_idx == m_ctx)
    is_end = (t_idx == m_ctx - 1) | (t_idx == n_rows - 1)
    xp = jnp.where(is_start, 0.0, pltpu.roll(x, 1, axis=0))
    xn = jnp.where(is_end, 0.0, pltpu.roll(x, n_rows - 1, axis=0))
    return w[0:1] * xp + w[1:2] * x + w[2:3] * xn + b, xp, xn, is_start, is_end


def conv_fwd(x, conv_w, conv_b, m_ctx):
    n_rows, width = x.arr.shape[0], x.width
    c0 = x.off // LANES

    def body(x_ref, w_ref, b_ref, o_ref):
        t_idx = lax.broadcasted_iota(jnp.int32, (n_rows, 1), 0)
        pre = _conv_pre(x_ref[...], w_ref[...], b_ref[...], t_idx, n_rows, m_ctx)[0]
        o_ref[...] = pre * jax.nn.sigmoid(pre)

    return pl.pallas_call(
        body, name="conv_fwd", grid=(width // LANES,),
        in_specs=[pl.BlockSpec((n_rows, LANES), lambda c: (0, c0 + c)),
                  pl.BlockSpec((3, LANES), lambda c: (0, c)), pl.BlockSpec((1, LANES), lambda c: (0, c))],
        out_specs=pl.BlockSpec((n_rows, LANES), lambda c: (0, c)),
        out_shape=jax.ShapeDtypeStruct((n_rows, width), F32),
        compiler_params=_cp(("parallel",)))(x.arr, conv_w, conv_b)


def conv_bwd(x, conv_w, conv_b, dy_a, dy_b, dxs_extra, m_ctx):
    n_rows, width = x.arr.shape[0], x.width
    c0 = x.off // LANES
    n_extra = dxs_extra.shape[1] // LANES

    def body(x_ref, w_ref, b_ref, dya_ref, dyb_ref, ex_ref, dx_ref, dw_ref, db_ref):
        c = pl.program_id(0)
        t_idx = lax.broadcasted_iota(jnp.int32, (n_rows, 1), 0)
        w = w_ref[...]
        pre, xp, xn, is_start, is_end = _conv_pre(x_ref[...], w, b_ref[...], t_idx, n_rows, m_ctx)
        sg = jax.nn.sigmoid(pre)
        dyv = dya_ref[...] + dyb_ref[...] + jnp.where(c < n_extra, ex_ref[...], 0.0)
        dpre = dyv * (sg * (1.0 + pre * (1.0 - sg)))
        d_next = jnp.where(is_end, 0.0, pltpu.roll(dpre, n_rows - 1, axis=0))
        d_prev = jnp.where(is_start, 0.0, pltpu.roll(dpre, 1, axis=0))
        dx_ref[...] = (w[1:2] * dpre + w[0:1] * d_next + w[2:3] * d_prev).astype(dx_ref.dtype)
        dw_ref[...] = jnp.concatenate([jnp.sum(dpre * xp, axis=0, keepdims=True),
                                       jnp.sum(dpre * x_ref[...], axis=0, keepdims=True),
                                       jnp.sum(dpre * xn, axis=0, keepdims=True)], axis=0)
        db_ref[...] = jnp.sum(dpre, axis=0, keepdims=True)

    return pl.pallas_call(
        body, name="conv_bwd", grid=(width // LANES,),
        in_specs=[pl.BlockSpec((n_rows, LANES), lambda c: (0, c0 + c)),
                  pl.BlockSpec((3, LANES), lambda c: (0, c)), pl.BlockSpec((1, LANES), lambda c: (0, c)),
                  pl.BlockSpec((n_rows, LANES), lambda c: (0, c)), pl.BlockSpec((n_rows, LANES), lambda c: (0, c)),
                  pl.BlockSpec((n_rows, LANES), lambda c: (0, jnp.minimum(c, n_extra - 1)))],
        out_specs=[pl.BlockSpec((n_rows, LANES), lambda c: (0, c)),
                   pl.BlockSpec((3, LANES), lambda c: (0, c)), pl.BlockSpec((1, LANES), lambda c: (0, c))],
        out_shape=[jax.ShapeDtypeStruct((n_rows, width), CDT), jax.ShapeDtypeStruct((3, width), F32),
                   jax.ShapeDtypeStruct((1, width), F32)],
        compiler_params=_cp(("parallel",)))(x.arr, conv_w, conv_b, dy_a, dy_b, dxs_extra)


def loss_head(h, target, final_w, m_ctx):
    n_rows, d = h.shape
    tm = min(ROW_TILE, n_rows)
    nb_ctx = m_ctx // tm

    def f(hb, w, tgt):
        err = _rms(hb, w) - tgt
        return 0.5 * jnp.sum(jnp.mean(err * err, axis=-1))

    def body(h_ref, t_ref, w_ref, loss_ref, dh_ref, dw_ref):
        i = pl.program_id(0)

        @pl.when(i < nb_ctx)
        def _():
            dh_ref[...] = jnp.zeros_like(dh_ref)

        @pl.when(i == 0)
        def _():
            loss_ref[...] = jnp.zeros_like(loss_ref)
            dw_ref[...] = jnp.zeros_like(dw_ref)

        @pl.when(i >= nb_ctx)
        def _():
            val, vjp = jax.vjp(lambda hb, w: f(hb, w, t_ref[...]), h_ref[...], w_ref[...])
            dh, dw = vjp(jnp.ones((), F32))
            dh_ref[...] = dh
            dw_ref[...] += dw
            loss_ref[...] += jnp.broadcast_to(val, loss_ref.shape)

    return pl.pallas_call(
        body, name="loss_head", grid=(n_rows // tm,),
        in_specs=[pl.BlockSpec((tm, d), lambda i: (i, 0)),
                  pl.BlockSpec((tm, d), lambda i: (jnp.maximum(i - nb_ctx, 0), 0)), _const_spec((1, d))],
        out_specs=[_const_spec((1, LANES)), pl.BlockSpec((tm, d), lambda i: (i, 0)), _const_spec((1, d))],
        out_shape=[jax.ShapeDtypeStruct((1, LANES), F32), jax.ShapeDtypeStruct((n_rows, d), F32),
                   jax.ShapeDtypeStruct((1, d), F32)],
        compiler_params=_cp(("arbitrary",)))(h, target, final_w)


def adamw(name, w, m, v, g_parts):
    lead, rows, cols = w.shape
    tr = _pick(rows, (256, 128, 64, 32, 16, 8))
    npart = len(g_parts)
    c1 = 1.0 - ADAM_B1 ** ADAM_STEP
    c2 = 1.0 - ADAM_B2 ** ADAM_STEP

    def body(*refs):
        w_ref, m_ref, v_ref = refs[:3]
        g = refs[3][...].astype(F32)
        for r in refs[4:3 + npart]:
            g = g + r[...].astype(F32)
        g_ref, d_ref, nm_ref, nv_ref = refs[3 + npart:]
        nm = ADAM_B1 * m_ref[...] + (1.0 - ADAM_B1) * g
        nv = ADAM_B2 * v_ref[...] + (1.0 - ADAM_B2) * (g * g)
        g_ref[...] = g
        nm_ref[...] = nm
        nv_ref[...] = nv
        d_ref[...] = -ADAM_LR * ((nm / c1) / (jnp.sqrt(nv / c2) + ADAM_EPS) + ADAM_WD * w_ref[...])

    spec = pl.BlockSpec((None, tr, cols), lambda l, i: (l, i, 0))
    return pl.pallas_call(
        body, name=name, grid=(lead, rows // tr), in_specs=[spec] * (3 + npart), out_specs=[spec] * 4,
        out_shape=[jax.ShapeDtypeStruct(w.shape, F32)] * 4,
        compiler_params=_cp(("parallel", "parallel")))(w, m, v, *g_parts)


MESH = pl.DeviceIdType.MESH
_HBM = pl.BlockSpec(memory_space=pl.ANY)


def _chip_peers():
    x, y, c = lax.axis_index("x"), lax.axis_index("y"), lax.axis_index("c")
    return x, y, c, [(1 - x, y), (x, 1 - y), (1 - x, 1 - y)]


def _window(ref, kind, chip, rows, cols):
    if kind == "cols":
        return ref.at[:, pl.ds(pl.multiple_of(chip * cols, LANES), cols)]
    if kind == "rows":
        return ref.at[pl.ds(pl.multiple_of(chip * rows, 8), rows), :]
    return ref.at[chip]


def _gathered_shape(kind, rows, cols):
    return {"cols": (rows, 4 * cols), "rows": (4 * rows, cols), "slices": (4, rows, cols)}[kind]


def gather_layers(name, shards, kinds):
    n = len(shards)

    def body(*refs):
        x_refs, o_refs = refs[:n], refs[n:2 * n]
        send_sems, recv_sems, local_sems = refs[2 * n:]
        x, y, c, peers = _chip_peers()
        me = 2 * x + y
        started = []
        for a in range(n):
            _, rows, cols = shards[a].shape
            src = x_refs[a].at[c]
            mine = pltpu.make_async_copy(src, _window(o_refs[a], kinds[a], me, rows, cols), local_sems.at[a])
            mine.start()
            started.append(mine.wait)
            for k, (px, py) in enumerate(peers):
                cp = pltpu.make_async_remote_copy(
                    src_ref=src, dst_ref=_window(o_refs[a], kinds[a], me, rows, cols), send_sem=send_sems.at[3 * a + k],
                    recv_sem=recv_sems.at[3 * a + k], device_id=(px, py, c), device_id_type=MESH)
                cp.start()
                started.append(cp.wait_send)
        for a in range(n):
            _, rows, cols = shards[a].shape
            for k, (px, py) in enumerate(peers):
                pltpu.make_async_remote_copy(
                    src_ref=x_refs[a].at[c], dst_ref=_window(o_refs[a], kinds[a], 2 * px + py, rows, cols),
                    send_sem=send_sems.at[3 * a + k], recv_sem=recv_sems.at[3 * a + k], device_id=(px, py, c),
                    device_id_type=MESH).wait_recv()
        for wait in started:
            wait()

    return pl.pallas_call(
        body, name=name, in_specs=[_HBM] * n, out_specs=[_HBM] * n,
        out_shape=[jax.ShapeDtypeStruct(_gathered_shape(kinds[a], *shards[a].shape[1:]), shards[a].dtype)
                   for a in range(n)],
        scratch_shapes=[pltpu.SemaphoreType.DMA((3 * n,)), pltpu.SemaphoreType.DMA((3 * n,)),
                        pltpu.SemaphoreType.DMA((n,))],
        )(*shards)


def scatter_pieces(name, pieces):
    n = len(pieces)

    def body(*refs):
        p_refs, o_refs = refs[:n], refs[n:2 * n]
        send_sems, recv_sems, local_sems = refs[2 * n:]
        x, y, c, peers = _chip_peers()
        me = 2 * x + y
        started = []
        for a in range(n):
            mine = pltpu.make_async_copy(p_refs[a].at[me], o_refs[a].at[me], local_sems.at[a])
            mine.start()
            started.append(mine.wait)
            for k, (px, py) in enumerate(peers):
                cp = pltpu.make_async_remote_copy(
                    src_ref=p_refs[a].at[2 * px + py], dst_ref=o_refs[a].at[me], send_sem=send_sems.at[3 * a + k],
                    recv_sem=recv_sems.at[3 * a + k], device_id=(px, py, c), device_id_type=MESH)
                cp.start()
                started.append(cp.wait_send)
        for a in range(n):
            for k, (px, py) in enumerate(peers):
                pltpu.make_async_remote_copy(
                    src_ref=p_refs[a].at[me], dst_ref=o_refs[a].at[2 * px + py], send_sem=send_sems.at[3 * a + k],
                    recv_sem=recv_sems.at[3 * a + k], device_id=(px, py, c), device_id_type=MESH).wait_recv()
        for wait in started:
            wait()

    return pl.pallas_call(
        body, name=name, in_specs=[_HBM] * n, out_specs=[_HBM] * n,
        out_shape=[jax.ShapeDtypeStruct(p.shape, p.dtype) for p in pieces],
        scratch_shapes=[pltpu.SemaphoreType.DMA((3 * n,)), pltpu.SemaphoreType.DMA((3 * n,)),
                        pltpu.SemaphoreType.DMA((n,))],
        )(*pieces)


def _pair_step(n_steps, x_ref, land, send_sems, recv_sems, credits, consume):
    x, y, c = lax.axis_index("x"), lax.axis_index("y"), lax.axis_index("c")
    sib = (x, y, 1 - c)
    i = pl.program_id(0)
    slot = i % 2

    @pl.when(i >= 2)
    def _():
        pl.semaphore_wait(credits.at[slot], 1)

    cp = pltpu.make_async_remote_copy(src_ref=x_ref, dst_ref=land.at[slot], send_sem=send_sems.at[slot],
                                      recv_sem=recv_sems.at[slot], device_id=sib, device_id_type=MESH)
    cp.start()
    cp.wait_recv()
    consume(land[slot])

    @pl.when(i < n_steps - 2)
    def _():
        pl.semaphore_signal(credits.at[slot], inc=1, device_id=sib, device_id_type=MESH)

    cp.wait_send()


def _pair_call(name, body, n_steps, in_specs, out_spec, out_shape, blk_shape, dtype, operands, extra_scratch=()):
    grid_spec = pltpu.PrefetchScalarGridSpec(
        num_scalar_prefetch=1, grid=(n_steps,), in_specs=in_specs, out_specs=out_spec,
        scratch_shapes=[pltpu.VMEM((2,) + blk_shape, dtype), pltpu.SemaphoreType.DMA((2,)),
                        pltpu.SemaphoreType.DMA((2,)), pltpu.SemaphoreType.REGULAR((2,)), *extra_scratch])
    return pl.pallas_call(body, name=name, grid_spec=grid_spec, out_shape=out_shape,
                          compiler_params=_cp(("arbitrary",)))(*operands)


def _place():
    return jnp.stack([lax.axis_index("x"), lax.axis_index("y"), lax.axis_index("c")]).astype(jnp.int32)


def _pair_rows(rows, row_bytes):
    for cand in (4096, 2048, 1024, 768, 512, 384, 256, 192, 128, 96, 64, 48, 32, 16):
        if rows % cand == 0 and cand * row_bytes <= PAIR_BLOCK_BYTES:
            return cand
    return _pick(rows, (16, 8))


def exchange_both(name, mine):
    rows, cols = mine.shape
    tr = _pair_rows(rows, cols * mine.dtype.itemsize)
    n_steps = rows // tr

    def body(s_ref, x_ref, o_ref, land, send_sems, recv_sems, credits):
        c = lax.axis_index("c")
        o_ref[c] = x_ref[...]

        def consume(v):
            o_ref[1 - c] = v
        _pair_step(n_steps, x_ref, land, send_sems, recv_sems, credits, consume)

    return _pair_call(name, body, n_steps, [pl.BlockSpec((tr, cols), lambda i, s: (i, 0))],
                      pl.BlockSpec((2, tr, cols), lambda i, s: (0, i, 0)),
                      jax.ShapeDtypeStruct((2, rows, cols), mine.dtype), (tr, cols), mine.dtype, (_place(), mine))


def exchange_add(name, layer0, layer1):
    rows, cols = layer0.shape
    tr = _pair_rows(rows, cols * layer0.dtype.itemsize)
    nb = rows // tr

    def body(s_ref, l0_ref, l1_ref, o_ref, land, send_sems, recv_sems, credits, send_buf):
        first = lax.axis_index("c") == 0
        send_buf[...] = jnp.where(first, l1_ref[...], l0_ref[...])
        mine = jnp.where(first, l0_ref[...], l1_ref[...]).astype(F32)

        def consume(v):
            o_ref[...] = (mine + v.astype(F32)).astype(o_ref.dtype)
        _pair_step(nb, send_buf, land, send_sems, recv_sems, credits, consume)

    spec = pl.BlockSpec((tr, cols), lambda i, s: (i, 0))
    return _pair_call(name, body, nb, [spec, spec], spec, jax.ShapeDtypeStruct((rows, cols), CDT),
                      (tr, cols), layer0.dtype, (_place(), layer0, layer1),
                      extra_scratch=(pltpu.VMEM((tr, cols), layer0.dtype),))


def sum_exchange(name, parts):
    npart, rows, cols = parts.shape
    tr = _pair_rows(rows, cols * 4)
    n_steps = rows // tr

    def body(s_ref, x_ref, o_ref, land, send_sems, recv_sems, credits, mine):
        c = lax.axis_index("c")
        acc = x_ref[0].astype(F32)
        for k in range(1, npart):
            acc = acc + x_ref[k].astype(F32)
        mine[...] = acc
        o_ref[c] = acc

        def consume(v):
            o_ref[1 - c] = v
        _pair_step(n_steps, mine, land, send_sems, recv_sems, credits, consume)

    return _pair_call(name, body, n_steps, [pl.BlockSpec((npart, tr, cols), lambda i, s: (0, i, 0))],
                      pl.BlockSpec((2, tr, cols), lambda i, s: (0, i, 0)),
                      jax.ShapeDtypeStruct((2, rows, cols), F32), (tr, cols), F32, (_place(), parts),
                      extra_scratch=(pltpu.VMEM((tr, cols), F32),))


def allreduce_small(name, buf):
    rows = buf.shape[0]

    def body(x_ref, out_ref, gath, send_sems, recv_sems):
        x, y, c = lax.axis_index("x"), lax.axis_index("y"), lax.axis_index("c")
        me = 4 * x + 2 * y + c
        masks = [(k >> 2 & 1, k >> 1 & 1, k & 1) for k in range(1, 8)]

        def flip(v, bit):
            return 1 - v if bit else v

        sends = []
        for k, (bx, by, bc) in enumerate(masks):
            cp = pltpu.make_async_remote_copy(src_ref=x_ref, dst_ref=gath.at[me], send_sem=send_sems.at[k],
                                              recv_sem=recv_sems.at[k],
                                              device_id=(flip(x, bx), flip(y, by), flip(c, bc)), device_id_type=MESH)
            cp.start()
            sends.append(cp)
        gath[me] = x_ref[...]
        for k, (bx, by, bc) in enumerate(masks):
            px, py, pc = flip(x, bx), flip(y, by), flip(c, bc)
            pltpu.make_async_remote_copy(src_ref=x_ref, dst_ref=gath.at[4 * px + 2 * py + pc],
                                         send_sem=send_sems.at[k], recv_sem=recv_sems.at[k],
                                         device_id=(px, py, pc), device_id_type=MESH).wait_recv()
        for cp in sends:
            cp.wait_send()
        acc = gath[0]
        for d in range(1, 8):
            acc = acc + gath[d]
        out_ref[...] = acc

    return pl.pallas_call(
        body, name=name, in_specs=[pl.BlockSpec(memory_space=pltpu.VMEM)],
        out_specs=pl.BlockSpec(memory_space=pltpu.VMEM), out_shape=jax.ShapeDtypeStruct(buf.shape, F32),
        scratch_shapes=[pltpu.VMEM((8, rows, LANES), F32), pltpu.SemaphoreType.DMA((7,)),
                        pltpu.SemaphoreType.DMA((7,))],
        )(buf)


def _pack_flat(arrs, dtype, width, row_mult=8):
    flat = jnp.concatenate([a.reshape(-1).astype(dtype) for a in arrs])
    pad = (-flat.shape[0]) % (row_mult * width)
    if pad:
        flat = jnp.concatenate([flat, jnp.zeros((pad,), dtype)])
    return flat.reshape(-1, width)


def _unpack_flat(buf, shapes):
    flat = buf.reshape(-1)
    out, off = [], 0
    for s in shapes:
        n = math.prod(s)
        out.append(flat[off:off + n].reshape(s))
        off += n
    return out


def _in_to_padded(w):
    parts = []
    for name in IN_NEW_ORDER:
        _, width, o_off, o_w = IN_LAYOUT[name]
        parts.append(w[..., o_off:o_off + o_w])
        if o_w < width:
            parts.append(jnp.zeros(w.shape[:-1] + (width - o_w,), w.dtype))
    used = sum(IN_LAYOUT[n][1] for n in IN_NEW_ORDER)
    parts.append(jnp.zeros(w.shape[:-1] + (IN_PAD - used,), w.dtype))
    return jnp.concatenate(parts, axis=-1)


def _in_from_padded(g):
    parts = []
    for name in IN_ORIG_ORDER:
        off, _, _, o_w = IN_LAYOUT[name]
        parts.append(g[..., off:off + o_w])
    return jnp.concatenate(parts, axis=-1)


def _pcol(p, name):
    off, width, _, _ = IN_LAYOUT[name]
    return Cols(p, off, width)


def _lane_pad(v, width=LANES):
    v = v.reshape(-1)
    return jnp.concatenate([v, jnp.zeros((width - v.shape[0],), v.dtype)]).reshape(1, width)


def _f_norm_mod(h, sh, sc, w):
    return (_rms(h, w) * (1.0 + sc) + sh,)


def _f_norm_mod_thru(h, sh, sc, w):
    return h, _rms(h, w) * (1.0 + sc) + sh


def _f_attn_prep(qraw, kraw, vraw, cos2, sin2, qw, kw, gq, gk):
    q = qraw * lax.rsqrt(_group_mean(qraw * qraw, gq) + NORM_EPS) * qw
    q = _rope32(q, jnp.tile(cos2, (1, 4)), jnp.tile(sin2, (1, 4))) * (ATTN_HD ** -0.5 * LOG2E)
    k = kraw * lax.rsqrt(_group_mean(kraw * kraw, gk) + NORM_EPS) * kw
    return q, _rope32(k, cos2, sin2), vraw


def _f_ssd_finish(yf, yb, xs, z, d_exp, nw):
    y = (yf + yb + d_exp * xs) * (z * jax.nn.sigmoid(z))
    return (_rms(y, nw),)


def _f_ret_prep(rq, rk, cos1, sin1):
    cos_full, sin_signed = jnp.tile(cos1, (1, 4)), jnp.tile(sin1, (1, 4))
    return _rope64(rq, cos_full, sin_signed), _rope64(rk, cos_full, sin_signed) * (RET_DK ** -0.5)


def _f_ret_finish(yf, yb, g, gw):
    y = yf + yb
    outs = []
    for h in range(RET_HEADS):
        yh = y[:, h * RET_DK:(h + 1) * RET_DK]
        yc = yh - jnp.mean(yh, axis=-1, keepdims=True)
        outs.append(yc * lax.rsqrt(jnp.mean(yc * yc, axis=-1, keepdims=True) + NORM_EPS))
    return (jnp.concatenate(outs, axis=1) * gw * (g * jax.nn.sigmoid(g)),)


def _f_merge(p0, p1, p2, g0, g1, g2):
    return (jax.nn.sigmoid(g0) * p0 + jax.nn.sigmoid(g1) * p1 + jax.nn.sigmoid(g2) * p2,)


def _f_mid(h, mix, g1, sh2, sc2, w2):
    h_mid = h + g1 * mix
    return h_mid, _rms(h_mid, w2) * (1.0 + sc2) + sh2


def _epi_merge_bwd(dmerged, p0, p1, p2, g0, g1, g2):
    dps, dgs = [], []
    for pb, gate in ((p0, g0), (p1, g1), (p2, g2)):
        s = jax.nn.sigmoid(gate)
        dps.append(dmerged * s)
        dgs.append(dmerged * pb.astype(F32) * (s * (1.0 - s)))
    return tuple(dps + dgs)


def _epi_mid(mix, h, g1, sh2, sc2, w2):
    return (mix,) + _f_mid(h, mix, g1, sh2, sc2, w2)


def _epi_resid(o, h_mid, g2):
    return (o,) + _f_residual(h_mid, o, g2)


def _epi_sqrelu(a):
    r = jnp.maximum(a, 0.0)
    return a, r * r


def _epi_sqrelu_bwd(dhh, a):
    return (dhh * (2.0 * jnp.maximum(a.astype(F32), 0.0)),)


def _f_residual(h_mid, o, g2):
    return (h_mid + g2 * o,)


def _f_silu(x):
    return (x * jax.nn.sigmoid(x),)


def _f_bias(x, b):
    return (x + b,)


def _ssd_rows(xbc, p):
    rows = [Cols(xbc, LANES * k, LANES) for k in range(4)]
    rows += [Cols(xbc, 512 + LANES * g, LANES) for g in range(2)]
    rows += [Cols(xbc, 768 + LANES * g, LANES) for g in range(2)]
    return rows + [_pcol(p, "dt")]


def _ret_rows(rq, rk, p):
    off_v = IN_LAYOUT["rv"][0]
    return ([Cols(rq, LANES * h, LANES) for h in range(4)] + [Cols(rk, LANES * h, LANES) for h in range(4)]
            + [Cols(p, off_v + LANES * h, LANES) for h in range(4)])


def layer_fwd(li, h, mod, lw, tabs, m_ctx):
    t = h.shape[0]
    nb = m_ctx // min(ROW_TILE, t)
    sh1, sc1, g1, sh2, sc2, g2 = mod
    nm = lambda s: f"l{li}_{s}"
    sv = {}
    (u,) = rowwise_fwd(nm("norm1"), _f_norm_mod, [h], [sh1, sc1], [lw["norm1_w"]], [(D_MODEL, CDT)], t, nb)
    p = mm(nm("in_proj"), u, lw["w_in"], F32)
    q, k, v = rowwise_fwd(
        nm("attn_prep"), _f_attn_prep,
        [_pcol(p, "q"), _pcol(p, "k"), _pcol(p, "v"), tabs["ca"], tabs["sa"]], [],
        [lw["qw"], lw["kw"], tabs["gq"], tabs["gk"]], [(512, CDT), (128, CDT), (128, CDT)], t, nb)
    tq = min(ATTN_TQ, m_ctx)
    kk, vv = _split_kv(k), _split_kv(v)
    ones_rows = jnp.concatenate([jnp.ones((2, 1, t), CDT), jnp.zeros((2, ATTN_ONES_ROWS - 1, t), CDT)], axis=1)
    attn_o, qT, oT, lse = attn_fwd(nm("attn"), q, kk, jnp.concatenate([vv.transpose(0, 2, 1), ones_rows], axis=1),
                                   m_ctx, tq)

    xbc = conv_fwd(_pcol(p, "xbc"), lw["conv_w"], lw["conv_b"], m_ctx)
    ssd_sh = [lw["dt_bias"], lw["a_log"]]
    yf, yb, sf, sb = scan_fwd(nm("ssd"), _make_ssd_chunk, _ssd_rows(xbc, p), ssd_sh, 4, 512, t, m_ctx)
    (ssd_o,) = rowwise_fwd(nm("ssd_fin"), _f_ssd_finish, [yf, yb, Cols(xbc, 0, 512), _pcol(p, "z")], [],
                           [lw["d_exp"], lw["ssd_nw"]], [(512, CDT)], t, nb)

    rq, rk = rowwise_fwd(nm("ret_prep"), _f_ret_prep, [_pcol(p, "rq"), _pcol(p, "rk"), tabs["rc"], tabs["rs"]],
                         [], [], [(512, F32), (512, F32)], t, nb)
    rf, rb, rsf, rsb = scan_fwd(nm("ret"), _make_ret_chunk, _ret_rows(rq, rk, p), [lw["ret_lg"]], 4, 512, t, m_ctx)
    (ret_o,) = rowwise_fwd(nm("ret_fin"), _f_ret_finish, [rf, rb, _pcol(p, "rg")], [], [lw["ret_gw"]],
                           [(512, CDT)], t, nb)

    pbs = [mm(nm(f"branch{b}"), br, lw["w_branch"][b], CDT) for b, br in enumerate((attn_o, ssd_o, ret_o))]
    gl = [Cols(p, 1024 * b, 1024) for b in range(3)]
    (merged,) = rowwise_fwd(nm("merge"), _f_merge, pbs + gl, [], [], [(D_MODEL, CDT)], t, nb)
    mix, h_mid, vv2 = mm(nm("out_proj"), merged, lw["w_out"], None, epilogue=_epi_mid, tile_ins=[h],
                         typed_ins=[g1, sh2, sc2], shared_ins=[lw["norm2_w"]], out_dtypes=[F32, F32, CDT], nb_ctx=nb,
                         whole_rows=True)
    a, hh = mm(nm("mlp1"), vv2, lw["w_mlp1"], None, epilogue=_epi_sqrelu, out_dtypes=[CDT, CDT])
    o, h_out = mm(nm("mlp2"), hh, lw["w_mlp2"], None, epilogue=_epi_resid, tile_ins=[h_mid], typed_ins=[g2],
                  out_dtypes=[F32, F32], nb_ctx=nb)
    sv.update(h=h, u=u, p=p, qT=qT, kk=kk, vv=vv, oT=oT, lse=lse, attn_o=attn_o, xbc=xbc, yf=yf, yb=yb,
              sf=sf, sb=sb, ssd_o=ssd_o, rq=rq, rk=rk, rf=rf, rb=rb, rsf=rsf, rsb=rsb, ret_o=ret_o, pbs=pbs,
              merged=merged, mix=mix, h_mid=h_mid, v=vv2, a=a, hh=hh, o=o)
    return h_out, sv


def layer_bwd(li, dh_out, sv, mod, lw, tabs, m_ctx):
    t = dh_out.shape[0]
    nb = m_ctx // min(ROW_TILE, t)
    sh1, sc1, g1, sh2, sc2, g2 = mod
    nm = lambda s: f"l{li}_{s}_bwd"
    gw = {}
    p = sv["p"]
    (do,), (dg2,), _ = rowwise_bwd(nm("resid"), _f_residual, [sv["h_mid"], sv["o"]], [g2], [], [dh_out],
                                   [False, True], [], [CDT], t, nb)
    (da,) = mm(nm("mlp2_dx"), do, lw["w_mlp2"], None, transpose_b=True, epilogue=_epi_sqrelu_bwd,
               tile_ins=[sv["a"]], out_dtypes=[CDT])
    gw["w_mlp2"] = mm_tn(nm("mlp2_dw"), sv["hh"], do, CDT, ("rows", lw["w_mlp2"].shape[0] // 4))
    dv = mm(nm("mlp1_dx"), da, lw["w_mlp1"], F32, transpose_b=True)
    gw["w_mlp1"] = mm_tn(nm("mlp1_dw"), sv["v"], da, CDT, ("cols", lw["w_mlp1"].shape[1] // 4))
    (dh_a, dmix), (dg1, dsh2, dsc2), (gw["norm2_w"],) = rowwise_bwd(
        nm("mid"), _f_mid, [sv["h"], sv["mix"]], [g1, sh2, sc2], [lw["norm2_w"]], [dh_out, dv],
        [True, True], [True], [F32, CDT], t, nb)
    gl = [Cols(p, 1024 * b, 1024) for b in range(3)]
    dmg = mm(nm("out_dx"), dmix, lw["w_out"], None, transpose_b=True, epilogue=_epi_merge_bwd,
             tile_ins=sv["pbs"] + gl, out_dtypes=[CDT] * 6)
    gw["w_out"] = mm_tn(nm("out_dw"), sv["merged"], dmix, CDT, ("rows", lw["w_out"].shape[0] // 4))
    dpb, dgl = dmg[:3], dmg[3:]
    brs = (sv["attn_o"], sv["ssd_o"], sv["ret_o"])
    d_attn_o = mm(nm("branch0_dx"), dpb[0], lw["w_branch"][0], CDT, transpose_b=True)
    d_ssd_o = mm(nm("branch1_dx"), dpb[1], lw["w_branch"][1], F32, transpose_b=True)
    d_ret_o = mm(nm("branch2_dx"), dpb[2], lw["w_branch"][2], F32, transpose_b=True)
    n_loc = lw["w_branch"].shape[2] // 4
    gw["w_branch"] = jnp.stack([mm_tn(nm(f"branch{b}_dw"), brs[b], dpb[b], CDT, ("cols", n_loc)) for b in range(3)],
                               axis=1).reshape(4, -1, n_loc)
    tq = min(ATTN_TQ, m_ctx)
    dq_rows, dk_s, dv_s = attn_bwd(nm("attn"), sv["qT"], d_attn_o, sv["oT"], sv["lse"], sv["kk"],
                                   sv["kk"].transpose(0, 2, 1), sv["vv"], m_ctx)
    (dq_raw, dk_raw, dv_raw), _, (gw["qw"], gw["kw"]) = rowwise_bwd(
        nm("attn_prep"), _f_attn_prep,
        [_pcol(p, "q"), _pcol(p, "k"), _pcol(p, "v"), tabs["ca"], tabs["sa"]], [],
        [lw["qw"], lw["kw"], tabs["gq"], tabs["gk"]],
        [dq_rows, _merge_kv(dk_s) * LN2, _merge_kv(dv_s)],
        [True, True, True, False, False], [True, True, False, False], [CDT] * 3, t, nb)
    (dy_ssd, dxs_fin, dz), _, (gw["d_exp"], gw["ssd_nw"]) = rowwise_bwd(
        nm("ssd_fin"), _f_ssd_finish, [sv["yf"], sv["yb"], Cols(sv["xbc"], 0, 512), _pcol(p, "z")], [],
        [lw["d_exp"], lw["ssd_nw"]], [d_ssd_o], [True, False, True, True], [True, True], [F32, F32, CDT], t, nb)
    ssd_sh = [lw["dt_bias"], lw["a_log"]]
    post_ssd = lambda d: [jnp.concatenate(d[0:8], axis=1), d[8]]
    (dxbc_f, ddt_f), dsh_f = scan_bwd(nm("ssd_f"), _make_ssd_chunk, _ssd_rows(sv["xbc"], p), ssd_sh, (sv["sf"],),
                                      dy_ssd, post_ssd, [(1024, F32), (LANES, F32)], 4, t, m_ctx, dirs=(0,))
    (dxbc_b, ddt_b), dsh_b = scan_bwd(nm("ssd_b"), _make_ssd_chunk, _ssd_rows(sv["xbc"], p), ssd_sh, (sv["sb"],),
                                      dy_ssd, post_ssd, [(1024, F32), (LANES, F32)], 4, t, m_ctx, dirs=(1,))
    gw["dt_bias"], gw["a_log"] = dsh_f[0] + dsh_b[0], dsh_f[1] + dsh_b[1]
    ddt = (ddt_f + ddt_b).astype(CDT)
    dxbc_raw, gw["conv_w"], gw["conv_b"] = conv_bwd(_pcol(p, "xbc"), lw["conv_w"], lw["conv_b"], dxbc_f, dxbc_b,
                                                    dxs_fin, m_ctx)
    (dy_ret, drg), _, (gw["ret_gw"],) = rowwise_bwd(
        nm("ret_fin"), _f_ret_finish, [sv["rf"], sv["rb"], _pcol(p, "rg")], [], [lw["ret_gw"]], [d_ret_o],
        [True, False, True], [True], [F32, CDT], t, nb)
    post_ret = lambda d: [jnp.concatenate(d[0:4], axis=1), jnp.concatenate(d[4:8], axis=1),
                          jnp.concatenate(d[8:12], axis=1)]
    rrows = _ret_rows(sv["rq"], sv["rk"], p)
    (dq_f, dk_f, dv_f), (dq_b, dk_b, dv_b), (gw["ret_lg"],) = scan_bwd(
        nm("ret"), _make_ret_chunk, rrows, [lw["ret_lg"]], (sv["rsf"], sv["rsb"]), dy_ret, post_ret,
        [(512, F32)] * 3, 4, t, m_ctx)
    drv = (dv_f + dv_b).astype(CDT)
    (drq, drk), _, _ = rowwise_bwd(nm("ret_prep"), _f_ret_prep,
                                   [_pcol(p, "rq"), _pcol(p, "rk"), tabs["rc"], tabs["rs"]], [], [],
                                   [(dq_f, dq_b), (dk_f, dk_b)], [True, True, False, False], [], [CDT, CDT], t, nb)
    pieces = {"gates": None, "xbc": dxbc_raw, "q": dq_raw, "z": dz, "rq": drq, "rk": drk, "rv": drv, "rg": drg,
              "k": dk_raw, "v": dv_raw, "dt": ddt}
    cols = list(dgl) + [pieces[n] for n in IN_NEW_ORDER[1:]]
    used = sum(c.shape[1] for c in cols)
    cols.append(jnp.zeros((t, IN_PAD - used), CDT))
    dp = jnp.concatenate(cols, axis=1)
    du = mm(nm("in_dx"), dp, lw["w_in"], F32, transpose_b=True)
    gw["w_in"] = mm_tn(nm("in_dw"), sv["u"], dp, CDT)
    (dh_in,), (dsh1, dsc1), (gw["norm1_w"],) = rowwise_bwd(
        nm("norm1"), _f_norm_mod_thru, [sv["h"]], [sh1, sc1], [lw["norm1_w"]], [dh_a, du], [True], [True], [F32],
        t, nb)
    return dh_in, [dsh1, dsc1, dg1, dsh2, dsc2, dg2], gw


def _rope_tables(n_lat, m_ctx):
    rows = n_lat // GRID_W
    row = jnp.repeat(jnp.arange(rows, dtype=F32), GRID_W)
    col = jnp.tile(jnp.arange(GRID_W, dtype=F32), rows)
    nfreq = ATTN_HD // 4
    inv = ROPE_THETA ** (-jnp.arange(nfreq, dtype=F32) / nfreq)
    ang = jnp.concatenate([row[:, None] * inv, col[:, None] * inv], axis=-1)
    cos = jnp.concatenate([jnp.ones((m_ctx, ATTN_HD // 2), F32), jnp.cos(ang)], axis=0)
    sin = jnp.concatenate([jnp.zeros((m_ctx, ATTN_HD // 2), F32), jnp.sin(ang)], axis=0)
    c64 = jnp.concatenate([cos, cos], axis=1)
    s64 = jnp.concatenate([-sin, sin], axis=1)
    pos = jnp.arange(m_ctx + n_lat, dtype=F32)
    inv_r = ROPE_THETA ** (-jnp.linspace(0.0, 1.0, RET_DK // 2, dtype=F32))
    ang_r = pos[:, None] * inv_r
    rc = jnp.concatenate([jnp.cos(ang_r)] * 2, axis=1)
    rs = jnp.concatenate([-jnp.sin(ang_r), jnp.sin(ang_r)], axis=1)
    return dict(ca=jnp.tile(c64, (1, 2)), sa=jnp.tile(s64, (1, 2)), rc=rc, rs=rs, gq=_group_matrix(512, ATTN_HD),
                gk=_group_matrix(128, ATTN_HD))


def _layer_weights(full, small, layer):
    return dict(
        w_in=full["w_in"][layer], w_branch=full["w_branch"][layer], w_out=full["w_out"][layer],
        w_mlp1=full["w_mlp1"][layer], w_mlp2=full["w_mlp2"][layer],
        norm1_w=small["norm1_w"][layer][None], norm2_w=small["norm2_w"][layer][None],
        qw=jnp.tile(small["attn_q_norm"][layer], 8)[None], kw=jnp.tile(small["attn_k_norm"][layer], 2)[None],
        conv_w=small["ssd_conv_w"][layer], conv_b=small["ssd_conv_b"][layer][None],
        dt_bias=_lane_pad(small["ssd_dt_bias"][layer]), a_log=_lane_pad(small["ssd_a_log"][layer]),
        d_exp=jnp.repeat(small["ssd_d"][layer], SSD_HD)[None], ssd_nw=small["ssd_norm_w"][layer][None],
        ret_lg=_lane_pad(small["ret_log_decay"][layer]), ret_gw=small["ret_gn_w"][layer][None])


def local_step(x, c, ctx, full, small, loss_target):
    n_lat, d = x.shape
    m_ctx = ctx.shape[0]
    t = n_lat + m_ctx
    depth = small["norm1_w"].shape[0]
    tabs = _rope_tables(n_lat, m_ctx)
    h = jnp.concatenate([ctx, x], axis=0)
    cc = jnp.concatenate([small["c_ctx"][None], c, jnp.zeros((COND_ROWS - 2, d), F32)], axis=0)
    (scc,) = rowwise_fwd("cond_silu", _f_silu, [cc], [], [], [(d, CDT)], COND_ROWS, 0)
    mods, saved, lws = [], [], []
    for layer in range(depth):
        lw = _layer_weights(full, small, layer)
        mod_raw = mm(f"l{layer}_mod", scc, full["w_mod"][layer], F32)
        (mod8,) = rowwise_fwd(f"l{layer}_mod_bias", _f_bias, [mod_raw], [], [small["b_mod"][layer][None]],
                              [(6 * d, F32)], COND_ROWS, 0)
        mod = [mod8[0:2, k * d:(k + 1) * d].reshape(2, 1, d) for k in range(6)]
        h, sv = layer_fwd(layer, h, mod, lw, tabs, m_ctx)
        mods.append(mod)
        saved.append(sv)
        lws.append(lw)
    loss, dh, d_final = loss_head(h, loss_target, small["final_norm_w"][None], m_ctx)

    gbig = {k: [None] * depth for k in BIG}
    gs = {k: [None] * depth for k in SMALL if k not in ("c_ctx", "final_norm_w")}
    d_scc = None
    for layer in reversed(range(depth)):
        lw = lws[layer]
        dh, dmod, gw = layer_bwd(layer, dh, saved[layer], mods[layer], lw, tabs, m_ctx)
        dmod8 = jnp.concatenate([jnp.concatenate([g_.reshape(2, d) for g_ in dmod], axis=1),
                                 jnp.zeros((COND_ROWS - 2, 6 * d), F32)], axis=0)
        (dmod_c,), _, (db_mod,) = rowwise_bwd(f"l{layer}_mod_bias_bwd", _f_bias, [dmod8], [],
                                              [small["b_mod"][layer][None]], [dmod8], [True], [True], [CDT], COND_ROWS, 0)
        gbig["w_mod"][layer] = mm_tn(f"l{layer}_mod_dw", scc, dmod_c, CDT, ("cols", 6 * d // 4))
        part = mm(f"l{layer}_mod_dx", dmod_c, full["w_mod"][layer], F32, transpose_b=True)
        d_scc = part if d_scc is None else d_scc + part
        g_in = _in_from_padded(gw["w_in"])
        gbig["w_in"][layer] = g_in.reshape(d, 4, g_in.shape[1] // 4).transpose(1, 0, 2)
        for k in ("w_branch", "w_out", "w_mlp1", "w_mlp2"):
            gbig[k][layer] = gw[k]
        gs["b_mod"][layer] = db_mod.reshape(-1)
        gs["norm1_w"][layer] = gw["norm1_w"].reshape(-1)
        gs["norm2_w"][layer] = gw["norm2_w"].reshape(-1)
        gs["attn_q_norm"][layer] = gw["qw"].reshape(8, ATTN_HD).sum(0)
        gs["attn_k_norm"][layer] = gw["kw"].reshape(2, ATTN_HD).sum(0)
        gs["ssd_conv_w"][layer] = gw["conv_w"]
        gs["ssd_conv_b"][layer] = gw["conv_b"].reshape(-1)
        gs["ssd_dt_bias"][layer] = gw["dt_bias"][0, :16].reshape(2, 8)
        gs["ssd_a_log"][layer] = gw["a_log"][0, :16].reshape(2, 8)
        gs["ssd_d"][layer] = gw["d_exp"].reshape(SSD_HEADS, SSD_HD).sum(1)
        gs["ssd_norm_w"][layer] = gw["ssd_nw"].reshape(-1)
        gs["ret_log_decay"][layer] = gw["ret_lg"][0, :8].reshape(2, 4)
        gs["ret_gn_w"][layer] = gw["ret_gw"].reshape(-1)
    (d_cc,), _, _ = rowwise_bwd("cond_silu_bwd", _f_silu, [cc], [], [], [d_scc], [True], [], [F32], COND_ROWS, 0)
    g_small = {k: jnp.stack(v) for k, v in gs.items()}
    g_small["c_ctx"] = d_cc[0]
    g_small["final_norm_w"] = d_final.reshape(-1)
    return loss, dh[m_ctx:], gbig, g_small


def kernel(x, c, ctx, c_ctx, w_mod, b_mod, norm1_w, norm2_w, w_in, attn_q_norm, attn_k_norm, ssd_conv_w, ssd_conv_b, ssd_dt_bias, ssd_a_log, ssd_d, ssd_norm_w, ret_log_decay, ret_gn_w, w_branch, w_out, w_mlp1, w_mlp2, final_norm_w, loss_target, m_c_ctx, m_w_mod, m_b_mod, m_norm1_w, m_norm2_w, m_w_in, m_attn_q_norm, m_attn_k_norm, m_ssd_conv_w, m_ssd_conv_b, m_ssd_dt_bias, m_ssd_a_log, m_ssd_d, m_ssd_norm_w, m_ret_log_decay, m_ret_gn_w, m_w_branch, m_w_out, m_w_mlp1, m_w_mlp2, m_final_norm_w, v_c_ctx, v_w_mod, v_b_mod, v_norm1_w, v_norm2_w, v_w_in, v_attn_q_norm, v_attn_k_norm, v_ssd_conv_w, v_ssd_conv_b, v_ssd_dt_bias, v_ssd_a_log, v_ssd_d, v_ssd_norm_w, v_ret_log_decay, v_ret_gn_w, v_w_branch, v_w_out, v_w_mlp1, v_w_mlp2, v_final_norm_w):
    env = dict(locals())
    w_loc = {k: env[k] for k in WEIGHTS}
    m_loc = {k: env["m_" + k] for k in WEIGHTS}
    v_loc = {k: env["v_" + k] for k in WEIGHTS}
    chip = 2 * lax.axis_index("x") + lax.axis_index("y")
    core = lax.axis_index("c")

    depth = w_loc["w_mod"].shape[0]
    assert depth == 2, "the exchanges split the layers between a chip's two cores"
    shards = [w_loc[k].astype(CDT).reshape(depth, -1, w_loc[k].shape[-1]) for k in BIG]
    mine = gather_layers("gather_weights", shards, [BIG_KIND[k] for k in BIG])
    full = {}
    for k, arr in zip(BIG, mine):
        both = exchange_both("share_" + k, arr.reshape(-1, arr.shape[-1]))
        if k == "w_in":
            both = both.reshape(depth, 4, -1, both.shape[-1]).transpose(0, 2, 1, 3)
            both = _in_to_padded(both.reshape(depth, both.shape[1], -1))
        full[k] = both.reshape((depth,) + w_loc[k].shape[1:-1] + (-1,)) if BIG_KIND[k] == "cols" else \
            both.reshape((depth,) + w_loc[k].shape[1:-2] + (-1, w_loc[k].shape[-1])) if BIG_KIND[k] == "rows" else both

    cw = w_loc["ssd_conv_w"]
    cw_w = cw.shape[-1]
    placed = lax.dynamic_update_slice(jnp.zeros(cw.shape[:-1] + (4 * cw_w,), F32),
                                      cw * (core == 0).astype(F32), (0, 0, chip * cw_w))
    conv_full = _unpack_flat(allreduce_small("gather_conv_w", _pack_flat([placed], F32, LANES)), [placed.shape])[0]
    small = {k: w_loc[k] for k in SMALL}
    small["ssd_conv_w"] = conv_full

    loss_l, grad_x, g_big, g_small = local_step(x[0], c, ctx[0], full, small, loss_target[0])

    small_shapes = [g_small[k].shape for k in SMALL] + [(LANES,)]
    summed = _unpack_flat(allreduce_small("reduce_small", _pack_flat([g_small[k] for k in SMALL] + [loss_l], F32, LANES)),
                          small_shapes)
    gsum = dict(zip(SMALL, summed[:-1]))
    loss = summed[-1][0]
    gsum["ssd_conv_w"] = lax.dynamic_slice(gsum["ssd_conv_w"], (0, 0, chip * cw_w), cw.shape)

    pair = []
    for k in BIG:
        _, rows, cols = g_big[k][0].shape
        pair.append(exchange_add("pair_" + k, g_big[k][0].reshape(4 * rows, cols),
                                 g_big[k][1].reshape(4 * rows, cols)).reshape(4, rows, cols))
    landed = scatter_pieces("scatter_grads", pair)
    g_sum = [sum_exchange("sum_" + k, parts) for k, parts in zip(BIG, landed)]

    grads, deltas, new_m, new_v = {}, {}, {}, {}
    for i, k in enumerate(BIG):
        shp = w_loc[k].shape
        three_d = lambda a, shp=shp: a.reshape((-1,) + shp[-2:])
        res = adamw("adamw_" + k, three_d(w_loc[k]), three_d(m_loc[k]), three_d(v_loc[k]), [three_d(g_sum[i])])
        grads[k], deltas[k], new_m[k], new_v[k] = [r.reshape(shp) for r in res]
    small_loc_shapes = [w_loc[k].shape for k in SMALL]
    res = adamw("adamw_small", _pack_flat([w_loc[k] for k in SMALL], F32, LANES)[None],
                _pack_flat([m_loc[k] for k in SMALL], F32, LANES)[None],
                _pack_flat([v_loc[k] for k in SMALL], F32, LANES)[None],
                [_pack_flat([gsum[k] for k in SMALL], F32, LANES)[None]])
    for dst, r in zip((grads, deltas, new_m, new_v), res):
        dst.update(dict(zip(SMALL, _unpack_flat(r, small_loc_shapes))))

    return (loss, grad_x[None], *[grads[k] for k in WEIGHTS], *[deltas[k] for k in WEIGHTS],
            *[new_m[k] for k in WEIGHTS], *[new_v[k] for k in WEIGHTS])
```

```python
import math
from typing import NamedTuple

import jax
import jax.numpy as jnp
from jax import lax
from jax.experimental import pallas as pl
from jax.experimental.pallas import tpu as pltpu

F32 = jnp.float32
CDT = jnp.bfloat16
NORM_EPS = 1e-6
ROPE_THETA = 10000.0
GRID_W = 64
D_MODEL = 1024
ATTN_HD = 64
SSD_HEADS, SSD_HD = 8, 64
RET_HEADS, RET_DK = 4, 128
CHUNK = 256
ROW_TILE = 256
ROW_TILE_TALL = 768
MM_ROWS = 768
MM_TN_ROWS = 2816
MM_VMEM_BUDGET = 44 * 1024 * 1024
ATTN_TQ, ATTN_TK = 256, 256
ATTN_ONES_ROWS = 16
ATTN_TK_BWD = 2048
LOG2E, LN2 = 1.4426950408889634, 0.6931471805599453
ATTN_TK_FWD = 2048
LANES = 128
PAIR_BLOCK_BYTES = 2 * 1024 * 1024
COND_ROWS = 16
VMEM_LIMIT = 56 * 1024 * 1024

ADAM_LR, ADAM_B1, ADAM_B2, ADAM_EPS, ADAM_WD, ADAM_STEP = 0.001, 0.9, 0.999, 1e-08, 0.01, 10

IN_LAYOUT = {
    "gates": (0, 3072, 4368, 3072), "xbc": (3072, 1024, 1280, 1024), "q": (4096, 512, 0, 512),
    "z": (4608, 512, 768, 512), "rq": (5120, 512, 2320, 512), "rk": (5632, 512, 2832, 512),
    "rv": (6144, 512, 3344, 512), "rg": (6656, 512, 3856, 512), "k": (7168, 128, 512, 128),
    "v": (7296, 128, 640, 128), "dt": (7424, 128, 2304, 16),
}
IN_PAD = 7680
IN_ORIG_ORDER = ("q", "k", "v", "z", "xbc", "dt", "rq", "rk", "rv", "rg", "gates")
IN_NEW_ORDER = ("gates", "xbc", "q", "z", "rq", "rk", "rv", "rg", "k", "v", "dt")

BIG = ("w_mod", "w_in", "w_branch", "w_out", "w_mlp1", "w_mlp2")
BIG_KIND = {"w_mod": "cols", "w_in": "slices", "w_branch": "cols", "w_out": "rows", "w_mlp1": "cols", "w_mlp2": "rows"}
SMALL = ("c_ctx", "b_mod", "norm1_w", "norm2_w", "attn_q_norm", "attn_k_norm", "ssd_conv_w", "ssd_conv_b",
         "ssd_dt_bias", "ssd_a_log", "ssd_d", "ssd_norm_w", "ret_log_decay", "ret_gn_w", "final_norm_w")
WEIGHTS = ("c_ctx", "w_mod", "b_mod", "norm1_w", "norm2_w", "w_in", "attn_q_norm", "attn_k_norm", "ssd_conv_w",
           "ssd_conv_b", "ssd_dt_bias", "ssd_a_log", "ssd_d", "ssd_norm_w", "ret_log_decay", "ret_gn_w",
           "w_branch", "w_out", "w_mlp1", "w_mlp2", "final_norm_w")


def _cp(sem):
    return pltpu.CompilerParams(dimension_semantics=sem, vmem_limit_bytes=VMEM_LIMIT)


class Cols(NamedTuple):
    arr: jax.Array
    off: int
    width: int


def _width(item):
    return item.width if isinstance(item, Cols) else item.shape[1]


def _row_in(item, rows, imap=None):
    imap = imap or (lambda i: i)
    if isinstance(item, Cols):
        assert item.off % item.width == 0
        blk = item.off // item.width
        return item.arr, pl.BlockSpec((rows, item.width), lambda i, blk=blk: (imap(i), blk))
    return item, pl.BlockSpec((rows, item.shape[1]), lambda i: (imap(i), 0))


def _const_spec(shape):
    return pl.BlockSpec(shape, lambda *_: (0,) * len(shape))


def _mxu(a, b, dims=(((1,), (0,)), ((), ()))):
    return lax.dot_general(a.astype(CDT), b.astype(CDT), dims, preferred_element_type=F32)


_NT = (((1,), (1,)), ((), ()))
_TN = (((0,), (0,)), ((), ()))


@jax.custom_vjp
def _softplus(x):
    return jnp.maximum(x, 0.0) + jnp.log1p(jnp.exp(-jnp.abs(x)))


def _softplus_fwd(x):
    return _softplus(x), x


def _softplus_bwd(x, g):
    return (g * jax.nn.sigmoid(x),)


_softplus.defvjp(_softplus_fwd, _softplus_bwd)


def _group_mean_impl(x, gmat):
    hi = x.astype(CDT)
    lo = (x - hi.astype(F32)).astype(CDT)
    return (jnp.dot(hi, gmat, preferred_element_type=F32) + jnp.dot(lo, gmat, preferred_element_type=F32))


@jax.custom_vjp
def _group_mean(x, gmat):
    return _group_mean_impl(x, gmat)


def _group_mean_fwd(x, gmat):
    return _group_mean_impl(x, gmat), gmat


def _group_mean_bwd(gmat, g):
    return _group_mean_impl(g, gmat), jnp.zeros_like(gmat)


_group_mean.defvjp(_group_mean_fwd, _group_mean_bwd)


def _group_matrix(width, group):
    r = jnp.arange(width) // group
    return jnp.where(r[:, None] == r[None, :], 1.0 / group, 0.0).astype(CDT)


def _make_rope(half):
    def partner(x):
        w = x.shape[1]
        lane = lax.broadcasted_iota(jnp.int32, x.shape, 1)
        first = (lane % (2 * half)) < half
        return jnp.where(first, pltpu.roll(x, w - half, axis=1), pltpu.roll(x, half, axis=1))

    def impl(x, cos_full, sin_signed):
        return x * cos_full + partner(x) * sin_signed

    @jax.custom_vjp
    def rope(x, cos_full, sin_signed):
        return impl(x, cos_full, sin_signed)

    def fwd(x, cos_full, sin_signed):
        return impl(x, cos_full, sin_signed), (cos_full, sin_signed)

    def bwd(res, g):
        cos_full, sin_signed = res
        return impl(g, cos_full, -sin_signed), jnp.zeros_like(cos_full), jnp.zeros_like(sin_signed)

    rope.defvjp(fwd, bwd)
    return rope


_rope32 = _make_rope(32)
_rope64 = _make_rope(64)


def _rms(x, w):
    return x * lax.rsqrt(jnp.mean(x * x, axis=-1, keepdims=True) + NORM_EPS) * w


def _col(v, lane_index):
    lane = lax.broadcasted_iota(jnp.int32, v.shape, 1)
    return jnp.sum(jnp.where(lane == lane_index, v, 0.0), axis=1, keepdims=True)


def _typed_spec(width, nb_ctx):
    return pl.BlockSpec((None, 1, width), lambda i: (jnp.where(i >= nb_ctx, 1, 0), 0, 0))


def _row_tile(n_rows, typed):
    return min(ROW_TILE, n_rows) if typed else _pick(n_rows, (ROW_TILE_TALL, ROW_TILE))


def rowwise_fwd(name, f, rows, typed, shared, outs, n_rows, nb_ctx):
    tm = _row_tile(n_rows, typed)
    nin = len(rows) + len(typed) + len(shared)

    def body(*refs):
        res = f(*[r[...] for r in refs[:nin]])
        for o_ref, o in zip(refs[nin:], res):
            o_ref[...] = o.astype(o_ref.dtype)

    arrs, specs = [], []
    for it in rows:
        a, s = _row_in(it, tm)
        arrs.append(a)
        specs.append(s)
    for t in typed:
        arrs.append(t)
        specs.append(_typed_spec(t.shape[-1], nb_ctx))
    for s_ in shared:
        arrs.append(s_)
        specs.append(_const_spec(s_.shape))
    res = pl.pallas_call(
        body, name=name, grid=(n_rows // tm,), in_specs=specs,
        out_specs=[pl.BlockSpec((tm, w), lambda i: (i, 0)) for w, _ in outs],
        out_shape=[jax.ShapeDtypeStruct((n_rows, w), dt) for w, dt in outs],
        compiler_params=_cp(("parallel",)))(*arrs)
    return res


def rowwise_bwd(name, f, rows, typed, shared, cots, row_diff, shared_diff, drow_dtypes, n_rows, nb_ctx):
    tm = _row_tile(n_rows, typed)
    cot_groups = [c_ if isinstance(c_, tuple) else (c_,) for c_ in cots]
    cots = [a for grp in cot_groups for a in grp]
    nr, nt, ns, nc = len(rows), len(typed), len(shared), len(cots)
    nin = nr + nt + ns
    d_rows = [k for k in range(nr) if row_diff[k]]
    d_sh = [k for k in range(ns) if shared_diff[k]]

    def body(*refs):
        rvals = [r[...] for r in refs[:nr]]
        tvals = [r[...] for r in refs[nr:nr + nt]]
        svals = [r[...] for r in refs[nr + nt:nin]]
        cparts = [r[...].astype(F32) for r in refs[nin:nin + nc]]
        cvals = []
        for grp in cot_groups:
            cvals.append(sum(cparts[1:len(grp)], cparts[0]))
            cparts = cparts[len(grp):]
        out_refs = refs[nin + nc:]

        def g(*dv):
            dv = list(dv)
            rv = list(rvals)
            for k in d_rows:
                rv[k] = dv.pop(0)
            tv = [dv.pop(0) for _ in range(nt)]
            sv = list(svals)
            for k in d_sh:
                sv[k] = dv.pop(0)
            return tuple(o.astype(F32) for o in f(*rv, *tv, *sv))

        prim = [rvals[k].astype(F32) for k in d_rows] + tvals + [svals[k] for k in d_sh]
        _, vjp = jax.vjp(g, *prim)
        grads = list(vjp(tuple(cvals)))
        i = pl.program_id(0)
        for ref in out_refs[:len(d_rows)]:
            ref[...] = grads.pop(0).astype(ref.dtype)
        first_typed = (i == 0) | (i == nb_ctx)
        for ref in out_refs[len(d_rows):len(d_rows) + nt]:
            gr = grads.pop(0)

            @pl.when(first_typed)
            def _(ref=ref, gr=gr):
                ref[...] = gr

            @pl.when(jnp.logical_not(first_typed))
            def _(ref=ref, gr=gr):
                ref[...] += gr
        for ref in out_refs[len(d_rows) + nt:]:
            gr = grads.pop(0)

            @pl.when(i == 0)
            def _(ref=ref, gr=gr):
                ref[...] = gr

            @pl.when(i != 0)
            def _(ref=ref, gr=gr):
                ref[...] += gr

    arrs, specs = [], []
    for it in list(rows):
        a, s = _row_in(it, tm)
        arrs.append(a)
        specs.append(s)
    for t in typed:
        arrs.append(t)
        specs.append(_typed_spec(t.shape[-1], nb_ctx))
    for s_ in shared:
        arrs.append(s_)
        specs.append(_const_spec(s_.shape))
    for c_ in cots:
        a, s = _row_in(c_, tm)
        arrs.append(a)
        specs.append(s)
    out_specs, out_shape = [], []
    for k, dt in zip(d_rows, drow_dtypes):
        w = _width(rows[k])
        out_specs.append(pl.BlockSpec((tm, w), lambda i: (i, 0)))
        out_shape.append(jax.ShapeDtypeStruct((n_rows, w), dt))
    for t in typed:
        out_specs.append(_typed_spec(t.shape[-1], nb_ctx))
        out_shape.append(jax.ShapeDtypeStruct(t.shape, F32))
    for k in d_sh:
        out_specs.append(_const_spec(shared[k].shape))
        out_shape.append(jax.ShapeDtypeStruct(shared[k].shape, F32))
    res = pl.pallas_call(body, name=name, grid=(n_rows // tm,), in_specs=specs, out_specs=out_specs,
                         out_shape=out_shape, compiler_params=_cp(("arbitrary",)))(*arrs)
    n1, n2 = len(d_rows), len(d_rows) + nt
    return list(res[:n1]), list(res[n1:n2]), list(res[n2:])


def _pick(n, prefs):
    for p in prefs:
        if n % p == 0:
            return p
    return n


def mm(name, a, b, out_dtype, transpose_b=False, epilogue=None, tile_ins=(), typed_ins=(), shared_ins=(),
       out_dtypes=None, nb_ctx=None, whole_rows=False):
    n, k = b.shape if transpose_b else b.shape[::-1]
    m = (a.arr if isinstance(a, Cols) else a).shape[0]
    assert _width(a) == k
    tm = min(ROW_TILE, m) if typed_ins else _pick(m, (MM_ROWS, 256))
    out_dtypes = out_dtypes or [out_dtype]
    tile_arrs = [x.arr if isinstance(x, Cols) else x for x in tile_ins]
    tile_bytes = sum(jnp.dtype(d).itemsize for d in out_dtypes) + sum(x.dtype.itemsize for x in tile_arrs)
    fits = lambda c: 2 * (tm * k * 2 + k * c * 2 + tm * c * tile_bytes) <= MM_VMEM_BUDGET
    tn = next(c for c in (2560, 2048, 1536, 1024, 512, 256, 128, n) if n % c == 0 and fits(c) or c == n)
    if whole_rows:
        assert fits(n)
        tn = n
    dims = _NT if transpose_b else (((1,), (0,)), ((), ()))
    n_ex = len(tile_ins) + len(typed_ins) + len(shared_ins)

    def body(a_ref, b_ref, *refs):
        prod = lax.dot_general(a_ref[...], b_ref[...], dims, preferred_element_type=F32)
        outs = (prod,) if epilogue is None else epilogue(prod, *[r[...] for r in refs[:n_ex]])
        for o_ref, o in zip(refs[n_ex:], outs):
            o_ref[...] = o.astype(o_ref.dtype)

    a_arr, a_spec = _row_in(a, tm)
    a_spec = pl.BlockSpec(a_spec.block_shape, lambda j, i, f=a_spec.index_map: f(i))
    b_spec = pl.BlockSpec((tn, k), lambda j, i: (j, 0)) if transpose_b else pl.BlockSpec((k, tn), lambda j, i: (0, j))
    tile = pl.BlockSpec((tm, tn), lambda j, i: (i, j))
    ex_specs = []
    for x in tile_ins:
        base = x.off // tn if isinstance(x, Cols) else 0
        assert not isinstance(x, Cols) or (x.off % tn == 0 and x.width == n)
        ex_specs.append(pl.BlockSpec((tm, tn), lambda j, i, base=base: (i, base + j)))
    ex_specs += [pl.BlockSpec((None, 1, tn), lambda j, i: (jnp.where(i >= nb_ctx, 1, 0), 0, j))] * len(typed_ins)
    ex_specs += [pl.BlockSpec((1, tn), lambda j, i: (0, j))] * len(shared_ins)
    res = pl.pallas_call(
        body, name=name, grid=(n // tn, m // tm), in_specs=[a_spec, b_spec] + ex_specs,
        out_specs=[tile] * len(out_dtypes),
        out_shape=[jax.ShapeDtypeStruct((m, n), d) for d in out_dtypes],
        compiler_params=_cp(("parallel", "parallel")))(a_arr, b, *tile_arrs, *typed_ins, *shared_ins)
    return res[0] if epilogue is None else res


def mm_tn(name, a, b, out_dtype=F32, pieces=None):
    t = (a.arr if isinstance(a, Cols) else a).shape[0]
    k, n = _width(a), _width(b)
    tt = _pick(t, (MM_TN_ROWS, MM_ROWS, 256))
    k_unit = pieces[1] if pieces and pieces[0] == "rows" else k
    n_unit = pieces[1] if pieces and pieces[0] == "cols" else n
    tk = _pick(k_unit, (1024, 512, 256, 128))
    tn = _pick(n_unit, (1280, 1024, 512, 256, 128))
    n_t = t // tt

    def body(a_ref, b_ref, o_ref, acc):
        part = lax.dot_general(a_ref[...], b_ref[...], _TN, preferred_element_type=F32)
        ti = pl.program_id(2)

        @pl.when(ti == 0)
        def _():
            acc[...] = part

        @pl.when(ti != 0)
        def _():
            acc[...] += part

        @pl.when(ti == n_t - 1)
        def _():
            o_ref[...] = acc[...].astype(o_ref.dtype)

    def win(item, width):
        if isinstance(item, Cols):
            assert item.off % width == 0
            return item.arr, item.off // width
        return item, 0

    a_arr, a0 = win(a, tk)
    b_arr, b0 = win(b, tn)
    if pieces is None:
        out_spec = pl.BlockSpec((tk, tn), lambda ki, ni, ti: (ki, ni))
        out_shape = (k, n)
    elif pieces[0] == "cols":
        per = n_unit // tn
        out_spec = pl.BlockSpec((None, tk, tn), lambda ki, ni, ti: (ni // per, ki, ni % per))
        out_shape = (4, k, n_unit)
    else:
        per = k_unit // tk
        out_spec = pl.BlockSpec((None, tk, tn), lambda ki, ni, ti: (ki // per, ki % per, ni))
        out_shape = (4, k_unit, n)
    return pl.pallas_call(
        body, name=name, grid=(k // tk, n // tn, n_t),
        in_specs=[pl.BlockSpec((tt, tk), lambda ki, ni, ti: (ti, a0 + ki)),
                  pl.BlockSpec((tt, tn), lambda ki, ni, ti: (ti, b0 + ni))],
        out_specs=out_spec, out_shape=jax.ShapeDtypeStruct(out_shape, out_dtype),
        scratch_shapes=[pltpu.VMEM((tk, tn), F32)],
        compiler_params=_cp(("parallel", "parallel", "arbitrary")))(a_arr, b_arr)


def _heads_t(rows_blk):
    blk = rows_blk.astype(F32).T
    return jnp.concatenate([blk[hh * ATTN_HD:(hh + 1) * ATTN_HD, :] for hh in range(4)], axis=1)


def _heads_rows(t_blk):
    tq = t_blk.shape[1] // 4
    return jnp.concatenate([t_blk[:, hh * tq:(hh + 1) * tq] for hh in range(4)], axis=0).T


def attn_fwd(name, q, kk, vT_ones, m_ctx, tq):
    t, hd, hd_ext = kk.shape[1], ATTN_HD, vT_ones.shape[1]
    nq, r = t // tq, 4 * tq
    tk = _pick(t - m_ctx, (ATTN_TK_FWD, ATTN_TK))
    nqc, n_lat_tiles = m_ctx // tq, (t - m_ctx) // tk

    def body(q_ref, k_ref, vT_ref, o_ref, qT_ref, oT_ref, lse_ref):
        i = pl.program_id(1)
        q_t = _heads_t(q_ref[...]).astype(CDT)
        qT_ref[...] = q_t

        def tile(off, size, carry):
            mi, acc = carry
            sub = min(size, ATTN_TK)
            offs = [off + u * sub for u in range(size // sub)]
            sts = [jnp.dot(k_ref[pl.ds(o, sub), :], q_t, preferred_element_type=F32) for o in offs]
            for o, st in zip(offs, sts):
                mn = jnp.maximum(mi, jnp.max(st, axis=0, keepdims=True))
                pt = jnp.exp2(st - mn)
                acc = jnp.exp2(mi - mn) * acc + jnp.dot(vT_ref[:, pl.ds(o, sub)], pt.astype(CDT),
                                                        preferred_element_type=F32)
                mi = mn
            return mi, acc

        carry = tile(0, m_ctx, (jnp.full((1, r), -1e30, F32), jnp.zeros((hd_ext, r), F32)))
        mi, acc = lax.fori_loop(
            0, jnp.where(i < nqc, 0, n_lat_tiles),
            lambda j, cr: tile(pl.multiple_of(m_ctx + j * tk, ATTN_TK), tk, cr), carry)
        li = acc[hd:hd + 1]
        o_t = acc[:hd] / li
        oT_ref[...] = o_t.astype(oT_ref.dtype)
        o_ref[...] = _heads_rows(o_t).astype(o_ref.dtype)
        lse_ref[...] = mi + jnp.log2(li)

    blk_t = pl.BlockSpec((None, None, hd, r), lambda g, i: (g, i, 0, 0))
    rows = pl.BlockSpec((tq, 4 * hd), lambda g, i: (i, g))
    return pl.pallas_call(
        body, name=name, grid=(2, nq),
        in_specs=[rows, pl.BlockSpec((None, t, hd), lambda g, i: (g, 0, 0)),
                  pl.BlockSpec((None, hd_ext, t), lambda g, i: (g, 0, 0))],
        out_specs=[rows, blk_t, blk_t, pl.BlockSpec((None, None, 1, r), lambda g, i: (g, i, 0, 0))],
        out_shape=[jax.ShapeDtypeStruct((t, 8 * hd), CDT), jax.ShapeDtypeStruct((2, nq, hd, r), CDT),
                   jax.ShapeDtypeStruct((2, nq, hd, r), CDT), jax.ShapeDtypeStruct((2, nq, 1, r), F32)],
        compiler_params=_cp(("parallel", "arbitrary")))(q, kk, vT_ones)


def attn_bwd(name, qT, do, oT, lse, kk, kT, vv, m_ctx):
    _, nq, hd, r = qT.shape
    t = kk.shape[1]
    tq = r // 4
    tk = _pick(t - m_ctx, (ATTN_TK_BWD, ATTN_TK))
    nqc, n_lat_tiles = m_ctx // tq, (t - m_ctx) // tk

    def body(qT_ref, do_ref, oT_ref, lse_ref, k_ref, kT_ref, v_ref, dq_ref, dk_ref, dv_ref):
        i = pl.program_id(1)

        @pl.when(i == 0)
        def _():
            dk_ref[...] = jnp.zeros_like(dk_ref)
            dv_ref[...] = jnp.zeros_like(dv_ref)

        q_t = qT_ref[...]
        do_f = _heads_t(do_ref[...])
        do_t = do_f.astype(CDT)
        lse = lse_ref[...]
        delta = jnp.sum(do_f * oT_ref[...].astype(F32), axis=0, keepdims=True)

        def tile(off, size, dq):
            sub = min(size, ATTN_TK)
            offs = [off + u * sub for u in range(size // sub)]
            sts = [jnp.dot(k_ref[pl.ds(o, sub), :], q_t, preferred_element_type=F32) for o in offs]
            dpts = [jnp.dot(v_ref[pl.ds(o, sub), :], do_t, preferred_element_type=F32) for o in offs]
            for o, st, dpt in zip(offs, sts, dpts):
                pt = jnp.exp2(st - lse)
                dv_ref[pl.ds(o, sub), :] += lax.dot_general(pt.astype(CDT), do_t, _NT, preferred_element_type=F32)
                dst = (pt * (dpt - delta)).astype(CDT)
                dk_ref[pl.ds(o, sub), :] += lax.dot_general(dst, q_t, _NT, preferred_element_type=F32)
                dq = dq + jnp.dot(kT_ref[:, pl.ds(o, sub)], dst, preferred_element_type=F32)
            return dq

        dq = tile(0, m_ctx, jnp.zeros((hd, r), F32))
        dq = lax.fori_loop(0, jnp.where(i < nqc, 0, n_lat_tiles),
                           lambda j, acc: tile(pl.multiple_of(m_ctx + j * tk, ATTN_TK), tk, acc), dq)
        dq_ref[...] = _heads_rows(dq * LN2)

    blk_t = pl.BlockSpec((None, None, hd, r), lambda g, i: (g, i, 0, 0))
    row = pl.BlockSpec((None, None, 1, r), lambda g, i: (g, i, 0, 0))
    kv = pl.BlockSpec((None, t, hd), lambda g, i: (g, 0, 0))
    rows = pl.BlockSpec((tq, 4 * hd), lambda g, i: (i, g))
    return pl.pallas_call(
        body, name=name, grid=(2, nq),
        in_specs=[blk_t, rows, blk_t, row, kv, pl.BlockSpec((None, hd, t), lambda g, i: (g, 0, 0)), kv],
        out_specs=[rows, kv, kv],
        out_shape=[jax.ShapeDtypeStruct((t, 8 * hd), F32), jax.ShapeDtypeStruct(kk.shape, F32),
                   jax.ShapeDtypeStruct(kk.shape, F32)],
        compiler_params=_cp(("parallel", "arbitrary")))(qT, do, oT, lse, kk, kT, vv)


def _split_kv(a):
    return a.reshape(a.shape[0], 2, ATTN_HD).transpose(1, 0, 2)


def _merge_kv(a):
    return a.transpose(1, 0, 2).reshape(a.shape[1], 2 * ATTN_HD)


def _chunk_order(rev, ncc, nct):
    if not rev:
        return lambda s: s
    return lambda s: jnp.where(s < ncc, ncc - 1 - s, nct - 1 - (s - ncc))


def scan_fwd(name, make_fn, rows, shared, n_state, y_width, n_rows, m_ctx):
    nct, ncc = n_rows // CHUNK, m_ctx // CHUNK
    orders = [_chunk_order(rev, ncc, nct) for rev in (False, True)]
    fns = [make_fn(0), make_fn(1)]
    nr, ns = len(rows), len(shared)

    def body(*refs):
        svals = [r[...] for r in refs[2 * nr:2 * nr + ns]]
        y_refs, sin_refs, st = refs[2 * nr + ns:2 * nr + ns + 2], refs[2 * nr + ns + 2:2 * nr + ns + 4], refs[-1]

        @pl.when(pl.program_id(0) == 0)
        def _():
            st[...] = jnp.zeros_like(st)

        for d in range(2):
            rvals = [r[...] for r in refs[d * nr:(d + 1) * nr]]
            prev = [st[d, k] for k in range(n_state)]
            sin_refs[d][...] = st[d]
            y, new = fns[d](rvals, svals, prev)
            y_refs[d][...] = y
            for k in range(n_state):
                st[d, k] = new[k]

    arrs, specs = [], []
    for order in orders:
        for it in rows:
            a, s = _row_in(it, CHUNK, order)
            arrs.append(a)
            specs.append(s)
    for s_ in shared:
        arrs.append(s_)
        specs.append(_const_spec(s_.shape))
    return pl.pallas_call(
        body, name=name, grid=(nct,), in_specs=specs,
        out_specs=[pl.BlockSpec((CHUNK, y_width), lambda s, o=o: (o(s), 0)) for o in orders]
        + [pl.BlockSpec((None, n_state, LANES, LANES), lambda s, o=o: (o(s), 0, 0, 0)) for o in orders],
        out_shape=[jax.ShapeDtypeStruct((n_rows, y_width), F32)] * 2
        + [jax.ShapeDtypeStruct((nct, n_state, LANES, LANES), F32)] * 2,
        scratch_shapes=[pltpu.VMEM((2, n_state, LANES, LANES), F32)],
        compiler_params=_cp(("arbitrary",)))(*arrs)


def scan_bwd(name, make_fn, rows, shared, states_in, dy, post, outs, n_state, n_rows, m_ctx, dirs=(0, 1)):
    nct, ncc = n_rows // CHUNK, m_ctx // CHUNK
    orders = [(lambda r, f=_chunk_order(d == 1, ncc, nct): f(nct - 1 - r)) for d in dirs]
    fns = [make_fn(d) for d in dirs]
    nd = len(dirs)
    nr, ns, no = len(rows), len(shared), len(outs)
    n_in = nd * nr + ns

    def body(*refs):
        svals = [r[...] for r in refs[nd * nr:n_in]]
        sin_refs, dy_refs = refs[n_in:n_in + nd], refs[n_in + nd:n_in + 2 * nd]
        out_refs = refs[n_in + 2 * nd:n_in + 2 * nd + nd * no]
        dsh_refs = refs[n_in + 2 * nd + nd * no:-1]
        dst = refs[-1]
        r = pl.program_id(0)

        @pl.when(r == 0)
        def _():
            dst[...] = jnp.zeros_like(dst)

        d_shared = None
        for d in range(nd):
            rvals = [x[...] for x in refs[d * nr:(d + 1) * nr]]
            prev = [sin_refs[d][k] for k in range(n_state)]
            _, vjp = jax.vjp(fns[d], rvals, svals, prev)
            d_rows, d_sh, d_prev = vjp((dy_refs[d][...], [dst[d, k] for k in range(n_state)]))
            for ref, val in zip(out_refs[d * no:(d + 1) * no], post(d_rows)):
                ref[...] = val.astype(ref.dtype)
            d_shared = d_sh if d_shared is None else [a + b for a, b in zip(d_shared, d_sh)]
            for k in range(n_state):
                dst[d, k] = d_prev[k]
        for ref, gr in zip(dsh_refs, d_shared):
            @pl.when(r == 0)
            def _(ref=ref, gr=gr):
                ref[...] = gr

            @pl.when(r != 0)
            def _(ref=ref, gr=gr):
                ref[...] += gr

    arrs, specs = [], []
    for order in orders:
        for it in rows:
            a, s = _row_in(it, CHUNK, order)
            arrs.append(a)
            specs.append(s)
    for s_ in shared:
        arrs.append(s_)
        specs.append(_const_spec(s_.shape))
    for sin, order in zip(states_in, orders):
        arrs.append(sin)
        specs.append(pl.BlockSpec((None, n_state, LANES, LANES), lambda r, o=order: (o(r), 0, 0, 0)))
    for order in orders:
        a, s = _row_in(dy, CHUNK, order)
        arrs.append(a)
        specs.append(s)
    out_specs = [pl.BlockSpec((CHUNK, w), lambda r, o=o: (o(r), 0)) for o in orders for w, _ in outs]
    out_shape = [jax.ShapeDtypeStruct((n_rows, w), dt) for _ in orders for w, dt in outs]
    for s_ in shared:
        out_specs.append(_const_spec(s_.shape))
        out_shape.append(jax.ShapeDtypeStruct(s_.shape, F32))
    res = pl.pallas_call(body, name=name, grid=(nct,), in_specs=specs, out_specs=out_specs, out_shape=out_shape,
                         scratch_shapes=[pltpu.VMEM((nd, n_state, LANES, LANES), F32)],
                         compiler_params=_cp(("arbitrary",)))(*arrs)
    return [list(res[d * no:(d + 1) * no]) for d in range(nd)] + [list(res[nd * no:])]


def _make_ssd_chunk(direction):
    rev = direction == 1
    base = 8 * direction

    def fn(rows, shared, prev):
        xs, bms, cms, dtraw = rows[0:4], rows[4:6], rows[6:8], rows[8]
        dt_bias, a_log = shared
        ln = dtraw.shape[0]
        dt_all = _softplus(dtraw + dt_bias)
        a_all = dt_all * (-jnp.exp(a_log))
        r_i = lax.broadcasted_iota(jnp.int32, (ln, ln), 0)
        c_i = lax.broadcasted_iota(jnp.int32, (ln, ln), 1)
        tri = (r_i <= c_i) if rev else (r_i >= c_i)
        a_cum_all = jnp.dot(tri.astype(F32), a_all, precision=lax.Precision.HIGHEST, preferred_element_type=F32)
        a_tot_all = jnp.sum(a_all, axis=0, keepdims=True)
        first = lax.broadcasted_iota(jnp.int32, (ln, LANES), 1) < SSD_HD
        first_row = lax.broadcasted_iota(jnp.int32, (LANES, 1), 0) < SSD_HD

        def lmat(acol):
            a_b = jnp.broadcast_to(acol, (ln, ln))
            seg = a_b - a_b.T
            return jnp.where(tri, jnp.exp(jnp.where(tri, seg, 0.0)), 0.0)

        ys, new = [], []
        for g in range(2):
            bm, cm = bms[g], cms[g]
            cb = _mxu(cm, bm, _NT)
            for jj in range(2):
                pr = 2 * g + jj
                h0, h1 = base + 2 * pr, base + 2 * pr + 1
                ac0, ac1 = _col(a_cum_all, h0), _col(a_cum_all, h1)
                at0, at1 = _col(a_tot_all, h0), _col(a_tot_all, h1)
                dt_pair = jnp.where(first, _col(dt_all, h0), _col(dt_all, h1))
                acum_pair = jnp.where(first, ac0, ac1)
                atot_pair = jnp.where(first[0:1], at0, at1)
                xd = xs[pr] * dt_pair
                st = _mxu(xd * jnp.exp(atot_pair - acum_pair), bm, _TN)
                new.append(prev[pr] * jnp.where(first_row, jnp.exp(at0), jnp.exp(at1)) + st)
                y0 = _mxu(cb * lmat(ac0), xd)
                y1 = _mxu(cb * lmat(ac1), xd)
                y_off = _mxu(cm, prev[pr], _NT) * jnp.exp(acum_pair)
                ys.append(jnp.where(first, y0, y1) + y_off)
        return jnp.concatenate(ys, axis=1), new

    return fn


def _make_ret_chunk(direction):
    rev = direction == 1
    base = 4 * direction

    def fn(rows, shared, prev):
        qs, ks, vs = rows[0:4], rows[4:8], rows[8:12]
        lg_all = -jnp.exp(shared[0])
        ln = qs[0].shape[0]
        pos = lax.broadcasted_iota(jnp.int32, (ln, 1), 0).astype(F32)
        r_i = lax.broadcasted_iota(jnp.int32, (ln, ln), 0)
        c_i = lax.broadcasted_iota(jnp.int32, (ln, ln), 1)
        diff = ((c_i - r_i) if rev else (r_i - c_i))
        mask = diff >= 0
        dpos = jnp.maximum(diff, 0).astype(F32)
        k_pow = pos if rev else (ln - 1.0 - pos)
        q_pow = (ln - pos) if rev else (pos + 1.0)
        ys, new = [], []
        for h in range(RET_HEADS):
            lg = _col(lg_all, base + h)
            dmat = jnp.where(mask, jnp.exp(dpos * lg), 0.0)
            st = _mxu(ks[h] * jnp.exp(k_pow * lg), vs[h], _TN)
            new.append(prev[h] * jnp.exp(ln * lg) + st)
            s = _mxu(qs[h], ks[h], _NT) * dmat
            ys.append(_mxu(s, vs[h]) + _mxu(qs[h], prev[h]) * jnp.exp(q_pow * lg))
        return jnp.concatenate(ys, axis=1), new

    return fn


def _conv_pre(x, w, b, t_idx, n_rows, m_ctx):
    is_start = (t_idx == 0) | (t_idx == m_ctx)
    is_end = (t_idx == m_ctx - 1) | (t_idx == n_rows - 1)
    xp = jnp.where(is_start, 0.0, pltpu.roll(x, 1, axis=0))
    xn = jnp.where(is_end, 0.0, pltpu.roll(x, n_rows - 1, axis=0))
    return w[0:1] * xp + w[1:2] * x + w[2:3] * xn + b, xp, xn, is_start, is_end


def conv_fwd(x, conv_w, conv_b, m_ctx):
    n_rows, width = x.arr.shape[0], x.width
    c0 = x.off // LANES

    def body(x_ref, w_ref, b_ref, o_ref):
        t_idx = lax.broadcasted_iota(jnp.int32, (n_rows, 1), 0)
        pre = _conv_pre(x_ref[...], w_ref[...], b_ref[...], t_idx, n_rows, m_ctx)[0]
        o_ref[...] = pre * jax.nn.sigmoid(pre)

    return pl.pallas_call(
        body, name="conv_fwd", grid=(width // LANES,),
        in_specs=[pl.BlockSpec((n_rows, LANES), lambda c: (0, c0 + c)),
                  pl.BlockSpec((3, LANES), lambda c: (0, c)), pl.BlockSpec((1, LANES), lambda c: (0, c))],
        out_specs=pl.BlockSpec((n_rows, LANES), lambda c: (0, c)),
        out_shape=jax.ShapeDtypeStruct((n_rows, width), F32),
        compiler_params=_cp(("parallel",)))(x.arr, conv_w, conv_b)


def conv_bwd(x, conv_w, conv_b, dy_a, dy_b, dxs_extra, m_ctx):
    n_rows, width = x.arr.shape[0], x.width
    c0 = x.off // LANES
    n_extra = dxs_extra.shape[1] // LANES

    def body(x_ref, w_ref, b_ref, dya_ref, dyb_ref, ex_ref, dx_ref, dw_ref, db_ref):
        c = pl.program_id(0)
        t_idx = lax.broadcasted_iota(jnp.int32, (n_rows, 1), 0)
        w = w_ref[...]
        pre, xp, xn, is_start, is_end = _conv_pre(x_ref[...], w, b_ref[...], t_idx, n_rows, m_ctx)
        sg = jax.nn.sigmoid(pre)
        dyv = dya_ref[...] + dyb_ref[...] + jnp.where(c < n_extra, ex_ref[...], 0.0)
        dpre = dyv * (sg * (1.0 + pre * (1.0 - sg)))
        d_next = jnp.where(is_end, 0.0, pltpu.roll(dpre, n_rows - 1, axis=0))
        d_prev = jnp.where(is_start, 0.0, pltpu.roll(dpre, 1, axis=0))
        dx_ref[...] = (w[1:2] * dpre + w[0:1] * d_next + w[2:3] * d_prev).astype(dx_ref.dtype)
        dw_ref[...] = jnp.concatenate([jnp.sum(dpre * xp, axis=0, keepdims=True),
                                       jnp.sum(dpre * x_ref[...], axis=0, keepdims=True),
                                       jnp.sum(dpre * xn, axis=0, keepdims=True)], axis=0)
        db_ref[...] = jnp.sum(dpre, axis=0, keepdims=True)

    return pl.pallas_call(
        body, name="conv_bwd", grid=(width // LANES,),
        in_specs=[pl.BlockSpec((n_rows, LANES), lambda c: (0, c0 + c)),
                  pl.BlockSpec((3, LANES), lambda c: (0, c)), pl.BlockSpec((1, LANES), lambda c: (0, c)),
                  pl.BlockSpec((n_rows, LANES), lambda c: (0, c)), pl.BlockSpec((n_rows, LANES), lambda c: (0, c)),
                  pl.BlockSpec((n_rows, LANES), lambda c: (0, jnp.minimum(c, n_extra - 1)))],
        out_specs=[pl.BlockSpec((n_rows, LANES), lambda c: (0, c)),
                   pl.BlockSpec((3, LANES), lambda c: (0, c)), pl.BlockSpec((1, LANES), lambda c: (0, c))],
        out_shape=[jax.ShapeDtypeStruct((n_rows, width), CDT), jax.ShapeDtypeStruct((3, width), F32),
                   jax.ShapeDtypeStruct((1, width), F32)],
        compiler_params=_cp(("parallel",)))(x.arr, conv_w, conv_b, dy_a, dy_b, dxs_extra)


def loss_head(h, target, final_w, m_ctx):
    n_rows, d = h.shape
    tm = min(ROW_TILE, n_rows)
    nb_ctx = m_ctx // tm

    def f(hb, w, tgt):
        err = _rms(hb, w) - tgt
        return 0.5 * jnp.sum(jnp.mean(err * err, axis=-1))

    def body(h_ref, t_ref, w_ref, loss_ref, dh_ref, dw_ref):
        i = pl.program_id(0)

        @pl.when(i < nb_ctx)
        def _():
            dh_ref[...] = jnp.zeros_like(dh_ref)

        @pl.when(i == 0)
        def _():
            loss_ref[...] = jnp.zeros_like(loss_ref)
            dw_ref[...] = jnp.zeros_like(dw_ref)

        @pl.when(i >= nb_ctx)
        def _():
            val, vjp = jax.vjp(lambda hb, w: f(hb, w, t_ref[...]), h_ref[...], w_ref[...])
            dh, dw = vjp(jnp.ones((), F32))
            dh_ref[...] = dh
            dw_ref[...] += dw
            loss_ref[...] += jnp.broadcast_to(val, loss_ref.shape)

    return pl.pallas_call(
        body, name="loss_head", grid=(n_rows // tm,),
        in_specs=[pl.BlockSpec((tm, d), lambda i: (i, 0)),
                  pl.BlockSpec((tm, d), lambda i: (jnp.maximum(i - nb_ctx, 0), 0)), _const_spec((1, d))],
        out_specs=[_const_spec((1, LANES)), pl.BlockSpec((tm, d), lambda i: (i, 0)), _const_spec((1, d))],
        out_shape=[jax.ShapeDtypeStruct((1, LANES), F32), jax.ShapeDtypeStruct((n_rows, d), F32),
                   jax.ShapeDtypeStruct((1, d), F32)],
        compiler_params=_cp(("arbitrary",)))(h, target, final_w)


def adamw(name, w, m, v, g_parts):
    lead, rows, cols = w.shape
    tr = _pick(rows, (256, 128, 64, 32, 16, 8))
    npart = len(g_parts)
    c1 = 1.0 - ADAM_B1 ** ADAM_STEP
    c2 = 1.0 - ADAM_B2 ** ADAM_STEP

    def body(*refs):
        w_ref, m_ref, v_ref = refs[:3]
        g = refs[3][...].astype(F32)
        for r in refs[4:3 + npart]:
            g = g + r[...].astype(F32)
        g_ref, d_ref, nm_ref, nv_ref = refs[3 + npart:]
        nm = ADAM_B1 * m_ref[...] + (1.0 - ADAM_B1) * g
        nv = ADAM_B2 * v_ref[...] + (1.0 - ADAM_B2) * (g * g)
        g_ref[...] = g
        nm_ref[...] = nm
        nv_ref[...] = nv
        d_ref[...] = -ADAM_LR * ((nm / c1) / (jnp.sqrt(nv / c2) + ADAM_EPS) + ADAM_WD * w_ref[...])

    spec = pl.BlockSpec((None, tr, cols), lambda l, i: (l, i, 0))
    return pl.pallas_call(
        body, name=name, grid=(lead, rows // tr), in_specs=[spec] * (3 + npart), out_specs=[spec] * 4,
        out_shape=[jax.ShapeDtypeStruct(w.shape, F32)] * 4,
        compiler_params=_cp(("parallel", "parallel")))(w, m, v, *g_parts)


MESH = pl.DeviceIdType.MESH
_HBM = pl.BlockSpec(memory_space=pl.ANY)


def _chip_peers():
    x, y, c = lax.axis_index("x"), lax.axis_index("y"), lax.axis_index("c")
    return x, y, c, [(1 - x, y), (x, 1 - y), (1 - x, 1 - y)]


def _window(ref, kind, chip, rows, cols):
    if kind == "cols":
        return ref.at[:, pl.ds(pl.multiple_of(chip * cols, LANES), cols)]
    if kind == "rows":
        return ref.at[pl.ds(pl.multiple_of(chip * rows, 8), rows), :]
    return ref.at[chip]


def _gathered_shape(kind, rows, cols):
    return {"cols": (rows, 4 * cols), "rows": (4 * rows, cols), "slices": (4, rows, cols)}[kind]


def gather_layers(name, shards, kinds):
    n = len(shards)

    def body(*refs):
        x_refs, o_refs = refs[:n], refs[n:2 * n]
        send_sems, recv_sems, local_sems = refs[2 * n:]
        x, y, c, peers = _chip_peers()
        me = 2 * x + y
        started = []
        for a in range(n):
            _, rows, cols = shards[a].shape
            src = x_refs[a].at[c]
            mine = pltpu.make_async_copy(src, _window(o_refs[a], kinds[a], me, rows, cols), local_sems.at[a])
            mine.start()
            started.append(mine.wait)
            for k, (px, py) in enumerate(peers):
                cp = pltpu.make_async_remote_copy(
                    src_ref=src, dst_ref=_window(o_refs[a], kinds[a], me, rows, cols), send_sem=send_sems.at[3 * a + k],
                    recv_sem=recv_sems.at[3 * a + k], device_id=(px, py, c), device_id_type=MESH)
                cp.start()
                started.append(cp.wait_send)
        for a in range(n):
            _, rows, cols = shards[a].shape
            for k, (px, py) in enumerate(peers):
                pltpu.make_async_remote_copy(
                    src_ref=x_refs[a].at[c], dst_ref=_window(o_refs[a], kinds[a], 2 * px + py, rows, cols),
                    send_sem=send_sems.at[3 * a + k], recv_sem=recv_sems.at[3 * a + k], device_id=(px, py, c),
                    device_id_type=MESH).wait_recv()
        for wait in started:
            wait()

    return pl.pallas_call(
        body, name=name, in_specs=[_HBM] * n, out_specs=[_HBM] * n,
        out_shape=[jax.ShapeDtypeStruct(_gathered_shape(kinds[a], *shards[a].shape[1:]), shards[a].dtype)
                   for a in range(n)],
        scratch_shapes=[pltpu.SemaphoreType.DMA((3 * n,)), pltpu.SemaphoreType.DMA((3 * n,)),
                        pltpu.SemaphoreType.DMA((n,))],
        )(*shards)


def scatter_pieces(name, pieces):
    n = len(pieces)

    def body(*refs):
        p_refs, o_refs = refs[:n], refs[n:2 * n]
        send_sems, recv_sems, local_sems = refs[2 * n:]
        x, y, c, peers = _chip_peers()
        me = 2 * x + y
        started = []
        for a in range(n):
            mine = pltpu.make_async_copy(p_refs[a].at[me], o_refs[a].at[me], local_sems.at[a])
            mine.start()
            started.append(mine.wait)
            for k, (px, py) in enumerate(peers):
                cp = pltpu.make_async_remote_copy(
                    src_ref=p_refs[a].at[2 * px + py], dst_ref=o_refs[a].at[me], send_sem=send_sems.at[3 * a + k],
                    recv_sem=recv_sems.at[3 * a + k], device_id=(px, py, c), device_id_type=MESH)
                cp.start()
                started.append(cp.wait_send)
        for a in range(n):
            for k, (px, py) in enumerate(peers):
                pltpu.make_async_remote_copy(
                    src_ref=p_refs[a].at[me], dst_ref=o_refs[a].at[2 * px + py], send_sem=send_sems.at[3 * a + k],
                    recv_sem=recv_sems.at[3 * a + k], device_id=(px, py, c), device_id_type=MESH).wait_recv()
        for wait in started:
            wait()

    return pl.pallas_call(
        body, name=name, in_specs=[_HBM] * n, out_specs=[_HBM] * n,
        out_shape=[jax.ShapeDtypeStruct(p.shape, p.dtype) for p in pieces],
        scratch_shapes=[pltpu.SemaphoreType.DMA((3 * n,)), pltpu.SemaphoreType.DMA((3 * n,)),
                        pltpu.SemaphoreType.DMA((n,))],
        )(*pieces)


def _pair_step(n_steps, x_ref, land, send_sems, recv_sems, credits, consume):
    x, y, c = lax.axis_index("x"), lax.axis_index("y"), lax.axis_index("c")
    sib = (x, y, 1 - c)
    i = pl.program_id(0)
    slot = i % 2

    @pl.when(i >= 2)
    def _():
        pl.semaphore_wait(credits.at[slot], 1)

    cp = pltpu.make_async_remote_copy(src_ref=x_ref, dst_ref=land.at[slot], send_sem=send_sems.at[slot],
                                      recv_sem=recv_sems.at[slot], device_id=sib, device_id_type=MESH)
    cp.start()
    cp.wait_recv()
    consume(land[slot])

    @pl.when(i < n_steps - 2)
    def _():
        pl.semaphore_signal(credits.at[slot], inc=1, device_id=sib, device_id_type=MESH)

    cp.wait_send()


def _pair_call(name, body, n_steps, in_specs, out_spec, out_shape, blk_shape, dtype, operands, extra_scratch=()):
    grid_spec = pltpu.PrefetchScalarGridSpec(
        num_scalar_prefetch=1, grid=(n_steps,), in_specs=in_specs, out_specs=out_spec,
        scratch_shapes=[pltpu.VMEM((2,) + blk_shape, dtype), pltpu.SemaphoreType.DMA((2,)),
                        pltpu.SemaphoreType.DMA((2,)), pltpu.SemaphoreType.REGULAR((2,)), *extra_scratch])
    return pl.pallas_call(body, name=name, grid_spec=grid_spec, out_shape=out_shape,
                          compiler_params=_cp(("arbitrary",)))(*operands)


def _place():
    return jnp.stack([lax.axis_index("x"), lax.axis_index("y"), lax.axis_index("c")]).astype(jnp.int32)


def _pair_rows(rows, row_bytes):
    for cand in (4096, 2048, 1024, 768, 512, 384, 256, 192, 128, 96, 64, 48, 32, 16):
        if rows % cand == 0 and cand * row_bytes <= PAIR_BLOCK_BYTES:
            return cand
    return _pick(rows, (16, 8))


def exchange_both(name, mine):
    rows, cols = mine.shape
    tr = _pair_rows(rows, cols * mine.dtype.itemsize)
    n_steps = rows // tr

    def body(s_ref, x_ref, o_ref, land, send_sems, recv_sems, credits):
        c = lax.axis_index("c")
        o_ref[c] = x_ref[...]

        def consume(v):
            o_ref[1 - c] = v
        _pair_step(n_steps, x_ref, land, send_sems, recv_sems, credits, consume)

    return _pair_call(name, body, n_steps, [pl.BlockSpec((tr, cols), lambda i, s: (i, 0))],
                      pl.BlockSpec((2, tr, cols), lambda i, s: (0, i, 0)),
                      jax.ShapeDtypeStruct((2, rows, cols), mine.dtype), (tr, cols), mine.dtype, (_place(), mine))


def exchange_add(name, layer0, layer1):
    rows, cols = layer0.shape
    tr = _pair_rows(rows, cols * layer0.dtype.itemsize)
    nb = rows // tr

    def body(s_ref, l0_ref, l1_ref, o_ref, land, send_sems, recv_sems, credits, send_buf):
        first = lax.axis_index("c") == 0
        send_buf[...] = jnp.where(first, l1_ref[...], l0_ref[...])
        mine = jnp.where(first, l0_ref[...], l1_ref[...]).astype(F32)

        def consume(v):
            o_ref[...] = (mine + v.astype(F32)).astype(o_ref.dtype)
        _pair_step(nb, send_buf, land, send_sems, recv_sems, credits, consume)

    spec = pl.BlockSpec((tr, cols), lambda i, s: (i, 0))
    return _pair_call(name, body, nb, [spec, spec], spec, jax.ShapeDtypeStruct((rows, cols), CDT),
                      (tr, cols), layer0.dtype, (_place(), layer0, layer1),
                      extra_scratch=(pltpu.VMEM((tr, cols), layer0.dtype),))


def sum_exchange(name, parts):
    npart, rows, cols = parts.shape
    tr = _pair_rows(rows, cols * 4)
    n_steps = rows // tr

    def body(s_ref, x_ref, o_ref, land, send_sems, recv_sems, credits, mine):
        c = lax.axis_index("c")
        acc = x_ref[0].astype(F32)
        for k in range(1, npart):
            acc = acc + x_ref[k].astype(F32)
        mine[...] = acc
        o_ref[c] = acc

        def consume(v):
            o_ref[1 - c] = v
        _pair_step(n_steps, mine, land, send_sems, recv_sems, credits, consume)

    return _pair_call(name, body, n_steps, [pl.BlockSpec((npart, tr, cols), lambda i, s: (0, i, 0))],
                      pl.BlockSpec((2, tr, cols), lambda i, s: (0, i, 0)),
                      jax.ShapeDtypeStruct((2, rows, cols), F32), (tr, cols), F32, (_place(), parts),
                      extra_scratch=(pltpu.VMEM((tr, cols), F32),))


def allreduce_small(name, buf):
    rows = buf.shape[0]

    def body(x_ref, out_ref, gath, send_sems, recv_sems):
        x, y, c = lax.axis_index("x"), lax.axis_index("y"), lax.axis_index("c")
        me = 4 * x + 2 * y + c
        masks = [(k >> 2 & 1, k >> 1 & 1, k & 1) for k in range(1, 8)]

        def flip(v, bit):
            return 1 - v if bit else v

        sends = []
        for k, (bx, by, bc) in enumerate(masks):
            cp = pltpu.make_async_remote_copy(src_ref=x_ref, dst_ref=gath.at[me], send_sem=send_sems.at[k],
                                              recv_sem=recv_sems.at[k],
                                              device_id=(flip(x, bx), flip(y, by), flip(c, bc)), device_id_type=MESH)
            cp.start()
            sends.append(cp)
        gath[me] = x_ref[...]
        for k, (bx, by, bc) in enumerate(masks):
            px, py, pc = flip(x, bx), flip(y, by), flip(c, bc)
            pltpu.make_async_remote_copy(src_ref=x_ref, dst_ref=gath.at[4 * px + 2 * py + pc],
                                         send_sem=send_sems.at[k], recv_sem=recv_sems.at[k],
                                         device_id=(px, py, pc), device_id_type=MESH).wait_recv()
        for cp in sends:
            cp.wait_send()
        acc = gath[0]
        for d in range(1, 8):
            acc = acc + gath[d]
        out_ref[...] = acc

    return pl.pallas_call(
        body, name=name, in_specs=[pl.BlockSpec(memory_space=pltpu.VMEM)],
        out_specs=pl.BlockSpec(memory_space=pltpu.VMEM), out_shape=jax.ShapeDtypeStruct(buf.shape, F32),
        scratch_shapes=[pltpu.VMEM((8, rows, LANES), F32), pltpu.SemaphoreType.DMA((7,)),
                        pltpu.SemaphoreType.DMA((7,))],
        )(buf)


def _pack_flat(arrs, dtype, width, row_mult=8):
    flat = jnp.concatenate([a.reshape(-1).astype(dtype) for a in arrs])
    pad = (-flat.shape[0]) % (row_mult * width)
    if pad:
        flat = jnp.concatenate([flat, jnp.zeros((pad,), dtype)])
    return flat.reshape(-1, width)


def _unpack_flat(buf, shapes):
    flat = buf.reshape(-1)
    out, off = [], 0
    for s in shapes:
        n = math.prod(s)
        out.append(flat[off:off + n].reshape(s))
        off += n
    return out


def _in_to_padded(w):
    parts = []
    for name in IN_NEW_ORDER:
        _, width, o_off, o_w = IN_LAYOUT[name]
        parts.append(w[..., o_off:o_off + o_w])
        if o_w < width:
            parts.append(jnp.zeros(w.shape[:-1] + (width - o_w,), w.dtype))
    used = sum(IN_LAYOUT[n][1] for n in IN_NEW_ORDER)
    parts.append(jnp.zeros(w.shape[:-1] + (IN_PAD - used,), w.dtype))
    return jnp.concatenate(parts, axis=-1)


def _in_from_padded(g):
    parts = []
    for name in IN_ORIG_ORDER:
        off, _, _, o_w = IN_LAYOUT[name]
        parts.append(g[..., off:off + o_w])
    return jnp.concatenate(parts, axis=-1)


def _pcol(p, name):
    off, width, _, _ = IN_LAYOUT[name]
    return Cols(p, off, width)


def _lane_pad(v, width=LANES):
    v = v.reshape(-1)
    return jnp.concatenate([v, jnp.zeros((width - v.shape[0],), v.dtype)]).reshape(1, width)


def _f_norm_mod(h, sh, sc, w):
    return (_rms(h, w) * (1.0 + sc) + sh,)


def _f_norm_mod_thru(h, sh, sc, w):
    return h, _rms(h, w) * (1.0 + sc) + sh


def _f_attn_prep(qraw, kraw, vraw, cos2, sin2, qw, kw, gq, gk):
    q = qraw * lax.rsqrt(_group_mean(qraw * qraw, gq) + NORM_EPS) * qw
    q = _rope32(q, jnp.tile(cos2, (1, 4)), jnp.tile(sin2, (1, 4))) * (ATTN_HD ** -0.5 * LOG2E)
    k = kraw * lax.rsqrt(_group_mean(kraw * kraw, gk) + NORM_EPS) * kw
    return q, _rope32(k, cos2, sin2), vraw


def _f_ssd_finish(yf, yb, xs, z, d_exp, nw):
    y = (yf + yb + d_exp * xs) * (z * jax.nn.sigmoid(z))
    return (_rms(y, nw),)


def _f_ret_prep(rq, rk, cos1, sin1):
    cos_full, sin_signed = jnp.tile(cos1, (1, 4)), jnp.tile(sin1, (1, 4))
    return _rope64(rq, cos_full, sin_signed), _rope64(rk, cos_full, sin_signed) * (RET_DK ** -0.5)


def _f_ret_finish(yf, yb, g, gw):
    y = yf + yb
    outs = []
    for h in range(RET_HEADS):
        yh = y[:, h * RET_DK:(h + 1) * RET_DK]
        yc = yh - jnp.mean(yh, axis=-1, keepdims=True)
        outs.append(yc * lax.rsqrt(jnp.mean(yc * yc, axis=-1, keepdims=True) + NORM_EPS))
    return (jnp.concatenate(outs, axis=1) * gw * (g * jax.nn.sigmoid(g)),)


def _f_merge(p0, p1, p2, g0, g1, g2):
    return (jax.nn.sigmoid(g0) * p0 + jax.nn.sigmoid(g1) * p1 + jax.nn.sigmoid(g2) * p2,)


def _f_mid(h, mix, g1, sh2, sc2, w2):
    h_mid = h + g1 * mix
    return h_mid, _rms(h_mid, w2) * (1.0 + sc2) + sh2


def _epi_merge(p2, p0, p1, g0, g1, g2):
    return (p2,) + _f_merge(p0, p1, p2, g0, g1, g2)


def _epi_merge_bwd(dmerged, p0, p1, p2, g0, g1, g2):
    dps, dgs = [], []
    for pb, gate in ((p0, g0), (p1, g1), (p2, g2)):
        s = jax.nn.sigmoid(gate)
        dps.append(dmerged * s)
        dgs.append(dmerged * pb.astype(F32) * (s * (1.0 - s)))
    return tuple(dps + dgs)


def _epi_mid(mix, h, g1, sh2, sc2, w2):
    return (mix,) + _f_mid(h, mix, g1, sh2, sc2, w2)


def _epi_resid(o, h_mid, g2):
    return (o,) + _f_residual(h_mid, o, g2)


def _epi_sqrelu(a):
    r = jnp.maximum(a, 0.0)
    return a, r * r


def _epi_sqrelu_bwd(dhh, a):
    return (dhh * (2.0 * jnp.maximum(a.astype(F32), 0.0)),)


def _f_residual(h_mid, o, g2):
    return (h_mid + g2 * o,)


def _f_silu(x):
    return (x * jax.nn.sigmoid(x),)


def _f_bias(x, b):
    return (x + b,)


def _ssd_rows(xbc, p):
    rows = [Cols(xbc, LANES * k, LANES) for k in range(4)]
    rows += [Cols(xbc, 512 + LANES * g, LANES) for g in range(2)]
    rows += [Cols(xbc, 768 + LANES * g, LANES) for g in range(2)]
    return rows + [_pcol(p, "dt")]


def _ret_rows(rq, rk, p):
    off_v = IN_LAYOUT["rv"][0]
    return ([Cols(rq, LANES * h, LANES) for h in range(4)] + [Cols(rk, LANES * h, LANES) for h in range(4)]
            + [Cols(p, off_v + LANES * h, LANES) for h in range(4)])


def layer_fwd(li, h, mod, lw, tabs, m_ctx):
    t = h.shape[0]
    nb = m_ctx // min(ROW_TILE, t)
    sh1, sc1, g1, sh2, sc2, g2 = mod
    nm = lambda s: f"l{li}_{s}"
    sv = {}
    (u,) = rowwise_fwd(nm("norm1"), _f_norm_mod, [h], [sh1, sc1], [lw["norm1_w"]], [(D_MODEL, CDT)], t, nb)
    p = mm(nm("in_proj"), u, lw["w_in"], F32)
    q, k, v = rowwise_fwd(
        nm("attn_prep"), _f_attn_prep,
        [_pcol(p, "q"), _pcol(p, "k"), _pcol(p, "v"), tabs["ca"], tabs["sa"]], [],
        [lw["qw"], lw["kw"], tabs["gq"], tabs["gk"]], [(512, CDT), (128, CDT), (128, CDT)], t, nb)
    tq = min(ATTN_TQ, m_ctx)
    kk, vv = _split_kv(k), _split_kv(v)
    ones_rows = jnp.concatenate([jnp.ones((2, 1, t), CDT), jnp.zeros((2, ATTN_ONES_ROWS - 1, t), CDT)], axis=1)
    attn_o, qT, oT, lse = attn_fwd(nm("attn"), q, kk, jnp.concatenate([vv.transpose(0, 2, 1), ones_rows], axis=1),
                                   m_ctx, tq)

    xbc = conv_fwd(_pcol(p, "xbc"), lw["conv_w"], lw["conv_b"], m_ctx)
    ssd_sh = [lw["dt_bias"], lw["a_log"]]
    yf, yb, sf, sb = scan_fwd(nm("ssd"), _make_ssd_chunk, _ssd_rows(xbc, p), ssd_sh, 4, 512, t, m_ctx)
    (ssd_o,) = rowwise_fwd(nm("ssd_fin"), _f_ssd_finish, [yf, yb, Cols(xbc, 0, 512), _pcol(p, "z")], [],
                           [lw["d_exp"], lw["ssd_nw"]], [(512, CDT)], t, nb)

    rq, rk = rowwise_fwd(nm("ret_prep"), _f_ret_prep, [_pcol(p, "rq"), _pcol(p, "rk"), tabs["rc"], tabs["rs"]],
                         [], [], [(512, F32), (512, F32)], t, nb)
    rf, rb, rsf, rsb = scan_fwd(nm("ret"), _make_ret_chunk, _ret_rows(rq, rk, p), [lw["ret_lg"]], 4, 512, t, m_ctx)
    (ret_o,) = rowwise_fwd(nm("ret_fin"), _f_ret_finish, [rf, rb, _pcol(p, "rg")], [], [lw["ret_gw"]],
                           [(512, CDT)], t, nb)

    pbs = [mm(nm(f"branch{b}"), br, lw["w_branch"][b], CDT) for b, br in enumerate((attn_o, ssd_o))]
    gl = [Cols(p, 1024 * b, 1024) for b in range(3)]
    pb2, merged = mm(nm("branch2"), ret_o, lw["w_branch"][2], None, epilogue=_epi_merge, tile_ins=pbs + gl,
                     out_dtypes=[CDT, CDT])
    pbs.append(pb2)
    mix, h_mid, vv2 = mm(nm("out_proj"), merged, lw["w_out"], None, epilogue=_epi_mid, tile_ins=[h],
                         typed_ins=[g1, sh2, sc2], shared_ins=[lw["norm2_w"]], out_dtypes=[F32, F32, CDT], nb_ctx=nb,
                         whole_rows=True)
    a, hh = mm(nm("mlp1"), vv2, lw["w_mlp1"], None, epilogue=_epi_sqrelu, out_dtypes=[CDT, CDT])
    o, h_out = mm(nm("mlp2"), hh, lw["w_mlp2"], None, epilogue=_epi_resid, tile_ins=[h_mid], typed_ins=[g2],
                  out_dtypes=[F32, F32], nb_ctx=nb)
    sv.update(h=h, u=u, p=p, qT=qT, kk=kk, vv=vv, oT=oT, lse=lse, attn_o=attn_o, xbc=xbc, yf=yf, yb=yb,
              sf=sf, sb=sb, ssd_o=ssd_o, rq=rq, rk=rk, rf=rf, rb=rb, rsf=rsf, rsb=rsb, ret_o=ret_o, pbs=pbs,
              merged=merged, mix=mix, h_mid=h_mid, v=vv2, a=a, hh=hh, o=o)
    return h_out, sv


def layer_bwd(li, dh_out, sv, mod, lw, tabs, m_ctx):
    t = dh_out.shape[0]
    nb = m_ctx // min(ROW_TILE, t)
    sh1, sc1, g1, sh2, sc2, g2 = mod
    nm = lambda s: f"l{li}_{s}_bwd"
    gw = {}
    p = sv["p"]
    (do,), (dg2,), _ = rowwise_bwd(nm("resid"), _f_residual, [sv["h_mid"], sv["o"]], [g2], [], [dh_out],
                                   [False, True], [], [CDT], t, nb)
    (da,) = mm(nm("mlp2_dx"), do, lw["w_mlp2"], None, transpose_b=True, epilogue=_epi_sqrelu_bwd,
               tile_ins=[sv["a"]], out_dtypes=[CDT])
    gw["w_mlp2"] = mm_tn(nm("mlp2_dw"), sv["hh"], do, CDT, ("rows", lw["w_mlp2"].shape[0] // 4))
    dv = mm(nm("mlp1_dx"), da, lw["w_mlp1"], F32, transpose_b=True)
    gw["w_mlp1"] = mm_tn(nm("mlp1_dw"), sv["v"], da, CDT, ("cols", lw["w_mlp1"].shape[1] // 4))
    (dh_a, dmix), (dg1, dsh2, dsc2), (gw["norm2_w"],) = rowwise_bwd(
        nm("mid"), _f_mid, [sv["h"], sv["mix"]], [g1, sh2, sc2], [lw["norm2_w"]], [dh_out, dv],
        [True, True], [True], [F32, CDT], t, nb)
    gl = [Cols(p, 1024 * b, 1024) for b in range(3)]
    dmg = mm(nm("out_dx"), dmix, lw["w_out"], None, transpose_b=True, epilogue=_epi_merge_bwd,
             tile_ins=sv["pbs"] + gl, out_dtypes=[CDT] * 6)
    gw["w_out"] = mm_tn(nm("out_dw"), sv["merged"], dmix, CDT, ("rows", lw["w_out"].shape[0] // 4))
    dpb, dgl = dmg[:3], dmg[3:]
    brs = (sv["attn_o"], sv["ssd_o"], sv["ret_o"])
    d_attn_o = mm(nm("branch0_dx"), dpb[0], lw["w_branch"][0], CDT, transpose_b=True)
    d_ssd_o = mm(nm("branch1_dx"), dpb[1], lw["w_branch"][1], F32, transpose_b=True)
    d_ret_o = mm(nm("branch2_dx"), dpb[2], lw["w_branch"][2], F32, transpose_b=True)
    n_loc = lw["w_branch"].shape[2] // 4
    gw["w_branch"] = jnp.stack([mm_tn(nm(f"branch{b}_dw"), brs[b], dpb[b], CDT, ("cols", n_loc)) for b in range(3)],
                               axis=1).reshape(4, -1, n_loc)
    tq = min(ATTN_TQ, m_ctx)
    dq_rows, dk_s, dv_s = attn_bwd(nm("attn"), sv["qT"], d_attn_o, sv["oT"], sv["lse"], sv["kk"],
                                   sv["kk"].transpose(0, 2, 1), sv["vv"], m_ctx)
    (dq_raw, dk_raw, dv_raw), _, (gw["qw"], gw["kw"]) = rowwise_bwd(
        nm("attn_prep"), _f_attn_prep,
        [_pcol(p, "q"), _pcol(p, "k"), _pcol(p, "v"), tabs["ca"], tabs["sa"]], [],
        [lw["qw"], lw["kw"], tabs["gq"], tabs["gk"]],
        [dq_rows, _merge_kv(dk_s) * LN2, _merge_kv(dv_s)],
        [True, True, True, False, False], [True, True, False, False], [CDT] * 3, t, nb)
    (dy_ssd, dxs_fin, dz), _, (gw["d_exp"], gw["ssd_nw"]) = rowwise_bwd(
        nm("ssd_fin"), _f_ssd_finish, [sv["yf"], sv["yb"], Cols(sv["xbc"], 0, 512), _pcol(p, "z")], [],
        [lw["d_exp"], lw["ssd_nw"]], [d_ssd_o], [True, False, True, True], [True, True], [F32, F32, CDT], t, nb)
    ssd_sh = [lw["dt_bias"], lw["a_log"]]
    post_ssd = lambda d: [jnp.concatenate(d[0:8], axis=1), d[8]]
    (dxbc_f, ddt_f), dsh_f = scan_bwd(nm("ssd_f"), _make_ssd_chunk, _ssd_rows(sv["xbc"], p), ssd_sh, (sv["sf"],),
                                      dy_ssd, post_ssd, [(1024, F32), (LANES, F32)], 4, t, m_ctx, dirs=(0,))
    (dxbc_b, ddt_b), dsh_b = scan_bwd(nm("ssd_b"), _make_ssd_chunk, _ssd_rows(sv["xbc"], p), ssd_sh, (sv["sb"],),
                                      dy_ssd, post_ssd, [(1024, F32), (LANES, F32)], 4, t, m_ctx, dirs=(1,))
    gw["dt_bias"], gw["a_log"] = dsh_f[0] + dsh_b[0], dsh_f[1] + dsh_b[1]
    ddt = (ddt_f + ddt_b).astype(CDT)
    dxbc_raw, gw["conv_w"], gw["conv_b"] = conv_bwd(_pcol(p, "xbc"), lw["conv_w"], lw["conv_b"], dxbc_f, dxbc_b,
                                                    dxs_fin, m_ctx)
    (dy_ret, drg), _, (gw["ret_gw"],) = rowwise_bwd(
        nm("ret_fin"), _f_ret_finish, [sv["rf"], sv["rb"], _pcol(p, "rg")], [], [lw["ret_gw"]], [d_ret_o],
        [True, False, True], [True], [F32, CDT], t, nb)
    post_ret = lambda d: [jnp.concatenate(d[0:4], axis=1), jnp.concatenate(d[4:8], axis=1),
                          jnp.concatenate(d[8:12], axis=1)]
    rrows = _ret_rows(sv["rq"], sv["rk"], p)
    (dq_f, dk_f, dv_f), (dq_b, dk_b, dv_b), (gw["ret_lg"],) = scan_bwd(
        nm("ret"), _make_ret_chunk, rrows, [lw["ret_lg"]], (sv["rsf"], sv["rsb"]), dy_ret, post_ret,
        [(512, F32)] * 3, 4, t, m_ctx)
    drv = (dv_f + dv_b).astype(CDT)
    (drq, drk), _, _ = rowwise_bwd(nm("ret_prep"), _f_ret_prep,
                                   [_pcol(p, "rq"), _pcol(p, "rk"), tabs["rc"], tabs["rs"]], [], [],
                                   [(dq_f, dq_b), (dk_f, dk_b)], [True, True, False, False], [], [CDT, CDT], t, nb)
    pieces = {"gates": None, "xbc": dxbc_raw, "q": dq_raw, "z": dz, "rq": drq, "rk": drk, "rv": drv, "rg": drg,
              "k": dk_raw, "v": dv_raw, "dt": ddt}
    cols = list(dgl) + [pieces[n] for n in IN_NEW_ORDER[1:]]
    used = sum(c.shape[1] for c in cols)
    cols.append(jnp.zeros((t, IN_PAD - used), CDT))
    dp = jnp.concatenate(cols, axis=1)
    du = mm(nm("in_dx"), dp, lw["w_in"], F32, transpose_b=True)
    gw["w_in"] = mm_tn(nm("in_dw"), sv["u"], dp, CDT)
    (dh_in,), (dsh1, dsc1), (gw["norm1_w"],) = rowwise_bwd(
        nm("norm1"), _f_norm_mod_thru, [sv["h"]], [sh1, sc1], [lw["norm1_w"]], [dh_a, du], [True], [True], [F32],
        t, nb)
    return dh_in, [dsh1, dsc1, dg1, dsh2, dsc2, dg2], gw


def _rope_tables(n_lat, m_ctx):
    rows = n_lat // GRID_W
    row = jnp.repeat(jnp.arange(rows, dtype=F32), GRID_W)
    col = jnp.tile(jnp.arange(GRID_W, dtype=F32), rows)
    nfreq = ATTN_HD // 4
    inv = ROPE_THETA ** (-jnp.arange(nfreq, dtype=F32) / nfreq)
    ang = jnp.concatenate([row[:, None] * inv, col[:, None] * inv], axis=-1)
    cos = jnp.concatenate([jnp.ones((m_ctx, ATTN_HD // 2), F32), jnp.cos(ang)], axis=0)
    sin = jnp.concatenate([jnp.zeros((m_ctx, ATTN_HD // 2), F32), jnp.sin(ang)], axis=0)
    c64 = jnp.concatenate([cos, cos], axis=1)
    s64 = jnp.concatenate([-sin, sin], axis=1)
    pos = jnp.arange(m_ctx + n_lat, dtype=F32)
    inv_r = ROPE_THETA ** (-jnp.linspace(0.0, 1.0, RET_DK // 2, dtype=F32))
    ang_r = pos[:, None] * inv_r
    rc = jnp.concatenate([jnp.cos(ang_r)] * 2, axis=1)
    rs = jnp.concatenate([-jnp.sin(ang_r), jnp.sin(ang_r)], axis=1)
    return dict(ca=jnp.tile(c64, (1, 2)), sa=jnp.tile(s64, (1, 2)), rc=rc, rs=rs, gq=_group_matrix(512, ATTN_HD),
                gk=_group_matrix(128, ATTN_HD))


def _layer_weights(full, small, layer):
    return dict(
        w_in=full["w_in"][layer], w_branch=full["w_branch"][layer], w_out=full["w_out"][layer],
        w_mlp1=full["w_mlp1"][layer], w_mlp2=full["w_mlp2"][layer],
        norm1_w=small["norm1_w"][layer][None], norm2_w=small["norm2_w"][layer][None],
        qw=jnp.tile(small["attn_q_norm"][layer], 8)[None], kw=jnp.tile(small["attn_k_norm"][layer], 2)[None],
        conv_w=small["ssd_conv_w"][layer], conv_b=small["ssd_conv_b"][layer][None],
        dt_bias=_lane_pad(small["ssd_dt_bias"][layer]), a_log=_lane_pad(small["ssd_a_log"][layer]),
        d_exp=jnp.repeat(small["ssd_d"][layer], SSD_HD)[None], ssd_nw=small["ssd_norm_w"][layer][None],
        ret_lg=_lane_pad(small["ret_log_decay"][layer]), ret_gw=small["ret_gn_w"][layer][None])


def local_step(x, c, ctx, full, small, loss_target):
    n_lat, d = x.shape
    m_ctx = ctx.shape[0]
    t = n_lat + m_ctx
    depth = small["norm1_w"].shape[0]
    tabs = _rope_tables(n_lat, m_ctx)
    h = jnp.concatenate([ctx, x], axis=0)
    cc = jnp.concatenate([small["c_ctx"][None], c, jnp.zeros((COND_ROWS - 2, d), F32)], axis=0)
    (scc,) = rowwise_fwd("cond_silu", _f_silu, [cc], [], [], [(d, CDT)], COND_ROWS, 0)
    mods, saved, lws = [], [], []
    for layer in range(depth):
        lw = _layer_weights(full, small, layer)
        mod_raw = mm(f"l{layer}_mod", scc, full["w_mod"][layer], F32)
        (mod8,) = rowwise_fwd(f"l{layer}_mod_bias", _f_bias, [mod_raw], [], [small["b_mod"][layer][None]],
                              [(6 * d, F32)], COND_ROWS, 0)
        mod = [mod8[0:2, k * d:(k + 1) * d].reshape(2, 1, d) for k in range(6)]
        h, sv = layer_fwd(layer, h, mod, lw, tabs, m_ctx)
        mods.append(mod)
        saved.append(sv)
        lws.append(lw)
    loss, dh, d_final = loss_head(h, loss_target, small["final_norm_w"][None], m_ctx)

    gbig = {k: [None] * depth for k in BIG}
    gs = {k: [None] * depth for k in SMALL if k not in ("c_ctx", "final_norm_w")}
    d_scc = None
    for layer in reversed(range(depth)):
        lw = lws[layer]
        dh, dmod, gw = layer_bwd(layer, dh, saved[layer], mods[layer], lw, tabs, m_ctx)
        dmod8 = jnp.concatenate([jnp.concatenate([g_.reshape(2, d) for g_ in dmod], axis=1),
                                 jnp.zeros((COND_ROWS - 2, 6 * d), F32)], axis=0)
        (dmod_c,), _, (db_mod,) = rowwise_bwd(f"l{layer}_mod_bias_bwd", _f_bias, [dmod8], [],
                                              [small["b_mod"][layer][None]], [dmod8], [True], [True], [CDT], COND_ROWS, 0)
        gbig["w_mod"][layer] = mm_tn(f"l{layer}_mod_dw", scc, dmod_c, CDT, ("cols", 6 * d // 4))
        part = mm(f"l{layer}_mod_dx", dmod_c, full["w_mod"][layer], F32, transpose_b=True)
        d_scc = part if d_scc is None else d_scc + part
        g_in = _in_from_padded(gw["w_in"])
        gbig["w_in"][layer] = g_in.reshape(d, 4, g_in.shape[1] // 4).transpose(1, 0, 2)
        for k in ("w_branch", "w_out", "w_mlp1", "w_mlp2"):
            gbig[k][layer] = gw[k]
        gs["b_mod"][layer] = db_mod.reshape(-1)
        gs["norm1_w"][layer] = gw["norm1_w"].reshape(-1)
        gs["norm2_w"][layer] = gw["norm2_w"].reshape(-1)
        gs["attn_q_norm"][layer] = gw["qw"].reshape(8, ATTN_HD).sum(0)
        gs["attn_k_norm"][layer] = gw["kw"].reshape(2, ATTN_HD).sum(0)
        gs["ssd_conv_w"][layer] = gw["conv_w"]
        gs["ssd_conv_b"][layer] = gw["conv_b"].reshape(-1)
        gs["ssd_dt_bias"][layer] = gw["dt_bias"][0, :16].reshape(2, 8)
        gs["ssd_a_log"][layer] = gw["a_log"][0, :16].reshape(2, 8)
        gs["ssd_d"][layer] = gw["d_exp"].reshape(SSD_HEADS, SSD_HD).sum(1)
        gs["ssd_norm_w"][layer] = gw["ssd_nw"].reshape(-1)
        gs["ret_log_decay"][layer] = gw["ret_lg"][0, :8].reshape(2, 4)
        gs["ret_gn_w"][layer] = gw["ret_gw"].reshape(-1)
    (d_cc,), _, _ = rowwise_bwd("cond_silu_bwd", _f_silu, [cc], [], [], [d_scc], [True], [], [F32], COND_ROWS, 0)
    g_small = {k: jnp.stack(v) for k, v in gs.items()}
    g_small["c_ctx"] = d_cc[0]
    g_small["final_norm_w"] = d_final.reshape(-1)
    return loss, dh[m_ctx:], gbig, g_small


def kernel(x, c, ctx, c_ctx, w_mod, b_mod, norm1_w, norm2_w, w_in, attn_q_norm, attn_k_norm, ssd_conv_w, ssd_conv_b, ssd_dt_bias, ssd_a_log, ssd_d, ssd_norm_w, ret_log_decay, ret_gn_w, w_branch, w_out, w_mlp1, w_mlp2, final_norm_w, loss_target, m_c_ctx, m_w_mod, m_b_mod, m_norm1_w, m_norm2_w, m_w_in, m_attn_q_norm, m_attn_k_norm, m_ssd_conv_w, m_ssd_conv_b, m_ssd_dt_bias, m_ssd_a_log, m_ssd_d, m_ssd_norm_w, m_ret_log_decay, m_ret_gn_w, m_w_branch, m_w_out, m_w_mlp1, m_w_mlp2, m_final_norm_w, v_c_ctx, v_w_mod, v_b_mod, v_norm1_w, v_norm2_w, v_w_in, v_attn_q_norm, v_attn_k_norm, v_ssd_conv_w, v_ssd_conv_b, v_ssd_dt_bias, v_ssd_a_log, v_ssd_d, v_ssd_norm_w, v_ret_log_decay, v_ret_gn_w, v_w_branch, v_w_out, v_w_mlp1, v_w_mlp2, v_final_norm_w):
    env = dict(locals())
    w_loc = {k: env[k] for k in WEIGHTS}
    m_loc = {k: env["m_" + k] for k in WEIGHTS}
    v_loc = {k: env["v_" + k] for k in WEIGHTS}
    chip = 2 * lax.axis_index("x") + lax.axis_index("y")
    core = lax.axis_index("c")

    depth = w_loc["w_mod"].shape[0]
    assert depth == 2, "the exchanges split the layers between a chip's two cores"
    shards = [w_loc[k].astype(CDT).reshape(depth, -1, w_loc[k].shape[-1]) for k in BIG]
    mine = gather_layers("gather_weights", shards, [BIG_KIND[k] for k in BIG])
    full = {}
    for k, arr in zip(BIG, mine):
        both = exchange_both("share_" + k, arr.reshape(-1, arr.shape[-1]))
        if k == "w_in":
            both = both.reshape(depth, 4, -1, both.shape[-1]).transpose(0, 2, 1, 3)
            both = _in_to_padded(both.reshape(depth, both.shape[1], -1))
        full[k] = both.reshape((depth,) + w_loc[k].shape[1:-1] + (-1,)) if BIG_KIND[k] == "cols" else \
            both.reshape((depth,) + w_loc[k].shape[1:-2] + (-1, w_loc[k].shape[-1])) if BIG_KIND[k] == "rows" else both

    cw = w_loc["ssd_conv_w"]
    cw_w = cw.shape[-1]
    placed = lax.dynamic_update_slice(jnp.zeros(cw.shape[:-1] + (4 * cw_w,), F32),
                                      cw * (core == 0).astype(F32), (0, 0, chip * cw_w))
    conv_full = _unpack_flat(allreduce_small("gather_conv_w", _pack_flat([placed], F32, LANES)), [placed.shape])[0]
    small = {k: w_loc[k] for k in SMALL}
    small["ssd_conv_w"] = conv_full

    loss_l, grad_x, g_big, g_small = local_step(x[0], c, ctx[0], full, small, loss_target[0])

    small_shapes = [g_small[k].shape for k in SMALL] + [(LANES,)]
    summed = _unpack_flat(allreduce_small("reduce_small", _pack_flat([g_small[k] for k in SMALL] + [loss_l], F32, LANES)),
                          small_shapes)
    gsum = dict(zip(SMALL, summed[:-1]))
    loss = summed[-1][0]
    gsum["ssd_conv_w"] = lax.dynamic_slice(gsum["ssd_conv_w"], (0, 0, chip * cw_w), cw.shape)

    pair = []
    for k in BIG:
        _, rows, cols = g_big[k][0].shape
        pair.append(exchange_add("pair_" + k, g_big[k][0].reshape(4 * rows, cols),
                                 g_big[k][1].reshape(4 * rows, cols)).reshape(4, rows, cols))
    landed = scatter_pieces("scatter_grads", pair)
    g_sum = [sum_exchange("sum_" + k, parts) for k, parts in zip(BIG, landed)]

    grads, deltas, new_m, new_v = {}, {}, {}, {}
    for i, k in enumerate(BIG):
        shp = w_loc[k].shape
        three_d = lambda a, shp=shp: a.reshape((-1,) + shp[-2:])
        res = adamw("adamw_" + k, three_d(w_loc[k]), three_d(m_loc[k]), three_d(v_loc[k]), [three_d(g_sum[i])])
        grads[k], deltas[k], new_m[k], new_v[k] = [r.reshape(shp) for r in res]
    small_loc_shapes = [w_loc[k].shape for k in SMALL]
    res = adamw("adamw_small", _pack_flat([w_loc[k] for k in SMALL], F32, LANES)[None],
                _pack_flat([m_loc[k] for k in SMALL], F32, LANES)[None],
                _pack_flat([v_loc[k] for k in SMALL], F32, LANES)[None],
                [_pack_flat([gsum[k] for k in SMALL], F32, LANES)[None]])
    for dst, r in zip((grads, deltas, new_m, new_v), res):
        dst.update(dict(zip(SMALL, _unpack_flat(r, small_loc_shapes))))

    return (loss, grad_x[None], *[grads[k] for k in WEIGHTS], *[deltas[k] for k in WEIGHTS],
            *[new_m[k] for k in WEIGHTS], *[new_v[k] for k in WEIGHTS])
```
